```python
import jax
import jax.numpy as jnp
from jax import lax
import numpy as np

D_MODEL = 1024
BATCH = 8
SEQ = 4096
DEPTH = 2

HEAD_DIM = 64
SB_HEADS = D_MODEL // 256
SB_WIDTH = SB_HEADS * HEAD_DIM
CV_WIDTH = D_MODEL // 4
CV_KERNEL = 31
DL_HEADS = D_MODEL // 128
DL_WIDTH = DL_HEADS * HEAD_DIM
MIX_WIDTH = SB_WIDTH + CV_WIDTH + DL_WIDTH
IN_WIDTH = 3 * SB_WIDTH + 2 * CV_WIDTH + 3 * DL_WIDTH
DL_PATTERN = ((128, 1), (512, 4), (2048, 16))
BLOCK = 128
ROPE_THETA = 10000.0
N_MEM = 256
X_HEADS = 4
X_HEAD_DIM = D_MODEL // X_HEADS
D_FF = 2816
FFN_KERNEL = 3
EPS = 1e-6

kernel_name = 'hybrid_stickbreak_conformer_dilated'


def rms_norm(x, g):
    xf = x.astype(jnp.float32)
    y = xf * lax.rsqrt(jnp.mean(xf * xf, axis=-1, keepdims=True) + EPS)
    return (y * g.astype(jnp.float32)).astype(x.dtype)


def layer_norm(x, g, b):
    xf = x.astype(jnp.float32)
    mu = jnp.mean(xf, axis=-1, keepdims=True)
    var = jnp.mean(jnp.square(xf - mu), axis=-1, keepdims=True)
    y = (xf - mu) * lax.rsqrt(var + EPS)
    return (y * g.astype(jnp.float32) + b.astype(jnp.float32)).astype(x.dtype)


def causal_depthwise_conv(x, w, b):
    k_width, ch = w.shape
    y = lax.conv_general_dilated(
        x, w[:, None, :].astype(x.dtype), window_strides=(1,),
        padding=((k_width - 1, 0),), dimension_numbers=('NWC', 'WIO', 'NWC'),
        feature_group_count=ch)
    return y + b.astype(x.dtype)


def rope(x, positions):
    half = x.shape[-1] // 2
    inv_freq = ROPE_THETA ** (-jnp.arange(half, dtype=jnp.float32) / half)
    ang = positions.astype(jnp.float32)[..., None] * inv_freq
    cos = jnp.cos(ang)[:, :, None, :]
    sin = jnp.sin(ang)[:, :, None, :]
    xf = x.astype(jnp.float32)
    x1, x2 = xf[..., :half], xf[..., half:]
    return jnp.concatenate([x1 * cos - x2 * sin, x2 * cos + x1 * sin], axis=-1).astype(x.dtype)


def stick_breaking_attention(q, k, v):
    b_, s_, h_, hd = q.shape
    scale = hd ** -0.5
    qh, kh, vh = (t.transpose(0, 2, 1, 3) for t in (q, k, v))
    outs = []
    for n in range(s_ // BLOCK):
        t0, t1 = n * BLOCK, (n + 1) * BLOCK
        z = jnp.einsum('bhqc,bhkc->bhqk', qh[:, :, t0:t1], kh[:, :, :t1]).astype(jnp.float32) * scale
        t_idx = t0 + jnp.arange(BLOCK)[:, None]
        s_idx = jnp.arange(t1)[None, :]
        before = s_idx < t_idx
        log_keep = jnp.where(before, jax.nn.log_sigmoid(-z), 0.0)
        between = lax.cumsum(log_keep, axis=3, reverse=True) - log_keep
        a = jnp.where(before, jnp.exp(jax.nn.log_sigmoid(z) + between), 0.0)
        outs.append(jnp.einsum('bhqk,bhkc->bhqc', a.astype(vh.dtype), vh[:, :, :t1]))
    o = jnp.concatenate(outs, axis=2)
    return o.transpose(0, 2, 1, 3)


def dilated_branch(q, k, v, window, dilation):
    b_, s_, h_, hd = q.shape
    w_steps = window // dilation
    assert w_steps <= BLOCK
    seq_sub = s_ // dilation
    nb = -(-seq_sub // BLOCK)
    pad_len = nb * BLOCK - seq_sub
    scale = hd ** -0.5

    def to_sub(t):
        t = t.reshape(b_, seq_sub, dilation, h_, hd).transpose(0, 2, 3, 1, 4)
        t = jnp.pad(t, ((0, 0), (0, 0), (0, 0), (0, pad_len), (0, 0)))
        return t.reshape(b_, dilation, h_, nb, BLOCK, hd)

    def with_prev(t):
        prev = jnp.pad(t, ((0, 0), (0, 0), (0, 0), (1, 0), (0, 0), (0, 0)))[:, :, :, :-1]
        return jnp.concatenate([prev, t], axis=4)

    qs = to_sub(q)
    kw = with_prev(to_sub(k))
    vw = with_prev(to_sub(v))
    s = jnp.einsum('brhnqc,brhnkc->brhnqk', qs, kw).astype(jnp.float32) * scale
    qi = jnp.arange(nb)[:, None, None] * BLOCK + jnp.arange(BLOCK)[None, :, None]
    ki = (jnp.arange(nb)[:, None, None] - 1) * BLOCK + jnp.arange(2 * BLOCK)[None, None, :]
    dist = qi - ki
    valid = (dist >= 0) & (dist <= w_steps) & (ki >= 0)
    s = jnp.where(valid, s, -jnp.inf)
    m = jnp.max(s, axis=-1, keepdims=True)
    p = jnp.exp(s - m)
    den = jnp.sum(p, axis=-1, keepdims=True)
    o = jnp.einsum('brhnqk,brhnkc->brhnqc', (p / den).astype(v.dtype), vw)
    lse = (m + jnp.log(den))[..., 0]
    o = o.reshape(b_, dilation, h_, nb * BLOCK, hd)[:, :, :, :seq_sub]
    o = o.transpose(0, 3, 1, 2, 4).reshape(b_, s_, h_, hd)
    lse = lse.reshape(b_, dilation, h_, nb * BLOCK)[:, :, :, :seq_sub]
    lse = lse.transpose(0, 3, 1, 2).reshape(b_, s_, h_)
    return o, lse


def dilated_mixture(q, k, v):
    outs, lses = [], []
    for window, dilation in DL_PATTERN:
        o, lse = dilated_branch(q, k, v, window, dilation)
        outs.append(o)
        lses.append(lse)
    wts = jax.nn.softmax(jnp.stack(lses, axis=-1), axis=-1)
    o = jnp.einsum('bshn,nbshc->bshc', wts, jnp.stack(outs, axis=0).astype(jnp.float32))
    return o.astype(v.dtype)


def conformer_conv(val, gate, cv_w, cv_b, cv_ln_g, cv_ln_b, cv_pw_w, cv_pw_b):
    g = val * jax.nn.sigmoid(gate)
    c = causal_depthwise_conv(g, cv_w, cv_b)
    c = jax.nn.silu(layer_norm(c, cv_ln_g, cv_ln_b))
    return c @ cv_pw_w + cv_pw_b


def hybrid_mixer(h, positions, w_in, cv_w, cv_b, cv_ln_g, cv_ln_b, cv_pw_w, cv_pw_b, w_out):
    b_, s_, _ = h.shape
    u = h @ w_in
    o1 = 3 * SB_WIDTH
    o2 = o1 + 2 * CV_WIDTH
    sb = u[..., :o1].reshape(b_, s_, 3, SB_HEADS, HEAD_DIM)
    cv = u[..., o1:o2]
    dl = u[..., o2:].reshape(b_, s_, 3, DL_HEADS, HEAD_DIM)
    a_out = stick_breaking_attention(sb[:, :, 0], sb[:, :, 1], sb[:, :, 2]).reshape(b_, s_, SB_WIDTH)
    b_out = conformer_conv(cv[..., :CV_WIDTH], cv[..., CV_WIDTH:], cv_w, cv_b, cv_ln_g, cv_ln_b, cv_pw_w, cv_pw_b)
    q = rope(dl[:, :, 0], positions)
    k = rope(dl[:, :, 1], positions)
    c_out = dilated_mixture(q, k, dl[:, :, 2]).reshape(b_, s_, DL_WIDTH)
    return jnp.concatenate([a_out, b_out, c_out], axis=-1) @ w_out


def memory_cross_attention(h, mem_n, wq, wk, wv, wo):
    b_, s_, _ = h.shape
    q = (h @ wq).reshape(b_, s_, X_HEADS, X_HEAD_DIM)
    k = (mem_n @ wk).reshape(b_, -1, X_HEADS, X_HEAD_DIM)
    v = (mem_n @ wv).reshape(b_, -1, X_HEADS, X_HEAD_DIM)
    s = jnp.einsum('bshc,bmhc->bhsm', q, k).astype(jnp.float32) * (X_HEAD_DIM ** -0.5)
    p = jax.nn.softmax(s, axis=-1).astype(v.dtype)
    o = jnp.einsum('bhsm,bmhc->bshc', p, v).reshape(b_, s_, D_MODEL)
    return o @ wo


def conv_ffn(h, w_up, conv_w, conv_b, w_down):
    u = causal_depthwise_conv(h @ w_up, conv_w, conv_b)
    gate, val = u[..., :D_FF], u[..., D_FF:]
    return (jax.nn.gelu(gate, approximate=True) * val) @ w_down


def _fwd_setup_inputs(seed: int = 0) -> dict:
    key = jax.random.key(seed)
    ks = jax.random.split(key, 32)
    f32 = jnp.float32

    def dense(k, shape, fan_in):
        return jax.random.normal(k, shape, f32) * (fan_in ** -0.5)

    def gain(k, shape):
        return 1.0 + 0.02 * jax.random.normal(k, shape, f32)

    def bias(k, shape):
        return 0.02 * jax.random.normal(k, shape, f32)

    offset = jax.random.randint(ks[2], (BATCH, 1), 0, 1024, dtype=jnp.int32)
    positions = (offset + jnp.arange(SEQ, dtype=jnp.int32)[None, :]).astype(jnp.int32)
    return {
        'x': jax.random.normal(ks[0], (BATCH, SEQ, D_MODEL), f32),
        'mem': jax.random.normal(ks[1], (BATCH, N_MEM, D_MODEL), f32),
        'positions': positions,
        'mix_norm_pre': gain(ks[3], (DEPTH, D_MODEL)),
        'w_in': dense(ks[4], (DEPTH, D_MODEL, IN_WIDTH), D_MODEL),
        'cv_w': dense(ks[5], (DEPTH, CV_KERNEL, CV_WIDTH), CV_KERNEL),
        'cv_b': bias(ks[6], (DEPTH, CV_WIDTH)),
        'cv_ln_g': gain(ks[7], (DEPTH, CV_WIDTH)),
        'cv_ln_b': bias(ks[8], (DEPTH, CV_WIDTH)),
        'cv_pw_w': dense(ks[9], (DEPTH, CV_WIDTH, CV_WIDTH), CV_WIDTH),
        'cv_pw_b': bias(ks[10], (DEPTH, CV_WIDTH)),
        'w_out': dense(ks[11], (DEPTH, MIX_WIDTH, D_MODEL), MIX_WIDTH),
        'mix_norm_post': gain(ks[12], (DEPTH, D_MODEL)),
        'x_norm_pre': gain(ks[13], (DEPTH, D_MODEL)),
        'mem_norm': gain(ks[14], (DEPTH, D_MODEL)),
        'x_wq': dense(ks[15], (DEPTH, D_MODEL, D_MODEL), D_MODEL),
        'x_wk': dense(ks[16], (DEPTH, D_MODEL, D_MODEL), D_MODEL),
        'x_wv': dense(ks[17], (DEPTH, D_MODEL, D_MODEL), D_MODEL),
        'x_wo': dense(ks[18], (DEPTH, D_MODEL, D_MODEL), D_MODEL),
        'x_norm_post': gain(ks[19], (DEPTH, D_MODEL)),
        'ffn_norm_pre': gain(ks[20], (DEPTH, D_MODEL)),
        'ffn_w_up': dense(ks[21], (DEPTH, D_MODEL, 2 * D_FF), D_MODEL),
        'ffn_conv_w': dense(ks[22], (DEPTH, FFN_KERNEL, 2 * D_FF), FFN_KERNEL),
        'ffn_conv_b': bias(ks[23], (DEPTH, 2 * D_FF)),
        'ffn_w_down': dense(ks[24], (DEPTH, D_FF, D_MODEL), D_FF),
        'ffn_norm_post': gain(ks[25], (DEPTH, D_MODEL)),
    }


def _fwd_reference(x, mem, positions, mix_norm_pre, w_in, cv_w, cv_b, cv_ln_g, cv_ln_b, cv_pw_w, cv_pw_b,
              w_out, mix_norm_post, x_norm_pre, mem_norm, x_wq, x_wk, x_wv, x_wo, x_norm_post,
              ffn_norm_pre, ffn_w_up, ffn_conv_w, ffn_conv_b, ffn_w_down, ffn_norm_post):
    h = x
    for l in range(DEPTH):
        y = hybrid_mixer(rms_norm(h, mix_norm_pre[l]), positions, w_in[l], cv_w[l], cv_b[l],
                         cv_ln_g[l], cv_ln_b[l], cv_pw_w[l], cv_pw_b[l], w_out[l])
        h = h + rms_norm(y, mix_norm_post[l])
        y = memory_cross_attention(rms_norm(h, x_norm_pre[l]), rms_norm(mem, mem_norm[l]),
                                   x_wq[l], x_wk[l], x_wv[l], x_wo[l])
        h = h + rms_norm(y, x_norm_post[l])
        y = conv_ffn(rms_norm(h, ffn_norm_pre[l]), ffn_w_up[l], ffn_conv_w[l], ffn_conv_b[l], ffn_w_down[l])
        h = h + rms_norm(y, ffn_norm_post[l])
    return h


import jax as _jax
import jax.numpy as _jnp

TWIN_FORMAT = 'train_step'
FWD_PARAMS = ['x', 'mem', 'positions', 'mix_norm_pre', 'w_in', 'cv_w', 'cv_b', 'cv_ln_g', 'cv_ln_b', 'cv_pw_w', 'cv_pw_b', 'w_out', 'mix_norm_post', 'x_norm_pre', 'mem_norm', 'x_wq', 'x_wk', 'x_wv', 'x_wo', 'x_norm_post', 'ffn_norm_pre', 'ffn_w_up', 'ffn_conv_w', 'ffn_conv_b', 'ffn_w_down', 'ffn_norm_post']
TWIN_WEIGHTS = ['mix_norm_pre', 'w_in', 'cv_w', 'cv_b', 'cv_ln_g', 'cv_ln_b', 'cv_pw_w', 'cv_pw_b', 'w_out', 'mix_norm_post', 'x_norm_pre', 'mem_norm', 'x_wq', 'x_wk', 'x_wv', 'x_wo', 'x_norm_post', 'ffn_norm_pre', 'ffn_w_up', 'ffn_conv_w', 'ffn_conv_b', 'ffn_w_down', 'ffn_norm_post']
TWIN_DIFF_INPUT = 'x'
TWIN_INPUTS = ['x', 'mem', 'positions', 'mix_norm_pre', 'w_in', 'cv_w', 'cv_b', 'cv_ln_g', 'cv_ln_b', 'cv_pw_w', 'cv_pw_b', 'w_out', 'mix_norm_post', 'x_norm_pre', 'mem_norm', 'x_wq', 'x_wk', 'x_wv', 'x_wo', 'x_norm_post', 'ffn_norm_pre', 'ffn_w_up', 'ffn_conv_w', 'ffn_conv_b', 'ffn_w_down', 'ffn_norm_post', 'loss_target', 'm_mix_norm_pre', 'm_w_in', 'm_cv_w', 'm_cv_b', 'm_cv_ln_g', 'm_cv_ln_b', 'm_cv_pw_w', 'm_cv_pw_b', 'm_w_out', 'm_mix_norm_post', 'm_x_norm_pre', 'm_mem_norm', 'm_x_wq', 'm_x_wk', 'm_x_wv', 'm_x_wo', 'm_x_norm_post', 'm_ffn_norm_pre', 'm_ffn_w_up', 'm_ffn_conv_w', 'm_ffn_conv_b', 'm_ffn_w_down', 'm_ffn_norm_post', 'v_mix_norm_pre', 'v_w_in', 'v_cv_w', 'v_cv_b', 'v_cv_ln_g', 'v_cv_ln_b', 'v_cv_pw_w', 'v_cv_pw_b', 'v_w_out', 'v_mix_norm_post', 'v_x_norm_pre', 'v_mem_norm', 'v_x_wq', 'v_x_wk', 'v_x_wv', 'v_x_wo', 'v_x_norm_post', 'v_ffn_norm_pre', 'v_ffn_w_up', 'v_ffn_conv_w', 'v_ffn_conv_b', 'v_ffn_w_down', 'v_ffn_norm_post']
TWIN_OUTPUTS = ['loss', 'grad_x', 'grad_mix_norm_pre', 'grad_w_in', 'grad_cv_w', 'grad_cv_b', 'grad_cv_ln_g', 'grad_cv_ln_b', 'grad_cv_pw_w', 'grad_cv_pw_b', 'grad_w_out', 'grad_mix_norm_post', 'grad_x_norm_pre', 'grad_mem_norm', 'grad_x_wq', 'grad_x_wk', 'grad_x_wv', 'grad_x_wo', 'grad_x_norm_post', 'grad_ffn_norm_pre', 'grad_ffn_w_up', 'grad_ffn_conv_w', 'grad_ffn_conv_b', 'grad_ffn_w_down', 'grad_ffn_norm_post', 'delta_mix_norm_pre', 'delta_w_in', 'delta_cv_w', 'delta_cv_b', 'delta_cv_ln_g', 'delta_cv_ln_b', 'delta_cv_pw_w', 'delta_cv_pw_b', 'delta_w_out', 'delta_mix_norm_post', 'delta_x_norm_pre', 'delta_mem_norm', 'delta_x_wq', 'delta_x_wk', 'delta_x_wv', 'delta_x_wo', 'delta_x_norm_post', 'delta_ffn_norm_pre', 'delta_ffn_w_up', 'delta_ffn_conv_w', 'delta_ffn_conv_b', 'delta_ffn_w_down', 'delta_ffn_norm_post', 'new_m_mix_norm_pre', 'new_m_w_in', 'new_m_cv_w', 'new_m_cv_b', 'new_m_cv_ln_g', 'new_m_cv_ln_b', 'new_m_cv_pw_w', 'new_m_cv_pw_b', 'new_m_w_out', 'new_m_mix_norm_post', 'new_m_x_norm_pre', 'new_m_mem_norm', 'new_m_x_wq', 'new_m_x_wk', 'new_m_x_wv', 'new_m_x_wo', 'new_m_x_norm_post', 'new_m_ffn_norm_pre', 'new_m_ffn_w_up', 'new_m_ffn_conv_w', 'new_m_ffn_conv_b', 'new_m_ffn_w_down', 'new_m_ffn_norm_post', 'new_v_mix_norm_pre', 'new_v_w_in', 'new_v_cv_w', 'new_v_cv_b', 'new_v_cv_ln_g', 'new_v_cv_ln_b', 'new_v_cv_pw_w', 'new_v_cv_pw_b', 'new_v_w_out', 'new_v_mix_norm_post', 'new_v_x_norm_pre', 'new_v_mem_norm', 'new_v_x_wq', 'new_v_x_wk', 'new_v_x_wv', 'new_v_x_wo', 'new_v_x_norm_post', 'new_v_ffn_norm_pre', 'new_v_ffn_w_up', 'new_v_ffn_conv_w', 'new_v_ffn_conv_b', 'new_v_ffn_w_down', 'new_v_ffn_norm_post']
TWIN_LEAF_KINDS = {'loss': 'loss', 'grad_x': 'grad_x', 'grad_mix_norm_pre': 'grad_w', 'grad_w_in': 'grad_w', 'grad_cv_w': 'grad_w', 'grad_cv_b': 'grad_w', 'grad_cv_ln_g': 'grad_w', 'grad_cv_ln_b': 'grad_w', 'grad_cv_pw_w': 'grad_w', 'grad_cv_pw_b': 'grad_w', 'grad_w_out': 'grad_w', 'grad_mix_norm_post': 'grad_w', 'grad_x_norm_pre': 'grad_w', 'grad_mem_norm': 'grad_w', 'grad_x_wq': 'grad_w', 'grad_x_wk': 'grad_w', 'grad_x_wv': 'grad_w', 'grad_x_wo': 'grad_w', 'grad_x_norm_post': 'grad_w', 'grad_ffn_norm_pre': 'grad_w', 'grad_ffn_w_up': 'grad_w', 'grad_ffn_conv_w': 'grad_w', 'grad_ffn_conv_b': 'grad_w', 'grad_ffn_w_down': 'grad_w', 'grad_ffn_norm_post': 'grad_w', 'delta_mix_norm_pre': 'delta_w', 'delta_w_in': 'delta_w', 'delta_cv_w': 'delta_w', 'delta_cv_b': 'delta_w', 'delta_cv_ln_g': 'delta_w', 'delta_cv_ln_b': 'delta_w', 'delta_cv_pw_w': 'delta_w', 'delta_cv_pw_b': 'delta_w', 'delta_w_out': 'delta_w', 'delta_mix_norm_post': 'delta_w', 'delta_x_norm_pre': 'delta_w', 'delta_mem_norm': 'delta_w', 'delta_x_wq': 'delta_w', 'delta_x_wk': 'delta_w', 'delta_x_wv': 'delta_w', 'delta_x_wo': 'delta_w', 'delta_x_norm_post': 'delta_w', 'delta_ffn_norm_pre': 'delta_w', 'delta_ffn_w_up': 'delta_w', 'delta_ffn_conv_w': 'delta_w', 'delta_ffn_conv_b': 'delta_w', 'delta_ffn_w_down': 'delta_w', 'delta_ffn_norm_post': 'delta_w', 'new_m_mix_norm_pre': 'new_m', 'new_m_w_in': 'new_m', 'new_m_cv_w': 'new_m', 'new_m_cv_b': 'new_m', 'new_m_cv_ln_g': 'new_m', 'new_m_cv_ln_b': 'new_m', 'new_m_cv_pw_w': 'new_m', 'new_m_cv_pw_b': 'new_m', 'new_m_w_out': 'new_m', 'new_m_mix_norm_post': 'new_m', 'new_m_x_norm_pre': 'new_m', 'new_m_mem_norm': 'new_m', 'new_m_x_wq': 'new_m', 'new_m_x_wk': 'new_m', 'new_m_x_wv': 'new_m', 'new_m_x_wo': 'new_m', 'new_m_x_norm_post': 'new_m', 'new_m_ffn_norm_pre': 'new_m', 'new_m_ffn_w_up': 'new_m', 'new_m_ffn_conv_w': 'new_m', 'new_m_ffn_conv_b': 'new_m', 'new_m_ffn_w_down': 'new_m', 'new_m_ffn_norm_post': 'new_m', 'new_v_mix_norm_pre': 'new_v', 'new_v_w_in': 'new_v', 'new_v_cv_w': 'new_v', 'new_v_cv_b': 'new_v', 'new_v_cv_ln_g': 'new_v', 'new_v_cv_ln_b': 'new_v', 'new_v_cv_pw_w': 'new_v', 'new_v_cv_pw_b': 'new_v', 'new_v_w_out': 'new_v', 'new_v_mix_norm_post': 'new_v', 'new_v_x_norm_pre': 'new_v', 'new_v_mem_norm': 'new_v', 'new_v_x_wq': 'new_v', 'new_v_x_wk': 'new_v', 'new_v_x_wv': 'new_v', 'new_v_x_wo': 'new_v', 'new_v_x_norm_post': 'new_v', 'new_v_ffn_norm_pre': 'new_v', 'new_v_ffn_w_up': 'new_v', 'new_v_ffn_conv_w': 'new_v', 'new_v_ffn_conv_b': 'new_v', 'new_v_ffn_w_down': 'new_v', 'new_v_ffn_norm_post': 'new_v'}


def _forward(args):
    return _fwd_reference(*[args[k] for k in FWD_PARAMS])


def _output_shape():
    out = _jax.eval_shape(lambda: _forward(_fwd_setup_inputs(0)))
    return out.shape, out.dtype

N_MICROBATCH = 1
ADAM_LR = 0.001
ADAM_B1 = 0.9
ADAM_B2 = 0.999
ADAM_EPS = 1e-08
ADAM_WD = 0.01
ADAM_STEP = 10
PER_EXAMPLE_BATCH_AXIS = {'x': 0, 'mem': 0, 'positions': 0, 'loss_target': 0}
SHARED_INPUTS = []
_WEIGHT_DTYPES = {'mix_norm_pre': _jnp.float32, 'w_in': _jnp.float32, 'cv_w': _jnp.float32, 'cv_b': _jnp.float32, 'cv_ln_g': _jnp.float32, 'cv_ln_b': _jnp.float32, 'cv_pw_w': _jnp.float32, 'cv_pw_b': _jnp.float32, 'w_out': _jnp.float32, 'mix_norm_post': _jnp.float32, 'x_norm_pre': _jnp.float32, 'mem_norm': _jnp.float32, 'x_wq': _jnp.float32, 'x_wk': _jnp.float32, 'x_wv': _jnp.float32, 'x_wo': _jnp.float32, 'x_norm_post': _jnp.float32, 'ffn_norm_pre': _jnp.float32, 'ffn_w_up': _jnp.float32, 'ffn_conv_w': _jnp.float32, 'ffn_conv_b': _jnp.float32, 'ffn_w_down': _jnp.float32, 'ffn_norm_post': _jnp.float32}
MOMENT_SCALE = {'mix_norm_pre': 5.216061e+00, 'w_in': 3.224351e+00, 'cv_w': 3.622196e+00, 'cv_b': 4.309268e+01, 'cv_ln_g': 1.720558e+01, 'cv_ln_b': 2.318519e+01, 'cv_pw_w': 1.033118e+01, 'cv_pw_b': 4.752930e+01, 'w_out': 7.592994e+00, 'mix_norm_post': 3.348552e+01, 'x_norm_pre': 2.905475e+00, 'mem_norm': 1.201593e+01, 'x_wq': 2.949433e+00, 'x_wk': 2.967743e+00, 'x_wv': 1.160632e+01, 'x_wo': 1.170498e+01, 'x_norm_post': 3.458064e+01, 'ffn_norm_pre': 4.491227e+00, 'ffn_w_up': 1.844864e+00, 'ffn_conv_w': 2.301448e+00, 'ffn_conv_b': 5.794906e+00, 'ffn_w_down': 4.114114e+00, 'ffn_norm_post': 3.231548e+01}


def _to_microbatches(a, axis):
    t = _jnp.moveaxis(a, axis, 0)
    t = t.reshape((N_MICROBATCH, t.shape[0] // N_MICROBATCH) + t.shape[1:])
    return _jnp.moveaxis(t, 1, axis + 1)


def setup_inputs(seed: int = 0) -> dict:
    inp = _fwd_setup_inputs(seed)
    key = _jax.random.fold_in(_jax.random.key(seed), 7919)
    shape, _ = _output_shape()
    out = dict(inp)
    out["loss_target"] = _jax.random.normal(_jax.random.fold_in(key, 0), shape, _jnp.float32)
    for i, name in enumerate(TWIN_WEIGHTS):
        w = inp[name].astype(_jnp.float32)
        if MOMENT_SCALE is None:
            s = _jnp.sqrt(_jnp.mean(_jnp.square(w)) + 1e-30)
        else:
            s = MOMENT_SCALE[name]
        km, kv = _jax.random.split(_jax.random.fold_in(key, i + 1))
        out[name] = w
        out["m_" + name] = s * _jax.random.normal(km, w.shape, _jnp.float32)
        out["v_" + name] = (s * s) * _jax.random.uniform(kv, w.shape, _jnp.float32, 0.5, 1.5)
    if N_MICROBATCH > 1:
        for name, axis in PER_EXAMPLE_BATCH_AXIS.items():
            out[name] = _to_microbatches(out[name], axis)
    return {'x': out['x'], 'mem': out['mem'], 'positions': out['positions'], 'mix_norm_pre': out['mix_norm_pre'], 'w_in': out['w_in'], 'cv_w': out['cv_w'], 'cv_b': out['cv_b'], 'cv_ln_g': out['cv_ln_g'], 'cv_ln_b': out['cv_ln_b'], 'cv_pw_w': out['cv_pw_w'], 'cv_pw_b': out['cv_pw_b'], 'w_out': out['w_out'], 'mix_norm_post': out['mix_norm_post'], 'x_norm_pre': out['x_norm_pre'], 'mem_norm': out['mem_norm'], 'x_wq': out['x_wq'], 'x_wk': out['x_wk'], 'x_wv': out['x_wv'], 'x_wo': out['x_wo'], 'x_norm_post': out['x_norm_post'], 'ffn_norm_pre': out['ffn_norm_pre'], 'ffn_w_up': out['ffn_w_up'], 'ffn_conv_w': out['ffn_conv_w'], 'ffn_conv_b': out['ffn_conv_b'], 'ffn_w_down': out['ffn_w_down'], 'ffn_norm_post': out['ffn_norm_post'], 'loss_target': out['loss_target'], 'm_mix_norm_pre': out['m_mix_norm_pre'], 'm_w_in': out['m_w_in'], 'm_cv_w': out['m_cv_w'], 'm_cv_b': out['m_cv_b'], 'm_cv_ln_g': out['m_cv_ln_g'], 'm_cv_ln_b': out['m_cv_ln_b'], 'm_cv_pw_w': out['m_cv_pw_w'], 'm_cv_pw_b': out['m_cv_pw_b'], 'm_w_out': out['m_w_out'], 'm_mix_norm_post': out['m_mix_norm_post'], 'm_x_norm_pre': out['m_x_norm_pre'], 'm_mem_norm': out['m_mem_norm'], 'm_x_wq': out['m_x_wq'], 'm_x_wk': out['m_x_wk'], 'm_x_wv': out['m_x_wv'], 'm_x_wo': out['m_x_wo'], 'm_x_norm_post': out['m_x_norm_post'], 'm_ffn_norm_pre': out['m_ffn_norm_pre'], 'm_ffn_w_up': out['m_ffn_w_up'], 'm_ffn_conv_w': out['m_ffn_conv_w'], 'm_ffn_conv_b': out['m_ffn_conv_b'], 'm_ffn_w_down': out['m_ffn_w_down'], 'm_ffn_norm_post': out['m_ffn_norm_post'], 'v_mix_norm_pre': out['v_mix_norm_pre'], 'v_w_in': out['v_w_in'], 'v_cv_w': out['v_cv_w'], 'v_cv_b': out['v_cv_b'], 'v_cv_ln_g': out['v_cv_ln_g'], 'v_cv_ln_b': out['v_cv_ln_b'], 'v_cv_pw_w': out['v_cv_pw_w'], 'v_cv_pw_b': out['v_cv_pw_b'], 'v_w_out': out['v_w_out'], 'v_mix_norm_post': out['v_mix_norm_post'], 'v_x_norm_pre': out['v_x_norm_pre'], 'v_mem_norm': out['v_mem_norm'], 'v_x_wq': out['v_x_wq'], 'v_x_wk': out['v_x_wk'], 'v_x_wv': out['v_x_wv'], 'v_x_wo': out['v_x_wo'], 'v_x_norm_post': out['v_x_norm_post'], 'v_ffn_norm_pre': out['v_ffn_norm_pre'], 'v_ffn_w_up': out['v_ffn_w_up'], 'v_ffn_conv_w': out['v_ffn_conv_w'], 'v_ffn_conv_b': out['v_ffn_conv_b'], 'v_ffn_w_down': out['v_ffn_w_down'], 'v_ffn_norm_post': out['v_ffn_norm_post']}


def _loss(weights, diff, rest, loss_target):
    with _jax.named_scope("forward"):
        args = {**rest, TWIN_DIFF_INPUT: diff, **{k: w.astype(_WEIGHT_DTYPES[k]) for k, w in weights.items()}}
        y = _forward(args)
    with _jax.named_scope("loss_head"):
        err = _jnp.square(y.astype(_jnp.float32) - loss_target)
        return 0.5 * _jnp.sum(_jnp.mean(err, axis=-1)) if err.ndim else 0.5 * err


def _adamw(w, g, m, v):
    m = ADAM_B1 * m + (1.0 - ADAM_B1) * g
    v = ADAM_B2 * v + (1.0 - ADAM_B2) * _jnp.square(g)
    m_hat = m / (1.0 - ADAM_B1 ** ADAM_STEP)
    v_hat = v / (1.0 - ADAM_B2 ** ADAM_STEP)
    delta = -ADAM_LR * (m_hat / (_jnp.sqrt(v_hat) + ADAM_EPS) + ADAM_WD * w)
    return delta, m, v


def reference(x, mem, positions, mix_norm_pre, w_in, cv_w, cv_b, cv_ln_g, cv_ln_b, cv_pw_w, cv_pw_b, w_out, mix_norm_post, x_norm_pre, mem_norm, x_wq, x_wk, x_wv, x_wo, x_norm_post, ffn_norm_pre, ffn_w_up, ffn_conv_w, ffn_conv_b, ffn_w_down, ffn_norm_post, loss_target, m_mix_norm_pre, m_w_in, m_cv_w, m_cv_b, m_cv_ln_g, m_cv_ln_b, m_cv_pw_w, m_cv_pw_b, m_w_out, m_mix_norm_post, m_x_norm_pre, m_mem_norm, m_x_wq, m_x_wk, m_x_wv, m_x_wo, m_x_norm_post, m_ffn_norm_pre, m_ffn_w_up, m_ffn_conv_w, m_ffn_conv_b, m_ffn_w_down, m_ffn_norm_post, v_mix_norm_pre, v_w_in, v_cv_w, v_cv_b, v_cv_ln_g, v_cv_ln_b, v_cv_pw_w, v_cv_pw_b, v_w_out, v_mix_norm_post, v_x_norm_pre, v_mem_norm, v_x_wq, v_x_wk, v_x_wv, v_x_wo, v_x_norm_post, v_ffn_norm_pre, v_ffn_w_up, v_ffn_conv_w, v_ffn_conv_b, v_ffn_w_down, v_ffn_norm_post):
    given = dict(x=x, mem=mem, positions=positions, mix_norm_pre=mix_norm_pre, w_in=w_in, cv_w=cv_w, cv_b=cv_b, cv_ln_g=cv_ln_g, cv_ln_b=cv_ln_b, cv_pw_w=cv_pw_w, cv_pw_b=cv_pw_b, w_out=w_out, mix_norm_post=mix_norm_post, x_norm_pre=x_norm_pre, mem_norm=mem_norm, x_wq=x_wq, x_wk=x_wk, x_wv=x_wv, x_wo=x_wo, x_norm_post=x_norm_post, ffn_norm_pre=ffn_norm_pre, ffn_w_up=ffn_w_up, ffn_conv_w=ffn_conv_w, ffn_conv_b=ffn_conv_b, ffn_w_down=ffn_w_down, ffn_norm_post=ffn_norm_post, loss_target=loss_target, m_mix_norm_pre=m_mix_norm_pre, m_w_in=m_w_in, m_cv_w=m_cv_w, m_cv_b=m_cv_b, m_cv_ln_g=m_cv_ln_g, m_cv_ln_b=m_cv_ln_b, m_cv_pw_w=m_cv_pw_w, m_cv_pw_b=m_cv_pw_b, m_w_out=m_w_out, m_mix_norm_post=m_mix_norm_post, m_x_norm_pre=m_x_norm_pre, m_mem_norm=m_mem_norm, m_x_wq=m_x_wq, m_x_wk=m_x_wk, m_x_wv=m_x_wv, m_x_wo=m_x_wo, m_x_norm_post=m_x_norm_post, m_ffn_norm_pre=m_ffn_norm_pre, m_ffn_w_up=m_ffn_w_up, m_ffn_conv_w=m_ffn_conv_w, m_ffn_conv_b=m_ffn_conv_b, m_ffn_w_down=m_ffn_w_down, m_ffn_norm_post=m_ffn_norm_post, v_mix_norm_pre=v_mix_norm_pre, v_w_in=v_w_in, v_cv_w=v_cv_w, v_cv_b=v_cv_b, v_cv_ln_g=v_cv_ln_g, v_cv_ln_b=v_cv_ln_b, v_cv_pw_w=v_cv_pw_w, v_cv_pw_b=v_cv_pw_b, v_w_out=v_w_out, v_mix_norm_post=v_mix_norm_post, v_x_norm_pre=v_x_norm_pre, v_mem_norm=v_mem_norm, v_x_wq=v_x_wq, v_x_wk=v_x_wk, v_x_wv=v_x_wv, v_x_wo=v_x_wo, v_x_norm_post=v_x_norm_post, v_ffn_norm_pre=v_ffn_norm_pre, v_ffn_w_up=v_ffn_w_up, v_ffn_conv_w=v_ffn_conv_w, v_ffn_conv_b=v_ffn_conv_b, v_ffn_w_down=v_ffn_w_down, v_ffn_norm_post=v_ffn_norm_post)
    weights = {n: given[n] for n in TWIN_WEIGHTS}
    shared = {n: given[n] for n in SHARED_INPUTS}
    per_example = {n: given[n] for n in ['x', 'mem', 'positions']}
    grad_fn = _jax.value_and_grad(_loss, argnums=(0, 1))

    def one_microbatch(ex, loss_target):
        ex = dict(ex)
        diff = ex.pop(TWIN_DIFF_INPUT)
        return grad_fn(weights, diff, {**shared, **ex}, loss_target)

    if N_MICROBATCH == 1:
        loss, (grad_w, grad_x) = one_microbatch(per_example, given["loss_target"])
    else:
        def body(carry, xs):
            loss_sum, grad_sum = carry
            l_k, (gw_k, gx_k) = one_microbatch(xs[0], xs[1])
            with _jax.named_scope("update"):
                return (loss_sum + l_k, _jax.tree.map(_jnp.add, grad_sum, gw_k)), gx_k

        init = (_jnp.zeros((), _jnp.float32), _jax.tree.map(_jnp.zeros_like, weights))
        (loss, grad_w), grad_x = _jax.lax.scan(body, init, (per_example, given["loss_target"]))
    with _jax.named_scope("update"):
        delta_w, new_m, new_v = {}, {}, {}
        for n in TWIN_WEIGHTS:
            delta_w[n], new_m[n], new_v[n] = _adamw(weights[n], grad_w[n], given["m_" + n], given["v_" + n])
    return (loss, grad_x, *[grad_w[n] for n in TWIN_WEIGHTS], *[delta_w[n] for n in TWIN_WEIGHTS],
            *[new_m[n] for n in TWIN_WEIGHTS], *[new_v[n] for n in TWIN_WEIGHTS])
```

```python
import functools
import math

import jax
import jax.numpy as jnp
from jax import lax
from jax.experimental import pallas as pl
from jax.experimental.pallas import tpu as pltpu

F32, BF16 = jnp.float32, jnp.bfloat16
SDS = jax.ShapeDtypeStruct

D_MODEL = 1024
SEQ = 4096
DEPTH = 2
HEAD_DIM = 64
SB_HEADS = 4
SB_WIDTH = 256
CV_WIDTH = 256
CV_KERNEL = 31
DL_HEADS = 8
DL_WIDTH = 512
IN_WIDTH = 2816
DL_PATTERN = ((128, 1), (512, 4), (2048, 16))
BLOCK = 128
ROPE_THETA = 10000.0
N_MEM = 256
X_HEADS = 4
X_HEAD_DIM = 256
D_FF = 2816
EPS = 1e-6
N_DEV = 8
LANES = 128

ADAM_LR = 0.001
ADAM_B1 = 0.9
ADAM_B2 = 0.999
ADAM_EPS = 1e-08
ADAM_WD = 0.01
ADAM_STEP = 10

VMEM_LIMIT_BYTES = 56 * 1024 * 1024
MESH = pl.DeviceIdType.MESH
NEG = -1e30


def _params(**kw):
    return pltpu.CompilerParams(vmem_limit_bytes=VMEM_LIMIT_BYTES, **kw)


def _pick(n, cands):
    for c in cands:
        if n % c == 0:
            return c
    return n


def _mm(a, b, mode, out_dtype, name, bias=None):
    if mode == "nn":
        (m, k), (k2, n) = a.shape, b.shape
    elif mode == "nt":
        (m, k), (n, k2) = a.shape, b.shape
    else:
        (k, m), (k2, n) = a.shape, b.shape
    assert k == k2, (a.shape, b.shape, mode)
    tm = _pick(m, (1024, 1408, 512, 256, 128))
    tn = _pick(n, (512, 256, 128))
    tk = k if k <= 2048 else _pick(k, (2048, 1408, 1024, 512))
    nk = k // tk
    dims = {"nn": ((1,), (0,)), "nt": ((1,), (1,)), "tn": ((0,), (0,))}[mode]

    def body(*refs):
        if bias is None:
            a_ref, b_ref, o_ref, acc_ref = refs
            bias_ref = None
        else:
            a_ref, b_ref, bias_ref, o_ref, acc_ref = refs
        p = lax.dot_general(a_ref[...].astype(BF16), b_ref[...].astype(BF16), (dims, ((), ())),
                            preferred_element_type=F32)

        def finish(v):
            if bias_ref is not None:
                v = v + bias_ref[...]
            o_ref[...] = v.astype(out_dtype)

        if nk == 1:
            finish(p)
        else:
            kk = pl.program_id(2)

            @pl.when(kk == 0)
            def _():
                acc_ref[...] = p

            @pl.when(kk > 0)
            def _():
                acc_ref[...] += p

            @pl.when(kk == nk - 1)
            def _():
                finish(acc_ref[...])

    a_spec = pl.BlockSpec((tk, tm), lambda i, j, kk: (kk, i)) if mode == "tn" else pl.BlockSpec((tm, tk), lambda i, j, kk: (i, kk))
    b_spec = pl.BlockSpec((tn, tk), lambda i, j, kk: (j, kk)) if mode == "nt" else pl.BlockSpec((tk, tn), lambda i, j, kk: (kk, j))
    in_specs = [a_spec, b_spec]
    args = [a, b]
    if bias is not None:
        in_specs.append(pl.BlockSpec((1, tn), lambda i, j, kk: (0, j)))
        args.append(bias)
    return pl.pallas_call(
        body, name=name, out_shape=SDS((m, n), out_dtype), grid=(m // tm, n // tn, nk),
        in_specs=in_specs, out_specs=pl.BlockSpec((tm, tn), lambda i, j, kk: (i, j)),
        scratch_shapes=[pltpu.VMEM((tm, tn), F32)], compiler_params=_params(),
    )(*args)


def _rms(x, g):
    r = lax.rsqrt(jnp.mean(x * x, axis=-1, keepdims=True) + EPS)
    return x * r * g


def _rms_bwd(x, g, dy):
    r = lax.rsqrt(jnp.mean(x * x, axis=-1, keepdims=True) + EPS)
    xh = x * r
    dyg = dy * g
    dx = r * (dyg - xh * jnp.mean(dyg * xh, axis=-1, keepdims=True))
    return dx, dy * xh


def _rms_fwd(x, g, name):
    rows, d = x.shape
    t = min(rows, 512)

    def body(x_ref, g_ref, o_ref):
        o_ref[...] = _rms(x_ref[...], g_ref[...]).astype(BF16)

    return pl.pallas_call(
        body, name=name, out_shape=SDS((rows, d), BF16), grid=(rows // t,),
        in_specs=[pl.BlockSpec((t, d), lambda i: (i, 0)), pl.BlockSpec((1, d), lambda i: (0, 0))],
        out_specs=pl.BlockSpec((t, d), lambda i: (i, 0)), compiler_params=_params(),
    )(x, g)


def _res_norm_fwd(h, y, g_post, g_next, name):
    rows, d = h.shape
    t = 512
    has_next = g_next is not None

    def body(*refs):
        if has_next:
            h_ref, y_ref, gp_ref, gn_ref, h1_ref, hn_ref = refs
        else:
            h_ref, y_ref, gp_ref, h1_ref = refs
        h1 = h_ref[...] + _rms(y_ref[...], gp_ref[...])
        h1_ref[...] = h1
        if has_next:
            hn_ref[...] = _rms(h1, gn_ref[...]).astype(BF16)

    row = pl.BlockSpec((t, d), lambda i: (i, 0))
    vec = pl.BlockSpec((1, d), lambda i: (0, 0))
    in_specs = [row, row, vec] + ([vec] if has_next else [])
    args = [h, y, g_post] + ([g_next] if has_next else [])
    out_shape = [SDS((rows, d), F32)] + ([SDS((rows, d), BF16)] if has_next else [])
    out_specs = [row] + ([row] if has_next else [])
    res = pl.pallas_call(body, name=name, out_shape=out_shape, grid=(rows // t,), in_specs=in_specs,
                         out_specs=out_specs, compiler_params=_params())(*args)
    return (res[0], res[1]) if has_next else (res[0], None)


def _norm_bwd(dh, pre, post, name):
    rows, d = dh.shape
    t = 512
    has_pre, has_post = pre is not None, post is not None

    def body(*refs):
        refs = list(refs)
        dh_ref = refs.pop(0)
        if has_pre:
            h_ref, gpre_ref, dhn_ref = refs.pop(0), refs.pop(0), refs.pop(0)
        if has_post:
            y_ref, gpost_ref = refs.pop(0), refs.pop(0)
        dht_ref = refs.pop(0)
        if has_post:
            dy_ref = refs.pop(0)
        if has_pre:
            dgpre_ref = refs.pop(0)
        if has_post:
            dgpost_ref = refs.pop(0)
        i = pl.program_id(0)
        dht = dh_ref[...]
        if has_pre:
            dx, dgr = _rms_bwd(h_ref[...], gpre_ref[...], dhn_ref[...])
            dht = dht + dx

            @pl.when(i == 0)
            def _():
                dgpre_ref[...] = jnp.zeros_like(dgpre_ref)

            dgpre_ref[...] += jnp.sum(dgr, axis=0, keepdims=True)
        dht_ref[...] = dht
        if has_post:
            dy, dgr = _rms_bwd(y_ref[...], gpost_ref[...], dht)
            dy_ref[...] = dy.astype(BF16)

            @pl.when(i == 0)
            def _():
                dgpost_ref[...] = jnp.zeros_like(dgpost_ref)

            dgpost_ref[...] += jnp.sum(dgr, axis=0, keepdims=True)

    row = pl.BlockSpec((t, d), lambda i: (i, 0))
    vec = pl.BlockSpec((1, d), lambda i: (0, 0))
    in_specs, args = [row], [dh]
    if has_pre:
        in_specs += [row, vec, row]
        args += list(pre)
    if has_post:
        in_specs += [row, vec]
        args += list(post)
    out_shape, out_specs = [SDS((rows, d), F32)], [row]
    if has_post:
        out_shape.append(SDS((rows, d), BF16))
        out_specs.append(row)
    if has_pre:
        out_shape.append(SDS((1, d), F32))
        out_specs.append(vec)
    if has_post:
        out_shape.append(SDS((1, d), F32))
        out_specs.append(vec)
    res = list(pl.pallas_call(body, name=name, out_shape=out_shape, grid=(rows // t,), in_specs=in_specs,
                              out_specs=out_specs, compiler_params=_params())(*args))
    dht = res.pop(0)
    dy = res.pop(0) if has_post else None
    dgpre = res.pop(0) if has_pre else None
    dgpost = res.pop(0) if has_post else None
    return dht, dy, dgpre, dgpost


def _rms_gain_grad(x, g, dy, name):
    rows, d = x.shape

    def body(x_ref, g_ref, dy_ref, dg_ref):
        _, dgr = _rms_bwd(x_ref[...], g_ref[...], dy_ref[...])
        dg_ref[...] = jnp.sum(dgr, axis=0, keepdims=True)

    return pl.pallas_call(body, name=name, out_shape=SDS((1, d), F32), compiler_params=_params())(x, g, dy)


def _loss_fwd(h, target, name):
    rows, d = h.shape
    t = 512

    def body(h_ref, t_ref, loss_ref, dh_ref):
        i = pl.program_id(0)
        err = h_ref[...] - t_ref[...]
        dh_ref[...] = err * (1.0 / d)

        @pl.when(i == 0)
        def _():
            loss_ref[...] = jnp.zeros_like(loss_ref)

        part = jnp.sum(jnp.sum(err * err, axis=1, keepdims=True), axis=0, keepdims=True) * (0.5 / d)
        loss_ref[...] += jnp.broadcast_to(part, loss_ref.shape)

    row = pl.BlockSpec((t, d), lambda i: (i, 0))
    return pl.pallas_call(
        body, name=name, out_shape=(SDS((1, LANES), F32), SDS((rows, d), F32)), grid=(rows // t,),
        in_specs=[row, row], out_specs=(pl.BlockSpec((1, LANES), lambda i: (0, 0)), row), compiler_params=_params(),
    )(h, target)


def _rot_half(x, sign):
    w = x.shape[-1]
    lane = lax.broadcasted_iota(jnp.int32, x.shape, 1)
    first = (lane % HEAD_DIM) < (HEAD_DIM // 2)
    return jnp.where(first, -sign * pltpu.roll(x, w - HEAD_DIM // 2, axis=1), sign * pltpu.roll(x, HEAD_DIM // 2, axis=1))


def _rope_fwd(u, cos, sin, name):
    rows = u.shape[0]
    t, cw = 512, 256
    first_col = (3 * SB_WIDTH + 2 * CV_WIDTH) // cw

    def body(u_ref, c_ref, s_ref, o_ref):
        x = u_ref[...]
        c = jnp.tile(c_ref[...], (1, cw // LANES))
        s = jnp.tile(s_ref[...], (1, cw // LANES))
        o_ref[...] = (x * c + _rot_half(x, 1.0) * s).astype(BF16)

    tab = pl.BlockSpec((t, LANES), lambda i, j: (i, 0))
    return pl.pallas_call(
        body, name=name, out_shape=SDS((rows, 2 * DL_WIDTH), BF16), grid=(rows // t, 2 * DL_WIDTH // cw),
        in_specs=[pl.BlockSpec((t, cw), lambda i, j: (i, first_col + j)), tab, tab],
        out_specs=pl.BlockSpec((t, cw), lambda i, j: (i, j)), compiler_params=_params(),
    )(u, cos, sin)


def _rope_bwd(dqs, dks, dvs, cos, sin, name):
    rows = dqs[0].shape[0]
    t, w = 256, DL_WIDTH

    def body(*refs):
        c = jnp.tile(refs[9][...], (1, w // LANES))
        s = jnp.tile(refs[10][...], (1, w // LANES))
        o_ref = refs[11]
        dq = refs[0][...] + refs[1][...] + refs[2][...]
        dk = refs[3][...] + refs[4][...] + refs[5][...]
        dv = refs[6][...] + refs[7][...] + refs[8][...]
        o_ref[:, 0:w] = (dq * c + _rot_half(dq, -1.0) * s).astype(BF16)
        o_ref[:, w:2 * w] = (dk * c + _rot_half(dk, -1.0) * s).astype(BF16)
        o_ref[:, 2 * w:3 * w] = dv.astype(BF16)

    row = pl.BlockSpec((t, w), lambda i: (i, 0))
    tab = pl.BlockSpec((t, LANES), lambda i: (i, 0))
    return pl.pallas_call(
        body, name=name, out_shape=SDS((rows, 3 * w), BF16), grid=(rows // t,), in_specs=[row] * 9 + [tab, tab],
        out_specs=pl.BlockSpec((t, 3 * w), lambda i: (i, 0)), compiler_params=_params(),
    )(*dqs, *dks, *dvs, cos, sin)


SB_TILE = 256


def _softplus(z):
    return jnp.maximum(z, 0.0) + jnp.log(1.0 + jnp.exp(-jnp.abs(z)))


def _split_dot(x, tri, passes):
    acc = None
    rem = x
    for _ in range(passes):
        part = rem.astype(BF16)
        rem = rem - part.astype(F32)
        d = jnp.dot(part, tri, preferred_element_type=F32)
        acc = d if acc is None else acc + d
    return acc


def _tri(t, rel):
    j = lax.broadcasted_iota(jnp.int32, (t, t), 0)
    s = lax.broadcasted_iota(jnp.int32, (t, t), 1)
    return rel(j, s).astype(BF16)


def _sb_fwd(q, k, v, name):
    h, s_len, hd = q.shape
    t = SB_TILE
    nq = s_len // t
    scale = hd ** -0.5

    def body(q_ref, k_ref, v_ref, o_ref, tot_ref):
        i = pl.program_id(1)
        qv = q_ref[0]
        upper = _tri(t, lambda j, s: j > s)
        row = lax.broadcasted_iota(jnp.int32, (t, t), 0)
        col = lax.broadcasted_iota(jnp.int32, (t, t), 1)

        def step(jj, carry):
            acc, run = carry
            j = i - jj
            start = pl.multiple_of(j * t, t)
            kv = k_ref[0, pl.ds(start, t), :]
            vv = v_ref[0, pl.ds(start, t), :]
            z = lax.dot_general(qv, kv, (((1,), (1,)), ((), ())), preferred_element_type=F32) * scale
            mask = (j * t + col) < (i * t + row)
            sp = _softplus(z)
            spm = jnp.where(mask, sp, 0.0)
            later = _split_dot(spm, upper, 2)
            a = jnp.where(mask, jnp.exp((z - sp) - (run + later)), 0.0)
            acc = acc + jnp.dot(a.astype(BF16), vv, preferred_element_type=F32)
            run = run + jnp.sum(spm, axis=1, keepdims=True)
            return acc, run

        acc, run = lax.fori_loop(0, i + 1, step, (jnp.zeros((t, hd), F32), jnp.zeros((t, 1), F32)))
        o_ref[0] = acc.astype(BF16)
        tot_ref[0] = run

    full = pl.BlockSpec((1, s_len, hd), lambda hh, i: (hh, 0, 0))
    tile = pl.BlockSpec((1, t, hd), lambda hh, i: (hh, i, 0))
    return pl.pallas_call(
        body, name=name, out_shape=(SDS((h, s_len, hd), BF16), SDS((h, s_len, 1), F32)), grid=(h, nq),
        in_specs=[tile, full, full], out_specs=(tile, pl.BlockSpec((1, t, 1), lambda hh, i: (hh, i, 0))),
        compiler_params=_params(),
    )(q, k, v)


def _sb_bwd(q, k, v, do, tot, name):
    h, s_len, hd = q.shape
    t = SB_TILE
    nq = s_len // t
    scale = hd ** -0.5

    def body(q_ref, k_ref, v_ref, do_ref, tot_ref, dq_ref, dk_ref, dv_ref, dk_acc, dv_acc):
        i = pl.program_id(1)

        @pl.when(i == 0)
        def _():
            dk_acc[...] = jnp.zeros_like(dk_acc)
            dv_acc[...] = jnp.zeros_like(dv_acc)

        qv = q_ref[0]
        dov = do_ref[0]
        total = tot_ref[0]
        upto = _tri(t, lambda j, s: j <= s)
        before = _tri(t, lambda j, s: j < s)
        row = lax.broadcasted_iota(jnp.int32, (t, t), 0)
        col = lax.broadcasted_iota(jnp.int32, (t, t), 1)

        def step(j, carry):
            dq, run_sp, run_g = carry
            start = pl.multiple_of(j * t, t)
            kv = k_ref[0, pl.ds(start, t), :]
            vv = v_ref[0, pl.ds(start, t), :]
            z = lax.dot_general(qv, kv, (((1,), (1,)), ((), ())), preferred_element_type=F32) * scale
            mask = (j * t + col) < (i * t + row)
            sp = _softplus(z)
            spm = jnp.where(mask, sp, 0.0)
            between = total - (run_sp + _split_dot(spm, upto, 2))
            log_sig = z - sp
            a = jnp.where(mask, jnp.exp(log_sig - between), 0.0)
            da = lax.dot_general(dov, vv, (((1,), (1,)), ((), ())), preferred_element_type=F32)
            g = a * da
            g_before = run_g + _split_dot(g, before, 3)
            sig = jnp.exp(log_sig)
            dz = jnp.where(mask, (g * (1.0 - sig) - sig * g_before) * scale, 0.0).astype(BF16)
            dq = dq + jnp.dot(dz, kv, preferred_element_type=F32)
            dk_acc[pl.ds(start, t), :] += lax.dot_general(dz, qv, (((0,), (0,)), ((), ())), preferred_element_type=F32)
            dv_acc[pl.ds(start, t), :] += lax.dot_general(a.astype(BF16), dov, (((0,), (0,)), ((), ())),
                                                          preferred_element_type=F32)
            return dq, run_sp + jnp.sum(spm, axis=1, keepdims=True), run_g + jnp.sum(g, axis=1, keepdims=True)

        zero = jnp.zeros((t, 1), F32)
        dq, _, _ = lax.fori_loop(0, i + 1, step, (jnp.zeros((t, hd), F32), zero, zero))
        dq_ref[0] = dq.astype(BF16)

        @pl.when(i == nq - 1)
        def _():
            dk_ref[0] = dk_acc[...].astype(BF16)
            dv_ref[0] = dv_acc[...].astype(BF16)

    full = pl.BlockSpec((1, s_len, hd), lambda hh, i: (hh, 0, 0))
    tile = pl.BlockSpec((1, t, hd), lambda hh, i: (hh, i, 0))
    out = SDS((h, s_len, hd), BF16)
    return pl.pallas_call(
        body, name=name, out_shape=(out, out, out), grid=(h, nq),
        in_specs=[tile, full, full, tile, pl.BlockSpec((1, t, 1), lambda hh, i: (hh, i, 0))],
        out_specs=(tile, full, full), scratch_shapes=[pltpu.VMEM((s_len, hd), F32), pltpu.VMEM((s_len, hd), F32)],
        compiler_params=_params(),
    )(q, k, v, do, tot)


def _dl_scores(qv, kk, n):
    s = lax.dot_general(qv, kk, (((1,), (1,)), ((), ())), preferred_element_type=F32) * (HEAD_DIM ** -0.5)
    r = lax.broadcasted_iota(jnp.int32, s.shape, 0)
    c = lax.broadcasted_iota(jnp.int32, s.shape, 1)
    valid = (c >= r) & (c - r <= BLOCK) & ((n > 0) | (c >= BLOCK))
    return jnp.where(valid, s, NEG)


def _dl_window(ref, n):
    prev = ref[0, pl.ds(pl.multiple_of(jnp.maximum(n - 1, 0) * BLOCK, BLOCK), BLOCK), :]
    cur = ref[0, pl.ds(pl.multiple_of(n * BLOCK, BLOCK), BLOCK), :]
    return jnp.concatenate([prev, cur], axis=0)


def _dl_fwd(q, k, v, name):
    g, l, hd = q.shape
    nb = l // BLOCK

    def body(q_ref, k_ref, v_ref, o_ref, lse_ref):
        def step(n, _):
            rows = pl.ds(pl.multiple_of(n * BLOCK, BLOCK), BLOCK)
            s = _dl_scores(q_ref[0, rows, :], _dl_window(k_ref, n), n)
            m = jnp.max(s, axis=-1, keepdims=True)
            p = jnp.exp(s - m)
            den = jnp.sum(p, axis=-1, keepdims=True)
            o_ref[0, rows, :] = jnp.dot((p / den).astype(BF16), _dl_window(v_ref, n), preferred_element_type=F32)
            lse_ref[0, rows, :] = jnp.broadcast_to(m + jnp.log(den), (BLOCK, hd))
            return 0

        lax.fori_loop(0, nb, step, 0)

    spec = pl.BlockSpec((1, l, hd), lambda i: (i, 0, 0))
    out = SDS((g, l, hd), F32)
    return pl.pallas_call(body, name=name, out_shape=(out, out), grid=(g,), in_specs=[spec] * 3, out_specs=(spec, spec),
                          compiler_params=_params())(q, k, v)


def _dl_bwd(q, k, v, do, o_mix, wt, lse, name):
    g, l, hd = q.shape
    nb = l // BLOCK
    scale = HEAD_DIM ** -0.5

    def body(q_ref, k_ref, v_ref, do_ref, om_ref, wt_ref, lse_ref, dq_ref, dk_ref, dv_ref):
        dk_ref[...] = jnp.zeros_like(dk_ref)
        dv_ref[...] = jnp.zeros_like(dv_ref)

        def step(n, _):
            rows = pl.ds(pl.multiple_of(n * BLOCK, BLOCK), BLOCK)
            prev = pl.ds(pl.multiple_of(jnp.maximum(n - 1, 0) * BLOCK, BLOCK), BLOCK)
            qv = q_ref[0, rows, :]
            kk = _dl_window(k_ref, n)
            vv = _dl_window(v_ref, n)
            s = _dl_scores(qv, kk, n)
            p = jnp.exp(s - jnp.max(lse_ref[0, rows, :], axis=-1, keepdims=True))
            dov = do_ref[0, rows, :]
            w = jnp.max(wt_ref[0, rows, :], axis=-1, keepdims=True)
            d_all = jnp.sum(dov * om_ref[0, rows, :], axis=-1, keepdims=True)
            do_n = (dov * w).astype(BF16)
            dp = lax.dot_general(do_n, vv, (((1,), (1,)), ((), ())), preferred_element_type=F32)
            ds = (p * (dp - w * d_all) * scale).astype(BF16)
            dq_ref[0, rows, :] = jnp.dot(ds, kk, preferred_element_type=F32)
            dkk = lax.dot_general(ds, qv, (((0,), (0,)), ((), ())), preferred_element_type=F32)
            dvv = lax.dot_general(p.astype(BF16), do_n, (((0,), (0,)), ((), ())), preferred_element_type=F32)
            dk_ref[0, prev, :] += dkk[:BLOCK]
            dv_ref[0, prev, :] += dvv[:BLOCK]
            dk_ref[0, rows, :] += dkk[BLOCK:]
            dv_ref[0, rows, :] += dvv[BLOCK:]
            return 0

        lax.fori_loop(0, nb, step, 0)

    spec = pl.BlockSpec((1, l, hd), lambda i: (i, 0, 0))
    out = SDS((g, l, hd), F32)
    return pl.pallas_call(body, name=name, out_shape=(out, out, out), grid=(g,), in_specs=[spec] * 7,
                          out_specs=(spec, spec, spec), compiler_params=_params())(q, k, v, do, o_mix, wt, lse)


def _dl_mix_fwd(outs, lses, name):
    rows, w = outs[0].shape
    t = 256

    def body(o1, o2, o3, l1, l2, l3, ob_ref, of_ref, w1, w2, w3):
        a, b, c = l1[...], l2[...], l3[...]
        m = jnp.maximum(jnp.maximum(a, b), c)
        ea, eb, ec = jnp.exp(a - m), jnp.exp(b - m), jnp.exp(c - m)
        den = ea + eb + ec
        wa, wb, wc = ea / den, eb / den, ec / den
        o = wa * o1[...] + wb * o2[...] + wc * o3[...]
        ob_ref[...] = o.astype(BF16)
        of_ref[...] = o
        w1[...] = wa
        w2[...] = wb
        w3[...] = wc

    row = pl.BlockSpec((t, w), lambda i: (i, 0))
    f = SDS((rows, w), F32)
    return pl.pallas_call(body, name=name, out_shape=(SDS((rows, w), BF16), f, f, f, f), grid=(rows // t,),
                          in_specs=[row] * 6, out_specs=(row,) * 5, compiler_params=_params())(*outs, *lses)


def _x_probs(qh, kh):
    s = lax.dot_general(qh, kh, (((1,), (1,)), ((), ())), preferred_element_type=F32) * (X_HEAD_DIM ** -0.5)
    e = jnp.exp(s - jnp.max(s, axis=-1, keepdims=True))
    return e / jnp.sum(e, axis=-1, keepdims=True)


def _xattn_fwd(q, k, v, name):
    rows, d = q.shape
    t = 512

    def body(q_ref, k_ref, v_ref, o_ref):
        for hh in range(X_HEADS):
            cols = slice(hh * X_HEAD_DIM, (hh + 1) * X_HEAD_DIM)
            p = _x_probs(q_ref[:, cols], k_ref[:, cols])
            o_ref[:, cols] = jnp.dot(p.astype(BF16), v_ref[:, cols], preferred_element_type=F32).astype(BF16)

    row = pl.BlockSpec((t, d), lambda i: (i, 0))
    mem = pl.BlockSpec((N_MEM, d), lambda i: (0, 0))
    return pl.pallas_call(body, name=name, out_shape=SDS((rows, d), BF16), grid=(rows // t,), in_specs=[row, mem, mem],
                          out_specs=row, compiler_params=_params())(q, k, v)


def _xattn_bwd(q, k, v, do, name):
    rows, d = q.shape
    t = 512
    scale = X_HEAD_DIM ** -0.5

    def body(q_ref, k_ref, v_ref, do_ref, dq_ref, dk_ref, dv_ref):
        @pl.when(pl.program_id(0) == 0)
        def _():
            dk_ref[...] = jnp.zeros_like(dk_ref)
            dv_ref[...] = jnp.zeros_like(dv_ref)

        for hh in range(X_HEADS):
            cols = slice(hh * X_HEAD_DIM, (hh + 1) * X_HEAD_DIM)
            qh, kh, vh, doh = q_ref[:, cols], k_ref[:, cols], v_ref[:, cols], do_ref[:, cols]
            p = _x_probs(qh, kh)
            dp = lax.dot_general(doh, vh, (((1,), (1,)), ((), ())), preferred_element_type=F32)
            ds = (p * (dp - jnp.sum(p * dp, axis=-1, keepdims=True)) * scale).astype(BF16)
            dq_ref[:, cols] = jnp.dot(ds, kh, preferred_element_type=F32).astype(BF16)
            dk_ref[:, cols] += lax.dot_general(ds, qh, (((0,), (0,)), ((), ())), preferred_element_type=F32)
            dv_ref[:, cols] += lax.dot_general(p.astype(BF16), doh, (((0,), (0,)), ((), ())), preferred_element_type=F32)

    row = pl.BlockSpec((t, d), lambda i: (i, 0))
    mem = pl.BlockSpec((N_MEM, d), lambda i: (0, 0))
    return pl.pallas_call(
        body, name=name, out_shape=(SDS((rows, d), BF16), SDS((N_MEM, d), F32), SDS((N_MEM, d), F32)), grid=(rows // t,),
        in_specs=[row, mem, mem, row], out_specs=(row, mem, mem), compiler_params=_params(),
    )(q, k, v, do)


CV_TILE = 256
CV_HALO = 32
CV_LEAD = CV_HALO - (CV_KERNEL - 1)


def _shifted(win, off, rows):
    n = win.shape[0]
    return pltpu.roll(win, (n - off) % n, axis=0)[:rows]


def _glu(val, gate):
    return val * jax.nn.sigmoid(gate)


def _ln_parts(c):
    mu = jnp.mean(c, axis=-1, keepdims=True)
    xc = c - mu
    rstd = lax.rsqrt(jnp.mean(xc * xc, axis=-1, keepdims=True) + EPS)
    return xc * rstd, rstd


def _cv_fwd(u, cv_w, cv_b, ln_g, ln_b, name):
    rows = u.shape[0]
    t, w = CV_TILE, CV_WIDTH
    val_col = 3 * SB_WIDTH // w
    ratio = t // CV_HALO

    def body(val_ref, gate_ref, pval_ref, pgate_ref, w_ref, b_ref, g_ref, beta_ref, s_ref, c_ref):
        i = pl.program_id(0)
        hist = jnp.where(i > 0, _glu(pval_ref[...], pgate_ref[...]), 0.0)
        win = jnp.concatenate([hist, _glu(val_ref[...], gate_ref[...])], axis=0)
        acc = jnp.broadcast_to(b_ref[...], (t, w))
        for kk in range(CV_KERNEL):
            acc = acc + _shifted(win, CV_LEAD + kk, t) * w_ref[kk:kk + 1, :]
        c_ref[...] = acc
        n, _ = _ln_parts(acc)
        y = n * g_ref[...] + beta_ref[...]
        s_ref[...] = (y * jax.nn.sigmoid(y)).astype(BF16)

    cur = lambda col: pl.BlockSpec((t, w), lambda i: (i, col))
    prev = lambda col: pl.BlockSpec((CV_HALO, w), lambda i: (jnp.maximum(i * ratio - 1, 0), col))
    vec = pl.BlockSpec((1, w), lambda i: (0, 0))
    return pl.pallas_call(
        body, name=name, out_shape=(SDS((rows, w), BF16), SDS((rows, w), F32)), grid=(rows // t,),
        in_specs=[cur(val_col), cur(val_col + 1), prev(val_col), prev(val_col + 1),
                  pl.BlockSpec((CV_KERNEL, w), lambda i: (0, 0)), vec, vec, vec],
        out_specs=(pl.BlockSpec((t, w), lambda i: (i, 0)),) * 2, compiler_params=_params(),
    )(u, u, u, u, cv_w, cv_b, ln_g, ln_b)


def _cv_bwd(u, c, ds, db_out, cv_w, ln_g, ln_b, name):
    rows = u.shape[0]
    t, w = CV_TILE, CV_WIDTH
    val_col = 3 * SB_WIDTH // w
    ratio = t // CV_HALO
    nt = rows // t

    def conv_out_grad(c_v, ds_v, g_v, beta_v):
        n, rstd = _ln_parts(c_v)
        y = n * g_v + beta_v
        sig = jax.nn.sigmoid(y)
        dy = ds_v * (sig * (1.0 + y * (1.0 - sig)))
        dn = dy * g_v
        dc = rstd * (dn - jnp.mean(dn, axis=-1, keepdims=True) - n * jnp.mean(dn * n, axis=-1, keepdims=True))
        return dc, dy, n

    def body(val_ref, gate_ref, pval_ref, pgate_ref, c_ref, nc_ref, ds_ref, nds_ref, dbo_ref, w_ref, g_ref, beta_ref,
             dvg_ref, dw_ref, db_ref, dg_ref, dbeta_ref, dpwb_ref):
        i = pl.program_id(0)

        @pl.when(i == 0)
        def _():
            for r in (dw_ref, db_ref, dg_ref, dbeta_ref, dpwb_ref):
                r[...] = jnp.zeros_like(r)

        g_v, beta_v = g_ref[...], beta_ref[...]
        dc, dy, n = conv_out_grad(c_ref[...], ds_ref[...], g_v, beta_v)
        dc_next, _, _ = conv_out_grad(nc_ref[...], nds_ref[...], g_v, beta_v)
        dc_next = jnp.where(i < nt - 1, dc_next, 0.0)
        dg_ref[...] += jnp.sum(dy * n, axis=0, keepdims=True)
        dbeta_ref[...] += jnp.sum(dy, axis=0, keepdims=True)
        db_ref[...] += jnp.sum(dc, axis=0, keepdims=True)
        dpwb_ref[...] += jnp.sum(dbo_ref[...], axis=0, keepdims=True)

        val, gate = val_ref[...], gate_ref[...]
        hist = jnp.where(i > 0, _glu(pval_ref[...], pgate_ref[...]), 0.0)
        win = jnp.concatenate([hist, _glu(val, gate)], axis=0)
        dc_ext = jnp.concatenate([dc, dc_next], axis=0)
        dglu = jnp.zeros((t, w), F32)
        for kk in range(CV_KERNEL):
            dw_ref[kk:kk + 1, :] += jnp.sum(dc * _shifted(win, CV_LEAD + kk, t), axis=0, keepdims=True)
            dglu = dglu + _shifted(dc_ext, CV_KERNEL - 1 - kk, t) * w_ref[kk:kk + 1, :]
        sig = jax.nn.sigmoid(gate)
        dvg_ref[:, 0:w] = (dglu * sig).astype(BF16)
        dvg_ref[:, w:2 * w] = (dglu * val * sig * (1.0 - sig)).astype(BF16)

    cur = lambda col: pl.BlockSpec((t, w), lambda i: (i, col))
    prev = lambda col: pl.BlockSpec((CV_HALO, w), lambda i: (jnp.maximum(i * ratio - 1, 0), col))
    nxt = pl.BlockSpec((CV_HALO, w), lambda i: (jnp.minimum((i + 1) * ratio, rows // CV_HALO - 1), 0))
    vec = pl.BlockSpec((1, w), lambda i: (0, 0))
    return pl.pallas_call(
        body, name=name,
        out_shape=(SDS((rows, 2 * w), BF16), SDS((CV_HALO, w), F32), SDS((1, w), F32), SDS((1, w), F32), SDS((1, w), F32),
                   SDS((1, w), F32)),
        grid=(nt,),
        in_specs=[cur(val_col), cur(val_col + 1), prev(val_col), prev(val_col + 1), cur(0), nxt, cur(0), nxt, cur(0),
                  pl.BlockSpec((CV_KERNEL, w), lambda i: (0, 0)), vec, vec],
        out_specs=(pl.BlockSpec((t, 2 * w), lambda i: (i, 0)), pl.BlockSpec((CV_HALO, w), lambda i: (0, 0)), vec, vec, vec, vec),
        compiler_params=_params(),
    )(u, u, u, u, c, c, ds, ds, db_out, cv_w, ln_g, ln_b)


FFN_TILE = 256
FFN_COLS = 256
FFN_HALO = 8
FFN_KERNEL = 3
N_FF_BLOCKS = D_FF // FFN_COLS


def _conv3(prev8, cur, w_ref, b_ref, first):
    t = cur.shape[0]
    win = jnp.concatenate([jnp.where(first, 0.0, prev8), cur], axis=0)
    return (b_ref[...] + _shifted(win, FFN_HALO - 2, t) * w_ref[0:1, :] + _shifted(win, FFN_HALO - 1, t) * w_ref[1:2, :]
            + cur * w_ref[2:3, :])


def _gelu_gate(gate, val):
    return jax.nn.gelu(gate, approximate=True) * val


def _ffn_specs(t, ratio):
    cur = lambda off: pl.BlockSpec((t, FFN_COLS), lambda j, i: (i, j % N_FF_BLOCKS + off))
    prev = lambda off: pl.BlockSpec((FFN_HALO, FFN_COLS), lambda j, i: (jnp.maximum(i * ratio - 1, 0), j % N_FF_BLOCKS + off))
    wsp = lambda off: pl.BlockSpec((FFN_KERNEL, FFN_COLS), lambda j, i: (0, j % N_FF_BLOCKS + off))
    bsp = lambda off: pl.BlockSpec((1, FFN_COLS), lambda j, i: (0, j % N_FF_BLOCKS + off))
    return cur, prev, wsp, bsp


def _ffn_act_fwd(up, conv_w, conv_b, name):
    rows = up.shape[0]
    t = FFN_TILE
    cur, prev, wsp, bsp = _ffn_specs(t, t // FFN_HALO)

    def body(g_ref, v_ref, pg_ref, pv_ref, wg_ref, wv_ref, bg_ref, bv_ref, o_ref):
        first = pl.program_id(1) == 0
        gate = _conv3(pg_ref[...], g_ref[...], wg_ref, bg_ref, first)
        val = _conv3(pv_ref[...], v_ref[...], wv_ref, bv_ref, first)
        o_ref[...] = _gelu_gate(gate, val).astype(BF16)

    nb = N_FF_BLOCKS
    return pl.pallas_call(
        body, name=name, out_shape=SDS((rows, D_FF), BF16), grid=(nb, rows // t),
        in_specs=[cur(0), cur(nb), prev(0), prev(nb), wsp(0), wsp(nb), bsp(0), bsp(nb)],
        out_specs=pl.BlockSpec((t, FFN_COLS), lambda j, i: (i, j)), compiler_params=_params(),
    )(up, up, up, up, conv_w, conv_w, conv_b, conv_b)


def _ffn_conv_grad(up, dact, conv_w, conv_b, name):
    rows = up.shape[0]
    t = FFN_TILE
    nb = N_FF_BLOCKS
    cur, prev, wsp, bsp = _ffn_specs(t, t // FFN_HALO)

    def body(g_ref, v_ref, pg_ref, pv_ref, da_ref, wg_ref, wv_ref, bg_ref, bv_ref, dc_ref, dw_ref, db_ref):
        j, i = pl.program_id(0), pl.program_id(1)
        first = i == 0
        gate_in, val_in = g_ref[...], v_ref[...]
        pg, pv = pg_ref[...], pv_ref[...]
        gate = _conv3(pg, gate_in, wg_ref, bg_ref, first)
        val = _conv3(pv, val_in, wv_ref, bv_ref, first)
        _, vjp = jax.vjp(_gelu_gate, gate, val)
        dgate, dval = vjp(da_ref[...])
        is_gate = j < nb
        dc = jnp.where(is_gate, dgate, dval)
        dc_ref[...] = dc
        win = jnp.concatenate([jnp.where(first, 0.0, jnp.where(is_gate, pg, pv)), jnp.where(is_gate, gate_in, val_in)], axis=0)

        @pl.when(first)
        def _():
            dw_ref[...] = jnp.zeros_like(dw_ref)
            db_ref[...] = jnp.zeros_like(db_ref)

        for kk in range(FFN_KERNEL):
            dw_ref[kk:kk + 1, :] += jnp.sum(dc * _shifted(win, FFN_HALO - 2 + kk, t), axis=0, keepdims=True)
        db_ref[...] += jnp.sum(dc, axis=0, keepdims=True)

    return pl.pallas_call(
        body, name=name,
        out_shape=(SDS((rows, 2 * D_FF), F32), SDS((FFN_KERNEL, 2 * D_FF), F32), SDS((1, 2 * D_FF), F32)),
        grid=(2 * nb, rows // t),
        in_specs=[cur(0), cur(nb), prev(0), prev(nb), pl.BlockSpec((t, FFN_COLS), lambda j, i: (i, j % nb)),
                  wsp(0), wsp(nb), bsp(0), bsp(nb)],
        out_specs=(pl.BlockSpec((t, FFN_COLS), lambda j, i: (i, j)), pl.BlockSpec((FFN_KERNEL, FFN_COLS), lambda j, i: (0, j)),
                   pl.BlockSpec((1, FFN_COLS), lambda j, i: (0, j))),
        compiler_params=_params(),
    )(up, up, up, up, dact, conv_w, conv_w, conv_b, conv_b)


def _ffn_conv_bwd(dc, conv_w, name):
    rows, cols = dc.shape
    t = FFN_TILE
    ratio = t // FFN_HALO
    nt = rows // t

    def body(dc_ref, nx_ref, w_ref, o_ref):
        i = pl.program_id(1)
        cur = dc_ref[...]
        ext = jnp.concatenate([cur, jnp.where(i < nt - 1, nx_ref[...], 0.0)], axis=0)
        o_ref[...] = (cur * w_ref[2:3, :] + _shifted(ext, 1, t) * w_ref[1:2, :] + _shifted(ext, 2, t) * w_ref[0:1, :]).astype(BF16)

    return pl.pallas_call(
        body, name=name, out_shape=SDS((rows, cols), BF16), grid=(cols // FFN_COLS, nt),
        in_specs=[pl.BlockSpec((t, FFN_COLS), lambda j, i: (i, j)),
                  pl.BlockSpec((FFN_HALO, FFN_COLS), lambda j, i: (jnp.minimum((i + 1) * ratio, rows // FFN_HALO - 1), j)),
                  pl.BlockSpec((FFN_KERNEL, FFN_COLS), lambda j, i: (0, j))],
        out_specs=pl.BlockSpec((t, FFN_COLS), lambda j, i: (i, j)), compiler_params=_params(),
    )(dc, dc, conv_w)


def _adamw(parts, w, m, v, name):
    rows, cols = w.shape
    t = _pick(rows, (512, 256, 128)) if rows > 512 else rows

    def body(p_ref, w_ref, m_ref, v_ref, g_ref, d_ref, nm_ref, nv_ref):
        g = p_ref[0].astype(F32)
        for s in range(1, N_DEV):
            g = g + p_ref[s].astype(F32)
        nm = ADAM_B1 * m_ref[...] + (1.0 - ADAM_B1) * g
        nv = ADAM_B2 * v_ref[...] + (1.0 - ADAM_B2) * jnp.square(g)
        m_hat = nm / (1.0 - ADAM_B1 ** ADAM_STEP)
        v_hat = nv / (1.0 - ADAM_B2 ** ADAM_STEP)
        g_ref[...] = g
        d_ref[...] = -ADAM_LR * (m_hat / (jnp.sqrt(v_hat) + ADAM_EPS) + ADAM_WD * w_ref[...])
        nm_ref[...] = nm
        nv_ref[...] = nv

    row = pl.BlockSpec((t, cols), lambda i: (i, 0))
    out = SDS((rows, cols), F32)
    return pl.pallas_call(
        body, name=name, out_shape=(out,) * 4, grid=(rows // t,),
        in_specs=[pl.BlockSpec((N_DEV, t, cols), lambda i: (0, i, 0)), row, row, row], out_specs=(row,) * 4,
        compiler_params=_params(),
    )(parts, w, m, v)


def _all_gather(x, name):
    def body(x_ref, out_ref, send_sems, recv_sems, local_sem):
        x_, y_, c_ = lax.axis_index("x"), lax.axis_index("y"), lax.axis_index("c")
        me, sibling = (x_, y_, c_), (x_, y_, 1 - c_)
        chips = [(1 - x_, y_), (x_, 1 - y_), (1 - x_, 1 - y_)]

        def slot(px, py, pc):
            return out_ref.at[4 * px + 2 * py + pc]

        def copy(kk, block, to, src=None):
            return pltpu.make_async_remote_copy(
                src_ref=slot(*block) if src is None else src, dst_ref=slot(*block),
                send_sem=send_sems.at[kk], recv_sem=recv_sems.at[kk], device_id=to, device_id_type=MESH)

        mine = pltpu.make_async_copy(x_ref, slot(*me), local_sem)
        mine.start()
        first = [copy(0, me, sibling, src=x_ref)]
        first += [copy(1 + j, me, (*chip, c_), src=x_ref) for j, chip in enumerate(chips)]
        for cp in first:
            cp.start()
        passed = [copy(4 + j, (*chip, c_), sibling) for j, chip in enumerate(chips)]
        for j, chip in enumerate(chips):
            copy(1 + j, (*chip, c_), me).wait_recv()
            passed[j].start()
        copy(0, sibling, me).wait_recv()
        for j, chip in enumerate(chips):
            copy(4 + j, (*chip, 1 - c_), me).wait_recv()
        for cp in first + passed:
            cp.wait_send()
        mine.wait()

    return pl.pallas_call(
        body, name=name, out_shape=SDS((N_DEV,) + x.shape, x.dtype),
        in_specs=[pl.BlockSpec(memory_space=pl.ANY)], out_specs=pl.BlockSpec(memory_space=pl.ANY),
        scratch_shapes=[pltpu.SemaphoreType.DMA((7,)), pltpu.SemaphoreType.DMA((7,)), pltpu.SemaphoreType.DMA],
        compiler_params=pltpu.CompilerParams(has_side_effects=True),
    )(x)


def _all_to_all(x, name):
    def body(x_ref, out_ref, send_sems, recv_sems, local_sem):
        x_, y_, c_ = lax.axis_index("x"), lax.axis_index("y"), lax.axis_index("c")
        me = 4 * x_ + 2 * y_ + c_
        mine = pltpu.make_async_copy(x_ref.at[me], out_ref.at[me], local_sem)
        mine.start()
        copies = []
        for r in range(1, N_DEV):
            px = 1 - x_ if r & 4 else x_
            py = 1 - y_ if r & 2 else y_
            pc = 1 - c_ if r & 1 else c_
            cp = pltpu.make_async_remote_copy(
                src_ref=x_ref.at[4 * px + 2 * py + pc], dst_ref=out_ref.at[me],
                send_sem=send_sems.at[r - 1], recv_sem=recv_sems.at[r - 1], device_id=(px, py, pc), device_id_type=MESH)
            cp.start()
            copies.append(cp)
        for cp in copies:
            cp.wait_recv()
        for cp in copies:
            cp.wait_send()
        mine.wait()

    return pl.pallas_call(
        body, name=name, out_shape=SDS(x.shape, x.dtype),
        in_specs=[pl.BlockSpec(memory_space=pl.ANY)], out_specs=pl.BlockSpec(memory_space=pl.ANY),
        scratch_shapes=[pltpu.SemaphoreType.DMA((7,)), pltpu.SemaphoreType.DMA((7,)), pltpu.SemaphoreType.DMA],
        compiler_params=pltpu.CompilerParams(has_side_effects=True),
    )(x)


BIG = ("w_in", "cv_pw_w", "w_out", "x_wq", "x_wk", "x_wv", "x_wo", "ffn_w_up", "ffn_w_down")
COL_SHARDED = ("w_in", "ffn_w_up", "cv_w", "ffn_conv_w")
SMALL_SHARDED = ("cv_w", "ffn_conv_w")
REPLICATED = ("mix_norm_pre", "cv_b", "cv_ln_g", "cv_ln_b", "cv_pw_b", "mix_norm_post", "x_norm_pre", "mem_norm",
              "x_norm_post", "ffn_norm_pre", "ffn_conv_b", "ffn_norm_post")
WEIGHTS = ("mix_norm_pre", "w_in", "cv_w", "cv_b", "cv_ln_g", "cv_ln_b", "cv_pw_w", "cv_pw_b", "w_out", "mix_norm_post",
           "x_norm_pre", "mem_norm", "x_wq", "x_wk", "x_wv", "x_wo", "x_norm_post", "ffn_norm_pre", "ffn_w_up",
           "ffn_conv_w", "ffn_conv_b", "ffn_w_down", "ffn_norm_post")
PAYLOAD_COLS = 1024


PAYLOAD_ROW_TILE = 16


def _pad_rows(flat, cols):
    n = flat.shape[-1]
    rows = -(-n // (cols * PAYLOAD_ROW_TILE)) * PAYLOAD_ROW_TILE
    pad = rows * cols - n
    if pad:
        flat = jnp.concatenate([flat, jnp.zeros(flat.shape[:-1] + (pad,), flat.dtype)], axis=-1)
    return flat.reshape(flat.shape[:-1] + (rows, cols))


def _unshard(name, parts):
    n, depth, r, c = parts.shape
    if name in COL_SHARDED:
        return parts.transpose(1, 2, 0, 3).reshape(depth, r, n * c)
    return parts.transpose(1, 0, 2, 3).reshape(depth, n * r, c)


def _to_shards(name, full):
    depth, r, c = full.shape
    if name in COL_SHARDED:
        return full.reshape(depth, r, N_DEV, c // N_DEV).transpose(2, 0, 1, 3).reshape(N_DEV, -1)
    return full.reshape(depth, N_DEV, r // N_DEV, c).transpose(1, 0, 2, 3).reshape(N_DEV, -1)


def _to_sub(x, dil):
    s_len, w = x.shape
    h = w // HEAD_DIM
    return x.reshape(s_len // dil, dil, h, HEAD_DIM).transpose(1, 2, 0, 3).reshape(dil * h, s_len // dil, HEAD_DIM)


def _from_sub(x, dil):
    g, l, hd = x.shape
    h = g // dil
    return x.reshape(dil, h, l, hd).transpose(2, 0, 1, 3).reshape(l * dil, h * hd)


def _heads_major(x, h):
    return x.reshape(x.shape[0], h, HEAD_DIM).transpose(1, 0, 2)


def _tokens_major(x):
    return x.transpose(1, 0, 2).reshape(x.shape[1], -1)


def _layer_fwd(l, h, hn, p, mem, cos, sin, g_next):
    sv = {"h0": h, "hn0": hn}
    u = _mm(hn, p["w_in"], "nn", F32, f"l{l}_in_proj")
    sv["u"] = u
    sb = _heads_major(u[:, :3 * SB_WIDTH].astype(BF16), 3 * SB_HEADS)
    sb_q, sb_k, sb_v = sb[:SB_HEADS], sb[SB_HEADS:2 * SB_HEADS], sb[2 * SB_HEADS:]
    a_out, sb_tot = _sb_fwd(sb_q, sb_k, sb_v, f"l{l}_sb_fwd")
    sv.update(sb_q=sb_q, sb_k=sb_k, sb_v=sb_v, sb_tot=sb_tot)

    cv_s, cv_c = _cv_fwd(u, p["cv_w"], p["cv_b"], p["cv_ln_g"], p["cv_ln_b"], f"l{l}_cv_fwd")
    b_out = _mm(cv_s, p["cv_pw_w"], "nn", BF16, f"l{l}_cv_pw", bias=p["cv_pw_b"])
    sv.update(cv_s=cv_s, cv_c=cv_c)

    qk = _rope_fwd(u, cos, sin, f"l{l}_rope_fwd")
    dl_v = u[:, IN_WIDTH - DL_WIDTH:].astype(BF16)
    outs, lses, subs = [], [], []
    for b, (_, dil) in enumerate(DL_PATTERN):
        qs, ks, vs = _to_sub(qk[:, :DL_WIDTH], dil), _to_sub(qk[:, DL_WIDTH:], dil), _to_sub(dl_v, dil)
        o, lse = _dl_fwd(qs, ks, vs, f"l{l}_dl{b}_fwd")
        subs.append((qs, ks, vs, lse))
        outs.append(_from_sub(o, dil))
        lses.append(_from_sub(lse, dil))
    c_out, c_out_f32, w1, w2, w3 = _dl_mix_fwd(outs, lses, f"l{l}_dl_mix")
    sv.update(dl_subs=subs, dl_o=c_out_f32, dl_w=(w1, w2, w3))

    mix = jnp.concatenate([_tokens_major(a_out), b_out, c_out], axis=-1)
    y = _mm(mix, p["w_out"], "nn", F32, f"l{l}_out_proj")
    h1, hn1 = _res_norm_fwd(h, y, p["mix_norm_post"], p["x_norm_pre"], f"l{l}_mix_post")
    sv.update(mix=mix, y_mix=y, h1=h1, hn1=hn1)

    xq = _mm(hn1, p["x_wq"], "nn", BF16, f"l{l}_xq")
    memn = _rms_fwd(mem, p["mem_norm"], f"l{l}_mem_norm")
    xk = _mm(memn, p["x_wk"], "nn", BF16, f"l{l}_xk")
    xv = _mm(memn, p["x_wv"], "nn", BF16, f"l{l}_xv")
    xo = _xattn_fwd(xq, xk, xv, f"l{l}_xattn_fwd")
    y = _mm(xo, p["x_wo"], "nn", F32, f"l{l}_xo_proj")
    h2, hn2 = _res_norm_fwd(h1, y, p["x_norm_post"], p["ffn_norm_pre"], f"l{l}_x_post")
    sv.update(xq=xq, xk=xk, xv=xv, xo=xo, memn=memn, y_x=y, h2=h2, hn2=hn2)

    up = _mm(hn2, p["ffn_w_up"], "nn", F32, f"l{l}_ffn_up")
    act = _ffn_act_fwd(up, p["ffn_conv_w"], p["ffn_conv_b"], f"l{l}_ffn_act")
    y = _mm(act, p["ffn_w_down"], "nn", F32, f"l{l}_ffn_down")
    h3, hn3 = _res_norm_fwd(h2, y, p["ffn_norm_post"], g_next, f"l{l}_ffn_post")
    sv.update(up=up, act=act, y_ffn=y)
    return h3, hn3, sv


def _layer_bwd(l, dh, dy, p, sv, mem, cos, sin, first_layer):
    gr = {}
    dact = _mm(dy, p["ffn_w_down"], "nt", F32, f"l{l}_d_act")
    gr["ffn_w_down"] = _mm(sv["act"], dy, "tn", F32, f"l{l}_dw_down")
    dc, gr["ffn_conv_w"], gr["ffn_conv_b"] = _ffn_conv_grad(sv["up"], dact, p["ffn_conv_w"], p["ffn_conv_b"], f"l{l}_ffn_dconv")
    dup = _ffn_conv_bwd(dc, p["ffn_conv_w"], f"l{l}_ffn_dup")
    dhn = _mm(dup, p["ffn_w_up"], "nt", F32, f"l{l}_d_hn2")
    gr["ffn_w_up"] = _mm(sv["hn2"], dup, "tn", F32, f"l{l}_dw_up")
    dh, dy, gr["ffn_norm_pre"], gr["x_norm_post"] = _norm_bwd(
        dh, (sv["h2"], p["ffn_norm_pre"], dhn), (sv["y_x"], p["x_norm_post"]), f"l{l}_x_post_bwd")

    do = _mm(dy, p["x_wo"], "nt", BF16, f"l{l}_d_xo")
    gr["x_wo"] = _mm(sv["xo"], dy, "tn", F32, f"l{l}_dw_xo")
    dq, dk, dv = _xattn_bwd(sv["xq"], sv["xk"], sv["xv"], do, f"l{l}_xattn_bwd")
    dhn = _mm(dq, p["x_wq"], "nt", F32, f"l{l}_d_hn1")
    gr["x_wq"] = _mm(sv["hn1"], dq, "tn", F32, f"l{l}_dw_xq")
    gr["x_wk"] = _mm(sv["memn"], dk, "tn", F32, f"l{l}_dw_xk")
    gr["x_wv"] = _mm(sv["memn"], dv, "tn", F32, f"l{l}_dw_xv")
    dmemn = _mm(dk, p["x_wk"], "nt", F32, f"l{l}_d_memn_k") + _mm(dv, p["x_wv"], "nt", F32, f"l{l}_d_memn_v")
    gr["mem_norm"] = _rms_gain_grad(mem, p["mem_norm"], dmemn, f"l{l}_mem_norm_bwd")
    dh, dy, gr["x_norm_pre"], gr["mix_norm_post"] = _norm_bwd(
        dh, (sv["h1"], p["x_norm_pre"], dhn), (sv["y_mix"], p["mix_norm_post"]), f"l{l}_mix_post_bwd")

    dmix = _mm(dy, p["w_out"], "nt", F32, f"l{l}_d_mix")
    gr["w_out"] = _mm(sv["mix"], dy, "tn", F32, f"l{l}_dw_out")
    do_a = _heads_major(dmix[:, :SB_WIDTH].astype(BF16), SB_HEADS)
    dq, dk, dv = _sb_bwd(sv["sb_q"], sv["sb_k"], sv["sb_v"], do_a, sv["sb_tot"], f"l{l}_sb_bwd")
    du_sb = _tokens_major(jnp.concatenate([dq, dk, dv], axis=0))

    db_out = dmix[:, SB_WIDTH:SB_WIDTH + CV_WIDTH]
    ds = _mm(db_out, p["cv_pw_w"], "nt", F32, f"l{l}_d_cv_s")
    gr["cv_pw_w"] = _mm(sv["cv_s"], db_out, "tn", F32, f"l{l}_dw_cv_pw")
    du_cv, dcvw, gr["cv_b"], gr["cv_ln_g"], gr["cv_ln_b"], gr["cv_pw_b"] = _cv_bwd(
        sv["u"], sv["cv_c"], ds, db_out, p["cv_w"], p["cv_ln_g"], p["cv_ln_b"], f"l{l}_cv_bwd")
    gr["cv_w"] = dcvw[:CV_KERNEL]

    dc_out = dmix[:, SB_WIDTH + CV_WIDTH:]
    dqs, dks, dvs = [], [], []
    for b, (_, dil) in enumerate(DL_PATTERN):
        qs, ks, vs, lse = sv["dl_subs"][b]
        dq, dk, dv = _dl_bwd(qs, ks, vs, _to_sub(dc_out, dil), _to_sub(sv["dl_o"], dil), _to_sub(sv["dl_w"][b], dil), lse,
                             f"l{l}_dl{b}_bwd")
        dqs.append(_from_sub(dq, dil))
        dks.append(_from_sub(dk, dil))
        dvs.append(_from_sub(dv, dil))
    du_dl = _rope_bwd(dqs, dks, dvs, cos, sin, f"l{l}_rope_bwd")

    du = jnp.concatenate([du_sb, du_cv, du_dl], axis=-1)
    dhn = _mm(du, p["w_in"], "nt", F32, f"l{l}_d_hn0")
    gr["w_in"] = _mm(sv["hn0"], du, "tn", F32, f"l{l}_dw_in")
    post = None if first_layer else (sv["y_prev"], p["prev_ffn_norm_post"])
    dh, dy, gr["mix_norm_pre"], dg_prev = _norm_bwd(dh, (sv["h0"], p["mix_norm_pre"], dhn), post, f"l{l}_in_bwd")
    return dh, dy, gr, dg_prev


def kernel(x, mem, positions, mix_norm_pre, w_in, cv_w, cv_b, cv_ln_g, cv_ln_b, cv_pw_w, cv_pw_b, w_out, mix_norm_post, x_norm_pre, mem_norm, x_wq, x_wk, x_wv, x_wo, x_norm_post, ffn_norm_pre, ffn_w_up, ffn_conv_w, ffn_conv_b, ffn_w_down, ffn_norm_post, loss_target, m_mix_norm_pre, m_w_in, m_cv_w, m_cv_b, m_cv_ln_g, m_cv_ln_b, m_cv_pw_w, m_cv_pw_b, m_w_out, m_mix_norm_post, m_x_norm_pre, m_mem_norm, m_x_wq, m_x_wk, m_x_wv, m_x_wo, m_x_norm_post, m_ffn_norm_pre, m_ffn_w_up, m_ffn_conv_w, m_ffn_conv_b, m_ffn_w_down, m_ffn_norm_post, v_mix_norm_pre, v_w_in, v_cv_w, v_cv_b, v_cv_ln_g, v_cv_ln_b, v_cv_pw_w, v_cv_pw_b, v_w_out, v_mix_norm_post, v_x_norm_pre, v_mem_norm, v_x_wq, v_x_wk, v_x_wv, v_x_wo, v_x_norm_post, v_ffn_norm_pre, v_ffn_w_up, v_ffn_conv_w, v_ffn_conv_b, v_ffn_w_down, v_ffn_norm_post):
    args = locals()
    wts = {n: args[n] for n in WEIGHTS}
    mom = {n: args["m_" + n] for n in WEIGHTS}
    var = {n: args["v_" + n] for n in WEIGHTS}

    payload = _pad_rows(jnp.concatenate([wts[n].astype(BF16).reshape(-1) for n in BIG]), PAYLOAD_COLS)
    gathered = _all_gather(payload, "weights_all_gather").reshape(N_DEV, -1)
    small_payload = _pad_rows(jnp.concatenate([wts[n].reshape(-1) for n in SMALL_SHARDED]), PAYLOAD_COLS)
    small = _all_gather(small_payload, "small_weights_all_gather").reshape(N_DEV, -1)
    full = {}
    for names, src in ((BIG, gathered), (SMALL_SHARDED, small)):
        off = 0
        for n in names:
            size = wts[n].size
            full[n] = _unshard(n, src[:, off:off + size].reshape((N_DEV,) + wts[n].shape))
            off += size
    for n in REPLICATED:
        full[n] = wts[n]
    loss_part, grad_x, grads = _local_step(x[0], mem[0], positions[0], loss_target[0], full)
    loss = lax.psum(loss_part, ("x", "y", "c"))

    big_rows = jnp.concatenate([_to_shards(n, jnp.stack(grads[n])).astype(BF16) for n in BIG], axis=1)
    small_rows = jnp.concatenate([_to_shards(n, jnp.stack(grads[n])) for n in SMALL_SHARDED], axis=1)
    rep_flat = jnp.concatenate([jnp.stack([g.reshape(-1) for g in grads[n]]).reshape(-1) for n in REPLICATED])
    rep_rows = jnp.broadcast_to(rep_flat[None], (N_DEV, rep_flat.shape[0]))
    f32_rows = _pad_rows(jnp.concatenate([small_rows, rep_rows], axis=1), PAYLOAD_COLS)
    received = _all_to_all(_pad_rows(big_rows, PAYLOAD_COLS), "grads_all_to_all").reshape(N_DEV, -1)
    small_parts = _all_to_all(f32_rows, "small_grads_all_to_all")

    res = {}
    off = 0
    for n in BIG:
        shape = wts[n].shape
        size = wts[n].size
        two_d = (shape[0] * shape[1], shape[2])
        parts = received[:, off:off + size].reshape((N_DEV,) + two_d)
        outs = _adamw(parts, wts[n].reshape(two_d), mom[n].reshape(two_d), var[n].reshape(two_d), f"adamw_{n}")
        res[n] = [o.reshape(shape) for o in outs]
        off += size
    small_names = SMALL_SHARDED + REPLICATED
    flat_w = _pad_rows(jnp.concatenate([wts[n].reshape(-1) for n in small_names]), PAYLOAD_COLS)
    flat_m = _pad_rows(jnp.concatenate([mom[n].reshape(-1) for n in small_names]), PAYLOAD_COLS)
    flat_v = _pad_rows(jnp.concatenate([var[n].reshape(-1) for n in small_names]), PAYLOAD_COLS)
    outs = _adamw(small_parts, flat_w, flat_m, flat_v, "adamw_small")
    outs = [o.reshape(-1) for o in outs]
    off = 0
    for n in small_names:
        size = wts[n].size
        res[n] = [o[off:off + size].reshape(wts[n].shape) for o in outs]
        off += size

    result = [loss, grad_x[None]]
    for kind in range(4):
        result += [res[n][kind] for n in WEIGHTS]
    return tuple(result)


def _local_step(x2, mem2, positions, target, full):
    def layer_params(l):
        p = {n: full[n][l] for n in BIG + SMALL_SHARDED}
        p.update({n: full[n][l][None, :] for n in REPLICATED})
        return p

    pos = positions.astype(F32)
    half = HEAD_DIM // 2
    inv_freq = ROPE_THETA ** (-jnp.arange(half, dtype=F32) / half)
    ang = pos[:, None] * inv_freq
    cos = jnp.tile(jnp.cos(ang), (1, LANES // half))
    sin = jnp.tile(jnp.sin(ang), (1, LANES // half))

    params = [layer_params(l) for l in range(DEPTH)]
    h = x2
    hn = _rms_fwd(h, params[0]["mix_norm_pre"], "l0_in_norm")
    saved = []
    for l in range(DEPTH):
        g_next = params[l + 1]["mix_norm_pre"] if l + 1 < DEPTH else None
        h, hn, sv = _layer_fwd(l, h, hn, params[l], mem2, cos, sin, g_next)
        saved.append(sv)
    loss_part, dh = _loss_fwd(h, target, "loss")

    grads = {n: [None] * DEPTH for n in WEIGHTS}
    dh, dy, _, dg = _norm_bwd(dh, None, (saved[-1]["y_ffn"], params[-1]["ffn_norm_post"]), "last_post_bwd")
    grads["ffn_norm_post"][DEPTH - 1] = dg
    for l in reversed(range(DEPTH)):
        p = dict(params[l])
        sv = dict(saved[l])
        if l > 0:
            p["prev_ffn_norm_post"] = params[l - 1]["ffn_norm_post"]
            sv["y_prev"] = saved[l - 1]["y_ffn"]
        dh, dy, gr, dg_prev = _layer_bwd(l, dh, dy, p, sv, mem2, cos, sin, first_layer=(l == 0))
        for n, g in gr.items():
            grads[n][l] = g
        if l > 0:
            grads["ffn_norm_post"][l - 1] = dg_prev
    return loss_part[0, 0], dh, grads
```

```python
import functools
import math

import jax
import jax.numpy as jnp
from jax import lax
from jax.experimental import pallas as pl
from jax.experimental.pallas import tpu as pltpu

F32, BF16 = jnp.float32, jnp.bfloat16
SDS = jax.ShapeDtypeStruct

D_MODEL = 1024
SEQ = 4096
DEPTH = 2
HEAD_DIM = 64
SB_HEADS = 4
SB_WIDTH = 256
CV_WIDTH = 256
CV_KERNEL = 31
DL_HEADS = 8
DL_WIDTH = 512
IN_WIDTH = 2816
DL_PATTERN = ((128, 1), (512, 4), (2048, 16))
BLOCK = 128
ROPE_THETA = 10000.0
N_MEM = 256
X_HEADS = 4
X_HEAD_DIM = 256
D_FF = 2816
EPS = 1e-6
N_DEV = 8
LANES = 128

ADAM_LR = 0.001
ADAM_B1 = 0.9
ADAM_B2 = 0.999
ADAM_EPS = 1e-08
ADAM_WD = 0.01
ADAM_STEP = 10

VMEM_LIMIT_BYTES = 56 * 1024 * 1024
MESH = pl.DeviceIdType.MESH
NEG = -1e30


def _params(**kw):
    return pltpu.CompilerParams(vmem_limit_bytes=VMEM_LIMIT_BYTES, **kw)


def _pick(n, cands):
    for c in cands:
        if n % c == 0:
            return c
    return n


def _mm(a, b, mode, out_dtype, name, bias=None):
    if mode == "nn":
        (m, k), (k2, n) = a.shape, b.shape
    elif mode == "nt":
        (m, k), (n, k2) = a.shape, b.shape
    else:
        (k, m), (k2, n) = a.shape, b.shape
    assert k == k2, (a.shape, b.shape, mode)
    tm = _pick(m, (1024, 1408, 512, 256, 128))
    tn = _pick(n, (512, 256, 128))
    tk = k if k <= 2048 else _pick(k, (2048, 1408, 1024, 512))
    nk = k // tk
    dims = {"nn": ((1,), (0,)), "nt": ((1,), (1,)), "tn": ((0,), (0,))}[mode]

    def body(*refs):
        if bias is None:
            a_ref, b_ref, o_ref, acc_ref = refs
            bias_ref = None
        else:
            a_ref, b_ref, bias_ref, o_ref, acc_ref = refs
        p = lax.dot_general(a_ref[...].astype(BF16), b_ref[...].astype(BF16), (dims, ((), ())),
                            preferred_element_type=F32)

        def finish(v):
            if bias_ref is not None:
                v = v + bias_ref[...]
            o_ref[...] = v.astype(out_dtype)

        if nk == 1:
            finish(p)
        else:
            kk = pl.program_id(2)

            @pl.when(kk == 0)
            def _():
                acc_ref[...] = p

            @pl.when(kk > 0)
            def _():
                acc_ref[...] += p

            @pl.when(kk == nk - 1)
            def _():
                finish(acc_ref[...])

    a_spec = pl.BlockSpec((tk, tm), lambda i, j, kk: (kk, i)) if mode == "tn" else pl.BlockSpec((tm, tk), lambda i, j, kk: (i, kk))
    b_spec = pl.BlockSpec((tn, tk), lambda i, j, kk: (j, kk)) if mode == "nt" else pl.BlockSpec((tk, tn), lambda i, j, kk: (kk, j))
    in_specs = [a_spec, b_spec]
    args = [a, b]
    if bias is not None:
        in_specs.append(pl.BlockSpec((1, tn), lambda i, j, kk: (0, j)))
        args.append(bias)
    return pl.pallas_call(
        body, name=name, out_shape=SDS((m, n), out_dtype), grid=(m // tm, n // tn, nk),
        in_specs=in_specs, out_specs=pl.BlockSpec((tm, tn), lambda i, j, kk: (i, j)),
        scratch_shapes=[pltpu.VMEM((tm, tn), F32)], compiler_params=_params(),
    )(*args)


def _rms(x, g):
    r = lax.rsqrt(jnp.mean(x * x, axis=-1, keepdims=True) + EPS)
    return x * r * g


def _rms_bwd(x, g, dy):
    r = lax.rsqrt(jnp.mean(x * x, axis=-1, keepdims=True) + EPS)
    xh = x * r
    dyg = dy * g
    dx = r * (dyg - xh * jnp.mean(dyg * xh, axis=-1, keepdims=True))
    return dx, dy * xh


def _rms_fwd(x, g, name):
    rows, d = x.shape
    t = min(rows, 512)

    def body(x_ref, g_ref, o_ref):
        o_ref[...] = _rms(x_ref[...], g_ref[...]).astype(BF16)

    return pl.pallas_call(
        body, name=name, out_shape=SDS((rows, d), BF16), grid=(rows // t,),
        in_specs=[pl.BlockSpec((t, d), lambda i: (i, 0)), pl.BlockSpec((1, d), lambda i: (0, 0))],
        out_specs=pl.BlockSpec((t, d), lambda i: (i, 0)), compiler_params=_params(),
    )(x, g)


def _res_norm_fwd(h, y, g_post, g_next, name):
    rows, d = h.shape
    t = 512
    has_next = g_next is not None

    def body(*refs):
        if has_next:
            h_ref, y_ref, gp_ref, gn_ref, h1_ref, hn_ref = refs
        else:
            h_ref, y_ref, gp_ref, h1_ref = refs
        h1 = h_ref[...] + _rms(y_ref[...], gp_ref[...])
        h1_ref[...] = h1
        if has_next:
            hn_ref[...] = _rms(h1, gn_ref[...]).astype(BF16)

    row = pl.BlockSpec((t, d), lambda i: (i, 0))
    vec = pl.BlockSpec((1, d), lambda i: (0, 0))
    in_specs = [row, row, vec] + ([vec] if has_next else [])
    args = [h, y, g_post] + ([g_next] if has_next else [])
    out_shape = [SDS((rows, d), F32)] + ([SDS((rows, d), BF16)] if has_next else [])
    out_specs = [row] + ([row] if has_next else [])
    res = pl.pallas_call(body, name=name, out_shape=out_shape, grid=(rows // t,), in_specs=in_specs,
                         out_specs=out_specs, compiler_params=_params())(*args)
    return (res[0], res[1]) if has_next else (res[0], None)


def _norm_bwd(dh, pre, post, name):
    rows, d = dh.shape
    t = 512
    has_pre, has_post = pre is not None, post is not None
    if has_pre:
        dhns = pre[2] if isinstance(pre[2], tuple) else (pre[2],)
        pre = (pre[0], pre[1]) + dhns

    def body(*refs):
        refs = list(refs)
        dh_ref = refs.pop(0)
        if has_pre:
            h_ref, gpre_ref = refs.pop(0), refs.pop(0)
            dhn_refs = [refs.pop(0) for _ in dhns]
        if has_post:
            y_ref, gpost_ref = refs.pop(0), refs.pop(0)
        dht_ref = refs.pop(0)
        if has_post:
            dy_ref = refs.pop(0)
        if has_pre:
            dgpre_ref = refs.pop(0)
        if has_post:
            dgpost_ref = refs.pop(0)
        i = pl.program_id(0)
        dht = dh_ref[...]
        if has_pre:
            dhn = dhn_refs[0][...]
            for r in dhn_refs[1:]:
                dhn = dhn + r[...]
            dx, dgr = _rms_bwd(h_ref[...], gpre_ref[...], dhn)
            dht = dht + dx

            @pl.when(i == 0)
            def _():
                dgpre_ref[...] = jnp.zeros_like(dgpre_ref)

            dgpre_ref[...] += jnp.sum(dgr, axis=0, keepdims=True)
        dht_ref[...] = dht
        if has_post:
            dy, dgr = _rms_bwd(y_ref[...], gpost_ref[...], dht)
            dy_ref[...] = dy.astype(BF16)

            @pl.when(i == 0)
            def _():
                dgpost_ref[...] = jnp.zeros_like(dgpost_ref)

            dgpost_ref[...] += jnp.sum(dgr, axis=0, keepdims=True)

    row = pl.BlockSpec((t, d), lambda i: (i, 0))
    vec = pl.BlockSpec((1, d), lambda i: (0, 0))
    in_specs, args = [row], [dh]
    if has_pre:
        in_specs += [row, vec] + [row] * len(dhns)
        args += list(pre)
    if has_post:
        in_specs += [row, vec]
        args += list(post)
    out_shape, out_specs = [SDS((rows, d), F32)], [row]
    if has_post:
        out_shape.append(SDS((rows, d), BF16))
        out_specs.append(row)
    if has_pre:
        out_shape.append(SDS((1, d), F32))
        out_specs.append(vec)
    if has_post:
        out_shape.append(SDS((1, d), F32))
        out_specs.append(vec)
    res = list(pl.pallas_call(body, name=name, out_shape=out_shape, grid=(rows // t,), in_specs=in_specs,
                              out_specs=out_specs, compiler_params=_params())(*args))
    dht = res.pop(0)
    dy = res.pop(0) if has_post else None
    dgpre = res.pop(0) if has_pre else None
    dgpost = res.pop(0) if has_post else None
    return dht, dy, dgpre, dgpost


def _rms_gain_grad(x, g, dy, name):
    rows, d = x.shape

    def body(x_ref, g_ref, dy_ref, dg_ref):
        _, dgr = _rms_bwd(x_ref[...], g_ref[...], dy_ref[...])
        dg_ref[...] = jnp.sum(dgr, axis=0, keepdims=True)

    return pl.pallas_call(body, name=name, out_shape=SDS((1, d), F32), compiler_params=_params())(x, g, dy)


def _loss_fwd(h, target, name):
    rows, d = h.shape
    t = 512

    def body(h_ref, t_ref, loss_ref, dh_ref):
        i = pl.program_id(0)
        err = h_ref[...] - t_ref[...]
        dh_ref[...] = err * (1.0 / d)

        @pl.when(i == 0)
        def _():
            loss_ref[...] = jnp.zeros_like(loss_ref)

        part = jnp.sum(jnp.sum(err * err, axis=1, keepdims=True), axis=0, keepdims=True) * (0.5 / d)
        loss_ref[...] += jnp.broadcast_to(part, loss_ref.shape)

    row = pl.BlockSpec((t, d), lambda i: (i, 0))
    return pl.pallas_call(
        body, name=name, out_shape=(SDS((1, LANES), F32), SDS((rows, d), F32)), grid=(rows // t,),
        in_specs=[row, row], out_specs=(pl.BlockSpec((1, LANES), lambda i: (0, 0)), row), compiler_params=_params(),
    )(h, target)


def _rot_half(x, sign):
    w = x.shape[-1]
    lane = lax.broadcasted_iota(jnp.int32, x.shape, 1)
    first = (lane % HEAD_DIM) < (HEAD_DIM // 2)
    return jnp.where(first, -sign * pltpu.roll(x, w - HEAD_DIM // 2, axis=1), sign * pltpu.roll(x, HEAD_DIM // 2, axis=1))


def _rope_fwd(u, cos, sin, name):
    rows = u.shape[0]
    t, cw = 512, 256
    first_col = (3 * SB_WIDTH + 2 * CV_WIDTH) // cw

    def body(u_ref, c_ref, s_ref, o_ref):
        x = u_ref[...]
        c = jnp.tile(c_ref[...], (1, cw // LANES))
        s = jnp.tile(s_ref[...], (1, cw // LANES))
        o_ref[...] = (x * c + _rot_half(x, 1.0) * s).astype(BF16)

    tab = pl.BlockSpec((t, LANES), lambda i, j: (i, 0))
    return pl.pallas_call(
        body, name=name, out_shape=SDS((rows, 2 * DL_WIDTH), BF16), grid=(rows // t, 2 * DL_WIDTH // cw),
        in_specs=[pl.BlockSpec((t, cw), lambda i, j: (i, first_col + j)), tab, tab],
        out_specs=pl.BlockSpec((t, cw), lambda i, j: (i, j)), compiler_params=_params(),
    )(u, cos, sin)


def _rope_bwd(dqs, dks, dvs, cos, sin, name):
    rows = dqs[0].shape[0]
    t, w = 256, DL_WIDTH

    def body(*refs):
        c = jnp.tile(refs[9][...], (1, w // LANES))
        s = jnp.tile(refs[10][...], (1, w // LANES))
        o_ref = refs[11]
        dq = refs[0][...] + refs[1][...] + refs[2][...]
        dk = refs[3][...] + refs[4][...] + refs[5][...]
        dv = refs[6][...] + refs[7][...] + refs[8][...]
        o_ref[:, 0:w] = (dq * c + _rot_half(dq, -1.0) * s).astype(BF16)
        o_ref[:, w:2 * w] = (dk * c + _rot_half(dk, -1.0) * s).astype(BF16)
        o_ref[:, 2 * w:3 * w] = dv.astype(BF16)

    row = pl.BlockSpec((t, w), lambda i: (i, 0))
    tab = pl.BlockSpec((t, LANES), lambda i: (i, 0))
    return pl.pallas_call(
        body, name=name, out_shape=SDS((rows, 3 * w), BF16), grid=(rows // t,), in_specs=[row] * 9 + [tab, tab],
        out_specs=pl.BlockSpec((t, 3 * w), lambda i: (i, 0)), compiler_params=_params(),
    )(*dqs, *dks, *dvs, cos, sin)


SB_TILE = 256


def _softplus(z):
    return jnp.maximum(z, 0.0) + jnp.log(1.0 + jnp.exp(-jnp.abs(z)))


def _split_dot(x, tri, passes):
    acc = None
    rem = x
    for _ in range(passes):
        part = rem.astype(BF16)
        rem = rem - part.astype(F32)
        d = jnp.dot(part, tri, preferred_element_type=F32)
        acc = d if acc is None else acc + d
    return acc


def _tri(t, rel):
    j = lax.broadcasted_iota(jnp.int32, (t, t), 0)
    s = lax.broadcasted_iota(jnp.int32, (t, t), 1)
    return rel(j, s).astype(BF16)


def _sb_fwd(q, k, v, name):
    h, s_len, hd = q.shape
    t = SB_TILE
    nq = s_len // t
    scale = hd ** -0.5

    def body(q_ref, k_ref, v_ref, o_ref, tot_ref):
        i = pl.program_id(1)
        qv = q_ref[0]
        upper = _tri(t, lambda j, s: j > s)

        def tile(j, carry, diagonal):
            acc, run = carry
            start = pl.multiple_of(j * t, t)
            kv = k_ref[0, pl.ds(start, t), :]
            vv = v_ref[0, pl.ds(start, t), :]
            z = lax.dot_general(qv, kv, (((1,), (1,)), ((), ())), preferred_element_type=F32) * scale
            sp = _softplus(z)
            if diagonal:
                mask = lax.broadcasted_iota(jnp.int32, (t, t), 1) < lax.broadcasted_iota(jnp.int32, (t, t), 0)
                sp = jnp.where(mask, sp, 0.0)
            a = jnp.exp((z - sp) - (run + _split_dot(sp, upper, 2)))
            if diagonal:
                a = jnp.where(mask, a, 0.0)
            acc = acc + jnp.dot(a.astype(BF16), vv, preferred_element_type=F32)
            return acc, run + jnp.sum(sp, axis=1, keepdims=True)

        def pair(pp, carry):
            j = i - 1 - 2 * pp
            return tile(j - 1, tile(j, carry, False), False)

        carry = tile(i, (jnp.zeros((t, hd), F32), jnp.zeros((t, 1), F32)), True)
        carry = lax.fori_loop(0, i // 2, pair, carry)
        acc, run = lax.fori_loop(0, i % 2, lambda _, c: tile(0, c, False), carry)
        o_ref[0] = acc.astype(BF16)
        tot_ref[0] = run

    full = pl.BlockSpec((1, s_len, hd), lambda hh, i: (hh, 0, 0))
    tile = pl.BlockSpec((1, t, hd), lambda hh, i: (hh, i, 0))
    return pl.pallas_call(
        body, name=name, out_shape=(SDS((h, s_len, hd), BF16), SDS((h, s_len, 1), F32)), grid=(h, nq),
        in_specs=[tile, full, full], out_specs=(tile, pl.BlockSpec((1, t, 1), lambda hh, i: (hh, i, 0))),
        compiler_params=_params(),
    )(q, k, v)


def _sb_bwd(q, k, v, do, tot, name):
    h, s_len, hd = q.shape
    t = SB_TILE
    nq = s_len // t
    scale = hd ** -0.5

    def body(q_ref, k_ref, v_ref, do_ref, tot_ref, dq_ref, dk_ref, dv_ref, dk_acc, dv_acc):
        i = pl.program_id(1)

        @pl.when(i == 0)
        def _():
            dk_acc[...] = jnp.zeros_like(dk_acc)
            dv_acc[...] = jnp.zeros_like(dv_acc)

        qv = q_ref[0]
        dov = do_ref[0]
        total = tot_ref[0]
        upto = _tri(t, lambda j, s: j <= s)
        before = _tri(t, lambda j, s: j < s)

        def tile(j, carry, diagonal):
            dq, run_sp, run_g = carry
            start = pl.multiple_of(j * t, t)
            kv = k_ref[0, pl.ds(start, t), :]
            vv = v_ref[0, pl.ds(start, t), :]
            z = lax.dot_general(qv, kv, (((1,), (1,)), ((), ())), preferred_element_type=F32) * scale
            sp = _softplus(z)
            log_sig = z - sp
            if diagonal:
                mask = lax.broadcasted_iota(jnp.int32, (t, t), 1) < lax.broadcasted_iota(jnp.int32, (t, t), 0)
                sp = jnp.where(mask, sp, 0.0)
            between = total - (run_sp + _split_dot(sp, upto, 2))
            a = jnp.exp(log_sig - between)
            if diagonal:
                a = jnp.where(mask, a, 0.0)
            da = lax.dot_general(dov, vv, (((1,), (1,)), ((), ())), preferred_element_type=F32)
            g = a * da
            g_before = run_g + _split_dot(g, before, 3)
            sig = jnp.exp(log_sig)
            dz = (g * (1.0 - sig) - sig * g_before) * scale
            if diagonal:
                dz = jnp.where(mask, dz, 0.0)
            dz = dz.astype(BF16)
            dq = dq + jnp.dot(dz, kv, preferred_element_type=F32)
            dk_acc[pl.ds(start, t), :] += lax.dot_general(dz, qv, (((0,), (0,)), ((), ())), preferred_element_type=F32)
            dv_acc[pl.ds(start, t), :] += lax.dot_general(a.astype(BF16), dov, (((0,), (0,)), ((), ())),
                                                          preferred_element_type=F32)
            return dq, run_sp + jnp.sum(sp, axis=1, keepdims=True), run_g + jnp.sum(g, axis=1, keepdims=True)

        def pair(pp, carry):
            return tile(2 * pp + 1, tile(2 * pp, carry, False), False)

        zero = jnp.zeros((t, 1), F32)
        carry = lax.fori_loop(0, i // 2, pair, (jnp.zeros((t, hd), F32), zero, zero))
        carry = lax.fori_loop(0, i % 2, lambda _, c: tile(i - 1, c, False), carry)
        dq, _, _ = tile(i, carry, True)
        dq_ref[0] = dq.astype(BF16)

        @pl.when(i == nq - 1)
        def _():
            dk_ref[0] = dk_acc[...].astype(BF16)
            dv_ref[0] = dv_acc[...].astype(BF16)

    full = pl.BlockSpec((1, s_len, hd), lambda hh, i: (hh, 0, 0))
    tile = pl.BlockSpec((1, t, hd), lambda hh, i: (hh, i, 0))
    out = SDS((h, s_len, hd), BF16)
    return pl.pallas_call(
        body, name=name, out_shape=(out, out, out), grid=(h, nq),
        in_specs=[tile, full, full, tile, pl.BlockSpec((1, t, 1), lambda hh, i: (hh, i, 0))],
        out_specs=(tile, full, full), scratch_shapes=[pltpu.VMEM((s_len, hd), F32), pltpu.VMEM((s_len, hd), F32)],
        compiler_params=_params(),
    )(q, k, v, do, tot)


def _dl_scores(qv, kk, n):
    s = lax.dot_general(qv, kk, (((1,), (1,)), ((), ())), preferred_element_type=F32) * (HEAD_DIM ** -0.5)
    r = lax.broadcasted_iota(jnp.int32, s.shape, 0)
    c = lax.broadcasted_iota(jnp.int32, s.shape, 1)
    valid = (c >= r) & (c - r <= BLOCK) & ((n > 0) | (c >= BLOCK))
    return jnp.where(valid, s, NEG)


DL_BLOCKS_PER_STEP = SEQ // BLOCK
DL_UNROLL = 4


def _dl_window(ref, gi, n):
    prev = ref[gi, pl.ds(pl.multiple_of(jnp.maximum(n - 1, 0) * BLOCK, BLOCK), BLOCK), :]
    cur = ref[gi, pl.ds(pl.multiple_of(n * BLOCK, BLOCK), BLOCK), :]
    return jnp.concatenate([prev, cur], axis=0)


def _dl_fwd(q, k, v, name):
    g, l, hd = q.shape
    nb = l // BLOCK
    gb = DL_BLOCKS_PER_STEP // nb

    def body(q_ref, k_ref, v_ref, o_ref, lse_ref):
        def step(idx, _):
            gi, n = idx // nb, idx % nb
            rows = pl.ds(pl.multiple_of(n * BLOCK, BLOCK), BLOCK)
            s = _dl_scores(q_ref[gi, rows, :], _dl_window(k_ref, gi, n), n)
            m = jnp.max(s, axis=-1, keepdims=True)
            p = jnp.exp(s - m)
            den = jnp.sum(p, axis=-1, keepdims=True)
            o_ref[gi, rows, :] = jnp.dot((p / den).astype(BF16), _dl_window(v_ref, gi, n), preferred_element_type=F32)
            lse_ref[gi, rows, :] = jnp.broadcast_to(m + jnp.log(den), (BLOCK, hd))
            return 0

        lax.fori_loop(0, gb * nb, step, 0, unroll=DL_UNROLL)

    spec = pl.BlockSpec((gb, l, hd), lambda i: (i, 0, 0))
    out = SDS((g, l, hd), F32)
    return pl.pallas_call(body, name=name, out_shape=(out, out), grid=(g // gb,), in_specs=[spec] * 3, out_specs=(spec, spec),
                          compiler_params=_params())(q, k, v)


def _dl_bwd(q, k, v, do, o_mix, wt, lse, name):
    g, l, hd = q.shape
    nb = l // BLOCK
    gb = DL_BLOCKS_PER_STEP // nb
    scale = HEAD_DIM ** -0.5

    def body(q_ref, k_ref, v_ref, do_ref, om_ref, wt_ref, lse_ref, dq_ref, dk_ref, dv_ref):
        dk_ref[...] = jnp.zeros_like(dk_ref)
        dv_ref[...] = jnp.zeros_like(dv_ref)

        def step(idx, _):
            gi, n = idx // nb, idx % nb
            rows = pl.ds(pl.multiple_of(n * BLOCK, BLOCK), BLOCK)
            prev = pl.ds(pl.multiple_of(jnp.maximum(n - 1, 0) * BLOCK, BLOCK), BLOCK)
            qv = q_ref[gi, rows, :]
            kk = _dl_window(k_ref, gi, n)
            vv = _dl_window(v_ref, gi, n)
            s = _dl_scores(qv, kk, n)
            p = jnp.exp(s - jnp.max(lse_ref[gi, rows, :], axis=-1, keepdims=True))
            dov = do_ref[gi, rows, :]
            w = jnp.max(wt_ref[gi, rows, :], axis=-1, keepdims=True)
            d_all = jnp.sum(dov * om_ref[gi, rows, :], axis=-1, keepdims=True)
            do_n = (dov * w).astype(BF16)
            dp = lax.dot_general(do_n, vv, (((1,), (1,)), ((), ())), preferred_element_type=F32)
            ds = (p * (dp - w * d_all) * scale).astype(BF16)
            dq_ref[gi, rows, :] = jnp.dot(ds, kk, preferred_element_type=F32)
            dkk = lax.dot_general(ds, qv, (((0,), (0,)), ((), ())), preferred_element_type=F32)
            dvv = lax.dot_general(p.astype(BF16), do_n, (((0,), (0,)), ((), ())), preferred_element_type=F32)
            dk_ref[gi, prev, :] += dkk[:BLOCK]
            dv_ref[gi, prev, :] += dvv[:BLOCK]
            dk_ref[gi, rows, :] += dkk[BLOCK:]
            dv_ref[gi, rows, :] += dvv[BLOCK:]
            return 0

        lax.fori_loop(0, gb * nb, step, 0, unroll=DL_UNROLL)

    spec = pl.BlockSpec((gb, l, hd), lambda i: (i, 0, 0))
    out = SDS((g, l, hd), F32)
    return pl.pallas_call(body, name=name, out_shape=(out, out, out), grid=(g // gb,), in_specs=[spec] * 7,
                          out_specs=(spec, spec, spec), compiler_params=_params())(q, k, v, do, o_mix, wt, lse)


def _dl_mix_fwd(outs, lses, name):
    rows, w = outs[0].shape
    t = 256

    def body(o1, o2, o3, l1, l2, l3, ob_ref, of_ref, w1, w2, w3):
        a, b, c = l1[...], l2[...], l3[...]
        m = jnp.maximum(jnp.maximum(a, b), c)
        ea, eb, ec = jnp.exp(a - m), jnp.exp(b - m), jnp.exp(c - m)
        den = ea + eb + ec
        wa, wb, wc = ea / den, eb / den, ec / den
        o = wa * o1[...] + wb * o2[...] + wc * o3[...]
        ob_ref[...] = o.astype(BF16)
        of_ref[...] = o
        w1[...] = wa
        w2[...] = wb
        w3[...] = wc

    row = pl.BlockSpec((t, w), lambda i: (i, 0))
    f = SDS((rows, w), F32)
    return pl.pallas_call(body, name=name, out_shape=(SDS((rows, w), BF16), f, f, f, f), grid=(rows // t,),
                          in_specs=[row] * 6, out_specs=(row,) * 5, compiler_params=_params())(*outs, *lses)


def _x_probs(qh, kh):
    s = lax.dot_general(qh, kh, (((1,), (1,)), ((), ())), preferred_element_type=F32) * (X_HEAD_DIM ** -0.5)
    e = jnp.exp(s - jnp.max(s, axis=-1, keepdims=True))
    return e / jnp.sum(e, axis=-1, keepdims=True)


def _xattn_fwd(q, k, v, name):
    rows, d = q.shape
    t = 512

    def body(q_ref, k_ref, v_ref, o_ref):
        for hh in range(X_HEADS):
            cols = slice(hh * X_HEAD_DIM, (hh + 1) * X_HEAD_DIM)
            p = _x_probs(q_ref[:, cols], k_ref[:, cols])
            o_ref[:, cols] = jnp.dot(p.astype(BF16), v_ref[:, cols], preferred_element_type=F32).astype(BF16)

    row = pl.BlockSpec((t, d), lambda i: (i, 0))
    mem = pl.BlockSpec((N_MEM, d), lambda i: (0, 0))
    return pl.pallas_call(body, name=name, out_shape=SDS((rows, d), BF16), grid=(rows // t,), in_specs=[row, mem, mem],
                          out_specs=row, compiler_params=_params())(q, k, v)


def _xattn_bwd(q, k, v, do, name):
    rows, d = q.shape
    t = 512
    scale = X_HEAD_DIM ** -0.5

    def body(q_ref, k_ref, v_ref, do_ref, dq_ref, dk_ref, dv_ref):
        @pl.when(pl.program_id(0) == 0)
        def _():
            dk_ref[...] = jnp.zeros_like(dk_ref)
            dv_ref[...] = jnp.zeros_like(dv_ref)

        for hh in range(X_HEADS):
            cols = slice(hh * X_HEAD_DIM, (hh + 1) * X_HEAD_DIM)
            qh, kh, vh, doh = q_ref[:, cols], k_ref[:, cols], v_ref[:, cols], do_ref[:, cols]
            p = _x_probs(qh, kh)
            dp = lax.dot_general(doh, vh, (((1,), (1,)), ((), ())), preferred_element_type=F32)
            ds = (p * (dp - jnp.sum(p * dp, axis=-1, keepdims=True)) * scale).astype(BF16)
            dq_ref[:, cols] = jnp.dot(ds, kh, preferred_element_type=F32).astype(BF16)
            dk_ref[:, cols] += lax.dot_general(ds, qh, (((0,), (0,)), ((), ())), preferred_element_type=F32)
            dv_ref[:, cols] += lax.dot_general(p.astype(BF16), doh, (((0,), (0,)), ((), ())), preferred_element_type=F32)

    row = pl.BlockSpec((t, d), lambda i: (i, 0))
    mem = pl.BlockSpec((N_MEM, d), lambda i: (0, 0))
    return pl.pallas_call(
        body, name=name, out_shape=(SDS((rows, d), BF16), SDS((N_MEM, d), F32), SDS((N_MEM, d), F32)), grid=(rows // t,),
        in_specs=[row, mem, mem, row], out_specs=(row, mem, mem), compiler_params=_params(),
    )(q, k, v, do)


CV_TILE = 256
CV_HALO = 32
CV_LEAD = CV_HALO - (CV_KERNEL - 1)


def _shifted(win, off, rows):
    n = win.shape[0]
    return pltpu.roll(win, (n - off) % n, axis=0)[:rows]


def _glu(val, gate):
    return val * jax.nn.sigmoid(gate)


def _ln_parts(c):
    mu = jnp.mean(c, axis=-1, keepdims=True)
    xc = c - mu
    rstd = lax.rsqrt(jnp.mean(xc * xc, axis=-1, keepdims=True) + EPS)
    return xc * rstd, rstd


def _cv_fwd(u, cv_w, cv_b, ln_g, ln_b, name):
    rows = u.shape[0]
    t, w = CV_TILE, CV_WIDTH
    val_col = 3 * SB_WIDTH // w
    ratio = t // CV_HALO

    def body(val_ref, gate_ref, pval_ref, pgate_ref, w_ref, b_ref, g_ref, beta_ref, s_ref, c_ref):
        i = pl.program_id(0)
        hist = jnp.where(i > 0, _glu(pval_ref[...], pgate_ref[...]), 0.0)
        win = jnp.concatenate([hist, _glu(val_ref[...], gate_ref[...])], axis=0)
        acc = jnp.broadcast_to(b_ref[...], (t, w))
        for kk in range(CV_KERNEL):
            acc = acc + _shifted(win, CV_LEAD + kk, t) * w_ref[kk:kk + 1, :]
        c_ref[...] = acc
        n, _ = _ln_parts(acc)
        y = n * g_ref[...] + beta_ref[...]
        s_ref[...] = (y * jax.nn.sigmoid(y)).astype(BF16)

    cur = lambda col: pl.BlockSpec((t, w), lambda i: (i, col))
    prev = lambda col: pl.BlockSpec((CV_HALO, w), lambda i: (jnp.maximum(i * ratio - 1, 0), col))
    vec = pl.BlockSpec((1, w), lambda i: (0, 0))
    return pl.pallas_call(
        body, name=name, out_shape=(SDS((rows, w), BF16), SDS((rows, w), F32)), grid=(rows // t,),
        in_specs=[cur(val_col), cur(val_col + 1), prev(val_col), prev(val_col + 1),
                  pl.BlockSpec((CV_KERNEL, w), lambda i: (0, 0)), vec, vec, vec],
        out_specs=(pl.BlockSpec((t, w), lambda i: (i, 0)),) * 2, compiler_params=_params(),
    )(u, u, u, u, cv_w, cv_b, ln_g, ln_b)


def _cv_bwd(u, c, ds, db_out, cv_w, ln_g, ln_b, name):
    rows = u.shape[0]
    t, w = CV_TILE, CV_WIDTH
    val_col = 3 * SB_WIDTH // w
    ratio = t // CV_HALO
    nt = rows // t

    def conv_out_grad(c_v, ds_v, g_v, beta_v):
        n, rstd = _ln_parts(c_v)
        y = n * g_v + beta_v
        sig = jax.nn.sigmoid(y)
        dy = ds_v * (sig * (1.0 + y * (1.0 - sig)))
        dn = dy * g_v
        dc = rstd * (dn - jnp.mean(dn, axis=-1, keepdims=True) - n * jnp.mean(dn * n, axis=-1, keepdims=True))
        return dc, dy, n

    def body(val_ref, gate_ref, pval_ref, pgate_ref, c_ref, nc_ref, ds_ref, nds_ref, dbo_ref, w_ref, g_ref, beta_ref,
             dvg_ref, dw_ref, db_ref, dg_ref, dbeta_ref, dpwb_ref):
        i = pl.program_id(0)

        @pl.when(i == 0)
        def _():
            for r in (dw_ref, db_ref, dg_ref, dbeta_ref, dpwb_ref):
                r[...] = jnp.zeros_like(r)

        g_v, beta_v = g_ref[...], beta_ref[...]
        dc, dy, n = conv_out_grad(c_ref[...], ds_ref[...], g_v, beta_v)
        dc_next, _, _ = conv_out_grad(nc_ref[...], nds_ref[...], g_v, beta_v)
        dc_next = jnp.where(i < nt - 1, dc_next, 0.0)
        dg_ref[...] += jnp.sum(dy * n, axis=0, keepdims=True)
        dbeta_ref[...] += jnp.sum(dy, axis=0, keepdims=True)
        db_ref[...] += jnp.sum(dc, axis=0, keepdims=True)
        dpwb_ref[...] += jnp.sum(dbo_ref[...], axis=0, keepdims=True)

        val, gate = val_ref[...], gate_ref[...]
        hist = jnp.where(i > 0, _glu(pval_ref[...], pgate_ref[...]), 0.0)
        win = jnp.concatenate([hist, _glu(val, gate)], axis=0)
        dc_ext = jnp.concatenate([dc, dc_next], axis=0)
        dglu = jnp.zeros((t, w), F32)
        for kk in range(CV_KERNEL):
            dw_ref[kk:kk + 1, :] += jnp.sum(dc * _shifted(win, CV_LEAD + kk, t), axis=0, keepdims=True)
            dglu = dglu + _shifted(dc_ext, CV_KERNEL - 1 - kk, t) * w_ref[kk:kk + 1, :]
        sig = jax.nn.sigmoid(gate)
        dvg_ref[:, 0:w] = (dglu * sig).astype(BF16)
        dvg_ref[:, w:2 * w] = (dglu * val * sig * (1.0 - sig)).astype(BF16)

    cur = lambda col: pl.BlockSpec((t, w), lambda i: (i, col))
    prev = lambda col: pl.BlockSpec((CV_HALO, w), lambda i: (jnp.maximum(i * ratio - 1, 0), col))
    nxt = pl.BlockSpec((CV_HALO, w), lambda i: (jnp.minimum((i + 1) * ratio, rows // CV_HALO - 1), 0))
    vec = pl.BlockSpec((1, w), lambda i: (0, 0))
    return pl.pallas_call(
        body, name=name,
        out_shape=(SDS((rows, 2 * w), BF16), SDS((CV_HALO, w), F32), SDS((1, w), F32), SDS((1, w), F32), SDS((1, w), F32),
                   SDS((1, w), F32)),
        grid=(nt,),
        in_specs=[cur(val_col), cur(val_col + 1), prev(val_col), prev(val_col + 1), cur(0), nxt, cur(0), nxt, cur(0),
                  pl.BlockSpec((CV_KERNEL, w), lambda i: (0, 0)), vec, vec],
        out_specs=(pl.BlockSpec((t, 2 * w), lambda i: (i, 0)), pl.BlockSpec((CV_HALO, w), lambda i: (0, 0)), vec, vec, vec, vec),
        compiler_params=_params(),
    )(u, u, u, u, c, c, ds, ds, db_out, cv_w, ln_g, ln_b)


FFN_TILE = 512
FFN_COLS = 256
FFN_HALO = 8
FFN_KERNEL = 3
N_FF_BLOCKS = D_FF // FFN_COLS


def _conv3(prev8, cur, w_ref, b_ref, first):
    t = cur.shape[0]
    win = jnp.concatenate([jnp.where(first, 0.0, prev8), cur], axis=0)
    return (b_ref[...] + _shifted(win, FFN_HALO - 2, t) * w_ref[0:1, :] + _shifted(win, FFN_HALO - 1, t) * w_ref[1:2, :]
            + cur * w_ref[2:3, :])


def _gelu_gate(gate, val):
    return jax.nn.gelu(gate, approximate=True) * val


def _ffn_specs(t):
    ratio = t // FFN_HALO
    cur = pl.BlockSpec((t, FFN_COLS), lambda j, i: (i, j))
    prev = pl.BlockSpec((FFN_HALO, FFN_COLS), lambda j, i: (jnp.maximum(i * ratio - 1, 0), j))
    wsp = pl.BlockSpec((FFN_KERNEL, FFN_COLS), lambda j, i: (0, j))
    bsp = pl.BlockSpec((1, FFN_COLS), lambda j, i: (0, j))
    return cur, prev, wsp, bsp


def _ffn_act_fwd(up_g, up_v, w_g, w_v, b_g, b_v, name):
    rows = up_g.shape[0]
    t = FFN_TILE
    cur, prev, wsp, bsp = _ffn_specs(t)

    def body(g_ref, v_ref, pg_ref, pv_ref, wg_ref, wv_ref, bg_ref, bv_ref, o_ref):
        first = pl.program_id(1) == 0
        gate = _conv3(pg_ref[...], g_ref[...], wg_ref, bg_ref, first)
        val = _conv3(pv_ref[...], v_ref[...], wv_ref, bv_ref, first)
        o_ref[...] = _gelu_gate(gate, val).astype(BF16)

    return pl.pallas_call(
        body, name=name, out_shape=SDS((rows, D_FF), BF16), grid=(N_FF_BLOCKS, rows // t),
        in_specs=[cur, cur, prev, prev, wsp, wsp, bsp, bsp], out_specs=cur, compiler_params=_params(),
    )(up_g, up_v, up_g, up_v, w_g, w_v, b_g, b_v)


def _ffn_conv_grad(up_g, up_v, dact, w_g, w_v, b_g, b_v, name):
    rows = up_g.shape[0]
    t = FFN_TILE
    cur, prev, wsp, bsp = _ffn_specs(t)

    def body(g_ref, v_ref, pg_ref, pv_ref, da_ref, wg_ref, wv_ref, bg_ref, bv_ref,
             dcg_ref, dcv_ref, dwg_ref, dwv_ref, dbg_ref, dbv_ref):
        first = pl.program_id(1) == 0
        gate_in, val_in = g_ref[...], v_ref[...]
        pg, pv = pg_ref[...], pv_ref[...]
        gate = _conv3(pg, gate_in, wg_ref, bg_ref, first)
        val = _conv3(pv, val_in, wv_ref, bv_ref, first)
        _, vjp = jax.vjp(_gelu_gate, gate, val)
        dgate, dval = vjp(da_ref[...])
        dcg_ref[...] = dgate
        dcv_ref[...] = dval

        @pl.when(first)
        def _():
            for r in (dwg_ref, dwv_ref, dbg_ref, dbv_ref):
                r[...] = jnp.zeros_like(r)

        for dc, hist, x, dw_ref, db_ref in ((dgate, pg, gate_in, dwg_ref, dbg_ref), (dval, pv, val_in, dwv_ref, dbv_ref)):
            win = jnp.concatenate([jnp.where(first, 0.0, hist), x], axis=0)
            for kk in range(FFN_KERNEL):
                dw_ref[kk:kk + 1, :] += jnp.sum(dc * _shifted(win, FFN_HALO - 2 + kk, t), axis=0, keepdims=True)
            db_ref[...] += jnp.sum(dc, axis=0, keepdims=True)

    big, wshape, bshape = SDS((rows, D_FF), F32), SDS((FFN_KERNEL, D_FF), F32), SDS((1, D_FF), F32)
    return pl.pallas_call(
        body, name=name, out_shape=(big, big, wshape, wshape, bshape, bshape), grid=(N_FF_BLOCKS, rows // t),
        in_specs=[cur, cur, prev, prev, cur, wsp, wsp, bsp, bsp], out_specs=(cur, cur, wsp, wsp, bsp, bsp),
        compiler_params=_params(),
    )(up_g, up_v, up_g, up_v, dact, w_g, w_v, b_g, b_v)


def _ffn_conv_bwd(dc, conv_w, name):
    rows, cols = dc.shape
    t = FFN_TILE
    ratio = t // FFN_HALO
    nt = rows // t

    def body(dc_ref, nx_ref, w_ref, o_ref):
        i = pl.program_id(1)
        cur = dc_ref[...]
        ext = jnp.concatenate([cur, jnp.where(i < nt - 1, nx_ref[...], 0.0)], axis=0)
        o_ref[...] = (cur * w_ref[2:3, :] + _shifted(ext, 1, t) * w_ref[1:2, :] + _shifted(ext, 2, t) * w_ref[0:1, :]).astype(BF16)

    return pl.pallas_call(
        body, name=name, out_shape=SDS((rows, cols), BF16), grid=(cols // FFN_COLS, nt),
        in_specs=[pl.BlockSpec((t, FFN_COLS), lambda j, i: (i, j)),
                  pl.BlockSpec((FFN_HALO, FFN_COLS), lambda j, i: (jnp.minimum((i + 1) * ratio, rows // FFN_HALO - 1), j)),
                  pl.BlockSpec((FFN_KERNEL, FFN_COLS), lambda j, i: (0, j))],
        out_specs=pl.BlockSpec((t, FFN_COLS), lambda j, i: (i, j)), compiler_params=_params(),
    )(dc, dc, conv_w)


def _adamw(parts, w, m, v, name, first_row=0, t=None):
    rows, cols = w.shape
    if t is None:
        t = _pick(rows, (512, 256, 128)) if rows > 512 else rows
    assert first_row % t == 0 and rows % t == 0
    first_block = first_row // t

    def body(p_ref, w_ref, m_ref, v_ref, g_ref, d_ref, nm_ref, nv_ref):
        g = p_ref[0].astype(F32)
        for s in range(1, N_DEV):
            g = g + p_ref[s].astype(F32)
        nm = ADAM_B1 * m_ref[...] + (1.0 - ADAM_B1) * g
        nv = ADAM_B2 * v_ref[...] + (1.0 - ADAM_B2) * jnp.square(g)
        m_hat = nm / (1.0 - ADAM_B1 ** ADAM_STEP)
        v_hat = nv / (1.0 - ADAM_B2 ** ADAM_STEP)
        g_ref[...] = g
        d_ref[...] = -ADAM_LR * (m_hat / (jnp.sqrt(v_hat) + ADAM_EPS) + ADAM_WD * w_ref[...])
        nm_ref[...] = nm
        nv_ref[...] = nv

    row = pl.BlockSpec((t, cols), lambda i: (i, 0))
    out = SDS((rows, cols), F32)
    return pl.pallas_call(
        body, name=name, out_shape=(out,) * 4, grid=(rows // t,),
        in_specs=[pl.BlockSpec((N_DEV, t, cols), lambda i: (0, first_block + i, 0)), row, row, row], out_specs=(row,) * 4,
        compiler_params=_params(),
    )(parts, w, m, v)


def _all_gather(x, name):
    def body(x_ref, out_ref, send_sems, recv_sems, local_sem):
        x_, y_, c_ = lax.axis_index("x"), lax.axis_index("y"), lax.axis_index("c")
        me, sibling = (x_, y_, c_), (x_, y_, 1 - c_)
        chips = [(1 - x_, y_), (x_, 1 - y_), (1 - x_, 1 - y_)]

        def slot(px, py, pc):
            return out_ref.at[4 * px + 2 * py + pc]

        def copy(kk, block, to, src=None):
            return pltpu.make_async_remote_copy(
                src_ref=slot(*block) if src is None else src, dst_ref=slot(*block),
                send_sem=send_sems.at[kk], recv_sem=recv_sems.at[kk], device_id=to, device_id_type=MESH)

        mine = pltpu.make_async_copy(x_ref, slot(*me), local_sem)
        mine.start()
        first = [copy(0, me, sibling, src=x_ref)]
        first += [copy(1 + j, me, (*chip, c_), src=x_ref) for j, chip in enumerate(chips)]
        for cp in first:
            cp.start()
        passed = [copy(4 + j, (*chip, c_), sibling) for j, chip in enumerate(chips)]
        for j, chip in enumerate(chips):
            copy(1 + j, (*chip, c_), me).wait_recv()
            passed[j].start()
        copy(0, sibling, me).wait_recv()
        for j, chip in enumerate(chips):
            copy(4 + j, (*chip, 1 - c_), me).wait_recv()
        for cp in first + passed:
            cp.wait_send()
        mine.wait()

    return pl.pallas_call(
        body, name=name, out_shape=SDS((N_DEV,) + x.shape, x.dtype),
        in_specs=[pl.BlockSpec(memory_space=pl.ANY)], out_specs=pl.BlockSpec(memory_space=pl.ANY),
        scratch_shapes=[pltpu.SemaphoreType.DMA((7,)), pltpu.SemaphoreType.DMA((7,)), pltpu.SemaphoreType.DMA],
        compiler_params=pltpu.CompilerParams(has_side_effects=True),
    )(x)


def _all_to_all(x, name):
    def body(x_ref, out_ref, send_sems, recv_sems, local_sem):
        x_, y_, c_ = lax.axis_index("x"), lax.axis_index("y"), lax.axis_index("c")
        me = 4 * x_ + 2 * y_ + c_
        mine = pltpu.make_async_copy(x_ref.at[me], out_ref.at[me], local_sem)
        mine.start()
        copies = []
        for r in range(1, N_DEV):
            px = 1 - x_ if r & 4 else x_
            py = 1 - y_ if r & 2 else y_
            pc = 1 - c_ if r & 1 else c_
            cp = pltpu.make_async_remote_copy(
                src_ref=x_ref.at[4 * px + 2 * py + pc], dst_ref=out_ref.at[me],
                send_sem=send_sems.at[r - 1], recv_sem=recv_sems.at[r - 1], device_id=(px, py, pc), device_id_type=MESH)
            cp.start()
            copies.append(cp)
        for cp in copies:
            cp.wait_recv()
        for cp in copies:
            cp.wait_send()
        mine.wait()

    return pl.pallas_call(
        body, name=name, out_shape=SDS(x.shape, x.dtype),
        in_specs=[pl.BlockSpec(memory_space=pl.ANY)], out_specs=pl.BlockSpec(memory_space=pl.ANY),
        scratch_shapes=[pltpu.SemaphoreType.DMA((7,)), pltpu.SemaphoreType.DMA((7,)), pltpu.SemaphoreType.DMA],
        compiler_params=pltpu.CompilerParams(has_side_effects=True),
    )(x)


BIG = ("w_in", "cv_pw_w", "w_out", "x_wq", "x_wk", "x_wv", "x_wo", "ffn_w_up", "ffn_w_down")
PACKED = (("ffn_w_up", 704, 0), ("w_in", 352, 1408), ("ffn_w_down", 352, 2112), ("w_out", 128, 2816),
          ("x_wq", 128, 3072), ("x_wk", 128, 3328), ("x_wv", 128, 3584), ("x_wo", 128, 3840))
TRANSPOSED = ("w_in", "ffn_w_up")
PW_FIRST_ROW = 4096
PW_ROWS = 16
PACK_ROWS = PW_FIRST_ROW + PW_ROWS
COL_SHARDED = ("w_in", "ffn_w_up", "cv_w", "ffn_conv_w")
SMALL_SHARDED = ("cv_w", "ffn_conv_w")
REPLICATED = ("mix_norm_pre", "cv_b", "cv_ln_g", "cv_ln_b", "cv_pw_b", "mix_norm_post", "x_norm_pre", "mem_norm",
              "x_norm_post", "ffn_norm_pre", "ffn_conv_b", "ffn_norm_post")
WEIGHTS = ("mix_norm_pre", "w_in", "cv_w", "cv_b", "cv_ln_g", "cv_ln_b", "cv_pw_w", "cv_pw_b", "w_out", "mix_norm_post",
           "x_norm_pre", "mem_norm", "x_wq", "x_wk", "x_wv", "x_wo", "x_norm_post", "ffn_norm_pre", "ffn_w_up",
           "ffn_conv_w", "ffn_conv_b", "ffn_w_down", "ffn_norm_post")
PAYLOAD_COLS = 1024


PAYLOAD_ROW_TILE = 16


def _pad_rows(flat, cols):
    n = flat.shape[-1]
    rows = -(-n // (cols * PAYLOAD_ROW_TILE)) * PAYLOAD_ROW_TILE
    pad = rows * cols - n
    if pad:
        flat = jnp.concatenate([flat, jnp.zeros(flat.shape[:-1] + (pad,), flat.dtype)], axis=-1)
    return flat.reshape(flat.shape[:-1] + (rows, cols))


def _unshard(name, parts):
    n, depth, r, c = parts.shape
    if name in COL_SHARDED:
        return parts.transpose(1, 2, 0, 3).reshape(depth, r, n * c)
    return parts.transpose(1, 0, 2, 3).reshape(depth, n * r, c)


def _to_shards(name, full):
    depth, r, c = full.shape
    if name in COL_SHARDED:
        return full.reshape(depth, r, N_DEV, c // N_DEV).transpose(2, 0, 1, 3).reshape(N_DEV, -1)
    return full.reshape(depth, N_DEV, r // N_DEV, c).transpose(1, 0, 2, 3).reshape(N_DEV, -1)


def _to_sub(x, dil):
    s_len, w = x.shape
    h = w // HEAD_DIM
    return x.reshape(s_len // dil, dil, h, HEAD_DIM).transpose(1, 2, 0, 3).reshape(dil * h, s_len // dil, HEAD_DIM)


def _from_sub(x, dil):
    g, l, hd = x.shape
    h = g // dil
    return x.reshape(dil, h, l, hd).transpose(2, 0, 1, 3).reshape(l * dil, h * hd)


def _heads_major(x, h):
    return x.reshape(x.shape[0], h, HEAD_DIM).transpose(1, 0, 2)


def _tokens_major(x):
    return x.transpose(1, 0, 2).reshape(x.shape[1], -1)


def _ffn_halves(p):
    w, b = p["ffn_conv_w"], p["ffn_conv_b"]
    return w[:, :D_FF], w[:, D_FF:], b[:, :D_FF], b[:, D_FF:]


def _layer_fwd(l, h, hn, p, mem, cos, sin, g_next):
    sv = {"h0": h, "hn0": hn}
    u = _mm(hn, p["w_in"], "nt", F32, f"l{l}_in_proj")
    sv["u"] = u
    sb = _heads_major(u[:, :3 * SB_WIDTH].astype(BF16), 3 * SB_HEADS)
    sb_q, sb_k, sb_v = sb[:SB_HEADS], sb[SB_HEADS:2 * SB_HEADS], sb[2 * SB_HEADS:]
    a_out, sb_tot = _sb_fwd(sb_q, sb_k, sb_v, f"l{l}_sb_fwd")
    sv.update(sb_q=sb_q, sb_k=sb_k, sb_v=sb_v, sb_tot=sb_tot)

    cv_s, cv_c = _cv_fwd(u, p["cv_w"], p["cv_b"], p["cv_ln_g"], p["cv_ln_b"], f"l{l}_cv_fwd")
    b_out = _mm(cv_s, p["cv_pw_w"], "nn", BF16, f"l{l}_cv_pw", bias=p["cv_pw_b"])
    sv.update(cv_s=cv_s, cv_c=cv_c)

    qk = _rope_fwd(u, cos, sin, f"l{l}_rope_fwd")
    dl_v = u[:, IN_WIDTH - DL_WIDTH:].astype(BF16)
    outs, lses, subs = [], [], []
    for b, (_, dil) in enumerate(DL_PATTERN):
        qs, ks, vs = _to_sub(qk[:, :DL_WIDTH], dil), _to_sub(qk[:, DL_WIDTH:], dil), _to_sub(dl_v, dil)
        o, lse = _dl_fwd(qs, ks, vs, f"l{l}_dl{b}_fwd")
        subs.append((qs, ks, vs, lse))
        outs.append(_from_sub(o, dil))
        lses.append(_from_sub(lse, dil))
    c_out, c_out_f32, w1, w2, w3 = _dl_mix_fwd(outs, lses, f"l{l}_dl_mix")
    sv.update(dl_subs=subs, dl_o=c_out_f32, dl_w=(w1, w2, w3))

    mix = jnp.concatenate([_tokens_major(a_out), b_out, c_out], axis=-1)
    y = _mm(mix, p["w_out"], "nn", F32, f"l{l}_out_proj")
    h1, hn1 = _res_norm_fwd(h, y, p["mix_norm_post"], p["x_norm_pre"], f"l{l}_mix_post")
    sv.update(mix=mix, y_mix=y, h1=h1, hn1=hn1)

    xq = _mm(hn1, p["x_wq"], "nn", BF16, f"l{l}_xq")
    memn = _rms_fwd(mem, p["mem_norm"], f"l{l}_mem_norm")
    xk = _mm(memn, p["x_wk"], "nn", BF16, f"l{l}_xk")
    xv = _mm(memn, p["x_wv"], "nn", BF16, f"l{l}_xv")
    xo = _xattn_fwd(xq, xk, xv, f"l{l}_xattn_fwd")
    y = _mm(xo, p["x_wo"], "nn", F32, f"l{l}_xo_proj")
    h2, hn2 = _res_norm_fwd(h1, y, p["x_norm_post"], p["ffn_norm_pre"], f"l{l}_x_post")
    sv.update(xq=xq, xk=xk, xv=xv, xo=xo, memn=memn, y_x=y, h2=h2, hn2=hn2)

    up_g = _mm(hn2, p["ffn_w_up"][0], "nt", F32, f"l{l}_ffn_up_gate")
    up_v = _mm(hn2, p["ffn_w_up"][1], "nt", F32, f"l{l}_ffn_up_val")
    act = _ffn_act_fwd(up_g, up_v, *_ffn_halves(p), f"l{l}_ffn_act")
    y = _mm(act, p["ffn_w_down"], "nn", F32, f"l{l}_ffn_down")
    h3, hn3 = _res_norm_fwd(h2, y, p["ffn_norm_post"], g_next, f"l{l}_ffn_post")
    sv.update(up_g=up_g, up_v=up_v, act=act, y_ffn=y)
    return h3, hn3, sv


def _layer_bwd(l, dh, dy, p, sv, mem, cos, sin, first_layer):
    gr = {}
    dact = _mm(dy, p["ffn_w_down"], "nt", F32, f"l{l}_d_act")
    gr["ffn_w_down"] = _mm(sv["act"], dy, "tn", BF16, f"l{l}_dw_down")
    halves = _ffn_halves(p)
    dcg, dcv, dwg, dwv, dbg, dbv = _ffn_conv_grad(sv["up_g"], sv["up_v"], dact, *halves, f"l{l}_ffn_dconv")
    gr["ffn_conv_w"] = jnp.concatenate([dwg, dwv], axis=1)
    gr["ffn_conv_b"] = jnp.concatenate([dbg, dbv], axis=1)
    dup_g = _ffn_conv_bwd(dcg, halves[0], f"l{l}_ffn_dup_gate")
    dup_v = _ffn_conv_bwd(dcv, halves[1], f"l{l}_ffn_dup_val")
    dhn = (_mm(dup_g, p["ffn_w_up"][0], "nn", F32, f"l{l}_d_hn2_gate"), _mm(dup_v, p["ffn_w_up"][1], "nn", F32, f"l{l}_d_hn2_val"))
    gr["ffn_w_up"] = (_mm(dup_g, sv["hn2"], "tn", BF16, f"l{l}_dw_up_gate"), _mm(dup_v, sv["hn2"], "tn", BF16, f"l{l}_dw_up_val"))
    dh, dy, gr["ffn_norm_pre"], gr["x_norm_post"] = _norm_bwd(
        dh, (sv["h2"], p["ffn_norm_pre"], dhn), (sv["y_x"], p["x_norm_post"]), f"l{l}_x_post_bwd")

    do = _mm(dy, p["x_wo"], "nt", BF16, f"l{l}_d_xo")
    gr["x_wo"] = _mm(sv["xo"], dy, "tn", BF16, f"l{l}_dw_xo")
    dq, dk, dv = _xattn_bwd(sv["xq"], sv["xk"], sv["xv"], do, f"l{l}_xattn_bwd")
    dhn = _mm(dq, p["x_wq"], "nt", F32, f"l{l}_d_hn1")
    gr["x_wq"] = _mm(sv["hn1"], dq, "tn", BF16, f"l{l}_dw_xq")
    gr["x_wk"] = _mm(sv["memn"], dk, "tn", BF16, f"l{l}_dw_xk")
    gr["x_wv"] = _mm(sv["memn"], dv, "tn", BF16, f"l{l}_dw_xv")
    dmemn = _mm(dk, p["x_wk"], "nt", F32, f"l{l}_d_memn_k") + _mm(dv, p["x_wv"], "nt", F32, f"l{l}_d_memn_v")
    gr["mem_norm"] = _rms_gain_grad(mem, p["mem_norm"], dmemn, f"l{l}_mem_norm_bwd")
    dh, dy, gr["x_norm_pre"], gr["mix_norm_post"] = _norm_bwd(
        dh, (sv["h1"], p["x_norm_pre"], dhn), (sv["y_mix"], p["mix_norm_post"]), f"l{l}_mix_post_bwd")

    dmix = _mm(dy, p["w_out"], "nt", F32, f"l{l}_d_mix")
    gr["w_out"] = _mm(sv["mix"], dy, "tn", BF16, f"l{l}_dw_out")
    do_a = _heads_major(dmix[:, :SB_WIDTH].astype(BF16), SB_HEADS)
    dq, dk, dv = _sb_bwd(sv["sb_q"], sv["sb_k"], sv["sb_v"], do_a, sv["sb_tot"], f"l{l}_sb_bwd")
    du_sb = _tokens_major(jnp.concatenate([dq, dk, dv], axis=0))

    db_out = dmix[:, SB_WIDTH:SB_WIDTH + CV_WIDTH]
    ds = _mm(db_out, p["cv_pw_w"], "nt", F32, f"l{l}_d_cv_s")
    gr["cv_pw_w"] = _mm(sv["cv_s"], db_out, "tn", BF16, f"l{l}_dw_cv_pw")
    du_cv, dcvw, gr["cv_b"], gr["cv_ln_g"], gr["cv_ln_b"], gr["cv_pw_b"] = _cv_bwd(
        sv["u"], sv["cv_c"], ds, db_out, p["cv_w"], p["cv_ln_g"], p["cv_ln_b"], f"l{l}_cv_bwd")
    gr["cv_w"] = dcvw[:CV_KERNEL]

    dc_out = dmix[:, SB_WIDTH + CV_WIDTH:]
    dqs, dks, dvs = [], [], []
    for b, (_, dil) in enumerate(DL_PATTERN):
        qs, ks, vs, lse = sv["dl_subs"][b]
        dq, dk, dv = _dl_bwd(qs, ks, vs, _to_sub(dc_out, dil), _to_sub(sv["dl_o"], dil), _to_sub(sv["dl_w"][b], dil), lse,
                             f"l{l}_dl{b}_bwd")
        dqs.append(_from_sub(dq, dil))
        dks.append(_from_sub(dk, dil))
        dvs.append(_from_sub(dv, dil))
    du_dl = _rope_bwd(dqs, dks, dvs, cos, sin, f"l{l}_rope_bwd")

    du = jnp.concatenate([du_sb, du_cv, du_dl], axis=-1)
    dhn = _mm(du, p["w_in"], "nn", F32, f"l{l}_d_hn0")
    gr["w_in"] = _mm(du, sv["hn0"], "tn", BF16, f"l{l}_dw_in")
    post = None if first_layer else (sv["y_prev"], p["prev_ffn_norm_post"])
    dh, dy, gr["mix_norm_pre"], dg_prev = _norm_bwd(dh, (sv["h0"], p["mix_norm_pre"], dhn), post, f"l{l}_in_bwd")
    return dh, dy, gr, dg_prev


def kernel(x, mem, positions, mix_norm_pre, w_in, cv_w, cv_b, cv_ln_g, cv_ln_b, cv_pw_w, cv_pw_b, w_out, mix_norm_post, x_norm_pre, mem_norm, x_wq, x_wk, x_wv, x_wo, x_norm_post, ffn_norm_pre, ffn_w_up, ffn_conv_w, ffn_conv_b, ffn_w_down, ffn_norm_post, loss_target, m_mix_norm_pre, m_w_in, m_cv_w, m_cv_b, m_cv_ln_g, m_cv_ln_b, m_cv_pw_w, m_cv_pw_b, m_w_out, m_mix_norm_post, m_x_norm_pre, m_mem_norm, m_x_wq, m_x_wk, m_x_wv, m_x_wo, m_x_norm_post, m_ffn_norm_pre, m_ffn_w_up, m_ffn_conv_w, m_ffn_conv_b, m_ffn_w_down, m_ffn_norm_post, v_mix_norm_pre, v_w_in, v_cv_w, v_cv_b, v_cv_ln_g, v_cv_ln_b, v_cv_pw_w, v_cv_pw_b, v_w_out, v_mix_norm_post, v_x_norm_pre, v_mem_norm, v_x_wq, v_x_wk, v_x_wv, v_x_wo, v_x_norm_post, v_ffn_norm_pre, v_ffn_w_up, v_ffn_conv_w, v_ffn_conv_b, v_ffn_w_down, v_ffn_norm_post):
    args = locals()
    wts = {n: args[n] for n in WEIGHTS}
    mom = {n: args["m_" + n] for n in WEIGHTS}
    var = {n: args["v_" + n] for n in WEIGHTS}

    pieces = []
    for n, r, _ in PACKED:
        for l in range(DEPTH):
            w = wts[n][l].astype(BF16)
            pieces.append(w.T if n in TRANSPOSED else w)
    pieces.append(wts["cv_pw_w"].astype(BF16).reshape(PW_ROWS, PAYLOAD_COLS))
    gathered = _all_gather(jnp.concatenate(pieces, axis=0), "weights_all_gather")
    small_payload = _pad_rows(jnp.concatenate([wts[n].reshape(-1) for n in SMALL_SHARDED]), PAYLOAD_COLS)
    small = _all_gather(small_payload, "small_weights_all_gather").reshape(N_DEV, -1)
    full = {}
    half = N_DEV // 2
    for n, r, first in PACKED:
        blocks = [gathered[:, first + l * r:first + (l + 1) * r, :] for l in range(DEPTH)]
        if n == "ffn_w_up":
            full[n] = [(b[:half].reshape(half * r, PAYLOAD_COLS), b[half:].reshape(half * r, PAYLOAD_COLS)) for b in blocks]
        else:
            full[n] = [b.reshape(N_DEV * r, PAYLOAD_COLS) for b in blocks]
    full["cv_pw_w"] = _unshard("cv_pw_w", gathered[:, PW_FIRST_ROW:, :].reshape((N_DEV,) + wts["cv_pw_w"].shape))
    off = 0
    for n in SMALL_SHARDED:
        size = wts[n].size
        full[n] = _unshard(n, small[:, off:off + size].reshape((N_DEV,) + wts[n].shape))
        off += size
    for n in REPLICATED:
        full[n] = wts[n]
    loss_part, grad_x, grads = _local_step(x[0], mem[0], positions[0], loss_target[0], full)
    loss = lax.psum(loss_part, ("x", "y", "c"))

    pieces = []
    for n, r, _ in PACKED:
        for l in range(DEPTH):
            parts = grads[n][l] if isinstance(grads[n][l], tuple) else (grads[n][l],)
            pieces.append(jnp.concatenate([g.reshape(-1, r, PAYLOAD_COLS) for g in parts], axis=0))
    pieces.append(_to_shards("cv_pw_w", jnp.stack(grads["cv_pw_w"])).reshape(N_DEV, PW_ROWS, PAYLOAD_COLS))
    big_rows = jnp.concatenate(pieces, axis=1)
    small_rows = jnp.concatenate([_to_shards(n, jnp.stack(grads[n])) for n in SMALL_SHARDED], axis=1)
    rep_flat = jnp.concatenate([jnp.stack([g.reshape(-1) for g in grads[n]]).reshape(-1) for n in REPLICATED])
    rep_rows = jnp.broadcast_to(rep_flat[None], (N_DEV, rep_flat.shape[0]))
    f32_rows = _pad_rows(jnp.concatenate([small_rows, rep_rows], axis=1), PAYLOAD_COLS)
    received = _all_to_all(big_rows, "grads_all_to_all")
    small_parts = _all_to_all(f32_rows, "small_grads_all_to_all")

    res = {}
    for n, r, first in PACKED:
        shape = wts[n].shape
        two_d = (shape[0] * shape[1], shape[2])
        if n in TRANSPOSED:
            parts = received[:, first:first + DEPTH * r, :].reshape(N_DEV, DEPTH, r, PAYLOAD_COLS)
            parts, first, t = parts.transpose(0, 1, 3, 2).reshape((N_DEV,) + two_d), 0, None
        else:
            parts, t = received, r
        outs = _adamw(parts, wts[n].reshape(two_d), mom[n].reshape(two_d), var[n].reshape(two_d), f"adamw_{n}", first, t)
        res[n] = [o.reshape(shape) for o in outs]
    shape = wts["cv_pw_w"].shape
    two_d = (shape[0] * shape[1], shape[2])
    parts = received[:, PW_FIRST_ROW:, :].reshape((N_DEV,) + two_d)
    outs = _adamw(parts, wts["cv_pw_w"].reshape(two_d), mom["cv_pw_w"].reshape(two_d), var["cv_pw_w"].reshape(two_d),
                  "adamw_cv_pw_w")
    res["cv_pw_w"] = [o.reshape(shape) for o in outs]
    small_names = SMALL_SHARDED + REPLICATED
    flat_w = _pad_rows(jnp.concatenate([wts[n].reshape(-1) for n in small_names]), PAYLOAD_COLS)
    flat_m = _pad_rows(jnp.concatenate([mom[n].reshape(-1) for n in small_names]), PAYLOAD_COLS)
    flat_v = _pad_rows(jnp.concatenate([var[n].reshape(-1) for n in small_names]), PAYLOAD_COLS)
    outs = _adamw(small_parts, flat_w, flat_m, flat_v, "adamw_small")
    outs = [o.reshape(-1) for o in outs]
    off = 0
    for n in small_names:
        size = wts[n].size
        res[n] = [o[off:off + size].reshape(wts[n].shape) for o in outs]
        off += size

    result = [loss, grad_x[None]]
    for kind in range(4):
        result += [res[n][kind] for n in WEIGHTS]
    return tuple(result)


def _local_step(x2, mem2, positions, target, full):
    def layer_params(l):
        p = {n: full[n][l] for n in BIG + SMALL_SHARDED}
        p.update({n: full[n][l][None, :] for n in REPLICATED})
        return p

    pos = positions.astype(F32)
    half = HEAD_DIM // 2
    inv_freq = ROPE_THETA ** (-jnp.arange(half, dtype=F32) / half)
    ang = pos[:, None] * inv_freq
    cos = jnp.tile(jnp.cos(ang), (1, LANES // half))
    sin = jnp.tile(jnp.sin(ang), (1, LANES // half))

    params = [layer_params(l) for l in range(DEPTH)]
    h = x2
    hn = _rms_fwd(h, params[0]["mix_norm_pre"], "l0_in_norm")
    saved = []
    for l in range(DEPTH):
        g_next = params[l + 1]["mix_norm_pre"] if l + 1 < DEPTH else None
        h, hn, sv = _layer_fwd(l, h, hn, params[l], mem2, cos, sin, g_next)
        saved.append(sv)
    loss_part, dh = _loss_fwd(h, target, "loss")

    grads = {n: [None] * DEPTH for n in WEIGHTS}
    dh, dy, _, dg = _norm_bwd(dh, None, (saved[-1]["y_ffn"], params[-1]["ffn_norm_post"]), "last_post_bwd")
    grads["ffn_norm_post"][DEPTH - 1] = dg
    for l in reversed(range(DEPTH)):
        p = dict(params[l])
        sv = dict(saved[l])
        if l > 0:
            p["prev_ffn_norm_post"] = params[l - 1]["ffn_norm_post"]
            sv["y_prev"] = saved[l - 1]["y_ffn"]
        dh, dy, gr, dg_prev = _layer_bwd(l, dh, dy, p, sv, mem2, cos, sin, first_layer=(l == 0))
        for n, g in gr.items():
            grads[n][l] = g
        if l > 0:
            grads["ffn_norm_post"][l - 1] = dg_prev
    return loss_part[0, 0], dh, grads
```

```python
import functools
import math

import jax
import jax.numpy as jnp
from jax import lax
from jax.experimental import pallas as pl
from jax.experimental.pallas import tpu as pltpu

F32, BF16 = jnp.float32, jnp.bfloat16
SDS = jax.ShapeDtypeStruct

D_MODEL = 1024
SEQ = 4096
DEPTH = 2
HEAD_DIM = 64
SB_HEADS = 4
SB_WIDTH = 256
CV_WIDTH = 256
CV_KERNEL = 31
DL_HEADS = 8
DL_WIDTH = 512
IN_WIDTH = 2816
DL_PATTERN = ((128, 1), (512, 4), (2048, 16))
BLOCK = 128
ROPE_THETA = 10000.0
N_MEM = 256
X_HEADS = 4
X_HEAD_DIM = 256
D_FF = 2816
EPS = 1e-6
N_DEV = 8
LANES = 128

ADAM_LR = 0.001
ADAM_B1 = 0.9
ADAM_B2 = 0.999
ADAM_EPS = 1e-08
ADAM_WD = 0.01
ADAM_STEP = 10

VMEM_LIMIT_BYTES = 56 * 1024 * 1024
MESH = pl.DeviceIdType.MESH
NEG = -1e30


def _params(**kw):
    return pltpu.CompilerParams(vmem_limit_bytes=VMEM_LIMIT_BYTES, **kw)


def _pick(n, cands):
    for c in cands:
        if n % c == 0:
            return c
    return n


def _mm(a, b, mode, out_dtype, name, bias=None):
    if mode == "nn":
        (m, k), (k2, n) = a.shape, b.shape
    elif mode == "nt":
        (m, k), (n, k2) = a.shape, b.shape
    else:
        (k, m), (k2, n) = a.shape, b.shape
    assert k == k2, (a.shape, b.shape, mode)
    tm = _pick(m, (1024, 1408, 512, 256, 128))
    tn = _pick(n, (1024, 1408, 512, 256, 128))
    tk = k if k <= 2048 else _pick(k, (2048, 1408, 1024, 512))
    nk = k // tk
    dims = {"nn": ((1,), (0,)), "nt": ((1,), (1,)), "tn": ((0,), (0,))}[mode]

    def body(*refs):
        refs = list(refs)
        acc_ref = refs.pop() if nk > 1 else None
        a_ref, b_ref = refs[0], refs[1]
        bias_ref = refs[2] if bias is not None else None
        o_ref = refs[-1]
        p = lax.dot_general(a_ref[...].astype(BF16), b_ref[...].astype(BF16), (dims, ((), ())),
                            preferred_element_type=F32)

        def finish(v):
            if bias_ref is not None:
                v = v + bias_ref[...]
            o_ref[...] = v.astype(out_dtype)

        if nk == 1:
            finish(p)
        else:
            kk = pl.program_id(2)

            @pl.when(kk == 0)
            def _():
                acc_ref[...] = p

            @pl.when(kk > 0)
            def _():
                acc_ref[...] += p

            @pl.when(kk == nk - 1)
            def _():
                finish(acc_ref[...])

    a_spec = pl.BlockSpec((tk, tm), lambda i, j, kk: (kk, i)) if mode == "tn" else pl.BlockSpec((tm, tk), lambda i, j, kk: (i, kk))
    b_spec = pl.BlockSpec((tn, tk), lambda i, j, kk: (j, kk)) if mode == "nt" else pl.BlockSpec((tk, tn), lambda i, j, kk: (kk, j))
    in_specs = [a_spec, b_spec]
    args = [a, b]
    if bias is not None:
        in_specs.append(pl.BlockSpec((1, tn), lambda i, j, kk: (0, j)))
        args.append(bias)
    return pl.pallas_call(
        body, name=name, out_shape=SDS((m, n), out_dtype), grid=(m // tm, n // tn, nk),
        in_specs=in_specs, out_specs=pl.BlockSpec((tm, tn), lambda i, j, kk: (i, j)),
        scratch_shapes=[pltpu.VMEM((tm, tn), F32)] if nk > 1 else [], compiler_params=_params(),
    )(*args)


def _rms(x, g):
    r = lax.rsqrt(jnp.mean(x * x, axis=-1, keepdims=True) + EPS)
    return x * r * g


def _rms_bwd(x, g, dy):
    r = lax.rsqrt(jnp.mean(x * x, axis=-1, keepdims=True) + EPS)
    xh = x * r
    dyg = dy * g
    dx = r * (dyg - xh * jnp.mean(dyg * xh, axis=-1, keepdims=True))
    return dx, dy * xh


def _rms_fwd(x, g, name):
    rows, d = x.shape
    t = min(rows, 512)

    def body(x_ref, g_ref, o_ref):
        o_ref[...] = _rms(x_ref[...], g_ref[...]).astype(BF16)

    return pl.pallas_call(
        body, name=name, out_shape=SDS((rows, d), BF16), grid=(rows // t,),
        in_specs=[pl.BlockSpec((t, d), lambda i: (i, 0)), pl.BlockSpec((1, d), lambda i: (0, 0))],
        out_specs=pl.BlockSpec((t, d), lambda i: (i, 0)), compiler_params=_params(),
    )(x, g)


def _res_norm_fwd(h, y, g_post, g_next, name):
    rows, d = h.shape
    t = 512
    has_next = g_next is not None

    def body(*refs):
        if has_next:
            h_ref, y_ref, gp_ref, gn_ref, h1_ref, hn_ref = refs
        else:
            h_ref, y_ref, gp_ref, h1_ref = refs
        h1 = h_ref[...] + _rms(y_ref[...], gp_ref[...])
        h1_ref[...] = h1
        if has_next:
            hn_ref[...] = _rms(h1, gn_ref[...]).astype(BF16)

    row = pl.BlockSpec((t, d), lambda i: (i, 0))
    vec = pl.BlockSpec((1, d), lambda i: (0, 0))
    in_specs = [row, row, vec] + ([vec] if has_next else [])
    args = [h, y, g_post] + ([g_next] if has_next else [])
    out_shape = [SDS((rows, d), F32)] + ([SDS((rows, d), BF16)] if has_next else [])
    out_specs = [row] + ([row] if has_next else [])
    res = pl.pallas_call(body, name=name, out_shape=out_shape, grid=(rows // t,), in_specs=in_specs,
                         out_specs=out_specs, compiler_params=_params())(*args)
    return (res[0], res[1]) if has_next else (res[0], None)


def _norm_bwd(dh, pre, post, name):
    rows, d = dh.shape
    t = 512
    has_pre, has_post = pre is not None, post is not None
    if has_pre:
        dhns = pre[2] if isinstance(pre[2], tuple) else (pre[2],)
        pre = (pre[0], pre[1]) + dhns

    def body(*refs):
        refs = list(refs)
        dh_ref = refs.pop(0)
        if has_pre:
            h_ref, gpre_ref = refs.pop(0), refs.pop(0)
            dhn_refs = [refs.pop(0) for _ in dhns]
        if has_post:
            y_ref, gpost_ref = refs.pop(0), refs.pop(0)
        dht_ref = refs.pop(0)
        if has_post:
            dy_ref = refs.pop(0)
        if has_pre:
            dgpre_ref = refs.pop(0)
        if has_post:
            dgpost_ref = refs.pop(0)
        i = pl.program_id(0)
        dht = dh_ref[...]
        if has_pre:
            dhn = dhn_refs[0][...]
            for r in dhn_refs[1:]:
                dhn = dhn + r[...]
            dx, dgr = _rms_bwd(h_ref[...], gpre_ref[...], dhn)
            dht = dht + dx

            @pl.when(i == 0)
            def _():
                dgpre_ref[...] = jnp.zeros_like(dgpre_ref)

            dgpre_ref[...] += jnp.sum(dgr, axis=0, keepdims=True)
        dht_ref[...] = dht
        if has_post:
            dy, dgr = _rms_bwd(y_ref[...], gpost_ref[...], dht)
            dy_ref[...] = dy.astype(BF16)

            @pl.when(i == 0)
            def _():
                dgpost_ref[...] = jnp.zeros_like(dgpost_ref)

            dgpost_ref[...] += jnp.sum(dgr, axis=0, keepdims=True)

    row = pl.BlockSpec((t, d), lambda i: (i, 0))
    vec = pl.BlockSpec((1, d), lambda i: (0, 0))
    in_specs, args = [row], [dh]
    if has_pre:
        in_specs += [row, vec] + [row] * len(dhns)
        args += list(pre)
    if has_post:
        in_specs += [row, vec]
        args += list(post)
    out_shape, out_specs = [SDS((rows, d), F32)], [row]
    if has_post:
        out_shape.append(SDS((rows, d), BF16))
        out_specs.append(row)
    if has_pre:
        out_shape.append(SDS((1, d), F32))
        out_specs.append(vec)
    if has_post:
        out_shape.append(SDS((1, d), F32))
        out_specs.append(vec)
    res = list(pl.pallas_call(body, name=name, out_shape=out_shape, grid=(rows // t,), in_specs=in_specs,
                              out_specs=out_specs, compiler_params=_params())(*args))
    dht = res.pop(0)
    dy = res.pop(0) if has_post else None
    dgpre = res.pop(0) if has_pre else None
    dgpost = res.pop(0) if has_post else None
    return dht, dy, dgpre, dgpost


def _rms_gain_grad(x, g, dy, name):
    rows, d = x.shape

    def body(x_ref, g_ref, dy_ref, dg_ref):
        _, dgr = _rms_bwd(x_ref[...], g_ref[...], dy_ref[...])
        dg_ref[...] = jnp.sum(dgr, axis=0, keepdims=True)

    return pl.pallas_call(body, name=name, out_shape=SDS((1, d), F32), compiler_params=_params())(x, g, dy)


def _loss_fwd(h, target, name):
    rows, d = h.shape
    t = 512

    def body(h_ref, t_ref, loss_ref, dh_ref):
        i = pl.program_id(0)
        err = h_ref[...] - t_ref[...]
        dh_ref[...] = err * (1.0 / d)

        @pl.when(i == 0)
        def _():
            loss_ref[...] = jnp.zeros_like(loss_ref)

        part = jnp.sum(jnp.sum(err * err, axis=1, keepdims=True), axis=0, keepdims=True) * (0.5 / d)
        loss_ref[...] += jnp.broadcast_to(part, loss_ref.shape)

    row = pl.BlockSpec((t, d), lambda i: (i, 0))
    return pl.pallas_call(
        body, name=name, out_shape=(SDS((1, LANES), F32), SDS((rows, d), F32)), grid=(rows // t,),
        in_specs=[row, row], out_specs=(pl.BlockSpec((1, LANES), lambda i: (0, 0)), row), compiler_params=_params(),
    )(h, target)


def _rot_half(x, sign):
    w = x.shape[-1]
    lane = lax.broadcasted_iota(jnp.int32, x.shape, 1)
    first = (lane % HEAD_DIM) < (HEAD_DIM // 2)
    return jnp.where(first, -sign * pltpu.roll(x, w - HEAD_DIM // 2, axis=1), sign * pltpu.roll(x, HEAD_DIM // 2, axis=1))


def _rope_fwd(u, cos, sin, name):
    rows = u.shape[0]
    t, cw = 512, 256
    first_col = (3 * SB_WIDTH + 2 * CV_WIDTH) // cw

    def body(u_ref, c_ref, s_ref, o_ref):
        x = u_ref[...]
        c = jnp.tile(c_ref[...], (1, cw // LANES))
        s = jnp.tile(s_ref[...], (1, cw // LANES))
        o_ref[...] = (x * c + _rot_half(x, 1.0) * s).astype(BF16)

    tab = pl.BlockSpec((t, LANES), lambda i, j: (i, 0))
    return pl.pallas_call(
        body, name=name, out_shape=SDS((rows, 2 * DL_WIDTH), BF16), grid=(rows // t, 2 * DL_WIDTH // cw),
        in_specs=[pl.BlockSpec((t, cw), lambda i, j: (i, first_col + j)), tab, tab],
        out_specs=pl.BlockSpec((t, cw), lambda i, j: (i, j)), compiler_params=_params(),
    )(u, cos, sin)


def _rope_bwd(dqs, dks, dvs, cos, sin, name):
    rows = dqs[0].shape[0]
    t, w = 256, DL_WIDTH

    def body(*refs):
        c = jnp.tile(refs[9][...], (1, w // LANES))
        s = jnp.tile(refs[10][...], (1, w // LANES))
        o_ref = refs[11]
        dq = refs[0][...] + refs[1][...] + refs[2][...]
        dk = refs[3][...] + refs[4][...] + refs[5][...]
        dv = refs[6][...] + refs[7][...] + refs[8][...]
        o_ref[:, 0:w] = (dq * c + _rot_half(dq, -1.0) * s).astype(BF16)
        o_ref[:, w:2 * w] = (dk * c + _rot_half(dk, -1.0) * s).astype(BF16)
        o_ref[:, 2 * w:3 * w] = dv.astype(BF16)

    row = pl.BlockSpec((t, w), lambda i: (i, 0))
    tab = pl.BlockSpec((t, LANES), lambda i: (i, 0))
    return pl.pallas_call(
        body, name=name, out_shape=SDS((rows, 3 * w), BF16), grid=(rows // t,), in_specs=[row] * 9 + [tab, tab],
        out_specs=pl.BlockSpec((t, 3 * w), lambda i: (i, 0)), compiler_params=_params(),
    )(*dqs, *dks, *dvs, cos, sin)


SB_TILE = 256


def _softplus(z):
    return jnp.maximum(z, 0.0) + jnp.log(1.0 + jnp.exp(-jnp.abs(z)))


def _split_dot(x, tri, passes):
    acc = None
    rem = x
    for _ in range(passes):
        part = rem.astype(BF16)
        rem = rem - part.astype(F32)
        d = jnp.dot(part, tri, preferred_element_type=F32)
        acc = d if acc is None else acc + d
    return acc


def _tri(t, rel):
    j = lax.broadcasted_iota(jnp.int32, (t, t), 0)
    s = lax.broadcasted_iota(jnp.int32, (t, t), 1)
    return rel(j, s).astype(BF16)


def _sb_fwd(q, k, v, name):
    h, s_len, hd = q.shape
    t = SB_TILE
    nq = s_len // t
    scale = hd ** -0.5

    def body(q_ref, k_ref, v_ref, o_ref, tot_ref):
        i = pl.program_id(1)
        qv = q_ref[0] * scale
        upper = _tri(t, lambda j, s: j > s)

        def tiles(js, carry, diagonal):
            acc, run = carry
            starts = [pl.multiple_of(j * t, t) for j in js]
            zs = [lax.dot_general(qv, k_ref[0, pl.ds(st, t), :], (((1,), (1,)), ((), ())), preferred_element_type=F32)
                  for st in starts]
            sps = [_softplus(z) for z in zs]
            if diagonal:
                mask = lax.broadcasted_iota(jnp.int32, (t, t), 1) < lax.broadcasted_iota(jnp.int32, (t, t), 0)
                sps = [jnp.where(mask, sp, 0.0) for sp in sps]
            laters = [_split_dot(sp, upper, 2) for sp in sps]
            for st, z, sp, later in zip(starts, zs, sps, laters):
                a = jnp.exp((z - sp) - (run + later))
                if diagonal:
                    a = jnp.where(mask, a, 0.0)
                acc = acc + jnp.dot(a.astype(BF16), v_ref[0, pl.ds(st, t), :], preferred_element_type=F32)
                run = run + jnp.sum(sp, axis=1, keepdims=True)
            return acc, run

        def pair(pp, carry):
            j = i - 1 - 2 * pp
            return tiles([j, j - 1], carry, False)

        carry = tiles([i], (jnp.zeros((t, hd), F32), jnp.zeros((t, 1), F32)), True)
        carry = lax.fori_loop(0, i // 2, pair, carry)
        acc, run = lax.fori_loop(0, i % 2, lambda _, c: tiles([0], c, False), carry)
        o_ref[0] = acc.astype(BF16)
        tot_ref[0] = run

    full = pl.BlockSpec((1, s_len, hd), lambda hh, i: (hh, 0, 0))
    tile = pl.BlockSpec((1, t, hd), lambda hh, i: (hh, i, 0))
    return pl.pallas_call(
        body, name=name, out_shape=(SDS((h, s_len, hd), BF16), SDS((h, s_len, 1), F32)), grid=(h, nq),
        in_specs=[tile, full, full], out_specs=(tile, pl.BlockSpec((1, t, 1), lambda hh, i: (hh, i, 0))),
        compiler_params=_params(),
    )(q, k, v)


def _sb_bwd(q, k, v, do, tot, name):
    h, s_len, hd = q.shape
    t = SB_TILE
    nq = s_len // t
    scale = hd ** -0.5

    def body(q_ref, k_ref, v_ref, do_ref, tot_ref, dq_ref, dk_ref, dv_ref, dk_acc, dv_acc):
        i = pl.program_id(1)

        @pl.when(i == 0)
        def _():
            dk_acc[...] = jnp.zeros_like(dk_acc)
            dv_acc[...] = jnp.zeros_like(dv_acc)

        qv = q_ref[0] * scale
        dov = do_ref[0]
        total = tot_ref[0]
        upto = _tri(t, lambda j, s: j <= s)
        before = _tri(t, lambda j, s: j < s)
        nt_dims = (((1,), (1,)), ((), ()))
        tn_dims = (((0,), (0,)), ((), ()))

        def tiles(js, carry, diagonal):
            dq, run_sp, run_g = carry
            starts = [pl.multiple_of(j * t, t) for j in js]
            zs = [lax.dot_general(qv, k_ref[0, pl.ds(st, t), :], nt_dims, preferred_element_type=F32) for st in starts]
            das = [lax.dot_general(dov, v_ref[0, pl.ds(st, t), :], nt_dims, preferred_element_type=F32) for st in starts]
            sps = [_softplus(z) for z in zs]
            log_sigs = [z - sp for z, sp in zip(zs, sps)]
            if diagonal:
                mask = lax.broadcasted_iota(jnp.int32, (t, t), 1) < lax.broadcasted_iota(jnp.int32, (t, t), 0)
                sps = [jnp.where(mask, sp, 0.0) for sp in sps]
            pres = [_split_dot(sp, upto, 2) for sp in sps]
            a_s, gs = [], []
            for sp, log_sig, pre, da in zip(sps, log_sigs, pres, das):
                a = jnp.exp(log_sig - (total - (run_sp + pre)))
                if diagonal:
                    a = jnp.where(mask, a, 0.0)
                a_s.append(a)
                gs.append(a * da)
                run_sp = run_sp + jnp.sum(sp, axis=1, keepdims=True)
            g_pres = [_split_dot(g, before, 3) for g in gs]
            for st, a, g, g_pre, log_sig in zip(starts, a_s, gs, g_pres, log_sigs):
                sig = jnp.exp(log_sig)
                dz = g * (1.0 - sig) - sig * (run_g + g_pre)
                if diagonal:
                    dz = jnp.where(mask, dz, 0.0)
                dz = dz.astype(BF16)
                dq = dq + jnp.dot(dz, k_ref[0, pl.ds(st, t), :], preferred_element_type=F32)
                dk_acc[pl.ds(st, t), :] += lax.dot_general(dz, qv, tn_dims, preferred_element_type=F32)
                dv_acc[pl.ds(st, t), :] += lax.dot_general(a.astype(BF16), dov, tn_dims, preferred_element_type=F32)
                run_g = run_g + jnp.sum(g, axis=1, keepdims=True)
            return dq, run_sp, run_g

        zero = jnp.zeros((t, 1), F32)
        carry = lax.fori_loop(0, i // 2, lambda pp, c: tiles([2 * pp, 2 * pp + 1], c, False), (jnp.zeros((t, hd), F32), zero, zero))
        carry = lax.fori_loop(0, i % 2, lambda _, c: tiles([i - 1], c, False), carry)
        dq, _, _ = tiles([i], carry, True)
        dq_ref[0] = (dq * scale).astype(BF16)

        @pl.when(i == nq - 1)
        def _():
            dk_ref[0] = dk_acc[...].astype(BF16)
            dv_ref[0] = dv_acc[...].astype(BF16)

    full = pl.BlockSpec((1, s_len, hd), lambda hh, i: (hh, 0, 0))
    tile = pl.BlockSpec((1, t, hd), lambda hh, i: (hh, i, 0))
    out = SDS((h, s_len, hd), BF16)
    return pl.pallas_call(
        body, name=name, out_shape=(out, out, out), grid=(h, nq),
        in_specs=[tile, full, full, tile, pl.BlockSpec((1, t, 1), lambda hh, i: (hh, i, 0))],
        out_specs=(tile, full, full), scratch_shapes=[pltpu.VMEM((s_len, hd), F32), pltpu.VMEM((s_len, hd), F32)],
        compiler_params=_params(),
    )(q, k, v, do, tot)


def _dl_scores(qv, kk, n):
    s = lax.dot_general(qv, kk, (((1,), (1,)), ((), ())), preferred_element_type=F32) * (HEAD_DIM ** -0.5)
    r = lax.broadcasted_iota(jnp.int32, s.shape, 0)
    c = lax.broadcasted_iota(jnp.int32, s.shape, 1)
    valid = (c >= r) & (c - r <= BLOCK) & ((n > 0) | (c >= BLOCK))
    return jnp.where(valid, s, NEG)


DL_BLOCKS_PER_STEP = SEQ // BLOCK
DL_UNROLL = 4


def _dl_window(ref, gi, n):
    prev = ref[gi, pl.ds(pl.multiple_of(jnp.maximum(n - 1, 0) * BLOCK, BLOCK), BLOCK), :]
    cur = ref[gi, pl.ds(pl.multiple_of(n * BLOCK, BLOCK), BLOCK), :]
    return jnp.concatenate([prev, cur], axis=0)


def _dl_fwd(q, k, v, name):
    g, l, hd = q.shape
    nb = l // BLOCK
    gb = DL_BLOCKS_PER_STEP // nb

    def body(q_ref, k_ref, v_ref, o_ref, lse_ref):
        def step(idx, _):
            gi, n = idx // nb, idx % nb
            rows = pl.ds(pl.multiple_of(n * BLOCK, BLOCK), BLOCK)
            s = _dl_scores(q_ref[gi, rows, :], _dl_window(k_ref, gi, n), n)
            m = jnp.max(s, axis=-1, keepdims=True)
            p = jnp.exp(s - m)
            den = jnp.sum(p, axis=-1, keepdims=True)
            o_ref[gi, rows, :] = jnp.dot((p / den).astype(BF16), _dl_window(v_ref, gi, n), preferred_element_type=F32)
            lse_ref[gi, rows, :] = jnp.broadcast_to(m + jnp.log(den), (BLOCK, hd))
            return 0

        lax.fori_loop(0, gb * nb, step, 0, unroll=DL_UNROLL)

    spec = pl.BlockSpec((gb, l, hd), lambda i: (i, 0, 0))
    out = SDS((g, l, hd), F32)
    return pl.pallas_call(body, name=name, out_shape=(out, out), grid=(g // gb,), in_specs=[spec] * 3, out_specs=(spec, spec),
                          compiler_params=_params())(q, k, v)


def _dl_bwd(q, k, v, do, o_mix, wt, lse, name):
    g, l, hd = q.shape
    nb = l // BLOCK
    gb = DL_BLOCKS_PER_STEP // nb
    scale = HEAD_DIM ** -0.5

    def body(q_ref, k_ref, v_ref, do_ref, om_ref, wt_ref, lse_ref, dq_ref, dk_ref, dv_ref):
        dk_ref[...] = jnp.zeros_like(dk_ref)
        dv_ref[...] = jnp.zeros_like(dv_ref)

        def step(idx, _):
            gi, n = idx // nb, idx % nb
            rows = pl.ds(pl.multiple_of(n * BLOCK, BLOCK), BLOCK)
            prev = pl.ds(pl.multiple_of(jnp.maximum(n - 1, 0) * BLOCK, BLOCK), BLOCK)
            qv = q_ref[gi, rows, :]
            kk = _dl_window(k_ref, gi, n)
            vv = _dl_window(v_ref, gi, n)
            s = _dl_scores(qv, kk, n)
            p = jnp.exp(s - jnp.max(lse_ref[gi, rows, :], axis=-1, keepdims=True))
            dov = do_ref[gi, rows, :]
            w = jnp.max(wt_ref[gi, rows, :], axis=-1, keepdims=True)
            d_all = jnp.sum(dov * om_ref[gi, rows, :], axis=-1, keepdims=True)
            do_n = (dov * w).astype(BF16)
            dp = lax.dot_general(do_n, vv, (((1,), (1,)), ((), ())), preferred_element_type=F32)
            ds = (p * (dp - w * d_all) * scale).astype(BF16)
            dq_ref[gi, rows, :] = jnp.dot(ds, kk, preferred_element_type=F32)
            dkk = lax.dot_general(ds, qv, (((0,), (0,)), ((), ())), preferred_element_type=F32)
            dvv = lax.dot_general(p.astype(BF16), do_n, (((0,), (0,)), ((), ())), preferred_element_type=F32)
            dk_ref[gi, prev, :] += dkk[:BLOCK]
            dv_ref[gi, prev, :] += dvv[:BLOCK]
            dk_ref[gi, rows, :] += dkk[BLOCK:]
            dv_ref[gi, rows, :] += dvv[BLOCK:]
            return 0

        lax.fori_loop(0, gb * nb, step, 0, unroll=DL_UNROLL)

    spec = pl.BlockSpec((gb, l, hd), lambda i: (i, 0, 0))
    out = SDS((g, l, hd), F32)
    return pl.pallas_call(body, name=name, out_shape=(out, out, out), grid=(g // gb,), in_specs=[spec] * 7,
                          out_specs=(spec, spec, spec), compiler_params=_params())(q, k, v, do, o_mix, wt, lse)


def _dl_mix_fwd(outs, lses, name):
    rows, w = outs[0].shape
    t = 256

    def body(o1, o2, o3, l1, l2, l3, ob_ref, of_ref, w1, w2, w3):
        a, b, c = l1[...], l2[...], l3[...]
        m = jnp.maximum(jnp.maximum(a, b), c)
        ea, eb, ec = jnp.exp(a - m), jnp.exp(b - m), jnp.exp(c - m)
        den = ea + eb + ec
        wa, wb, wc = ea / den, eb / den, ec / den
        o = wa * o1[...] + wb * o2[...] + wc * o3[...]
        ob_ref[...] = o.astype(BF16)
        of_ref[...] = o
        w1[...] = wa
        w2[...] = wb
        w3[...] = wc

    row = pl.BlockSpec((t, w), lambda i: (i, 0))
    f = SDS((rows, w), F32)
    return pl.pallas_call(body, name=name, out_shape=(SDS((rows, w), BF16), f, f, f, f), grid=(rows // t,),
                          in_specs=[row] * 6, out_specs=(row,) * 5, compiler_params=_params())(*outs, *lses)


def _x_probs(qh, kh):
    s = lax.dot_general(qh, kh, (((1,), (1,)), ((), ())), preferred_element_type=F32) * (X_HEAD_DIM ** -0.5)
    e = jnp.exp(s - jnp.max(s, axis=-1, keepdims=True))
    return e / jnp.sum(e, axis=-1, keepdims=True)


def _xattn_fwd(q, k, v, name):
    rows, d = q.shape
    t = 512

    def body(q_ref, k_ref, v_ref, o_ref):
        for hh in range(X_HEADS):
            cols = slice(hh * X_HEAD_DIM, (hh + 1) * X_HEAD_DIM)
            p = _x_probs(q_ref[:, cols], k_ref[:, cols])
            o_ref[:, cols] = jnp.dot(p.astype(BF16), v_ref[:, cols], preferred_element_type=F32).astype(BF16)

    row = pl.BlockSpec((t, d), lambda i: (i, 0))
    mem = pl.BlockSpec((N_MEM, d), lambda i: (0, 0))
    return pl.pallas_call(body, name=name, out_shape=SDS((rows, d), BF16), grid=(rows // t,), in_specs=[row, mem, mem],
                          out_specs=row, compiler_params=_params())(q, k, v)


def _xattn_bwd(q, k, v, do, name):
    rows, d = q.shape
    t = 512
    scale = X_HEAD_DIM ** -0.5

    def body(q_ref, k_ref, v_ref, do_ref, dq_ref, dk_ref, dv_ref):
        @pl.when(pl.program_id(0) == 0)
        def _():
            dk_ref[...] = jnp.zeros_like(dk_ref)
            dv_ref[...] = jnp.zeros_like(dv_ref)

        for hh in range(X_HEADS):
            cols = slice(hh * X_HEAD_DIM, (hh + 1) * X_HEAD_DIM)
            qh, kh, vh, doh = q_ref[:, cols], k_ref[:, cols], v_ref[:, cols], do_ref[:, cols]
            p = _x_probs(qh, kh)
            dp = lax.dot_general(doh, vh, (((1,), (1,)), ((), ())), preferred_element_type=F32)
            ds = (p * (dp - jnp.sum(p * dp, axis=-1, keepdims=True)) * scale).astype(BF16)
            dq_ref[:, cols] = jnp.dot(ds, kh, preferred_element_type=F32).astype(BF16)
            dk_ref[:, cols] += lax.dot_general(ds, qh, (((0,), (0,)), ((), ())), preferred_element_type=F32)
            dv_ref[:, cols] += lax.dot_general(p.astype(BF16), doh, (((0,), (0,)), ((), ())), preferred_element_type=F32)

    row = pl.BlockSpec((t, d), lambda i: (i, 0))
    mem = pl.BlockSpec((N_MEM, d), lambda i: (0, 0))
    return pl.pallas_call(
        body, name=name, out_shape=(SDS((rows, d), BF16), SDS((N_MEM, d), F32), SDS((N_MEM, d), F32)), grid=(rows // t,),
        in_specs=[row, mem, mem, row], out_specs=(row, mem, mem), compiler_params=_params(),
    )(q, k, v, do)


CV_TILE = 256
CV_HALO = 32
CV_LEAD = CV_HALO - (CV_KERNEL - 1)


def _shifted(win, off, rows):
    n = win.shape[0]
    return pltpu.roll(win, (n - off) % n, axis=0)[:rows]


def _glu(val, gate):
    return val * jax.nn.sigmoid(gate)


def _ln_parts(c):
    mu = jnp.mean(c, axis=-1, keepdims=True)
    xc = c - mu
    rstd = lax.rsqrt(jnp.mean(xc * xc, axis=-1, keepdims=True) + EPS)
    return xc * rstd, rstd


def _cv_fwd(u, cv_w, cv_b, ln_g, ln_b, name):
    rows = u.shape[0]
    t, w = CV_TILE, CV_WIDTH
    val_col = 3 * SB_WIDTH // w
    ratio = t // CV_HALO

    def body(val_ref, gate_ref, pval_ref, pgate_ref, w_ref, b_ref, g_ref, beta_ref, s_ref, c_ref):
        i = pl.program_id(0)
        hist = jnp.where(i > 0, _glu(pval_ref[...], pgate_ref[...]), 0.0)
        win = jnp.concatenate([hist, _glu(val_ref[...], gate_ref[...])], axis=0)
        acc = jnp.broadcast_to(b_ref[...], (t, w))
        for kk in range(CV_KERNEL):
            acc = acc + _shifted(win, CV_LEAD + kk, t) * w_ref[kk:kk + 1, :]
        c_ref[...] = acc
        n, _ = _ln_parts(acc)
        y = n * g_ref[...] + beta_ref[...]
        s_ref[...] = (y * jax.nn.sigmoid(y)).astype(BF16)

    cur = lambda col: pl.BlockSpec((t, w), lambda i: (i, col))
    prev = lambda col: pl.BlockSpec((CV_HALO, w), lambda i: (jnp.maximum(i * ratio - 1, 0), col))
    vec = pl.BlockSpec((1, w), lambda i: (0, 0))
    return pl.pallas_call(
        body, name=name, out_shape=(SDS((rows, w), BF16), SDS((rows, w), F32)), grid=(rows // t,),
        in_specs=[cur(val_col), cur(val_col + 1), prev(val_col), prev(val_col + 1),
                  pl.BlockSpec((CV_KERNEL, w), lambda i: (0, 0)), vec, vec, vec],
        out_specs=(pl.BlockSpec((t, w), lambda i: (i, 0)),) * 2, compiler_params=_params(),
    )(u, u, u, u, cv_w, cv_b, ln_g, ln_b)


def _cv_bwd(u, c, ds, db_out, cv_w, ln_g, ln_b, name):
    rows = u.shape[0]
    t, w = CV_TILE, CV_WIDTH
    val_col = 3 * SB_WIDTH // w
    ratio = t // CV_HALO
    nt = rows // t

    def conv_out_grad(c_v, ds_v, g_v, beta_v):
        n, rstd = _ln_parts(c_v)
        y = n * g_v + beta_v
        sig = jax.nn.sigmoid(y)
        dy = ds_v * (sig * (1.0 + y * (1.0 - sig)))
        dn = dy * g_v
        dc = rstd * (dn - jnp.mean(dn, axis=-1, keepdims=True) - n * jnp.mean(dn * n, axis=-1, keepdims=True))
        return dc, dy, n

    def body(val_ref, gate_ref, pval_ref, pgate_ref, c_ref, nc_ref, ds_ref, nds_ref, dbo_ref, w_ref, g_ref, beta_ref,
             dvg_ref, dw_ref, db_ref, dg_ref, dbeta_ref, dpwb_ref):
        i = pl.program_id(0)

        @pl.when(i == 0)
        def _():
            for r in (dw_ref, db_ref, dg_ref, dbeta_ref, dpwb_ref):
                r[...] = jnp.zeros_like(r)

        g_v, beta_v = g_ref[...], beta_ref[...]
        dc, dy, n = conv_out_grad(c_ref[...], ds_ref[...], g_v, beta_v)
        dc_next, _, _ = conv_out_grad(nc_ref[...], nds_ref[...], g_v, beta_v)
        dc_next = jnp.where(i < nt - 1, dc_next, 0.0)
        dg_ref[...] += jnp.sum(dy * n, axis=0, keepdims=True)
        dbeta_ref[...] += jnp.sum(dy, axis=0, keepdims=True)
        db_ref[...] += jnp.sum(dc, axis=0, keepdims=True)
        dpwb_ref[...] += jnp.sum(dbo_ref[...], axis=0, keepdims=True)

        val, gate = val_ref[...], gate_ref[...]
        hist = jnp.where(i > 0, _glu(pval_ref[...], pgate_ref[...]), 0.0)
        win = jnp.concatenate([hist, _glu(val, gate)], axis=0)
        dc_ext = jnp.concatenate([dc, dc_next], axis=0)
        dglu = jnp.zeros((t, w), F32)
        for kk in range(CV_KERNEL):
            dw_ref[kk:kk + 1, :] += jnp.sum(dc * _shifted(win, CV_LEAD + kk, t), axis=0, keepdims=True)
            dglu = dglu + _shifted(dc_ext, CV_KERNEL - 1 - kk, t) * w_ref[kk:kk + 1, :]
        sig = jax.nn.sigmoid(gate)
        dvg_ref[:, 0:w] = (dglu * sig).astype(BF16)
        dvg_ref[:, w:2 * w] = (dglu * val * sig * (1.0 - sig)).astype(BF16)

    cur = lambda col: pl.BlockSpec((t, w), lambda i: (i, col))
    prev = lambda col: pl.BlockSpec((CV_HALO, w), lambda i: (jnp.maximum(i * ratio - 1, 0), col))
    nxt = pl.BlockSpec((CV_HALO, w), lambda i: (jnp.minimum((i + 1) * ratio, rows // CV_HALO - 1), 0))
    vec = pl.BlockSpec((1, w), lambda i: (0, 0))
    return pl.pallas_call(
        body, name=name,
        out_shape=(SDS((rows, 2 * w), BF16), SDS((CV_HALO, w), F32), SDS((1, w), F32), SDS((1, w), F32), SDS((1, w), F32),
                   SDS((1, w), F32)),
        grid=(nt,),
        in_specs=[cur(val_col), cur(val_col + 1), prev(val_col), prev(val_col + 1), cur(0), nxt, cur(0), nxt, cur(0),
                  pl.BlockSpec((CV_KERNEL, w), lambda i: (0, 0)), vec, vec],
        out_specs=(pl.BlockSpec((t, 2 * w), lambda i: (i, 0)), pl.BlockSpec((CV_HALO, w), lambda i: (0, 0)), vec, vec, vec, vec),
        compiler_params=_params(),
    )(u, u, u, u, c, c, ds, ds, db_out, cv_w, ln_g, ln_b)


FFN_TILE = 512
FFN_COLS = 256
FFN_HALO = 8
FFN_KERNEL = 3
N_FF_BLOCKS = D_FF // FFN_COLS


def _conv3(prev8, cur, w_ref, b_ref, first):
    t = cur.shape[0]
    win = jnp.concatenate([jnp.where(first, 0.0, prev8), cur], axis=0)
    return (b_ref[...] + _shifted(win, FFN_HALO - 2, t) * w_ref[0:1, :] + _shifted(win, FFN_HALO - 1, t) * w_ref[1:2, :]
            + cur * w_ref[2:3, :])


def _gelu_gate(gate, val):
    return jax.nn.gelu(gate, approximate=True) * val


def _ffn_specs(t):
    ratio = t // FFN_HALO
    cur = pl.BlockSpec((t, FFN_COLS), lambda j, i: (i, j))
    prev = pl.BlockSpec((FFN_HALO, FFN_COLS), lambda j, i: (jnp.maximum(i * ratio - 1, 0), j))
    wsp = pl.BlockSpec((FFN_KERNEL, FFN_COLS), lambda j, i: (0, j))
    bsp = pl.BlockSpec((1, FFN_COLS), lambda j, i: (0, j))
    return cur, prev, wsp, bsp


def _ffn_act_fwd(up_g, up_v, w_g, w_v, b_g, b_v, name):
    rows = up_g.shape[0]
    t = FFN_TILE
    cur, prev, wsp, bsp = _ffn_specs(t)

    def body(g_ref, v_ref, pg_ref, pv_ref, wg_ref, wv_ref, bg_ref, bv_ref, o_ref):
        first = pl.program_id(1) == 0
        gate = _conv3(pg_ref[...], g_ref[...], wg_ref, bg_ref, first)
        val = _conv3(pv_ref[...], v_ref[...], wv_ref, bv_ref, first)
        o_ref[...] = _gelu_gate(gate, val).astype(BF16)

    return pl.pallas_call(
        body, name=name, out_shape=SDS((rows, D_FF), BF16), grid=(N_FF_BLOCKS, rows // t),
        in_specs=[cur, cur, prev, prev, wsp, wsp, bsp, bsp], out_specs=cur, compiler_params=_params(),
    )(up_g, up_v, up_g, up_v, w_g, w_v, b_g, b_v)


def _ffn_act_bwd(up_g, up_v, dact, w_g, w_v, b_g, b_v, name):
    rows = up_g.shape[0]
    t = FFN_TILE
    te = t + FFN_HALO
    ratio = t // FFN_HALO
    nt = rows // t
    cur, prev, wsp, bsp = _ffn_specs(t)
    nxt = pl.BlockSpec((FFN_HALO, FFN_COLS), lambda j, i: (jnp.minimum((i + 1) * ratio, rows // FFN_HALO - 1), j))

    def conv_ext(pre, x, nx, w_ref, b_ref, first):
        win = jnp.concatenate([jnp.where(first, 0.0, pre), x, nx], axis=0)
        out = (b_ref[...] + _shifted(win, FFN_HALO - 2, te) * w_ref[0:1, :] + _shifted(win, FFN_HALO - 1, te) * w_ref[1:2, :]
               + _shifted(win, FFN_HALO, te) * w_ref[2:3, :])
        return out, win

    def body(g_ref, v_ref, pg_ref, pv_ref, ng_ref, nv_ref, da_ref, nda_ref, wg_ref, wv_ref, bg_ref, bv_ref,
             dug_ref, duv_ref, dwg_ref, dwv_ref, dbg_ref, dbv_ref):
        i = pl.program_id(1)
        first = i == 0
        gate, win_g = conv_ext(pg_ref[...], g_ref[...], ng_ref[...], wg_ref, bg_ref, first)
        val, win_v = conv_ext(pv_ref[...], v_ref[...], nv_ref[...], wv_ref, bv_ref, first)
        da = jnp.concatenate([da_ref[...], jnp.where(i < nt - 1, nda_ref[...], 0.0)], axis=0)
        _, vjp = jax.vjp(_gelu_gate, gate, val)
        dgate, dval = vjp(da)

        @pl.when(first)
        def _():
            for r in (dwg_ref, dwv_ref, dbg_ref, dbv_ref):
                r[...] = jnp.zeros_like(r)

        for dc_ext, win, w_ref, du_ref, dw_ref, db_ref in ((dgate, win_g, wg_ref, dug_ref, dwg_ref, dbg_ref),
                                                          (dval, win_v, wv_ref, duv_ref, dwv_ref, dbv_ref)):
            dc = dc_ext[:t]
            du_ref[...] = (dc * w_ref[2:3, :] + _shifted(dc_ext, 1, t) * w_ref[1:2, :]
                           + _shifted(dc_ext, 2, t) * w_ref[0:1, :]).astype(BF16)
            for kk in range(FFN_KERNEL):
                dw_ref[kk:kk + 1, :] += jnp.sum(dc * _shifted(win, FFN_HALO - 2 + kk, t), axis=0, keepdims=True)
            db_ref[...] += jnp.sum(dc, axis=0, keepdims=True)

    big, wshape, bshape = SDS((rows, D_FF), BF16), SDS((FFN_KERNEL, D_FF), F32), SDS((1, D_FF), F32)
    return pl.pallas_call(
        body, name=name, out_shape=(big, big, wshape, wshape, bshape, bshape), grid=(N_FF_BLOCKS, nt),
        in_specs=[cur, cur, prev, prev, nxt, nxt, cur, nxt, wsp, wsp, bsp, bsp], out_specs=(cur, cur, wsp, wsp, bsp, bsp),
        compiler_params=_params(),
    )(up_g, up_v, up_g, up_v, up_g, up_v, dact, dact, w_g, w_v, b_g, b_v)


def _adamw(parts, w, m, v, name, first_row=0, t=None):
    rows, cols = w.shape
    if t is None:
        t = _pick(rows, (512, 256, 128)) if rows > 512 else rows
    assert first_row % t == 0 and rows % t == 0
    first_block = first_row // t

    def body(p_ref, w_ref, m_ref, v_ref, g_ref, d_ref, nm_ref, nv_ref):
        g = p_ref[0].astype(F32)
        for s in range(1, N_DEV):
            g = g + p_ref[s].astype(F32)
        nm = ADAM_B1 * m_ref[...] + (1.0 - ADAM_B1) * g
        nv = ADAM_B2 * v_ref[...] + (1.0 - ADAM_B2) * jnp.square(g)
        m_hat = nm / (1.0 - ADAM_B1 ** ADAM_STEP)
        v_hat = nv / (1.0 - ADAM_B2 ** ADAM_STEP)
        g_ref[...] = g
        d_ref[...] = -ADAM_LR * (m_hat / (jnp.sqrt(v_hat) + ADAM_EPS) + ADAM_WD * w_ref[...])
        nm_ref[...] = nm
        nv_ref[...] = nv

    row = pl.BlockSpec((t, cols), lambda i: (i, 0))
    out = SDS((rows, cols), F32)
    return pl.pallas_call(
        body, name=name, out_shape=(out,) * 4, grid=(rows // t,),
        in_specs=[pl.BlockSpec((N_DEV, t, cols), lambda i: (0, first_block + i, 0)), row, row, row], out_specs=(row,) * 4,
        compiler_params=_params(),
    )(parts, w, m, v)


def _all_gather(x, name):
    def body(x_ref, out_ref, send_sems, recv_sems, local_sem):
        x_, y_, c_ = lax.axis_index("x"), lax.axis_index("y"), lax.axis_index("c")
        me, sibling = (x_, y_, c_), (x_, y_, 1 - c_)
        chips = [(1 - x_, y_), (x_, 1 - y_), (1 - x_, 1 - y_)]

        def slot(px, py, pc):
            return out_ref.at[4 * px + 2 * py + pc]

        def copy(kk, block, to, src=None):
            return pltpu.make_async_remote_copy(
                src_ref=slot(*block) if src is None else src, dst_ref=slot(*block),
                send_sem=send_sems.at[kk], recv_sem=recv_sems.at[kk], device_id=to, device_id_type=MESH)

        mine = pltpu.make_async_copy(x_ref, slot(*me), local_sem)
        mine.start()
        first = [copy(0, me, sibling, src=x_ref)]
        first += [copy(1 + j, me, (*chip, c_), src=x_ref) for j, chip in enumerate(chips)]
        for cp in first:
            cp.start()
        passed = [copy(4 + j, (*chip, c_), sibling) for j, chip in enumerate(chips)]
        for j, chip in enumerate(chips):
            copy(1 + j, (*chip, c_), me).wait_recv()
            passed[j].start()
        copy(0, sibling, me).wait_recv()
        for j, chip in enumerate(chips):
            copy(4 + j, (*chip, 1 - c_), me).wait_recv()
        for cp in first + passed:
            cp.wait_send()
        mine.wait()

    return pl.pallas_call(
        body, name=name, out_shape=SDS((N_DEV,) + x.shape, x.dtype),
        in_specs=[pl.BlockSpec(memory_space=pl.ANY)], out_specs=pl.BlockSpec(memory_space=pl.ANY),
        scratch_shapes=[pltpu.SemaphoreType.DMA((7,)), pltpu.SemaphoreType.DMA((7,)), pltpu.SemaphoreType.DMA],
        compiler_params=pltpu.CompilerParams(has_side_effects=True),
    )(x)


def _all_to_all(x, name):
    def body(x_ref, out_ref, send_sems, recv_sems, local_sem):
        x_, y_, c_ = lax.axis_index("x"), lax.axis_index("y"), lax.axis_index("c")
        me = 4 * x_ + 2 * y_ + c_
        mine = pltpu.make_async_copy(x_ref.at[me], out_ref.at[me], local_sem)
        mine.start()
        copies = []
        for r in range(1, N_DEV):
            px = 1 - x_ if r & 4 else x_
            py = 1 - y_ if r & 2 else y_
            pc = 1 - c_ if r & 1 else c_
            cp = pltpu.make_async_remote_copy(
                src_ref=x_ref.at[4 * px + 2 * py + pc], dst_ref=out_ref.at[me],
                send_sem=send_sems.at[r - 1], recv_sem=recv_sems.at[r - 1], device_id=(px, py, pc), device_id_type=MESH)
            cp.start()
            copies.append(cp)
        for cp in copies:
            cp.wait_recv()
        for cp in copies:
            cp.wait_send()
        mine.wait()

    return pl.pallas_call(
        body, name=name, out_shape=SDS(x.shape, x.dtype),
        in_specs=[pl.BlockSpec(memory_space=pl.ANY)], out_specs=pl.BlockSpec(memory_space=pl.ANY),
        scratch_shapes=[pltpu.SemaphoreType.DMA((7,)), pltpu.SemaphoreType.DMA((7,)), pltpu.SemaphoreType.DMA],
        compiler_params=pltpu.CompilerParams(has_side_effects=True),
    )(x)


BIG = ("w_in", "cv_pw_w", "w_out", "x_wq", "x_wk", "x_wv", "x_wo", "ffn_w_up", "ffn_w_down")
PACKED = (("ffn_w_up", 704, 0), ("w_in", 352, 1408), ("ffn_w_down", 352, 2112), ("w_out", 128, 2816),
          ("x_wq", 128, 3072), ("x_wk", 128, 3328), ("x_wv", 128, 3584), ("x_wo", 128, 3840))
TRANSPOSED = ("w_in", "ffn_w_up")
PW_FIRST_ROW = 4096
PW_ROWS = 16
PACK_ROWS = PW_FIRST_ROW + PW_ROWS
COL_SHARDED = ("w_in", "ffn_w_up", "cv_w", "ffn_conv_w")
SMALL_SHARDED = ("cv_w", "ffn_conv_w")
REPLICATED = ("mix_norm_pre", "cv_b", "cv_ln_g", "cv_ln_b", "cv_pw_b", "mix_norm_post", "x_norm_pre", "mem_norm",
              "x_norm_post", "ffn_norm_pre", "ffn_conv_b", "ffn_norm_post")
WEIGHTS = ("mix_norm_pre", "w_in", "cv_w", "cv_b", "cv_ln_g", "cv_ln_b", "cv_pw_w", "cv_pw_b", "w_out", "mix_norm_post",
           "x_norm_pre", "mem_norm", "x_wq", "x_wk", "x_wv", "x_wo", "x_norm_post", "ffn_norm_pre", "ffn_w_up",
           "ffn_conv_w", "ffn_conv_b", "ffn_w_down", "ffn_norm_post")
PAYLOAD_COLS = 1024


PAYLOAD_ROW_TILE = 16


def _pad_rows(flat, cols):
    n = flat.shape[-1]
    rows = -(-n // (cols * PAYLOAD_ROW_TILE)) * PAYLOAD_ROW_TILE
    pad = rows * cols - n
    if pad:
        flat = jnp.concatenate([flat, jnp.zeros(flat.shape[:-1] + (pad,), flat.dtype)], axis=-1)
    return flat.reshape(flat.shape[:-1] + (rows, cols))


def _unshard(name, parts):
    n, depth, r, c = parts.shape
    if name in COL_SHARDED:
        return parts.transpose(1, 2, 0, 3).reshape(depth, r, n * c)
    return parts.transpose(1, 0, 2, 3).reshape(depth, n * r, c)


def _to_shards(name, full):
    depth, r, c = full.shape
    if name in COL_SHARDED:
        return full.reshape(depth, r, N_DEV, c // N_DEV).transpose(2, 0, 1, 3).reshape(N_DEV, -1)
    return full.reshape(depth, N_DEV, r // N_DEV, c).transpose(1, 0, 2, 3).reshape(N_DEV, -1)


def _to_sub(x, dil):
    s_len, w = x.shape
    h = w // HEAD_DIM
    return x.reshape(s_len // dil, dil, h, HEAD_DIM).transpose(1, 2, 0, 3).reshape(dil * h, s_len // dil, HEAD_DIM)


def _from_sub(x, dil):
    g, l, hd = x.shape
    h = g // dil
    return x.reshape(dil, h, l, hd).transpose(2, 0, 1, 3).reshape(l * dil, h * hd)


def _heads_major(x, h):
    return x.reshape(x.shape[0], h, HEAD_DIM).transpose(1, 0, 2)


def _tokens_major(x):
    return x.transpose(1, 0, 2).reshape(x.shape[1], -1)


def _ffn_halves(p):
    w, b = p["ffn_conv_w"], p["ffn_conv_b"]
    return w[:, :D_FF], w[:, D_FF:], b[:, :D_FF], b[:, D_FF:]


def _layer_fwd(l, h, hn, p, mem, cos, sin, g_next):
    sv = {"h0": h, "hn0": hn}
    u = _mm(hn, p["w_in"], "nt", F32, f"l{l}_in_proj")
    sv["u"] = u
    sb = _heads_major(u[:, :3 * SB_WIDTH].astype(BF16), 3 * SB_HEADS)
    sb_q, sb_k, sb_v = sb[:SB_HEADS], sb[SB_HEADS:2 * SB_HEADS], sb[2 * SB_HEADS:]
    a_out, sb_tot = _sb_fwd(sb_q, sb_k, sb_v, f"l{l}_sb_fwd")
    sv.update(sb_q=sb_q, sb_k=sb_k, sb_v=sb_v, sb_tot=sb_tot)

    cv_s, cv_c = _cv_fwd(u, p["cv_w"], p["cv_b"], p["cv_ln_g"], p["cv_ln_b"], f"l{l}_cv_fwd")
    b_out = _mm(cv_s, p["cv_pw_w"], "nn", BF16, f"l{l}_cv_pw", bias=p["cv_pw_b"])
    sv.update(cv_s=cv_s, cv_c=cv_c)

    qk = _rope_fwd(u, cos, sin, f"l{l}_rope_fwd")
    dl_v = u[:, IN_WIDTH - DL_WIDTH:].astype(BF16)
    outs, lses, subs = [], [], []
    for b, (_, dil) in enumerate(DL_PATTERN):
        qs, ks, vs = _to_sub(qk[:, :DL_WIDTH], dil), _to_sub(qk[:, DL_WIDTH:], dil), _to_sub(dl_v, dil)
        o, lse = _dl_fwd(qs, ks, vs, f"l{l}_dl{b}_fwd")
        subs.append((qs, ks, vs, lse))
        outs.append(_from_sub(o, dil))
        lses.append(_from_sub(lse, dil))
    c_out, c_out_f32, w1, w2, w3 = _dl_mix_fwd(outs, lses, f"l{l}_dl_mix")
    sv.update(dl_subs=subs, dl_o=c_out_f32, dl_w=(w1, w2, w3))

    mix = jnp.concatenate([_tokens_major(a_out), b_out, c_out], axis=-1)
    y = _mm(mix, p["w_out"], "nn", F32, f"l{l}_out_proj")
    h1, hn1 = _res_norm_fwd(h, y, p["mix_norm_post"], p["x_norm_pre"], f"l{l}_mix_post")
    sv.update(mix=mix, y_mix=y, h1=h1, hn1=hn1)

    xq = _mm(hn1, p["x_wq"], "nn", BF16, f"l{l}_xq")
    memn = _rms_fwd(mem, p["mem_norm"], f"l{l}_mem_norm")
    xk = _mm(memn, p["x_wk"], "nn", BF16, f"l{l}_xk")
    xv = _mm(memn, p["x_wv"], "nn", BF16, f"l{l}_xv")
    xo = _xattn_fwd(xq, xk, xv, f"l{l}_xattn_fwd")
    y = _mm(xo, p["x_wo"], "nn", F32, f"l{l}_xo_proj")
    h2, hn2 = _res_norm_fwd(h1, y, p["x_norm_post"], p["ffn_norm_pre"], f"l{l}_x_post")
    sv.update(xq=xq, xk=xk, xv=xv, xo=xo, memn=memn, y_x=y, h2=h2, hn2=hn2)

    up_g = _mm(hn2, p["ffn_w_up"][0], "nt", F32, f"l{l}_ffn_up_gate")
    up_v = _mm(hn2, p["ffn_w_up"][1], "nt", F32, f"l{l}_ffn_up_val")
    act = _ffn_act_fwd(up_g, up_v, *_ffn_halves(p), f"l{l}_ffn_act")
    y = _mm(act, p["ffn_w_down"], "nn", F32, f"l{l}_ffn_down")
    h3, hn3 = _res_norm_fwd(h2, y, p["ffn_norm_post"], g_next, f"l{l}_ffn_post")
    sv.update(up_g=up_g, up_v=up_v, act=act, y_ffn=y)
    return h3, hn3, sv


def _layer_bwd(l, dh, dy, p, sv, mem, cos, sin, first_layer):
    gr = {}
    dact = _mm(dy, p["ffn_w_down"], "nt", F32, f"l{l}_d_act")
    gr["ffn_w_down"] = _mm(sv["act"], dy, "tn", BF16, f"l{l}_dw_down")
    dup_g, dup_v, dwg, dwv, dbg, dbv = _ffn_act_bwd(sv["up_g"], sv["up_v"], dact, *_ffn_halves(p), f"l{l}_ffn_act_bwd")
    gr["ffn_conv_w"] = jnp.concatenate([dwg, dwv], axis=1)
    gr["ffn_conv_b"] = jnp.concatenate([dbg, dbv], axis=1)
    dhn = (_mm(dup_g, p["ffn_w_up"][0], "nn", F32, f"l{l}_d_hn2_gate"), _mm(dup_v, p["ffn_w_up"][1], "nn", F32, f"l{l}_d_hn2_val"))
    gr["ffn_w_up"] = (_mm(dup_g, sv["hn2"], "tn", BF16, f"l{l}_dw_up_gate"), _mm(dup_v, sv["hn2"], "tn", BF16, f"l{l}_dw_up_val"))
    dh, dy, gr["ffn_norm_pre"], gr["x_norm_post"] = _norm_bwd(
        dh, (sv["h2"], p["ffn_norm_pre"], dhn), (sv["y_x"], p["x_norm_post"]), f"l{l}_x_post_bwd")

    do = _mm(dy, p["x_wo"], "nt", BF16, f"l{l}_d_xo")
    gr["x_wo"] = _mm(sv["xo"], dy, "tn", BF16, f"l{l}_dw_xo")
    dq, dk, dv = _xattn_bwd(sv["xq"], sv["xk"], sv["xv"], do, f"l{l}_xattn_bwd")
    dhn = _mm(dq, p["x_wq"], "nt", F32, f"l{l}_d_hn1")
    gr["x_wq"] = _mm(sv["hn1"], dq, "tn", BF16, f"l{l}_dw_xq")
    gr["x_wk"] = _mm(sv["memn"], dk, "tn", BF16, f"l{l}_dw_xk")
    gr["x_wv"] = _mm(sv["memn"], dv, "tn", BF16, f"l{l}_dw_xv")
    dmemn = _mm(dk, p["x_wk"], "nt", F32, f"l{l}_d_memn_k") + _mm(dv, p["x_wv"], "nt", F32, f"l{l}_d_memn_v")
    gr["mem_norm"] = _rms_gain_grad(mem, p["mem_norm"], dmemn, f"l{l}_mem_norm_bwd")
    dh, dy, gr["x_norm_pre"], gr["mix_norm_post"] = _norm_bwd(
        dh, (sv["h1"], p["x_norm_pre"], dhn), (sv["y_mix"], p["mix_norm_post"]), f"l{l}_mix_post_bwd")

    dmix = _mm(dy, p["w_out"], "nt", F32, f"l{l}_d_mix")
    gr["w_out"] = _mm(sv["mix"], dy, "tn", BF16, f"l{l}_dw_out")
    do_a = _heads_major(dmix[:, :SB_WIDTH].astype(BF16), SB_HEADS)
    dq, dk, dv = _sb_bwd(sv["sb_q"], sv["sb_k"], sv["sb_v"], do_a, sv["sb_tot"], f"l{l}_sb_bwd")
    du_sb = _tokens_major(jnp.concatenate([dq, dk, dv], axis=0))

    db_out = dmix[:, SB_WIDTH:SB_WIDTH + CV_WIDTH]
    ds = _mm(db_out, p["cv_pw_w"], "nt", F32, f"l{l}_d_cv_s")
    gr["cv_pw_w"] = _mm(sv["cv_s"], db_out, "tn", BF16, f"l{l}_dw_cv_pw")
    du_cv, dcvw, gr["cv_b"], gr["cv_ln_g"], gr["cv_ln_b"], gr["cv_pw_b"] = _cv_bwd(
        sv["u"], sv["cv_c"], ds, db_out, p["cv_w"], p["cv_ln_g"], p["cv_ln_b"], f"l{l}_cv_bwd")
    gr["cv_w"] = dcvw[:CV_KERNEL]

    dc_out = dmix[:, SB_WIDTH + CV_WIDTH:]
    dqs, dks, dvs = [], [], []
    for b, (_, dil) in enumerate(DL_PATTERN):
        qs, ks, vs, lse = sv["dl_subs"][b]
        dq, dk, dv = _dl_bwd(qs, ks, vs, _to_sub(dc_out, dil), _to_sub(sv["dl_o"], dil), _to_sub(sv["dl_w"][b], dil), lse,
                             f"l{l}_dl{b}_bwd")
        dqs.append(_from_sub(dq, dil))
        dks.append(_from_sub(dk, dil))
        dvs.append(_from_sub(dv, dil))
    du_dl = _rope_bwd(dqs, dks, dvs, cos, sin, f"l{l}_rope_bwd")

    du = jnp.concatenate([du_sb, du_cv, du_dl], axis=-1)
    dhn = _mm(du, p["w_in"], "nn", F32, f"l{l}_d_hn0")
    gr["w_in"] = _mm(du, sv["hn0"], "tn", BF16, f"l{l}_dw_in")
    post = None if first_layer else (sv["y_prev"], p["prev_ffn_norm_post"])
    dh, dy, gr["mix_norm_pre"], dg_prev = _norm_bwd(dh, (sv["h0"], p["mix_norm_pre"], dhn), post, f"l{l}_in_bwd")
    return dh, dy, gr, dg_prev


def kernel(x, mem, positions, mix_norm_pre, w_in, cv_w, cv_b, cv_ln_g, cv_ln_b, cv_pw_w, cv_pw_b, w_out, mix_norm_post, x_norm_pre, mem_norm, x_wq, x_wk, x_wv, x_wo, x_norm_post, ffn_norm_pre, ffn_w_up, ffn_conv_w, ffn_conv_b, ffn_w_down, ffn_norm_post, loss_target, m_mix_norm_pre, m_w_in, m_cv_w, m_cv_b, m_cv_ln_g, m_cv_ln_b, m_cv_pw_w, m_cv_pw_b, m_w_out, m_mix_norm_post, m_x_norm_pre, m_mem_norm, m_x_wq, m_x_wk, m_x_wv, m_x_wo, m_x_norm_post, m_ffn_norm_pre, m_ffn_w_up, m_ffn_conv_w, m_ffn_conv_b, m_ffn_w_down, m_ffn_norm_post, v_mix_norm_pre, v_w_in, v_cv_w, v_cv_b, v_cv_ln_g, v_cv_ln_b, v_cv_pw_w, v_cv_pw_b, v_w_out, v_mix_norm_post, v_x_norm_pre, v_mem_norm, v_x_wq, v_x_wk, v_x_wv, v_x_wo, v_x_norm_post, v_ffn_norm_pre, v_ffn_w_up, v_ffn_conv_w, v_ffn_conv_b, v_ffn_w_down, v_ffn_norm_post):
    args = locals()
    wts = {n: args[n] for n in WEIGHTS}
    mom = {n: args["m_" + n] for n in WEIGHTS}
    var = {n: args["v_" + n] for n in WEIGHTS}

    pieces = []
    for n, r, _ in PACKED:
        for l in range(DEPTH):
            w = wts[n][l].astype(BF16)
            pieces.append(w.T if n in TRANSPOSED else w)
    pieces.append(wts["cv_pw_w"].astype(BF16).reshape(PW_ROWS, PAYLOAD_COLS))
    gathered = _all_gather(jnp.concatenate(pieces, axis=0), "weights_all_gather")
    small_payload = _pad_rows(jnp.concatenate([wts[n].reshape(-1) for n in SMALL_SHARDED]), PAYLOAD_COLS)
    small = _all_gather(small_payload, "small_weights_all_gather").reshape(N_DEV, -1)
    full = {}
    half = N_DEV // 2
    for n, r, first in PACKED:
        blocks = [gathered[:, first + l * r:first + (l + 1) * r, :] for l in range(DEPTH)]
        if n == "ffn_w_up":
            full[n] = [(b[:half].reshape(half * r, PAYLOAD_COLS), b[half:].reshape(half * r, PAYLOAD_COLS)) for b in blocks]
        else:
            full[n] = [b.reshape(N_DEV * r, PAYLOAD_COLS) for b in blocks]
    full["cv_pw_w"] = _unshard("cv_pw_w", gathered[:, PW_FIRST_ROW:, :].reshape((N_DEV,) + wts["cv_pw_w"].shape))
    off = 0
    for n in SMALL_SHARDED:
        size = wts[n].size
        full[n] = _unshard(n, small[:, off:off + size].reshape((N_DEV,) + wts[n].shape))
        off += size
    for n in REPLICATED:
        full[n] = wts[n]
    loss_part, grad_x, grads = _local_step(x[0], mem[0], positions[0], loss_target[0], full)
    loss = lax.psum(loss_part, ("x", "y", "c"))

    pieces = []
    for n, r, _ in PACKED:
        for l in range(DEPTH):
            parts = grads[n][l] if isinstance(grads[n][l], tuple) else (grads[n][l],)
            pieces.append(jnp.concatenate([g.reshape(-1, r, PAYLOAD_COLS) for g in parts], axis=0))
    pieces.append(_to_shards("cv_pw_w", jnp.stack(grads["cv_pw_w"])).reshape(N_DEV, PW_ROWS, PAYLOAD_COLS))
    big_rows = jnp.concatenate(pieces, axis=1)
    small_rows = jnp.concatenate([_to_shards(n, jnp.stack(grads[n])) for n in SMALL_SHARDED], axis=1)
    rep_flat = jnp.concatenate([jnp.stack([g.reshape(-1) for g in grads[n]]).reshape(-1) for n in REPLICATED])
    rep_rows = jnp.broadcast_to(rep_flat[None], (N_DEV, rep_flat.shape[0]))
    f32_rows = _pad_rows(jnp.concatenate([small_rows, rep_rows], axis=1), PAYLOAD_COLS)
    received = _all_to_all(big_rows, "grads_all_to_all")
    small_parts = _all_to_all(f32_rows, "small_grads_all_to_all")

    res = {}
    for n, r, first in PACKED:
        shape = wts[n].shape
        two_d = (shape[0] * shape[1], shape[2])
        if n in TRANSPOSED:
            parts = received[:, first:first + DEPTH * r, :].reshape(N_DEV, DEPTH, r, PAYLOAD_COLS)
            parts, first, t = parts.transpose(0, 1, 3, 2).reshape((N_DEV,) + two_d), 0, None
        else:
            parts, t = received, r
        outs = _adamw(parts, wts[n].reshape(two_d), mom[n].reshape(two_d), var[n].reshape(two_d), f"adamw_{n}", first, t)
        res[n] = [o.reshape(shape) for o in outs]
    shape = wts["cv_pw_w"].shape
    two_d = (shape[0] * shape[1], shape[2])
    parts = received[:, PW_FIRST_ROW:, :].reshape((N_DEV,) + two_d)
    outs = _adamw(parts, wts["cv_pw_w"].reshape(two_d), mom["cv_pw_w"].reshape(two_d), var["cv_pw_w"].reshape(two_d),
                  "adamw_cv_pw_w")
    res["cv_pw_w"] = [o.reshape(shape) for o in outs]
    small_names = SMALL_SHARDED + REPLICATED
    flat_w = _pad_rows(jnp.concatenate([wts[n].reshape(-1) for n in small_names]), PAYLOAD_COLS)
    flat_m = _pad_rows(jnp.concatenate([mom[n].reshape(-1) for n in small_names]), PAYLOAD_COLS)
    flat_v = _pad_rows(jnp.concatenate([var[n].reshape(-1) for n in small_names]), PAYLOAD_COLS)
    outs = _adamw(small_parts, flat_w, flat_m, flat_v, "adamw_small")
    outs = [o.reshape(-1) for o in outs]
    off = 0
    for n in small_names:
        size = wts[n].size
        res[n] = [o[off:off + size].reshape(wts[n].shape) for o in outs]
        off += size

    result = [loss, grad_x[None]]
    for kind in range(4):
        result += [res[n][kind] for n in WEIGHTS]
    return tuple(result)


def _local_step(x2, mem2, positions, target, full):
    def layer_params(l):
        p = {n: full[n][l] for n in BIG + SMALL_SHARDED}
        p.update({n: full[n][l][None, :] for n in REPLICATED})
        return p

    pos = positions.astype(F32)
    half = HEAD_DIM // 2
    inv_freq = ROPE_THETA ** (-jnp.arange(half, dtype=F32) / half)
    ang = pos[:, None] * inv_freq
    cos = jnp.tile(jnp.cos(ang), (1, LANES // half))
    sin = jnp.tile(jnp.sin(ang), (1, LANES // half))

    params = [layer_params(l) for l in range(DEPTH)]
    h = x2
    hn = _rms_fwd(h, params[0]["mix_norm_pre"], "l0_in_norm")
    saved = []
    for l in range(DEPTH):
        g_next = params[l + 1]["mix_norm_pre"] if l + 1 < DEPTH else None
        h, hn, sv = _layer_fwd(l, h, hn, params[l], mem2, cos, sin, g_next)
        saved.append(sv)
    loss_part, dh = _loss_fwd(h, target, "loss")

    grads = {n: [None] * DEPTH for n in WEIGHTS}
    dh, dy, _, dg = _norm_bwd(dh, None, (saved[-1]["y_ffn"], params[-1]["ffn_norm_post"]), "last_post_bwd")
    grads["ffn_norm_post"][DEPTH - 1] = dg
    for l in reversed(range(DEPTH)):
        p = dict(params[l])
        sv = dict(saved[l])
        if l > 0:
            p["prev_ffn_norm_post"] = params[l - 1]["ffn_norm_post"]
            sv["y_prev"] = saved[l - 1]["y_ffn"]
        dh, dy, gr, dg_prev = _layer_bwd(l, dh, dy, p, sv, mem2, cos, sin, first_layer=(l == 0))
        for n, g in gr.items():
            grads[n][l] = g
        if l > 0:
            grads["ffn_norm_post"][l - 1] = dg_prev
    return loss_part[0, 0], dh, grads
```

```python
import functools
import math

import jax
import jax.numpy as jnp
from jax import lax
from jax.experimental import pallas as pl
from jax.experimental.pallas import tpu as pltpu

F32, BF16 = jnp.float32, jnp.bfloat16
SDS = jax.ShapeDtypeStruct

D_MODEL = 1024
SEQ = 4096
DEPTH = 2
HEAD_DIM = 64
SB_HEADS = 4
SB_WIDTH = 256
CV_WIDTH = 256
CV_KERNEL = 31
DL_HEADS = 8
DL_WIDTH = 512
IN_WIDTH = 2816
DL_PATTERN = ((128, 1), (512, 4), (2048, 16))
BLOCK = 128
ROPE_THETA = 10000.0
N_MEM = 256
X_HEADS = 4
X_HEAD_DIM = 256
D_FF = 2816
EPS = 1e-6
N_DEV = 8
LANES = 128

ADAM_LR = 0.001
ADAM_B1 = 0.9
ADAM_B2 = 0.999
ADAM_EPS = 1e-08
ADAM_WD = 0.01
ADAM_STEP = 10

VMEM_LIMIT_BYTES = 56 * 1024 * 1024
MESH = pl.DeviceIdType.MESH
NEG = -1e30


def _params(**kw):
    return pltpu.CompilerParams(vmem_limit_bytes=VMEM_LIMIT_BYTES, **kw)


def _pick(n, cands):
    for c in cands:
        if n % c == 0:
            return c
    return n


def _mm(a, b, mode, out_dtype, name, bias=None):
    if mode == "nn":
        (m, k), (k2, n) = a.shape, b.shape
    elif mode == "nt":
        (m, k), (n, k2) = a.shape, b.shape
    else:
        (k, m), (k2, n) = a.shape, b.shape
    assert k == k2, (a.shape, b.shape, mode)
    tm = _pick(m, (1024, 1408, 512, 256, 128))
    tn = _pick(n, (1024, 1408, 512, 256, 128))
    tk = k if k <= 2048 else _pick(k, (2048, 1408, 1024, 512))
    nk = k // tk
    dims = {"nn": ((1,), (0,)), "nt": ((1,), (1,)), "tn": ((0,), (0,))}[mode]

    def body(*refs):
        refs = list(refs)
        acc_ref = refs.pop() if nk > 1 else None
        a_ref, b_ref = refs[0], refs[1]
        bias_ref = refs[2] if bias is not None else None
        o_ref = refs[-1]
        p = lax.dot_general(a_ref[...].astype(BF16), b_ref[...].astype(BF16), (dims, ((), ())),
                            preferred_element_type=F32)

        def finish(v):
            if bias_ref is not None:
                v = v + bias_ref[...]
            o_ref[...] = v.astype(out_dtype)

        if nk == 1:
            finish(p)
        else:
            kk = pl.program_id(2)

            @pl.when(kk == 0)
            def _():
                acc_ref[...] = p

            @pl.when(kk > 0)
            def _():
                acc_ref[...] += p

            @pl.when(kk == nk - 1)
            def _():
                finish(acc_ref[...])

    a_spec = pl.BlockSpec((tk, tm), lambda i, j, kk: (kk, i)) if mode == "tn" else pl.BlockSpec((tm, tk), lambda i, j, kk: (i, kk))
    b_spec = pl.BlockSpec((tn, tk), lambda i, j, kk: (j, kk)) if mode == "nt" else pl.BlockSpec((tk, tn), lambda i, j, kk: (kk, j))
    in_specs = [a_spec, b_spec]
    args = [a, b]
    if bias is not None:
        in_specs.append(pl.BlockSpec((1, tn), lambda i, j, kk: (0, j)))
        args.append(bias)
    return pl.pallas_call(
        body, name=name, out_shape=SDS((m, n), out_dtype), grid=(m // tm, n // tn, nk),
        in_specs=in_specs, out_specs=pl.BlockSpec((tm, tn), lambda i, j, kk: (i, j)),
        scratch_shapes=[pltpu.VMEM((tm, tn), F32)] if nk > 1 else [], compiler_params=_params(),
    )(*args)


def _rms(x, g):
    r = lax.rsqrt(jnp.mean(x * x, axis=-1, keepdims=True) + EPS)
    return x * r * g


def _rms_bwd(x, g, dy):
    r = lax.rsqrt(jnp.mean(x * x, axis=-1, keepdims=True) + EPS)
    xh = x * r
    dyg = dy * g
    dx = r * (dyg - xh * jnp.mean(dyg * xh, axis=-1, keepdims=True))
    return dx, dy * xh


def _rms_fwd(x, g, name):
    rows, d = x.shape
    t = min(rows, 512)

    def body(x_ref, g_ref, o_ref):
        o_ref[...] = _rms(x_ref[...], g_ref[...]).astype(BF16)

    return pl.pallas_call(
        body, name=name, out_shape=SDS((rows, d), BF16), grid=(rows // t,),
        in_specs=[pl.BlockSpec((t, d), lambda i: (i, 0)), pl.BlockSpec((1, d), lambda i: (0, 0))],
        out_specs=pl.BlockSpec((t, d), lambda i: (i, 0)), compiler_params=_params(),
    )(x, g)


def _res_norm_fwd(h, y, g_post, g_next, name):
    rows, d = h.shape
    t = 512
    has_next = g_next is not None

    def body(*refs):
        if has_next:
            h_ref, y_ref, gp_ref, gn_ref, h1_ref, hn_ref = refs
        else:
            h_ref, y_ref, gp_ref, h1_ref = refs
        h1 = h_ref[...] + _rms(y_ref[...], gp_ref[...])
        h1_ref[...] = h1
        if has_next:
            hn_ref[...] = _rms(h1, gn_ref[...]).astype(BF16)

    row = pl.BlockSpec((t, d), lambda i: (i, 0))
    vec = pl.BlockSpec((1, d), lambda i: (0, 0))
    in_specs = [row, row, vec] + ([vec] if has_next else [])
    args = [h, y, g_post] + ([g_next] if has_next else [])
    out_shape = [SDS((rows, d), F32)] + ([SDS((rows, d), BF16)] if has_next else [])
    out_specs = [row] + ([row] if has_next else [])
    res = pl.pallas_call(body, name=name, out_shape=out_shape, grid=(rows // t,), in_specs=in_specs,
                         out_specs=out_specs, compiler_params=_params())(*args)
    return (res[0], res[1]) if has_next else (res[0], None)


def _norm_bwd(dh, pre, post, name):
    rows, d = dh.shape
    t = 512
    has_pre, has_post = pre is not None, post is not None
    if has_pre:
        dhns = pre[2] if isinstance(pre[2], tuple) else (pre[2],)
        pre = (pre[0], pre[1]) + dhns

    def body(*refs):
        refs = list(refs)
        dh_ref = refs.pop(0)
        if has_pre:
            h_ref, gpre_ref = refs.pop(0), refs.pop(0)
            dhn_refs = [refs.pop(0) for _ in dhns]
        if has_post:
            y_ref, gpost_ref = refs.pop(0), refs.pop(0)
        dht_ref = refs.pop(0)
        if has_post:
            dy_ref = refs.pop(0)
        if has_pre:
            dgpre_ref = refs.pop(0)
        if has_post:
            dgpost_ref = refs.pop(0)
        i = pl.program_id(0)
        dht = dh_ref[...]
        if has_pre:
            dhn = dhn_refs[0][...]
            for r in dhn_refs[1:]:
                dhn = dhn + r[...]
            dx, dgr = _rms_bwd(h_ref[...], gpre_ref[...], dhn)
            dht = dht + dx

            @pl.when(i == 0)
            def _():
                dgpre_ref[...] = jnp.zeros_like(dgpre_ref)

            dgpre_ref[...] += jnp.sum(dgr, axis=0, keepdims=True)
        dht_ref[...] = dht
        if has_post:
            dy, dgr = _rms_bwd(y_ref[...], gpost_ref[...], dht)
            dy_ref[...] = dy.astype(BF16)

            @pl.when(i == 0)
            def _():
                dgpost_ref[...] = jnp.zeros_like(dgpost_ref)

            dgpost_ref[...] += jnp.sum(dgr, axis=0, keepdims=True)

    row = pl.BlockSpec((t, d), lambda i: (i, 0))
    vec = pl.BlockSpec((1, d), lambda i: (0, 0))
    in_specs, args = [row], [dh]
    if has_pre:
        in_specs += [row, vec] + [row] * len(dhns)
        args += list(pre)
    if has_post:
        in_specs += [row, vec]
        args += list(post)
    out_shape, out_specs = [SDS((rows, d), F32)], [row]
    if has_post:
        out_shape.append(SDS((rows, d), BF16))
        out_specs.append(row)
    if has_pre:
        out_shape.append(SDS((1, d), F32))
        out_specs.append(vec)
    if has_post:
        out_shape.append(SDS((1, d), F32))
        out_specs.append(vec)
    res = list(pl.pallas_call(body, name=name, out_shape=out_shape, grid=(rows // t,), in_specs=in_specs,
                              out_specs=out_specs, compiler_params=_params())(*args))
    dht = res.pop(0)
    dy = res.pop(0) if has_post else None
    dgpre = res.pop(0) if has_pre else None
    dgpost = res.pop(0) if has_post else None
    return dht, dy, dgpre, dgpost


def _rms_gain_grad(x, g, dy, name):
    rows, d = x.shape

    def body(x_ref, g_ref, dy_ref, dg_ref):
        _, dgr = _rms_bwd(x_ref[...], g_ref[...], dy_ref[...])
        dg_ref[...] = jnp.sum(dgr, axis=0, keepdims=True)

    return pl.pallas_call(body, name=name, out_shape=SDS((1, d), F32), compiler_params=_params())(x, g, dy)


def _loss_fwd(h, target, name):
    rows, d = h.shape
    t = 512

    def body(h_ref, t_ref, loss_ref, dh_ref):
        i = pl.program_id(0)
        err = h_ref[...] - t_ref[...]
        dh_ref[...] = err * (1.0 / d)

        @pl.when(i == 0)
        def _():
            loss_ref[...] = jnp.zeros_like(loss_ref)

        part = jnp.sum(jnp.sum(err * err, axis=1, keepdims=True), axis=0, keepdims=True) * (0.5 / d)
        loss_ref[...] += jnp.broadcast_to(part, loss_ref.shape)

    row = pl.BlockSpec((t, d), lambda i: (i, 0))
    return pl.pallas_call(
        body, name=name, out_shape=(SDS((1, LANES), F32), SDS((rows, d), F32)), grid=(rows // t,),
        in_specs=[row, row], out_specs=(pl.BlockSpec((1, LANES), lambda i: (0, 0)), row), compiler_params=_params(),
    )(h, target)


def _rot_half(x, sign):
    w = x.shape[-1]
    lane = lax.broadcasted_iota(jnp.int32, x.shape, 1)
    first = (lane % HEAD_DIM) < (HEAD_DIM // 2)
    return jnp.where(first, -sign * pltpu.roll(x, w - HEAD_DIM // 2, axis=1), sign * pltpu.roll(x, HEAD_DIM // 2, axis=1))


def _rope_fwd(u, cos, sin, name):
    rows = u.shape[0]
    t, cw = 512, 256
    first_col = (3 * SB_WIDTH + 2 * CV_WIDTH) // cw

    def body(u_ref, c_ref, s_ref, o_ref):
        x = u_ref[...]
        c = jnp.tile(c_ref[...], (1, cw // LANES))
        s = jnp.tile(s_ref[...], (1, cw // LANES))
        o_ref[...] = (x * c + _rot_half(x, 1.0) * s).astype(BF16)

    tab = pl.BlockSpec((t, LANES), lambda i, j: (i, 0))
    return pl.pallas_call(
        body, name=name, out_shape=SDS((rows, 2 * DL_WIDTH), BF16), grid=(rows // t, 2 * DL_WIDTH // cw),
        in_specs=[pl.BlockSpec((t, cw), lambda i, j: (i, first_col + j)), tab, tab],
        out_specs=pl.BlockSpec((t, cw), lambda i, j: (i, j)), compiler_params=_params(),
    )(u, cos, sin)


def _rope_bwd(dqs, dks, dvs, cos, sin, name):
    rows = dqs[0].shape[0]
    t, w = 256, DL_WIDTH

    def body(*refs):
        c = jnp.tile(refs[9][...], (1, w // LANES))
        s = jnp.tile(refs[10][...], (1, w // LANES))
        o_ref = refs[11]
        dq = refs[0][...] + refs[1][...] + refs[2][...]
        dk = refs[3][...] + refs[4][...] + refs[5][...]
        dv = refs[6][...] + refs[7][...] + refs[8][...]
        o_ref[:, 0:w] = (dq * c + _rot_half(dq, -1.0) * s).astype(BF16)
        o_ref[:, w:2 * w] = (dk * c + _rot_half(dk, -1.0) * s).astype(BF16)
        o_ref[:, 2 * w:3 * w] = dv.astype(BF16)

    row = pl.BlockSpec((t, w), lambda i: (i, 0))
    tab = pl.BlockSpec((t, LANES), lambda i: (i, 0))
    return pl.pallas_call(
        body, name=name, out_shape=SDS((rows, 3 * w), BF16), grid=(rows // t,), in_specs=[row] * 9 + [tab, tab],
        out_specs=pl.BlockSpec((t, 3 * w), lambda i: (i, 0)), compiler_params=_params(),
    )(*dqs, *dks, *dvs, cos, sin)


SB_TILE = 256


def _softplus(z):
    return jnp.maximum(z, 0.0) + jnp.log(1.0 + jnp.exp(-jnp.abs(z)))


def _split_dot(x, tri, passes):
    acc = None
    rem = x
    for _ in range(passes):
        part = rem.astype(BF16)
        rem = rem - part.astype(F32)
        d = jnp.dot(part, tri, preferred_element_type=F32)
        acc = d if acc is None else acc + d
    return acc


def _tri(t, rel):
    j = lax.broadcasted_iota(jnp.int32, (t, t), 0)
    s = lax.broadcasted_iota(jnp.int32, (t, t), 1)
    return rel(j, s).astype(BF16)


def _sb_fwd(q, k, v, payload, name):
    h, s_len, hd = q.shape
    t = SB_TILE
    nq = s_len // t
    scale = hd ** -0.5

    def body(q_ref, k_ref, v_ref, pay_ref, o_ref, tot_ref, gathered_ref, send_sems, recv_sems, local_sem):
        hh, i = pl.program_id(0), pl.program_id(1)
        start, forward, finish = _gather_steps(pay_ref, gathered_ref, send_sems, recv_sems, local_sem)
        pl.when((hh == 0) & (i == 0))(start)
        pl.when((hh == h - 1) & (i == 0))(forward)
        qv = q_ref[0] * scale
        upper = _tri(t, lambda j, s: j > s)

        def tiles(js, carry, diagonal):
            acc, run = carry
            starts = [pl.multiple_of(j * t, t) for j in js]
            zs = [lax.dot_general(qv, k_ref[0, pl.ds(st, t), :], (((1,), (1,)), ((), ())), preferred_element_type=F32)
                  for st in starts]
            sps = [_softplus(z) for z in zs]
            if diagonal:
                mask = lax.broadcasted_iota(jnp.int32, (t, t), 1) < lax.broadcasted_iota(jnp.int32, (t, t), 0)
                sps = [jnp.where(mask, sp, 0.0) for sp in sps]
            laters = [_split_dot(sp, upper, 2) for sp in sps]
            for st, z, sp, later in zip(starts, zs, sps, laters):
                a = jnp.exp((z - sp) - (run + later))
                if diagonal:
                    a = jnp.where(mask, a, 0.0)
                acc = acc + jnp.dot(a.astype(BF16), v_ref[0, pl.ds(st, t), :], preferred_element_type=F32)
                run = run + jnp.sum(sp, axis=1, keepdims=True)
            return acc, run

        def pair(pp, carry):
            j = i - 1 - 2 * pp
            return tiles([j, j - 1], carry, False)

        carry = tiles([i], (jnp.zeros((t, hd), F32), jnp.zeros((t, 1), F32)), True)
        carry = lax.fori_loop(0, i // 2, pair, carry)
        acc, run = lax.fori_loop(0, i % 2, lambda _, c: tiles([0], c, False), carry)
        o_ref[0] = acc.astype(BF16)
        tot_ref[0] = run
        pl.when((hh == h - 1) & (i == nq - 1))(finish)

    full = pl.BlockSpec((1, s_len, hd), lambda hh, i: (hh, 0, 0))
    tile = pl.BlockSpec((1, t, hd), lambda hh, i: (hh, i, 0))
    hbm = pl.BlockSpec(memory_space=pl.ANY)
    return pl.pallas_call(
        body, name=name,
        out_shape=(SDS((h, s_len, hd), BF16), SDS((h, s_len, 1), F32), SDS((N_DEV,) + payload.shape, payload.dtype)),
        grid=(h, nq), in_specs=[tile, full, full, hbm],
        out_specs=(tile, pl.BlockSpec((1, t, 1), lambda hh, i: (hh, i, 0)), hbm),
        scratch_shapes=_COMM_SEMAPHORES, compiler_params=_params(has_side_effects=True),
    )(q, k, v, payload)


def _sb_bwd(q, k, v, do, tot, payload, name):
    h, s_len, hd = q.shape
    t = SB_TILE
    nq = s_len // t
    scale = hd ** -0.5

    def body(q_ref, k_ref, v_ref, do_ref, tot_ref, pay_ref, dq_ref, dk_ref, dv_ref, received_ref, dk_acc, dv_acc,
             send_sems, recv_sems, local_sem):
        hh, i = pl.program_id(0), pl.program_id(1)
        start, finish = _exchange_steps(pay_ref, received_ref, send_sems, recv_sems, local_sem)
        pl.when((hh == 0) & (i == 0))(start)

        @pl.when(i == 0)
        def _():
            dk_acc[...] = jnp.zeros_like(dk_acc)
            dv_acc[...] = jnp.zeros_like(dv_acc)

        qv = q_ref[0] * scale
        dov = do_ref[0]
        total = tot_ref[0]
        upto = _tri(t, lambda j, s: j <= s)
        before = _tri(t, lambda j, s: j < s)
        nt_dims = (((1,), (1,)), ((), ()))
        tn_dims = (((0,), (0,)), ((), ()))

        def tiles(js, carry, diagonal):
            dq, run_sp, run_g = carry
            starts = [pl.multiple_of(j * t, t) for j in js]
            zs = [lax.dot_general(qv, k_ref[0, pl.ds(st, t), :], nt_dims, preferred_element_type=F32) for st in starts]
            das = [lax.dot_general(dov, v_ref[0, pl.ds(st, t), :], nt_dims, preferred_element_type=F32) for st in starts]
            sps = [_softplus(z) for z in zs]
            log_sigs = [z - sp for z, sp in zip(zs, sps)]
            if diagonal:
                mask = lax.broadcasted_iota(jnp.int32, (t, t), 1) < lax.broadcasted_iota(jnp.int32, (t, t), 0)
                sps = [jnp.where(mask, sp, 0.0) for sp in sps]
            pres = [_split_dot(sp, upto, 2) for sp in sps]
            a_s, gs = [], []
            for sp, log_sig, pre, da in zip(sps, log_sigs, pres, das):
                a = jnp.exp(log_sig - (total - (run_sp + pre)))
                if diagonal:
                    a = jnp.where(mask, a, 0.0)
                a_s.append(a)
                gs.append(a * da)
                run_sp = run_sp + jnp.sum(sp, axis=1, keepdims=True)
            g_pres = [_split_dot(g, before, 3) for g in gs]
            for st, a, g, g_pre, log_sig in zip(starts, a_s, gs, g_pres, log_sigs):
                sig = jnp.exp(log_sig)
                dz = g * (1.0 - sig) - sig * (run_g + g_pre)
                if diagonal:
                    dz = jnp.where(mask, dz, 0.0)
                dz = dz.astype(BF16)
                dq = dq + jnp.dot(dz, k_ref[0, pl.ds(st, t), :], preferred_element_type=F32)
                dk_acc[pl.ds(st, t), :] += lax.dot_general(dz, qv, tn_dims, preferred_element_type=F32)
                dv_acc[pl.ds(st, t), :] += lax.dot_general(a.astype(BF16), dov, tn_dims, preferred_element_type=F32)
                run_g = run_g + jnp.sum(g, axis=1, keepdims=True)
            return dq, run_sp, run_g

        zero = jnp.zeros((t, 1), F32)
        carry = lax.fori_loop(0, i // 2, lambda pp, c: tiles([2 * pp, 2 * pp + 1], c, False), (jnp.zeros((t, hd), F32), zero, zero))
        carry = lax.fori_loop(0, i % 2, lambda _, c: tiles([i - 1], c, False), carry)
        dq, _, _ = tiles([i], carry, True)
        dq_ref[0] = (dq * scale).astype(BF16)

        @pl.when(i == nq - 1)
        def _():
            dk_ref[0] = dk_acc[...].astype(BF16)
            dv_ref[0] = dv_acc[...].astype(BF16)

        pl.when((hh == h - 1) & (i == nq - 1))(finish)

    full = pl.BlockSpec((1, s_len, hd), lambda hh, i: (hh, 0, 0))
    tile = pl.BlockSpec((1, t, hd), lambda hh, i: (hh, i, 0))
    hbm = pl.BlockSpec(memory_space=pl.ANY)
    out = SDS((h, s_len, hd), BF16)
    return pl.pallas_call(
        body, name=name, out_shape=(out, out, out, SDS(payload.shape, payload.dtype)), grid=(h, nq),
        in_specs=[tile, full, full, tile, pl.BlockSpec((1, t, 1), lambda hh, i: (hh, i, 0)), hbm],
        out_specs=(tile, full, full, hbm),
        scratch_shapes=[pltpu.VMEM((s_len, hd), F32), pltpu.VMEM((s_len, hd), F32)] + _COMM_SEMAPHORES,
        compiler_params=_params(has_side_effects=True),
    )(q, k, v, do, tot, payload)


def _dl_scores(qv, kk, n):
    s = lax.dot_general(qv, kk, (((1,), (1,)), ((), ())), preferred_element_type=F32) * (HEAD_DIM ** -0.5)
    r = lax.broadcasted_iota(jnp.int32, s.shape, 0)
    c = lax.broadcasted_iota(jnp.int32, s.shape, 1)
    valid = (c >= r) & (c - r <= BLOCK) & ((n > 0) | (c >= BLOCK))
    return jnp.where(valid, s, NEG)


DL_BLOCKS_PER_STEP = SEQ // BLOCK
DL_UNROLL = 4


def _dl_window(ref, gi, n):
    prev = ref[gi, pl.ds(pl.multiple_of(jnp.maximum(n - 1, 0) * BLOCK, BLOCK), BLOCK), :]
    cur = ref[gi, pl.ds(pl.multiple_of(n * BLOCK, BLOCK), BLOCK), :]
    return jnp.concatenate([prev, cur], axis=0)


def _dl_fwd(q, k, v, name):
    g, l, hd = q.shape
    nb = l // BLOCK
    gb = DL_BLOCKS_PER_STEP // nb

    def body(q_ref, k_ref, v_ref, o_ref, lse_ref):
        def step(idx, _):
            gi, n = idx // nb, idx % nb
            rows = pl.ds(pl.multiple_of(n * BLOCK, BLOCK), BLOCK)
            s = _dl_scores(q_ref[gi, rows, :], _dl_window(k_ref, gi, n), n)
            m = jnp.max(s, axis=-1, keepdims=True)
            p = jnp.exp(s - m)
            den = jnp.sum(p, axis=-1, keepdims=True)
            o_ref[gi, rows, :] = jnp.dot((p / den).astype(BF16), _dl_window(v_ref, gi, n), preferred_element_type=F32)
            lse_ref[gi, rows, :] = jnp.broadcast_to(m + jnp.log(den), (BLOCK, hd))
            return 0

        lax.fori_loop(0, gb * nb, step, 0, unroll=DL_UNROLL)

    spec = pl.BlockSpec((gb, l, hd), lambda i: (i, 0, 0))
    out = SDS((g, l, hd), F32)
    return pl.pallas_call(body, name=name, out_shape=(out, out), grid=(g // gb,), in_specs=[spec] * 3, out_specs=(spec, spec),
                          compiler_params=_params())(q, k, v)


def _dl_bwd(q, k, v, do, o_mix, wt, lse, name):
    g, l, hd = q.shape
    nb = l // BLOCK
    gb = DL_BLOCKS_PER_STEP // nb
    scale = HEAD_DIM ** -0.5

    def body(q_ref, k_ref, v_ref, do_ref, om_ref, wt_ref, lse_ref, dq_ref, dk_ref, dv_ref):
        dk_ref[...] = jnp.zeros_like(dk_ref)
        dv_ref[...] = jnp.zeros_like(dv_ref)

        def step(idx, _):
            gi, n = idx // nb, idx % nb
            rows = pl.ds(pl.multiple_of(n * BLOCK, BLOCK), BLOCK)
            prev = pl.ds(pl.multiple_of(jnp.maximum(n - 1, 0) * BLOCK, BLOCK), BLOCK)
            qv = q_ref[gi, rows, :]
            kk = _dl_window(k_ref, gi, n)
            vv = _dl_window(v_ref, gi, n)
            s = _dl_scores(qv, kk, n)
            p = jnp.exp(s - jnp.max(lse_ref[gi, rows, :], axis=-1, keepdims=True))
            dov = do_ref[gi, rows, :]
            w = jnp.max(wt_ref[gi, rows, :], axis=-1, keepdims=True)
            d_all = jnp.sum(dov * om_ref[gi, rows, :], axis=-1, keepdims=True)
            do_n = (dov * w).astype(BF16)
            dp = lax.dot_general(do_n, vv, (((1,), (1,)), ((), ())), preferred_element_type=F32)
            ds = (p * (dp - w * d_all) * scale).astype(BF16)
            dq_ref[gi, rows, :] = jnp.dot(ds, kk, preferred_element_type=F32)
            dkk = lax.dot_general(ds, qv, (((0,), (0,)), ((), ())), preferred_element_type=F32)
            dvv = lax.dot_general(p.astype(BF16), do_n, (((0,), (0,)), ((), ())), preferred_element_type=F32)
            dk_ref[gi, prev, :] += dkk[:BLOCK]
            dv_ref[gi, prev, :] += dvv[:BLOCK]
            dk_ref[gi, rows, :] += dkk[BLOCK:]
            dv_ref[gi, rows, :] += dvv[BLOCK:]
            return 0

        lax.fori_loop(0, gb * nb, step, 0, unroll=DL_UNROLL)

    spec = pl.BlockSpec((gb, l, hd), lambda i: (i, 0, 0))
    out = SDS((g, l, hd), F32)
    return pl.pallas_call(body, name=name, out_shape=(out, out, out), grid=(g // gb,), in_specs=[spec] * 7,
                          out_specs=(spec, spec, spec), compiler_params=_params())(q, k, v, do, o_mix, wt, lse)


def _dl_mix_fwd(outs, lses, name):
    rows, w = outs[0].shape
    t = 256

    def body(o1, o2, o3, l1, l2, l3, ob_ref, of_ref, w1, w2, w3):
        a, b, c = l1[...], l2[...], l3[...]
        m = jnp.maximum(jnp.maximum(a, b), c)
        ea, eb, ec = jnp.exp(a - m), jnp.exp(b - m), jnp.exp(c - m)
        den = ea + eb + ec
        wa, wb, wc = ea / den, eb / den, ec / den
        o = wa * o1[...] + wb * o2[...] + wc * o3[...]
        ob_ref[...] = o.astype(BF16)
        of_ref[...] = o
        w1[...] = wa
        w2[...] = wb
        w3[...] = wc

    row = pl.BlockSpec((t, w), lambda i: (i, 0))
    f = SDS((rows, w), F32)
    return pl.pallas_call(body, name=name, out_shape=(SDS((rows, w), BF16), f, f, f, f), grid=(rows // t,),
                          in_specs=[row] * 6, out_specs=(row,) * 5, compiler_params=_params())(*outs, *lses)


def _x_probs(qh, kh):
    s = lax.dot_general(qh, kh, (((1,), (1,)), ((), ())), preferred_element_type=F32) * (X_HEAD_DIM ** -0.5)
    e = jnp.exp(s - jnp.max(s, axis=-1, keepdims=True))
    return e / jnp.sum(e, axis=-1, keepdims=True)


def _xattn_fwd(q, k, v, name):
    rows, d = q.shape
    t = 512

    def body(q_ref, k_ref, v_ref, o_ref):
        for hh in range(X_HEADS):
            cols = slice(hh * X_HEAD_DIM, (hh + 1) * X_HEAD_DIM)
            p = _x_probs(q_ref[:, cols], k_ref[:, cols])
            o_ref[:, cols] = jnp.dot(p.astype(BF16), v_ref[:, cols], preferred_element_type=F32).astype(BF16)

    row = pl.BlockSpec((t, d), lambda i: (i, 0))
    mem = pl.BlockSpec((N_MEM, d), lambda i: (0, 0))
    return pl.pallas_call(body, name=name, out_shape=SDS((rows, d), BF16), grid=(rows // t,), in_specs=[row, mem, mem],
                          out_specs=row, compiler_params=_params())(q, k, v)


def _xattn_bwd(q, k, v, do, name):
    rows, d = q.shape
    t = 512
    scale = X_HEAD_DIM ** -0.5

    def body(q_ref, k_ref, v_ref, do_ref, dq_ref, dk_ref, dv_ref):
        @pl.when(pl.program_id(0) == 0)
        def _():
            dk_ref[...] = jnp.zeros_like(dk_ref)
            dv_ref[...] = jnp.zeros_like(dv_ref)

        for hh in range(X_HEADS):
            cols = slice(hh * X_HEAD_DIM, (hh + 1) * X_HEAD_DIM)
            qh, kh, vh, doh = q_ref[:, cols], k_ref[:, cols], v_ref[:, cols], do_ref[:, cols]
            p = _x_probs(qh, kh)
            dp = lax.dot_general(doh, vh, (((1,), (1,)), ((), ())), preferred_element_type=F32)
            ds = (p * (dp - jnp.sum(p * dp, axis=-1, keepdims=True)) * scale).astype(BF16)
            dq_ref[:, cols] = jnp.dot(ds, kh, preferred_element_type=F32).astype(BF16)
            dk_ref[:, cols] += lax.dot_general(ds, qh, (((0,), (0,)), ((), ())), preferred_element_type=F32)
            dv_ref[:, cols] += lax.dot_general(p.astype(BF16), doh, (((0,), (0,)), ((), ())), preferred_element_type=F32)

    row = pl.BlockSpec((t, d), lambda i: (i, 0))
    mem = pl.BlockSpec((N_MEM, d), lambda i: (0, 0))
    return pl.pallas_call(
        body, name=name, out_shape=(SDS((rows, d), BF16), SDS((N_MEM, d), F32), SDS((N_MEM, d), F32)), grid=(rows // t,),
        in_specs=[row, mem, mem, row], out_specs=(row, mem, mem), compiler_params=_params(),
    )(q, k, v, do)


CV_TILE = 256
CV_HALO = 32
CV_LEAD = CV_HALO - (CV_KERNEL - 1)


def _shifted(win, off, rows):
    n = win.shape[0]
    return pltpu.roll(win, (n - off) % n, axis=0)[:rows]


def _glu(val, gate):
    return val * jax.nn.sigmoid(gate)


def _ln_parts(c):
    mu = jnp.mean(c, axis=-1, keepdims=True)
    xc = c - mu
    rstd = lax.rsqrt(jnp.mean(xc * xc, axis=-1, keepdims=True) + EPS)
    return xc * rstd, rstd


def _cv_fwd(u, cv_w, cv_b, ln_g, ln_b, name):
    rows = u.shape[0]
    t, w = CV_TILE, CV_WIDTH
    val_col = 3 * SB_WIDTH // w
    ratio = t // CV_HALO

    def body(val_ref, gate_ref, pval_ref, pgate_ref, w_ref, b_ref, g_ref, beta_ref, s_ref, c_ref):
        i = pl.program_id(0)
        hist = jnp.where(i > 0, _glu(pval_ref[...], pgate_ref[...]), 0.0)
        win = jnp.concatenate([hist, _glu(val_ref[...], gate_ref[...])], axis=0)
        acc = jnp.broadcast_to(b_ref[...], (t, w))
        for kk in range(CV_KERNEL):
            acc = acc + _shifted(win, CV_LEAD + kk, t) * w_ref[kk:kk + 1, :]
        c_ref[...] = acc
        n, _ = _ln_parts(acc)
        y = n * g_ref[...] + beta_ref[...]
        s_ref[...] = (y * jax.nn.sigmoid(y)).astype(BF16)

    cur = lambda col: pl.BlockSpec((t, w), lambda i: (i, col))
    prev = lambda col: pl.BlockSpec((CV_HALO, w), lambda i: (jnp.maximum(i * ratio - 1, 0), col))
    vec = pl.BlockSpec((1, w), lambda i: (0, 0))
    return pl.pallas_call(
        body, name=name, out_shape=(SDS((rows, w), BF16), SDS((rows, w), F32)), grid=(rows // t,),
        in_specs=[cur(val_col), cur(val_col + 1), prev(val_col), prev(val_col + 1),
                  pl.BlockSpec((CV_KERNEL, w), lambda i: (0, 0)), vec, vec, vec],
        out_specs=(pl.BlockSpec((t, w), lambda i: (i, 0)),) * 2, compiler_params=_params(),
    )(u, u, u, u, cv_w, cv_b, ln_g, ln_b)


def _cv_bwd(u, c, ds, db_out, cv_w, ln_g, ln_b, name):
    rows = u.shape[0]
    t, w = CV_TILE, CV_WIDTH
    val_col = 3 * SB_WIDTH // w
    ratio = t // CV_HALO
    nt = rows // t

    def conv_out_grad(c_v, ds_v, g_v, beta_v):
        n, rstd = _ln_parts(c_v)
        y = n * g_v + beta_v
        sig = jax.nn.sigmoid(y)
        dy = ds_v * (sig * (1.0 + y * (1.0 - sig)))
        dn = dy * g_v
        dc = rstd * (dn - jnp.mean(dn, axis=-1, keepdims=True) - n * jnp.mean(dn * n, axis=-1, keepdims=True))
        return dc, dy, n

    def body(val_ref, gate_ref, pval_ref, pgate_ref, c_ref, nc_ref, ds_ref, nds_ref, dbo_ref, w_ref, g_ref, beta_ref,
             dvg_ref, dw_ref, db_ref, dg_ref, dbeta_ref, dpwb_ref):
        i = pl.program_id(0)

        @pl.when(i == 0)
        def _():
            for r in (dw_ref, db_ref, dg_ref, dbeta_ref, dpwb_ref):
                r[...] = jnp.zeros_like(r)

        g_v, beta_v = g_ref[...], beta_ref[...]
        dc, dy, n = conv_out_grad(c_ref[...], ds_ref[...], g_v, beta_v)
        dc_next, _, _ = conv_out_grad(nc_ref[...], nds_ref[...], g_v, beta_v)
        dc_next = jnp.where(i < nt - 1, dc_next, 0.0)
        dg_ref[...] += jnp.sum(dy * n, axis=0, keepdims=True)
        dbeta_ref[...] += jnp.sum(dy, axis=0, keepdims=True)
        db_ref[...] += jnp.sum(dc, axis=0, keepdims=True)
        dpwb_ref[...] += jnp.sum(dbo_ref[...], axis=0, keepdims=True)

        val, gate = val_ref[...], gate_ref[...]
        hist = jnp.where(i > 0, _glu(pval_ref[...], pgate_ref[...]), 0.0)
        win = jnp.concatenate([hist, _glu(val, gate)], axis=0)
        dc_ext = jnp.concatenate([dc, dc_next], axis=0)
        dglu = jnp.zeros((t, w), F32)
        for kk in range(CV_KERNEL):
            dw_ref[kk:kk + 1, :] += jnp.sum(dc * _shifted(win, CV_LEAD + kk, t), axis=0, keepdims=True)
            dglu = dglu + _shifted(dc_ext, CV_KERNEL - 1 - kk, t) * w_ref[kk:kk + 1, :]
        sig = jax.nn.sigmoid(gate)
        dvg_ref[:, 0:w] = (dglu * sig).astype(BF16)
        dvg_ref[:, w:2 * w] = (dglu * val * sig * (1.0 - sig)).astype(BF16)

    cur = lambda col: pl.BlockSpec((t, w), lambda i: (i, col))
    prev = lambda col: pl.BlockSpec((CV_HALO, w), lambda i: (jnp.maximum(i * ratio - 1, 0), col))
    nxt = pl.BlockSpec((CV_HALO, w), lambda i: (jnp.minimum((i + 1) * ratio, rows // CV_HALO - 1), 0))
    vec = pl.BlockSpec((1, w), lambda i: (0, 0))
    return pl.pallas_call(
        body, name=name,
        out_shape=(SDS((rows, 2 * w), BF16), SDS((CV_HALO, w), F32), SDS((1, w), F32), SDS((1, w), F32), SDS((1, w), F32),
                   SDS((1, w), F32)),
        grid=(nt,),
        in_specs=[cur(val_col), cur(val_col + 1), prev(val_col), prev(val_col + 1), cur(0), nxt, cur(0), nxt, cur(0),
                  pl.BlockSpec((CV_KERNEL, w), lambda i: (0, 0)), vec, vec],
        out_specs=(pl.BlockSpec((t, 2 * w), lambda i: (i, 0)), pl.BlockSpec((CV_HALO, w), lambda i: (0, 0)), vec, vec, vec, vec),
        compiler_params=_params(),
    )(u, u, u, u, c, c, ds, ds, db_out, cv_w, ln_g, ln_b)


FFN_TILE = 512
FFN_COLS = 256
FFN_HALO = 8
FFN_KERNEL = 3
N_FF_BLOCKS = D_FF // FFN_COLS


def _conv3(prev8, cur, w_ref, b_ref, first):
    t = cur.shape[0]
    win = jnp.concatenate([jnp.where(first, 0.0, prev8), cur], axis=0)
    return (b_ref[...] + _shifted(win, FFN_HALO - 2, t) * w_ref[0:1, :] + _shifted(win, FFN_HALO - 1, t) * w_ref[1:2, :]
            + cur * w_ref[2:3, :])


def _gelu_gate(gate, val):
    return jax.nn.gelu(gate, approximate=True) * val


def _ffn_specs(t):
    ratio = t // FFN_HALO
    cur = pl.BlockSpec((t, FFN_COLS), lambda j, i: (i, j))
    prev = pl.BlockSpec((FFN_HALO, FFN_COLS), lambda j, i: (jnp.maximum(i * ratio - 1, 0), j))
    wsp = pl.BlockSpec((FFN_KERNEL, FFN_COLS), lambda j, i: (0, j))
    bsp = pl.BlockSpec((1, FFN_COLS), lambda j, i: (0, j))
    return cur, prev, wsp, bsp


def _ffn_act_fwd(up_g, up_v, w_g, w_v, b_g, b_v, name):
    rows = up_g.shape[0]
    t = FFN_TILE
    cur, prev, wsp, bsp = _ffn_specs(t)

    def body(g_ref, v_ref, pg_ref, pv_ref, wg_ref, wv_ref, bg_ref, bv_ref, o_ref):
        first = pl.program_id(1) == 0
        gate = _conv3(pg_ref[...], g_ref[...], wg_ref, bg_ref, first)
        val = _conv3(pv_ref[...], v_ref[...], wv_ref, bv_ref, first)
        o_ref[...] = _gelu_gate(gate, val).astype(BF16)

    return pl.pallas_call(
        body, name=name, out_shape=SDS((rows, D_FF), BF16), grid=(N_FF_BLOCKS, rows // t),
        in_specs=[cur, cur, prev, prev, wsp, wsp, bsp, bsp], out_specs=cur, compiler_params=_params(),
    )(up_g, up_v, up_g, up_v, w_g, w_v, b_g, b_v)


def _ffn_act_bwd(up_g, up_v, dact, w_g, w_v, b_g, b_v, name):
    rows = up_g.shape[0]
    t = FFN_TILE
    te = t + FFN_HALO
    ratio = t // FFN_HALO
    nt = rows // t
    cur, prev, wsp, bsp = _ffn_specs(t)
    nxt = pl.BlockSpec((FFN_HALO, FFN_COLS), lambda j, i: (jnp.minimum((i + 1) * ratio, rows // FFN_HALO - 1), j))

    def conv_ext(pre, x, nx, w_ref, b_ref, first):
        win = jnp.concatenate([jnp.where(first, 0.0, pre), x, nx], axis=0)
        out = (b_ref[...] + _shifted(win, FFN_HALO - 2, te) * w_ref[0:1, :] + _shifted(win, FFN_HALO - 1, te) * w_ref[1:2, :]
               + _shifted(win, FFN_HALO, te) * w_ref[2:3, :])
        return out, win

    def body(g_ref, v_ref, pg_ref, pv_ref, ng_ref, nv_ref, da_ref, nda_ref, wg_ref, wv_ref, bg_ref, bv_ref,
             dug_ref, duv_ref, dwg_ref, dwv_ref, dbg_ref, dbv_ref):
        i = pl.program_id(1)
        first = i == 0
        gate, win_g = conv_ext(pg_ref[...], g_ref[...], ng_ref[...], wg_ref, bg_ref, first)
        val, win_v = conv_ext(pv_ref[...], v_ref[...], nv_ref[...], wv_ref, bv_ref, first)
        da = jnp.concatenate([da_ref[...], jnp.where(i < nt - 1, nda_ref[...], 0.0)], axis=0)
        _, vjp = jax.vjp(_gelu_gate, gate, val)
        dgate, dval = vjp(da)

        @pl.when(first)
        def _():
            for r in (dwg_ref, dwv_ref, dbg_ref, dbv_ref):
                r[...] = jnp.zeros_like(r)

        for dc_ext, win, w_ref, du_ref, dw_ref, db_ref in ((dgate, win_g, wg_ref, dug_ref, dwg_ref, dbg_ref),
                                                          (dval, win_v, wv_ref, duv_ref, dwv_ref, dbv_ref)):
            dc = dc_ext[:t]
            du_ref[...] = (dc * w_ref[2:3, :] + _shifted(dc_ext, 1, t) * w_ref[1:2, :]
                           + _shifted(dc_ext, 2, t) * w_ref[0:1, :]).astype(BF16)
            for kk in range(FFN_KERNEL):
                dw_ref[kk:kk + 1, :] += jnp.sum(dc * _shifted(win, FFN_HALO - 2 + kk, t), axis=0, keepdims=True)
            db_ref[...] += jnp.sum(dc, axis=0, keepdims=True)

    big, wshape, bshape = SDS((rows, D_FF), BF16), SDS((FFN_KERNEL, D_FF), F32), SDS((1, D_FF), F32)
    return pl.pallas_call(
        body, name=name, out_shape=(big, big, wshape, wshape, bshape, bshape), grid=(N_FF_BLOCKS, nt),
        in_specs=[cur, cur, prev, prev, nxt, nxt, cur, nxt, wsp, wsp, bsp, bsp], out_specs=(cur, cur, wsp, wsp, bsp, bsp),
        compiler_params=_params(),
    )(up_g, up_v, up_g, up_v, up_g, up_v, dact, dact, w_g, w_v, b_g, b_v)


def _adamw_update(parts, w_ref, m_ref, v_ref, g_ref, d_ref, nm_ref, nv_ref):
    g = parts[0].astype(F32)
    for s in range(1, N_DEV):
        g = g + parts[s].astype(F32)
    nm = ADAM_B1 * m_ref[...] + (1.0 - ADAM_B1) * g
    nv = ADAM_B2 * v_ref[...] + (1.0 - ADAM_B2) * jnp.square(g)
    m_hat = nm / (1.0 - ADAM_B1 ** ADAM_STEP)
    v_hat = nv / (1.0 - ADAM_B2 ** ADAM_STEP)
    g_ref[...] = g
    d_ref[...] = -ADAM_LR * (m_hat / (jnp.sqrt(v_hat) + ADAM_EPS) + ADAM_WD * w_ref[...])
    nm_ref[...] = nm
    nv_ref[...] = nv


def _adamw(parts, w, m, v, name):
    rows, cols = w.shape
    t = _pick(rows, (512, 256, 128)) if rows > 512 else rows

    def body(p_ref, *refs):
        _adamw_update(p_ref[...], *refs)

    row = pl.BlockSpec((t, cols), lambda i: (i, 0))
    out = SDS((rows, cols), F32)
    return pl.pallas_call(
        body, name=name, out_shape=(out,) * 4, grid=(rows // t,),
        in_specs=[pl.BlockSpec((N_DEV, t, cols), lambda i: (0, i, 0)), row, row, row], out_specs=(row,) * 4,
        compiler_params=_params(),
    )(parts, w, m, v)


def _adamw_packed(sources, w, m, v, name):
    rows, cols = w.shape
    t = ADAMW_ROW_BLOCK
    nb = rows // DEPTH // t
    (src0, first0), (src1, first1) = sources
    assert first0 % t == 0 and first1 % t == 0 and rows % (DEPTH * t) == 0

    def body(p0_ref, p1_ref, *refs):
        layer = pl.program_id(0)
        _adamw_update(jnp.where(layer == 0, p0_ref[...], p1_ref[...]), *refs)

    spec0 = pl.BlockSpec((N_DEV, t, cols), lambda l, i: (0, first0 // t + i * (1 - l) + (nb - 1) * l, 0))
    spec1 = pl.BlockSpec((N_DEV, t, cols), lambda l, i: (0, first1 // t + i * l, 0))
    row = pl.BlockSpec((t, cols), lambda l, i: (l * nb + i, 0))
    out = SDS((rows, cols), F32)
    return pl.pallas_call(body, name=name, out_shape=(out,) * 4, grid=(DEPTH, nb), in_specs=[spec0, spec1, row, row, row],
                          out_specs=(row,) * 4, compiler_params=_params())(src0, src1, w, m, v)


_COMM_SEMAPHORES = [pltpu.SemaphoreType.DMA((N_DEV - 1,)), pltpu.SemaphoreType.DMA((N_DEV - 1,)), pltpu.SemaphoreType.DMA]


def _gather_steps(x_ref, out_ref, send_sems, recv_sems, local_sem):
    x_, y_, c_ = lax.axis_index("x"), lax.axis_index("y"), lax.axis_index("c")
    me, sibling = (x_, y_, c_), (x_, y_, 1 - c_)
    chips = [(1 - x_, y_), (x_, 1 - y_), (1 - x_, 1 - y_)]

    def slot(px, py, pc):
        return out_ref.at[4 * px + 2 * py + pc]

    def copy(kk, block, to, src=None):
        return pltpu.make_async_remote_copy(
            src_ref=slot(*block) if src is None else src, dst_ref=slot(*block),
            send_sem=send_sems.at[kk], recv_sem=recv_sems.at[kk], device_id=to, device_id_type=MESH)

    def mine():
        return pltpu.make_async_copy(x_ref, slot(*me), local_sem)

    def first():
        return [copy(0, me, sibling, src=x_ref)] + [copy(1 + j, me, (*chip, c_), src=x_ref) for j, chip in enumerate(chips)]

    def passed():
        return [copy(4 + j, (*chip, c_), sibling) for j, chip in enumerate(chips)]

    def start():
        mine().start()
        for cp in first():
            cp.start()

    def forward():
        for j, (chip, cp) in enumerate(zip(chips, passed())):
            copy(1 + j, (*chip, c_), me).wait_recv()
            cp.start()

    def finish():
        copy(0, sibling, me).wait_recv()
        for j, chip in enumerate(chips):
            copy(4 + j, (*chip, 1 - c_), me).wait_recv()
        for cp in first() + passed():
            cp.wait_send()
        mine().wait()

    return start, forward, finish


def _exchange_steps(x_ref, out_ref, send_sems, recv_sems, local_sem):
    x_, y_, c_ = lax.axis_index("x"), lax.axis_index("y"), lax.axis_index("c")
    me = 4 * x_ + 2 * y_ + c_

    def mine():
        return pltpu.make_async_copy(x_ref.at[me], out_ref.at[me], local_sem)

    def copies():
        out = []
        for r in range(1, N_DEV):
            px = 1 - x_ if r & 4 else x_
            py = 1 - y_ if r & 2 else y_
            pc = 1 - c_ if r & 1 else c_
            out.append(pltpu.make_async_remote_copy(
                src_ref=x_ref.at[4 * px + 2 * py + pc], dst_ref=out_ref.at[me],
                send_sem=send_sems.at[r - 1], recv_sem=recv_sems.at[r - 1], device_id=(px, py, pc), device_id_type=MESH))
        return out

    def start():
        mine().start()
        for cp in copies():
            cp.start()

    def finish():
        for cp in copies():
            cp.wait_recv()
        for cp in copies():
            cp.wait_send()
        mine().wait()

    return start, finish


def _all_gather(x, name):
    def body(x_ref, out_ref, send_sems, recv_sems, local_sem):
        for step in _gather_steps(x_ref, out_ref, send_sems, recv_sems, local_sem):
            step()

    hbm = pl.BlockSpec(memory_space=pl.ANY)
    return pl.pallas_call(body, name=name, out_shape=SDS((N_DEV,) + x.shape, x.dtype), in_specs=[hbm], out_specs=hbm,
                          scratch_shapes=_COMM_SEMAPHORES, compiler_params=pltpu.CompilerParams(has_side_effects=True))(x)


def _all_to_all(x, name):
    def body(x_ref, out_ref, send_sems, recv_sems, local_sem):
        for step in _exchange_steps(x_ref, out_ref, send_sems, recv_sems, local_sem):
            step()

    hbm = pl.BlockSpec(memory_space=pl.ANY)
    return pl.pallas_call(body, name=name, out_shape=SDS(x.shape, x.dtype), in_specs=[hbm], out_specs=hbm,
                          scratch_shapes=_COMM_SEMAPHORES, compiler_params=pltpu.CompilerParams(has_side_effects=True))(x)


BIG = ("w_in", "cv_pw_w", "w_out", "x_wq", "x_wk", "x_wv", "x_wo", "ffn_w_up", "ffn_w_down")
_NON_MIXER = (("ffn_w_up", 704), ("ffn_w_down", 352), ("x_wq", 128), ("x_wk", 128), ("x_wv", 128), ("x_wo", 128))
GROUPS = {
    "a": (("w_in", 0, 352), ("w_out", 0, 128)),
    "b": tuple((n, 0, r) for n, r in _NON_MIXER) + (("w_in", 1, 352), ("w_out", 1, 128)),
    "c": tuple((n, 1, r) for n, r in _NON_MIXER),
}
TRANSPOSED = ("w_in", "ffn_w_up")
PW_ROWS = 16
ADAMW_ROW_BLOCK = 32


def _group_rows(group):
    out, first = {}, 0
    for n, l, r in GROUPS[group]:
        out[(n, l)] = (first, r)
        first += r
    return out


def _where_is(name, layer):
    for group in GROUPS:
        rows = _group_rows(group)
        if (name, layer) in rows:
            return (group,) + rows[(name, layer)]
    raise KeyError((name, layer))


def _pack_weights(group, wts):
    pieces = []
    for n, l, _ in GROUPS[group]:
        w = wts[n][l].astype(BF16)
        pieces.append(w.T if n in TRANSPOSED else w)
    if group == "a":
        pieces.append(wts["cv_pw_w"].astype(BF16).reshape(PW_ROWS, PAYLOAD_COLS))
    return jnp.concatenate(pieces, axis=0)


def _unpack_weights(group, gathered):
    out = {}
    half = N_DEV // 2
    for (n, l), (first, r) in _group_rows(group).items():
        block = gathered[:, first:first + r, :]
        if n == "ffn_w_up":
            out[(n, l)] = (block[:half].reshape(half * r, PAYLOAD_COLS), block[half:].reshape(half * r, PAYLOAD_COLS))
        else:
            out[(n, l)] = block.reshape(N_DEV * r, PAYLOAD_COLS)
    return out


def _pack_grads(group, grads):
    pieces = []
    for n, l, r in GROUPS[group]:
        g = grads[n][l]
        parts = g if isinstance(g, tuple) else (g,)
        pieces.append(jnp.concatenate([p.reshape(-1, r, PAYLOAD_COLS) for p in parts], axis=0))
    if group == "a":
        pieces.append(_to_shards("cv_pw_w", jnp.stack(grads["cv_pw_w"])).reshape(N_DEV, PW_ROWS, PAYLOAD_COLS))
    return jnp.concatenate(pieces, axis=1)


COL_SHARDED = ("w_in", "ffn_w_up", "cv_w", "ffn_conv_w")
SMALL_SHARDED = ("cv_w", "ffn_conv_w")
REPLICATED = ("mix_norm_pre", "cv_b", "cv_ln_g", "cv_ln_b", "cv_pw_b", "mix_norm_post", "x_norm_pre", "mem_norm",
              "x_norm_post", "ffn_norm_pre", "ffn_conv_b", "ffn_norm_post")
WEIGHTS = ("mix_norm_pre", "w_in", "cv_w", "cv_b", "cv_ln_g", "cv_ln_b", "cv_pw_w", "cv_pw_b", "w_out", "mix_norm_post",
           "x_norm_pre", "mem_norm", "x_wq", "x_wk", "x_wv", "x_wo", "x_norm_post", "ffn_norm_pre", "ffn_w_up",
           "ffn_conv_w", "ffn_conv_b", "ffn_w_down", "ffn_norm_post")
PAYLOAD_COLS = 1024


PAYLOAD_ROW_TILE = 16


def _pad_rows(flat, cols):
    n = flat.shape[-1]
    rows = -(-n // (cols * PAYLOAD_ROW_TILE)) * PAYLOAD_ROW_TILE
    pad = rows * cols - n
    if pad:
        flat = jnp.concatenate([flat, jnp.zeros(flat.shape[:-1] + (pad,), flat.dtype)], axis=-1)
    return flat.reshape(flat.shape[:-1] + (rows, cols))


def _unshard(name, parts):
    n, depth, r, c = parts.shape
    if name in COL_SHARDED:
        return parts.transpose(1, 2, 0, 3).reshape(depth, r, n * c)
    return parts.transpose(1, 0, 2, 3).reshape(depth, n * r, c)


def _to_shards(name, full):
    depth, r, c = full.shape
    if name in COL_SHARDED:
        return full.reshape(depth, r, N_DEV, c // N_DEV).transpose(2, 0, 1, 3).reshape(N_DEV, -1)
    return full.reshape(depth, N_DEV, r // N_DEV, c).transpose(1, 0, 2, 3).reshape(N_DEV, -1)


def _to_sub(x, dil):
    s_len, w = x.shape
    h = w // HEAD_DIM
    return x.reshape(s_len // dil, dil, h, HEAD_DIM).transpose(1, 2, 0, 3).reshape(dil * h, s_len // dil, HEAD_DIM)


def _from_sub(x, dil):
    g, l, hd = x.shape
    h = g // dil
    return x.reshape(dil, h, l, hd).transpose(2, 0, 1, 3).reshape(l * dil, h * hd)


def _heads_major(x, h):
    return x.reshape(x.shape[0], h, HEAD_DIM).transpose(1, 0, 2)


def _tokens_major(x):
    return x.transpose(1, 0, 2).reshape(x.shape[1], -1)


def _ffn_halves(p):
    w, b = p["ffn_conv_w"], p["ffn_conv_b"]
    return w[:, :D_FF], w[:, D_FF:], b[:, :D_FF], b[:, D_FF:]


def _layer_fwd(l, h, hn, p, mem, cos, sin, g_next, payload, unpack):
    p = dict(p)
    sv = {"h0": h, "hn0": hn}
    u = _mm(hn, p["w_in"], "nt", F32, f"l{l}_in_proj")
    sv["u"] = u
    sb = _heads_major(u[:, :3 * SB_WIDTH].astype(BF16), 3 * SB_HEADS)
    sb_q, sb_k, sb_v = sb[:SB_HEADS], sb[SB_HEADS:2 * SB_HEADS], sb[2 * SB_HEADS:]
    a_out, sb_tot, gathered = _sb_fwd(sb_q, sb_k, sb_v, payload, f"l{l}_sb_fwd")
    p.update(unpack(gathered))
    sv.update(sb_q=sb_q, sb_k=sb_k, sb_v=sb_v, sb_tot=sb_tot, p=p)

    cv_s, cv_c = _cv_fwd(u, p["cv_w"], p["cv_b"], p["cv_ln_g"], p["cv_ln_b"], f"l{l}_cv_fwd")
    b_out = _mm(cv_s, p["cv_pw_w"], "nn", BF16, f"l{l}_cv_pw", bias=p["cv_pw_b"])
    sv.update(cv_s=cv_s, cv_c=cv_c)

    qk = _rope_fwd(u, cos, sin, f"l{l}_rope_fwd")
    dl_v = u[:, IN_WIDTH - DL_WIDTH:].astype(BF16)
    outs, lses, subs = [], [], []
    for b, (_, dil) in enumerate(DL_PATTERN):
        qs, ks, vs = _to_sub(qk[:, :DL_WIDTH], dil), _to_sub(qk[:, DL_WIDTH:], dil), _to_sub(dl_v, dil)
        o, lse = _dl_fwd(qs, ks, vs, f"l{l}_dl{b}_fwd")
        subs.append((qs, ks, vs, lse))
        outs.append(_from_sub(o, dil))
        lses.append(_from_sub(lse, dil))
    c_out, c_out_f32, w1, w2, w3 = _dl_mix_fwd(outs, lses, f"l{l}_dl_mix")
    sv.update(dl_subs=subs, dl_o=c_out_f32, dl_w=(w1, w2, w3))

    mix = jnp.concatenate([_tokens_major(a_out), b_out, c_out], axis=-1)
    y = _mm(mix, p["w_out"], "nn", F32, f"l{l}_out_proj")
    h1, hn1 = _res_norm_fwd(h, y, p["mix_norm_post"], p["x_norm_pre"], f"l{l}_mix_post")
    sv.update(mix=mix, y_mix=y, h1=h1, hn1=hn1)

    xq = _mm(hn1, p["x_wq"], "nn", BF16, f"l{l}_xq")
    memn = _rms_fwd(mem, p["mem_norm"], f"l{l}_mem_norm")
    xk = _mm(memn, p["x_wk"], "nn", BF16, f"l{l}_xk")
    xv = _mm(memn, p["x_wv"], "nn", BF16, f"l{l}_xv")
    xo = _xattn_fwd(xq, xk, xv, f"l{l}_xattn_fwd")
    y = _mm(xo, p["x_wo"], "nn", F32, f"l{l}_xo_proj")
    h2, hn2 = _res_norm_fwd(h1, y, p["x_norm_post"], p["ffn_norm_pre"], f"l{l}_x_post")
    sv.update(xq=xq, xk=xk, xv=xv, xo=xo, memn=memn, y_x=y, h2=h2, hn2=hn2)

    up_g = _mm(hn2, p["ffn_w_up"][0], "nt", F32, f"l{l}_ffn_up_gate")
    up_v = _mm(hn2, p["ffn_w_up"][1], "nt", F32, f"l{l}_ffn_up_val")
    act = _ffn_act_fwd(up_g, up_v, *_ffn_halves(p), f"l{l}_ffn_act")
    y = _mm(act, p["ffn_w_down"], "nn", F32, f"l{l}_ffn_down")
    h3, hn3 = _res_norm_fwd(h2, y, p["ffn_norm_post"], g_next, f"l{l}_ffn_post")
    sv.update(up_g=up_g, up_v=up_v, act=act, y_ffn=y)
    return h3, hn3, sv, gathered


def _layer_bwd(l, dh, dy, sv, mem, cos, sin, prev_post, make_payload):
    p = sv["p"]
    gr = {}
    dact = _mm(dy, p["ffn_w_down"], "nt", F32, f"l{l}_d_act")
    gr["ffn_w_down"] = _mm(sv["act"], dy, "tn", BF16, f"l{l}_dw_down")
    dup_g, dup_v, dwg, dwv, dbg, dbv = _ffn_act_bwd(sv["up_g"], sv["up_v"], dact, *_ffn_halves(p), f"l{l}_ffn_act_bwd")
    gr["ffn_conv_w"] = jnp.concatenate([dwg, dwv], axis=1)
    gr["ffn_conv_b"] = jnp.concatenate([dbg, dbv], axis=1)
    dhn = (_mm(dup_g, p["ffn_w_up"][0], "nn", F32, f"l{l}_d_hn2_gate"), _mm(dup_v, p["ffn_w_up"][1], "nn", F32, f"l{l}_d_hn2_val"))
    gr["ffn_w_up"] = (_mm(dup_g, sv["hn2"], "tn", BF16, f"l{l}_dw_up_gate"), _mm(dup_v, sv["hn2"], "tn", BF16, f"l{l}_dw_up_val"))
    dh, dy, gr["ffn_norm_pre"], gr["x_norm_post"] = _norm_bwd(
        dh, (sv["h2"], p["ffn_norm_pre"], dhn), (sv["y_x"], p["x_norm_post"]), f"l{l}_x_post_bwd")

    do = _mm(dy, p["x_wo"], "nt", BF16, f"l{l}_d_xo")
    gr["x_wo"] = _mm(sv["xo"], dy, "tn", BF16, f"l{l}_dw_xo")
    dq, dk, dv = _xattn_bwd(sv["xq"], sv["xk"], sv["xv"], do, f"l{l}_xattn_bwd")
    dhn = _mm(dq, p["x_wq"], "nt", F32, f"l{l}_d_hn1")
    gr["x_wq"] = _mm(sv["hn1"], dq, "tn", BF16, f"l{l}_dw_xq")
    gr["x_wk"] = _mm(sv["memn"], dk, "tn", BF16, f"l{l}_dw_xk")
    gr["x_wv"] = _mm(sv["memn"], dv, "tn", BF16, f"l{l}_dw_xv")
    dmemn = _mm(dk, p["x_wk"], "nt", F32, f"l{l}_d_memn_k") + _mm(dv, p["x_wv"], "nt", F32, f"l{l}_d_memn_v")
    gr["mem_norm"] = _rms_gain_grad(mem, p["mem_norm"], dmemn, f"l{l}_mem_norm_bwd")
    dh, dy, gr["x_norm_pre"], gr["mix_norm_post"] = _norm_bwd(
        dh, (sv["h1"], p["x_norm_pre"], dhn), (sv["y_mix"], p["mix_norm_post"]), f"l{l}_mix_post_bwd")

    dmix = _mm(dy, p["w_out"], "nt", F32, f"l{l}_d_mix")
    gr["w_out"] = _mm(sv["mix"], dy, "tn", BF16, f"l{l}_dw_out")
    do_a = _heads_major(dmix[:, :SB_WIDTH].astype(BF16), SB_HEADS)
    dq, dk, dv, received = _sb_bwd(sv["sb_q"], sv["sb_k"], sv["sb_v"], do_a, sv["sb_tot"], make_payload(gr), f"l{l}_sb_bwd")
    du_sb = _tokens_major(jnp.concatenate([dq, dk, dv], axis=0))

    db_out = dmix[:, SB_WIDTH:SB_WIDTH + CV_WIDTH]
    ds = _mm(db_out, p["cv_pw_w"], "nt", F32, f"l{l}_d_cv_s")
    gr["cv_pw_w"] = _mm(sv["cv_s"], db_out, "tn", BF16, f"l{l}_dw_cv_pw")
    du_cv, dcvw, gr["cv_b"], gr["cv_ln_g"], gr["cv_ln_b"], gr["cv_pw_b"] = _cv_bwd(
        sv["u"], sv["cv_c"], ds, db_out, p["cv_w"], p["cv_ln_g"], p["cv_ln_b"], f"l{l}_cv_bwd")
    gr["cv_w"] = dcvw[:CV_KERNEL]

    dc_out = dmix[:, SB_WIDTH + CV_WIDTH:]
    dqs, dks, dvs = [], [], []
    for b, (_, dil) in enumerate(DL_PATTERN):
        qs, ks, vs, lse = sv["dl_subs"][b]
        dq, dk, dv = _dl_bwd(qs, ks, vs, _to_sub(dc_out, dil), _to_sub(sv["dl_o"], dil), _to_sub(sv["dl_w"][b], dil), lse,
                             f"l{l}_dl{b}_bwd")
        dqs.append(_from_sub(dq, dil))
        dks.append(_from_sub(dk, dil))
        dvs.append(_from_sub(dv, dil))
    du_dl = _rope_bwd(dqs, dks, dvs, cos, sin, f"l{l}_rope_bwd")

    du = jnp.concatenate([du_sb, du_cv, du_dl], axis=-1)
    dhn = _mm(du, p["w_in"], "nn", F32, f"l{l}_d_hn0")
    gr["w_in"] = _mm(du, sv["hn0"], "tn", BF16, f"l{l}_dw_in")
    dh, dy, gr["mix_norm_pre"], dg_prev = _norm_bwd(dh, (sv["h0"], p["mix_norm_pre"], dhn), prev_post, f"l{l}_in_bwd")
    return dh, dy, gr, dg_prev, received


def kernel(x, mem, positions, mix_norm_pre, w_in, cv_w, cv_b, cv_ln_g, cv_ln_b, cv_pw_w, cv_pw_b, w_out, mix_norm_post, x_norm_pre, mem_norm, x_wq, x_wk, x_wv, x_wo, x_norm_post, ffn_norm_pre, ffn_w_up, ffn_conv_w, ffn_conv_b, ffn_w_down, ffn_norm_post, loss_target, m_mix_norm_pre, m_w_in, m_cv_w, m_cv_b, m_cv_ln_g, m_cv_ln_b, m_cv_pw_w, m_cv_pw_b, m_w_out, m_mix_norm_post, m_x_norm_pre, m_mem_norm, m_x_wq, m_x_wk, m_x_wv, m_x_wo, m_x_norm_post, m_ffn_norm_pre, m_ffn_w_up, m_ffn_conv_w, m_ffn_conv_b, m_ffn_w_down, m_ffn_norm_post, v_mix_norm_pre, v_w_in, v_cv_w, v_cv_b, v_cv_ln_g, v_cv_ln_b, v_cv_pw_w, v_cv_pw_b, v_w_out, v_mix_norm_post, v_x_norm_pre, v_mem_norm, v_x_wq, v_x_wk, v_x_wv, v_x_wo, v_x_norm_post, v_ffn_norm_pre, v_ffn_w_up, v_ffn_conv_w, v_ffn_conv_b, v_ffn_w_down, v_ffn_norm_post):
    args = locals()
    wts = {n: args[n] for n in WEIGHTS}
    mom = {n: args["m_" + n] for n in WEIGHTS}
    var = {n: args["v_" + n] for n in WEIGHTS}

    x2, mem2, target = x[0], mem[0], loss_target[0]

    gathered_a = _all_gather(_pack_weights("a", wts), "weights_all_gather")
    small_payload = _pad_rows(jnp.concatenate([wts[n].reshape(-1) for n in SMALL_SHARDED]), PAYLOAD_COLS)
    small = _all_gather(small_payload, "small_weights_all_gather").reshape(N_DEV, -1)
    small_full = {}
    off = 0
    for n in SMALL_SHARDED:
        size = wts[n].size
        small_full[n] = _unshard(n, small[:, off:off + size].reshape((N_DEV,) + wts[n].shape))
        off += size
    pw_first = sum(r for _, _, r in GROUPS["a"])
    pw_full = _unshard("cv_pw_w", gathered_a[:, pw_first:, :].reshape((N_DEV,) + wts["cv_pw_w"].shape))

    def mixer_params(l, unpacked):
        p = {n: wts[n][l][None, :] for n in REPLICATED}
        p.update({n: small_full[n][l] for n in SMALL_SHARDED})
        p.update(cv_pw_w=pw_full[l], w_in=unpacked[("w_in", l)], w_out=unpacked[("w_out", l)])
        return p

    def of_layer(group, l):
        return lambda gathered: {n: w for (n, ll), w in _unpack_weights(group, gathered).items() if ll == l}

    pos = positions[0].astype(F32)
    half = HEAD_DIM // 2
    inv_freq = ROPE_THETA ** (-jnp.arange(half, dtype=F32) / half)
    ang = pos[:, None] * inv_freq
    cos = jnp.tile(jnp.cos(ang), (1, LANES // half))
    sin = jnp.tile(jnp.sin(ang), (1, LANES // half))

    p0 = mixer_params(0, _unpack_weights("a", gathered_a))
    hn = _rms_fwd(x2, p0["mix_norm_pre"], "l0_in_norm")
    h, hn, sv0, gathered_b = _layer_fwd(0, x2, hn, p0, mem2, cos, sin, wts["mix_norm_pre"][1][None, :],
                                        _pack_weights("b", wts), of_layer("b", 0))
    p1 = mixer_params(1, _unpack_weights("b", gathered_b))
    h, _, sv1, _ = _layer_fwd(1, h, hn, p1, mem2, cos, sin, None, _pack_weights("c", wts), of_layer("c", 1))
    loss_part, dh = _loss_fwd(h, target, "loss")
    loss = lax.psum(loss_part[0, 0], ("x", "y", "c"))

    grads = {n: [None] * DEPTH for n in WEIGHTS}
    dh, dy, _, grads["ffn_norm_post"][1] = _norm_bwd(dh, None, (sv1["y_ffn"], sv1["p"]["ffn_norm_post"]), "last_post_bwd")

    def payload_c(gr):
        return _pack_grads("c", {n: {1: g} for n, g in gr.items()})

    dh, dy, gr, grads["ffn_norm_post"][0], received_c = _layer_bwd(
        1, dh, dy, sv1, mem2, cos, sin, (sv0["y_ffn"], sv0["p"]["ffn_norm_post"]), payload_c)
    for n, g in gr.items():
        grads[n][1] = g

    def payload_b(gr):
        both = {n: {0: g} for n, g in gr.items()}
        both.update(w_in={1: grads["w_in"][1]}, w_out={1: grads["w_out"][1]})
        return _pack_grads("b", both)

    dh, _, gr, _, received_b = _layer_bwd(0, dh, dy, sv0, mem2, cos, sin, None, payload_b)
    for n, g in gr.items():
        grads[n][0] = g
    grad_x = dh

    received_a = _all_to_all(_pack_grads("a", grads), "grads_all_to_all")
    received = {"a": received_a, "b": received_b, "c": received_c}
    small_rows = jnp.concatenate([_to_shards(n, jnp.stack(grads[n])) for n in SMALL_SHARDED], axis=1)
    rep_flat = jnp.concatenate([jnp.stack([g.reshape(-1) for g in grads[n]]).reshape(-1) for n in REPLICATED])
    rep_rows = jnp.broadcast_to(rep_flat[None], (N_DEV, rep_flat.shape[0]))
    f32_rows = _pad_rows(jnp.concatenate([small_rows, rep_rows], axis=1), PAYLOAD_COLS)
    small_parts = _all_to_all(f32_rows, "small_grads_all_to_all")

    res = {}
    for n in BIG:
        shape = wts[n].shape
        two_d = (shape[0] * shape[1], shape[2])
        operands = (wts[n].reshape(two_d), mom[n].reshape(two_d), var[n].reshape(two_d))
        if n == "cv_pw_w":
            outs = _adamw(received_a[:, pw_first:, :].reshape((N_DEV,) + two_d), *operands, f"adamw_{n}")
        elif n in TRANSPOSED:
            layers = []
            for l in range(DEPTH):
                group, first, r = _where_is(n, l)
                layers.append(received[group][:, first:first + r, :])
            parts = jnp.stack(layers, axis=1).transpose(0, 1, 3, 2).reshape((N_DEV,) + two_d)
            outs = _adamw(parts, *operands, f"adamw_{n}")
        else:
            sources = []
            for l in range(DEPTH):
                group, first, _ = _where_is(n, l)
                sources.append((received[group], first))
            outs = _adamw_packed(sources, *operands, f"adamw_{n}")
        res[n] = [o.reshape(shape) for o in outs]
    small_names = SMALL_SHARDED + REPLICATED
    flat_w = _pad_rows(jnp.concatenate([wts[n].reshape(-1) for n in small_names]), PAYLOAD_COLS)
    flat_m = _pad_rows(jnp.concatenate([mom[n].reshape(-1) for n in small_names]), PAYLOAD_COLS)
    flat_v = _pad_rows(jnp.concatenate([var[n].reshape(-1) for n in small_names]), PAYLOAD_COLS)
    outs = _adamw(small_parts, flat_w, flat_m, flat_v, "adamw_small")
    outs = [o.reshape(-1) for o in outs]
    off = 0
    for n in small_names:
        size = wts[n].size
        res[n] = [o[off:off + size].reshape(wts[n].shape) for o in outs]
        off += size

    result = [loss, grad_x[None]]
    for kind in range(4):
        result += [res[n][kind] for n in WEIGHTS]
    return tuple(result)
```

```python
import functools
import math

import jax
import jax.numpy as jnp
from jax import lax
from jax.experimental import pallas as pl
from jax.experimental.pallas import tpu as pltpu

F32, BF16 = jnp.float32, jnp.bfloat16
SDS = jax.ShapeDtypeStruct

D_MODEL = 1024
SEQ = 4096
DEPTH = 2
HEAD_DIM = 64
SB_HEADS = 4
SB_WIDTH = 256
CV_WIDTH = 256
CV_KERNEL = 31
DL_HEADS = 8
DL_WIDTH = 512
IN_WIDTH = 2816
DL_PATTERN = ((128, 1), (512, 4), (2048, 16))
BLOCK = 128
ROPE_THETA = 10000.0
N_MEM = 256
X_HEADS = 4
X_HEAD_DIM = 256
D_FF = 2816
EPS = 1e-6
N_DEV = 8
LANES = 128

ADAM_LR = 0.001
ADAM_B1 = 0.9
ADAM_B2 = 0.999
ADAM_EPS = 1e-08
ADAM_WD = 0.01
ADAM_STEP = 10

VMEM_LIMIT_BYTES = 56 * 1024 * 1024
MESH = pl.DeviceIdType.MESH
NEG = -1e30


def _params(**kw):
    return pltpu.CompilerParams(vmem_limit_bytes=VMEM_LIMIT_BYTES, **kw)


def _pick(n, cands):
    for c in cands:
        if n % c == 0:
            return c
    return n


def _mm(a, b, mode, out_dtype, name, bias=None):
    if mode == "nn":
        (m, k), (k2, n) = a.shape, b.shape
    elif mode == "nt":
        (m, k), (n, k2) = a.shape, b.shape
    else:
        (k, m), (k2, n) = a.shape, b.shape
    assert k == k2, (a.shape, b.shape, mode)
    tm = _pick(m, (1024, 1408, 512, 256, 128))
    tn = _pick(n, (1024, 1408, 512, 256, 128))
    tk = k if k <= 2048 else _pick(k, (2048, 1408, 1024, 512))
    nk = k // tk
    dims = {"nn": ((1,), (0,)), "nt": ((1,), (1,)), "tn": ((0,), (0,))}[mode]

    def body(*refs):
        refs = list(refs)
        acc_ref = refs.pop() if nk > 1 else None
        a_ref, b_ref = refs[0], refs[1]
        bias_ref = refs[2] if bias is not None else None
        o_ref = refs[-1]
        p = lax.dot_general(a_ref[...].astype(BF16), b_ref[...].astype(BF16), (dims, ((), ())),
                            preferred_element_type=F32)

        def finish(v):
            if bias_ref is not None:
                v = v + bias_ref[...]
            o_ref[...] = v.astype(out_dtype)

        if nk == 1:
            finish(p)
        else:
            kk = pl.program_id(2)

            @pl.when(kk == 0)
            def _():
                acc_ref[...] = p

            @pl.when(kk > 0)
            def _():
                acc_ref[...] += p

            @pl.when(kk == nk - 1)
            def _():
                finish(acc_ref[...])

    a_spec = pl.BlockSpec((tk, tm), lambda i, j, kk: (kk, i)) if mode == "tn" else pl.BlockSpec((tm, tk), lambda i, j, kk: (i, kk))
    b_spec = pl.BlockSpec((tn, tk), lambda i, j, kk: (j, kk)) if mode == "nt" else pl.BlockSpec((tk, tn), lambda i, j, kk: (kk, j))
    in_specs = [a_spec, b_spec]
    args = [a, b]
    if bias is not None:
        in_specs.append(pl.BlockSpec((1, tn), lambda i, j, kk: (0, j)))
        args.append(bias)
    return pl.pallas_call(
        body, name=name, out_shape=SDS((m, n), out_dtype), grid=(m // tm, n // tn, nk),
        in_specs=in_specs, out_specs=pl.BlockSpec((tm, tn), lambda i, j, kk: (i, j)),
        scratch_shapes=[pltpu.VMEM((tm, tn), F32)] if nk > 1 else [], compiler_params=_params(),
    )(*args)


def _rms(x, g):
    r = lax.rsqrt(jnp.mean(x * x, axis=-1, keepdims=True) + EPS)
    return x * r * g


def _rms_bwd(x, g, dy):
    r = lax.rsqrt(jnp.mean(x * x, axis=-1, keepdims=True) + EPS)
    xh = x * r
    dyg = dy * g
    dx = r * (dyg - xh * jnp.mean(dyg * xh, axis=-1, keepdims=True))
    return dx, dy * xh


def _rms_fwd(x, g, name):
    rows, d = x.shape
    t = min(rows, 512)

    def body(x_ref, g_ref, o_ref):
        o_ref[...] = _rms(x_ref[...], g_ref[...]).astype(BF16)

    return pl.pallas_call(
        body, name=name, out_shape=SDS((rows, d), BF16), grid=(rows // t,),
        in_specs=[pl.BlockSpec((t, d), lambda i: (i, 0)), pl.BlockSpec((1, d), lambda i: (0, 0))],
        out_specs=pl.BlockSpec((t, d), lambda i: (i, 0)), compiler_params=_params(),
    )(x, g)


def _res_norm_fwd(h, y, g_post, g_next, name):
    rows, d = h.shape
    t = 512
    has_next = g_next is not None

    def body(*refs):
        if has_next:
            h_ref, y_ref, gp_ref, gn_ref, h1_ref, hn_ref = refs
        else:
            h_ref, y_ref, gp_ref, h1_ref = refs
        h1 = h_ref[...] + _rms(y_ref[...], gp_ref[...])
        h1_ref[...] = h1
        if has_next:
            hn_ref[...] = _rms(h1, gn_ref[...]).astype(BF16)

    row = pl.BlockSpec((t, d), lambda i: (i, 0))
    vec = pl.BlockSpec((1, d), lambda i: (0, 0))
    in_specs = [row, row, vec] + ([vec] if has_next else [])
    args = [h, y, g_post] + ([g_next] if has_next else [])
    out_shape = [SDS((rows, d), F32)] + ([SDS((rows, d), BF16)] if has_next else [])
    out_specs = [row] + ([row] if has_next else [])
    res = pl.pallas_call(body, name=name, out_shape=out_shape, grid=(rows // t,), in_specs=in_specs,
                         out_specs=out_specs, compiler_params=_params())(*args)
    return (res[0], res[1]) if has_next else (res[0], None)


def _norm_bwd(dh, pre, post, name):
    rows, d = dh.shape
    t = 512
    has_pre, has_post = pre is not None, post is not None
    if has_pre:
        dhns = pre[2] if isinstance(pre[2], tuple) else (pre[2],)
        pre = (pre[0], pre[1]) + dhns

    def body(*refs):
        refs = list(refs)
        dh_ref = refs.pop(0)
        if has_pre:
            h_ref, gpre_ref = refs.pop(0), refs.pop(0)
            dhn_refs = [refs.pop(0) for _ in dhns]
        if has_post:
            y_ref, gpost_ref = refs.pop(0), refs.pop(0)
        dht_ref = refs.pop(0)
        if has_post:
            dy_ref = refs.pop(0)
        if has_pre:
            dgpre_ref = refs.pop(0)
        if has_post:
            dgpost_ref = refs.pop(0)
        i = pl.program_id(0)
        dht = dh_ref[...]
        if has_pre:
            dhn = dhn_refs[0][...]
            for r in dhn_refs[1:]:
                dhn = dhn + r[...]
            dx, dgr = _rms_bwd(h_ref[...], gpre_ref[...], dhn)
            dht = dht + dx

            @pl.when(i == 0)
            def _():
                dgpre_ref[...] = jnp.zeros_like(dgpre_ref)

            dgpre_ref[...] += jnp.sum(dgr, axis=0, keepdims=True)
        dht_ref[...] = dht
        if has_post:
            dy, dgr = _rms_bwd(y_ref[...], gpost_ref[...], dht)
            dy_ref[...] = dy.astype(BF16)

            @pl.when(i == 0)
            def _():
                dgpost_ref[...] = jnp.zeros_like(dgpost_ref)

            dgpost_ref[...] += jnp.sum(dgr, axis=0, keepdims=True)

    row = pl.BlockSpec((t, d), lambda i: (i, 0))
    vec = pl.BlockSpec((1, d), lambda i: (0, 0))
    in_specs, args = [row], [dh]
    if has_pre:
        in_specs += [row, vec] + [row] * len(dhns)
        args += list(pre)
    if has_post:
        in_specs += [row, vec]
        args += list(post)
    out_shape, out_specs = [SDS((rows, d), F32)], [row]
    if has_post:
        out_shape.append(SDS((rows, d), BF16))
        out_specs.append(row)
    if has_pre:
        out_shape.append(SDS((1, d), F32))
        out_specs.append(vec)
    if has_post:
        out_shape.append(SDS((1, d), F32))
        out_specs.append(vec)
    res = list(pl.pallas_call(body, name=name, out_shape=out_shape, grid=(rows // t,), in_specs=in_specs,
                              out_specs=out_specs, compiler_params=_params())(*args))
    dht = res.pop(0)
    dy = res.pop(0) if has_post else None
    dgpre = res.pop(0) if has_pre else None
    dgpost = res.pop(0) if has_post else None
    return dht, dy, dgpre, dgpost


def _rms_gain_grad(x, g, dy, name):
    rows, d = x.shape

    def body(x_ref, g_ref, dy_ref, dg_ref):
        _, dgr = _rms_bwd(x_ref[...], g_ref[...], dy_ref[...])
        dg_ref[...] = jnp.sum(dgr, axis=0, keepdims=True)

    return pl.pallas_call(body, name=name, out_shape=SDS((1, d), F32), compiler_params=_params())(x, g, dy)


def _loss_fwd(h, target, name):
    rows, d = h.shape
    t = 512

    def body(h_ref, t_ref, loss_ref, dh_ref):
        i = pl.program_id(0)
        err = h_ref[...] - t_ref[...]
        dh_ref[...] = err * (1.0 / d)

        @pl.when(i == 0)
        def _():
            loss_ref[...] = jnp.zeros_like(loss_ref)

        part = jnp.sum(jnp.sum(err * err, axis=1, keepdims=True), axis=0, keepdims=True) * (0.5 / d)
        loss_ref[...] += jnp.broadcast_to(part, loss_ref.shape)

    row = pl.BlockSpec((t, d), lambda i: (i, 0))
    return pl.pallas_call(
        body, name=name, out_shape=(SDS((1, LANES), F32), SDS((rows, d), F32)), grid=(rows // t,),
        in_specs=[row, row], out_specs=(pl.BlockSpec((1, LANES), lambda i: (0, 0)), row), compiler_params=_params(),
    )(h, target)


def _rot_half(x, sign):
    w = x.shape[-1]
    lane = lax.broadcasted_iota(jnp.int32, x.shape, 1)
    first = (lane % HEAD_DIM) < (HEAD_DIM // 2)
    return jnp.where(first, -sign * pltpu.roll(x, w - HEAD_DIM // 2, axis=1), sign * pltpu.roll(x, HEAD_DIM // 2, axis=1))


def _rope_fwd(u, cos, sin, name):
    rows = u.shape[0]
    t, cw = 512, 256
    first_col = (3 * SB_WIDTH + 2 * CV_WIDTH) // cw

    def body(u_ref, c_ref, s_ref, o_ref):
        x = u_ref[...]
        c = jnp.tile(c_ref[...], (1, cw // LANES))
        s = jnp.tile(s_ref[...], (1, cw // LANES))
        o_ref[...] = x * c + _rot_half(x, 1.0) * s

    tab = pl.BlockSpec((t, LANES), lambda i, j: (i, 0))
    return pl.pallas_call(
        body, name=name, out_shape=SDS((rows, 2 * DL_WIDTH), F32), grid=(rows // t, 2 * DL_WIDTH // cw),
        in_specs=[pl.BlockSpec((t, cw), lambda i, j: (i, first_col + j)), tab, tab],
        out_specs=pl.BlockSpec((t, cw), lambda i, j: (i, j)), compiler_params=_params(),
    )(u, cos, sin)


def _rope_bwd(dqs, dks, dvs, cos, sin, name):
    rows = dqs[0].shape[0]
    t, w = 256, DL_WIDTH

    def body(*refs):
        c = jnp.tile(refs[9][...], (1, w // LANES))
        s = jnp.tile(refs[10][...], (1, w // LANES))
        o_ref = refs[11]
        dq = refs[0][...] + refs[1][...] + refs[2][...]
        dk = refs[3][...] + refs[4][...] + refs[5][...]
        dv = refs[6][...] + refs[7][...] + refs[8][...]
        o_ref[:, 0:w] = (dq * c + _rot_half(dq, -1.0) * s).astype(BF16)
        o_ref[:, w:2 * w] = (dk * c + _rot_half(dk, -1.0) * s).astype(BF16)
        o_ref[:, 2 * w:3 * w] = dv.astype(BF16)

    row = pl.BlockSpec((t, w), lambda i: (i, 0))
    tab = pl.BlockSpec((t, LANES), lambda i: (i, 0))
    return pl.pallas_call(
        body, name=name, out_shape=SDS((rows, 3 * w), BF16), grid=(rows // t,), in_specs=[row] * 9 + [tab, tab],
        out_specs=pl.BlockSpec((t, 3 * w), lambda i: (i, 0)), compiler_params=_params(),
    )(*dqs, *dks, *dvs, cos, sin)


SB_TILE = 256


def _softplus(z):
    return jnp.maximum(z, 0.0) + jnp.log(1.0 + jnp.exp(-jnp.abs(z)))


def _split_dot(x, tri, passes):
    acc = None
    rem = x
    for _ in range(passes):
        part = rem.astype(BF16)
        rem = rem - part.astype(F32)
        d = jnp.dot(part, tri, preferred_element_type=F32)
        acc = d if acc is None else acc + d
    return acc


def _tri(t, rel):
    j = lax.broadcasted_iota(jnp.int32, (t, t), 0)
    s = lax.broadcasted_iota(jnp.int32, (t, t), 1)
    return rel(j, s).astype(BF16)


def _sb_fwd(q, k, v, payload, name):
    h, s_len, hd = q.shape
    t = SB_TILE
    nq = s_len // t
    scale = hd ** -0.5

    def body(q_ref, k_ref, v_ref, pay_ref, o_ref, tot_ref, gathered_ref, send_sems, recv_sems, local_sem):
        hh, i = pl.program_id(0), pl.program_id(1)
        start, forward, finish = _gather_steps(pay_ref, gathered_ref, send_sems, recv_sems, local_sem)
        pl.when((hh == 0) & (i == 0))(start)
        pl.when((hh == h - 1) & (i == 0))(forward)
        qv = q_ref[0] * scale
        upper = _tri(t, lambda j, s: j > s)

        def tiles(js, carry, diagonal):
            acc, run = carry
            starts = [pl.multiple_of(j * t, t) for j in js]
            zs = [lax.dot_general(qv, k_ref[0, pl.ds(st, t), :], (((1,), (1,)), ((), ())), preferred_element_type=F32)
                  for st in starts]
            sps = [_softplus(z) for z in zs]
            if diagonal:
                mask = lax.broadcasted_iota(jnp.int32, (t, t), 1) < lax.broadcasted_iota(jnp.int32, (t, t), 0)
                sps = [jnp.where(mask, sp, 0.0) for sp in sps]
            laters = [_split_dot(sp, upper, 2) for sp in sps]
            for st, z, sp, later in zip(starts, zs, sps, laters):
                a = jnp.exp((z - sp) - (run + later))
                if diagonal:
                    a = jnp.where(mask, a, 0.0)
                acc = acc + jnp.dot(a.astype(BF16), v_ref[0, pl.ds(st, t), :], preferred_element_type=F32)
                run = run + jnp.sum(sp, axis=1, keepdims=True)
            return acc, run

        def pair(pp, carry):
            j = i - 1 - 2 * pp
            return tiles([j, j - 1], carry, False)

        carry = tiles([i], (jnp.zeros((t, hd), F32), jnp.zeros((t, 1), F32)), True)
        carry = lax.fori_loop(0, i // 2, pair, carry)
        acc, run = lax.fori_loop(0, i % 2, lambda _, c: tiles([0], c, False), carry)
        o_ref[0] = acc.astype(BF16)
        tot_ref[0] = run
        pl.when((hh == h - 1) & (i == nq - 1))(finish)

    full = pl.BlockSpec((1, s_len, hd), lambda hh, i: (hh, 0, 0))
    tile = pl.BlockSpec((1, t, hd), lambda hh, i: (hh, i, 0))
    hbm = pl.BlockSpec(memory_space=pl.ANY)
    return pl.pallas_call(
        body, name=name,
        out_shape=(SDS((h, s_len, hd), BF16), SDS((h, s_len, 1), F32), SDS((N_DEV,) + payload.shape, payload.dtype)),
        grid=(h, nq), in_specs=[tile, full, full, hbm],
        out_specs=(tile, pl.BlockSpec((1, t, 1), lambda hh, i: (hh, i, 0)), hbm),
        scratch_shapes=_COMM_SEMAPHORES, compiler_params=_params(has_side_effects=True),
    )(q, k, v, payload)


def _sb_bwd(q, k, v, do, tot, payload, name):
    h, s_len, hd = q.shape
    t = SB_TILE
    nq = s_len // t
    scale = hd ** -0.5

    def body(q_ref, k_ref, v_ref, do_ref, tot_ref, pay_ref, dq_ref, dk_ref, dv_ref, received_ref, dk_acc, dv_acc,
             send_sems, recv_sems, local_sem):
        hh, i = pl.program_id(0), pl.program_id(1)
        start, finish = _exchange_steps(pay_ref, received_ref, send_sems, recv_sems, local_sem)
        pl.when((hh == 0) & (i == 0))(start)

        @pl.when(i == 0)
        def _():
            dk_acc[...] = jnp.zeros_like(dk_acc)
            dv_acc[...] = jnp.zeros_like(dv_acc)

        qv = q_ref[0] * scale
        dov = do_ref[0]
        total = tot_ref[0]
        upto = _tri(t, lambda j, s: j <= s)
        before = _tri(t, lambda j, s: j < s)
        nt_dims = (((1,), (1,)), ((), ()))
        tn_dims = (((0,), (0,)), ((), ()))

        def tiles(js, carry, diagonal):
            dq, run_sp, run_g = carry
            starts = [pl.multiple_of(j * t, t) for j in js]
            zs = [lax.dot_general(qv, k_ref[0, pl.ds(st, t), :], nt_dims, preferred_element_type=F32) for st in starts]
            das = [lax.dot_general(dov, v_ref[0, pl.ds(st, t), :], nt_dims, preferred_element_type=F32) for st in starts]
            sps = [_softplus(z) for z in zs]
            log_sigs = [z - sp for z, sp in zip(zs, sps)]
            if diagonal:
                mask = lax.broadcasted_iota(jnp.int32, (t, t), 1) < lax.broadcasted_iota(jnp.int32, (t, t), 0)
                sps = [jnp.where(mask, sp, 0.0) for sp in sps]
            pres = [_split_dot(sp, upto, 2) for sp in sps]
            a_s, gs = [], []
            for sp, log_sig, pre, da in zip(sps, log_sigs, pres, das):
                a = jnp.exp(log_sig - (total - (run_sp + pre)))
                if diagonal:
                    a = jnp.where(mask, a, 0.0)
                a_s.append(a)
                gs.append(a * da)
                run_sp = run_sp + jnp.sum(sp, axis=1, keepdims=True)
            g_pres = [_split_dot(g, before, 3) for g in gs]
            for st, a, g, g_pre, log_sig in zip(starts, a_s, gs, g_pres, log_sigs):
                sig = jnp.exp(log_sig)
                dz = g * (1.0 - sig) - sig * (run_g + g_pre)
                if diagonal:
                    dz = jnp.where(mask, dz, 0.0)
                dz = dz.astype(BF16)
                dq = dq + jnp.dot(dz, k_ref[0, pl.ds(st, t), :], preferred_element_type=F32)
                dk_acc[pl.ds(st, t), :] += lax.dot_general(dz, qv, tn_dims, preferred_element_type=F32)
                dv_acc[pl.ds(st, t), :] += lax.dot_general(a.astype(BF16), dov, tn_dims, preferred_element_type=F32)
                run_g = run_g + jnp.sum(g, axis=1, keepdims=True)
            return dq, run_sp, run_g

        zero = jnp.zeros((t, 1), F32)
        carry = lax.fori_loop(0, i // 2, lambda pp, c: tiles([2 * pp, 2 * pp + 1], c, False), (jnp.zeros((t, hd), F32), zero, zero))
        carry = lax.fori_loop(0, i % 2, lambda _, c: tiles([i - 1], c, False), carry)
        dq, _, _ = tiles([i], carry, True)
        dq_ref[0] = (dq * scale).astype(BF16)

        @pl.when(i == nq - 1)
        def _():
            dk_ref[0] = dk_acc[...].astype(BF16)
            dv_ref[0] = dv_acc[...].astype(BF16)

        pl.when((hh == h - 1) & (i == nq - 1))(finish)

    full = pl.BlockSpec((1, s_len, hd), lambda hh, i: (hh, 0, 0))
    tile = pl.BlockSpec((1, t, hd), lambda hh, i: (hh, i, 0))
    hbm = pl.BlockSpec(memory_space=pl.ANY)
    out = SDS((h, s_len, hd), BF16)
    return pl.pallas_call(
        body, name=name, out_shape=(out, out, out, SDS(payload.shape, payload.dtype)), grid=(h, nq),
        in_specs=[tile, full, full, tile, pl.BlockSpec((1, t, 1), lambda hh, i: (hh, i, 0)), hbm],
        out_specs=(tile, full, full, hbm),
        scratch_shapes=[pltpu.VMEM((s_len, hd), F32), pltpu.VMEM((s_len, hd), F32)] + _COMM_SEMAPHORES,
        compiler_params=_params(has_side_effects=True),
    )(q, k, v, do, tot, payload)


def _dl_scores(qv, kk, n):
    s = lax.dot_general(qv, kk, (((1,), (1,)), ((), ())), preferred_element_type=F32) * (HEAD_DIM ** -0.5)
    r = lax.broadcasted_iota(jnp.int32, s.shape, 0)
    c = lax.broadcasted_iota(jnp.int32, s.shape, 1)
    valid = (c >= r) & (c - r <= BLOCK) & ((n > 0) | (c >= BLOCK))
    return jnp.where(valid, s, NEG)


DL_UNROLL = 2
DL_PAIR = 2 * HEAD_DIM
DL_Q_BLOCK0 = 0
DL_K_BLOCK0 = DL_WIDTH // DL_PAIR
DL_V_BLOCK0 = (IN_WIDTH - DL_WIDTH) // DL_PAIR
DL_DO_BLOCK0 = (SB_WIDTH + CV_WIDTH) // DL_PAIR


def _dl_rows(idx, nb, dil):
    r, n = idx // nb, idx % nb
    cur = pl.ds(r + n * (BLOCK * dil), BLOCK, stride=dil)
    prev = pl.ds(r + jnp.maximum(n - 1, 0) * (BLOCK * dil), BLOCK, stride=dil)
    return n, cur, prev


def _dl_window(ref, cur, prev):
    return jnp.concatenate([ref[prev, :], ref[cur, :]], axis=0).astype(BF16)


def _head_lanes():
    first = lax.broadcasted_iota(jnp.int32, (BLOCK, DL_PAIR), 1) < HEAD_DIM
    return first, jnp.logical_not(first)


def _dl_fwd(qk, u, dil, name):
    s_len = qk.shape[0]
    nb = s_len // dil // BLOCK

    def body(q_ref, k_ref, v_ref, o_ref, lse_ref):
        heads = _head_lanes()

        def step(idx, _):
            n, cur, prev = _dl_rows(idx, nb, dil)
            q = q_ref[cur, :]
            kk = _dl_window(k_ref, cur, prev)
            vv = _dl_window(v_ref, cur, prev)
            o, lse = None, None
            for lanes in heads:
                s = _dl_scores(jnp.where(lanes, q, 0.0).astype(BF16), kk, n)
                m = jnp.max(s, axis=-1, keepdims=True)
                p = jnp.exp(s - m)
                den = jnp.sum(p, axis=-1, keepdims=True)
                o_h = jnp.dot((p / den).astype(BF16), vv, preferred_element_type=F32)
                lse_h = jnp.broadcast_to(m + jnp.log(den), (BLOCK, DL_PAIR))
                o = o_h if o is None else jnp.where(heads[0], o, o_h)
                lse = lse_h if lse is None else jnp.where(heads[0], lse, lse_h)
            o_ref[cur, :] = o
            lse_ref[cur, :] = lse
            return 0

        lax.fori_loop(0, s_len // BLOCK, step, 0, unroll=DL_UNROLL)

    col = lambda first: pl.BlockSpec((s_len, DL_PAIR), lambda i: (0, first + i))
    out = SDS((s_len, DL_WIDTH), F32)
    return pl.pallas_call(body, name=name, out_shape=(out, out), grid=(DL_WIDTH // DL_PAIR,),
                          in_specs=[col(DL_Q_BLOCK0), col(DL_K_BLOCK0), col(DL_V_BLOCK0)], out_specs=(col(0), col(0)),
                          compiler_params=_params())(qk, qk, u)


def _dl_bwd(qk, u, dmix, o_mix, wt, lse, dil, name):
    s_len = qk.shape[0]
    nb = s_len // dil // BLOCK
    scale = HEAD_DIM ** -0.5
    nt_dims = (((1,), (1,)), ((), ()))
    tn_dims = (((0,), (0,)), ((), ()))

    def body(q_ref, k_ref, v_ref, do_ref, om_ref, wt_ref, lse_ref, dq_ref, dk_ref, dv_ref):
        dk_ref[...] = jnp.zeros_like(dk_ref)
        dv_ref[...] = jnp.zeros_like(dv_ref)
        heads = _head_lanes()

        def step(idx, _):
            n, cur, prev = _dl_rows(idx, nb, dil)
            q = q_ref[cur, :]
            kk = _dl_window(k_ref, cur, prev)
            vv = _dl_window(v_ref, cur, prev)
            dov = do_ref[cur, :]
            d_lanes = dov * om_ref[cur, :]
            w_lanes = wt_ref[cur, :]
            lse_lanes = lse_ref[cur, :]
            dq, dkk, dvv = None, None, None
            for lanes in heads:
                qm = jnp.where(lanes, q, 0.0).astype(BF16)
                s = _dl_scores(qm, kk, n)
                p = jnp.exp(s - jnp.max(jnp.where(lanes, lse_lanes, NEG), axis=-1, keepdims=True))
                w = jnp.max(jnp.where(lanes, w_lanes, 0.0), axis=-1, keepdims=True)
                d_all = jnp.sum(jnp.where(lanes, d_lanes, 0.0), axis=-1, keepdims=True)
                do_n = jnp.where(lanes, dov * w, 0.0).astype(BF16)
                dp = lax.dot_general(do_n, vv, nt_dims, preferred_element_type=F32)
                ds = (p * (dp - w * d_all) * scale).astype(BF16)
                dq_h = jnp.dot(ds, kk, preferred_element_type=F32)
                dkk_h = lax.dot_general(ds, qm, tn_dims, preferred_element_type=F32)
                dvv_h = lax.dot_general(p.astype(BF16), do_n, tn_dims, preferred_element_type=F32)
                dq = dq_h if dq is None else jnp.where(heads[0], dq, dq_h)
                dkk = dkk_h if dkk is None else dkk + dkk_h
                dvv = dvv_h if dvv is None else dvv + dvv_h
            dq_ref[cur, :] = dq
            dk_ref[prev, :] += dkk[:BLOCK]
            dv_ref[prev, :] += dvv[:BLOCK]
            dk_ref[cur, :] += dkk[BLOCK:]
            dv_ref[cur, :] += dvv[BLOCK:]
            return 0

        lax.fori_loop(0, s_len // BLOCK, step, 0, unroll=DL_UNROLL)

    col = lambda first: pl.BlockSpec((s_len, DL_PAIR), lambda i: (0, first + i))
    out = SDS((s_len, DL_WIDTH), F32)
    return pl.pallas_call(
        body, name=name, out_shape=(out, out, out), grid=(DL_WIDTH // DL_PAIR,),
        in_specs=[col(DL_Q_BLOCK0), col(DL_K_BLOCK0), col(DL_V_BLOCK0), col(DL_DO_BLOCK0), col(0), col(0), col(0)],
        out_specs=(col(0), col(0), col(0)), compiler_params=_params(),
    )(qk, qk, u, dmix, o_mix, wt, lse)


def _dl_mix_fwd(outs, lses, name):
    rows, w = outs[0].shape
    t = 256

    def body(o1, o2, o3, l1, l2, l3, ob_ref, of_ref, w1, w2, w3):
        a, b, c = l1[...], l2[...], l3[...]
        m = jnp.maximum(jnp.maximum(a, b), c)
        ea, eb, ec = jnp.exp(a - m), jnp.exp(b - m), jnp.exp(c - m)
        den = ea + eb + ec
        wa, wb, wc = ea / den, eb / den, ec / den
        o = wa * o1[...] + wb * o2[...] + wc * o3[...]
        ob_ref[...] = o.astype(BF16)
        of_ref[...] = o
        w1[...] = wa
        w2[...] = wb
        w3[...] = wc

    row = pl.BlockSpec((t, w), lambda i: (i, 0))
    f = SDS((rows, w), F32)
    return pl.pallas_call(body, name=name, out_shape=(SDS((rows, w), BF16), f, f, f, f), grid=(rows // t,),
                          in_specs=[row] * 6, out_specs=(row,) * 5, compiler_params=_params())(*outs, *lses)


def _x_probs(qh, kh):
    s = lax.dot_general(qh, kh, (((1,), (1,)), ((), ())), preferred_element_type=F32) * (X_HEAD_DIM ** -0.5)
    e = jnp.exp(s - jnp.max(s, axis=-1, keepdims=True))
    return e / jnp.sum(e, axis=-1, keepdims=True)


def _xattn_fwd(q, k, v, name):
    rows, d = q.shape
    t = 512

    def body(q_ref, k_ref, v_ref, o_ref):
        for hh in range(X_HEADS):
            cols = slice(hh * X_HEAD_DIM, (hh + 1) * X_HEAD_DIM)
            p = _x_probs(q_ref[:, cols], k_ref[:, cols])
            o_ref[:, cols] = jnp.dot(p.astype(BF16), v_ref[:, cols], preferred_element_type=F32).astype(BF16)

    row = pl.BlockSpec((t, d), lambda i: (i, 0))
    mem = pl.BlockSpec((N_MEM, d), lambda i: (0, 0))
    return pl.pallas_call(body, name=name, out_shape=SDS((rows, d), BF16), grid=(rows // t,), in_specs=[row, mem, mem],
                          out_specs=row, compiler_params=_params())(q, k, v)


def _xattn_bwd(q, k, v, do, name):
    rows, d = q.shape
    t = 512
    scale = X_HEAD_DIM ** -0.5

    def body(q_ref, k_ref, v_ref, do_ref, dq_ref, dk_ref, dv_ref):
        @pl.when(pl.program_id(0) == 0)
        def _():
            dk_ref[...] = jnp.zeros_like(dk_ref)
            dv_ref[...] = jnp.zeros_like(dv_ref)

        for hh in range(X_HEADS):
            cols = slice(hh * X_HEAD_DIM, (hh + 1) * X_HEAD_DIM)
            qh, kh, vh, doh = q_ref[:, cols], k_ref[:, cols], v_ref[:, cols], do_ref[:, cols]
            p = _x_probs(qh, kh)
            dp = lax.dot_general(doh, vh, (((1,), (1,)), ((), ())), preferred_element_type=F32)
            ds = (p * (dp - jnp.sum(p * dp, axis=-1, keepdims=True)) * scale).astype(BF16)
            dq_ref[:, cols] = jnp.dot(ds, kh, preferred_element_type=F32).astype(BF16)
            dk_ref[:, cols] += lax.dot_general(ds, qh, (((0,), (0,)), ((), ())), preferred_element_type=F32)
            dv_ref[:, cols] += lax.dot_general(p.astype(BF16), doh, (((0,), (0,)), ((), ())), preferred_element_type=F32)

    row = pl.BlockSpec((t, d), lambda i: (i, 0))
    mem = pl.BlockSpec((N_MEM, d), lambda i: (0, 0))
    return pl.pallas_call(
        body, name=name, out_shape=(SDS((rows, d), BF16), SDS((N_MEM, d), F32), SDS((N_MEM, d), F32)), grid=(rows // t,),
        in_specs=[row, mem, mem, row], out_specs=(row, mem, mem), compiler_params=_params(),
    )(q, k, v, do)


CV_TILE = 256
CV_HALO = 32
CV_LEAD = CV_HALO - (CV_KERNEL - 1)


def _shifted(win, off, rows):
    n = win.shape[0]
    return pltpu.roll(win, (n - off) % n, axis=0)[:rows]


def _glu(val, gate):
    return val * jax.nn.sigmoid(gate)


def _ln_parts(c):
    mu = jnp.mean(c, axis=-1, keepdims=True)
    xc = c - mu
    rstd = lax.rsqrt(jnp.mean(xc * xc, axis=-1, keepdims=True) + EPS)
    return xc * rstd, rstd


def _cv_fwd(u, cv_w, cv_b, ln_g, ln_b, name):
    rows = u.shape[0]
    t, w = CV_TILE, CV_WIDTH
    val_col = 3 * SB_WIDTH // w
    ratio = t // CV_HALO

    def body(val_ref, gate_ref, pval_ref, pgate_ref, w_ref, b_ref, g_ref, beta_ref, s_ref, c_ref):
        i = pl.program_id(0)
        hist = jnp.where(i > 0, _glu(pval_ref[...], pgate_ref[...]), 0.0)
        win = jnp.concatenate([hist, _glu(val_ref[...], gate_ref[...])], axis=0)
        acc = jnp.broadcast_to(b_ref[...], (t, w))
        for kk in range(CV_KERNEL):
            acc = acc + _shifted(win, CV_LEAD + kk, t) * w_ref[kk:kk + 1, :]
        c_ref[...] = acc
        n, _ = _ln_parts(acc)
        y = n * g_ref[...] + beta_ref[...]
        s_ref[...] = (y * jax.nn.sigmoid(y)).astype(BF16)

    cur = lambda col: pl.BlockSpec((t, w), lambda i: (i, col))
    prev = lambda col: pl.BlockSpec((CV_HALO, w), lambda i: (jnp.maximum(i * ratio - 1, 0), col))
    vec = pl.BlockSpec((1, w), lambda i: (0, 0))
    return pl.pallas_call(
        body, name=name, out_shape=(SDS((rows, w), BF16), SDS((rows, w), F32)), grid=(rows // t,),
        in_specs=[cur(val_col), cur(val_col + 1), prev(val_col), prev(val_col + 1),
                  pl.BlockSpec((CV_KERNEL, w), lambda i: (0, 0)), vec, vec, vec],
        out_specs=(pl.BlockSpec((t, w), lambda i: (i, 0)),) * 2, compiler_params=_params(),
    )(u, u, u, u, cv_w, cv_b, ln_g, ln_b)


def _cv_bwd(u, c, ds, db_out, cv_w, ln_g, ln_b, name):
    rows = u.shape[0]
    t, w = CV_TILE, CV_WIDTH
    val_col = 3 * SB_WIDTH // w
    ratio = t // CV_HALO
    nt = rows // t

    def conv_out_grad(c_v, ds_v, g_v, beta_v):
        n, rstd = _ln_parts(c_v)
        y = n * g_v + beta_v
        sig = jax.nn.sigmoid(y)
        dy = ds_v * (sig * (1.0 + y * (1.0 - sig)))
        dn = dy * g_v
        dc = rstd * (dn - jnp.mean(dn, axis=-1, keepdims=True) - n * jnp.mean(dn * n, axis=-1, keepdims=True))
        return dc, dy, n

    def body(val_ref, gate_ref, pval_ref, pgate_ref, c_ref, nc_ref, ds_ref, nds_ref, dbo_ref, w_ref, g_ref, beta_ref,
             dvg_ref, dw_ref, db_ref, dg_ref, dbeta_ref, dpwb_ref):
        i = pl.program_id(0)

        @pl.when(i == 0)
        def _():
            for r in (dw_ref, db_ref, dg_ref, dbeta_ref, dpwb_ref):
                r[...] = jnp.zeros_like(r)

        g_v, beta_v = g_ref[...], beta_ref[...]
        dc, dy, n = conv_out_grad(c_ref[...], ds_ref[...], g_v, beta_v)
        dc_next, _, _ = conv_out_grad(nc_ref[...], nds_ref[...], g_v, beta_v)
        dc_next = jnp.where(i < nt - 1, dc_next, 0.0)
        dg_ref[...] += jnp.sum(dy * n, axis=0, keepdims=True)
        dbeta_ref[...] += jnp.sum(dy, axis=0, keepdims=True)
        db_ref[...] += jnp.sum(dc, axis=0, keepdims=True)
        dpwb_ref[...] += jnp.sum(dbo_ref[...], axis=0, keepdims=True)

        val, gate = val_ref[...], gate_ref[...]
        hist = jnp.where(i > 0, _glu(pval_ref[...], pgate_ref[...]), 0.0)
        win = jnp.concatenate([hist, _glu(val, gate)], axis=0)
        dc_ext = jnp.concatenate([dc, dc_next], axis=0)
        dglu = jnp.zeros((t, w), F32)
        for kk in range(CV_KERNEL):
            dw_ref[kk:kk + 1, :] += jnp.sum(dc * _shifted(win, CV_LEAD + kk, t), axis=0, keepdims=True)
            dglu = dglu + _shifted(dc_ext, CV_KERNEL - 1 - kk, t) * w_ref[kk:kk + 1, :]
        sig = jax.nn.sigmoid(gate)
        dvg_ref[:, 0:w] = (dglu * sig).astype(BF16)
        dvg_ref[:, w:2 * w] = (dglu * val * sig * (1.0 - sig)).astype(BF16)

    cur = lambda col: pl.BlockSpec((t, w), lambda i: (i, col))
    prev = lambda col: pl.BlockSpec((CV_HALO, w), lambda i: (jnp.maximum(i * ratio - 1, 0), col))
    nxt = pl.BlockSpec((CV_HALO, w), lambda i: (jnp.minimum((i + 1) * ratio, rows // CV_HALO - 1), 0))
    vec = pl.BlockSpec((1, w), lambda i: (0, 0))
    return pl.pallas_call(
        body, name=name,
        out_shape=(SDS((rows, 2 * w), BF16), SDS((CV_HALO, w), F32), SDS((1, w), F32), SDS((1, w), F32), SDS((1, w), F32),
                   SDS((1, w), F32)),
        grid=(nt,),
        in_specs=[cur(val_col), cur(val_col + 1), prev(val_col), prev(val_col + 1), cur(0), nxt, cur(0), nxt, cur(0),
                  pl.BlockSpec((CV_KERNEL, w), lambda i: (0, 0)), vec, vec],
        out_specs=(pl.BlockSpec((t, 2 * w), lambda i: (i, 0)), pl.BlockSpec((CV_HALO, w), lambda i: (0, 0)), vec, vec, vec, vec),
        compiler_params=_params(),
    )(u, u, u, u, c, c, ds, ds, db_out, cv_w, ln_g, ln_b)


FFN_TILE = 512
FFN_COLS = 256
FFN_HALO = 8
FFN_KERNEL = 3
N_FF_BLOCKS = D_FF // FFN_COLS


def _conv3(prev8, cur, w_ref, b_ref, first):
    t = cur.shape[0]
    win = jnp.concatenate([jnp.where(first, 0.0, prev8), cur], axis=0)
    return (b_ref[...] + _shifted(win, FFN_HALO - 2, t) * w_ref[0:1, :] + _shifted(win, FFN_HALO - 1, t) * w_ref[1:2, :]
            + cur * w_ref[2:3, :])


def _gelu_gate(gate, val):
    return jax.nn.gelu(gate, approximate=True) * val


def _ffn_specs(t):
    ratio = t // FFN_HALO
    cur = pl.BlockSpec((t, FFN_COLS), lambda j, i: (i, j))
    prev = pl.BlockSpec((FFN_HALO, FFN_COLS), lambda j, i: (jnp.maximum(i * ratio - 1, 0), j))
    wsp = pl.BlockSpec((FFN_KERNEL, FFN_COLS), lambda j, i: (0, j))
    bsp = pl.BlockSpec((1, FFN_COLS), lambda j, i: (0, j))
    return cur, prev, wsp, bsp


def _ffn_act_fwd(up_g, up_v, w_g, w_v, b_g, b_v, name):
    rows = up_g.shape[0]
    t = FFN_TILE
    cur, prev, wsp, bsp = _ffn_specs(t)

    def body(g_ref, v_ref, pg_ref, pv_ref, wg_ref, wv_ref, bg_ref, bv_ref, o_ref):
        first = pl.program_id(1) == 0
        gate = _conv3(pg_ref[...], g_ref[...], wg_ref, bg_ref, first)
        val = _conv3(pv_ref[...], v_ref[...], wv_ref, bv_ref, first)
        o_ref[...] = _gelu_gate(gate, val).astype(BF16)

    return pl.pallas_call(
        body, name=name, out_shape=SDS((rows, D_FF), BF16), grid=(N_FF_BLOCKS, rows // t),
        in_specs=[cur, cur, prev, prev, wsp, wsp, bsp, bsp], out_specs=cur, compiler_params=_params(),
    )(up_g, up_v, up_g, up_v, w_g, w_v, b_g, b_v)


def _ffn_act_bwd(up_g, up_v, dact, w_g, w_v, b_g, b_v, name):
    rows = up_g.shape[0]
    t = FFN_TILE
    te = t + FFN_HALO
    ratio = t // FFN_HALO
    nt = rows // t
    cur, prev, wsp, bsp = _ffn_specs(t)
    nxt = pl.BlockSpec((FFN_HALO, FFN_COLS), lambda j, i: (jnp.minimum((i + 1) * ratio, rows // FFN_HALO - 1), j))

    def conv_ext(pre, x, nx, w_ref, b_ref, first):
        win = jnp.concatenate([jnp.where(first, 0.0, pre), x, nx], axis=0)
        out = (b_ref[...] + _shifted(win, FFN_HALO - 2, te) * w_ref[0:1, :] + _shifted(win, FFN_HALO - 1, te) * w_ref[1:2, :]
               + _shifted(win, FFN_HALO, te) * w_ref[2:3, :])
        return out, win

    def body(g_ref, v_ref, pg_ref, pv_ref, ng_ref, nv_ref, da_ref, nda_ref, wg_ref, wv_ref, bg_ref, bv_ref,
             dug_ref, duv_ref, dwg_ref, dwv_ref, dbg_ref, dbv_ref):
        i = pl.program_id(1)
        first = i == 0
        gate, win_g = conv_ext(pg_ref[...], g_ref[...], ng_ref[...], wg_ref, bg_ref, first)
        val, win_v = conv_ext(pv_ref[...], v_ref[...], nv_ref[...], wv_ref, bv_ref, first)
        da = jnp.concatenate([da_ref[...], jnp.where(i < nt - 1, nda_ref[...], 0.0)], axis=0)
        _, vjp = jax.vjp(_gelu_gate, gate, val)
        dgate, dval = vjp(da)

        @pl.when(first)
        def _():
            for r in (dwg_ref, dwv_ref, dbg_ref, dbv_ref):
                r[...] = jnp.zeros_like(r)

        for dc_ext, win, w_ref, du_ref, dw_ref, db_ref in ((dgate, win_g, wg_ref, dug_ref, dwg_ref, dbg_ref),
                                                          (dval, win_v, wv_ref, duv_ref, dwv_ref, dbv_ref)):
            dc = dc_ext[:t]
            du_ref[...] = (dc * w_ref[2:3, :] + _shifted(dc_ext, 1, t) * w_ref[1:2, :]
                           + _shifted(dc_ext, 2, t) * w_ref[0:1, :]).astype(BF16)
            for kk in range(FFN_KERNEL):
                dw_ref[kk:kk + 1, :] += jnp.sum(dc * _shifted(win, FFN_HALO - 2 + kk, t), axis=0, keepdims=True)
            db_ref[...] += jnp.sum(dc, axis=0, keepdims=True)

    big, wshape, bshape = SDS((rows, D_FF), BF16), SDS((FFN_KERNEL, D_FF), F32), SDS((1, D_FF), F32)
    return pl.pallas_call(
        body, name=name, out_shape=(big, big, wshape, wshape, bshape, bshape), grid=(N_FF_BLOCKS, nt),
        in_specs=[cur, cur, prev, prev, nxt, nxt, cur, nxt, wsp, wsp, bsp, bsp], out_specs=(cur, cur, wsp, wsp, bsp, bsp),
        compiler_params=_params(),
    )(up_g, up_v, up_g, up_v, up_g, up_v, dact, dact, w_g, w_v, b_g, b_v)


def _adamw_update(parts, w_ref, m_ref, v_ref, g_ref, d_ref, nm_ref, nv_ref):
    g = parts[0].astype(F32)
    for s in range(1, N_DEV):
        g = g + parts[s].astype(F32)
    nm = ADAM_B1 * m_ref[...] + (1.0 - ADAM_B1) * g
    nv = ADAM_B2 * v_ref[...] + (1.0 - ADAM_B2) * jnp.square(g)
    m_hat = nm / (1.0 - ADAM_B1 ** ADAM_STEP)
    v_hat = nv / (1.0 - ADAM_B2 ** ADAM_STEP)
    g_ref[...] = g
    d_ref[...] = -ADAM_LR * (m_hat / (jnp.sqrt(v_hat) + ADAM_EPS) + ADAM_WD * w_ref[...])
    nm_ref[...] = nm
    nv_ref[...] = nv


def _adamw(parts, w, m, v, name):
    rows, cols = w.shape
    t = _pick(rows, (512, 256, 128)) if rows > 512 else rows

    def body(p_ref, *refs):
        _adamw_update(p_ref[...], *refs)

    row = pl.BlockSpec((t, cols), lambda i: (i, 0))
    out = SDS((rows, cols), F32)
    return pl.pallas_call(
        body, name=name, out_shape=(out,) * 4, grid=(rows // t,),
        in_specs=[pl.BlockSpec((N_DEV, t, cols), lambda i: (0, i, 0)), row, row, row], out_specs=(row,) * 4,
        compiler_params=_params(),
    )(parts, w, m, v)


def _adamw_packed(sources, w, m, v, name):
    rows, cols = w.shape
    t = ADAMW_ROW_BLOCK
    nb = rows // DEPTH // t
    (src0, first0), (src1, first1) = sources
    assert first0 % t == 0 and first1 % t == 0 and rows % (DEPTH * t) == 0

    def body(p0_ref, p1_ref, *refs):
        layer = pl.program_id(0)
        _adamw_update(jnp.where(layer == 0, p0_ref[...], p1_ref[...]), *refs)

    spec0 = pl.BlockSpec((N_DEV, t, cols), lambda l, i: (0, first0 // t + i * (1 - l) + (nb - 1) * l, 0))
    spec1 = pl.BlockSpec((N_DEV, t, cols), lambda l, i: (0, first1 // t + i * l, 0))
    row = pl.BlockSpec((t, cols), lambda l, i: (l * nb + i, 0))
    out = SDS((rows, cols), F32)
    return pl.pallas_call(body, name=name, out_shape=(out,) * 4, grid=(DEPTH, nb), in_specs=[spec0, spec1, row, row, row],
                          out_specs=(row,) * 4, compiler_params=_params())(src0, src1, w, m, v)


_COMM_SEMAPHORES = [pltpu.SemaphoreType.DMA((N_DEV - 1,)), pltpu.SemaphoreType.DMA((N_DEV - 1,)), pltpu.SemaphoreType.DMA]


def _gather_steps(x_ref, out_ref, send_sems, recv_sems, local_sem):
    x_, y_, c_ = lax.axis_index("x"), lax.axis_index("y"), lax.axis_index("c")
    me, sibling = (x_, y_, c_), (x_, y_, 1 - c_)
    chips = [(1 - x_, y_), (x_, 1 - y_), (1 - x_, 1 - y_)]

    def slot(px, py, pc):
        return out_ref.at[4 * px + 2 * py + pc]

    def copy(kk, block, to, src=None):
        return pltpu.make_async_remote_copy(
            src_ref=slot(*block) if src is None else src, dst_ref=slot(*block),
            send_sem=send_sems.at[kk], recv_sem=recv_sems.at[kk], device_id=to, device_id_type=MESH)

    def mine():
        return pltpu.make_async_copy(x_ref, slot(*me), local_sem)

    def first():
        return [copy(0, me, sibling, src=x_ref)] + [copy(1 + j, me, (*chip, c_), src=x_ref) for j, chip in enumerate(chips)]

    def passed():
        return [copy(4 + j, (*chip, c_), sibling) for j, chip in enumerate(chips)]

    def start():
        mine().start()
        for cp in first():
            cp.start()

    def forward():
        for j, (chip, cp) in enumerate(zip(chips, passed())):
            copy(1 + j, (*chip, c_), me).wait_recv()
            cp.start()

    def finish():
        copy(0, sibling, me).wait_recv()
        for j, chip in enumerate(chips):
            copy(4 + j, (*chip, 1 - c_), me).wait_recv()
        for cp in first() + passed():
            cp.wait_send()
        mine().wait()

    return start, forward, finish


def _exchange_steps(x_ref, out_ref, send_sems, recv_sems, local_sem):
    x_, y_, c_ = lax.axis_index("x"), lax.axis_index("y"), lax.axis_index("c")
    me = 4 * x_ + 2 * y_ + c_

    def mine():
        return pltpu.make_async_copy(x_ref.at[me], out_ref.at[me], local_sem)

    def copies():
        out = []
        for r in range(1, N_DEV):
            px = 1 - x_ if r & 4 else x_
            py = 1 - y_ if r & 2 else y_
            pc = 1 - c_ if r & 1 else c_
            out.append(pltpu.make_async_remote_copy(
                src_ref=x_ref.at[4 * px + 2 * py + pc], dst_ref=out_ref.at[me],
                send_sem=send_sems.at[r - 1], recv_sem=recv_sems.at[r - 1], device_id=(px, py, pc), device_id_type=MESH))
        return out

    def start():
        mine().start()
        for cp in copies():
            cp.start()

    def finish():
        for cp in copies():
            cp.wait_recv()
        for cp in copies():
            cp.wait_send()
        mine().wait()

    return start, finish


def _all_gather(x, name):
    def body(x_ref, out_ref, send_sems, recv_sems, local_sem):
        for step in _gather_steps(x_ref, out_ref, send_sems, recv_sems, local_sem):
            step()

    hbm = pl.BlockSpec(memory_space=pl.ANY)
    return pl.pallas_call(body, name=name, out_shape=SDS((N_DEV,) + x.shape, x.dtype), in_specs=[hbm], out_specs=hbm,
                          scratch_shapes=_COMM_SEMAPHORES, compiler_params=pltpu.CompilerParams(has_side_effects=True))(x)


def _all_to_all(x, name):
    def body(x_ref, out_ref, send_sems, recv_sems, local_sem):
        for step in _exchange_steps(x_ref, out_ref, send_sems, recv_sems, local_sem):
            step()

    hbm = pl.BlockSpec(memory_space=pl.ANY)
    return pl.pallas_call(body, name=name, out_shape=SDS(x.shape, x.dtype), in_specs=[hbm], out_specs=hbm,
                          scratch_shapes=_COMM_SEMAPHORES, compiler_params=pltpu.CompilerParams(has_side_effects=True))(x)


BIG = ("w_in", "cv_pw_w", "w_out", "x_wq", "x_wk", "x_wv", "x_wo", "ffn_w_up", "ffn_w_down")
_NON_MIXER = (("ffn_w_up", 704), ("ffn_w_down", 352), ("x_wq", 128), ("x_wk", 128), ("x_wv", 128), ("x_wo", 128))
GROUPS = {
    "a": (("w_in", 0, 352), ("w_out", 0, 128)),
    "b": tuple((n, 0, r) for n, r in _NON_MIXER) + (("w_in", 1, 352), ("w_out", 1, 128)),
    "c": tuple((n, 1, r) for n, r in _NON_MIXER),
}
TRANSPOSED = ("w_in", "ffn_w_up")
PW_ROWS = 16
ADAMW_ROW_BLOCK = 32


def _group_rows(group):
    out, first = {}, 0
    for n, l, r in GROUPS[group]:
        out[(n, l)] = (first, r)
        first += r
    return out


def _where_is(name, layer):
    for group in GROUPS:
        rows = _group_rows(group)
        if (name, layer) in rows:
            return (group,) + rows[(name, layer)]
    raise KeyError((name, layer))


def _pack_weights(group, wts):
    pieces = []
    for n, l, _ in GROUPS[group]:
        w = wts[n][l].astype(BF16)
        pieces.append(w.T if n in TRANSPOSED else w)
    if group == "a":
        pieces.append(wts["cv_pw_w"].astype(BF16).reshape(PW_ROWS, PAYLOAD_COLS))
    return jnp.concatenate(pieces, axis=0)


def _unpack_weights(group, gathered):
    out = {}
    half = N_DEV // 2
    for (n, l), (first, r) in _group_rows(group).items():
        block = gathered[:, first:first + r, :]
        if n == "ffn_w_up":
            out[(n, l)] = (block[:half].reshape(half * r, PAYLOAD_COLS), block[half:].reshape(half * r, PAYLOAD_COLS))
        else:
            out[(n, l)] = block.reshape(N_DEV * r, PAYLOAD_COLS)
    return out


def _pack_grads(group, grads):
    pieces = []
    for n, l, r in GROUPS[group]:
        g = grads[n][l]
        parts = g if isinstance(g, tuple) else (g,)
        pieces.append(jnp.concatenate([p.reshape(-1, r, PAYLOAD_COLS) for p in parts], axis=0))
    if group == "a":
        pieces.append(_to_shards("cv_pw_w", jnp.stack(grads["cv_pw_w"])).reshape(N_DEV, PW_ROWS, PAYLOAD_COLS))
    return jnp.concatenate(pieces, axis=1)


COL_SHARDED = ("w_in", "ffn_w_up", "cv_w", "ffn_conv_w")
SMALL_SHARDED = ("cv_w", "ffn_conv_w")
REPLICATED = ("mix_norm_pre", "cv_b", "cv_ln_g", "cv_ln_b", "cv_pw_b", "mix_norm_post", "x_norm_pre", "mem_norm",
              "x_norm_post", "ffn_norm_pre", "ffn_conv_b", "ffn_norm_post")
WEIGHTS = ("mix_norm_pre", "w_in", "cv_w", "cv_b", "cv_ln_g", "cv_ln_b", "cv_pw_w", "cv_pw_b", "w_out", "mix_norm_post",
           "x_norm_pre", "mem_norm", "x_wq", "x_wk", "x_wv", "x_wo", "x_norm_post", "ffn_norm_pre", "ffn_w_up",
           "ffn_conv_w", "ffn_conv_b", "ffn_w_down", "ffn_norm_post")
PAYLOAD_COLS = 1024


PAYLOAD_ROW_TILE = 16


def _pad_rows(flat, cols):
    n = flat.shape[-1]
    rows = -(-n // (cols * PAYLOAD_ROW_TILE)) * PAYLOAD_ROW_TILE
    pad = rows * cols - n
    if pad:
        flat = jnp.concatenate([flat, jnp.zeros(flat.shape[:-1] + (pad,), flat.dtype)], axis=-1)
    return flat.reshape(flat.shape[:-1] + (rows, cols))


def _unshard(name, parts):
    n, depth, r, c = parts.shape
    if name in COL_SHARDED:
        return parts.transpose(1, 2, 0, 3).reshape(depth, r, n * c)
    return parts.transpose(1, 0, 2, 3).reshape(depth, n * r, c)


def _to_shards(name, full):
    depth, r, c = full.shape
    if name in COL_SHARDED:
        return full.reshape(depth, r, N_DEV, c // N_DEV).transpose(2, 0, 1, 3).reshape(N_DEV, -1)
    return full.reshape(depth, N_DEV, r // N_DEV, c).transpose(1, 0, 2, 3).reshape(N_DEV, -1)


def _heads_major(x, h):
    return x.reshape(x.shape[0], h, HEAD_DIM).transpose(1, 0, 2)


def _tokens_major(x):
    return x.transpose(1, 0, 2).reshape(x.shape[1], -1)


def _ffn_halves(p):
    w, b = p["ffn_conv_w"], p["ffn_conv_b"]
    return w[:, :D_FF], w[:, D_FF:], b[:, :D_FF], b[:, D_FF:]


def _layer_fwd(l, h, hn, p, mem, cos, sin, g_next, payload, unpack):
    p = dict(p)
    sv = {"h0": h, "hn0": hn}
    u = _mm(hn, p["w_in"], "nt", F32, f"l{l}_in_proj")
    sv["u"] = u
    sb = _heads_major(u[:, :3 * SB_WIDTH].astype(BF16), 3 * SB_HEADS)
    sb_q, sb_k, sb_v = sb[:SB_HEADS], sb[SB_HEADS:2 * SB_HEADS], sb[2 * SB_HEADS:]
    a_out, sb_tot, gathered = _sb_fwd(sb_q, sb_k, sb_v, payload, f"l{l}_sb_fwd")
    p.update(unpack(gathered))
    sv.update(sb_q=sb_q, sb_k=sb_k, sb_v=sb_v, sb_tot=sb_tot, p=p)

    cv_s, cv_c = _cv_fwd(u, p["cv_w"], p["cv_b"], p["cv_ln_g"], p["cv_ln_b"], f"l{l}_cv_fwd")
    b_out = _mm(cv_s, p["cv_pw_w"], "nn", BF16, f"l{l}_cv_pw", bias=p["cv_pw_b"])
    sv.update(cv_s=cv_s, cv_c=cv_c)

    qk = _rope_fwd(u, cos, sin, f"l{l}_rope_fwd")
    outs, lses = [], []
    for b, (_, dil) in enumerate(DL_PATTERN):
        o, lse = _dl_fwd(qk, u, dil, f"l{l}_dl{b}_fwd")
        outs.append(o)
        lses.append(lse)
    c_out, c_out_f32, w1, w2, w3 = _dl_mix_fwd(outs, lses, f"l{l}_dl_mix")
    sv.update(dl_qk=qk, dl_lse=lses, dl_o=c_out_f32, dl_w=(w1, w2, w3))

    mix = jnp.concatenate([_tokens_major(a_out), b_out, c_out], axis=-1)
    y = _mm(mix, p["w_out"], "nn", F32, f"l{l}_out_proj")
    h1, hn1 = _res_norm_fwd(h, y, p["mix_norm_post"], p["x_norm_pre"], f"l{l}_mix_post")
    sv.update(mix=mix, y_mix=y, h1=h1, hn1=hn1)

    xq = _mm(hn1, p["x_wq"], "nn", BF16, f"l{l}_xq")
    memn = _rms_fwd(mem, p["mem_norm"], f"l{l}_mem_norm")
    xk = _mm(memn, p["x_wk"], "nn", BF16, f"l{l}_xk")
    xv = _mm(memn, p["x_wv"], "nn", BF16, f"l{l}_xv")
    xo = _xattn_fwd(xq, xk, xv, f"l{l}_xattn_fwd")
    y = _mm(xo, p["x_wo"], "nn", F32, f"l{l}_xo_proj")
    h2, hn2 = _res_norm_fwd(h1, y, p["x_norm_post"], p["ffn_norm_pre"], f"l{l}_x_post")
    sv.update(xq=xq, xk=xk, xv=xv, xo=xo, memn=memn, y_x=y, h2=h2, hn2=hn2)

    up_g = _mm(hn2, p["ffn_w_up"][0], "nt", F32, f"l{l}_ffn_up_gate")
    up_v = _mm(hn2, p["ffn_w_up"][1], "nt", F32, f"l{l}_ffn_up_val")
    act = _ffn_act_fwd(up_g, up_v, *_ffn_halves(p), f"l{l}_ffn_act")
    y = _mm(act, p["ffn_w_down"], "nn", F32, f"l{l}_ffn_down")
    h3, hn3 = _res_norm_fwd(h2, y, p["ffn_norm_post"], g_next, f"l{l}_ffn_post")
    sv.update(up_g=up_g, up_v=up_v, act=act, y_ffn=y)
    return h3, hn3, sv, gathered


def _layer_bwd(l, dh, dy, sv, mem, cos, sin, prev_post, make_payload):
    p = sv["p"]
    gr = {}
    dact = _mm(dy, p["ffn_w_down"], "nt", F32, f"l{l}_d_act")
    gr["ffn_w_down"] = _mm(sv["act"], dy, "tn", BF16, f"l{l}_dw_down")
    dup_g, dup_v, dwg, dwv, dbg, dbv = _ffn_act_bwd(sv["up_g"], sv["up_v"], dact, *_ffn_halves(p), f"l{l}_ffn_act_bwd")
    gr["ffn_conv_w"] = jnp.concatenate([dwg, dwv], axis=1)
    gr["ffn_conv_b"] = jnp.concatenate([dbg, dbv], axis=1)
    dhn = (_mm(dup_g, p["ffn_w_up"][0], "nn", F32, f"l{l}_d_hn2_gate"), _mm(dup_v, p["ffn_w_up"][1], "nn", F32, f"l{l}_d_hn2_val"))
    gr["ffn_w_up"] = (_mm(dup_g, sv["hn2"], "tn", BF16, f"l{l}_dw_up_gate"), _mm(dup_v, sv["hn2"], "tn", BF16, f"l{l}_dw_up_val"))
    dh, dy, gr["ffn_norm_pre"], gr["x_norm_post"] = _norm_bwd(
        dh, (sv["h2"], p["ffn_norm_pre"], dhn), (sv["y_x"], p["x_norm_post"]), f"l{l}_x_post_bwd")

    do = _mm(dy, p["x_wo"], "nt", BF16, f"l{l}_d_xo")
    gr["x_wo"] = _mm(sv["xo"], dy, "tn", BF16, f"l{l}_dw_xo")
    dq, dk, dv = _xattn_bwd(sv["xq"], sv["xk"], sv["xv"], do, f"l{l}_xattn_bwd")
    dhn = _mm(dq, p["x_wq"], "nt", F32, f"l{l}_d_hn1")
    gr["x_wq"] = _mm(sv["hn1"], dq, "tn", BF16, f"l{l}_dw_xq")
    gr["x_wk"] = _mm(sv["memn"], dk, "tn", BF16, f"l{l}_dw_xk")
    gr["x_wv"] = _mm(sv["memn"], dv, "tn", BF16, f"l{l}_dw_xv")
    dmemn = _mm(dk, p["x_wk"], "nt", F32, f"l{l}_d_memn_k") + _mm(dv, p["x_wv"], "nt", F32, f"l{l}_d_memn_v")
    gr["mem_norm"] = _rms_gain_grad(mem, p["mem_norm"], dmemn, f"l{l}_mem_norm_bwd")
    dh, dy, gr["x_norm_pre"], gr["mix_norm_post"] = _norm_bwd(
        dh, (sv["h1"], p["x_norm_pre"], dhn), (sv["y_mix"], p["mix_norm_post"]), f"l{l}_mix_post_bwd")

    dmix = _mm(dy, p["w_out"], "nt", F32, f"l{l}_d_mix")
    gr["w_out"] = _mm(sv["mix"], dy, "tn", BF16, f"l{l}_dw_out")
    do_a = _heads_major(dmix[:, :SB_WIDTH].astype(BF16), SB_HEADS)
    dq, dk, dv, received = _sb_bwd(sv["sb_q"], sv["sb_k"], sv["sb_v"], do_a, sv["sb_tot"], make_payload(gr), f"l{l}_sb_bwd")
    du_sb = _tokens_major(jnp.concatenate([dq, dk, dv], axis=0))

    db_out = dmix[:, SB_WIDTH:SB_WIDTH + CV_WIDTH]
    ds = _mm(db_out, p["cv_pw_w"], "nt", F32, f"l{l}_d_cv_s")
    gr["cv_pw_w"] = _mm(sv["cv_s"], db_out, "tn", BF16, f"l{l}_dw_cv_pw")
    du_cv, dcvw, gr["cv_b"], gr["cv_ln_g"], gr["cv_ln_b"], gr["cv_pw_b"] = _cv_bwd(
        sv["u"], sv["cv_c"], ds, db_out, p["cv_w"], p["cv_ln_g"], p["cv_ln_b"], f"l{l}_cv_bwd")
    gr["cv_w"] = dcvw[:CV_KERNEL]

    dqs, dks, dvs = [], [], []
    for b, (_, dil) in enumerate(DL_PATTERN):
        dq, dk, dv = _dl_bwd(sv["dl_qk"], sv["u"], dmix, sv["dl_o"], sv["dl_w"][b], sv["dl_lse"][b], dil, f"l{l}_dl{b}_bwd")
        dqs.append(dq)
        dks.append(dk)
        dvs.append(dv)
    du_dl = _rope_bwd(dqs, dks, dvs, cos, sin, f"l{l}_rope_bwd")

    du = jnp.concatenate([du_sb, du_cv, du_dl], axis=-1)
    dhn = _mm(du, p["w_in"], "nn", F32, f"l{l}_d_hn0")
    gr["w_in"] = _mm(du, sv["hn0"], "tn", BF16, f"l{l}_dw_in")
    dh, dy, gr["mix_norm_pre"], dg_prev = _norm_bwd(dh, (sv["h0"], p["mix_norm_pre"], dhn), prev_post, f"l{l}_in_bwd")
    return dh, dy, gr, dg_prev, received


def kernel(x, mem, positions, mix_norm_pre, w_in, cv_w, cv_b, cv_ln_g, cv_ln_b, cv_pw_w, cv_pw_b, w_out, mix_norm_post, x_norm_pre, mem_norm, x_wq, x_wk, x_wv, x_wo, x_norm_post, ffn_norm_pre, ffn_w_up, ffn_conv_w, ffn_conv_b, ffn_w_down, ffn_norm_post, loss_target, m_mix_norm_pre, m_w_in, m_cv_w, m_cv_b, m_cv_ln_g, m_cv_ln_b, m_cv_pw_w, m_cv_pw_b, m_w_out, m_mix_norm_post, m_x_norm_pre, m_mem_norm, m_x_wq, m_x_wk, m_x_wv, m_x_wo, m_x_norm_post, m_ffn_norm_pre, m_ffn_w_up, m_ffn_conv_w, m_ffn_conv_b, m_ffn_w_down, m_ffn_norm_post, v_mix_norm_pre, v_w_in, v_cv_w, v_cv_b, v_cv_ln_g, v_cv_ln_b, v_cv_pw_w, v_cv_pw_b, v_w_out, v_mix_norm_post, v_x_norm_pre, v_mem_norm, v_x_wq, v_x_wk, v_x_wv, v_x_wo, v_x_norm_post, v_ffn_norm_pre, v_ffn_w_up, v_ffn_conv_w, v_ffn_conv_b, v_ffn_w_down, v_ffn_norm_post):
    args = locals()
    wts = {n: args[n] for n in WEIGHTS}
    mom = {n: args["m_" + n] for n in WEIGHTS}
    var = {n: args["v_" + n] for n in WEIGHTS}

    x2, mem2, target = x[0], mem[0], loss_target[0]

    gathered_a = _all_gather(_pack_weights("a", wts), "weights_all_gather")
    small_payload = _pad_rows(jnp.concatenate([wts[n].reshape(-1) for n in SMALL_SHARDED]), PAYLOAD_COLS)
    small = _all_gather(small_payload, "small_weights_all_gather").reshape(N_DEV, -1)
    small_full = {}
    off = 0
    for n in SMALL_SHARDED:
        size = wts[n].size
        small_full[n] = _unshard(n, small[:, off:off + size].reshape((N_DEV,) + wts[n].shape))
        off += size
    pw_first = sum(r for _, _, r in GROUPS["a"])
    pw_full = _unshard("cv_pw_w", gathered_a[:, pw_first:, :].reshape((N_DEV,) + wts["cv_pw_w"].shape))

    def mixer_params(l, unpacked):
        p = {n: wts[n][l][None, :] for n in REPLICATED}
        p.update({n: small_full[n][l] for n in SMALL_SHARDED})
        p.update(cv_pw_w=pw_full[l], w_in=unpacked[("w_in", l)], w_out=unpacked[("w_out", l)])
        return p

    def of_layer(group, l):
        return lambda gathered: {n: w for (n, ll), w in _unpack_weights(group, gathered).items() if ll == l}

    pos = positions[0].astype(F32)
    half = HEAD_DIM // 2
    inv_freq = ROPE_THETA ** (-jnp.arange(half, dtype=F32) / half)
    ang = pos[:, None] * inv_freq
    cos = jnp.tile(jnp.cos(ang), (1, LANES // half))
    sin = jnp.tile(jnp.sin(ang), (1, LANES // half))

    p0 = mixer_params(0, _unpack_weights("a", gathered_a))
    hn = _rms_fwd(x2, p0["mix_norm_pre"], "l0_in_norm")
    h, hn, sv0, gathered_b = _layer_fwd(0, x2, hn, p0, mem2, cos, sin, wts["mix_norm_pre"][1][None, :],
                                        _pack_weights("b", wts), of_layer("b", 0))
    p1 = mixer_params(1, _unpack_weights("b", gathered_b))
    h, _, sv1, _ = _layer_fwd(1, h, hn, p1, mem2, cos, sin, None, _pack_weights("c", wts), of_layer("c", 1))
    loss_part, dh = _loss_fwd(h, target, "loss")
    loss = lax.psum(loss_part[0, 0], ("x", "y", "c"))

    grads = {n: [None] * DEPTH for n in WEIGHTS}
    dh, dy, _, grads["ffn_norm_post"][1] = _norm_bwd(dh, None, (sv1["y_ffn"], sv1["p"]["ffn_norm_post"]), "last_post_bwd")

    def payload_c(gr):
        return _pack_grads("c", {n: {1: g} for n, g in gr.items()})

    dh, dy, gr, grads["ffn_norm_post"][0], received_c = _layer_bwd(
        1, dh, dy, sv1, mem2, cos, sin, (sv0["y_ffn"], sv0["p"]["ffn_norm_post"]), payload_c)
    for n, g in gr.items():
        grads[n][1] = g

    def payload_b(gr):
        both = {n: {0: g} for n, g in gr.items()}
        both.update(w_in={1: grads["w_in"][1]}, w_out={1: grads["w_out"][1]})
        return _pack_grads("b", both)

    dh, _, gr, _, received_b = _layer_bwd(0, dh, dy, sv0, mem2, cos, sin, None, payload_b)
    for n, g in gr.items():
        grads[n][0] = g
    grad_x = dh

    received_a = _all_to_all(_pack_grads("a", grads), "grads_all_to_all")
    received = {"a": received_a, "b": received_b, "c": received_c}
    small_rows = jnp.concatenate([_to_shards(n, jnp.stack(grads[n])) for n in SMALL_SHARDED], axis=1)
    rep_flat = jnp.concatenate([jnp.stack([g.reshape(-1) for g in grads[n]]).reshape(-1) for n in REPLICATED])
    rep_rows = jnp.broadcast_to(rep_flat[None], (N_DEV, rep_flat.shape[0]))
    f32_rows = _pad_rows(jnp.concatenate([small_rows, rep_rows], axis=1), PAYLOAD_COLS)
    small_parts = _all_to_all(f32_rows, "small_grads_all_to_all")

    res = {}
    for n in BIG:
        shape = wts[n].shape
        two_d = (shape[0] * shape[1], shape[2])
        operands = (wts[n].reshape(two_d), mom[n].reshape(two_d), var[n].reshape(two_d))
        if n == "cv_pw_w":
            outs = _adamw(received_a[:, pw_first:, :].reshape((N_DEV,) + two_d), *operands, f"adamw_{n}")
        elif n in TRANSPOSED:
            layers = []
            for l in range(DEPTH):
                group, first, r = _where_is(n, l)
                layers.append(received[group][:, first:first + r, :])
            parts = jnp.stack(layers, axis=1).transpose(0, 1, 3, 2).reshape((N_DEV,) + two_d)
            outs = _adamw(parts, *operands, f"adamw_{n}")
        else:
            sources = []
            for l in range(DEPTH):
                group, first, _ = _where_is(n, l)
                sources.append((received[group], first))
            outs = _adamw_packed(sources, *operands, f"adamw_{n}")
        res[n] = [o.reshape(shape) for o in outs]
    small_names = SMALL_SHARDED + REPLICATED
    flat_w = _pad_rows(jnp.concatenate([wts[n].reshape(-1) for n in small_names]), PAYLOAD_COLS)
    flat_m = _pad_rows(jnp.concatenate([mom[n].reshape(-1) for n in small_names]), PAYLOAD_COLS)
    flat_v = _pad_rows(jnp.concatenate([var[n].reshape(-1) for n in small_names]), PAYLOAD_COLS)
    outs = _adamw(small_parts, flat_w, flat_m, flat_v, "adamw_small")
    outs = [o.reshape(-1) for o in outs]
    off = 0
    for n in small_names:
        size = wts[n].size
        res[n] = [o[off:off + size].reshape(wts[n].shape) for o in outs]
        off += size

    result = [loss, grad_x[None]]
    for kind in range(4):
        result += [res[n][kind] for n in WEIGHTS]
    return tuple(result)
```

```python
import functools
import math

import jax
import jax.numpy as jnp
from jax import lax
from jax.experimental import pallas as pl
from jax.experimental.pallas import tpu as pltpu

F32, BF16 = jnp.float32, jnp.bfloat16
SDS = jax.ShapeDtypeStruct

D_MODEL = 1024
SEQ = 4096
DEPTH = 2
HEAD_DIM = 64
SB_HEADS = 4
SB_WIDTH = 256
CV_WIDTH = 256
CV_KERNEL = 31
DL_HEADS = 8
DL_WIDTH = 512
IN_WIDTH = 2816
DL_PATTERN = ((128, 1), (512, 4), (2048, 16))
BLOCK = 128
ROPE_THETA = 10000.0
N_MEM = 256
X_HEADS = 4
X_HEAD_DIM = 256
D_FF = 2816
EPS = 1e-6
N_DEV = 8
LANES = 128

ADAM_LR = 0.001
ADAM_B1 = 0.9
ADAM_B2 = 0.999
ADAM_EPS = 1e-08
ADAM_WD = 0.01
ADAM_STEP = 10

VMEM_LIMIT_BYTES = 56 * 1024 * 1024
MESH = pl.DeviceIdType.MESH
NEG = -1e30


def _params(**kw):
    return pltpu.CompilerParams(vmem_limit_bytes=VMEM_LIMIT_BYTES, **kw)


def _pick(n, cands):
    for c in cands:
        if n % c == 0:
            return c
    return n


def _mm(a, b, mode, out_dtype, name, bias=None):
    if mode == "nn":
        (m, k), (k2, n) = a.shape, b.shape
    elif mode == "nt":
        (m, k), (n, k2) = a.shape, b.shape
    else:
        (k, m), (k2, n) = a.shape, b.shape
    assert k == k2, (a.shape, b.shape, mode)
    tm = _pick(m, (1024, 1408, 512, 256, 128))
    tn = _pick(n, (1024, 1408, 512, 256, 128))
    tk = k if k <= 2048 else _pick(k, (2048, 1408, 1024, 512))
    nk = k // tk
    dims = {"nn": ((1,), (0,)), "nt": ((1,), (1,)), "tn": ((0,), (0,))}[mode]

    def body(*refs):
        refs = list(refs)
        acc_ref = refs.pop() if nk > 1 else None
        a_ref, b_ref = refs[0], refs[1]
        bias_ref = refs[2] if bias is not None else None
        o_ref = refs[-1]
        p = lax.dot_general(a_ref[...].astype(BF16), b_ref[...].astype(BF16), (dims, ((), ())),
                            preferred_element_type=F32)

        def finish(v):
            if bias_ref is not None:
                v = v + bias_ref[...]
            o_ref[...] = v.astype(out_dtype)

        if nk == 1:
            finish(p)
        else:
            kk = pl.program_id(2)

            @pl.when(kk == 0)
            def _():
                acc_ref[...] = p

            @pl.when(kk > 0)
            def _():
                acc_ref[...] += p

            @pl.when(kk == nk - 1)
            def _():
                finish(acc_ref[...])

    a_spec = pl.BlockSpec((tk, tm), lambda i, j, kk: (kk, i)) if mode == "tn" else pl.BlockSpec((tm, tk), lambda i, j, kk: (i, kk))
    b_spec = pl.BlockSpec((tn, tk), lambda i, j, kk: (j, kk)) if mode == "nt" else pl.BlockSpec((tk, tn), lambda i, j, kk: (kk, j))
    in_specs = [a_spec, b_spec]
    args = [a, b]
    if bias is not None:
        in_specs.append(pl.BlockSpec((1, tn), lambda i, j, kk: (0, j)))
        args.append(bias)
    return pl.pallas_call(
        body, name=name, out_shape=SDS((m, n), out_dtype), grid=(m // tm, n // tn, nk),
        in_specs=in_specs, out_specs=pl.BlockSpec((tm, tn), lambda i, j, kk: (i, j)),
        scratch_shapes=[pltpu.VMEM((tm, tn), F32)] if nk > 1 else [], compiler_params=_params(),
    )(*args)


def _rms(x, g):
    r = lax.rsqrt(jnp.mean(x * x, axis=-1, keepdims=True) + EPS)
    return x * r * g


def _rms_bwd(x, g, dy):
    r = lax.rsqrt(jnp.mean(x * x, axis=-1, keepdims=True) + EPS)
    xh = x * r
    dyg = dy * g
    dx = r * (dyg - xh * jnp.mean(dyg * xh, axis=-1, keepdims=True))
    return dx, dy * xh


def _rms_fwd(x, g, name):
    rows, d = x.shape
    t = min(rows, 512)

    def body(x_ref, g_ref, o_ref):
        o_ref[...] = _rms(x_ref[...], g_ref[...]).astype(BF16)

    return pl.pallas_call(
        body, name=name, out_shape=SDS((rows, d), BF16), grid=(rows // t,),
        in_specs=[pl.BlockSpec((t, d), lambda i: (i, 0)), pl.BlockSpec((1, d), lambda i: (0, 0))],
        out_specs=pl.BlockSpec((t, d), lambda i: (i, 0)), compiler_params=_params(),
    )(x, g)


def _res_norm_fwd(h, y, g_post, g_next, name):
    rows, d = h.shape
    t = 512
    has_next = g_next is not None

    def body(*refs):
        if has_next:
            h_ref, y_ref, gp_ref, gn_ref, h1_ref, hn_ref = refs
        else:
            h_ref, y_ref, gp_ref, h1_ref = refs
        h1 = h_ref[...] + _rms(y_ref[...], gp_ref[...])
        h1_ref[...] = h1
        if has_next:
            hn_ref[...] = _rms(h1, gn_ref[...]).astype(BF16)

    row = pl.BlockSpec((t, d), lambda i: (i, 0))
    vec = pl.BlockSpec((1, d), lambda i: (0, 0))
    in_specs = [row, row, vec] + ([vec] if has_next else [])
    args = [h, y, g_post] + ([g_next] if has_next else [])
    out_shape = [SDS((rows, d), F32)] + ([SDS((rows, d), BF16)] if has_next else [])
    out_specs = [row] + ([row] if has_next else [])
    res = pl.pallas_call(body, name=name, out_shape=out_shape, grid=(rows // t,), in_specs=in_specs,
                         out_specs=out_specs, compiler_params=_params())(*args)
    return (res[0], res[1]) if has_next else (res[0], None)


def _norm_bwd(dh, pre, post, name):
    rows, d = dh.shape
    t = 512
    has_pre, has_post = pre is not None, post is not None
    if has_pre:
        dhns = pre[2] if isinstance(pre[2], tuple) else (pre[2],)
        pre = (pre[0], pre[1]) + dhns

    def body(*refs):
        refs = list(refs)
        dh_ref = refs.pop(0)
        if has_pre:
            h_ref, gpre_ref = refs.pop(0), refs.pop(0)
            dhn_refs = [refs.pop(0) for _ in dhns]
        if has_post:
            y_ref, gpost_ref = refs.pop(0), refs.pop(0)
        dht_ref = refs.pop(0)
        if has_post:
            dy_ref = refs.pop(0)
        if has_pre:
            dgpre_ref = refs.pop(0)
        if has_post:
            dgpost_ref = refs.pop(0)
        i = pl.program_id(0)
        dht = dh_ref[...]
        if has_pre:
            dhn = dhn_refs[0][...]
            for r in dhn_refs[1:]:
                dhn = dhn + r[...]
            dx, dgr = _rms_bwd(h_ref[...], gpre_ref[...], dhn)
            dht = dht + dx

            @pl.when(i == 0)
            def _():
                dgpre_ref[...] = jnp.zeros_like(dgpre_ref)

            dgpre_ref[...] += jnp.sum(dgr, axis=0, keepdims=True)
        dht_ref[...] = dht
        if has_post:
            dy, dgr = _rms_bwd(y_ref[...], gpost_ref[...], dht)
            dy_ref[...] = dy.astype(BF16)

            @pl.when(i == 0)
            def _():
                dgpost_ref[...] = jnp.zeros_like(dgpost_ref)

            dgpost_ref[...] += jnp.sum(dgr, axis=0, keepdims=True)

    row = pl.BlockSpec((t, d), lambda i: (i, 0))
    vec = pl.BlockSpec((1, d), lambda i: (0, 0))
    in_specs, args = [row], [dh]
    if has_pre:
        in_specs += [row, vec] + [row] * len(dhns)
        args += list(pre)
    if has_post:
        in_specs += [row, vec]
        args += list(post)
    out_shape, out_specs = [SDS((rows, d), F32)], [row]
    if has_post:
        out_shape.append(SDS((rows, d), BF16))
        out_specs.append(row)
    if has_pre:
        out_shape.append(SDS((1, d), F32))
        out_specs.append(vec)
    if has_post:
        out_shape.append(SDS((1, d), F32))
        out_specs.append(vec)
    res = list(pl.pallas_call(body, name=name, out_shape=out_shape, grid=(rows // t,), in_specs=in_specs,
                              out_specs=out_specs, compiler_params=_params())(*args))
    dht = res.pop(0)
    dy = res.pop(0) if has_post else None
    dgpre = res.pop(0) if has_pre else None
    dgpost = res.pop(0) if has_post else None
    return dht, dy, dgpre, dgpost


def _rms_gain_grad(x, g, dy, name):
    rows, d = x.shape

    def body(x_ref, g_ref, dy_ref, dg_ref):
        _, dgr = _rms_bwd(x_ref[...], g_ref[...], dy_ref[...])
        dg_ref[...] = jnp.sum(dgr, axis=0, keepdims=True)

    return pl.pallas_call(body, name=name, out_shape=SDS((1, d), F32), compiler_params=_params())(x, g, dy)


def _loss_fwd(h, target, name):
    rows, d = h.shape
    t = 512

    def body(h_ref, t_ref, loss_ref, dh_ref):
        i = pl.program_id(0)
        err = h_ref[...] - t_ref[...]
        dh_ref[...] = err * (1.0 / d)

        @pl.when(i == 0)
        def _():
            loss_ref[...] = jnp.zeros_like(loss_ref)

        part = jnp.sum(jnp.sum(err * err, axis=1, keepdims=True), axis=0, keepdims=True) * (0.5 / d)
        loss_ref[...] += jnp.broadcast_to(part, loss_ref.shape)

    row = pl.BlockSpec((t, d), lambda i: (i, 0))
    return pl.pallas_call(
        body, name=name, out_shape=(SDS((1, LANES), F32), SDS((rows, d), F32)), grid=(rows // t,),
        in_specs=[row, row], out_specs=(pl.BlockSpec((1, LANES), lambda i: (0, 0)), row), compiler_params=_params(),
    )(h, target)


def _rot_half(x, sign):
    w = x.shape[-1]
    lane = lax.broadcasted_iota(jnp.int32, x.shape, 1)
    first = (lane % HEAD_DIM) < (HEAD_DIM // 2)
    return jnp.where(first, -sign * pltpu.roll(x, w - HEAD_DIM // 2, axis=1), sign * pltpu.roll(x, HEAD_DIM // 2, axis=1))


def _rope_fwd(u, cos, sin, name):
    rows = u.shape[0]
    t, cw = 512, 256
    first_col = (3 * SB_WIDTH + 2 * CV_WIDTH) // cw

    def body(u_ref, c_ref, s_ref, o_ref):
        x = u_ref[...]
        c = jnp.tile(c_ref[...], (1, cw // LANES))
        s = jnp.tile(s_ref[...], (1, cw // LANES))
        o_ref[...] = x * c + _rot_half(x, 1.0) * s

    tab = pl.BlockSpec((t, LANES), lambda i, j: (i, 0))
    return pl.pallas_call(
        body, name=name, out_shape=SDS((rows, 2 * DL_WIDTH), F32), grid=(rows // t, 2 * DL_WIDTH // cw),
        in_specs=[pl.BlockSpec((t, cw), lambda i, j: (i, first_col + j)), tab, tab],
        out_specs=pl.BlockSpec((t, cw), lambda i, j: (i, j)), compiler_params=_params(),
    )(u, cos, sin)


def _rope_bwd(dqs, dks, dvs, cos, sin, name):
    rows = dqs[0].shape[0]
    t, w = 256, DL_WIDTH

    def body(*refs):
        c = jnp.tile(refs[9][...], (1, w // LANES))
        s = jnp.tile(refs[10][...], (1, w // LANES))
        o_ref = refs[11]
        dq = refs[0][...] + refs[1][...] + refs[2][...]
        dk = refs[3][...] + refs[4][...] + refs[5][...]
        dv = refs[6][...] + refs[7][...] + refs[8][...]
        o_ref[:, 0:w] = (dq * c + _rot_half(dq, -1.0) * s).astype(BF16)
        o_ref[:, w:2 * w] = (dk * c + _rot_half(dk, -1.0) * s).astype(BF16)
        o_ref[:, 2 * w:3 * w] = dv.astype(BF16)

    row = pl.BlockSpec((t, w), lambda i: (i, 0))
    tab = pl.BlockSpec((t, LANES), lambda i: (i, 0))
    return pl.pallas_call(
        body, name=name, out_shape=SDS((rows, 3 * w), BF16), grid=(rows // t,), in_specs=[row] * 9 + [tab, tab],
        out_specs=pl.BlockSpec((t, 3 * w), lambda i: (i, 0)), compiler_params=_params(),
    )(*dqs, *dks, *dvs, cos, sin)


SB_TILE = 256
SB_ZERO_AFTER = 110.0
SB_FIRST_BLOCK = (8, LANES)


def _softplus(z):
    return jnp.maximum(z, 0.0) + jnp.log(1.0 + jnp.exp(-jnp.abs(z)))


def _split_dot(x, tri, passes):
    acc = None
    rem = x
    for _ in range(passes):
        part = rem.astype(BF16)
        rem = rem - part.astype(F32)
        d = jnp.dot(part, tri, preferred_element_type=F32)
        acc = d if acc is None else acc + d
    return acc


def _tri(t, rel):
    j = lax.broadcasted_iota(jnp.int32, (t, t), 0)
    s = lax.broadcasted_iota(jnp.int32, (t, t), 1)
    return rel(j, s).astype(BF16)


def _sb_fwd(q, k, v, payload, name):
    h, s_len, hd = q.shape
    t = SB_TILE
    nq = s_len // t
    scale = hd ** -0.5

    def body(q_ref, k_ref, v_ref, pay_ref, o_ref, tot_ref, first_ref, gathered_ref, send_sems, recv_sems, local_sem):
        hh, i = pl.program_id(0), pl.program_id(1)
        start, forward, finish = _gather_steps(pay_ref, gathered_ref, send_sems, recv_sems, local_sem)
        pl.when((hh == 0) & (i == 0))(start)
        pl.when((hh == h - 1) & (i == 0))(forward)
        qv = q_ref[0] * scale
        upper = _tri(t, lambda j, s: j > s)

        def tiles(js, carry, diagonal):
            acc, run = carry
            starts = [pl.multiple_of(j * t, t) for j in js]
            zs = [lax.dot_general(qv, k_ref[0, pl.ds(st, t), :], (((1,), (1,)), ((), ())), preferred_element_type=F32)
                  for st in starts]
            sps = [_softplus(z) for z in zs]
            if diagonal:
                mask = lax.broadcasted_iota(jnp.int32, (t, t), 1) < lax.broadcasted_iota(jnp.int32, (t, t), 0)
                sps = [jnp.where(mask, sp, 0.0) for sp in sps]
            laters = [_split_dot(sp, upper, 2) for sp in sps]
            for st, z, sp, later in zip(starts, zs, sps, laters):
                a = jnp.exp((z - sp) - (run + later))
                if diagonal:
                    a = jnp.where(mask, a, 0.0)
                acc = acc + jnp.dot(a.astype(BF16), v_ref[0, pl.ds(st, t), :], preferred_element_type=F32)
                run = run + jnp.sum(sp, axis=1, keepdims=True)
            return acc, run

        def live(carry):
            return jnp.min(carry[1]) < SB_ZERO_AFTER

        def pair(state):
            pp, carry = state
            j = i - 1 - 2 * pp
            return pp + 1, tiles([j, j - 1], carry, False)

        carry = tiles([i], (jnp.zeros((t, hd), F32), jnp.zeros((t, 1), F32)), True)
        pairs, carry = lax.while_loop(lambda st: (st[0] < i // 2) & live(st[1]), pair, (0, carry))
        last = ((i % 2 == 1) & (pairs == i // 2) & live(carry)).astype(jnp.int32)
        acc, run = lax.fori_loop(0, last, lambda _, c: tiles([0], c, False), carry)
        o_ref[0] = acc.astype(BF16)
        tot_ref[0] = run
        first_ref[...] = jnp.full(first_ref.shape, i - 2 * pairs - last, jnp.int32).astype(F32)
        pl.when((hh == h - 1) & (i == nq - 1))(finish)

    full = pl.BlockSpec((1, s_len, hd), lambda hh, i: (hh, 0, 0))
    tile = pl.BlockSpec((1, t, hd), lambda hh, i: (hh, i, 0))
    hbm = pl.BlockSpec(memory_space=pl.ANY)
    return pl.pallas_call(
        body, name=name,
        out_shape=(SDS((h, s_len, hd), BF16), SDS((h, s_len, 1), F32), SDS((h, nq) + SB_FIRST_BLOCK, F32),
                   SDS((N_DEV,) + payload.shape, payload.dtype)),
        grid=(h, nq), in_specs=[tile, full, full, hbm],
        out_specs=(tile, pl.BlockSpec((1, t, 1), lambda hh, i: (hh, i, 0)),
                   pl.BlockSpec((1, 1) + SB_FIRST_BLOCK, lambda hh, i: (hh, i, 0, 0)), hbm),
        scratch_shapes=_COMM_SEMAPHORES, compiler_params=_params(has_side_effects=True),
    )(q, k, v, payload)


def _sb_bwd(q, k, v, do, tot, first, payload, name):
    h, s_len, hd = q.shape
    t = SB_TILE
    nq = s_len // t
    scale = hd ** -0.5

    def body(q_ref, k_ref, v_ref, do_ref, tot_ref, first_ref, pay_ref, dq_ref, dk_ref, dv_ref, received_ref, dk_acc, dv_acc,
             send_sems, recv_sems, local_sem):
        hh, i = pl.program_id(0), pl.program_id(1)
        start, finish = _exchange_steps(pay_ref, received_ref, send_sems, recv_sems, local_sem)
        pl.when((hh == 0) & (i == 0))(start)

        @pl.when(i == 0)
        def _():
            dk_acc[...] = jnp.zeros_like(dk_acc)
            dv_acc[...] = jnp.zeros_like(dv_acc)

        qv = q_ref[0] * scale
        dov = do_ref[0]
        total = tot_ref[0]
        upto = _tri(t, lambda j, s: j <= s)
        before = _tri(t, lambda j, s: j < s)
        nt_dims = (((1,), (1,)), ((), ()))
        tn_dims = (((0,), (0,)), ((), ()))

        def tiles(js, carry, diagonal):
            dq, run_sp, run_g = carry
            starts = [pl.multiple_of(j * t, t) for j in js]
            zs = [lax.dot_general(qv, k_ref[0, pl.ds(st, t), :], nt_dims, preferred_element_type=F32) for st in starts]
            das = [lax.dot_general(dov, v_ref[0, pl.ds(st, t), :], nt_dims, preferred_element_type=F32) for st in starts]
            sps = [_softplus(z) for z in zs]
            log_sigs = [z - sp for z, sp in zip(zs, sps)]
            if diagonal:
                mask = lax.broadcasted_iota(jnp.int32, (t, t), 1) < lax.broadcasted_iota(jnp.int32, (t, t), 0)
                sps = [jnp.where(mask, sp, 0.0) for sp in sps]
            pres = [_split_dot(sp, upto, 2) for sp in sps]
            a_s, gs = [], []
            for sp, log_sig, pre, da in zip(sps, log_sigs, pres, das):
                a = jnp.exp(log_sig - (total - (run_sp + pre)))
                if diagonal:
                    a = jnp.where(mask, a, 0.0)
                a_s.append(a)
                gs.append(a * da)
                run_sp = run_sp + jnp.sum(sp, axis=1, keepdims=True)
            g_pres = [_split_dot(g, before, 3) for g in gs]
            for st, a, g, g_pre, log_sig in zip(starts, a_s, gs, g_pres, log_sigs):
                sig = jnp.exp(log_sig)
                dz = g * (1.0 - sig) - sig * (run_g + g_pre)
                if diagonal:
                    dz = jnp.where(mask, dz, 0.0)
                dz = dz.astype(BF16)
                dq = dq + jnp.dot(dz, k_ref[0, pl.ds(st, t), :], preferred_element_type=F32)
                dk_acc[pl.ds(st, t), :] += lax.dot_general(dz, qv, tn_dims, preferred_element_type=F32)
                dv_acc[pl.ds(st, t), :] += lax.dot_general(a.astype(BF16), dov, tn_dims, preferred_element_type=F32)
                run_g = run_g + jnp.sum(g, axis=1, keepdims=True)
            return dq, run_sp, run_g

        zero = jnp.zeros((t, 1), F32)
        first = jnp.clip(first_ref[0, 0, 0, 0].astype(jnp.int32), 0, i)
        count = i - first
        carry = lax.fori_loop(0, count // 2, lambda pp, c: tiles([first + 2 * pp, first + 2 * pp + 1], c, False),
                              (jnp.zeros((t, hd), F32), zero, zero))
        carry = lax.fori_loop(0, count % 2, lambda _, c: tiles([i - 1], c, False), carry)
        dq, _, _ = tiles([i], carry, True)
        dq_ref[0] = (dq * scale).astype(BF16)

        @pl.when(i == nq - 1)
        def _():
            dk_ref[0] = dk_acc[...].astype(BF16)
            dv_ref[0] = dv_acc[...].astype(BF16)

        pl.when((hh == h - 1) & (i == nq - 1))(finish)

    full = pl.BlockSpec((1, s_len, hd), lambda hh, i: (hh, 0, 0))
    tile = pl.BlockSpec((1, t, hd), lambda hh, i: (hh, i, 0))
    hbm = pl.BlockSpec(memory_space=pl.ANY)
    out = SDS((h, s_len, hd), BF16)
    return pl.pallas_call(
        body, name=name, out_shape=(out, out, out, SDS(payload.shape, payload.dtype)), grid=(h, nq),
        in_specs=[tile, full, full, tile, pl.BlockSpec((1, t, 1), lambda hh, i: (hh, i, 0)),
                  pl.BlockSpec((1, 1) + SB_FIRST_BLOCK, lambda hh, i: (hh, i, 0, 0)), hbm],
        out_specs=(tile, full, full, hbm),
        scratch_shapes=[pltpu.VMEM((s_len, hd), F32), pltpu.VMEM((s_len, hd), F32)] + _COMM_SEMAPHORES,
        compiler_params=_params(has_side_effects=True),
    )(q, k, v, do, tot, first, payload)


def _dl_scores(qv, kk, n):
    s = lax.dot_general(qv, kk, (((1,), (1,)), ((), ())), preferred_element_type=F32) * (HEAD_DIM ** -0.5)
    r = lax.broadcasted_iota(jnp.int32, s.shape, 0)
    c = lax.broadcasted_iota(jnp.int32, s.shape, 1)
    valid = (c >= r) & (c - r <= BLOCK) & ((n > 0) | (c >= BLOCK))
    return jnp.where(valid, s, NEG)


DL_UNROLL = 4
DL_PAIR = 2 * HEAD_DIM
DL_Q_BLOCK0 = 0
DL_K_BLOCK0 = DL_WIDTH // DL_PAIR
DL_V_BLOCK0 = (IN_WIDTH - DL_WIDTH) // DL_PAIR
DL_DO_BLOCK0 = (SB_WIDTH + CV_WIDTH) // DL_PAIR


def _dl_rows(idx, nb, dil):
    r, n = idx // nb, idx % nb
    cur = pl.ds(r + n * (BLOCK * dil), BLOCK, stride=dil)
    prev = pl.ds(r + jnp.maximum(n - 1, 0) * (BLOCK * dil), BLOCK, stride=dil)
    return n, cur, prev


def _dl_window(ref, cur, prev):
    return jnp.concatenate([ref[prev, :], ref[cur, :]], axis=0).astype(BF16)


def _head_lanes():
    first = lax.broadcasted_iota(jnp.int32, (BLOCK, DL_PAIR), 1) < HEAD_DIM
    return first, jnp.logical_not(first)


def _dl_fwd(qk, u, dil, name):
    s_len = qk.shape[0]
    nb = s_len // dil // BLOCK

    def body(q_ref, k_ref, v_ref, o_ref, lse_ref):
        heads = _head_lanes()

        def step(idx, _):
            n, cur, prev = _dl_rows(idx, nb, dil)
            q = q_ref[cur, :]
            kk = _dl_window(k_ref, cur, prev)
            vv = _dl_window(v_ref, cur, prev)
            o, lse = None, None
            for lanes in heads:
                s = _dl_scores(jnp.where(lanes, q, 0.0).astype(BF16), kk, n)
                m = jnp.max(s, axis=-1, keepdims=True)
                p = jnp.exp(s - m)
                den = jnp.sum(p, axis=-1, keepdims=True)
                o_h = jnp.dot((p / den).astype(BF16), vv, preferred_element_type=F32)
                lse_h = jnp.broadcast_to(m + jnp.log(den), (BLOCK, DL_PAIR))
                o = o_h if o is None else jnp.where(heads[0], o, o_h)
                lse = lse_h if lse is None else jnp.where(heads[0], lse, lse_h)
            o_ref[cur, :] = o
            lse_ref[cur, :] = lse
            return 0

        lax.fori_loop(0, s_len // BLOCK, step, 0, unroll=DL_UNROLL)

    col = lambda first: pl.BlockSpec((s_len, DL_PAIR), lambda i: (0, first + i))
    out = SDS((s_len, DL_WIDTH), F32)
    return pl.pallas_call(body, name=name, out_shape=(out, out), grid=(DL_WIDTH // DL_PAIR,),
                          in_specs=[col(DL_Q_BLOCK0), col(DL_K_BLOCK0), col(DL_V_BLOCK0)], out_specs=(col(0), col(0)),
                          compiler_params=_params())(qk, qk, u)


def _dl_bwd(qk, u, dmix, o_mix, wt, lse, dil, name):
    s_len = qk.shape[0]
    nb = s_len // dil // BLOCK
    scale = HEAD_DIM ** -0.5
    nt_dims = (((1,), (1,)), ((), ()))
    tn_dims = (((0,), (0,)), ((), ()))

    def body(q_ref, k_ref, v_ref, do_ref, om_ref, wt_ref, lse_ref, dq_ref, dk_ref, dv_ref):
        dk_ref[...] = jnp.zeros_like(dk_ref)
        dv_ref[...] = jnp.zeros_like(dv_ref)
        heads = _head_lanes()

        def step(idx, _):
            n, cur, prev = _dl_rows(idx, nb, dil)
            q = q_ref[cur, :]
            kk = _dl_window(k_ref, cur, prev)
            vv = _dl_window(v_ref, cur, prev)
            dov = do_ref[cur, :]
            d_lanes = dov * om_ref[cur, :]
            w_lanes = wt_ref[cur, :]
            lse_lanes = lse_ref[cur, :]
            dq, dkk, dvv = None, None, None
            for lanes in heads:
                qm = jnp.where(lanes, q, 0.0).astype(BF16)
                s = _dl_scores(qm, kk, n)
                p = jnp.exp(s - jnp.max(jnp.where(lanes, lse_lanes, NEG), axis=-1, keepdims=True))
                w = jnp.max(jnp.where(lanes, w_lanes, 0.0), axis=-1, keepdims=True)
                d_all = jnp.sum(jnp.where(lanes, d_lanes, 0.0), axis=-1, keepdims=True)
                do_n = jnp.where(lanes, dov * w, 0.0).astype(BF16)
                dp = lax.dot_general(do_n, vv, nt_dims, preferred_element_type=F32)
                ds = (p * (dp - w * d_all) * scale).astype(BF16)
                dq_h = jnp.dot(ds, kk, preferred_element_type=F32)
                dkk_h = lax.dot_general(ds, qm, tn_dims, preferred_element_type=F32)
                dvv_h = lax.dot_general(p.astype(BF16), do_n, tn_dims, preferred_element_type=F32)
                dq = dq_h if dq is None else jnp.where(heads[0], dq, dq_h)
                dkk = dkk_h if dkk is None else dkk + dkk_h
                dvv = dvv_h if dvv is None else dvv + dvv_h
            dq_ref[cur, :] = dq
            dk_ref[prev, :] += dkk[:BLOCK]
            dv_ref[prev, :] += dvv[:BLOCK]
            dk_ref[cur, :] += dkk[BLOCK:]
            dv_ref[cur, :] += dvv[BLOCK:]
            return 0

        lax.fori_loop(0, s_len // BLOCK, step, 0, unroll=DL_UNROLL)

    col = lambda first: pl.BlockSpec((s_len, DL_PAIR), lambda i: (0, first + i))
    out = SDS((s_len, DL_WIDTH), F32)
    return pl.pallas_call(
        body, name=name, out_shape=(out, out, out), grid=(DL_WIDTH // DL_PAIR,),
        in_specs=[col(DL_Q_BLOCK0), col(DL_K_BLOCK0), col(DL_V_BLOCK0), col(DL_DO_BLOCK0), col(0), col(0), col(0)],
        out_specs=(col(0), col(0), col(0)), compiler_params=_params(),
    )(qk, qk, u, dmix, o_mix, wt, lse)


def _dl_mix_fwd(outs, lses, name):
    rows, w = outs[0].shape
    t = 256

    def body(o1, o2, o3, l1, l2, l3, ob_ref, of_ref, w1, w2, w3):
        a, b, c = l1[...], l2[...], l3[...]
        m = jnp.maximum(jnp.maximum(a, b), c)
        ea, eb, ec = jnp.exp(a - m), jnp.exp(b - m), jnp.exp(c - m)
        den = ea + eb + ec
        wa, wb, wc = ea / den, eb / den, ec / den
        o = wa * o1[...] + wb * o2[...] + wc * o3[...]
        ob_ref[...] = o.astype(BF16)
        of_ref[...] = o
        w1[...] = wa
        w2[...] = wb
        w3[...] = wc

    row = pl.BlockSpec((t, w), lambda i: (i, 0))
    f = SDS((rows, w), F32)
    return pl.pallas_call(body, name=name, out_shape=(SDS((rows, w), BF16), f, f, f, f), grid=(rows // t,),
                          in_specs=[row] * 6, out_specs=(row,) * 5, compiler_params=_params())(*outs, *lses)


def _x_probs(qh, kh):
    s = lax.dot_general(qh, kh, (((1,), (1,)), ((), ())), preferred_element_type=F32) * (X_HEAD_DIM ** -0.5)
    e = jnp.exp(s - jnp.max(s, axis=-1, keepdims=True))
    return e / jnp.sum(e, axis=-1, keepdims=True)


def _xattn_fwd(q, k, v, name):
    rows, d = q.shape
    t = 512

    def body(q_ref, k_ref, v_ref, o_ref):
        for hh in range(X_HEADS):
            cols = slice(hh * X_HEAD_DIM, (hh + 1) * X_HEAD_DIM)
            p = _x_probs(q_ref[:, cols], k_ref[:, cols])
            o_ref[:, cols] = jnp.dot(p.astype(BF16), v_ref[:, cols], preferred_element_type=F32).astype(BF16)

    row = pl.BlockSpec((t, d), lambda i: (i, 0))
    mem = pl.BlockSpec((N_MEM, d), lambda i: (0, 0))
    return pl.pallas_call(body, name=name, out_shape=SDS((rows, d), BF16), grid=(rows // t,), in_specs=[row, mem, mem],
                          out_specs=row, compiler_params=_params())(q, k, v)


def _xattn_bwd(q, k, v, do, name):
    rows, d = q.shape
    t = 512
    scale = X_HEAD_DIM ** -0.5

    def body(q_ref, k_ref, v_ref, do_ref, dq_ref, dk_ref, dv_ref):
        @pl.when(pl.program_id(0) == 0)
        def _():
            dk_ref[...] = jnp.zeros_like(dk_ref)
            dv_ref[...] = jnp.zeros_like(dv_ref)

        for hh in range(X_HEADS):
            cols = slice(hh * X_HEAD_DIM, (hh + 1) * X_HEAD_DIM)
            qh, kh, vh, doh = q_ref[:, cols], k_ref[:, cols], v_ref[:, cols], do_ref[:, cols]
            p = _x_probs(qh, kh)
            dp = lax.dot_general(doh, vh, (((1,), (1,)), ((), ())), preferred_element_type=F32)
            ds = (p * (dp - jnp.sum(p * dp, axis=-1, keepdims=True)) * scale).astype(BF16)
            dq_ref[:, cols] = jnp.dot(ds, kh, preferred_element_type=F32).astype(BF16)
            dk_ref[:, cols] += lax.dot_general(ds, qh, (((0,), (0,)), ((), ())), preferred_element_type=F32)
            dv_ref[:, cols] += lax.dot_general(p.astype(BF16), doh, (((0,), (0,)), ((), ())), preferred_element_type=F32)

    row = pl.BlockSpec((t, d), lambda i: (i, 0))
    mem = pl.BlockSpec((N_MEM, d), lambda i: (0, 0))
    return pl.pallas_call(
        body, name=name, out_shape=(SDS((rows, d), BF16), SDS((N_MEM, d), F32), SDS((N_MEM, d), F32)), grid=(rows // t,),
        in_specs=[row, mem, mem, row], out_specs=(row, mem, mem), compiler_params=_params(),
    )(q, k, v, do)


CV_TILE = 256
CV_HALO = 32
CV_LEAD = CV_HALO - (CV_KERNEL - 1)


def _shifted(win, off, rows):
    n = win.shape[0]
    return pltpu.roll(win, (n - off) % n, axis=0)[:rows]


def _glu(val, gate):
    return val * jax.nn.sigmoid(gate)


def _ln_parts(c):
    mu = jnp.mean(c, axis=-1, keepdims=True)
    xc = c - mu
    rstd = lax.rsqrt(jnp.mean(xc * xc, axis=-1, keepdims=True) + EPS)
    return xc * rstd, rstd


def _cv_fwd(u, cv_w, cv_b, ln_g, ln_b, name):
    rows = u.shape[0]
    t, w = CV_TILE, CV_WIDTH
    val_col = 3 * SB_WIDTH // w
    ratio = t // CV_HALO

    def body(val_ref, gate_ref, pval_ref, pgate_ref, w_ref, b_ref, g_ref, beta_ref, s_ref, c_ref):
        i = pl.program_id(0)
        hist = jnp.where(i > 0, _glu(pval_ref[...], pgate_ref[...]), 0.0)
        win = jnp.concatenate([hist, _glu(val_ref[...], gate_ref[...])], axis=0)
        acc = jnp.broadcast_to(b_ref[...], (t, w))
        for kk in range(CV_KERNEL):
            acc = acc + _shifted(win, CV_LEAD + kk, t) * w_ref[kk:kk + 1, :]
        c_ref[...] = acc
        n, _ = _ln_parts(acc)
        y = n * g_ref[...] + beta_ref[...]
        s_ref[...] = (y * jax.nn.sigmoid(y)).astype(BF16)

    cur = lambda col: pl.BlockSpec((t, w), lambda i: (i, col))
    prev = lambda col: pl.BlockSpec((CV_HALO, w), lambda i: (jnp.maximum(i * ratio - 1, 0), col))
    vec = pl.BlockSpec((1, w), lambda i: (0, 0))
    return pl.pallas_call(
        body, name=name, out_shape=(SDS((rows, w), BF16), SDS((rows, w), F32)), grid=(rows // t,),
        in_specs=[cur(val_col), cur(val_col + 1), prev(val_col), prev(val_col + 1),
                  pl.BlockSpec((CV_KERNEL, w), lambda i: (0, 0)), vec, vec, vec],
        out_specs=(pl.BlockSpec((t, w), lambda i: (i, 0)),) * 2, compiler_params=_params(),
    )(u, u, u, u, cv_w, cv_b, ln_g, ln_b)


def _cv_bwd(u, c, ds, db_out, cv_w, ln_g, ln_b, name):
    rows = u.shape[0]
    t, w = CV_TILE, CV_WIDTH
    val_col = 3 * SB_WIDTH // w
    ratio = t // CV_HALO
    nt = rows // t

    def conv_out_grad(c_v, ds_v, g_v, beta_v):
        n, rstd = _ln_parts(c_v)
        y = n * g_v + beta_v
        sig = jax.nn.sigmoid(y)
        dy = ds_v * (sig * (1.0 + y * (1.0 - sig)))
        dn = dy * g_v
        dc = rstd * (dn - jnp.mean(dn, axis=-1, keepdims=True) - n * jnp.mean(dn * n, axis=-1, keepdims=True))
        return dc, dy, n

    def body(val_ref, gate_ref, pval_ref, pgate_ref, c_ref, nc_ref, ds_ref, nds_ref, dbo_ref, w_ref, g_ref, beta_ref,
             dvg_ref, dw_ref, db_ref, dg_ref, dbeta_ref, dpwb_ref):
        i = pl.program_id(0)

        @pl.when(i == 0)
        def _():
            for r in (dw_ref, db_ref, dg_ref, dbeta_ref, dpwb_ref):
                r[...] = jnp.zeros_like(r)

        g_v, beta_v = g_ref[...], beta_ref[...]
        dc, dy, n = conv_out_grad(c_ref[...], ds_ref[...], g_v, beta_v)
        dc_next, _, _ = conv_out_grad(nc_ref[...], nds_ref[...], g_v, beta_v)
        dc_next = jnp.where(i < nt - 1, dc_next, 0.0)
        dg_ref[...] += jnp.sum(dy * n, axis=0, keepdims=True)
        dbeta_ref[...] += jnp.sum(dy, axis=0, keepdims=True)
        db_ref[...] += jnp.sum(dc, axis=0, keepdims=True)
        dpwb_ref[...] += jnp.sum(dbo_ref[...], axis=0, keepdims=True)

        val, gate = val_ref[...], gate_ref[...]
        hist = jnp.where(i > 0, _glu(pval_ref[...], pgate_ref[...]), 0.0)
        win = jnp.concatenate([hist, _glu(val, gate)], axis=0)
        dc_ext = jnp.concatenate([dc, dc_next], axis=0)
        dglu = jnp.zeros((t, w), F32)
        for kk in range(CV_KERNEL):
            dw_ref[kk:kk + 1, :] += jnp.sum(dc * _shifted(win, CV_LEAD + kk, t), axis=0, keepdims=True)
            dglu = dglu + _shifted(dc_ext, CV_KERNEL - 1 - kk, t) * w_ref[kk:kk + 1, :]
        sig = jax.nn.sigmoid(gate)
        dvg_ref[:, 0:w] = (dglu * sig).astype(BF16)
        dvg_ref[:, w:2 * w] = (dglu * val * sig * (1.0 - sig)).astype(BF16)

    cur = lambda col: pl.BlockSpec((t, w), lambda i: (i, col))
    prev = lambda col: pl.BlockSpec((CV_HALO, w), lambda i: (jnp.maximum(i * ratio - 1, 0), col))
    nxt = pl.BlockSpec((CV_HALO, w), lambda i: (jnp.minimum((i + 1) * ratio, rows // CV_HALO - 1), 0))
    vec = pl.BlockSpec((1, w), lambda i: (0, 0))
    return pl.pallas_call(
        body, name=name,
        out_shape=(SDS((rows, 2 * w), BF16), SDS((CV_HALO, w), F32), SDS((1, w), F32), SDS((1, w), F32), SDS((1, w), F32),
                   SDS((1, w), F32)),
        grid=(nt,),
        in_specs=[cur(val_col), cur(val_col + 1), prev(val_col), prev(val_col + 1), cur(0), nxt, cur(0), nxt, cur(0),
                  pl.BlockSpec((CV_KERNEL, w), lambda i: (0, 0)), vec, vec],
        out_specs=(pl.BlockSpec((t, 2 * w), lambda i: (i, 0)), pl.BlockSpec((CV_HALO, w), lambda i: (0, 0)), vec, vec, vec, vec),
        compiler_params=_params(),
    )(u, u, u, u, c, c, ds, ds, db_out, cv_w, ln_g, ln_b)


FFN_TILE = 512
FFN_COLS = 256
FFN_HALO = 8
FFN_KERNEL = 3
N_FF_BLOCKS = D_FF // FFN_COLS


def _conv3(prev8, cur, w_ref, b_ref, first):
    t = cur.shape[0]
    win = jnp.concatenate([jnp.where(first, 0.0, prev8), cur], axis=0)
    return (b_ref[...] + _shifted(win, FFN_HALO - 2, t) * w_ref[0:1, :] + _shifted(win, FFN_HALO - 1, t) * w_ref[1:2, :]
            + cur * w_ref[2:3, :])


def _gelu_gate(gate, val):
    return jax.nn.gelu(gate, approximate=True) * val


def _ffn_specs(t):
    ratio = t // FFN_HALO
    cur = pl.BlockSpec((t, FFN_COLS), lambda j, i: (i, j))
    prev = pl.BlockSpec((FFN_HALO, FFN_COLS), lambda j, i: (jnp.maximum(i * ratio - 1, 0), j))
    wsp = pl.BlockSpec((FFN_KERNEL, FFN_COLS), lambda j, i: (0, j))
    bsp = pl.BlockSpec((1, FFN_COLS), lambda j, i: (0, j))
    return cur, prev, wsp, bsp


def _ffn_act_fwd(up_g, up_v, w_g, w_v, b_g, b_v, name):
    rows = up_g.shape[0]
    t = FFN_TILE
    cur, prev, wsp, bsp = _ffn_specs(t)

    def body(g_ref, v_ref, pg_ref, pv_ref, wg_ref, wv_ref, bg_ref, bv_ref, o_ref):
        first = pl.program_id(1) == 0
        gate = _conv3(pg_ref[...], g_ref[...], wg_ref, bg_ref, first)
        val = _conv3(pv_ref[...], v_ref[...], wv_ref, bv_ref, first)
        o_ref[...] = _gelu_gate(gate, val).astype(BF16)

    return pl.pallas_call(
        body, name=name, out_shape=SDS((rows, D_FF), BF16), grid=(N_FF_BLOCKS, rows // t),
        in_specs=[cur, cur, prev, prev, wsp, wsp, bsp, bsp], out_specs=cur, compiler_params=_params(),
    )(up_g, up_v, up_g, up_v, w_g, w_v, b_g, b_v)


def _ffn_act_bwd(up_g, up_v, dact, w_g, w_v, b_g, b_v, name):
    rows = up_g.shape[0]
    t = FFN_TILE
    te = t + FFN_HALO
    ratio = t // FFN_HALO
    nt = rows // t
    cur, prev, wsp, bsp = _ffn_specs(t)
    nxt = pl.BlockSpec((FFN_HALO, FFN_COLS), lambda j, i: (jnp.minimum((i + 1) * ratio, rows // FFN_HALO - 1), j))

    def conv_ext(pre, x, nx, w_ref, b_ref, first):
        win = jnp.concatenate([jnp.where(first, 0.0, pre), x, nx], axis=0)
        out = (b_ref[...] + _shifted(win, FFN_HALO - 2, te) * w_ref[0:1, :] + _shifted(win, FFN_HALO - 1, te) * w_ref[1:2, :]
               + _shifted(win, FFN_HALO, te) * w_ref[2:3, :])
        return out, win

    def body(g_ref, v_ref, pg_ref, pv_ref, ng_ref, nv_ref, da_ref, nda_ref, wg_ref, wv_ref, bg_ref, bv_ref,
             dug_ref, duv_ref, dwg_ref, dwv_ref, dbg_ref, dbv_ref):
        i = pl.program_id(1)
        first = i == 0
        gate, win_g = conv_ext(pg_ref[...], g_ref[...], ng_ref[...], wg_ref, bg_ref, first)
        val, win_v = conv_ext(pv_ref[...], v_ref[...], nv_ref[...], wv_ref, bv_ref, first)
        da = jnp.concatenate([da_ref[...], jnp.where(i < nt - 1, nda_ref[...], 0.0)], axis=0)
        _, vjp = jax.vjp(_gelu_gate, gate, val)
        dgate, dval = vjp(da)

        @pl.when(first)
        def _():
            for r in (dwg_ref, dwv_ref, dbg_ref, dbv_ref):
                r[...] = jnp.zeros_like(r)

        for dc_ext, win, w_ref, du_ref, dw_ref, db_ref in ((dgate, win_g, wg_ref, dug_ref, dwg_ref, dbg_ref),
                                                          (dval, win_v, wv_ref, duv_ref, dwv_ref, dbv_ref)):
            dc = dc_ext[:t]
            du_ref[...] = (dc * w_ref[2:3, :] + _shifted(dc_ext, 1, t) * w_ref[1:2, :]
                           + _shifted(dc_ext, 2, t) * w_ref[0:1, :]).astype(BF16)
            for kk in range(FFN_KERNEL):
                dw_ref[kk:kk + 1, :] += jnp.sum(dc * _shifted(win, FFN_HALO - 2 + kk, t), axis=0, keepdims=True)
            db_ref[...] += jnp.sum(dc, axis=0, keepdims=True)

    big, wshape, bshape = SDS((rows, D_FF), BF16), SDS((FFN_KERNEL, D_FF), F32), SDS((1, D_FF), F32)
    return pl.pallas_call(
        body, name=name, out_shape=(big, big, wshape, wshape, bshape, bshape), grid=(N_FF_BLOCKS, nt),
        in_specs=[cur, cur, prev, prev, nxt, nxt, cur, nxt, wsp, wsp, bsp, bsp], out_specs=(cur, cur, wsp, wsp, bsp, bsp),
        compiler_params=_params(),
    )(up_g, up_v, up_g, up_v, up_g, up_v, dact, dact, w_g, w_v, b_g, b_v)


def _adamw_update(parts, w_ref, m_ref, v_ref, g_ref, d_ref, nm_ref, nv_ref):
    g = parts[0].astype(F32)
    for s in range(1, N_DEV):
        g = g + parts[s].astype(F32)
    nm = ADAM_B1 * m_ref[...] + (1.0 - ADAM_B1) * g
    nv = ADAM_B2 * v_ref[...] + (1.0 - ADAM_B2) * jnp.square(g)
    m_hat = nm / (1.0 - ADAM_B1 ** ADAM_STEP)
    v_hat = nv / (1.0 - ADAM_B2 ** ADAM_STEP)
    g_ref[...] = g
    d_ref[...] = -ADAM_LR * (m_hat / (jnp.sqrt(v_hat) + ADAM_EPS) + ADAM_WD * w_ref[...])
    nm_ref[...] = nm
    nv_ref[...] = nv


def _adamw(parts, w, m, v, name):
    rows, cols = w.shape
    t = _pick(rows, (512, 256, 128)) if rows > 512 else rows

    def body(p_ref, *refs):
        _adamw_update(p_ref[...], *refs)

    row = pl.BlockSpec((t, cols), lambda i: (i, 0))
    out = SDS((rows, cols), F32)
    return pl.pallas_call(
        body, name=name, out_shape=(out,) * 4, grid=(rows // t,),
        in_specs=[pl.BlockSpec((N_DEV, t, cols), lambda i: (0, i, 0)), row, row, row], out_specs=(row,) * 4,
        compiler_params=_params(),
    )(parts, w, m, v)


def _adamw_packed(sources, w, m, v, name):
    rows, cols = w.shape
    t = ADAMW_ROW_BLOCK
    nb = rows // DEPTH // t
    (src0, first0), (src1, first1) = sources
    assert first0 % t == 0 and first1 % t == 0 and rows % (DEPTH * t) == 0

    def body(p0_ref, p1_ref, *refs):
        layer = pl.program_id(0)
        _adamw_update(jnp.where(layer == 0, p0_ref[...], p1_ref[...]), *refs)

    spec0 = pl.BlockSpec((N_DEV, t, cols), lambda l, i: (0, first0 // t + i * (1 - l) + (nb - 1) * l, 0))
    spec1 = pl.BlockSpec((N_DEV, t, cols), lambda l, i: (0, first1 // t + i * l, 0))
    row = pl.BlockSpec((t, cols), lambda l, i: (l * nb + i, 0))
    out = SDS((rows, cols), F32)
    return pl.pallas_call(body, name=name, out_shape=(out,) * 4, grid=(DEPTH, nb), in_specs=[spec0, spec1, row, row, row],
                          out_specs=(row,) * 4, compiler_params=_params())(src0, src1, w, m, v)


_COMM_SEMAPHORES = [pltpu.SemaphoreType.DMA((N_DEV - 1,)), pltpu.SemaphoreType.DMA((N_DEV - 1,)), pltpu.SemaphoreType.DMA]


def _gather_steps(x_ref, out_ref, send_sems, recv_sems, local_sem):
    x_, y_, c_ = lax.axis_index("x"), lax.axis_index("y"), lax.axis_index("c")
    me, sibling = (x_, y_, c_), (x_, y_, 1 - c_)
    chips = [(1 - x_, y_), (x_, 1 - y_), (1 - x_, 1 - y_)]

    def slot(px, py, pc):
        return out_ref.at[4 * px + 2 * py + pc]

    def copy(kk, block, to, src=None):
        return pltpu.make_async_remote_copy(
            src_ref=slot(*block) if src is None else src, dst_ref=slot(*block),
            send_sem=send_sems.at[kk], recv_sem=recv_sems.at[kk], device_id=to, device_id_type=MESH)

    def mine():
        return pltpu.make_async_copy(x_ref, slot(*me), local_sem)

    def first():
        return [copy(0, me, sibling, src=x_ref)] + [copy(1 + j, me, (*chip, c_), src=x_ref) for j, chip in enumerate(chips)]

    def passed():
        return [copy(4 + j, (*chip, c_), sibling) for j, chip in enumerate(chips)]

    def start():
        mine().start()
        for cp in first():
            cp.start()

    def forward():
        for j, (chip, cp) in enumerate(zip(chips, passed())):
            copy(1 + j, (*chip, c_), me).wait_recv()
            cp.start()

    def finish():
        copy(0, sibling, me).wait_recv()
        for j, chip in enumerate(chips):
            copy(4 + j, (*chip, 1 - c_), me).wait_recv()
        for cp in first() + passed():
            cp.wait_send()
        mine().wait()

    return start, forward, finish


def _exchange_steps(x_ref, out_ref, send_sems, recv_sems, local_sem):
    x_, y_, c_ = lax.axis_index("x"), lax.axis_index("y"), lax.axis_index("c")
    me = 4 * x_ + 2 * y_ + c_

    def mine():
        return pltpu.make_async_copy(x_ref.at[me], out_ref.at[me], local_sem)

    def copies():
        out = []
        for r in range(1, N_DEV):
            px = 1 - x_ if r & 4 else x_
            py = 1 - y_ if r & 2 else y_
            pc = 1 - c_ if r & 1 else c_
            out.append(pltpu.make_async_remote_copy(
                src_ref=x_ref.at[4 * px + 2 * py + pc], dst_ref=out_ref.at[me],
                send_sem=send_sems.at[r - 1], recv_sem=recv_sems.at[r - 1], device_id=(px, py, pc), device_id_type=MESH))
        return out

    def start():
        mine().start()
        for cp in copies():
            cp.start()

    def finish():
        for cp in copies():
            cp.wait_recv()
        for cp in copies():
            cp.wait_send()
        mine().wait()

    return start, finish


def _all_gather(x, name):
    def body(x_ref, out_ref, send_sems, recv_sems, local_sem):
        for step in _gather_steps(x_ref, out_ref, send_sems, recv_sems, local_sem):
            step()

    hbm = pl.BlockSpec(memory_space=pl.ANY)
    return pl.pallas_call(body, name=name, out_shape=SDS((N_DEV,) + x.shape, x.dtype), in_specs=[hbm], out_specs=hbm,
                          scratch_shapes=_COMM_SEMAPHORES, compiler_params=pltpu.CompilerParams(has_side_effects=True))(x)


def _all_to_all(x, name):
    def body(x_ref, out_ref, send_sems, recv_sems, local_sem):
        for step in _exchange_steps(x_ref, out_ref, send_sems, recv_sems, local_sem):
            step()

    hbm = pl.BlockSpec(memory_space=pl.ANY)
    return pl.pallas_call(body, name=name, out_shape=SDS(x.shape, x.dtype), in_specs=[hbm], out_specs=hbm,
                          scratch_shapes=_COMM_SEMAPHORES, compiler_params=pltpu.CompilerParams(has_side_effects=True))(x)


BIG = ("w_in", "cv_pw_w", "w_out", "x_wq", "x_wk", "x_wv", "x_wo", "ffn_w_up", "ffn_w_down")
_NON_MIXER = (("ffn_w_up", 704), ("ffn_w_down", 352), ("x_wq", 128), ("x_wk", 128), ("x_wv", 128), ("x_wo", 128))
GROUPS = {
    "a": (("w_in", 0, 352), ("w_out", 0, 128)),
    "b": tuple((n, 0, r) for n, r in _NON_MIXER) + (("w_in", 1, 352), ("w_out", 1, 128)),
    "c": tuple((n, 1, r) for n, r in _NON_MIXER),
}
TRANSPOSED = ("w_in", "ffn_w_up")
PW_ROWS = 16
ADAMW_ROW_BLOCK = 32


def _group_rows(group):
    out, first = {}, 0
    for n, l, r in GROUPS[group]:
        out[(n, l)] = (first, r)
        first += r
    return out


def _where_is(name, layer):
    for group in GROUPS:
        rows = _group_rows(group)
        if (name, layer) in rows:
            return (group,) + rows[(name, layer)]
    raise KeyError((name, layer))


def _pack_weights(group, wts):
    pieces = []
    for n, l, _ in GROUPS[group]:
        w = wts[n][l].astype(BF16)
        pieces.append(w.T if n in TRANSPOSED else w)
    if group == "a":
        pieces.append(wts["cv_pw_w"].astype(BF16).reshape(PW_ROWS, PAYLOAD_COLS))
    return jnp.concatenate(pieces, axis=0)


def _unpack_weights(group, gathered):
    out = {}
    half = N_DEV // 2
    for (n, l), (first, r) in _group_rows(group).items():
        block = gathered[:, first:first + r, :]
        if n == "ffn_w_up":
            out[(n, l)] = (block[:half].reshape(half * r, PAYLOAD_COLS), block[half:].reshape(half * r, PAYLOAD_COLS))
        else:
            out[(n, l)] = block.reshape(N_DEV * r, PAYLOAD_COLS)
    return out


def _pack_grads(group, grads):
    pieces = []
    for n, l, r in GROUPS[group]:
        g = grads[n][l]
        parts = g if isinstance(g, tuple) else (g,)
        pieces.append(jnp.concatenate([p.reshape(-1, r, PAYLOAD_COLS) for p in parts], axis=0))
    if group == "a":
        pieces.append(_to_shards("cv_pw_w", jnp.stack(grads["cv_pw_w"])).reshape(N_DEV, PW_ROWS, PAYLOAD_COLS))
    return jnp.concatenate(pieces, axis=1)


COL_SHARDED = ("w_in", "ffn_w_up", "cv_w", "ffn_conv_w")
SMALL_SHARDED = ("cv_w", "ffn_conv_w")
REPLICATED = ("mix_norm_pre", "cv_b", "cv_ln_g", "cv_ln_b", "cv_pw_b", "mix_norm_post", "x_norm_pre", "mem_norm",
              "x_norm_post", "ffn_norm_pre", "ffn_conv_b", "ffn_norm_post")
WEIGHTS = ("mix_norm_pre", "w_in", "cv_w", "cv_b", "cv_ln_g", "cv_ln_b", "cv_pw_w", "cv_pw_b", "w_out", "mix_norm_post",
           "x_norm_pre", "mem_norm", "x_wq", "x_wk", "x_wv", "x_wo", "x_norm_post", "ffn_norm_pre", "ffn_w_up",
           "ffn_conv_w", "ffn_conv_b", "ffn_w_down", "ffn_norm_post")
PAYLOAD_COLS = 1024


PAYLOAD_ROW_TILE = 16


def _pad_rows(flat, cols):
    n = flat.shape[-1]
    rows = -(-n // (cols * PAYLOAD_ROW_TILE)) * PAYLOAD_ROW_TILE
    pad = rows * cols - n
    if pad:
        flat = jnp.concatenate([flat, jnp.zeros(flat.shape[:-1] + (pad,), flat.dtype)], axis=-1)
    return flat.reshape(flat.shape[:-1] + (rows, cols))


def _unshard(name, parts):
    n, depth, r, c = parts.shape
    if name in COL_SHARDED:
        return parts.transpose(1, 2, 0, 3).reshape(depth, r, n * c)
    return parts.transpose(1, 0, 2, 3).reshape(depth, n * r, c)


def _to_shards(name, full):
    depth, r, c = full.shape
    if name in COL_SHARDED:
        return full.reshape(depth, r, N_DEV, c // N_DEV).transpose(2, 0, 1, 3).reshape(N_DEV, -1)
    return full.reshape(depth, N_DEV, r // N_DEV, c).transpose(1, 0, 2, 3).reshape(N_DEV, -1)


def _heads_major(x, h):
    return x.reshape(x.shape[0], h, HEAD_DIM).transpose(1, 0, 2)


def _tokens_major(x):
    return x.transpose(1, 0, 2).reshape(x.shape[1], -1)


def _ffn_halves(p):
    w, b = p["ffn_conv_w"], p["ffn_conv_b"]
    return w[:, :D_FF], w[:, D_FF:], b[:, :D_FF], b[:, D_FF:]


def _layer_fwd(l, h, hn, p, mem, cos, sin, g_next, payload, unpack):
    p = dict(p)
    sv = {"h0": h, "hn0": hn}
    u = _mm(hn, p["w_in"], "nt", F32, f"l{l}_in_proj")
    sv["u"] = u
    sb = _heads_major(u[:, :3 * SB_WIDTH].astype(BF16), 3 * SB_HEADS)
    sb_q, sb_k, sb_v = sb[:SB_HEADS], sb[SB_HEADS:2 * SB_HEADS], sb[2 * SB_HEADS:]
    a_out, sb_tot, sb_first, gathered = _sb_fwd(sb_q, sb_k, sb_v, payload, f"l{l}_sb_fwd")
    p.update(unpack(gathered))
    sv.update(sb_q=sb_q, sb_k=sb_k, sb_v=sb_v, sb_tot=sb_tot, sb_first=sb_first, p=p)

    cv_s, cv_c = _cv_fwd(u, p["cv_w"], p["cv_b"], p["cv_ln_g"], p["cv_ln_b"], f"l{l}_cv_fwd")
    b_out = _mm(cv_s, p["cv_pw_w"], "nn", BF16, f"l{l}_cv_pw", bias=p["cv_pw_b"])
    sv.update(cv_s=cv_s, cv_c=cv_c)

    qk = _rope_fwd(u, cos, sin, f"l{l}_rope_fwd")
    outs, lses = [], []
    for b, (_, dil) in enumerate(DL_PATTERN):
        o, lse = _dl_fwd(qk, u, dil, f"l{l}_dl{b}_fwd")
        outs.append(o)
        lses.append(lse)
    c_out, c_out_f32, w1, w2, w3 = _dl_mix_fwd(outs, lses, f"l{l}_dl_mix")
    sv.update(dl_qk=qk, dl_lse=lses, dl_o=c_out_f32, dl_w=(w1, w2, w3))

    mix = jnp.concatenate([_tokens_major(a_out), b_out, c_out], axis=-1)
    y = _mm(mix, p["w_out"], "nn", F32, f"l{l}_out_proj")
    h1, hn1 = _res_norm_fwd(h, y, p["mix_norm_post"], p["x_norm_pre"], f"l{l}_mix_post")
    sv.update(mix=mix, y_mix=y, h1=h1, hn1=hn1)

    xq = _mm(hn1, p["x_wq"], "nn", BF16, f"l{l}_xq")
    memn = _rms_fwd(mem, p["mem_norm"], f"l{l}_mem_norm")
    xk = _mm(memn, p["x_wk"], "nn", BF16, f"l{l}_xk")
    xv = _mm(memn, p["x_wv"], "nn", BF16, f"l{l}_xv")
    xo = _xattn_fwd(xq, xk, xv, f"l{l}_xattn_fwd")
    y = _mm(xo, p["x_wo"], "nn", F32, f"l{l}_xo_proj")
    h2, hn2 = _res_norm_fwd(h1, y, p["x_norm_post"], p["ffn_norm_pre"], f"l{l}_x_post")
    sv.update(xq=xq, xk=xk, xv=xv, xo=xo, memn=memn, y_x=y, h2=h2, hn2=hn2)

    up_g = _mm(hn2, p["ffn_w_up"][0], "nt", F32, f"l{l}_ffn_up_gate")
    up_v = _mm(hn2, p["ffn_w_up"][1], "nt", F32, f"l{l}_ffn_up_val")
    act = _ffn_act_fwd(up_g, up_v, *_ffn_halves(p), f"l{l}_ffn_act")
    y = _mm(act, p["ffn_w_down"], "nn", F32, f"l{l}_ffn_down")
    h3, hn3 = _res_norm_fwd(h2, y, p["ffn_norm_post"], g_next, f"l{l}_ffn_post")
    sv.update(up_g=up_g, up_v=up_v, act=act, y_ffn=y)
    return h3, hn3, sv, gathered


def _layer_bwd(l, dh, dy, sv, mem, cos, sin, prev_post, make_payload):
    p = sv["p"]
    gr = {}
    dact = _mm(dy, p["ffn_w_down"], "nt", F32, f"l{l}_d_act")
    gr["ffn_w_down"] = _mm(sv["act"], dy, "tn", BF16, f"l{l}_dw_down")
    dup_g, dup_v, dwg, dwv, dbg, dbv = _ffn_act_bwd(sv["up_g"], sv["up_v"], dact, *_ffn_halves(p), f"l{l}_ffn_act_bwd")
    gr["ffn_conv_w"] = jnp.concatenate([dwg, dwv], axis=1)
    gr["ffn_conv_b"] = jnp.concatenate([dbg, dbv], axis=1)
    dhn = (_mm(dup_g, p["ffn_w_up"][0], "nn", F32, f"l{l}_d_hn2_gate"), _mm(dup_v, p["ffn_w_up"][1], "nn", F32, f"l{l}_d_hn2_val"))
    gr["ffn_w_up"] = (_mm(dup_g, sv["hn2"], "tn", BF16, f"l{l}_dw_up_gate"), _mm(dup_v, sv["hn2"], "tn", BF16, f"l{l}_dw_up_val"))
    dh, dy, gr["ffn_norm_pre"], gr["x_norm_post"] = _norm_bwd(
        dh, (sv["h2"], p["ffn_norm_pre"], dhn), (sv["y_x"], p["x_norm_post"]), f"l{l}_x_post_bwd")

    do = _mm(dy, p["x_wo"], "nt", BF16, f"l{l}_d_xo")
    gr["x_wo"] = _mm(sv["xo"], dy, "tn", BF16, f"l{l}_dw_xo")
    dq, dk, dv = _xattn_bwd(sv["xq"], sv["xk"], sv["xv"], do, f"l{l}_xattn_bwd")
    dhn = _mm(dq, p["x_wq"], "nt", F32, f"l{l}_d_hn1")
    gr["x_wq"] = _mm(sv["hn1"], dq, "tn", BF16, f"l{l}_dw_xq")
    gr["x_wk"] = _mm(sv["memn"], dk, "tn", BF16, f"l{l}_dw_xk")
    gr["x_wv"] = _mm(sv["memn"], dv, "tn", BF16, f"l{l}_dw_xv")
    dmemn = _mm(dk, p["x_wk"], "nt", F32, f"l{l}_d_memn_k") + _mm(dv, p["x_wv"], "nt", F32, f"l{l}_d_memn_v")
    gr["mem_norm"] = _rms_gain_grad(mem, p["mem_norm"], dmemn, f"l{l}_mem_norm_bwd")
    dh, dy, gr["x_norm_pre"], gr["mix_norm_post"] = _norm_bwd(
        dh, (sv["h1"], p["x_norm_pre"], dhn), (sv["y_mix"], p["mix_norm_post"]), f"l{l}_mix_post_bwd")

    dmix = _mm(dy, p["w_out"], "nt", F32, f"l{l}_d_mix")
    gr["w_out"] = _mm(sv["mix"], dy, "tn", BF16, f"l{l}_dw_out")
    do_a = _heads_major(dmix[:, :SB_WIDTH].astype(BF16), SB_HEADS)
    dq, dk, dv, received = _sb_bwd(sv["sb_q"], sv["sb_k"], sv["sb_v"], do_a, sv["sb_tot"], sv["sb_first"], make_payload(gr),
                                   f"l{l}_sb_bwd")
    du_sb = _tokens_major(jnp.concatenate([dq, dk, dv], axis=0))

    db_out = dmix[:, SB_WIDTH:SB_WIDTH + CV_WIDTH]
    ds = _mm(db_out, p["cv_pw_w"], "nt", F32, f"l{l}_d_cv_s")
    gr["cv_pw_w"] = _mm(sv["cv_s"], db_out, "tn", BF16, f"l{l}_dw_cv_pw")
    du_cv, dcvw, gr["cv_b"], gr["cv_ln_g"], gr["cv_ln_b"], gr["cv_pw_b"] = _cv_bwd(
        sv["u"], sv["cv_c"], ds, db_out, p["cv_w"], p["cv_ln_g"], p["cv_ln_b"], f"l{l}_cv_bwd")
    gr["cv_w"] = dcvw[:CV_KERNEL]

    dqs, dks, dvs = [], [], []
    for b, (_, dil) in enumerate(DL_PATTERN):
        dq, dk, dv = _dl_bwd(sv["dl_qk"], sv["u"], dmix, sv["dl_o"], sv["dl_w"][b], sv["dl_lse"][b], dil, f"l{l}_dl{b}_bwd")
        dqs.append(dq)
        dks.append(dk)
        dvs.append(dv)
    du_dl = _rope_bwd(dqs, dks, dvs, cos, sin, f"l{l}_rope_bwd")

    du = jnp.concatenate([du_sb, du_cv, du_dl], axis=-1)
    dhn = _mm(du, p["w_in"], "nn", F32, f"l{l}_d_hn0")
    gr["w_in"] = _mm(du, sv["hn0"], "tn", BF16, f"l{l}_dw_in")
    dh, dy, gr["mix_norm_pre"], dg_prev = _norm_bwd(dh, (sv["h0"], p["mix_norm_pre"], dhn), prev_post, f"l{l}_in_bwd")
    return dh, dy, gr, dg_prev, received


def kernel(x, mem, positions, mix_norm_pre, w_in, cv_w, cv_b, cv_ln_g, cv_ln_b, cv_pw_w, cv_pw_b, w_out, mix_norm_post, x_norm_pre, mem_norm, x_wq, x_wk, x_wv, x_wo, x_norm_post, ffn_norm_pre, ffn_w_up, ffn_conv_w, ffn_conv_b, ffn_w_down, ffn_norm_post, loss_target, m_mix_norm_pre, m_w_in, m_cv_w, m_cv_b, m_cv_ln_g, m_cv_ln_b, m_cv_pw_w, m_cv_pw_b, m_w_out, m_mix_norm_post, m_x_norm_pre, m_mem_norm, m_x_wq, m_x_wk, m_x_wv, m_x_wo, m_x_norm_post, m_ffn_norm_pre, m_ffn_w_up, m_ffn_conv_w, m_ffn_conv_b, m_ffn_w_down, m_ffn_norm_post, v_mix_norm_pre, v_w_in, v_cv_w, v_cv_b, v_cv_ln_g, v_cv_ln_b, v_cv_pw_w, v_cv_pw_b, v_w_out, v_mix_norm_post, v_x_norm_pre, v_mem_norm, v_x_wq, v_x_wk, v_x_wv, v_x_wo, v_x_norm_post, v_ffn_norm_pre, v_ffn_w_up, v_ffn_conv_w, v_ffn_conv_b, v_ffn_w_down, v_ffn_norm_post):
    args = locals()
    wts = {n: args[n] for n in WEIGHTS}
    mom = {n: args["m_" + n] for n in WEIGHTS}
    var = {n: args["v_" + n] for n in WEIGHTS}

    x2, mem2, target = x[0], mem[0], loss_target[0]

    gathered_a = _all_gather(_pack_weights("a", wts), "weights_all_gather")
    small_payload = _pad_rows(jnp.concatenate([wts[n].reshape(-1) for n in SMALL_SHARDED]), PAYLOAD_COLS)
    small = _all_gather(small_payload, "small_weights_all_gather").reshape(N_DEV, -1)
    small_full = {}
    off = 0
    for n in SMALL_SHARDED:
        size = wts[n].size
        small_full[n] = _unshard(n, small[:, off:off + size].reshape((N_DEV,) + wts[n].shape))
        off += size
    pw_first = sum(r for _, _, r in GROUPS["a"])
    pw_full = _unshard("cv_pw_w", gathered_a[:, pw_first:, :].reshape((N_DEV,) + wts["cv_pw_w"].shape))

    def mixer_params(l, unpacked):
        p = {n: wts[n][l][None, :] for n in REPLICATED}
        p.update({n: small_full[n][l] for n in SMALL_SHARDED})
        p.update(cv_pw_w=pw_full[l], w_in=unpacked[("w_in", l)], w_out=unpacked[("w_out", l)])
        return p

    def of_layer(group, l):
        return lambda gathered: {n: w for (n, ll), w in _unpack_weights(group, gathered).items() if ll == l}

    pos = positions[0].astype(F32)
    half = HEAD_DIM // 2
    inv_freq = ROPE_THETA ** (-jnp.arange(half, dtype=F32) / half)
    ang = pos[:, None] * inv_freq
    cos = jnp.tile(jnp.cos(ang), (1, LANES // half))
    sin = jnp.tile(jnp.sin(ang), (1, LANES // half))

    p0 = mixer_params(0, _unpack_weights("a", gathered_a))
    hn = _rms_fwd(x2, p0["mix_norm_pre"], "l0_in_norm")
    h, hn, sv0, gathered_b = _layer_fwd(0, x2, hn, p0, mem2, cos, sin, wts["mix_norm_pre"][1][None, :],
                                        _pack_weights("b", wts), of_layer("b", 0))
    p1 = mixer_params(1, _unpack_weights("b", gathered_b))
    h, _, sv1, _ = _layer_fwd(1, h, hn, p1, mem2, cos, sin, None, _pack_weights("c", wts), of_layer("c", 1))
    loss_part, dh = _loss_fwd(h, target, "loss")
    loss = lax.psum(loss_part[0, 0], ("x", "y", "c"))

    grads = {n: [None] * DEPTH for n in WEIGHTS}
    dh, dy, _, grads["ffn_norm_post"][1] = _norm_bwd(dh, None, (sv1["y_ffn"], sv1["p"]["ffn_norm_post"]), "last_post_bwd")

    def payload_c(gr):
        return _pack_grads("c", {n: {1: g} for n, g in gr.items()})

    dh, dy, gr, grads["ffn_norm_post"][0], received_c = _layer_bwd(
        1, dh, dy, sv1, mem2, cos, sin, (sv0["y_ffn"], sv0["p"]["ffn_norm_post"]), payload_c)
    for n, g in gr.items():
        grads[n][1] = g

    def payload_b(gr):
        both = {n: {0: g} for n, g in gr.items()}
        both.update(w_in={1: grads["w_in"][1]}, w_out={1: grads["w_out"][1]})
        return _pack_grads("b", both)

    dh, _, gr, _, received_b = _layer_bwd(0, dh, dy, sv0, mem2, cos, sin, None, payload_b)
    for n, g in gr.items():
        grads[n][0] = g
    grad_x = dh

    received_a = _all_to_all(_pack_grads("a", grads), "grads_all_to_all")
    received = {"a": received_a, "b": received_b, "c": received_c}
    small_rows = jnp.concatenate([_to_shards(n, jnp.stack(grads[n])) for n in SMALL_SHARDED], axis=1)
    rep_flat = jnp.concatenate([jnp.stack([g.reshape(-1) for g in grads[n]]).reshape(-1) for n in REPLICATED])
    rep_rows = jnp.broadcast_to(rep_flat[None], (N_DEV, rep_flat.shape[0]))
    f32_rows = _pad_rows(jnp.concatenate([small_rows, rep_rows], axis=1), PAYLOAD_COLS)
    small_parts = _all_to_all(f32_rows, "small_grads_all_to_all")

    res = {}
    for n in BIG:
        shape = wts[n].shape
        two_d = (shape[0] * shape[1], shape[2])
        operands = (wts[n].reshape(two_d), mom[n].reshape(two_d), var[n].reshape(two_d))
        if n == "cv_pw_w":
            outs = _adamw(received_a[:, pw_first:, :].reshape((N_DEV,) + two_d), *operands, f"adamw_{n}")
        elif n in TRANSPOSED:
            layers = []
            for l in range(DEPTH):
                group, first, r = _where_is(n, l)
                layers.append(received[group][:, first:first + r, :])
            parts = jnp.stack(layers, axis=1).transpose(0, 1, 3, 2).reshape((N_DEV,) + two_d)
            outs = _adamw(parts, *operands, f"adamw_{n}")
        else:
            sources = []
            for l in range(DEPTH):
                group, first, _ = _where_is(n, l)
                sources.append((received[group], first))
            outs = _adamw_packed(sources, *operands, f"adamw_{n}")
        res[n] = [o.reshape(shape) for o in outs]
    small_names = SMALL_SHARDED + REPLICATED
    flat_w = _pad_rows(jnp.concatenate([wts[n].reshape(-1) for n in small_names]), PAYLOAD_COLS)
    flat_m = _pad_rows(jnp.concatenate([mom[n].reshape(-1) for n in small_names]), PAYLOAD_COLS)
    flat_v = _pad_rows(jnp.concatenate([var[n].reshape(-1) for n in small_names]), PAYLOAD_COLS)
    outs = _adamw(small_parts, flat_w, flat_m, flat_v, "adamw_small")
    outs = [o.reshape(-1) for o in outs]
    off = 0
    for n in small_names:
        size = wts[n].size
        res[n] = [o[off:off + size].reshape(wts[n].shape) for o in outs]
        off += size

    result = [loss, grad_x[None]]
    for kind in range(4):
        result += [res[n][kind] for n in WEIGHTS]
    return tuple(result)
```

```python
import functools
import math

import jax
import jax.numpy as jnp
from jax import lax
from jax.experimental import pallas as pl
from jax.experimental.pallas import tpu as pltpu

F32, BF16 = jnp.float32, jnp.bfloat16
SDS = jax.ShapeDtypeStruct

D_MODEL = 1024
SEQ = 4096
DEPTH = 2
HEAD_DIM = 64
SB_HEADS = 4
SB_WIDTH = 256
CV_WIDTH = 256
CV_KERNEL = 31
DL_HEADS = 8
DL_WIDTH = 512
IN_WIDTH = 2816
DL_PATTERN = ((128, 1), (512, 4), (2048, 16))
BLOCK = 128
ROPE_THETA = 10000.0
N_MEM = 256
X_HEADS = 4
X_HEAD_DIM = 256
D_FF = 2816
EPS = 1e-6
N_DEV = 8
LANES = 128

ADAM_LR = 0.001
ADAM_B1 = 0.9
ADAM_B2 = 0.999
ADAM_EPS = 1e-08
ADAM_WD = 0.01
ADAM_STEP = 10

VMEM_LIMIT_BYTES = 56 * 1024 * 1024
MESH = pl.DeviceIdType.MESH
NEG = -1e30


def _params(**kw):
    return pltpu.CompilerParams(vmem_limit_bytes=VMEM_LIMIT_BYTES, **kw)


def _pick(n, cands):
    for c in cands:
        if n % c == 0:
            return c
    return n


def _mm(a, b, mode, out_dtype, name, bias=None):
    if mode == "nn":
        (m, k), (k2, n) = a.shape, b.shape
    elif mode == "nt":
        (m, k), (n, k2) = a.shape, b.shape
    else:
        (k, m), (k2, n) = a.shape, b.shape
    assert k == k2, (a.shape, b.shape, mode)
    tm = _pick(m, (1024, 1408, 512, 256, 128))
    tn = _pick(n, (1024, 1408, 512, 256, 128))
    tk = k if k <= 2048 else _pick(k, (2048, 1408, 1024, 512))
    nk = k // tk
    dims = {"nn": ((1,), (0,)), "nt": ((1,), (1,)), "tn": ((0,), (0,))}[mode]

    def body(*refs):
        refs = list(refs)
        acc_ref = refs.pop() if nk > 1 else None
        a_ref, b_ref = refs[0], refs[1]
        bias_ref = refs[2] if bias is not None else None
        o_ref = refs[-1]
        p = lax.dot_general(a_ref[...].astype(BF16), b_ref[...].astype(BF16), (dims, ((), ())),
                            preferred_element_type=F32)

        def finish(v):
            if bias_ref is not None:
                v = v + bias_ref[...]
            o_ref[...] = v.astype(out_dtype)

        if nk == 1:
            finish(p)
        else:
            kk = pl.program_id(2)

            @pl.when(kk == 0)
            def _():
                acc_ref[...] = p

            @pl.when(kk > 0)
            def _():
                acc_ref[...] += p

            @pl.when(kk == nk - 1)
            def _():
                finish(acc_ref[...])

    a_spec = pl.BlockSpec((tk, tm), lambda i, j, kk: (kk, i)) if mode == "tn" else pl.BlockSpec((tm, tk), lambda i, j, kk: (i, kk))
    b_spec = pl.BlockSpec((tn, tk), lambda i, j, kk: (j, kk)) if mode == "nt" else pl.BlockSpec((tk, tn), lambda i, j, kk: (kk, j))
    in_specs = [a_spec, b_spec]
    args = [a, b]
    if bias is not None:
        in_specs.append(pl.BlockSpec((1, tn), lambda i, j, kk: (0, j)))
        args.append(bias)
    return pl.pallas_call(
        body, name=name, out_shape=SDS((m, n), out_dtype), grid=(m // tm, n // tn, nk),
        in_specs=in_specs, out_specs=pl.BlockSpec((tm, tn), lambda i, j, kk: (i, j)),
        scratch_shapes=[pltpu.VMEM((tm, tn), F32)] if nk > 1 else [], compiler_params=_params(),
    )(*args)


def _rms(x, g):
    r = lax.rsqrt(jnp.mean(x * x, axis=-1, keepdims=True) + EPS)
    return x * r * g


def _rms_bwd(x, g, dy):
    r = lax.rsqrt(jnp.mean(x * x, axis=-1, keepdims=True) + EPS)
    xh = x * r
    dyg = dy * g
    dx = r * (dyg - xh * jnp.mean(dyg * xh, axis=-1, keepdims=True))
    return dx, dy * xh


def _rms_fwd(x, g, name):
    rows, d = x.shape
    t = min(rows, 512)

    def body(x_ref, g_ref, o_ref):
        o_ref[...] = _rms(x_ref[...], g_ref[...]).astype(BF16)

    return pl.pallas_call(
        body, name=name, out_shape=SDS((rows, d), BF16), grid=(rows // t,),
        in_specs=[pl.BlockSpec((t, d), lambda i: (i, 0)), pl.BlockSpec((1, d), lambda i: (0, 0))],
        out_specs=pl.BlockSpec((t, d), lambda i: (i, 0)), compiler_params=_params(),
    )(x, g)


def _res_norm_fwd(h, y, g_post, g_next, name):
    rows, d = h.shape
    t = 512
    has_next = g_next is not None

    def body(*refs):
        if has_next:
            h_ref, y_ref, gp_ref, gn_ref, h1_ref, hn_ref = refs
        else:
            h_ref, y_ref, gp_ref, h1_ref = refs
        h1 = h_ref[...] + _rms(y_ref[...], gp_ref[...])
        h1_ref[...] = h1
        if has_next:
            hn_ref[...] = _rms(h1, gn_ref[...]).astype(BF16)

    row = pl.BlockSpec((t, d), lambda i: (i, 0))
    vec = pl.BlockSpec((1, d), lambda i: (0, 0))
    in_specs = [row, row, vec] + ([vec] if has_next else [])
    args = [h, y, g_post] + ([g_next] if has_next else [])
    out_shape = [SDS((rows, d), F32)] + ([SDS((rows, d), BF16)] if has_next else [])
    out_specs = [row] + ([row] if has_next else [])
    res = pl.pallas_call(body, name=name, out_shape=out_shape, grid=(rows // t,), in_specs=in_specs,
                         out_specs=out_specs, compiler_params=_params())(*args)
    return (res[0], res[1]) if has_next else (res[0], None)


def _norm_bwd(dh, pre, post, name):
    rows, d = dh.shape
    t = 512
    has_pre, has_post = pre is not None, post is not None
    if has_pre:
        dhns = pre[2] if isinstance(pre[2], tuple) else (pre[2],)
        pre = (pre[0], pre[1]) + dhns

    def body(*refs):
        refs = list(refs)
        dh_ref = refs.pop(0)
        if has_pre:
            h_ref, gpre_ref = refs.pop(0), refs.pop(0)
            dhn_refs = [refs.pop(0) for _ in dhns]
        if has_post:
            y_ref, gpost_ref = refs.pop(0), refs.pop(0)
        dht_ref = refs.pop(0)
        if has_post:
            dy_ref = refs.pop(0)
        if has_pre:
            dgpre_ref = refs.pop(0)
        if has_post:
            dgpost_ref = refs.pop(0)
        i = pl.program_id(0)
        dht = dh_ref[...]
        if has_pre:
            dhn = dhn_refs[0][...]
            for r in dhn_refs[1:]:
                dhn = dhn + r[...]
            dx, dgr = _rms_bwd(h_ref[...], gpre_ref[...], dhn)
            dht = dht + dx

            @pl.when(i == 0)
            def _():
                dgpre_ref[...] = jnp.zeros_like(dgpre_ref)

            dgpre_ref[...] += jnp.sum(dgr, axis=0, keepdims=True)
        dht_ref[...] = dht
        if has_post:
            dy, dgr = _rms_bwd(y_ref[...], gpost_ref[...], dht)
            dy_ref[...] = dy.astype(BF16)

            @pl.when(i == 0)
            def _():
                dgpost_ref[...] = jnp.zeros_like(dgpost_ref)

            dgpost_ref[...] += jnp.sum(dgr, axis=0, keepdims=True)

    row = pl.BlockSpec((t, d), lambda i: (i, 0))
    vec = pl.BlockSpec((1, d), lambda i: (0, 0))
    in_specs, args = [row], [dh]
    if has_pre:
        in_specs += [row, vec] + [row] * len(dhns)
        args += list(pre)
    if has_post:
        in_specs += [row, vec]
        args += list(post)
    out_shape, out_specs = [SDS((rows, d), F32)], [row]
    if has_post:
        out_shape.append(SDS((rows, d), BF16))
        out_specs.append(row)
    if has_pre:
        out_shape.append(SDS((1, d), F32))
        out_specs.append(vec)
    if has_post:
        out_shape.append(SDS((1, d), F32))
        out_specs.append(vec)
    res = list(pl.pallas_call(body, name=name, out_shape=out_shape, grid=(rows // t,), in_specs=in_specs,
                              out_specs=out_specs, compiler_params=_params())(*args))
    dht = res.pop(0)
    dy = res.pop(0) if has_post else None
    dgpre = res.pop(0) if has_pre else None
    dgpost = res.pop(0) if has_post else None
    return dht, dy, dgpre, dgpost


def _rms_gain_grad(x, g, dy, name):
    rows, d = x.shape

    def body(x_ref, g_ref, dy_ref, dg_ref):
        _, dgr = _rms_bwd(x_ref[...], g_ref[...], dy_ref[...])
        dg_ref[...] = jnp.sum(dgr, axis=0, keepdims=True)

    return pl.pallas_call(body, name=name, out_shape=SDS((1, d), F32), compiler_params=_params())(x, g, dy)


def _loss_fwd(h, target, name):
    rows, d = h.shape
    t = 512

    def body(h_ref, t_ref, loss_ref, dh_ref):
        i = pl.program_id(0)
        err = h_ref[...] - t_ref[...]
        dh_ref[...] = err * (1.0 / d)

        @pl.when(i == 0)
        def _():
            loss_ref[...] = jnp.zeros_like(loss_ref)

        part = jnp.sum(jnp.sum(err * err, axis=1, keepdims=True), axis=0, keepdims=True) * (0.5 / d)
        loss_ref[...] += jnp.broadcast_to(part, loss_ref.shape)

    row = pl.BlockSpec((t, d), lambda i: (i, 0))
    return pl.pallas_call(
        body, name=name, out_shape=(SDS((1, LANES), F32), SDS((rows, d), F32)), grid=(rows // t,),
        in_specs=[row, row], out_specs=(pl.BlockSpec((1, LANES), lambda i: (0, 0)), row), compiler_params=_params(),
    )(h, target)


def _rot_half(x, sign):
    w = x.shape[-1]
    lane = lax.broadcasted_iota(jnp.int32, x.shape, 1)
    first = (lane % HEAD_DIM) < (HEAD_DIM // 2)
    return jnp.where(first, -sign * pltpu.roll(x, w - HEAD_DIM // 2, axis=1), sign * pltpu.roll(x, HEAD_DIM // 2, axis=1))


def _rope_fwd(u, cos, sin, name):
    rows = u.shape[0]
    t, cw = 512, 256
    first_col = (3 * SB_WIDTH + 2 * CV_WIDTH) // cw

    def body(u_ref, c_ref, s_ref, o_ref):
        x = u_ref[...]
        c = jnp.tile(c_ref[...], (1, cw // LANES))
        s = jnp.tile(s_ref[...], (1, cw // LANES))
        o_ref[...] = x * c + _rot_half(x, 1.0) * s

    tab = pl.BlockSpec((t, LANES), lambda i, j: (i, 0))
    return pl.pallas_call(
        body, name=name, out_shape=SDS((rows, 2 * DL_WIDTH), F32), grid=(rows // t, 2 * DL_WIDTH // cw),
        in_specs=[pl.BlockSpec((t, cw), lambda i, j: (i, first_col + j)), tab, tab],
        out_specs=pl.BlockSpec((t, cw), lambda i, j: (i, j)), compiler_params=_params(),
    )(u, cos, sin)


def _rope_bwd(dqs, dks, dvs, cos, sin, name):
    rows = dqs[0].shape[0]
    t, w = 256, DL_WIDTH

    def body(*refs):
        c = jnp.tile(refs[9][...], (1, w // LANES))
        s = jnp.tile(refs[10][...], (1, w // LANES))
        o_ref = refs[11]
        dq = refs[0][...] + refs[1][...] + refs[2][...]
        dk = refs[3][...] + refs[4][...] + refs[5][...]
        dv = refs[6][...] + refs[7][...] + refs[8][...]
        o_ref[:, 0:w] = (dq * c + _rot_half(dq, -1.0) * s).astype(BF16)
        o_ref[:, w:2 * w] = (dk * c + _rot_half(dk, -1.0) * s).astype(BF16)
        o_ref[:, 2 * w:3 * w] = dv.astype(BF16)

    row = pl.BlockSpec((t, w), lambda i: (i, 0))
    tab = pl.BlockSpec((t, LANES), lambda i: (i, 0))
    return pl.pallas_call(
        body, name=name, out_shape=SDS((rows, 3 * w), BF16), grid=(rows // t,), in_specs=[row] * 9 + [tab, tab],
        out_specs=pl.BlockSpec((t, 3 * w), lambda i: (i, 0)), compiler_params=_params(),
    )(*dqs, *dks, *dvs, cos, sin)


SB_TILE = 256
SB_ZERO_AFTER = 110.0
SB_FIRST_BLOCK = (8, LANES)


def _softplus(z):
    return jnp.maximum(z, 0.0) + jnp.log(1.0 + jnp.exp(-jnp.abs(z)))


def _split_dot(x, tri, passes):
    acc = None
    rem = x
    for _ in range(passes):
        part = rem.astype(BF16)
        rem = rem - part.astype(F32)
        d = jnp.dot(part, tri, preferred_element_type=F32)
        acc = d if acc is None else acc + d
    return acc


def _tri(t, rel):
    j = lax.broadcasted_iota(jnp.int32, (t, t), 0)
    s = lax.broadcasted_iota(jnp.int32, (t, t), 1)
    return rel(j, s).astype(BF16)


def _sb_fwd(q, k, v, payload, name):
    h, s_len, hd = q.shape
    t = SB_TILE
    nq = s_len // t
    scale = hd ** -0.5

    def body(q_ref, k_ref, v_ref, pay_ref, o_ref, tot_ref, first_ref, gathered_ref, send_sems, recv_sems, local_sem):
        hh, i = pl.program_id(0), pl.program_id(1)
        start, forward, finish = _gather_steps(pay_ref, gathered_ref, send_sems, recv_sems, local_sem)
        pl.when((hh == 0) & (i == 0))(start)
        pl.when((hh == h - 1) & (i == 0))(forward)
        qv = q_ref[0] * scale
        upper = _tri(t, lambda j, s: j > s)

        def tiles(js, carry, diagonal):
            acc, run = carry
            starts = [pl.multiple_of(j * t, t) for j in js]
            zs = [lax.dot_general(qv, k_ref[0, pl.ds(st, t), :], (((1,), (1,)), ((), ())), preferred_element_type=F32)
                  for st in starts]
            sps = [_softplus(z) for z in zs]
            if diagonal:
                mask = lax.broadcasted_iota(jnp.int32, (t, t), 1) < lax.broadcasted_iota(jnp.int32, (t, t), 0)
                sps = [jnp.where(mask, sp, 0.0) for sp in sps]
            laters = [_split_dot(sp, upper, 2) for sp in sps]
            for st, z, sp, later in zip(starts, zs, sps, laters):
                a = jnp.exp((z - sp) - (run + later))
                if diagonal:
                    a = jnp.where(mask, a, 0.0)
                acc = acc + jnp.dot(a.astype(BF16), v_ref[0, pl.ds(st, t), :], preferred_element_type=F32)
                run = run + jnp.sum(sp, axis=1, keepdims=True)
            return acc, run

        def live(carry):
            return jnp.min(carry[1]) < SB_ZERO_AFTER

        def pair(state):
            pp, carry = state
            j = i - 1 - 2 * pp
            return pp + 1, tiles([j, j - 1], carry, False)

        carry = tiles([i], (jnp.zeros((t, hd), F32), jnp.zeros((t, 1), F32)), True)
        pairs, carry = lax.while_loop(lambda st: (st[0] < i // 2) & live(st[1]), pair, (0, carry))
        last = ((i % 2 == 1) & (pairs == i // 2) & live(carry)).astype(jnp.int32)
        acc, run = lax.fori_loop(0, last, lambda _, c: tiles([0], c, False), carry)
        o_ref[0] = acc.astype(BF16)
        tot_ref[0] = run
        first_ref[...] = jnp.full(first_ref.shape, i - 2 * pairs - last, jnp.int32).astype(F32)
        pl.when((hh == h - 1) & (i == nq - 1))(finish)

    full = pl.BlockSpec((1, s_len, hd), lambda hh, i: (hh, 0, 0))
    tile = pl.BlockSpec((1, t, hd), lambda hh, i: (hh, i, 0))
    hbm = pl.BlockSpec(memory_space=pl.ANY)
    return pl.pallas_call(
        body, name=name,
        out_shape=(SDS((h, s_len, hd), BF16), SDS((h, s_len, 1), F32), SDS((h, nq) + SB_FIRST_BLOCK, F32),
                   SDS((N_DEV,) + payload.shape, payload.dtype)),
        grid=(h, nq), in_specs=[tile, full, full, hbm],
        out_specs=(tile, pl.BlockSpec((1, t, 1), lambda hh, i: (hh, i, 0)),
                   pl.BlockSpec((1, 1) + SB_FIRST_BLOCK, lambda hh, i: (hh, i, 0, 0)), hbm),
        scratch_shapes=_COMM_SEMAPHORES, compiler_params=_params(has_side_effects=True),
    )(q, k, v, payload)


def _sb_bwd(q, k, v, do, tot, first, payload, name):
    h, s_len, hd = q.shape
    t = SB_TILE
    nq = s_len // t
    scale = hd ** -0.5

    def body(q_ref, k_ref, v_ref, do_ref, tot_ref, first_ref, pay_ref, dq_ref, dk_ref, dv_ref, received_ref, dk_acc, dv_acc,
             send_sems, recv_sems, local_sem):
        hh, i = pl.program_id(0), pl.program_id(1)
        start, finish = _exchange_steps(pay_ref, received_ref, send_sems, recv_sems, local_sem)
        pl.when((hh == 0) & (i == 0))(start)

        @pl.when(i == 0)
        def _():
            dk_acc[...] = jnp.zeros_like(dk_acc)
            dv_acc[...] = jnp.zeros_like(dv_acc)

        qv = q_ref[0] * scale
        dov = do_ref[0]
        total = tot_ref[0]
        upto = _tri(t, lambda j, s: j <= s)
        before = _tri(t, lambda j, s: j < s)
        nt_dims = (((1,), (1,)), ((), ()))
        tn_dims = (((0,), (0,)), ((), ()))

        def tiles(js, carry, diagonal):
            dq, run_sp, run_g = carry
            starts = [pl.multiple_of(j * t, t) for j in js]
            zs = [lax.dot_general(qv, k_ref[0, pl.ds(st, t), :], nt_dims, preferred_element_type=F32) for st in starts]
            das = [lax.dot_general(dov, v_ref[0, pl.ds(st, t), :], nt_dims, preferred_element_type=F32) for st in starts]
            sps = [_softplus(z) for z in zs]
            log_sigs = [z - sp for z, sp in zip(zs, sps)]
            if diagonal:
                mask = lax.broadcasted_iota(jnp.int32, (t, t), 1) < lax.broadcasted_iota(jnp.int32, (t, t), 0)
                sps = [jnp.where(mask, sp, 0.0) for sp in sps]
            pres = [_split_dot(sp, upto, 2) for sp in sps]
            a_s, gs = [], []
            for sp, log_sig, pre, da in zip(sps, log_sigs, pres, das):
                a = jnp.exp(log_sig - (total - (run_sp + pre)))
                if diagonal:
                    a = jnp.where(mask, a, 0.0)
                a_s.append(a)
                gs.append(a * da)
                run_sp = run_sp + jnp.sum(sp, axis=1, keepdims=True)
            g_pres = [_split_dot(g, before, 3) for g in gs]
            for st, a, g, g_pre, log_sig in zip(starts, a_s, gs, g_pres, log_sigs):
                sig = jnp.exp(log_sig)
                dz = g * (1.0 - sig) - sig * (run_g + g_pre)
                if diagonal:
                    dz = jnp.where(mask, dz, 0.0)
                dz = dz.astype(BF16)
                dq = dq + jnp.dot(dz, k_ref[0, pl.ds(st, t), :], preferred_element_type=F32)
                dk_acc[pl.ds(st, t), :] += lax.dot_general(dz, qv, tn_dims, preferred_element_type=F32)
                dv_acc[pl.ds(st, t), :] += lax.dot_general(a.astype(BF16), dov, tn_dims, preferred_element_type=F32)
                run_g = run_g + jnp.sum(g, axis=1, keepdims=True)
            return dq, run_sp, run_g

        zero = jnp.zeros((t, 1), F32)
        first = jnp.clip(first_ref[0, 0, 0, 0].astype(jnp.int32), 0, i)
        count = i - first
        carry = lax.fori_loop(0, count // 2, lambda pp, c: tiles([first + 2 * pp, first + 2 * pp + 1], c, False),
                              (jnp.zeros((t, hd), F32), zero, zero))
        carry = lax.fori_loop(0, count % 2, lambda _, c: tiles([i - 1], c, False), carry)
        dq, _, _ = tiles([i], carry, True)
        dq_ref[0] = (dq * scale).astype(BF16)

        @pl.when(i == nq - 1)
        def _():
            dk_ref[0] = dk_acc[...].astype(BF16)
            dv_ref[0] = dv_acc[...].astype(BF16)

        pl.when((hh == h - 1) & (i == nq - 1))(finish)

    full = pl.BlockSpec((1, s_len, hd), lambda hh, i: (hh, 0, 0))
    tile = pl.BlockSpec((1, t, hd), lambda hh, i: (hh, i, 0))
    hbm = pl.BlockSpec(memory_space=pl.ANY)
    out = SDS((h, s_len, hd), BF16)
    return pl.pallas_call(
        body, name=name, out_shape=(out, out, out, SDS(payload.shape, payload.dtype)), grid=(h, nq),
        in_specs=[tile, full, full, tile, pl.BlockSpec((1, t, 1), lambda hh, i: (hh, i, 0)),
                  pl.BlockSpec((1, 1) + SB_FIRST_BLOCK, lambda hh, i: (hh, i, 0, 0)), hbm],
        out_specs=(tile, full, full, hbm),
        scratch_shapes=[pltpu.VMEM((s_len, hd), F32), pltpu.VMEM((s_len, hd), F32)] + _COMM_SEMAPHORES,
        compiler_params=_params(has_side_effects=True),
    )(q, k, v, do, tot, first, payload)


def _dl_scores(qv, kk, n):
    s = lax.dot_general(qv, kk, (((1,), (1,)), ((), ())), preferred_element_type=F32) * (HEAD_DIM ** -0.5)
    r = lax.broadcasted_iota(jnp.int32, s.shape, 0)
    c = lax.broadcasted_iota(jnp.int32, s.shape, 1)
    valid = (c >= r) & (c - r <= BLOCK) & ((n > 0) | (c >= BLOCK))
    return jnp.where(valid, s, NEG)


DL_UNROLL = 4
DL_PAIR = 2 * HEAD_DIM
DL_Q_BLOCK0 = 0
DL_K_BLOCK0 = DL_WIDTH // DL_PAIR
DL_V_BLOCK0 = (IN_WIDTH - DL_WIDTH) // DL_PAIR
DL_DO_BLOCK0 = (SB_WIDTH + CV_WIDTH) // DL_PAIR


def _dl_rows(idx, nb, dil):
    r, n = idx // nb, idx % nb
    cur = pl.ds(r + n * (BLOCK * dil), BLOCK, stride=dil)
    prev = pl.ds(r + jnp.maximum(n - 1, 0) * (BLOCK * dil), BLOCK, stride=dil)
    return n, cur, prev


def _dl_window(ref, cur, prev):
    return jnp.concatenate([ref[prev, :], ref[cur, :]], axis=0).astype(BF16)


def _head_lanes():
    first = lax.broadcasted_iota(jnp.int32, (BLOCK, DL_PAIR), 1) < HEAD_DIM
    return first, jnp.logical_not(first)


def _dl_fwd(qk, u, dil, name):
    s_len = qk.shape[0]
    nb = s_len // dil // BLOCK

    def body(q_ref, k_ref, v_ref, o_ref, lse_ref):
        heads = _head_lanes()

        def step(idx, _):
            n, cur, prev = _dl_rows(idx, nb, dil)
            q = q_ref[cur, :]
            kk = _dl_window(k_ref, cur, prev)
            vv = _dl_window(v_ref, cur, prev)
            o, lse = None, None
            for lanes in heads:
                s = _dl_scores(jnp.where(lanes, q, 0.0).astype(BF16), kk, n)
                m = jnp.max(s, axis=-1, keepdims=True)
                p = jnp.exp(s - m)
                den = jnp.sum(p, axis=-1, keepdims=True)
                o_h = jnp.dot((p / den).astype(BF16), vv, preferred_element_type=F32)
                lse_h = jnp.broadcast_to(m + jnp.log(den), (BLOCK, DL_PAIR))
                o = o_h if o is None else jnp.where(heads[0], o, o_h)
                lse = lse_h if lse is None else jnp.where(heads[0], lse, lse_h)
            o_ref[cur, :] = o
            lse_ref[cur, :] = lse
            return 0

        lax.fori_loop(0, s_len // BLOCK, step, 0, unroll=DL_UNROLL)

    col = lambda first: pl.BlockSpec((s_len, DL_PAIR), lambda i: (0, first + i))
    out = SDS((s_len, DL_WIDTH), F32)
    return pl.pallas_call(body, name=name, out_shape=(out, out), grid=(DL_WIDTH // DL_PAIR,),
                          in_specs=[col(DL_Q_BLOCK0), col(DL_K_BLOCK0), col(DL_V_BLOCK0)], out_specs=(col(0), col(0)),
                          compiler_params=_params())(qk, qk, u)


def _dl_bwd(qk, u, dmix, o_mix, wt, lse, dil, name, payload=None):
    s_len = qk.shape[0]
    nb = s_len // dil // BLOCK
    scale = HEAD_DIM ** -0.5
    nt_dims = (((1,), (1,)), ((), ()))
    tn_dims = (((0,), (0,)), ((), ()))

    def body(q_ref, k_ref, v_ref, do_ref, om_ref, wt_ref, lse_ref, dq_ref, dk_ref, dv_ref):
        dk_ref[...] = jnp.zeros_like(dk_ref)
        dv_ref[...] = jnp.zeros_like(dv_ref)
        heads = _head_lanes()

        def step(idx, _):
            n, cur, prev = _dl_rows(idx, nb, dil)
            q = q_ref[cur, :]
            kk = _dl_window(k_ref, cur, prev)
            vv = _dl_window(v_ref, cur, prev)
            dov = do_ref[cur, :]
            d_lanes = dov * om_ref[cur, :]
            w_lanes = wt_ref[cur, :]
            lse_lanes = lse_ref[cur, :]
            dq, dkk, dvv = None, None, None
            for lanes in heads:
                qm = jnp.where(lanes, q, 0.0).astype(BF16)
                s = _dl_scores(qm, kk, n)
                p = jnp.exp(s - jnp.max(jnp.where(lanes, lse_lanes, NEG), axis=-1, keepdims=True))
                w = jnp.max(jnp.where(lanes, w_lanes, 0.0), axis=-1, keepdims=True)
                d_all = jnp.sum(jnp.where(lanes, d_lanes, 0.0), axis=-1, keepdims=True)
                do_n = jnp.where(lanes, dov * w, 0.0).astype(BF16)
                dp = lax.dot_general(do_n, vv, nt_dims, preferred_element_type=F32)
                ds = (p * (dp - w * d_all) * scale).astype(BF16)
                dq_h = jnp.dot(ds, kk, preferred_element_type=F32)
                dkk_h = lax.dot_general(ds, qm, tn_dims, preferred_element_type=F32)
                dvv_h = lax.dot_general(p.astype(BF16), do_n, tn_dims, preferred_element_type=F32)
                dq = dq_h if dq is None else jnp.where(heads[0], dq, dq_h)
                dkk = dkk_h if dkk is None else dkk + dkk_h
                dvv = dvv_h if dvv is None else dvv + dvv_h
            dq_ref[cur, :] = dq
            dk_ref[prev, :] += dkk[:BLOCK]
            dv_ref[prev, :] += dvv[:BLOCK]
            dk_ref[cur, :] += dkk[BLOCK:]
            dv_ref[cur, :] += dvv[BLOCK:]
            return 0

        lax.fori_loop(0, s_len // BLOCK, step, 0, unroll=DL_UNROLL)

    col = lambda first: pl.BlockSpec((s_len, DL_PAIR), lambda i: (0, first + i))
    out = SDS((s_len, DL_WIDTH), F32)
    steps = DL_WIDTH // DL_PAIR
    if payload is not None:
        step = lambda: pl.program_id(0)
        payload = ("all_to_all", payload, lambda: (step() == 0, None, step() == steps - 1))
    return _hosted_call(
        body, payload, name=name, out_shape=(out, out, out), grid=(steps,),
        in_specs=[col(DL_Q_BLOCK0), col(DL_K_BLOCK0), col(DL_V_BLOCK0), col(DL_DO_BLOCK0), col(0), col(0), col(0)],
        out_specs=(col(0), col(0), col(0)), args=(qk, qk, u, dmix, o_mix, wt, lse))


def _dl_mix_fwd(outs, lses, name):
    rows, w = outs[0].shape
    t = 256

    def body(o1, o2, o3, l1, l2, l3, ob_ref, of_ref, w1, w2, w3):
        a, b, c = l1[...], l2[...], l3[...]
        m = jnp.maximum(jnp.maximum(a, b), c)
        ea, eb, ec = jnp.exp(a - m), jnp.exp(b - m), jnp.exp(c - m)
        den = ea + eb + ec
        wa, wb, wc = ea / den, eb / den, ec / den
        o = wa * o1[...] + wb * o2[...] + wc * o3[...]
        ob_ref[...] = o.astype(BF16)
        of_ref[...] = o
        w1[...] = wa
        w2[...] = wb
        w3[...] = wc

    row = pl.BlockSpec((t, w), lambda i: (i, 0))
    f = SDS((rows, w), F32)
    return pl.pallas_call(body, name=name, out_shape=(SDS((rows, w), BF16), f, f, f, f), grid=(rows // t,),
                          in_specs=[row] * 6, out_specs=(row,) * 5, compiler_params=_params())(*outs, *lses)


def _x_probs(qh, kh):
    s = lax.dot_general(qh, kh, (((1,), (1,)), ((), ())), preferred_element_type=F32) * (X_HEAD_DIM ** -0.5)
    e = jnp.exp(s - jnp.max(s, axis=-1, keepdims=True))
    return e / jnp.sum(e, axis=-1, keepdims=True)


def _xattn_fwd(q, k, v, name):
    rows, d = q.shape
    t = 512

    def body(q_ref, k_ref, v_ref, o_ref):
        for hh in range(X_HEADS):
            cols = slice(hh * X_HEAD_DIM, (hh + 1) * X_HEAD_DIM)
            p = _x_probs(q_ref[:, cols], k_ref[:, cols])
            o_ref[:, cols] = jnp.dot(p.astype(BF16), v_ref[:, cols], preferred_element_type=F32).astype(BF16)

    row = pl.BlockSpec((t, d), lambda i: (i, 0))
    mem = pl.BlockSpec((N_MEM, d), lambda i: (0, 0))
    return pl.pallas_call(body, name=name, out_shape=SDS((rows, d), BF16), grid=(rows // t,), in_specs=[row, mem, mem],
                          out_specs=row, compiler_params=_params())(q, k, v)


def _xattn_bwd(q, k, v, do, name):
    rows, d = q.shape
    t = 512
    scale = X_HEAD_DIM ** -0.5

    def body(q_ref, k_ref, v_ref, do_ref, dq_ref, dk_ref, dv_ref):
        @pl.when(pl.program_id(0) == 0)
        def _():
            dk_ref[...] = jnp.zeros_like(dk_ref)
            dv_ref[...] = jnp.zeros_like(dv_ref)

        for hh in range(X_HEADS):
            cols = slice(hh * X_HEAD_DIM, (hh + 1) * X_HEAD_DIM)
            qh, kh, vh, doh = q_ref[:, cols], k_ref[:, cols], v_ref[:, cols], do_ref[:, cols]
            p = _x_probs(qh, kh)
            dp = lax.dot_general(doh, vh, (((1,), (1,)), ((), ())), preferred_element_type=F32)
            ds = (p * (dp - jnp.sum(p * dp, axis=-1, keepdims=True)) * scale).astype(BF16)
            dq_ref[:, cols] = jnp.dot(ds, kh, preferred_element_type=F32).astype(BF16)
            dk_ref[:, cols] += lax.dot_general(ds, qh, (((0,), (0,)), ((), ())), preferred_element_type=F32)
            dv_ref[:, cols] += lax.dot_general(p.astype(BF16), doh, (((0,), (0,)), ((), ())), preferred_element_type=F32)

    row = pl.BlockSpec((t, d), lambda i: (i, 0))
    mem = pl.BlockSpec((N_MEM, d), lambda i: (0, 0))
    return pl.pallas_call(
        body, name=name, out_shape=(SDS((rows, d), BF16), SDS((N_MEM, d), F32), SDS((N_MEM, d), F32)), grid=(rows // t,),
        in_specs=[row, mem, mem, row], out_specs=(row, mem, mem), compiler_params=_params(),
    )(q, k, v, do)


CV_TILE = 256
CV_HALO = 32
CV_LEAD = CV_HALO - (CV_KERNEL - 1)


def _shifted(win, off, rows):
    n = win.shape[0]
    return pltpu.roll(win, (n - off) % n, axis=0)[:rows]


def _glu(val, gate):
    return val * jax.nn.sigmoid(gate)


def _ln_parts(c):
    mu = jnp.mean(c, axis=-1, keepdims=True)
    xc = c - mu
    rstd = lax.rsqrt(jnp.mean(xc * xc, axis=-1, keepdims=True) + EPS)
    return xc * rstd, rstd


def _cv_fwd(u, cv_w, cv_b, ln_g, ln_b, name):
    rows = u.shape[0]
    t, w = CV_TILE, CV_WIDTH
    val_col = 3 * SB_WIDTH // w
    ratio = t // CV_HALO

    def body(val_ref, gate_ref, pval_ref, pgate_ref, w_ref, b_ref, g_ref, beta_ref, s_ref, c_ref):
        i = pl.program_id(0)
        hist = jnp.where(i > 0, _glu(pval_ref[...], pgate_ref[...]), 0.0)
        win = jnp.concatenate([hist, _glu(val_ref[...], gate_ref[...])], axis=0)
        acc = jnp.broadcast_to(b_ref[...], (t, w))
        for kk in range(CV_KERNEL):
            acc = acc + _shifted(win, CV_LEAD + kk, t) * w_ref[kk:kk + 1, :]
        c_ref[...] = acc
        n, _ = _ln_parts(acc)
        y = n * g_ref[...] + beta_ref[...]
        s_ref[...] = (y * jax.nn.sigmoid(y)).astype(BF16)

    cur = lambda col: pl.BlockSpec((t, w), lambda i: (i, col))
    prev = lambda col: pl.BlockSpec((CV_HALO, w), lambda i: (jnp.maximum(i * ratio - 1, 0), col))
    vec = pl.BlockSpec((1, w), lambda i: (0, 0))
    return pl.pallas_call(
        body, name=name, out_shape=(SDS((rows, w), BF16), SDS((rows, w), F32)), grid=(rows // t,),
        in_specs=[cur(val_col), cur(val_col + 1), prev(val_col), prev(val_col + 1),
                  pl.BlockSpec((CV_KERNEL, w), lambda i: (0, 0)), vec, vec, vec],
        out_specs=(pl.BlockSpec((t, w), lambda i: (i, 0)),) * 2, compiler_params=_params(),
    )(u, u, u, u, cv_w, cv_b, ln_g, ln_b)


def _cv_bwd(u, c, ds, db_out, cv_w, ln_g, ln_b, name):
    rows = u.shape[0]
    t, w = CV_TILE, CV_WIDTH
    val_col = 3 * SB_WIDTH // w
    ratio = t // CV_HALO
    nt = rows // t

    def conv_out_grad(c_v, ds_v, g_v, beta_v):
        n, rstd = _ln_parts(c_v)
        y = n * g_v + beta_v
        sig = jax.nn.sigmoid(y)
        dy = ds_v * (sig * (1.0 + y * (1.0 - sig)))
        dn = dy * g_v
        dc = rstd * (dn - jnp.mean(dn, axis=-1, keepdims=True) - n * jnp.mean(dn * n, axis=-1, keepdims=True))
        return dc, dy, n

    def body(val_ref, gate_ref, pval_ref, pgate_ref, c_ref, nc_ref, ds_ref, nds_ref, dbo_ref, w_ref, g_ref, beta_ref,
             dvg_ref, dw_ref, db_ref, dg_ref, dbeta_ref, dpwb_ref):
        i = pl.program_id(0)

        @pl.when(i == 0)
        def _():
            for r in (dw_ref, db_ref, dg_ref, dbeta_ref, dpwb_ref):
                r[...] = jnp.zeros_like(r)

        g_v, beta_v = g_ref[...], beta_ref[...]
        dc, dy, n = conv_out_grad(c_ref[...], ds_ref[...], g_v, beta_v)
        dc_next, _, _ = conv_out_grad(nc_ref[...], nds_ref[...], g_v, beta_v)
        dc_next = jnp.where(i < nt - 1, dc_next, 0.0)
        dg_ref[...] += jnp.sum(dy * n, axis=0, keepdims=True)
        dbeta_ref[...] += jnp.sum(dy, axis=0, keepdims=True)
        db_ref[...] += jnp.sum(dc, axis=0, keepdims=True)
        dpwb_ref[...] += jnp.sum(dbo_ref[...], axis=0, keepdims=True)

        val, gate = val_ref[...], gate_ref[...]
        hist = jnp.where(i > 0, _glu(pval_ref[...], pgate_ref[...]), 0.0)
        win = jnp.concatenate([hist, _glu(val, gate)], axis=0)
        dc_ext = jnp.concatenate([dc, dc_next], axis=0)
        dglu = jnp.zeros((t, w), F32)
        for kk in range(CV_KERNEL):
            dw_ref[kk:kk + 1, :] += jnp.sum(dc * _shifted(win, CV_LEAD + kk, t), axis=0, keepdims=True)
            dglu = dglu + _shifted(dc_ext, CV_KERNEL - 1 - kk, t) * w_ref[kk:kk + 1, :]
        sig = jax.nn.sigmoid(gate)
        dvg_ref[:, 0:w] = (dglu * sig).astype(BF16)
        dvg_ref[:, w:2 * w] = (dglu * val * sig * (1.0 - sig)).astype(BF16)

    cur = lambda col: pl.BlockSpec((t, w), lambda i: (i, col))
    prev = lambda col: pl.BlockSpec((CV_HALO, w), lambda i: (jnp.maximum(i * ratio - 1, 0), col))
    nxt = pl.BlockSpec((CV_HALO, w), lambda i: (jnp.minimum((i + 1) * ratio, rows // CV_HALO - 1), 0))
    vec = pl.BlockSpec((1, w), lambda i: (0, 0))
    return pl.pallas_call(
        body, name=name,
        out_shape=(SDS((rows, 2 * w), BF16), SDS((CV_HALO, w), F32), SDS((1, w), F32), SDS((1, w), F32), SDS((1, w), F32),
                   SDS((1, w), F32)),
        grid=(nt,),
        in_specs=[cur(val_col), cur(val_col + 1), prev(val_col), prev(val_col + 1), cur(0), nxt, cur(0), nxt, cur(0),
                  pl.BlockSpec((CV_KERNEL, w), lambda i: (0, 0)), vec, vec],
        out_specs=(pl.BlockSpec((t, 2 * w), lambda i: (i, 0)), pl.BlockSpec((CV_HALO, w), lambda i: (0, 0)), vec, vec, vec, vec),
        compiler_params=_params(),
    )(u, u, u, u, c, c, ds, ds, db_out, cv_w, ln_g, ln_b)


FFN_TILE = 512
FFN_COLS = 256
FFN_HALO = 8
FFN_KERNEL = 3
N_FF_BLOCKS = D_FF // FFN_COLS


def _conv3(prev8, cur, w_ref, b_ref, first):
    t = cur.shape[0]
    win = jnp.concatenate([jnp.where(first, 0.0, prev8), cur], axis=0)
    return (b_ref[...] + _shifted(win, FFN_HALO - 2, t) * w_ref[0:1, :] + _shifted(win, FFN_HALO - 1, t) * w_ref[1:2, :]
            + cur * w_ref[2:3, :])


def _gelu_gate(gate, val):
    return jax.nn.gelu(gate, approximate=True) * val


def _ffn_specs(t):
    ratio = t // FFN_HALO
    cur = pl.BlockSpec((t, FFN_COLS), lambda j, i: (i, j))
    prev = pl.BlockSpec((FFN_HALO, FFN_COLS), lambda j, i: (jnp.maximum(i * ratio - 1, 0), j))
    wsp = pl.BlockSpec((FFN_KERNEL, FFN_COLS), lambda j, i: (0, j))
    bsp = pl.BlockSpec((1, FFN_COLS), lambda j, i: (0, j))
    return cur, prev, wsp, bsp


def _ffn_host_steps(row_tiles):
    def when():
        j, i = pl.program_id(0), pl.program_id(1)
        return (j == 0) & (i == 0), (j == (3 * N_FF_BLOCKS) // 4) & (i == 0), (j == N_FF_BLOCKS - 1) & (i == row_tiles - 1)
    return when


def _ffn_act_fwd(up_g, up_v, w_g, w_v, b_g, b_v, name, payload=None):
    rows = up_g.shape[0]
    t = FFN_TILE
    cur, prev, wsp, bsp = _ffn_specs(t)

    def body(g_ref, v_ref, pg_ref, pv_ref, wg_ref, wv_ref, bg_ref, bv_ref, o_ref):
        first = pl.program_id(1) == 0
        gate = _conv3(pg_ref[...], g_ref[...], wg_ref, bg_ref, first)
        val = _conv3(pv_ref[...], v_ref[...], wv_ref, bv_ref, first)
        o_ref[...] = _gelu_gate(gate, val).astype(BF16)

    if payload is not None:
        payload = ("gather", payload, _ffn_host_steps(rows // t))
    return _hosted_call(
        body, payload, name=name, out_shape=(SDS((rows, D_FF), BF16),), grid=(N_FF_BLOCKS, rows // t),
        in_specs=[cur, cur, prev, prev, wsp, wsp, bsp, bsp], out_specs=(cur,), args=(up_g, up_v, up_g, up_v, w_g, w_v, b_g, b_v))


def _ffn_act_bwd(up_g, up_v, dact, w_g, w_v, b_g, b_v, name, payload=None):
    rows = up_g.shape[0]
    t = FFN_TILE
    te = t + FFN_HALO
    ratio = t // FFN_HALO
    nt = rows // t
    cur, prev, wsp, bsp = _ffn_specs(t)
    nxt = pl.BlockSpec((FFN_HALO, FFN_COLS), lambda j, i: (jnp.minimum((i + 1) * ratio, rows // FFN_HALO - 1), j))

    def conv_ext(pre, x, nx, w_ref, b_ref, first):
        win = jnp.concatenate([jnp.where(first, 0.0, pre), x, nx], axis=0)
        out = (b_ref[...] + _shifted(win, FFN_HALO - 2, te) * w_ref[0:1, :] + _shifted(win, FFN_HALO - 1, te) * w_ref[1:2, :]
               + _shifted(win, FFN_HALO, te) * w_ref[2:3, :])
        return out, win

    def body(g_ref, v_ref, pg_ref, pv_ref, ng_ref, nv_ref, da_ref, nda_ref, wg_ref, wv_ref, bg_ref, bv_ref,
             dug_ref, duv_ref, dwg_ref, dwv_ref, dbg_ref, dbv_ref):
        i = pl.program_id(1)
        first = i == 0
        gate, win_g = conv_ext(pg_ref[...], g_ref[...], ng_ref[...], wg_ref, bg_ref, first)
        val, win_v = conv_ext(pv_ref[...], v_ref[...], nv_ref[...], wv_ref, bv_ref, first)
        da = jnp.concatenate([da_ref[...], jnp.where(i < nt - 1, nda_ref[...], 0.0)], axis=0)
        _, vjp = jax.vjp(_gelu_gate, gate, val)
        dgate, dval = vjp(da)

        @pl.when(first)
        def _():
            for r in (dwg_ref, dwv_ref, dbg_ref, dbv_ref):
                r[...] = jnp.zeros_like(r)

        for dc_ext, win, w_ref, du_ref, dw_ref, db_ref in ((dgate, win_g, wg_ref, dug_ref, dwg_ref, dbg_ref),
                                                          (dval, win_v, wv_ref, duv_ref, dwv_ref, dbv_ref)):
            dc = dc_ext[:t]
            du_ref[...] = (dc * w_ref[2:3, :] + _shifted(dc_ext, 1, t) * w_ref[1:2, :]
                           + _shifted(dc_ext, 2, t) * w_ref[0:1, :]).astype(BF16)
            for kk in range(FFN_KERNEL):
                dw_ref[kk:kk + 1, :] += jnp.sum(dc * _shifted(win, FFN_HALO - 2 + kk, t), axis=0, keepdims=True)
            db_ref[...] += jnp.sum(dc, axis=0, keepdims=True)

    big, wshape, bshape = SDS((rows, D_FF), BF16), SDS((FFN_KERNEL, D_FF), F32), SDS((1, D_FF), F32)
    if payload is not None:
        payload = ("all_to_all", payload, _ffn_host_steps(nt))
    return _hosted_call(
        body, payload, name=name, out_shape=(big, big, wshape, wshape, bshape, bshape), grid=(N_FF_BLOCKS, nt),
        in_specs=[cur, cur, prev, prev, nxt, nxt, cur, nxt, wsp, wsp, bsp, bsp], out_specs=(cur, cur, wsp, wsp, bsp, bsp),
        args=(up_g, up_v, up_g, up_v, up_g, up_v, dact, dact, w_g, w_v, b_g, b_v))


def _adamw_update(parts, w_ref, m_ref, v_ref, g_ref, d_ref, nm_ref, nv_ref):
    g = parts[0].astype(F32)
    for s in range(1, N_DEV):
        g = g + parts[s].astype(F32)
    nm = ADAM_B1 * m_ref[...] + (1.0 - ADAM_B1) * g
    nv = ADAM_B2 * v_ref[...] + (1.0 - ADAM_B2) * jnp.square(g)
    m_hat = nm / (1.0 - ADAM_B1 ** ADAM_STEP)
    v_hat = nv / (1.0 - ADAM_B2 ** ADAM_STEP)
    g_ref[...] = g
    d_ref[...] = -ADAM_LR * (m_hat / (jnp.sqrt(v_hat) + ADAM_EPS) + ADAM_WD * w_ref[...])
    nm_ref[...] = nm
    nv_ref[...] = nv


def _adamw(parts, w, m, v, name):
    rows, cols = w.shape
    t = _pick(rows, (512, 256, 128)) if rows > 512 else rows

    def body(p_ref, *refs):
        _adamw_update(p_ref[...], *refs)

    row = pl.BlockSpec((t, cols), lambda i: (i, 0))
    out = SDS((rows, cols), F32)
    return pl.pallas_call(
        body, name=name, out_shape=(out,) * 4, grid=(rows // t,),
        in_specs=[pl.BlockSpec((N_DEV, t, cols), lambda i: (0, i, 0)), row, row, row], out_specs=(row,) * 4,
        compiler_params=_params(),
    )(parts, w, m, v)


def _adamw_packed(sources, w, m, v, name):
    rows, cols = w.shape
    t = ADAMW_ROW_BLOCK
    nb = rows // DEPTH // t
    (src0, first0), (src1, first1) = sources
    assert first0 % t == 0 and first1 % t == 0 and rows % (DEPTH * t) == 0

    def body(p0_ref, p1_ref, *refs):
        layer = pl.program_id(0)
        _adamw_update(jnp.where(layer == 0, p0_ref[...], p1_ref[...]), *refs)

    spec0 = pl.BlockSpec((N_DEV, t, cols), lambda l, i: (0, first0 // t + i * (1 - l) + (nb - 1) * l, 0))
    spec1 = pl.BlockSpec((N_DEV, t, cols), lambda l, i: (0, first1 // t + i * l, 0))
    row = pl.BlockSpec((t, cols), lambda l, i: (l * nb + i, 0))
    out = SDS((rows, cols), F32)
    return pl.pallas_call(body, name=name, out_shape=(out,) * 4, grid=(DEPTH, nb), in_specs=[spec0, spec1, row, row, row],
                          out_specs=(row,) * 4, compiler_params=_params())(src0, src1, w, m, v)


_COMM_SEMAPHORES = [pltpu.SemaphoreType.DMA((N_DEV - 1,)), pltpu.SemaphoreType.DMA((N_DEV - 1,)), pltpu.SemaphoreType.DMA]


def _gather_steps(x_ref, out_ref, send_sems, recv_sems, local_sem):
    x_, y_, c_ = lax.axis_index("x"), lax.axis_index("y"), lax.axis_index("c")
    me, sibling = (x_, y_, c_), (x_, y_, 1 - c_)
    chips = [(1 - x_, y_), (x_, 1 - y_), (1 - x_, 1 - y_)]

    def slot(px, py, pc):
        return out_ref.at[4 * px + 2 * py + pc]

    def copy(kk, block, to, src=None):
        return pltpu.make_async_remote_copy(
            src_ref=slot(*block) if src is None else src, dst_ref=slot(*block),
            send_sem=send_sems.at[kk], recv_sem=recv_sems.at[kk], device_id=to, device_id_type=MESH)

    def mine():
        return pltpu.make_async_copy(x_ref, slot(*me), local_sem)

    def first():
        return [copy(0, me, sibling, src=x_ref)] + [copy(1 + j, me, (*chip, c_), src=x_ref) for j, chip in enumerate(chips)]

    def passed():
        return [copy(4 + j, (*chip, c_), sibling) for j, chip in enumerate(chips)]

    def start():
        mine().start()
        for cp in first():
            cp.start()

    def forward():
        for j, (chip, cp) in enumerate(zip(chips, passed())):
            copy(1 + j, (*chip, c_), me).wait_recv()
            cp.start()

    def finish():
        copy(0, sibling, me).wait_recv()
        for j, chip in enumerate(chips):
            copy(4 + j, (*chip, 1 - c_), me).wait_recv()
        for cp in first() + passed():
            cp.wait_send()
        mine().wait()

    return start, forward, finish


def _exchange_steps(x_ref, out_ref, send_sems, recv_sems, local_sem):
    x_, y_, c_ = lax.axis_index("x"), lax.axis_index("y"), lax.axis_index("c")
    me = 4 * x_ + 2 * y_ + c_

    def mine():
        return pltpu.make_async_copy(x_ref.at[me], out_ref.at[me], local_sem)

    def copies():
        out = []
        for r in range(1, N_DEV):
            px = 1 - x_ if r & 4 else x_
            py = 1 - y_ if r & 2 else y_
            pc = 1 - c_ if r & 1 else c_
            out.append(pltpu.make_async_remote_copy(
                src_ref=x_ref.at[4 * px + 2 * py + pc], dst_ref=out_ref.at[me],
                send_sem=send_sems.at[r - 1], recv_sem=recv_sems.at[r - 1], device_id=(px, py, pc), device_id_type=MESH))
        return out

    def start():
        mine().start()
        for cp in copies():
            cp.start()

    def finish():
        for cp in copies():
            cp.wait_recv()
        for cp in copies():
            cp.wait_send()
        mine().wait()

    return start, finish


def _hosted_call(body, exchange, *, name, out_shape, grid, in_specs, out_specs, args, scratch_shapes=()):
    if exchange is None:
        return pl.pallas_call(body, name=name, out_shape=out_shape, grid=grid, in_specs=in_specs, out_specs=out_specs,
                              scratch_shapes=list(scratch_shapes), compiler_params=_params())(*args)
    kind, payload, when = exchange
    n_in, n_out, n_scratch = len(in_specs), len(out_specs), len(scratch_shapes)
    result = SDS((N_DEV,) + payload.shape, payload.dtype) if kind == "gather" else SDS(payload.shape, payload.dtype)

    def hosting(*refs):
        ins, pay_ref = refs[:n_in], refs[n_in]
        outs, res_ref = refs[n_in + 1:n_in + 1 + n_out], refs[n_in + 1 + n_out]
        rest = refs[n_in + 2 + n_out:]
        scratch, sems = rest[:n_scratch], rest[n_scratch:]
        first, middle, last = when()
        if kind == "gather":
            start, forward, finish = _gather_steps(pay_ref, res_ref, *sems)
            pl.when(first)(start)
            pl.when(middle)(forward)
        else:
            start, finish = _exchange_steps(pay_ref, res_ref, *sems)
            pl.when(first)(start)
        body(*ins, *outs, *scratch)
        pl.when(last)(finish)

    hbm = pl.BlockSpec(memory_space=pl.ANY)
    return pl.pallas_call(
        hosting, name=name, out_shape=tuple(out_shape) + (result,), grid=grid, in_specs=list(in_specs) + [hbm],
        out_specs=tuple(out_specs) + (hbm,), scratch_shapes=list(scratch_shapes) + _COMM_SEMAPHORES,
        compiler_params=_params(has_side_effects=True),
    )(*args, payload)


def _all_gather(x, name):
    def body(x_ref, out_ref, send_sems, recv_sems, local_sem):
        for step in _gather_steps(x_ref, out_ref, send_sems, recv_sems, local_sem):
            step()

    hbm = pl.BlockSpec(memory_space=pl.ANY)
    return pl.pallas_call(body, name=name, out_shape=SDS((N_DEV,) + x.shape, x.dtype), in_specs=[hbm], out_specs=hbm,
                          scratch_shapes=_COMM_SEMAPHORES, compiler_params=pltpu.CompilerParams(has_side_effects=True))(x)


def _all_to_all(x, name):
    def body(x_ref, out_ref, send_sems, recv_sems, local_sem):
        for step in _exchange_steps(x_ref, out_ref, send_sems, recv_sems, local_sem):
            step()

    hbm = pl.BlockSpec(memory_space=pl.ANY)
    return pl.pallas_call(body, name=name, out_shape=SDS(x.shape, x.dtype), in_specs=[hbm], out_specs=hbm,
                          scratch_shapes=_COMM_SEMAPHORES, compiler_params=pltpu.CompilerParams(has_side_effects=True))(x)


BIG = ("w_in", "cv_pw_w", "w_out", "x_wq", "x_wk", "x_wv", "x_wo", "ffn_w_up", "ffn_w_down")
_MIXER = (("w_in", 352), ("w_out", 128))
_FFN = (("ffn_w_up", 704), ("ffn_w_down", 352))
_CROSS = (("x_wq", 128), ("x_wk", 128), ("x_wv", 128), ("x_wo", 128))
GROUPS = {
    "a": tuple((n, 0, r) for n, r in _MIXER),
    "m1": tuple((n, 1, r) for n, r in _MIXER),
    "b": tuple((n, 0, r) for n, r in _FFN + _CROSS),
    "c": tuple((n, 1, r) for n, r in _FFN + _CROSS),
    "bf": tuple((n, 0, r) for n, r in _FFN),
    "cf": tuple((n, 1, r) for n, r in _FFN),
    "bx": tuple((n, 0, r) for n, r in _CROSS),
    "cx": tuple((n, 1, r) for n, r in _CROSS),
}
GRADIENT_GROUPS = ("a", "m1", "bf", "bx", "cf", "cx")
TRANSPOSED = ("w_in", "ffn_w_up")
PW_ROWS = 16
ADAMW_ROW_BLOCK = 32


def _group_rows(group):
    out, first = {}, 0
    for n, l, r in GROUPS[group]:
        out[(n, l)] = (first, r)
        first += r
    return out


def _where_is(name, layer):
    for group in GRADIENT_GROUPS:
        rows = _group_rows(group)
        if (name, layer) in rows:
            return (group,) + rows[(name, layer)]
    raise KeyError((name, layer))


def _pack_weights(group, wts):
    pieces = []
    for n, l, _ in GROUPS[group]:
        w = wts[n][l].astype(BF16)
        pieces.append(w.T if n in TRANSPOSED else w)
    if group == "a":
        pieces.append(wts["cv_pw_w"].astype(BF16).reshape(PW_ROWS, PAYLOAD_COLS))
    return jnp.concatenate(pieces, axis=0)


def _unpack_weights(group, gathered):
    out = {}
    half = N_DEV // 2
    for (n, l), (first, r) in _group_rows(group).items():
        block = gathered[:, first:first + r, :]
        if n == "ffn_w_up":
            out[(n, l)] = (block[:half].reshape(half * r, PAYLOAD_COLS), block[half:].reshape(half * r, PAYLOAD_COLS))
        else:
            out[(n, l)] = block.reshape(N_DEV * r, PAYLOAD_COLS)
    return out


def _pack_grads(group, grads):
    pieces = []
    for n, l, r in GROUPS[group]:
        g = grads[n][l]
        parts = g if isinstance(g, tuple) else (g,)
        pieces.append(jnp.concatenate([p.reshape(-1, r, PAYLOAD_COLS) for p in parts], axis=0))
    if group == "a":
        pieces.append(_to_shards("cv_pw_w", jnp.stack(grads["cv_pw_w"])).reshape(N_DEV, PW_ROWS, PAYLOAD_COLS))
    return jnp.concatenate(pieces, axis=1)


COL_SHARDED = ("w_in", "ffn_w_up", "cv_w", "ffn_conv_w")
SMALL_SHARDED = ("cv_w", "ffn_conv_w")
REPLICATED = ("mix_norm_pre", "cv_b", "cv_ln_g", "cv_ln_b", "cv_pw_b", "mix_norm_post", "x_norm_pre", "mem_norm",
              "x_norm_post", "ffn_norm_pre", "ffn_conv_b", "ffn_norm_post")
WEIGHTS = ("mix_norm_pre", "w_in", "cv_w", "cv_b", "cv_ln_g", "cv_ln_b", "cv_pw_w", "cv_pw_b", "w_out", "mix_norm_post",
           "x_norm_pre", "mem_norm", "x_wq", "x_wk", "x_wv", "x_wo", "x_norm_post", "ffn_norm_pre", "ffn_w_up",
           "ffn_conv_w", "ffn_conv_b", "ffn_w_down", "ffn_norm_post")
PAYLOAD_COLS = 1024


PAYLOAD_ROW_TILE = 16


def _pad_rows(flat, cols):
    n = flat.shape[-1]
    rows = -(-n // (cols * PAYLOAD_ROW_TILE)) * PAYLOAD_ROW_TILE
    pad = rows * cols - n
    if pad:
        flat = jnp.concatenate([flat, jnp.zeros(flat.shape[:-1] + (pad,), flat.dtype)], axis=-1)
    return flat.reshape(flat.shape[:-1] + (rows, cols))


def _unshard(name, parts):
    n, depth, r, c = parts.shape
    if name in COL_SHARDED:
        return parts.transpose(1, 2, 0, 3).reshape(depth, r, n * c)
    return parts.transpose(1, 0, 2, 3).reshape(depth, n * r, c)


def _to_shards(name, full):
    depth, r, c = full.shape
    if name in COL_SHARDED:
        return full.reshape(depth, r, N_DEV, c // N_DEV).transpose(2, 0, 1, 3).reshape(N_DEV, -1)
    return full.reshape(depth, N_DEV, r // N_DEV, c).transpose(1, 0, 2, 3).reshape(N_DEV, -1)


def _heads_major(x, h):
    return x.reshape(x.shape[0], h, HEAD_DIM).transpose(1, 0, 2)


def _tokens_major(x):
    return x.transpose(1, 0, 2).reshape(x.shape[1], -1)


def _ffn_halves(p):
    w, b = p["ffn_conv_w"], p["ffn_conv_b"]
    return w[:, :D_FF], w[:, D_FF:], b[:, :D_FF], b[:, D_FF:]


def _layer_fwd(l, h, hn, p, mem, cos, sin, g_next, payload, unpack, ffn_payload):
    p = dict(p)
    sv = {"h0": h, "hn0": hn}
    u = _mm(hn, p["w_in"], "nt", F32, f"l{l}_in_proj")
    sv["u"] = u
    sb = _heads_major(u[:, :3 * SB_WIDTH].astype(BF16), 3 * SB_HEADS)
    sb_q, sb_k, sb_v = sb[:SB_HEADS], sb[SB_HEADS:2 * SB_HEADS], sb[2 * SB_HEADS:]
    a_out, sb_tot, sb_first, gathered = _sb_fwd(sb_q, sb_k, sb_v, payload, f"l{l}_sb_fwd")
    p.update(unpack(gathered))
    sv.update(sb_q=sb_q, sb_k=sb_k, sb_v=sb_v, sb_tot=sb_tot, sb_first=sb_first, p=p)

    cv_s, cv_c = _cv_fwd(u, p["cv_w"], p["cv_b"], p["cv_ln_g"], p["cv_ln_b"], f"l{l}_cv_fwd")
    b_out = _mm(cv_s, p["cv_pw_w"], "nn", BF16, f"l{l}_cv_pw", bias=p["cv_pw_b"])
    sv.update(cv_s=cv_s, cv_c=cv_c)

    qk = _rope_fwd(u, cos, sin, f"l{l}_rope_fwd")
    outs, lses = [], []
    for b, (_, dil) in enumerate(DL_PATTERN):
        o, lse = _dl_fwd(qk, u, dil, f"l{l}_dl{b}_fwd")
        outs.append(o)
        lses.append(lse)
    c_out, c_out_f32, w1, w2, w3 = _dl_mix_fwd(outs, lses, f"l{l}_dl_mix")
    sv.update(dl_qk=qk, dl_lse=lses, dl_o=c_out_f32, dl_w=(w1, w2, w3))

    mix = jnp.concatenate([_tokens_major(a_out), b_out, c_out], axis=-1)
    y = _mm(mix, p["w_out"], "nn", F32, f"l{l}_out_proj")
    h1, hn1 = _res_norm_fwd(h, y, p["mix_norm_post"], p["x_norm_pre"], f"l{l}_mix_post")
    sv.update(mix=mix, y_mix=y, h1=h1, hn1=hn1)

    xq = _mm(hn1, p["x_wq"], "nn", BF16, f"l{l}_xq")
    memn = _rms_fwd(mem, p["mem_norm"], f"l{l}_mem_norm")
    xk = _mm(memn, p["x_wk"], "nn", BF16, f"l{l}_xk")
    xv = _mm(memn, p["x_wv"], "nn", BF16, f"l{l}_xv")
    xo = _xattn_fwd(xq, xk, xv, f"l{l}_xattn_fwd")
    y = _mm(xo, p["x_wo"], "nn", F32, f"l{l}_xo_proj")
    h2, hn2 = _res_norm_fwd(h1, y, p["x_norm_post"], p["ffn_norm_pre"], f"l{l}_x_post")
    sv.update(xq=xq, xk=xk, xv=xv, xo=xo, memn=memn, y_x=y, h2=h2, hn2=hn2)

    up_g = _mm(hn2, p["ffn_w_up"][0], "nt", F32, f"l{l}_ffn_up_gate")
    up_v = _mm(hn2, p["ffn_w_up"][1], "nt", F32, f"l{l}_ffn_up_val")
    act, *ffn_gathered = _ffn_act_fwd(up_g, up_v, *_ffn_halves(p), f"l{l}_ffn_act", ffn_payload)
    y = _mm(act, p["ffn_w_down"], "nn", F32, f"l{l}_ffn_down")
    h3, hn3 = _res_norm_fwd(h2, y, p["ffn_norm_post"], g_next, f"l{l}_ffn_post")
    sv.update(up_g=up_g, up_v=up_v, act=act, y_ffn=y)
    return h3, hn3, sv, (ffn_gathered[0] if ffn_gathered else None)


def _layer_bwd(l, dh, dy, sv, mem, cos, sin, prev_post, ffn_payload, pack):
    p = sv["p"]
    gr = {}
    received = {}
    dact = _mm(dy, p["ffn_w_down"], "nt", F32, f"l{l}_d_act")
    gr["ffn_w_down"] = _mm(sv["act"], dy, "tn", BF16, f"l{l}_dw_down")
    dup_g, dup_v, dwg, dwv, dbg, dbv, *got = _ffn_act_bwd(sv["up_g"], sv["up_v"], dact, *_ffn_halves(p), f"l{l}_ffn_act_bwd",
                                                         ffn_payload)
    if got:
        received["ffn_payload"] = got[0]
    gr["ffn_conv_w"] = jnp.concatenate([dwg, dwv], axis=1)
    gr["ffn_conv_b"] = jnp.concatenate([dbg, dbv], axis=1)
    dhn = (_mm(dup_g, p["ffn_w_up"][0], "nn", F32, f"l{l}_d_hn2_gate"), _mm(dup_v, p["ffn_w_up"][1], "nn", F32, f"l{l}_d_hn2_val"))
    gr["ffn_w_up"] = (_mm(dup_g, sv["hn2"], "tn", BF16, f"l{l}_dw_up_gate"), _mm(dup_v, sv["hn2"], "tn", BF16, f"l{l}_dw_up_val"))
    dh, dy, gr["ffn_norm_pre"], gr["x_norm_post"] = _norm_bwd(
        dh, (sv["h2"], p["ffn_norm_pre"], dhn), (sv["y_x"], p["x_norm_post"]), f"l{l}_x_post_bwd")

    do = _mm(dy, p["x_wo"], "nt", BF16, f"l{l}_d_xo")
    gr["x_wo"] = _mm(sv["xo"], dy, "tn", BF16, f"l{l}_dw_xo")
    dq, dk, dv = _xattn_bwd(sv["xq"], sv["xk"], sv["xv"], do, f"l{l}_xattn_bwd")
    dhn = _mm(dq, p["x_wq"], "nt", F32, f"l{l}_d_hn1")
    gr["x_wq"] = _mm(sv["hn1"], dq, "tn", BF16, f"l{l}_dw_xq")
    gr["x_wk"] = _mm(sv["memn"], dk, "tn", BF16, f"l{l}_dw_xk")
    gr["x_wv"] = _mm(sv["memn"], dv, "tn", BF16, f"l{l}_dw_xv")
    dmemn = _mm(dk, p["x_wk"], "nt", F32, f"l{l}_d_memn_k") + _mm(dv, p["x_wv"], "nt", F32, f"l{l}_d_memn_v")
    gr["mem_norm"] = _rms_gain_grad(mem, p["mem_norm"], dmemn, f"l{l}_mem_norm_bwd")
    dh, dy, gr["x_norm_pre"], gr["mix_norm_post"] = _norm_bwd(
        dh, (sv["h1"], p["x_norm_pre"], dhn), (sv["y_mix"], p["mix_norm_post"]), f"l{l}_mix_post_bwd")

    dmix = _mm(dy, p["w_out"], "nt", F32, f"l{l}_d_mix")
    gr["w_out"] = _mm(sv["mix"], dy, "tn", BF16, f"l{l}_dw_out")
    do_a = _heads_major(dmix[:, :SB_WIDTH].astype(BF16), SB_HEADS)
    dq, dk, dv, received["ffn"] = _sb_bwd(sv["sb_q"], sv["sb_k"], sv["sb_v"], do_a, sv["sb_tot"], sv["sb_first"],
                                          pack("ffn", gr), f"l{l}_sb_bwd")
    du_sb = _tokens_major(jnp.concatenate([dq, dk, dv], axis=0))

    db_out = dmix[:, SB_WIDTH:SB_WIDTH + CV_WIDTH]
    ds = _mm(db_out, p["cv_pw_w"], "nt", F32, f"l{l}_d_cv_s")
    gr["cv_pw_w"] = _mm(sv["cv_s"], db_out, "tn", BF16, f"l{l}_dw_cv_pw")
    du_cv, dcvw, gr["cv_b"], gr["cv_ln_g"], gr["cv_ln_b"], gr["cv_pw_b"] = _cv_bwd(
        sv["u"], sv["cv_c"], ds, db_out, p["cv_w"], p["cv_ln_g"], p["cv_ln_b"], f"l{l}_cv_bwd")
    gr["cv_w"] = dcvw[:CV_KERNEL]

    dqs, dks, dvs = [], [], []
    for b, (_, dil) in enumerate(DL_PATTERN):
        carried = pack("cross", gr) if b == len(DL_PATTERN) - 1 else None
        dq, dk, dv, *got = _dl_bwd(sv["dl_qk"], sv["u"], dmix, sv["dl_o"], sv["dl_w"][b], sv["dl_lse"][b], dil,
                                   f"l{l}_dl{b}_bwd", carried)
        if got:
            received["cross"] = got[0]
        dqs.append(dq)
        dks.append(dk)
        dvs.append(dv)
    du_dl = _rope_bwd(dqs, dks, dvs, cos, sin, f"l{l}_rope_bwd")

    du = jnp.concatenate([du_sb, du_cv, du_dl], axis=-1)
    dhn = _mm(du, p["w_in"], "nn", F32, f"l{l}_d_hn0")
    gr["w_in"] = _mm(du, sv["hn0"], "tn", BF16, f"l{l}_dw_in")
    dh, dy, gr["mix_norm_pre"], dg_prev = _norm_bwd(dh, (sv["h0"], p["mix_norm_pre"], dhn), prev_post, f"l{l}_in_bwd")
    return dh, dy, gr, dg_prev, received


def kernel(x, mem, positions, mix_norm_pre, w_in, cv_w, cv_b, cv_ln_g, cv_ln_b, cv_pw_w, cv_pw_b, w_out, mix_norm_post, x_norm_pre, mem_norm, x_wq, x_wk, x_wv, x_wo, x_norm_post, ffn_norm_pre, ffn_w_up, ffn_conv_w, ffn_conv_b, ffn_w_down, ffn_norm_post, loss_target, m_mix_norm_pre, m_w_in, m_cv_w, m_cv_b, m_cv_ln_g, m_cv_ln_b, m_cv_pw_w, m_cv_pw_b, m_w_out, m_mix_norm_post, m_x_norm_pre, m_mem_norm, m_x_wq, m_x_wk, m_x_wv, m_x_wo, m_x_norm_post, m_ffn_norm_pre, m_ffn_w_up, m_ffn_conv_w, m_ffn_conv_b, m_ffn_w_down, m_ffn_norm_post, v_mix_norm_pre, v_w_in, v_cv_w, v_cv_b, v_cv_ln_g, v_cv_ln_b, v_cv_pw_w, v_cv_pw_b, v_w_out, v_mix_norm_post, v_x_norm_pre, v_mem_norm, v_x_wq, v_x_wk, v_x_wv, v_x_wo, v_x_norm_post, v_ffn_norm_pre, v_ffn_w_up, v_ffn_conv_w, v_ffn_conv_b, v_ffn_w_down, v_ffn_norm_post):
    args = locals()
    wts = {n: args[n] for n in WEIGHTS}
    mom = {n: args["m_" + n] for n in WEIGHTS}
    var = {n: args["v_" + n] for n in WEIGHTS}

    x2, mem2, target = x[0], mem[0], loss_target[0]

    gathered_a = _all_gather(_pack_weights("a", wts), "weights_all_gather")
    small_payload = _pad_rows(jnp.concatenate([wts[n].reshape(-1) for n in SMALL_SHARDED]), PAYLOAD_COLS)
    small = _all_gather(small_payload, "small_weights_all_gather").reshape(N_DEV, -1)
    small_full = {}
    off = 0
    for n in SMALL_SHARDED:
        size = wts[n].size
        small_full[n] = _unshard(n, small[:, off:off + size].reshape((N_DEV,) + wts[n].shape))
        off += size
    pw_first = sum(r for _, _, r in GROUPS["a"])
    pw_full = _unshard("cv_pw_w", gathered_a[:, pw_first:, :].reshape((N_DEV,) + wts["cv_pw_w"].shape))

    def mixer_params(l, unpacked):
        p = {n: wts[n][l][None, :] for n in REPLICATED}
        p.update({n: small_full[n][l] for n in SMALL_SHARDED})
        p.update(cv_pw_w=pw_full[l], w_in=unpacked[("w_in", l)], w_out=unpacked[("w_out", l)])
        return p

    def of_layer(group, l):
        return lambda gathered: {n: w for (n, ll), w in _unpack_weights(group, gathered).items() if ll == l}

    pos = positions[0].astype(F32)
    half = HEAD_DIM // 2
    inv_freq = ROPE_THETA ** (-jnp.arange(half, dtype=F32) / half)
    ang = pos[:, None] * inv_freq
    cos = jnp.tile(jnp.cos(ang), (1, LANES // half))
    sin = jnp.tile(jnp.sin(ang), (1, LANES // half))

    p0 = mixer_params(0, _unpack_weights("a", gathered_a))
    hn = _rms_fwd(x2, p0["mix_norm_pre"], "l0_in_norm")
    h, hn, sv0, gathered_m1 = _layer_fwd(0, x2, hn, p0, mem2, cos, sin, wts["mix_norm_pre"][1][None, :],
                                         _pack_weights("b", wts), of_layer("b", 0), _pack_weights("m1", wts))
    p1 = mixer_params(1, _unpack_weights("m1", gathered_m1))
    h, _, sv1, _ = _layer_fwd(1, h, hn, p1, mem2, cos, sin, None, _pack_weights("c", wts), of_layer("c", 1), None)
    loss_part, dh = _loss_fwd(h, target, "loss")
    loss = lax.psum(loss_part[0, 0], ("x", "y", "c"))

    grads = {n: [None] * DEPTH for n in WEIGHTS}
    dh, dy, _, grads["ffn_norm_post"][1] = _norm_bwd(dh, None, (sv1["y_ffn"], sv1["p"]["ffn_norm_post"]), "last_post_bwd")

    def packer(l, groups):
        return lambda which, gr: _pack_grads(groups[which], {n: {l: g} for n, g in gr.items()})

    dh, dy, gr, grads["ffn_norm_post"][0], got1 = _layer_bwd(
        1, dh, dy, sv1, mem2, cos, sin, (sv0["y_ffn"], sv0["p"]["ffn_norm_post"]), None, packer(1, {"ffn": "cf", "cross": "cx"}))
    for n, g in gr.items():
        grads[n][1] = g
    dh, _, gr, _, got0 = _layer_bwd(0, dh, dy, sv0, mem2, cos, sin, None, _pack_grads("m1", grads),
                                    packer(0, {"ffn": "bf", "cross": "bx"}))
    for n, g in gr.items():
        grads[n][0] = g
    grad_x = dh

    received_a = _all_to_all(_pack_grads("a", grads), "grads_all_to_all")
    received = {"a": received_a, "m1": got0["ffn_payload"], "bf": got0["ffn"], "bx": got0["cross"],
                "cf": got1["ffn"], "cx": got1["cross"]}
    small_rows = jnp.concatenate([_to_shards(n, jnp.stack(grads[n])) for n in SMALL_SHARDED], axis=1)
    rep_flat = jnp.concatenate([jnp.stack([g.reshape(-1) for g in grads[n]]).reshape(-1) for n in REPLICATED])
    rep_rows = jnp.broadcast_to(rep_flat[None], (N_DEV, rep_flat.shape[0]))
    f32_rows = _pad_rows(jnp.concatenate([small_rows, rep_rows], axis=1), PAYLOAD_COLS)
    small_parts = _all_to_all(f32_rows, "small_grads_all_to_all")

    res = {}
    for n in BIG:
        shape = wts[n].shape
        two_d = (shape[0] * shape[1], shape[2])
        operands = (wts[n].reshape(two_d), mom[n].reshape(two_d), var[n].reshape(two_d))
        if n == "cv_pw_w":
            outs = _adamw(received_a[:, pw_first:, :].reshape((N_DEV,) + two_d), *operands, f"adamw_{n}")
        elif n in TRANSPOSED:
            layers = []
            for l in range(DEPTH):
                group, first, r = _where_is(n, l)
                layers.append(received[group][:, first:first + r, :])
            parts = jnp.stack(layers, axis=1).transpose(0, 1, 3, 2).reshape((N_DEV,) + two_d)
            outs = _adamw(parts, *operands, f"adamw_{n}")
        else:
            sources = []
            for l in range(DEPTH):
                group, first, _ = _where_is(n, l)
                sources.append((received[group], first))
            outs = _adamw_packed(sources, *operands, f"adamw_{n}")
        res[n] = [o.reshape(shape) for o in outs]
    small_names = SMALL_SHARDED + REPLICATED
    flat_w = _pad_rows(jnp.concatenate([wts[n].reshape(-1) for n in small_names]), PAYLOAD_COLS)
    flat_m = _pad_rows(jnp.concatenate([mom[n].reshape(-1) for n in small_names]), PAYLOAD_COLS)
    flat_v = _pad_rows(jnp.concatenate([var[n].reshape(-1) for n in small_names]), PAYLOAD_COLS)
    outs = _adamw(small_parts, flat_w, flat_m, flat_v, "adamw_small")
    outs = [o.reshape(-1) for o in outs]
    off = 0
    for n in small_names:
        size = wts[n].size
        res[n] = [o[off:off + size].reshape(wts[n].shape) for o in outs]
        off += size

    result = [loss, grad_x[None]]
    for kind in range(4):
        result += [res[n][kind] for n in WEIGHTS]
    return tuple(result)
```

```python
import functools
import math

import jax
import jax.numpy as jnp
from jax import lax
from jax.experimental import pallas as pl
from jax.experimental.pallas import tpu as pltpu

F32, BF16 = jnp.float32, jnp.bfloat16
SDS = jax.ShapeDtypeStruct

D_MODEL = 1024
SEQ = 4096
DEPTH = 2
HEAD_DIM = 64
SB_HEADS = 4
SB_WIDTH = 256
CV_WIDTH = 256
CV_KERNEL = 31
DL_HEADS = 8
DL_WIDTH = 512
IN_WIDTH = 2816
DL_PATTERN = ((128, 1), (512, 4), (2048, 16))
BLOCK = 128
ROPE_THETA = 10000.0
N_MEM = 256
X_HEADS = 4
X_HEAD_DIM = 256
D_FF = 2816
EPS = 1e-6
N_DEV = 8
LANES = 128

ADAM_LR = 0.001
ADAM_B1 = 0.9
ADAM_B2 = 0.999
ADAM_EPS = 1e-08
ADAM_WD = 0.01
ADAM_STEP = 10

VMEM_LIMIT_BYTES = 56 * 1024 * 1024
MESH = pl.DeviceIdType.MESH
NEG = -1e30


def _params(**kw):
    return pltpu.CompilerParams(vmem_limit_bytes=VMEM_LIMIT_BYTES, **kw)


def _pick(n, cands):
    for c in cands:
        if n % c == 0:
            return c
    return n


def _mm(a, b, mode, out_dtype, name, bias=None):
    if mode == "nn":
        (m, k), (k2, n) = a.shape, b.shape
    elif mode == "nt":
        (m, k), (n, k2) = a.shape, b.shape
    else:
        (k, m), (k2, n) = a.shape, b.shape
    assert k == k2, (a.shape, b.shape, mode)
    tm = _pick(m, (1024, 1408, 512, 256, 128))
    tn = _pick(n, (1024, 1408, 512, 256, 128))
    tk = k if k <= 2048 else _pick(k, (2048, 1408, 1024, 512))
    nk = k // tk
    dims = {"nn": ((1,), (0,)), "nt": ((1,), (1,)), "tn": ((0,), (0,))}[mode]

    def body(*refs):
        refs = list(refs)
        acc_ref = refs.pop() if nk > 1 else None
        a_ref, b_ref = refs[0], refs[1]
        bias_ref = refs[2] if bias is not None else None
        o_ref = refs[-1]
        p = lax.dot_general(a_ref[...].astype(BF16), b_ref[...].astype(BF16), (dims, ((), ())),
                            preferred_element_type=F32)

        def finish(v):
            if bias_ref is not None:
                v = v + bias_ref[...]
            o_ref[...] = v.astype(out_dtype)

        if nk == 1:
            finish(p)
        else:
            kk = pl.program_id(2)

            @pl.when(kk == 0)
            def _():
                acc_ref[...] = p

            @pl.when(kk > 0)
            def _():
                acc_ref[...] += p

            @pl.when(kk == nk - 1)
            def _():
                finish(acc_ref[...])

    a_spec = pl.BlockSpec((tk, tm), lambda i, j, kk: (kk, i)) if mode == "tn" else pl.BlockSpec((tm, tk), lambda i, j, kk: (i, kk))
    b_spec = pl.BlockSpec((tn, tk), lambda i, j, kk: (j, kk)) if mode == "nt" else pl.BlockSpec((tk, tn), lambda i, j, kk: (kk, j))
    in_specs = [a_spec, b_spec]
    args = [a, b]
    if bias is not None:
        in_specs.append(pl.BlockSpec((1, tn), lambda i, j, kk: (0, j)))
        args.append(bias)
    return pl.pallas_call(
        body, name=name, out_shape=SDS((m, n), out_dtype), grid=(m // tm, n // tn, nk),
        in_specs=in_specs, out_specs=pl.BlockSpec((tm, tn), lambda i, j, kk: (i, j)),
        scratch_shapes=[pltpu.VMEM((tm, tn), F32)] if nk > 1 else [], compiler_params=_params(),
    )(*args)


def _rms(x, g):
    r = lax.rsqrt(jnp.mean(x * x, axis=-1, keepdims=True) + EPS)
    return x * r * g


def _rms_bwd(x, g, dy):
    r = lax.rsqrt(jnp.mean(x * x, axis=-1, keepdims=True) + EPS)
    xh = x * r
    dyg = dy * g
    dx = r * (dyg - xh * jnp.mean(dyg * xh, axis=-1, keepdims=True))
    return dx, dy * xh


def _rms_fwd(x, g, name):
    rows, d = x.shape
    t = min(rows, 512)

    def body(x_ref, g_ref, o_ref):
        o_ref[...] = _rms(x_ref[...], g_ref[...]).astype(BF16)

    return pl.pallas_call(
        body, name=name, out_shape=SDS((rows, d), BF16), grid=(rows // t,),
        in_specs=[pl.BlockSpec((t, d), lambda i: (i, 0)), pl.BlockSpec((1, d), lambda i: (0, 0))],
        out_specs=pl.BlockSpec((t, d), lambda i: (i, 0)), compiler_params=_params(),
    )(x, g)


def _res_norm_fwd(h, y, g_post, g_next, name):
    rows, d = h.shape
    t = 512
    has_next = g_next is not None

    def body(*refs):
        if has_next:
            h_ref, y_ref, gp_ref, gn_ref, h1_ref, hn_ref = refs
        else:
            h_ref, y_ref, gp_ref, h1_ref = refs
        h1 = h_ref[...] + _rms(y_ref[...], gp_ref[...])
        h1_ref[...] = h1
        if has_next:
            hn_ref[...] = _rms(h1, gn_ref[...]).astype(BF16)

    row = pl.BlockSpec((t, d), lambda i: (i, 0))
    vec = pl.BlockSpec((1, d), lambda i: (0, 0))
    in_specs = [row, row, vec] + ([vec] if has_next else [])
    args = [h, y, g_post] + ([g_next] if has_next else [])
    out_shape = [SDS((rows, d), F32)] + ([SDS((rows, d), BF16)] if has_next else [])
    out_specs = [row] + ([row] if has_next else [])
    res = pl.pallas_call(body, name=name, out_shape=out_shape, grid=(rows // t,), in_specs=in_specs,
                         out_specs=out_specs, compiler_params=_params())(*args)
    return (res[0], res[1]) if has_next else (res[0], None)


def _norm_bwd(dh, pre, post, name):
    rows, d = dh.shape
    t = 512
    has_pre, has_post = pre is not None, post is not None
    if has_pre:
        dhns = pre[2] if isinstance(pre[2], tuple) else (pre[2],)
        pre = (pre[0], pre[1]) + dhns

    def body(*refs):
        refs = list(refs)
        dh_ref = refs.pop(0)
        if has_pre:
            h_ref, gpre_ref = refs.pop(0), refs.pop(0)
            dhn_refs = [refs.pop(0) for _ in dhns]
        if has_post:
            y_ref, gpost_ref = refs.pop(0), refs.pop(0)
        dht_ref = refs.pop(0)
        if has_post:
            dy_ref = refs.pop(0)
        if has_pre:
            dgpre_ref = refs.pop(0)
        if has_post:
            dgpost_ref = refs.pop(0)
        i = pl.program_id(0)
        dht = dh_ref[...]
        if has_pre:
            dhn = dhn_refs[0][...]
            for r in dhn_refs[1:]:
                dhn = dhn + r[...]
            dx, dgr = _rms_bwd(h_ref[...], gpre_ref[...], dhn)
            dht = dht + dx

            @pl.when(i == 0)
            def _():
                dgpre_ref[...] = jnp.zeros_like(dgpre_ref)

            dgpre_ref[...] += jnp.sum(dgr, axis=0, keepdims=True)
        dht_ref[...] = dht
        if has_post:
            dy, dgr = _rms_bwd(y_ref[...], gpost_ref[...], dht)
            dy_ref[...] = dy.astype(BF16)

            @pl.when(i == 0)
            def _():
                dgpost_ref[...] = jnp.zeros_like(dgpost_ref)

            dgpost_ref[...] += jnp.sum(dgr, axis=0, keepdims=True)

    row = pl.BlockSpec((t, d), lambda i: (i, 0))
    vec = pl.BlockSpec((1, d), lambda i: (0, 0))
    in_specs, args = [row], [dh]
    if has_pre:
        in_specs += [row, vec] + [row] * len(dhns)
        args += list(pre)
    if has_post:
        in_specs += [row, vec]
        args += list(post)
    out_shape, out_specs = [SDS((rows, d), F32)], [row]
    if has_post:
        out_shape.append(SDS((rows, d), BF16))
        out_specs.append(row)
    if has_pre:
        out_shape.append(SDS((1, d), F32))
        out_specs.append(vec)
    if has_post:
        out_shape.append(SDS((1, d), F32))
        out_specs.append(vec)
    res = list(pl.pallas_call(body, name=name, out_shape=out_shape, grid=(rows // t,), in_specs=in_specs,
                              out_specs=out_specs, compiler_params=_params())(*args))
    dht = res.pop(0)
    dy = res.pop(0) if has_post else None
    dgpre = res.pop(0) if has_pre else None
    dgpost = res.pop(0) if has_post else None
    return dht, dy, dgpre, dgpost


def _rms_gain_grad(x, g, dy, name):
    rows, d = x.shape

    def body(x_ref, g_ref, dy_ref, dg_ref):
        _, dgr = _rms_bwd(x_ref[...], g_ref[...], dy_ref[...])
        dg_ref[...] = jnp.sum(dgr, axis=0, keepdims=True)

    return pl.pallas_call(body, name=name, out_shape=SDS((1, d), F32), compiler_params=_params())(x, g, dy)


def _loss_fwd(h, target, name):
    rows, d = h.shape
    t = 512

    def body(h_ref, t_ref, loss_ref, dh_ref):
        i = pl.program_id(0)
        err = h_ref[...] - t_ref[...]
        dh_ref[...] = err * (1.0 / d)

        @pl.when(i == 0)
        def _():
            loss_ref[...] = jnp.zeros_like(loss_ref)

        part = jnp.sum(jnp.sum(err * err, axis=1, keepdims=True), axis=0, keepdims=True) * (0.5 / d)
        loss_ref[...] += jnp.broadcast_to(part, loss_ref.shape)

    row = pl.BlockSpec((t, d), lambda i: (i, 0))
    return pl.pallas_call(
        body, name=name, out_shape=(SDS((1, LANES), F32), SDS((rows, d), F32)), grid=(rows // t,),
        in_specs=[row, row], out_specs=(pl.BlockSpec((1, LANES), lambda i: (0, 0)), row), compiler_params=_params(),
    )(h, target)


def _rot_half(x, sign):
    w = x.shape[-1]
    lane = lax.broadcasted_iota(jnp.int32, x.shape, 1)
    first = (lane % HEAD_DIM) < (HEAD_DIM // 2)
    return jnp.where(first, -sign * pltpu.roll(x, w - HEAD_DIM // 2, axis=1), sign * pltpu.roll(x, HEAD_DIM // 2, axis=1))


def _rope_fwd(u, cos, sin, name):
    rows = u.shape[0]
    t, cw = 512, 256
    first_col = (3 * SB_WIDTH + 2 * CV_WIDTH) // cw

    def body(u_ref, c_ref, s_ref, o_ref):
        x = u_ref[...]
        c = jnp.tile(c_ref[...], (1, cw // LANES))
        s = jnp.tile(s_ref[...], (1, cw // LANES))
        o_ref[...] = x * c + _rot_half(x, 1.0) * s

    tab = pl.BlockSpec((t, LANES), lambda i, j: (i, 0))
    return pl.pallas_call(
        body, name=name, out_shape=SDS((rows, 2 * DL_WIDTH), F32), grid=(rows // t, 2 * DL_WIDTH // cw),
        in_specs=[pl.BlockSpec((t, cw), lambda i, j: (i, first_col + j)), tab, tab],
        out_specs=pl.BlockSpec((t, cw), lambda i, j: (i, j)), compiler_params=_params(),
    )(u, cos, sin)


def _rope_bwd(dqs, dks, dvs, cos, sin, name):
    rows = dqs[0].shape[0]
    t, w = 256, DL_WIDTH

    def body(*refs):
        c = jnp.tile(refs[9][...], (1, w // LANES))
        s = jnp.tile(refs[10][...], (1, w // LANES))
        o_ref = refs[11]
        dq = refs[0][...] + refs[1][...] + refs[2][...]
        dk = refs[3][...] + refs[4][...] + refs[5][...]
        dv = refs[6][...] + refs[7][...] + refs[8][...]
        o_ref[:, 0:w] = (dq * c + _rot_half(dq, -1.0) * s).astype(BF16)
        o_ref[:, w:2 * w] = (dk * c + _rot_half(dk, -1.0) * s).astype(BF16)
        o_ref[:, 2 * w:3 * w] = dv.astype(BF16)

    row = pl.BlockSpec((t, w), lambda i: (i, 0))
    tab = pl.BlockSpec((t, LANES), lambda i: (i, 0))
    return pl.pallas_call(
        body, name=name, out_shape=SDS((rows, 3 * w), BF16), grid=(rows // t,), in_specs=[row] * 9 + [tab, tab],
        out_specs=pl.BlockSpec((t, 3 * w), lambda i: (i, 0)), compiler_params=_params(),
    )(*dqs, *dks, *dvs, cos, sin)


SB_TILE = 256
SB_ZERO_AFTER = 110.0
SB_FIRST_BLOCK = (8, LANES)


def _softplus(z):
    return jnp.maximum(z, 0.0) + jnp.log(1.0 + jnp.exp(-jnp.abs(z)))


def _split_dot(x, tri, passes):
    acc = None
    rem = x
    for _ in range(passes):
        part = rem.astype(BF16)
        rem = rem - part.astype(F32)
        d = jnp.dot(part, tri, preferred_element_type=F32)
        acc = d if acc is None else acc + d
    return acc


def _tri(t, rel):
    j = lax.broadcasted_iota(jnp.int32, (t, t), 0)
    s = lax.broadcasted_iota(jnp.int32, (t, t), 1)
    return rel(j, s).astype(BF16)


def _sb_fwd(q, k, v, payload, name):
    h, s_len, hd = q.shape
    t = SB_TILE
    nq = s_len // t
    scale = hd ** -0.5

    def body(q_ref, k_ref, v_ref, pay_ref, o_ref, tot_ref, first_ref, gathered_ref, send_sems, recv_sems, local_sem):
        hh, i = pl.program_id(0), pl.program_id(1)
        start, forward, finish = _gather_steps(pay_ref, gathered_ref, send_sems, recv_sems, local_sem)
        pl.when((hh == 0) & (i == 0))(start)
        pl.when((hh == h - 1) & (i == nq - 1))(forward)
        qv = q_ref[0] * scale
        upper = _tri(t, lambda j, s: j > s)

        def tiles(js, carry, diagonal):
            acc, run = carry
            starts = [pl.multiple_of(j * t, t) for j in js]
            zs = [lax.dot_general(qv, k_ref[0, pl.ds(st, t), :], (((1,), (1,)), ((), ())), preferred_element_type=F32)
                  for st in starts]
            sps = [_softplus(z) for z in zs]
            if diagonal:
                mask = lax.broadcasted_iota(jnp.int32, (t, t), 1) < lax.broadcasted_iota(jnp.int32, (t, t), 0)
                sps = [jnp.where(mask, sp, 0.0) for sp in sps]
            laters = [_split_dot(sp, upper, 2) for sp in sps]
            for st, z, sp, later in zip(starts, zs, sps, laters):
                a = jnp.exp((z - sp) - (run + later))
                if diagonal:
                    a = jnp.where(mask, a, 0.0)
                acc = acc + jnp.dot(a.astype(BF16), v_ref[0, pl.ds(st, t), :], preferred_element_type=F32)
                run = run + jnp.sum(sp, axis=1, keepdims=True)
            return acc, run

        def live(carry):
            return jnp.min(carry[1]) < SB_ZERO_AFTER

        def pair(state):
            pp, carry = state
            j = i - 1 - 2 * pp
            return pp + 1, tiles([j, j - 1], carry, False)

        carry = tiles([i], (jnp.zeros((t, hd), F32), jnp.zeros((t, 1), F32)), True)
        pairs, carry = lax.while_loop(lambda st: (st[0] < i // 2) & live(st[1]), pair, (0, carry))
        last = ((i % 2 == 1) & (pairs == i // 2) & live(carry)).astype(jnp.int32)
        acc, run = lax.fori_loop(0, last, lambda _, c: tiles([0], c, False), carry)
        o_ref[0] = acc.astype(BF16)
        tot_ref[0] = run
        first_ref[...] = jnp.full(first_ref.shape, i - 2 * pairs - last, jnp.int32).astype(F32)
        pl.when((hh == h - 1) & (i == nq - 1))(finish)

    full = pl.BlockSpec((1, s_len, hd), lambda hh, i: (hh, 0, 0))
    tile = pl.BlockSpec((1, t, hd), lambda hh, i: (hh, i, 0))
    hbm = pl.BlockSpec(memory_space=pl.ANY)
    return pl.pallas_call(
        body, name=name,
        out_shape=(SDS((h, s_len, hd), BF16), SDS((h, s_len, 1), F32), SDS((h, nq) + SB_FIRST_BLOCK, F32),
                   SDS((N_DEV,) + payload.shape, payload.dtype)),
        grid=(h, nq), in_specs=[tile, full, full, hbm],
        out_specs=(tile, pl.BlockSpec((1, t, 1), lambda hh, i: (hh, i, 0)),
                   pl.BlockSpec((1, 1) + SB_FIRST_BLOCK, lambda hh, i: (hh, i, 0, 0)), hbm),
        scratch_shapes=_COMM_SEMAPHORES, compiler_params=_params(has_side_effects=True),
    )(q, k, v, payload)


def _sb_bwd(q, k, v, do, tot, first, payload, name):
    h, s_len, hd = q.shape
    t = SB_TILE
    nq = s_len // t
    scale = hd ** -0.5

    def body(q_ref, k_ref, v_ref, do_ref, tot_ref, first_ref, pay_ref, dq_ref, dk_ref, dv_ref, received_ref, dk_acc, dv_acc,
             send_sems, recv_sems, local_sem):
        hh, i = pl.program_id(0), pl.program_id(1)
        start, finish = _exchange_steps(pay_ref, received_ref, send_sems, recv_sems, local_sem)
        pl.when((hh == 0) & (i == 0))(start)

        @pl.when(i == 0)
        def _():
            dk_acc[...] = jnp.zeros_like(dk_acc)
            dv_acc[...] = jnp.zeros_like(dv_acc)

        qv = q_ref[0] * scale
        dov = do_ref[0]
        total = tot_ref[0]
        upto = _tri(t, lambda j, s: j <= s)
        before = _tri(t, lambda j, s: j < s)
        nt_dims = (((1,), (1,)), ((), ()))
        tn_dims = (((0,), (0,)), ((), ()))

        def tiles(js, carry, diagonal):
            dq, run_sp, run_g = carry
            starts = [pl.multiple_of(j * t, t) for j in js]
            zs = [lax.dot_general(qv, k_ref[0, pl.ds(st, t), :], nt_dims, preferred_element_type=F32) for st in starts]
            das = [lax.dot_general(dov, v_ref[0, pl.ds(st, t), :], nt_dims, preferred_element_type=F32) for st in starts]
            sps = [_softplus(z) for z in zs]
            log_sigs = [z - sp for z, sp in zip(zs, sps)]
            if diagonal:
                mask = lax.broadcasted_iota(jnp.int32, (t, t), 1) < lax.broadcasted_iota(jnp.int32, (t, t), 0)
                sps = [jnp.where(mask, sp, 0.0) for sp in sps]
            pres = [_split_dot(sp, upto, 2) for sp in sps]
            a_s, gs = [], []
            for sp, log_sig, pre, da in zip(sps, log_sigs, pres, das):
                a = jnp.exp(log_sig - (total - (run_sp + pre)))
                if diagonal:
                    a = jnp.where(mask, a, 0.0)
                a_s.append(a)
                gs.append(a * da)
                run_sp = run_sp + jnp.sum(sp, axis=1, keepdims=True)
            g_pres = [_split_dot(g, before, 3) for g in gs]
            for st, a, g, g_pre, log_sig in zip(starts, a_s, gs, g_pres, log_sigs):
                sig = jnp.exp(log_sig)
                dz = g * (1.0 - sig) - sig * (run_g + g_pre)
                if diagonal:
                    dz = jnp.where(mask, dz, 0.0)
                dz = dz.astype(BF16)
                dq = dq + jnp.dot(dz, k_ref[0, pl.ds(st, t), :], preferred_element_type=F32)
                dk_acc[pl.ds(st, t), :] += lax.dot_general(dz, qv, tn_dims, preferred_element_type=F32)
                dv_acc[pl.ds(st, t), :] += lax.dot_general(a.astype(BF16), dov, tn_dims, preferred_element_type=F32)
                run_g = run_g + jnp.sum(g, axis=1, keepdims=True)
            return dq, run_sp, run_g

        zero = jnp.zeros((t, 1), F32)
        first = jnp.clip(first_ref[0, 0, 0, 0].astype(jnp.int32), 0, i)
        count = i - first
        carry = lax.fori_loop(0, count // 2, lambda pp, c: tiles([first + 2 * pp, first + 2 * pp + 1], c, False),
                              (jnp.zeros((t, hd), F32), zero, zero))
        carry = lax.fori_loop(0, count % 2, lambda _, c: tiles([i - 1], c, False), carry)
        dq, _, _ = tiles([i], carry, True)
        dq_ref[0] = (dq * scale).astype(BF16)

        @pl.when(i == nq - 1)
        def _():
            dk_ref[0] = dk_acc[...].astype(BF16)
            dv_ref[0] = dv_acc[...].astype(BF16)

        pl.when((hh == h - 1) & (i == nq - 1))(finish)

    full = pl.BlockSpec((1, s_len, hd), lambda hh, i: (hh, 0, 0))
    tile = pl.BlockSpec((1, t, hd), lambda hh, i: (hh, i, 0))
    hbm = pl.BlockSpec(memory_space=pl.ANY)
    out = SDS((h, s_len, hd), BF16)
    return pl.pallas_call(
        body, name=name, out_shape=(out, out, out, SDS(payload.shape, payload.dtype)), grid=(h, nq),
        in_specs=[tile, full, full, tile, pl.BlockSpec((1, t, 1), lambda hh, i: (hh, i, 0)),
                  pl.BlockSpec((1, 1) + SB_FIRST_BLOCK, lambda hh, i: (hh, i, 0, 0)), hbm],
        out_specs=(tile, full, full, hbm),
        scratch_shapes=[pltpu.VMEM((s_len, hd), F32), pltpu.VMEM((s_len, hd), F32)] + _COMM_SEMAPHORES,
        compiler_params=_params(has_side_effects=True),
    )(q, k, v, do, tot, first, payload)


def _dl_scores(qv, kk, n):
    s = lax.dot_general(qv, kk, (((1,), (1,)), ((), ())), preferred_element_type=F32) * (HEAD_DIM ** -0.5)
    r = lax.broadcasted_iota(jnp.int32, s.shape, 0)
    c = lax.broadcasted_iota(jnp.int32, s.shape, 1)
    valid = (c >= r) & (c - r <= BLOCK) & ((n > 0) | (c >= BLOCK))
    return jnp.where(valid, s, NEG)


DL_UNROLL = 4
DL_PAIR = 2 * HEAD_DIM
DL_Q_BLOCK0 = 0
DL_K_BLOCK0 = DL_WIDTH // DL_PAIR
DL_V_BLOCK0 = (IN_WIDTH - DL_WIDTH) // DL_PAIR
DL_DO_BLOCK0 = (SB_WIDTH + CV_WIDTH) // DL_PAIR


def _dl_rows(idx, nb, dil):
    r, n = idx // nb, idx % nb
    cur = pl.ds(r + n * (BLOCK * dil), BLOCK, stride=dil)
    prev = pl.ds(r + jnp.maximum(n - 1, 0) * (BLOCK * dil), BLOCK, stride=dil)
    return n, cur, prev


def _dl_window(ref, cur, prev):
    return jnp.concatenate([ref[prev, :], ref[cur, :]], axis=0).astype(BF16)


def _head_lanes():
    first = lax.broadcasted_iota(jnp.int32, (BLOCK, DL_PAIR), 1) < HEAD_DIM
    return first, jnp.logical_not(first)


def _dl_fwd(qk, u, dil, name, payload=None):
    s_len = qk.shape[0]
    nb = s_len // dil // BLOCK

    def body(q_ref, k_ref, v_ref, o_ref, lse_ref):
        heads = _head_lanes()

        def step(idx, _):
            n, cur, prev = _dl_rows(idx, nb, dil)
            q = q_ref[cur, :]
            kk = _dl_window(k_ref, cur, prev)
            vv = _dl_window(v_ref, cur, prev)
            o, lse = None, None
            for lanes in heads:
                s = _dl_scores(jnp.where(lanes, q, 0.0).astype(BF16), kk, n)
                m = jnp.max(s, axis=-1, keepdims=True)
                p = jnp.exp(s - m)
                den = jnp.sum(p, axis=-1, keepdims=True)
                o_h = jnp.dot((p / den).astype(BF16), vv, preferred_element_type=F32)
                lse_h = jnp.broadcast_to(m + jnp.log(den), (BLOCK, DL_PAIR))
                o = o_h if o is None else jnp.where(heads[0], o, o_h)
                lse = lse_h if lse is None else jnp.where(heads[0], lse, lse_h)
            o_ref[cur, :] = o
            lse_ref[cur, :] = lse
            return 0

        lax.fori_loop(0, s_len // BLOCK, step, 0, unroll=DL_UNROLL)

    col = lambda first: pl.BlockSpec((s_len, DL_PAIR), lambda i: (0, first + i))
    out = SDS((s_len, DL_WIDTH), F32)
    steps = DL_WIDTH // DL_PAIR
    if payload is not None:
        step = lambda: pl.program_id(0)
        payload = ("gather", payload, lambda: (step() == 0, step() == steps - 1, step() == steps - 1))
    return _hosted_call(body, payload, name=name, out_shape=(out, out), grid=(steps,),
                        in_specs=[col(DL_Q_BLOCK0), col(DL_K_BLOCK0), col(DL_V_BLOCK0)], out_specs=(col(0), col(0)),
                        args=(qk, qk, u))


def _dl_bwd(qk, u, dmix, o_mix, wt, lse, dil, name, payload=None):
    s_len = qk.shape[0]
    nb = s_len // dil // BLOCK
    scale = HEAD_DIM ** -0.5
    nt_dims = (((1,), (1,)), ((), ()))
    tn_dims = (((0,), (0,)), ((), ()))

    def body(q_ref, k_ref, v_ref, do_ref, om_ref, wt_ref, lse_ref, dq_ref, dk_ref, dv_ref):
        dk_ref[...] = jnp.zeros_like(dk_ref)
        dv_ref[...] = jnp.zeros_like(dv_ref)
        heads = _head_lanes()

        def step(idx, _):
            n, cur, prev = _dl_rows(idx, nb, dil)
            q = q_ref[cur, :]
            kk = _dl_window(k_ref, cur, prev)
            vv = _dl_window(v_ref, cur, prev)
            dov = do_ref[cur, :]
            d_lanes = dov * om_ref[cur, :]
            w_lanes = wt_ref[cur, :]
            lse_lanes = lse_ref[cur, :]
            dq, dkk, dvv = None, None, None
            for lanes in heads:
                qm = jnp.where(lanes, q, 0.0).astype(BF16)
                s = _dl_scores(qm, kk, n)
                p = jnp.exp(s - jnp.max(jnp.where(lanes, lse_lanes, NEG), axis=-1, keepdims=True))
                w = jnp.max(jnp.where(lanes, w_lanes, 0.0), axis=-1, keepdims=True)
                d_all = jnp.sum(jnp.where(lanes, d_lanes, 0.0), axis=-1, keepdims=True)
                do_n = jnp.where(lanes, dov * w, 0.0).astype(BF16)
                dp = lax.dot_general(do_n, vv, nt_dims, preferred_element_type=F32)
                ds = (p * (dp - w * d_all) * scale).astype(BF16)
                dq_h = jnp.dot(ds, kk, preferred_element_type=F32)
                dkk_h = lax.dot_general(ds, qm, tn_dims, preferred_element_type=F32)
                dvv_h = lax.dot_general(p.astype(BF16), do_n, tn_dims, preferred_element_type=F32)
                dq = dq_h if dq is None else jnp.where(heads[0], dq, dq_h)
                dkk = dkk_h if dkk is None else dkk + dkk_h
                dvv = dvv_h if dvv is None else dvv + dvv_h
            dq_ref[cur, :] = dq
            dk_ref[prev, :] += dkk[:BLOCK]
            dv_ref[prev, :] += dvv[:BLOCK]
            dk_ref[cur, :] += dkk[BLOCK:]
            dv_ref[cur, :] += dvv[BLOCK:]
            return 0

        lax.fori_loop(0, s_len // BLOCK, step, 0, unroll=DL_UNROLL)

    col = lambda first: pl.BlockSpec((s_len, DL_PAIR), lambda i: (0, first + i))
    out = SDS((s_len, DL_WIDTH), F32)
    steps = DL_WIDTH // DL_PAIR
    if payload is not None:
        step = lambda: pl.program_id(0)
        payload = ("all_to_all", payload, lambda: (step() == 0, None, step() == steps - 1))
    return _hosted_call(
        body, payload, name=name, out_shape=(out, out, out), grid=(steps,),
        in_specs=[col(DL_Q_BLOCK0), col(DL_K_BLOCK0), col(DL_V_BLOCK0), col(DL_DO_BLOCK0), col(0), col(0), col(0)],
        out_specs=(col(0), col(0), col(0)), args=(qk, qk, u, dmix, o_mix, wt, lse))


def _dl_mix_fwd(outs, lses, name):
    rows, w = outs[0].shape
    t = 256

    def body(o1, o2, o3, l1, l2, l3, ob_ref, of_ref, w1, w2, w3):
        a, b, c = l1[...], l2[...], l3[...]
        m = jnp.maximum(jnp.maximum(a, b), c)
        ea, eb, ec = jnp.exp(a - m), jnp.exp(b - m), jnp.exp(c - m)
        den = ea + eb + ec
        wa, wb, wc = ea / den, eb / den, ec / den
        o = wa * o1[...] + wb * o2[...] + wc * o3[...]
        ob_ref[...] = o.astype(BF16)
        of_ref[...] = o
        w1[...] = wa
        w2[...] = wb
        w3[...] = wc

    row = pl.BlockSpec((t, w), lambda i: (i, 0))
    f = SDS((rows, w), F32)
    return pl.pallas_call(body, name=name, out_shape=(SDS((rows, w), BF16), f, f, f, f), grid=(rows // t,),
                          in_specs=[row] * 6, out_specs=(row,) * 5, compiler_params=_params())(*outs, *lses)


def _x_probs(qh, kh):
    s = lax.dot_general(qh, kh, (((1,), (1,)), ((), ())), preferred_element_type=F32) * (X_HEAD_DIM ** -0.5)
    e = jnp.exp(s - jnp.max(s, axis=-1, keepdims=True))
    return e / jnp.sum(e, axis=-1, keepdims=True)


def _xattn_fwd(q, k, v, name):
    rows, d = q.shape
    t = 512

    def body(q_ref, k_ref, v_ref, o_ref):
        for hh in range(X_HEADS):
            cols = slice(hh * X_HEAD_DIM, (hh + 1) * X_HEAD_DIM)
            p = _x_probs(q_ref[:, cols], k_ref[:, cols])
            o_ref[:, cols] = jnp.dot(p.astype(BF16), v_ref[:, cols], preferred_element_type=F32).astype(BF16)

    row = pl.BlockSpec((t, d), lambda i: (i, 0))
    mem = pl.BlockSpec((N_MEM, d), lambda i: (0, 0))
    return pl.pallas_call(body, name=name, out_shape=SDS((rows, d), BF16), grid=(rows // t,), in_specs=[row, mem, mem],
                          out_specs=row, compiler_params=_params())(q, k, v)


def _xattn_bwd(q, k, v, do, name):
    rows, d = q.shape
    t = 512
    scale = X_HEAD_DIM ** -0.5

    def body(q_ref, k_ref, v_ref, do_ref, dq_ref, dk_ref, dv_ref):
        @pl.when(pl.program_id(0) == 0)
        def _():
            dk_ref[...] = jnp.zeros_like(dk_ref)
            dv_ref[...] = jnp.zeros_like(dv_ref)

        for hh in range(X_HEADS):
            cols = slice(hh * X_HEAD_DIM, (hh + 1) * X_HEAD_DIM)
            qh, kh, vh, doh = q_ref[:, cols], k_ref[:, cols], v_ref[:, cols], do_ref[:, cols]
            p = _x_probs(qh, kh)
            dp = lax.dot_general(doh, vh, (((1,), (1,)), ((), ())), preferred_element_type=F32)
            ds = (p * (dp - jnp.sum(p * dp, axis=-1, keepdims=True)) * scale).astype(BF16)
            dq_ref[:, cols] = jnp.dot(ds, kh, preferred_element_type=F32).astype(BF16)
            dk_ref[:, cols] += lax.dot_general(ds, qh, (((0,), (0,)), ((), ())), preferred_element_type=F32)
            dv_ref[:, cols] += lax.dot_general(p.astype(BF16), doh, (((0,), (0,)), ((), ())), preferred_element_type=F32)

    row = pl.BlockSpec((t, d), lambda i: (i, 0))
    mem = pl.BlockSpec((N_MEM, d), lambda i: (0, 0))
    return pl.pallas_call(
        body, name=name, out_shape=(SDS((rows, d), BF16), SDS((N_MEM, d), F32), SDS((N_MEM, d), F32)), grid=(rows // t,),
        in_specs=[row, mem, mem, row], out_specs=(row, mem, mem), compiler_params=_params(),
    )(q, k, v, do)


CV_TILE = 256
CV_HALO = 32
CV_LEAD = CV_HALO - (CV_KERNEL - 1)


def _shifted(win, off, rows):
    n = win.shape[0]
    return pltpu.roll(win, (n - off) % n, axis=0)[:rows]


def _glu(val, gate):
    return val * jax.nn.sigmoid(gate)


def _ln_parts(c):
    mu = jnp.mean(c, axis=-1, keepdims=True)
    xc = c - mu
    rstd = lax.rsqrt(jnp.mean(xc * xc, axis=-1, keepdims=True) + EPS)
    return xc * rstd, rstd


def _cv_fwd(u, cv_w, cv_b, ln_g, ln_b, name):
    rows = u.shape[0]
    t, w = CV_TILE, CV_WIDTH
    val_col = 3 * SB_WIDTH // w
    ratio = t // CV_HALO

    def body(val_ref, gate_ref, pval_ref, pgate_ref, w_ref, b_ref, g_ref, beta_ref, s_ref, c_ref):
        i = pl.program_id(0)
        hist = jnp.where(i > 0, _glu(pval_ref[...], pgate_ref[...]), 0.0)
        win = jnp.concatenate([hist, _glu(val_ref[...], gate_ref[...])], axis=0)
        acc = jnp.broadcast_to(b_ref[...], (t, w))
        for kk in range(CV_KERNEL):
            acc = acc + _shifted(win, CV_LEAD + kk, t) * w_ref[kk:kk + 1, :]
        c_ref[...] = acc
        n, _ = _ln_parts(acc)
        y = n * g_ref[...] + beta_ref[...]
        s_ref[...] = (y * jax.nn.sigmoid(y)).astype(BF16)

    cur = lambda col: pl.BlockSpec((t, w), lambda i: (i, col))
    prev = lambda col: pl.BlockSpec((CV_HALO, w), lambda i: (jnp.maximum(i * ratio - 1, 0), col))
    vec = pl.BlockSpec((1, w), lambda i: (0, 0))
    return pl.pallas_call(
        body, name=name, out_shape=(SDS((rows, w), BF16), SDS((rows, w), F32)), grid=(rows // t,),
        in_specs=[cur(val_col), cur(val_col + 1), prev(val_col), prev(val_col + 1),
                  pl.BlockSpec((CV_KERNEL, w), lambda i: (0, 0)), vec, vec, vec],
        out_specs=(pl.BlockSpec((t, w), lambda i: (i, 0)),) * 2, compiler_params=_params(),
    )(u, u, u, u, cv_w, cv_b, ln_g, ln_b)


def _cv_bwd(u, c, ds, db_out, cv_w, ln_g, ln_b, name):
    rows = u.shape[0]
    t, w = CV_TILE, CV_WIDTH
    val_col = 3 * SB_WIDTH // w
    ratio = t // CV_HALO
    nt = rows // t

    def conv_out_grad(c_v, ds_v, g_v, beta_v):
        n, rstd = _ln_parts(c_v)
        y = n * g_v + beta_v
        sig = jax.nn.sigmoid(y)
        dy = ds_v * (sig * (1.0 + y * (1.0 - sig)))
        dn = dy * g_v
        dc = rstd * (dn - jnp.mean(dn, axis=-1, keepdims=True) - n * jnp.mean(dn * n, axis=-1, keepdims=True))
        return dc, dy, n

    def body(val_ref, gate_ref, pval_ref, pgate_ref, c_ref, nc_ref, ds_ref, nds_ref, dbo_ref, w_ref, g_ref, beta_ref,
             dvg_ref, dw_ref, db_ref, dg_ref, dbeta_ref, dpwb_ref):
        i = pl.program_id(0)

        @pl.when(i == 0)
        def _():
            for r in (dw_ref, db_ref, dg_ref, dbeta_ref, dpwb_ref):
                r[...] = jnp.zeros_like(r)

        g_v, beta_v = g_ref[...], beta_ref[...]
        dc, dy, n = conv_out_grad(c_ref[...], ds_ref[...], g_v, beta_v)
        dc_next, _, _ = conv_out_grad(nc_ref[...], nds_ref[...], g_v, beta_v)
        dc_next = jnp.where(i < nt - 1, dc_next, 0.0)
        dg_ref[...] += jnp.sum(dy * n, axis=0, keepdims=True)
        dbeta_ref[...] += jnp.sum(dy, axis=0, keepdims=True)
        db_ref[...] += jnp.sum(dc, axis=0, keepdims=True)
        dpwb_ref[...] += jnp.sum(dbo_ref[...], axis=0, keepdims=True)

        val, gate = val_ref[...], gate_ref[...]
        hist = jnp.where(i > 0, _glu(pval_ref[...], pgate_ref[...]), 0.0)
        win = jnp.concatenate([hist, _glu(val, gate)], axis=0)
        dc_ext = jnp.concatenate([dc, dc_next], axis=0)
        dglu = jnp.zeros((t, w), F32)
        for kk in range(CV_KERNEL):
            dw_ref[kk:kk + 1, :] += jnp.sum(dc * _shifted(win, CV_LEAD + kk, t), axis=0, keepdims=True)
            dglu = dglu + _shifted(dc_ext, CV_KERNEL - 1 - kk, t) * w_ref[kk:kk + 1, :]
        sig = jax.nn.sigmoid(gate)
        dvg_ref[:, 0:w] = (dglu * sig).astype(BF16)
        dvg_ref[:, w:2 * w] = (dglu * val * sig * (1.0 - sig)).astype(BF16)

    cur = lambda col: pl.BlockSpec((t, w), lambda i: (i, col))
    prev = lambda col: pl.BlockSpec((CV_HALO, w), lambda i: (jnp.maximum(i * ratio - 1, 0), col))
    nxt = pl.BlockSpec((CV_HALO, w), lambda i: (jnp.minimum((i + 1) * ratio, rows // CV_HALO - 1), 0))
    vec = pl.BlockSpec((1, w), lambda i: (0, 0))
    return pl.pallas_call(
        body, name=name,
        out_shape=(SDS((rows, 2 * w), BF16), SDS((CV_HALO, w), F32), SDS((1, w), F32), SDS((1, w), F32), SDS((1, w), F32),
                   SDS((1, w), F32)),
        grid=(nt,),
        in_specs=[cur(val_col), cur(val_col + 1), prev(val_col), prev(val_col + 1), cur(0), nxt, cur(0), nxt, cur(0),
                  pl.BlockSpec((CV_KERNEL, w), lambda i: (0, 0)), vec, vec],
        out_specs=(pl.BlockSpec((t, 2 * w), lambda i: (i, 0)), pl.BlockSpec((CV_HALO, w), lambda i: (0, 0)), vec, vec, vec, vec),
        compiler_params=_params(),
    )(u, u, u, u, c, c, ds, ds, db_out, cv_w, ln_g, ln_b)


FFN_TILE = 512
FFN_CHUNK = 64
FFN_COLS = 256
FFN_HALO = 8
FFN_KERNEL = 3
N_FF_BLOCKS = D_FF // FFN_COLS


def _conv3(prev8, cur, w_ref, b_ref, first):
    t = cur.shape[0]
    win = jnp.concatenate([jnp.where(first, 0.0, prev8), cur], axis=0)
    return (b_ref[...] + _shifted(win, FFN_HALO - 2, t) * w_ref[0:1, :] + _shifted(win, FFN_HALO - 1, t) * w_ref[1:2, :]
            + cur * w_ref[2:3, :])


def _gelu_gate(gate, val):
    return jax.nn.gelu(gate, approximate=True) * val


def _ffn_specs(t):
    ratio = t // FFN_HALO
    cur = pl.BlockSpec((t, FFN_COLS), lambda j, i: (i, j))
    prev = pl.BlockSpec((FFN_HALO, FFN_COLS), lambda j, i: (jnp.maximum(i * ratio - 1, 0), j))
    wsp = pl.BlockSpec((FFN_KERNEL, FFN_COLS), lambda j, i: (0, j))
    bsp = pl.BlockSpec((1, FFN_COLS), lambda j, i: (0, j))
    return cur, prev, wsp, bsp


def _ffn_host_steps(row_tiles):
    def when():
        j, i = pl.program_id(0), pl.program_id(1)
        return (j == 0) & (i == 0), (j == (3 * N_FF_BLOCKS) // 4) & (i == 0), (j == N_FF_BLOCKS - 1) & (i == row_tiles - 1)
    return when


def _ffn_act_fwd(up_g, up_v, w_g, w_v, b_g, b_v, name, payload=None):
    rows = up_g.shape[0]
    t = FFN_TILE
    cur, prev, wsp, bsp = _ffn_specs(t)

    def body(g_ref, v_ref, pg_ref, pv_ref, wg_ref, wv_ref, bg_ref, bv_ref, o_ref):
        first = pl.program_id(1) == 0
        gate = _conv3(pg_ref[...], g_ref[...], wg_ref, bg_ref, first)
        val = _conv3(pv_ref[...], v_ref[...], wv_ref, bv_ref, first)
        o_ref[...] = _gelu_gate(gate, val).astype(BF16)

    if payload is not None:
        payload = ("gather", payload, _ffn_host_steps(rows // t))
    return _hosted_call(
        body, payload, name=name, out_shape=(SDS((rows, D_FF), BF16),), grid=(N_FF_BLOCKS, rows // t),
        in_specs=[cur, cur, prev, prev, wsp, wsp, bsp, bsp], out_specs=(cur,), args=(up_g, up_v, up_g, up_v, w_g, w_v, b_g, b_v))


def _ffn_act_bwd(up_g, up_v, dact, w_g, w_v, b_g, b_v, name, payload=None):
    rows = up_g.shape[0]
    t = FFN_TILE
    ch = FFN_CHUNK
    che = ch + FFN_HALO
    ratio = t // FFN_HALO
    nt = rows // t
    cur, prev, wsp, bsp = _ffn_specs(t)
    nxt = pl.BlockSpec((FFN_HALO, FFN_COLS), lambda j, i: (jnp.minimum((i + 1) * ratio, rows // FFN_HALO - 1), j))

    def body(g_ref, v_ref, pg_ref, pv_ref, ng_ref, nv_ref, da_ref, nda_ref, wg_ref, wv_ref, bg_ref, bv_ref,
             dug_ref, duv_ref, dwg_ref, dwv_ref, dbg_ref, dbv_ref, win_g, win_v, da_win):
        i = pl.program_id(1)
        first = i == 0

        @pl.when(first)
        def _():
            for r in (dwg_ref, dwv_ref, dbg_ref, dbv_ref):
                r[...] = jnp.zeros_like(r)

        for win, pre, x, nx in ((win_g, pg_ref, g_ref, ng_ref), (win_v, pv_ref, v_ref, nv_ref)):
            win[0:FFN_HALO, :] = jnp.where(first, 0.0, pre[...])
            win[FFN_HALO:FFN_HALO + t, :] = x[...]
            win[FFN_HALO + t:, :] = nx[...]
        da_win[0:t, :] = da_ref[...]
        da_win[t:, :] = jnp.where(i < nt - 1, nda_ref[...], 0.0)
        halves = ((win_g, wg_ref, bg_ref, dug_ref), (win_v, wv_ref, bv_ref, duv_ref))

        def chunk(c, sums):
            base = pl.multiple_of(c * ch, ch)
            taps, convs = [], []
            for win, w_ref, b_ref, _ in halves:
                w = win[pl.ds(base, ch + 2 * FFN_HALO), :]
                shifted = [_shifted(w, FFN_HALO - 2 + kk, che) for kk in range(FFN_KERNEL)]
                taps.append(shifted)
                convs.append(b_ref[...] + sum(s * w_ref[kk:kk + 1, :] for kk, s in enumerate(shifted)))
            _, vjp = jax.vjp(_gelu_gate, *convs)
            new = []
            for dc_ext, shifted, (_, w_ref, _, du_ref), (dw, db) in zip(vjp(da_win[pl.ds(base, che), :]), taps, halves, sums):
                dc = dc_ext[:ch]
                du_ref[pl.ds(base, ch), :] = (dc * w_ref[2:3, :] + _shifted(dc_ext, 1, ch) * w_ref[1:2, :]
                                              + _shifted(dc_ext, 2, ch) * w_ref[0:1, :]).astype(BF16)
                dw = [dw[kk] + jnp.sum(dc * shifted[kk][:ch], axis=0, keepdims=True) for kk in range(FFN_KERNEL)]
                new.append((dw, db + jnp.sum(dc, axis=0, keepdims=True)))
            return new

        zero = jnp.zeros((1, FFN_COLS), F32)
        sums = lax.fori_loop(0, t // ch, chunk, [([zero] * FFN_KERNEL, zero)] * 2, unroll=2)
        for (dw, db), dw_ref, db_ref in zip(sums, (dwg_ref, dwv_ref), (dbg_ref, dbv_ref)):
            for kk in range(FFN_KERNEL):
                dw_ref[kk:kk + 1, :] += dw[kk]
            db_ref[...] += db

    big, wshape, bshape = SDS((rows, D_FF), BF16), SDS((FFN_KERNEL, D_FF), F32), SDS((1, D_FF), F32)
    if payload is not None:
        payload = ("all_to_all", payload, _ffn_host_steps(nt))
    window = pltpu.VMEM((t + 2 * FFN_HALO, FFN_COLS), F32)
    return _hosted_call(
        body, payload, name=name, out_shape=(big, big, wshape, wshape, bshape, bshape), grid=(N_FF_BLOCKS, nt),
        in_specs=[cur, cur, prev, prev, nxt, nxt, cur, nxt, wsp, wsp, bsp, bsp], out_specs=(cur, cur, wsp, wsp, bsp, bsp),
        scratch_shapes=(window, window, pltpu.VMEM((t + FFN_HALO, FFN_COLS), F32)),
        args=(up_g, up_v, up_g, up_v, up_g, up_v, dact, dact, w_g, w_v, b_g, b_v))


def _adamw_update(parts, w_ref, m_ref, v_ref, g_ref, d_ref, nm_ref, nv_ref):
    g = parts[0].astype(F32)
    for s in range(1, N_DEV):
        g = g + parts[s].astype(F32)
    nm = ADAM_B1 * m_ref[...] + (1.0 - ADAM_B1) * g
    nv = ADAM_B2 * v_ref[...] + (1.0 - ADAM_B2) * jnp.square(g)
    m_hat = nm / (1.0 - ADAM_B1 ** ADAM_STEP)
    v_hat = nv / (1.0 - ADAM_B2 ** ADAM_STEP)
    g_ref[...] = g
    d_ref[...] = -ADAM_LR * (m_hat / (jnp.sqrt(v_hat) + ADAM_EPS) + ADAM_WD * w_ref[...])
    nm_ref[...] = nm
    nv_ref[...] = nv


def _adamw(parts, w, m, v, name):
    rows, cols = w.shape
    t = _pick(rows, (512, 256, 128)) if rows > 512 else rows

    def body(p_ref, *refs):
        _adamw_update(p_ref[...], *refs)

    row = pl.BlockSpec((t, cols), lambda i: (i, 0))
    out = SDS((rows, cols), F32)
    return pl.pallas_call(
        body, name=name, out_shape=(out,) * 4, grid=(rows // t,),
        in_specs=[pl.BlockSpec((N_DEV, t, cols), lambda i: (0, i, 0)), row, row, row], out_specs=(row,) * 4,
        compiler_params=_params(),
    )(parts, w, m, v)


def _adamw_packed(sources, w, m, v, name):
    rows, cols = w.shape
    t = ADAMW_ROW_BLOCK
    nb = rows // DEPTH // t
    (src0, first0), (src1, first1) = sources
    assert first0 % t == 0 and first1 % t == 0 and rows % (DEPTH * t) == 0

    def body(p0_ref, p1_ref, *refs):
        layer = pl.program_id(0)
        _adamw_update(jnp.where(layer == 0, p0_ref[...], p1_ref[...]), *refs)

    spec0 = pl.BlockSpec((N_DEV, t, cols), lambda l, i: (0, first0 // t + i * (1 - l) + (nb - 1) * l, 0))
    spec1 = pl.BlockSpec((N_DEV, t, cols), lambda l, i: (0, first1 // t + i * l, 0))
    row = pl.BlockSpec((t, cols), lambda l, i: (l * nb + i, 0))
    out = SDS((rows, cols), F32)
    return pl.pallas_call(body, name=name, out_shape=(out,) * 4, grid=(DEPTH, nb), in_specs=[spec0, spec1, row, row, row],
                          out_specs=(row,) * 4, compiler_params=_params())(src0, src1, w, m, v)


_COMM_SEMAPHORES = [pltpu.SemaphoreType.DMA((N_DEV - 1,)), pltpu.SemaphoreType.DMA((N_DEV - 1,)), pltpu.SemaphoreType.DMA]


def _gather_steps(x_ref, out_ref, send_sems, recv_sems, local_sem):
    x_, y_, c_ = lax.axis_index("x"), lax.axis_index("y"), lax.axis_index("c")
    me, sibling = (x_, y_, c_), (x_, y_, 1 - c_)
    chips = [(1 - x_, y_), (x_, 1 - y_), (1 - x_, 1 - y_)]

    def slot(px, py, pc):
        return out_ref.at[4 * px + 2 * py + pc]

    def copy(kk, block, to, src=None):
        return pltpu.make_async_remote_copy(
            src_ref=slot(*block) if src is None else src, dst_ref=slot(*block),
            send_sem=send_sems.at[kk], recv_sem=recv_sems.at[kk], device_id=to, device_id_type=MESH)

    def mine():
        return pltpu.make_async_copy(x_ref, slot(*me), local_sem)

    def first():
        return [copy(0, me, sibling, src=x_ref)] + [copy(1 + j, me, (*chip, c_), src=x_ref) for j, chip in enumerate(chips)]

    def passed():
        return [copy(4 + j, (*chip, c_), sibling) for j, chip in enumerate(chips)]

    def start():
        mine().start()
        for cp in first():
            cp.start()

    def forward():
        for j, (chip, cp) in enumerate(zip(chips, passed())):
            copy(1 + j, (*chip, c_), me).wait_recv()
            cp.start()

    def finish():
        copy(0, sibling, me).wait_recv()
        for j, chip in enumerate(chips):
            copy(4 + j, (*chip, 1 - c_), me).wait_recv()
        for cp in first() + passed():
            cp.wait_send()
        mine().wait()

    return start, forward, finish


def _exchange_steps(x_ref, out_ref, send_sems, recv_sems, local_sem):
    x_, y_, c_ = lax.axis_index("x"), lax.axis_index("y"), lax.axis_index("c")
    me = 4 * x_ + 2 * y_ + c_

    def mine():
        return pltpu.make_async_copy(x_ref.at[me], out_ref.at[me], local_sem)

    def copies():
        out = []
        for r in range(1, N_DEV):
            px = 1 - x_ if r & 4 else x_
            py = 1 - y_ if r & 2 else y_
            pc = 1 - c_ if r & 1 else c_
            out.append(pltpu.make_async_remote_copy(
                src_ref=x_ref.at[4 * px + 2 * py + pc], dst_ref=out_ref.at[me],
                send_sem=send_sems.at[r - 1], recv_sem=recv_sems.at[r - 1], device_id=(px, py, pc), device_id_type=MESH))
        return out

    def start():
        mine().start()
        for cp in copies():
            cp.start()

    def finish():
        for cp in copies():
            cp.wait_recv()
        for cp in copies():
            cp.wait_send()
        mine().wait()

    return start, finish


def _hosted_call(body, exchange, *, name, out_shape, grid, in_specs, out_specs, args, scratch_shapes=()):
    if exchange is None:
        return pl.pallas_call(body, name=name, out_shape=out_shape, grid=grid, in_specs=in_specs, out_specs=out_specs,
                              scratch_shapes=list(scratch_shapes), compiler_params=_params())(*args)
    kind, payload, when = exchange
    n_in, n_out, n_scratch = len(in_specs), len(out_specs), len(scratch_shapes)
    result = SDS((N_DEV,) + payload.shape, payload.dtype) if kind == "gather" else SDS(payload.shape, payload.dtype)

    def hosting(*refs):
        ins, pay_ref = refs[:n_in], refs[n_in]
        outs, res_ref = refs[n_in + 1:n_in + 1 + n_out], refs[n_in + 1 + n_out]
        rest = refs[n_in + 2 + n_out:]
        scratch, sems = rest[:n_scratch], rest[n_scratch:]
        first, middle, last = when()
        if kind == "gather":
            start, forward, finish = _gather_steps(pay_ref, res_ref, *sems)
            pl.when(first)(start)
            pl.when(middle)(forward)
        else:
            start, finish = _exchange_steps(pay_ref, res_ref, *sems)
            pl.when(first)(start)
        body(*ins, *outs, *scratch)
        pl.when(last)(finish)

    hbm = pl.BlockSpec(memory_space=pl.ANY)
    return pl.pallas_call(
        hosting, name=name, out_shape=tuple(out_shape) + (result,), grid=grid, in_specs=list(in_specs) + [hbm],
        out_specs=tuple(out_specs) + (hbm,), scratch_shapes=list(scratch_shapes) + _COMM_SEMAPHORES,
        compiler_params=_params(has_side_effects=True),
    )(*args, payload)


def _all_gather(x, name):
    def body(x_ref, out_ref, send_sems, recv_sems, local_sem):
        for step in _gather_steps(x_ref, out_ref, send_sems, recv_sems, local_sem):
            step()

    hbm = pl.BlockSpec(memory_space=pl.ANY)
    return pl.pallas_call(body, name=name, out_shape=SDS((N_DEV,) + x.shape, x.dtype), in_specs=[hbm], out_specs=hbm,
                          scratch_shapes=_COMM_SEMAPHORES, compiler_params=pltpu.CompilerParams(has_side_effects=True))(x)


def _all_to_all(x, name):
    def body(x_ref, out_ref, send_sems, recv_sems, local_sem):
        for step in _exchange_steps(x_ref, out_ref, send_sems, recv_sems, local_sem):
            step()

    hbm = pl.BlockSpec(memory_space=pl.ANY)
    return pl.pallas_call(body, name=name, out_shape=SDS(x.shape, x.dtype), in_specs=[hbm], out_specs=hbm,
                          scratch_shapes=_COMM_SEMAPHORES, compiler_params=pltpu.CompilerParams(has_side_effects=True))(x)


BIG = ("w_in", "cv_pw_w", "w_out", "x_wq", "x_wk", "x_wv", "x_wo", "ffn_w_up", "ffn_w_down")
_MIXER = (("w_in", 352), ("w_out", 128))
_FFN = (("ffn_w_up", 704), ("ffn_w_down", 352))
_CROSS = (("x_wq", 128), ("x_wk", 128), ("x_wv", 128), ("x_wo", 128))
GROUPS = {
    "a": tuple((n, 0, r) for n, r in _MIXER),
    "m1": tuple((n, 1, r) for n, r in _MIXER),
    "bx": tuple((n, 0, r) for n, r in _CROSS),
    "cx": tuple((n, 1, r) for n, r in _CROSS),
    "bf": tuple((n, 0, r) for n, r in _FFN),
    "cf": tuple((n, 1, r) for n, r in _FFN),
}
GRADIENT_GROUPS = ("a", "m1", "bf", "bx", "cf", "cx")
TRANSPOSED = ("w_in", "ffn_w_up")
PW_ROWS = 16
ADAMW_ROW_BLOCK = 32


def _group_rows(group):
    out, first = {}, 0
    for n, l, r in GROUPS[group]:
        out[(n, l)] = (first, r)
        first += r
    return out


def _where_is(name, layer):
    for group in GRADIENT_GROUPS:
        rows = _group_rows(group)
        if (name, layer) in rows:
            return (group,) + rows[(name, layer)]
    raise KeyError((name, layer))


def _pack_weights(group, wts):
    pieces = []
    for n, l, _ in GROUPS[group]:
        w = wts[n][l].astype(BF16)
        pieces.append(w.T if n in TRANSPOSED else w)
    if group == "a":
        pieces.append(wts["cv_pw_w"].astype(BF16).reshape(PW_ROWS, PAYLOAD_COLS))
    return jnp.concatenate(pieces, axis=0)


def _unpack_weights(group, gathered):
    return {key: gathered[:, first:first + r, :].reshape(N_DEV * r, PAYLOAD_COLS)
            for key, (first, r) in _group_rows(group).items()}


def _pack_grads(group, grads):
    pieces = []
    for n, l, r in GROUPS[group]:
        g = grads[n][l]
        parts = g if isinstance(g, tuple) else (g,)
        pieces.append(jnp.concatenate([p.reshape(-1, r, PAYLOAD_COLS) for p in parts], axis=0))
    if group == "a":
        pieces.append(_to_shards("cv_pw_w", jnp.stack(grads["cv_pw_w"])).reshape(N_DEV, PW_ROWS, PAYLOAD_COLS))
    return jnp.concatenate(pieces, axis=1)


COL_SHARDED = ("w_in", "ffn_w_up", "cv_w", "ffn_conv_w")
SMALL_SHARDED = ("cv_w", "ffn_conv_w")
REPLICATED = ("mix_norm_pre", "cv_b", "cv_ln_g", "cv_ln_b", "cv_pw_b", "mix_norm_post", "x_norm_pre", "mem_norm",
              "x_norm_post", "ffn_norm_pre", "ffn_conv_b", "ffn_norm_post")
WEIGHTS = ("mix_norm_pre", "w_in", "cv_w", "cv_b", "cv_ln_g", "cv_ln_b", "cv_pw_w", "cv_pw_b", "w_out", "mix_norm_post",
           "x_norm_pre", "mem_norm", "x_wq", "x_wk", "x_wv", "x_wo", "x_norm_post", "ffn_norm_pre", "ffn_w_up",
           "ffn_conv_w", "ffn_conv_b", "ffn_w_down", "ffn_norm_post")
PAYLOAD_COLS = 1024


PAYLOAD_ROW_TILE = 16


def _pad_rows(flat, cols):
    n = flat.shape[-1]
    rows = -(-n // (cols * PAYLOAD_ROW_TILE)) * PAYLOAD_ROW_TILE
    pad = rows * cols - n
    if pad:
        flat = jnp.concatenate([flat, jnp.zeros(flat.shape[:-1] + (pad,), flat.dtype)], axis=-1)
    return flat.reshape(flat.shape[:-1] + (rows, cols))


def _unshard(name, parts):
    n, depth, r, c = parts.shape
    if name in COL_SHARDED:
        return parts.transpose(1, 2, 0, 3).reshape(depth, r, n * c)
    return parts.transpose(1, 0, 2, 3).reshape(depth, n * r, c)


def _to_shards(name, full):
    depth, r, c = full.shape
    if name in COL_SHARDED:
        return full.reshape(depth, r, N_DEV, c // N_DEV).transpose(2, 0, 1, 3).reshape(N_DEV, -1)
    return full.reshape(depth, N_DEV, r // N_DEV, c).transpose(1, 0, 2, 3).reshape(N_DEV, -1)


def _heads_major(x, h):
    return x.reshape(x.shape[0], h, HEAD_DIM).transpose(1, 0, 2)


def _tokens_major(x):
    return x.transpose(1, 0, 2).reshape(x.shape[1], -1)


def _ffn_halves(p):
    w, b = p["ffn_conv_w"], p["ffn_conv_b"]
    return w[:, :D_FF], w[:, D_FF:], b[:, :D_FF], b[:, D_FF:]


def _layer_fwd(l, h, hn, p, mem, cos, sin, g_next, payload, unpack, ffn_shards, ffn_payload):
    p = dict(p)
    sv = {"h0": h, "hn0": hn}
    u = _mm(hn, p["w_in"], "nt", F32, f"l{l}_in_proj")
    sv["u"] = u
    sb = _heads_major(u[:, :3 * SB_WIDTH].astype(BF16), 3 * SB_HEADS)
    sb_q, sb_k, sb_v = sb[:SB_HEADS], sb[SB_HEADS:2 * SB_HEADS], sb[2 * SB_HEADS:]
    a_out, sb_tot, sb_first, gathered = _sb_fwd(sb_q, sb_k, sb_v, payload, f"l{l}_sb_fwd")
    p.update(unpack(gathered))
    sv.update(sb_q=sb_q, sb_k=sb_k, sb_v=sb_v, sb_tot=sb_tot, sb_first=sb_first, p=p)

    cv_s, cv_c = _cv_fwd(u, p["cv_w"], p["cv_b"], p["cv_ln_g"], p["cv_ln_b"], f"l{l}_cv_fwd")
    b_out = _mm(cv_s, p["cv_pw_w"], "nn", BF16, f"l{l}_cv_pw", bias=p["cv_pw_b"])
    sv.update(cv_s=cv_s, cv_c=cv_c)

    qk = _rope_fwd(u, cos, sin, f"l{l}_rope_fwd")
    up_t = ffn_shards[0].astype(BF16).T
    half_rows = up_t.shape[0] // 2
    carried = (up_t[:half_rows], up_t[half_rows:], ffn_shards[1].astype(BF16))
    outs, lses, got = [], [], []
    for b, (_, dil) in enumerate(DL_PATTERN):
        o, lse, gathered = _dl_fwd(qk, u, dil, f"l{l}_dl{b}_fwd", carried[b])
        outs.append(o)
        lses.append(lse)
        got.append(gathered)
    up_blocks = jnp.concatenate(got[:2], axis=1)
    half = N_DEV // 2
    p["ffn_w_up"] = (up_blocks[:half].reshape(-1, PAYLOAD_COLS), up_blocks[half:].reshape(-1, PAYLOAD_COLS))
    p["ffn_w_down"] = got[2].reshape(-1, PAYLOAD_COLS)
    c_out, c_out_f32, w1, w2, w3 = _dl_mix_fwd(outs, lses, f"l{l}_dl_mix")
    sv.update(dl_qk=qk, dl_lse=lses, dl_o=c_out_f32, dl_w=(w1, w2, w3))

    mix = jnp.concatenate([_tokens_major(a_out), b_out, c_out], axis=-1)
    y = _mm(mix, p["w_out"], "nn", F32, f"l{l}_out_proj")
    h1, hn1 = _res_norm_fwd(h, y, p["mix_norm_post"], p["x_norm_pre"], f"l{l}_mix_post")
    sv.update(mix=mix, y_mix=y, h1=h1, hn1=hn1)

    xq = _mm(hn1, p["x_wq"], "nn", BF16, f"l{l}_xq")
    memn = _rms_fwd(mem, p["mem_norm"], f"l{l}_mem_norm")
    xk = _mm(memn, p["x_wk"], "nn", BF16, f"l{l}_xk")
    xv = _mm(memn, p["x_wv"], "nn", BF16, f"l{l}_xv")
    xo = _xattn_fwd(xq, xk, xv, f"l{l}_xattn_fwd")
    y = _mm(xo, p["x_wo"], "nn", F32, f"l{l}_xo_proj")
    h2, hn2 = _res_norm_fwd(h1, y, p["x_norm_post"], p["ffn_norm_pre"], f"l{l}_x_post")
    sv.update(xq=xq, xk=xk, xv=xv, xo=xo, memn=memn, y_x=y, h2=h2, hn2=hn2)

    up_g = _mm(hn2, p["ffn_w_up"][0], "nt", F32, f"l{l}_ffn_up_gate")
    up_v = _mm(hn2, p["ffn_w_up"][1], "nt", F32, f"l{l}_ffn_up_val")
    act, *ffn_gathered = _ffn_act_fwd(up_g, up_v, *_ffn_halves(p), f"l{l}_ffn_act", ffn_payload)
    y = _mm(act, p["ffn_w_down"], "nn", F32, f"l{l}_ffn_down")
    h3, hn3 = _res_norm_fwd(h2, y, p["ffn_norm_post"], g_next, f"l{l}_ffn_post")
    sv.update(up_g=up_g, up_v=up_v, act=act, y_ffn=y)
    return h3, hn3, sv, (ffn_gathered[0] if ffn_gathered else None)


def _layer_bwd(l, dh, dy, sv, mem, cos, sin, prev_post, ffn_payload, pack):
    p = sv["p"]
    gr = {}
    received = {}
    dact = _mm(dy, p["ffn_w_down"], "nt", F32, f"l{l}_d_act")
    gr["ffn_w_down"] = _mm(sv["act"], dy, "tn", BF16, f"l{l}_dw_down")
    dup_g, dup_v, dwg, dwv, dbg, dbv, *got = _ffn_act_bwd(sv["up_g"], sv["up_v"], dact, *_ffn_halves(p), f"l{l}_ffn_act_bwd",
                                                         ffn_payload)
    if got:
        received["ffn_payload"] = got[0]
    gr["ffn_conv_w"] = jnp.concatenate([dwg, dwv], axis=1)
    gr["ffn_conv_b"] = jnp.concatenate([dbg, dbv], axis=1)
    dhn = (_mm(dup_g, p["ffn_w_up"][0], "nn", F32, f"l{l}_d_hn2_gate"), _mm(dup_v, p["ffn_w_up"][1], "nn", F32, f"l{l}_d_hn2_val"))
    gr["ffn_w_up"] = (_mm(dup_g, sv["hn2"], "tn", BF16, f"l{l}_dw_up_gate"), _mm(dup_v, sv["hn2"], "tn", BF16, f"l{l}_dw_up_val"))
    dh, dy, gr["ffn_norm_pre"], gr["x_norm_post"] = _norm_bwd(
        dh, (sv["h2"], p["ffn_norm_pre"], dhn), (sv["y_x"], p["x_norm_post"]), f"l{l}_x_post_bwd")

    do = _mm(dy, p["x_wo"], "nt", BF16, f"l{l}_d_xo")
    gr["x_wo"] = _mm(sv["xo"], dy, "tn", BF16, f"l{l}_dw_xo")
    dq, dk, dv = _xattn_bwd(sv["xq"], sv["xk"], sv["xv"], do, f"l{l}_xattn_bwd")
    dhn = _mm(dq, p["x_wq"], "nt", F32, f"l{l}_d_hn1")
    gr["x_wq"] = _mm(sv["hn1"], dq, "tn", BF16, f"l{l}_dw_xq")
    gr["x_wk"] = _mm(sv["memn"], dk, "tn", BF16, f"l{l}_dw_xk")
    gr["x_wv"] = _mm(sv["memn"], dv, "tn", BF16, f"l{l}_dw_xv")
    dmemn = _mm(dk, p["x_wk"], "nt", F32, f"l{l}_d_memn_k") + _mm(dv, p["x_wv"], "nt", F32, f"l{l}_d_memn_v")
    gr["mem_norm"] = _rms_gain_grad(mem, p["mem_norm"], dmemn, f"l{l}_mem_norm_bwd")
    dh, dy, gr["x_norm_pre"], gr["mix_norm_post"] = _norm_bwd(
        dh, (sv["h1"], p["x_norm_pre"], dhn), (sv["y_mix"], p["mix_norm_post"]), f"l{l}_mix_post_bwd")

    dmix = _mm(dy, p["w_out"], "nt", F32, f"l{l}_d_mix")
    gr["w_out"] = _mm(sv["mix"], dy, "tn", BF16, f"l{l}_dw_out")
    do_a = _heads_major(dmix[:, :SB_WIDTH].astype(BF16), SB_HEADS)
    dq, dk, dv, received["ffn"] = _sb_bwd(sv["sb_q"], sv["sb_k"], sv["sb_v"], do_a, sv["sb_tot"], sv["sb_first"],
                                          pack("ffn", gr), f"l{l}_sb_bwd")
    du_sb = _tokens_major(jnp.concatenate([dq, dk, dv], axis=0))

    db_out = dmix[:, SB_WIDTH:SB_WIDTH + CV_WIDTH]
    ds = _mm(db_out, p["cv_pw_w"], "nt", F32, f"l{l}_d_cv_s")
    gr["cv_pw_w"] = _mm(sv["cv_s"], db_out, "tn", BF16, f"l{l}_dw_cv_pw")
    du_cv, dcvw, gr["cv_b"], gr["cv_ln_g"], gr["cv_ln_b"], gr["cv_pw_b"] = _cv_bwd(
        sv["u"], sv["cv_c"], ds, db_out, p["cv_w"], p["cv_ln_g"], p["cv_ln_b"], f"l{l}_cv_bwd")
    gr["cv_w"] = dcvw[:CV_KERNEL]

    dqs, dks, dvs = [], [], []
    for b, (_, dil) in enumerate(DL_PATTERN):
        carried = pack("cross", gr) if b == len(DL_PATTERN) - 1 else None
        dq, dk, dv, *got = _dl_bwd(sv["dl_qk"], sv["u"], dmix, sv["dl_o"], sv["dl_w"][b], sv["dl_lse"][b], dil,
                                   f"l{l}_dl{b}_bwd", carried)
        if got:
            received["cross"] = got[0]
        dqs.append(dq)
        dks.append(dk)
        dvs.append(dv)
    du_dl = _rope_bwd(dqs, dks, dvs, cos, sin, f"l{l}_rope_bwd")

    du = jnp.concatenate([du_sb, du_cv, du_dl], axis=-1)
    dhn = _mm(du, p["w_in"], "nn", F32, f"l{l}_d_hn0")
    gr["w_in"] = _mm(du, sv["hn0"], "tn", BF16, f"l{l}_dw_in")
    dh, dy, gr["mix_norm_pre"], dg_prev = _norm_bwd(dh, (sv["h0"], p["mix_norm_pre"], dhn), prev_post, f"l{l}_in_bwd")
    return dh, dy, gr, dg_prev, received


def kernel(x, mem, positions, mix_norm_pre, w_in, cv_w, cv_b, cv_ln_g, cv_ln_b, cv_pw_w, cv_pw_b, w_out, mix_norm_post, x_norm_pre, mem_norm, x_wq, x_wk, x_wv, x_wo, x_norm_post, ffn_norm_pre, ffn_w_up, ffn_conv_w, ffn_conv_b, ffn_w_down, ffn_norm_post, loss_target, m_mix_norm_pre, m_w_in, m_cv_w, m_cv_b, m_cv_ln_g, m_cv_ln_b, m_cv_pw_w, m_cv_pw_b, m_w_out, m_mix_norm_post, m_x_norm_pre, m_mem_norm, m_x_wq, m_x_wk, m_x_wv, m_x_wo, m_x_norm_post, m_ffn_norm_pre, m_ffn_w_up, m_ffn_conv_w, m_ffn_conv_b, m_ffn_w_down, m_ffn_norm_post, v_mix_norm_pre, v_w_in, v_cv_w, v_cv_b, v_cv_ln_g, v_cv_ln_b, v_cv_pw_w, v_cv_pw_b, v_w_out, v_mix_norm_post, v_x_norm_pre, v_mem_norm, v_x_wq, v_x_wk, v_x_wv, v_x_wo, v_x_norm_post, v_ffn_norm_pre, v_ffn_w_up, v_ffn_conv_w, v_ffn_conv_b, v_ffn_w_down, v_ffn_norm_post):
    args = locals()
    wts = {n: args[n] for n in WEIGHTS}
    mom = {n: args["m_" + n] for n in WEIGHTS}
    var = {n: args["v_" + n] for n in WEIGHTS}

    x2, mem2, target = x[0], mem[0], loss_target[0]

    gathered_a = _all_gather(_pack_weights("a", wts), "weights_all_gather")
    small_payload = _pad_rows(jnp.concatenate([wts[n].reshape(-1) for n in SMALL_SHARDED]), PAYLOAD_COLS)
    small = _all_gather(small_payload, "small_weights_all_gather").reshape(N_DEV, -1)
    small_full = {}
    off = 0
    for n in SMALL_SHARDED:
        size = wts[n].size
        small_full[n] = _unshard(n, small[:, off:off + size].reshape((N_DEV,) + wts[n].shape))
        off += size
    pw_first = sum(r for _, _, r in GROUPS["a"])
    pw_full = _unshard("cv_pw_w", gathered_a[:, pw_first:, :].reshape((N_DEV,) + wts["cv_pw_w"].shape))

    def mixer_params(l, unpacked):
        p = {n: wts[n][l][None, :] for n in REPLICATED}
        p.update({n: small_full[n][l] for n in SMALL_SHARDED})
        p.update(cv_pw_w=pw_full[l], w_in=unpacked[("w_in", l)], w_out=unpacked[("w_out", l)])
        return p

    def of_layer(group, l):
        return lambda gathered: {n: w for (n, ll), w in _unpack_weights(group, gathered).items() if ll == l}

    pos = positions[0].astype(F32)
    half = HEAD_DIM // 2
    inv_freq = ROPE_THETA ** (-jnp.arange(half, dtype=F32) / half)
    ang = pos[:, None] * inv_freq
    cos = jnp.tile(jnp.cos(ang), (1, LANES // half))
    sin = jnp.tile(jnp.sin(ang), (1, LANES // half))

    p0 = mixer_params(0, _unpack_weights("a", gathered_a))
    hn = _rms_fwd(x2, p0["mix_norm_pre"], "l0_in_norm")
    ffn_shards = [(wts["ffn_w_up"][l], wts["ffn_w_down"][l]) for l in range(DEPTH)]
    h, hn, sv0, gathered_m1 = _layer_fwd(0, x2, hn, p0, mem2, cos, sin, wts["mix_norm_pre"][1][None, :],
                                         _pack_weights("bx", wts), of_layer("bx", 0), ffn_shards[0], _pack_weights("m1", wts))
    p1 = mixer_params(1, _unpack_weights("m1", gathered_m1))
    h, _, sv1, _ = _layer_fwd(1, h, hn, p1, mem2, cos, sin, None, _pack_weights("cx", wts), of_layer("cx", 1), ffn_shards[1],
                              None)
    loss_part, dh = _loss_fwd(h, target, "loss")
    loss = lax.psum(loss_part[0, 0], ("x", "y", "c"))

    grads = {n: [None] * DEPTH for n in WEIGHTS}
    dh, dy, _, grads["ffn_norm_post"][1] = _norm_bwd(dh, None, (sv1["y_ffn"], sv1["p"]["ffn_norm_post"]), "last_post_bwd")

    def packer(l, groups):
        return lambda which, gr: _pack_grads(groups[which], {n: {l: g} for n, g in gr.items()})

    dh, dy, gr, grads["ffn_norm_post"][0], got1 = _layer_bwd(
        1, dh, dy, sv1, mem2, cos, sin, (sv0["y_ffn"], sv0["p"]["ffn_norm_post"]), None, packer(1, {"ffn": "cf", "cross": "cx"}))
    for n, g in gr.items():
        grads[n][1] = g
    dh, _, gr, _, got0 = _layer_bwd(0, dh, dy, sv0, mem2, cos, sin, None, _pack_grads("m1", grads),
                                    packer(0, {"ffn": "bf", "cross": "bx"}))
    for n, g in gr.items():
        grads[n][0] = g
    grad_x = dh

    received_a = _all_to_all(_pack_grads("a", grads), "grads_all_to_all")
    received = {"a": received_a, "m1": got0["ffn_payload"], "bf": got0["ffn"], "bx": got0["cross"],
                "cf": got1["ffn"], "cx": got1["cross"]}
    small_rows = jnp.concatenate([_to_shards(n, jnp.stack(grads[n])) for n in SMALL_SHARDED], axis=1)
    rep_flat = jnp.concatenate([jnp.stack([g.reshape(-1) for g in grads[n]]).reshape(-1) for n in REPLICATED])
    rep_rows = jnp.broadcast_to(rep_flat[None], (N_DEV, rep_flat.shape[0]))
    f32_rows = _pad_rows(jnp.concatenate([small_rows, rep_rows], axis=1), PAYLOAD_COLS)
    small_parts = _all_to_all(f32_rows, "small_grads_all_to_all")

    res = {}
    for n in BIG:
        shape = wts[n].shape
        two_d = (shape[0] * shape[1], shape[2])
        operands = (wts[n].reshape(two_d), mom[n].reshape(two_d), var[n].reshape(two_d))
        if n == "cv_pw_w":
            outs = _adamw(received_a[:, pw_first:, :].reshape((N_DEV,) + two_d), *operands, f"adamw_{n}")
        elif n in TRANSPOSED:
            layers = []
            for l in range(DEPTH):
                group, first, r = _where_is(n, l)
                layers.append(received[group][:, first:first + r, :])
            parts = jnp.stack(layers, axis=1).transpose(0, 1, 3, 2).reshape((N_DEV,) + two_d)
            outs = _adamw(parts, *operands, f"adamw_{n}")
        else:
            sources = []
            for l in range(DEPTH):
                group, first, _ = _where_is(n, l)
                sources.append((received[group], first))
            outs = _adamw_packed(sources, *operands, f"adamw_{n}")
        res[n] = [o.reshape(shape) for o in outs]
    small_names = SMALL_SHARDED + REPLICATED
    flat_w = _pad_rows(jnp.concatenate([wts[n].reshape(-1) for n in small_names]), PAYLOAD_COLS)
    flat_m = _pad_rows(jnp.concatenate([mom[n].reshape(-1) for n in small_names]), PAYLOAD_COLS)
    flat_v = _pad_rows(jnp.concatenate([var[n].reshape(-1) for n in small_names]), PAYLOAD_COLS)
    outs = _adamw(small_parts, flat_w, flat_m, flat_v, "adamw_small")
    outs = [o.reshape(-1) for o in outs]
    off = 0
    for n in small_names:
        size = wts[n].size
        res[n] = [o[off:off + size].reshape(wts[n].shape) for o in outs]
        off += size

    result = [loss, grad_x[None]]
    for kind in range(4):
        result += [res[n][kind] for n in WEIGHTS]
    return tuple(result)
```

```python
import functools
import math

import jax
import jax.numpy as jnp
from jax import lax
from jax.experimental import pallas as pl
from jax.experimental.pallas import tpu as pltpu

F32, BF16 = jnp.float32, jnp.bfloat16
SDS = jax.ShapeDtypeStruct

D_MODEL = 1024
SEQ = 4096
DEPTH = 2
HEAD_DIM = 64
SB_HEADS = 4
SB_WIDTH = 256
CV_WIDTH = 256
CV_KERNEL = 31
DL_HEADS = 8
DL_WIDTH = 512
IN_WIDTH = 2816
DL_PATTERN = ((128, 1), (512, 4), (2048, 16))
BLOCK = 128
ROPE_THETA = 10000.0
N_MEM = 256
X_HEADS = 4
X_HEAD_DIM = 256
D_FF = 2816
EPS = 1e-6
N_DEV = 8
LANES = 128

ADAM_LR = 0.001
ADAM_B1 = 0.9
ADAM_B2 = 0.999
ADAM_EPS = 1e-08
ADAM_WD = 0.01
ADAM_STEP = 10

VMEM_LIMIT_BYTES = 56 * 1024 * 1024
MESH = pl.DeviceIdType.MESH
NEG = -1e30


def _params(**kw):
    return pltpu.CompilerParams(vmem_limit_bytes=VMEM_LIMIT_BYTES, **kw)


def _pick(n, cands):
    for c in cands:
        if n % c == 0:
            return c
    return n


def _mm(a, b, mode, out_dtype, name, bias=None):
    if mode == "nn":
        (m, k), (k2, n) = a.shape, b.shape
    elif mode == "nt":
        (m, k), (n, k2) = a.shape, b.shape
    else:
        (k, m), (k2, n) = a.shape, b.shape
    assert k == k2, (a.shape, b.shape, mode)
    tm = _pick(m, (1024, 1408, 512, 256, 128))
    tn = _pick(n, (1024, 1408, 512, 256, 128))
    tk = k if k <= 2048 else _pick(k, (2048, 1408, 1024, 512))
    nk = k // tk
    dims = {"nn": ((1,), (0,)), "nt": ((1,), (1,)), "tn": ((0,), (0,))}[mode]

    def body(*refs):
        refs = list(refs)
        acc_ref = refs.pop() if nk > 1 else None
        a_ref, b_ref = refs[0], refs[1]
        bias_ref = refs[2] if bias is not None else None
        o_ref = refs[-1]
        p = lax.dot_general(a_ref[...].astype(BF16), b_ref[...].astype(BF16), (dims, ((), ())),
                            preferred_element_type=F32)

        def finish(v):
            if bias_ref is not None:
                v = v + bias_ref[...]
            o_ref[...] = v.astype(out_dtype)

        if nk == 1:
            finish(p)
        else:
            kk = pl.program_id(2)

            @pl.when(kk == 0)
            def _():
                acc_ref[...] = p

            @pl.when(kk > 0)
            def _():
                acc_ref[...] += p

            @pl.when(kk == nk - 1)
            def _():
                finish(acc_ref[...])

    a_spec = pl.BlockSpec((tk, tm), lambda i, j, kk: (kk, i)) if mode == "tn" else pl.BlockSpec((tm, tk), lambda i, j, kk: (i, kk))
    b_spec = pl.BlockSpec((tn, tk), lambda i, j, kk: (j, kk)) if mode == "nt" else pl.BlockSpec((tk, tn), lambda i, j, kk: (kk, j))
    in_specs = [a_spec, b_spec]
    args = [a, b]
    if bias is not None:
        in_specs.append(pl.BlockSpec((1, tn), lambda i, j, kk: (0, j)))
        args.append(bias)
    return pl.pallas_call(
        body, name=name, out_shape=SDS((m, n), out_dtype), grid=(m // tm, n // tn, nk),
        in_specs=in_specs, out_specs=pl.BlockSpec((tm, tn), lambda i, j, kk: (i, j)),
        scratch_shapes=[pltpu.VMEM((tm, tn), F32)] if nk > 1 else [], compiler_params=_params(),
    )(*args)


def _rms(x, g):
    r = lax.rsqrt(jnp.mean(x * x, axis=-1, keepdims=True) + EPS)
    return x * r * g


def _rms_bwd(x, g, dy):
    r = lax.rsqrt(jnp.mean(x * x, axis=-1, keepdims=True) + EPS)
    xh = x * r
    dyg = dy * g
    dx = r * (dyg - xh * jnp.mean(dyg * xh, axis=-1, keepdims=True))
    return dx, dy * xh


def _rms_fwd(x, g, name):
    rows, d = x.shape
    t = min(rows, 512)

    def body(x_ref, g_ref, o_ref):
        o_ref[...] = _rms(x_ref[...], g_ref[...]).astype(BF16)

    return pl.pallas_call(
        body, name=name, out_shape=SDS((rows, d), BF16), grid=(rows // t,),
        in_specs=[pl.BlockSpec((t, d), lambda i: (i, 0)), pl.BlockSpec((1, d), lambda i: (0, 0))],
        out_specs=pl.BlockSpec((t, d), lambda i: (i, 0)), compiler_params=_params(),
    )(x, g)


def _res_norm_fwd(h, y, g_post, g_next, name):
    rows, d = h.shape
    t = 512
    has_next = g_next is not None

    def body(*refs):
        if has_next:
            h_ref, y_ref, gp_ref, gn_ref, h1_ref, hn_ref = refs
        else:
            h_ref, y_ref, gp_ref, h1_ref = refs
        h1 = h_ref[...] + _rms(y_ref[...], gp_ref[...])
        h1_ref[...] = h1
        if has_next:
            hn_ref[...] = _rms(h1, gn_ref[...]).astype(BF16)

    row = pl.BlockSpec((t, d), lambda i: (i, 0))
    vec = pl.BlockSpec((1, d), lambda i: (0, 0))
    in_specs = [row, row, vec] + ([vec] if has_next else [])
    args = [h, y, g_post] + ([g_next] if has_next else [])
    out_shape = [SDS((rows, d), F32)] + ([SDS((rows, d), BF16)] if has_next else [])
    out_specs = [row] + ([row] if has_next else [])
    res = pl.pallas_call(body, name=name, out_shape=out_shape, grid=(rows // t,), in_specs=in_specs,
                         out_specs=out_specs, compiler_params=_params())(*args)
    return (res[0], res[1]) if has_next else (res[0], None)


def _norm_bwd(dh, pre, post, name):
    rows, d = dh.shape
    t = 512
    has_pre, has_post = pre is not None, post is not None
    if has_pre:
        dhns = pre[2] if isinstance(pre[2], tuple) else (pre[2],)
        pre = (pre[0], pre[1]) + dhns

    def body(*refs):
        refs = list(refs)
        dh_ref = refs.pop(0)
        if has_pre:
            h_ref, gpre_ref = refs.pop(0), refs.pop(0)
            dhn_refs = [refs.pop(0) for _ in dhns]
        if has_post:
            y_ref, gpost_ref = refs.pop(0), refs.pop(0)
        dht_ref = refs.pop(0)
        if has_post:
            dy_ref = refs.pop(0)
        if has_pre:
            dgpre_ref = refs.pop(0)
        if has_post:
            dgpost_ref = refs.pop(0)
        i = pl.program_id(0)
        dht = dh_ref[...]
        if has_pre:
            dhn = dhn_refs[0][...]
            for r in dhn_refs[1:]:
                dhn = dhn + r[...]
            dx, dgr = _rms_bwd(h_ref[...], gpre_ref[...], dhn)
            dht = dht + dx

            @pl.when(i == 0)
            def _():
                dgpre_ref[...] = jnp.zeros_like(dgpre_ref)

            dgpre_ref[...] += jnp.sum(dgr, axis=0, keepdims=True)
        dht_ref[...] = dht
        if has_post:
            dy, dgr = _rms_bwd(y_ref[...], gpost_ref[...], dht)
            dy_ref[...] = dy.astype(BF16)

            @pl.when(i == 0)
            def _():
                dgpost_ref[...] = jnp.zeros_like(dgpost_ref)

            dgpost_ref[...] += jnp.sum(dgr, axis=0, keepdims=True)

    row = pl.BlockSpec((t, d), lambda i: (i, 0))
    vec = pl.BlockSpec((1, d), lambda i: (0, 0))
    in_specs, args = [row], [dh]
    if has_pre:
        in_specs += [row, vec] + [row] * len(dhns)
        args += list(pre)
    if has_post:
        in_specs += [row, vec]
        args += list(post)
    out_shape, out_specs = [SDS((rows, d), F32)], [row]
    if has_post:
        out_shape.append(SDS((rows, d), BF16))
        out_specs.append(row)
    if has_pre:
        out_shape.append(SDS((1, d), F32))
        out_specs.append(vec)
    if has_post:
        out_shape.append(SDS((1, d), F32))
        out_specs.append(vec)
    res = list(pl.pallas_call(body, name=name, out_shape=out_shape, grid=(rows // t,), in_specs=in_specs,
                              out_specs=out_specs, compiler_params=_params())(*args))
    dht = res.pop(0)
    dy = res.pop(0) if has_post else None
    dgpre = res.pop(0) if has_pre else None
    dgpost = res.pop(0) if has_post else None
    return dht, dy, dgpre, dgpost


def _rms_gain_grad(x, g, dy, name):
    rows, d = x.shape

    def body(x_ref, g_ref, dy_ref, dg_ref):
        _, dgr = _rms_bwd(x_ref[...], g_ref[...], dy_ref[...])
        dg_ref[...] = jnp.sum(dgr, axis=0, keepdims=True)

    return pl.pallas_call(body, name=name, out_shape=SDS((1, d), F32), compiler_params=_params())(x, g, dy)


def _loss_fwd(h, target, name):
    rows, d = h.shape
    t = 512

    def body(h_ref, t_ref, loss_ref, dh_ref):
        i = pl.program_id(0)
        err = h_ref[...] - t_ref[...]
        dh_ref[...] = err * (1.0 / d)

        @pl.when(i == 0)
        def _():
            loss_ref[...] = jnp.zeros_like(loss_ref)

        part = jnp.sum(jnp.sum(err * err, axis=1, keepdims=True), axis=0, keepdims=True) * (0.5 / d)
        loss_ref[...] += jnp.broadcast_to(part, loss_ref.shape)

    row = pl.BlockSpec((t, d), lambda i: (i, 0))
    return pl.pallas_call(
        body, name=name, out_shape=(SDS((1, LANES), F32), SDS((rows, d), F32)), grid=(rows // t,),
        in_specs=[row, row], out_specs=(pl.BlockSpec((1, LANES), lambda i: (0, 0)), row), compiler_params=_params(),
    )(h, target)


def _rot_half(x, sign):
    w = x.shape[-1]
    lane = lax.broadcasted_iota(jnp.int32, x.shape, 1)
    first = (lane % HEAD_DIM) < (HEAD_DIM // 2)
    return jnp.where(first, -sign * pltpu.roll(x, w - HEAD_DIM // 2, axis=1), sign * pltpu.roll(x, HEAD_DIM // 2, axis=1))


def _rope_fwd(u, cos, sin, name):
    rows = u.shape[0]
    t, cw = 512, 256
    first_col = (3 * SB_WIDTH + 2 * CV_WIDTH) // cw

    def body(u_ref, c_ref, s_ref, o_ref):
        x = u_ref[...]
        c = jnp.tile(c_ref[...], (1, cw // LANES))
        s = jnp.tile(s_ref[...], (1, cw // LANES))
        o_ref[...] = x * c + _rot_half(x, 1.0) * s

    tab = pl.BlockSpec((t, LANES), lambda i, j: (i, 0))
    return pl.pallas_call(
        body, name=name, out_shape=SDS((rows, 2 * DL_WIDTH), F32), grid=(rows // t, 2 * DL_WIDTH // cw),
        in_specs=[pl.BlockSpec((t, cw), lambda i, j: (i, first_col + j)), tab, tab],
        out_specs=pl.BlockSpec((t, cw), lambda i, j: (i, j)), compiler_params=_params(),
    )(u, cos, sin)


def _rope_bwd(dqs, dks, dvs, cos, sin, name):
    rows = dqs[0].shape[0]
    t, w = 256, DL_WIDTH

    def body(*refs):
        c = jnp.tile(refs[9][...], (1, w // LANES))
        s = jnp.tile(refs[10][...], (1, w // LANES))
        o_ref = refs[11]
        dq = refs[0][...] + refs[1][...] + refs[2][...]
        dk = refs[3][...] + refs[4][...] + refs[5][...]
        dv = refs[6][...] + refs[7][...] + refs[8][...]
        o_ref[:, 0:w] = (dq * c + _rot_half(dq, -1.0) * s).astype(BF16)
        o_ref[:, w:2 * w] = (dk * c + _rot_half(dk, -1.0) * s).astype(BF16)
        o_ref[:, 2 * w:3 * w] = dv.astype(BF16)

    row = pl.BlockSpec((t, w), lambda i: (i, 0))
    tab = pl.BlockSpec((t, LANES), lambda i: (i, 0))
    return pl.pallas_call(
        body, name=name, out_shape=SDS((rows, 3 * w), BF16), grid=(rows // t,), in_specs=[row] * 9 + [tab, tab],
        out_specs=pl.BlockSpec((t, 3 * w), lambda i: (i, 0)), compiler_params=_params(),
    )(*dqs, *dks, *dvs, cos, sin)


SB_TILE = 256
SB_ZERO_AFTER = 110.0
SB_FIRST_BLOCK = (8, LANES)


def _softplus(z):
    return jnp.maximum(z, 0.0) + jnp.log(1.0 + jnp.exp(-jnp.abs(z)))


def _split_dot(x, tri, passes):
    acc = None
    rem = x
    for _ in range(passes):
        part = rem.astype(BF16)
        rem = rem - part.astype(F32)
        d = jnp.dot(part, tri, preferred_element_type=F32)
        acc = d if acc is None else acc + d
    return acc


def _tri(t, rel):
    j = lax.broadcasted_iota(jnp.int32, (t, t), 0)
    s = lax.broadcasted_iota(jnp.int32, (t, t), 1)
    return rel(j, s).astype(BF16)


def _sb_masks(t, i):
    row = lax.broadcasted_iota(jnp.int32, (t, t), 0)
    col = lax.broadcasted_iota(jnp.int32, (t, t), 1)
    return col < row, (row >= 0) & (i >= 1)


def _sb_fwd(q, k, v, payload, name):
    h, s_len, hd = q.shape
    t = SB_TILE
    nq = s_len // t
    scale = hd ** -0.5

    def body(q_ref, k_ref, v_ref, pay_ref, o_ref, tot_ref, first_ref, gathered_ref, send_sems, recv_sems, local_sem):
        hh, i = pl.program_id(0), pl.program_id(1)
        start, forward, finish = _gather_steps(pay_ref, gathered_ref, send_sems, recv_sems, local_sem)
        pl.when((hh == 0) & (i == 0))(start)
        pl.when((hh == h - 1) & (i == nq - 1))(forward)
        qv = q_ref[0] * scale
        upper = _tri(t, lambda j, s: j > s)

        def tiles(js, carry, masks=(None, None)):
            acc, run = carry
            starts = [pl.multiple_of(j * t, t) for j in js]
            zs = [lax.dot_general(qv, k_ref[0, pl.ds(st, t), :], (((1,), (1,)), ((), ())), preferred_element_type=F32)
                  for st in starts]
            sps = [_softplus(z) for z in zs]
            sps = [sp if m is None else jnp.where(m, sp, 0.0) for sp, m in zip(sps, masks)]
            laters = [_split_dot(sp, upper, 2) for sp in sps]
            for st, z, sp, later, m in zip(starts, zs, sps, laters, masks):
                a = jnp.exp((z - sp) - (run + later))
                if m is not None:
                    a = jnp.where(m, a, 0.0)
                acc = acc + jnp.dot(a.astype(BF16), v_ref[0, pl.ds(st, t), :], preferred_element_type=F32)
                run = run + jnp.sum(sp, axis=1, keepdims=True)
            return acc, run

        def live(carry):
            return jnp.min(carry[1]) < SB_ZERO_AFTER

        below, whole = _sb_masks(t, i)
        top = jnp.maximum(i - 1, 0)
        carry = tiles([i, top], (jnp.zeros((t, hd), F32), jnp.zeros((t, 1), F32)), (below, whole))

        def pair(state):
            pp, carry = state
            j = top - 1 - 2 * pp
            return pp + 1, tiles([j, j - 1], carry)

        pairs, carry = lax.while_loop(lambda st: (st[0] < top // 2) & live(st[1]), pair, (0, carry))
        last = ((top % 2 == 1) & (pairs == top // 2) & live(carry)).astype(jnp.int32)
        acc, run = lax.fori_loop(0, last, lambda _, c: tiles([0], c), carry)
        o_ref[0] = acc.astype(BF16)
        tot_ref[0] = run
        first_ref[...] = jnp.full(first_ref.shape, top - 2 * pairs - last, jnp.int32).astype(F32)
        pl.when((hh == h - 1) & (i == nq - 1))(finish)

    full = pl.BlockSpec((1, s_len, hd), lambda hh, i: (hh, 0, 0))
    tile = pl.BlockSpec((1, t, hd), lambda hh, i: (hh, i, 0))
    hbm = pl.BlockSpec(memory_space=pl.ANY)
    return pl.pallas_call(
        body, name=name,
        out_shape=(SDS((h, s_len, hd), BF16), SDS((h, s_len, 1), F32), SDS((h, nq) + SB_FIRST_BLOCK, F32),
                   SDS((N_DEV,) + payload.shape, payload.dtype)),
        grid=(h, nq), in_specs=[tile, full, full, hbm],
        out_specs=(tile, pl.BlockSpec((1, t, 1), lambda hh, i: (hh, i, 0)),
                   pl.BlockSpec((1, 1) + SB_FIRST_BLOCK, lambda hh, i: (hh, i, 0, 0)), hbm),
        scratch_shapes=_COMM_SEMAPHORES, compiler_params=_params(has_side_effects=True),
    )(q, k, v, payload)


def _sb_bwd(q, k, v, do, tot, first, payload, name):
    h, s_len, hd = q.shape
    t = SB_TILE
    nq = s_len // t
    scale = hd ** -0.5

    def body(q_ref, k_ref, v_ref, do_ref, tot_ref, first_ref, pay_ref, dq_ref, dk_ref, dv_ref, received_ref, dk_acc, dv_acc,
             send_sems, recv_sems, local_sem):
        hh, i = pl.program_id(0), pl.program_id(1)
        start, finish = _exchange_steps(pay_ref, received_ref, send_sems, recv_sems, local_sem)
        pl.when((hh == 0) & (i == 0))(start)

        @pl.when(i == 0)
        def _():
            dk_acc[...] = jnp.zeros_like(dk_acc)
            dv_acc[...] = jnp.zeros_like(dv_acc)

        qv = q_ref[0] * scale
        dov = do_ref[0]
        total = tot_ref[0]
        upto = _tri(t, lambda j, s: j <= s)
        before = _tri(t, lambda j, s: j < s)
        nt_dims = (((1,), (1,)), ((), ()))
        tn_dims = (((0,), (0,)), ((), ()))

        def tiles(js, carry, masks=(None, None)):
            dq, run_sp, run_g = carry
            starts = [pl.multiple_of(j * t, t) for j in js]
            zs = [lax.dot_general(qv, k_ref[0, pl.ds(st, t), :], nt_dims, preferred_element_type=F32) for st in starts]
            das = [lax.dot_general(dov, v_ref[0, pl.ds(st, t), :], nt_dims, preferred_element_type=F32) for st in starts]
            sps = [_softplus(z) for z in zs]
            log_sigs = [z - sp for z, sp in zip(zs, sps)]
            sps = [sp if m is None else jnp.where(m, sp, 0.0) for sp, m in zip(sps, masks)]
            pres = [_split_dot(sp, upto, 2) for sp in sps]
            a_s, gs = [], []
            for sp, log_sig, pre, da, m in zip(sps, log_sigs, pres, das, masks):
                a = jnp.exp(log_sig - (total - (run_sp + pre)))
                if m is not None:
                    a = jnp.where(m, a, 0.0)
                a_s.append(a)
                gs.append(a * da)
                run_sp = run_sp + jnp.sum(sp, axis=1, keepdims=True)
            g_pres = [_split_dot(g, before, 3) for g in gs]
            for st, a, g, g_pre, log_sig, m in zip(starts, a_s, gs, g_pres, log_sigs, masks):
                sig = jnp.exp(log_sig)
                dz = g * (1.0 - sig) - sig * (run_g + g_pre)
                if m is not None:
                    dz = jnp.where(m, dz, 0.0)
                dz = dz.astype(BF16)
                dq = dq + jnp.dot(dz, k_ref[0, pl.ds(st, t), :], preferred_element_type=F32)
                dk_acc[pl.ds(st, t), :] += lax.dot_general(dz, qv, tn_dims, preferred_element_type=F32)
                dv_acc[pl.ds(st, t), :] += lax.dot_general(a.astype(BF16), dov, tn_dims, preferred_element_type=F32)
                run_g = run_g + jnp.sum(g, axis=1, keepdims=True)
            return dq, run_sp, run_g

        zero = jnp.zeros((t, 1), F32)
        top = jnp.maximum(i - 1, 0)
        first = jnp.clip(first_ref[0, 0, 0, 0].astype(jnp.int32), 0, top)
        count = top - first
        carry = lax.fori_loop(0, count // 2, lambda pp, c: tiles([first + 2 * pp, first + 2 * pp + 1], c),
                              (jnp.zeros((t, hd), F32), zero, zero))
        carry = lax.fori_loop(0, count % 2, lambda _, c: tiles([top - 1], c), carry)
        below, whole = _sb_masks(t, i)
        dq, _, _ = tiles([top, i], carry, (whole, below))
        dq_ref[0] = (dq * scale).astype(BF16)

        @pl.when(i == nq - 1)
        def _():
            dk_ref[0] = dk_acc[...].astype(BF16)
            dv_ref[0] = dv_acc[...].astype(BF16)

        pl.when((hh == h - 1) & (i == nq - 1))(finish)

    full = pl.BlockSpec((1, s_len, hd), lambda hh, i: (hh, 0, 0))
    tile = pl.BlockSpec((1, t, hd), lambda hh, i: (hh, i, 0))
    hbm = pl.BlockSpec(memory_space=pl.ANY)
    out = SDS((h, s_len, hd), BF16)
    return pl.pallas_call(
        body, name=name, out_shape=(out, out, out, SDS(payload.shape, payload.dtype)), grid=(h, nq),
        in_specs=[tile, full, full, tile, pl.BlockSpec((1, t, 1), lambda hh, i: (hh, i, 0)),
                  pl.BlockSpec((1, 1) + SB_FIRST_BLOCK, lambda hh, i: (hh, i, 0, 0)), hbm],
        out_specs=(tile, full, full, hbm),
        scratch_shapes=[pltpu.VMEM((s_len, hd), F32), pltpu.VMEM((s_len, hd), F32)] + _COMM_SEMAPHORES,
        compiler_params=_params(has_side_effects=True),
    )(q, k, v, do, tot, first, payload)


def _dl_scores(qv, kk, n):
    s = lax.dot_general(qv, kk, (((1,), (1,)), ((), ())), preferred_element_type=F32) * (HEAD_DIM ** -0.5)
    r = lax.broadcasted_iota(jnp.int32, s.shape, 0)
    c = lax.broadcasted_iota(jnp.int32, s.shape, 1)
    valid = (c >= r) & (c - r <= BLOCK) & ((n > 0) | (c >= BLOCK))
    return jnp.where(valid, s, NEG)


DL_UNROLL = 4
DL_PAIR = 2 * HEAD_DIM
DL_Q_BLOCK0 = 0
DL_K_BLOCK0 = DL_WIDTH // DL_PAIR
DL_V_BLOCK0 = (IN_WIDTH - DL_WIDTH) // DL_PAIR
DL_DO_BLOCK0 = (SB_WIDTH + CV_WIDTH) // DL_PAIR


def _dl_rows(idx, nb, dil):
    r, n = idx // nb, idx % nb
    cur = pl.ds(r + n * (BLOCK * dil), BLOCK, stride=dil)
    prev = pl.ds(r + jnp.maximum(n - 1, 0) * (BLOCK * dil), BLOCK, stride=dil)
    return n, cur, prev


def _dl_window(ref, cur, prev):
    return jnp.concatenate([ref[prev, :], ref[cur, :]], axis=0).astype(BF16)


def _head_lanes():
    first = lax.broadcasted_iota(jnp.int32, (BLOCK, DL_PAIR), 1) < HEAD_DIM
    return first, jnp.logical_not(first)


def _dl_fwd(qk, u, dil, name, payload=None):
    s_len = qk.shape[0]
    nb = s_len // dil // BLOCK

    def body(q_ref, k_ref, v_ref, o_ref, lse_ref):
        heads = _head_lanes()

        def step(idx, _):
            n, cur, prev = _dl_rows(idx, nb, dil)
            q = q_ref[cur, :]
            kk = _dl_window(k_ref, cur, prev)
            vv = _dl_window(v_ref, cur, prev)
            o, lse = None, None
            for lanes in heads:
                s = _dl_scores(jnp.where(lanes, q, 0.0).astype(BF16), kk, n)
                m = jnp.max(s, axis=-1, keepdims=True)
                p = jnp.exp(s - m)
                den = jnp.sum(p, axis=-1, keepdims=True)
                o_h = jnp.dot((p / den).astype(BF16), vv, preferred_element_type=F32)
                lse_h = jnp.broadcast_to(m + jnp.log(den), (BLOCK, DL_PAIR))
                o = o_h if o is None else jnp.where(heads[0], o, o_h)
                lse = lse_h if lse is None else jnp.where(heads[0], lse, lse_h)
            o_ref[cur, :] = o
            lse_ref[cur, :] = lse
            return 0

        lax.fori_loop(0, s_len // BLOCK, step, 0, unroll=DL_UNROLL)

    col = lambda first: pl.BlockSpec((s_len, DL_PAIR), lambda i: (0, first + i))
    out = SDS((s_len, DL_WIDTH), F32)
    steps = DL_WIDTH // DL_PAIR
    if payload is not None:
        step = lambda: pl.program_id(0)
        payload = ("gather", payload, lambda: (step() == 0, step() == steps - 1, step() == steps - 1))
    return _hosted_call(body, payload, name=name, out_shape=(out, out), grid=(steps,),
                        in_specs=[col(DL_Q_BLOCK0), col(DL_K_BLOCK0), col(DL_V_BLOCK0)], out_specs=(col(0), col(0)),
                        args=(qk, qk, u))


def _dl_bwd(qk, u, dmix, o_mix, wt, lse, dil, name, payload=None):
    s_len = qk.shape[0]
    nb = s_len // dil // BLOCK
    scale = HEAD_DIM ** -0.5
    nt_dims = (((1,), (1,)), ((), ()))
    tn_dims = (((0,), (0,)), ((), ()))

    def body(q_ref, k_ref, v_ref, do_ref, om_ref, wt_ref, lse_ref, dq_ref, dk_ref, dv_ref):
        dk_ref[...] = jnp.zeros_like(dk_ref)
        dv_ref[...] = jnp.zeros_like(dv_ref)
        heads = _head_lanes()

        def step(idx, _):
            n, cur, prev = _dl_rows(idx, nb, dil)
            q = q_ref[cur, :]
            kk = _dl_window(k_ref, cur, prev)
            vv = _dl_window(v_ref, cur, prev)
            dov = do_ref[cur, :]
            d_lanes = dov * om_ref[cur, :]
            w_lanes = wt_ref[cur, :]
            lse_lanes = lse_ref[cur, :]
            dq, dkk, dvv = None, None, None
            for lanes in heads:
                qm = jnp.where(lanes, q, 0.0).astype(BF16)
                s = _dl_scores(qm, kk, n)
                p = jnp.exp(s - jnp.max(jnp.where(lanes, lse_lanes, NEG), axis=-1, keepdims=True))
                w = jnp.max(jnp.where(lanes, w_lanes, 0.0), axis=-1, keepdims=True)
                d_all = jnp.sum(jnp.where(lanes, d_lanes, 0.0), axis=-1, keepdims=True)
                do_n = jnp.where(lanes, dov * w, 0.0).astype(BF16)
                dp = lax.dot_general(do_n, vv, nt_dims, preferred_element_type=F32)
                ds = (p * (dp - w * d_all) * scale).astype(BF16)
                dq_h = jnp.dot(ds, kk, preferred_element_type=F32)
                dkk_h = lax.dot_general(ds, qm, tn_dims, preferred_element_type=F32)
                dvv_h = lax.dot_general(p.astype(BF16), do_n, tn_dims, preferred_element_type=F32)
                dq = dq_h if dq is None else jnp.where(heads[0], dq, dq_h)
                dkk = dkk_h if dkk is None else dkk + dkk_h
                dvv = dvv_h if dvv is None else dvv + dvv_h
            dq_ref[cur, :] = dq
            dk_ref[prev, :] += dkk[:BLOCK]
            dv_ref[prev, :] += dvv[:BLOCK]
            dk_ref[cur, :] += dkk[BLOCK:]
            dv_ref[cur, :] += dvv[BLOCK:]
            return 0

        lax.fori_loop(0, s_len // BLOCK, step, 0, unroll=DL_UNROLL)

    col = lambda first: pl.BlockSpec((s_len, DL_PAIR), lambda i: (0, first + i))
    out = SDS((s_len, DL_WIDTH), F32)
    steps = DL_WIDTH // DL_PAIR
    if payload is not None:
        step = lambda: pl.program_id(0)
        payload = ("all_to_all", payload, lambda: (step() == 0, None, step() == steps - 1))
    return _hosted_call(
        body, payload, name=name, out_shape=(out, out, out), grid=(steps,),
        in_specs=[col(DL_Q_BLOCK0), col(DL_K_BLOCK0), col(DL_V_BLOCK0), col(DL_DO_BLOCK0), col(0), col(0), col(0)],
        out_specs=(col(0), col(0), col(0)), args=(qk, qk, u, dmix, o_mix, wt, lse))


def _dl_mix_fwd(outs, lses, name):
    rows, w = outs[0].shape
    t = 256

    def body(o1, o2, o3, l1, l2, l3, ob_ref, of_ref, w1, w2, w3):
        a, b, c = l1[...], l2[...], l3[...]
        m = jnp.maximum(jnp.maximum(a, b), c)
        ea, eb, ec = jnp.exp(a - m), jnp.exp(b - m), jnp.exp(c - m)
        den = ea + eb + ec
        wa, wb, wc = ea / den, eb / den, ec / den
        o = wa * o1[...] + wb * o2[...] + wc * o3[...]
        ob_ref[...] = o.astype(BF16)
        of_ref[...] = o
        w1[...] = wa
        w2[...] = wb
        w3[...] = wc

    row = pl.BlockSpec((t, w), lambda i: (i, 0))
    f = SDS((rows, w), F32)
    return pl.pallas_call(body, name=name, out_shape=(SDS((rows, w), BF16), f, f, f, f), grid=(rows // t,),
                          in_specs=[row] * 6, out_specs=(row,) * 5, compiler_params=_params())(*outs, *lses)


def _x_probs(qh, kh):
    s = lax.dot_general(qh, kh, (((1,), (1,)), ((), ())), preferred_element_type=F32) * (X_HEAD_DIM ** -0.5)
    e = jnp.exp(s - jnp.max(s, axis=-1, keepdims=True))
    return e / jnp.sum(e, axis=-1, keepdims=True)


def _xattn_fwd(q, k, v, name):
    rows, d = q.shape
    t = 512

    def body(q_ref, k_ref, v_ref, o_ref):
        for hh in range(X_HEADS):
            cols = slice(hh * X_HEAD_DIM, (hh + 1) * X_HEAD_DIM)
            p = _x_probs(q_ref[:, cols], k_ref[:, cols])
            o_ref[:, cols] = jnp.dot(p.astype(BF16), v_ref[:, cols], preferred_element_type=F32).astype(BF16)

    row = pl.BlockSpec((t, d), lambda i: (i, 0))
    mem = pl.BlockSpec((N_MEM, d), lambda i: (0, 0))
    return pl.pallas_call(body, name=name, out_shape=SDS((rows, d), BF16), grid=(rows // t,), in_specs=[row, mem, mem],
                          out_specs=row, compiler_params=_params())(q, k, v)


def _xattn_bwd(q, k, v, do, name):
    rows, d = q.shape
    t = 512
    scale = X_HEAD_DIM ** -0.5

    def body(q_ref, k_ref, v_ref, do_ref, dq_ref, dk_ref, dv_ref):
        @pl.when(pl.program_id(0) == 0)
        def _():
            dk_ref[...] = jnp.zeros_like(dk_ref)
            dv_ref[...] = jnp.zeros_like(dv_ref)

        for hh in range(X_HEADS):
            cols = slice(hh * X_HEAD_DIM, (hh + 1) * X_HEAD_DIM)
            qh, kh, vh, doh = q_ref[:, cols], k_ref[:, cols], v_ref[:, cols], do_ref[:, cols]
            p = _x_probs(qh, kh)
            dp = lax.dot_general(doh, vh, (((1,), (1,)), ((), ())), preferred_element_type=F32)
            ds = (p * (dp - jnp.sum(p * dp, axis=-1, keepdims=True)) * scale).astype(BF16)
            dq_ref[:, cols] = jnp.dot(ds, kh, preferred_element_type=F32).astype(BF16)
            dk_ref[:, cols] += lax.dot_general(ds, qh, (((0,), (0,)), ((), ())), preferred_element_type=F32)
            dv_ref[:, cols] += lax.dot_general(p.astype(BF16), doh, (((0,), (0,)), ((), ())), preferred_element_type=F32)

    row = pl.BlockSpec((t, d), lambda i: (i, 0))
    mem = pl.BlockSpec((N_MEM, d), lambda i: (0, 0))
    return pl.pallas_call(
        body, name=name, out_shape=(SDS((rows, d), BF16), SDS((N_MEM, d), F32), SDS((N_MEM, d), F32)), grid=(rows // t,),
        in_specs=[row, mem, mem, row], out_specs=(row, mem, mem), compiler_params=_params(),
    )(q, k, v, do)


CV_TILE = 256
CV_HALO = 32
CV_LEAD = CV_HALO - (CV_KERNEL - 1)


def _shifted(win, off, rows):
    n = win.shape[0]
    return pltpu.roll(win, (n - off) % n, axis=0)[:rows]


def _glu(val, gate):
    return val * jax.nn.sigmoid(gate)


def _ln_parts(c):
    mu = jnp.mean(c, axis=-1, keepdims=True)
    xc = c - mu
    rstd = lax.rsqrt(jnp.mean(xc * xc, axis=-1, keepdims=True) + EPS)
    return xc * rstd, rstd


def _cv_fwd(u, cv_w, cv_b, ln_g, ln_b, name):
    rows = u.shape[0]
    t, w = CV_TILE, CV_WIDTH
    val_col = 3 * SB_WIDTH // w
    ratio = t // CV_HALO

    def body(val_ref, gate_ref, pval_ref, pgate_ref, w_ref, b_ref, g_ref, beta_ref, s_ref, c_ref):
        i = pl.program_id(0)
        hist = jnp.where(i > 0, _glu(pval_ref[...], pgate_ref[...]), 0.0)
        win = jnp.concatenate([hist, _glu(val_ref[...], gate_ref[...])], axis=0)
        acc = jnp.broadcast_to(b_ref[...], (t, w))
        for kk in range(CV_KERNEL):
            acc = acc + _shifted(win, CV_LEAD + kk, t) * w_ref[kk:kk + 1, :]
        c_ref[...] = acc
        n, _ = _ln_parts(acc)
        y = n * g_ref[...] + beta_ref[...]
        s_ref[...] = (y * jax.nn.sigmoid(y)).astype(BF16)

    cur = lambda col: pl.BlockSpec((t, w), lambda i: (i, col))
    prev = lambda col: pl.BlockSpec((CV_HALO, w), lambda i: (jnp.maximum(i * ratio - 1, 0), col))
    vec = pl.BlockSpec((1, w), lambda i: (0, 0))
    return pl.pallas_call(
        body, name=name, out_shape=(SDS((rows, w), BF16), SDS((rows, w), F32)), grid=(rows // t,),
        in_specs=[cur(val_col), cur(val_col + 1), prev(val_col), prev(val_col + 1),
                  pl.BlockSpec((CV_KERNEL, w), lambda i: (0, 0)), vec, vec, vec],
        out_specs=(pl.BlockSpec((t, w), lambda i: (i, 0)),) * 2, compiler_params=_params(),
    )(u, u, u, u, cv_w, cv_b, ln_g, ln_b)


def _cv_bwd(u, c, ds, db_out, cv_w, ln_g, ln_b, name):
    rows = u.shape[0]
    t, w = CV_TILE, CV_WIDTH
    val_col = 3 * SB_WIDTH // w
    ratio = t // CV_HALO
    nt = rows // t

    def conv_out_grad(c_v, ds_v, g_v, beta_v):
        n, rstd = _ln_parts(c_v)
        y = n * g_v + beta_v
        sig = jax.nn.sigmoid(y)
        dy = ds_v * (sig * (1.0 + y * (1.0 - sig)))
        dn = dy * g_v
        dc = rstd * (dn - jnp.mean(dn, axis=-1, keepdims=True) - n * jnp.mean(dn * n, axis=-1, keepdims=True))
        return dc, dy, n

    def body(val_ref, gate_ref, pval_ref, pgate_ref, c_ref, nc_ref, ds_ref, nds_ref, dbo_ref, w_ref, g_ref, beta_ref,
             dvg_ref, dw_ref, db_ref, dg_ref, dbeta_ref, dpwb_ref):
        i = pl.program_id(0)

        @pl.when(i == 0)
        def _():
            for r in (dw_ref, db_ref, dg_ref, dbeta_ref, dpwb_ref):
                r[...] = jnp.zeros_like(r)

        g_v, beta_v = g_ref[...], beta_ref[...]
        dc, dy, n = conv_out_grad(c_ref[...], ds_ref[...], g_v, beta_v)
        dc_next, _, _ = conv_out_grad(nc_ref[...], nds_ref[...], g_v, beta_v)
        dc_next = jnp.where(i < nt - 1, dc_next, 0.0)
        dg_ref[...] += jnp.sum(dy * n, axis=0, keepdims=True)
        dbeta_ref[...] += jnp.sum(dy, axis=0, keepdims=True)
        db_ref[...] += jnp.sum(dc, axis=0, keepdims=True)
        dpwb_ref[...] += jnp.sum(dbo_ref[...], axis=0, keepdims=True)

        val, gate = val_ref[...], gate_ref[...]
        hist = jnp.where(i > 0, _glu(pval_ref[...], pgate_ref[...]), 0.0)
        win = jnp.concatenate([hist, _glu(val, gate)], axis=0)
        dc_ext = jnp.concatenate([dc, dc_next], axis=0)
        dglu = jnp.zeros((t, w), F32)
        for kk in range(CV_KERNEL):
            dw_ref[kk:kk + 1, :] += jnp.sum(dc * _shifted(win, CV_LEAD + kk, t), axis=0, keepdims=True)
            dglu = dglu + _shifted(dc_ext, CV_KERNEL - 1 - kk, t) * w_ref[kk:kk + 1, :]
        sig = jax.nn.sigmoid(gate)
        dvg_ref[:, 0:w] = (dglu * sig).astype(BF16)
        dvg_ref[:, w:2 * w] = (dglu * val * sig * (1.0 - sig)).astype(BF16)

    cur = lambda col: pl.BlockSpec((t, w), lambda i: (i, col))
    prev = lambda col: pl.BlockSpec((CV_HALO, w), lambda i: (jnp.maximum(i * ratio - 1, 0), col))
    nxt = pl.BlockSpec((CV_HALO, w), lambda i: (jnp.minimum((i + 1) * ratio, rows // CV_HALO - 1), 0))
    vec = pl.BlockSpec((1, w), lambda i: (0, 0))
    return pl.pallas_call(
        body, name=name,
        out_shape=(SDS((rows, 2 * w), BF16), SDS((CV_HALO, w), F32), SDS((1, w), F32), SDS((1, w), F32), SDS((1, w), F32),
                   SDS((1, w), F32)),
        grid=(nt,),
        in_specs=[cur(val_col), cur(val_col + 1), prev(val_col), prev(val_col + 1), cur(0), nxt, cur(0), nxt, cur(0),
                  pl.BlockSpec((CV_KERNEL, w), lambda i: (0, 0)), vec, vec],
        out_specs=(pl.BlockSpec((t, 2 * w), lambda i: (i, 0)), pl.BlockSpec((CV_HALO, w), lambda i: (0, 0)), vec, vec, vec, vec),
        compiler_params=_params(),
    )(u, u, u, u, c, c, ds, ds, db_out, cv_w, ln_g, ln_b)


FFN_TILE = 512
FFN_CHUNK = 64
FFN_COLS = 256
FFN_HALO = 8
FFN_KERNEL = 3
N_FF_BLOCKS = D_FF // FFN_COLS


def _conv3(prev8, cur, w_ref, b_ref, first):
    t = cur.shape[0]
    win = jnp.concatenate([jnp.where(first, 0.0, prev8), cur], axis=0)
    return (b_ref[...] + _shifted(win, FFN_HALO - 2, t) * w_ref[0:1, :] + _shifted(win, FFN_HALO - 1, t) * w_ref[1:2, :]
            + cur * w_ref[2:3, :])


def _gelu_gate(gate, val):
    return jax.nn.gelu(gate, approximate=True) * val


def _ffn_specs(t):
    ratio = t // FFN_HALO
    cur = pl.BlockSpec((t, FFN_COLS), lambda j, i: (i, j))
    prev = pl.BlockSpec((FFN_HALO, FFN_COLS), lambda j, i: (jnp.maximum(i * ratio - 1, 0), j))
    wsp = pl.BlockSpec((FFN_KERNEL, FFN_COLS), lambda j, i: (0, j))
    bsp = pl.BlockSpec((1, FFN_COLS), lambda j, i: (0, j))
    return cur, prev, wsp, bsp


def _ffn_host_steps(row_tiles):
    def when():
        j, i = pl.program_id(0), pl.program_id(1)
        return (j == 0) & (i == 0), (j == (3 * N_FF_BLOCKS) // 4) & (i == 0), (j == N_FF_BLOCKS - 1) & (i == row_tiles - 1)
    return when


def _ffn_act_fwd(up_g, up_v, w_g, w_v, b_g, b_v, name, payload=None):
    rows = up_g.shape[0]
    t = FFN_TILE
    cur, prev, wsp, bsp = _ffn_specs(t)

    def body(g_ref, v_ref, pg_ref, pv_ref, wg_ref, wv_ref, bg_ref, bv_ref, o_ref):
        first = pl.program_id(1) == 0
        gate = _conv3(pg_ref[...], g_ref[...], wg_ref, bg_ref, first)
        val = _conv3(pv_ref[...], v_ref[...], wv_ref, bv_ref, first)
        o_ref[...] = _gelu_gate(gate, val).astype(BF16)

    if payload is not None:
        payload = ("gather", payload, _ffn_host_steps(rows // t))
    return _hosted_call(
        body, payload, name=name, out_shape=(SDS((rows, D_FF), BF16),), grid=(N_FF_BLOCKS, rows // t),
        in_specs=[cur, cur, prev, prev, wsp, wsp, bsp, bsp], out_specs=(cur,), args=(up_g, up_v, up_g, up_v, w_g, w_v, b_g, b_v))


def _ffn_act_bwd(up_g, up_v, dact, w_g, w_v, b_g, b_v, name, payload=None):
    rows = up_g.shape[0]
    t = FFN_TILE
    ch = FFN_CHUNK
    che = ch + FFN_HALO
    ratio = t // FFN_HALO
    nt = rows // t
    cur, prev, wsp, bsp = _ffn_specs(t)
    nxt = pl.BlockSpec((FFN_HALO, FFN_COLS), lambda j, i: (jnp.minimum((i + 1) * ratio, rows // FFN_HALO - 1), j))

    def body(g_ref, v_ref, pg_ref, pv_ref, ng_ref, nv_ref, da_ref, nda_ref, wg_ref, wv_ref, bg_ref, bv_ref,
             dug_ref, duv_ref, dwg_ref, dwv_ref, dbg_ref, dbv_ref, win_g, win_v, da_win):
        i = pl.program_id(1)
        first = i == 0

        @pl.when(first)
        def _():
            for r in (dwg_ref, dwv_ref, dbg_ref, dbv_ref):
                r[...] = jnp.zeros_like(r)

        for win, pre, x, nx in ((win_g, pg_ref, g_ref, ng_ref), (win_v, pv_ref, v_ref, nv_ref)):
            win[0:FFN_HALO, :] = jnp.where(first, 0.0, pre[...])
            win[FFN_HALO:FFN_HALO + t, :] = x[...]
            win[FFN_HALO + t:, :] = nx[...]
        da_win[0:t, :] = da_ref[...]
        da_win[t:, :] = jnp.where(i < nt - 1, nda_ref[...], 0.0)
        halves = ((win_g, wg_ref, bg_ref, dug_ref), (win_v, wv_ref, bv_ref, duv_ref))

        def chunk(c, sums):
            base = pl.multiple_of(c * ch, ch)
            taps, convs = [], []
            for win, w_ref, b_ref, _ in halves:
                w = win[pl.ds(base, ch + 2 * FFN_HALO), :]
                shifted = [_shifted(w, FFN_HALO - 2 + kk, che) for kk in range(FFN_KERNEL)]
                taps.append(shifted)
                convs.append(b_ref[...] + sum(s * w_ref[kk:kk + 1, :] for kk, s in enumerate(shifted)))
            _, vjp = jax.vjp(_gelu_gate, *convs)
            new = []
            for dc_ext, shifted, (_, w_ref, _, du_ref), (dw, db) in zip(vjp(da_win[pl.ds(base, che), :]), taps, halves, sums):
                dc = dc_ext[:ch]
                du_ref[pl.ds(base, ch), :] = (dc * w_ref[2:3, :] + _shifted(dc_ext, 1, ch) * w_ref[1:2, :]
                                              + _shifted(dc_ext, 2, ch) * w_ref[0:1, :]).astype(BF16)
                dw = [dw[kk] + jnp.sum(dc * shifted[kk][:ch], axis=0, keepdims=True) for kk in range(FFN_KERNEL)]
                new.append((dw, db + jnp.sum(dc, axis=0, keepdims=True)))
            return new

        zero = jnp.zeros((1, FFN_COLS), F32)
        sums = lax.fori_loop(0, t // ch, chunk, [([zero] * FFN_KERNEL, zero)] * 2, unroll=2)
        for (dw, db), dw_ref, db_ref in zip(sums, (dwg_ref, dwv_ref), (dbg_ref, dbv_ref)):
            for kk in range(FFN_KERNEL):
                dw_ref[kk:kk + 1, :] += dw[kk]
            db_ref[...] += db

    big, wshape, bshape = SDS((rows, D_FF), BF16), SDS((FFN_KERNEL, D_FF), F32), SDS((1, D_FF), F32)
    if payload is not None:
        payload = ("all_to_all", payload, _ffn_host_steps(nt))
    window = pltpu.VMEM((t + 2 * FFN_HALO, FFN_COLS), F32)
    return _hosted_call(
        body, payload, name=name, out_shape=(big, big, wshape, wshape, bshape, bshape), grid=(N_FF_BLOCKS, nt),
        in_specs=[cur, cur, prev, prev, nxt, nxt, cur, nxt, wsp, wsp, bsp, bsp], out_specs=(cur, cur, wsp, wsp, bsp, bsp),
        scratch_shapes=(window, window, pltpu.VMEM((t + FFN_HALO, FFN_COLS), F32)),
        args=(up_g, up_v, up_g, up_v, up_g, up_v, dact, dact, w_g, w_v, b_g, b_v))


def _adamw_update(parts, w_ref, m_ref, v_ref, g_ref, d_ref, nm_ref, nv_ref):
    g = parts[0].astype(F32)
    for s in range(1, N_DEV):
        g = g + parts[s].astype(F32)
    nm = ADAM_B1 * m_ref[...] + (1.0 - ADAM_B1) * g
    nv = ADAM_B2 * v_ref[...] + (1.0 - ADAM_B2) * jnp.square(g)
    m_hat = nm / (1.0 - ADAM_B1 ** ADAM_STEP)
    v_hat = nv / (1.0 - ADAM_B2 ** ADAM_STEP)
    g_ref[...] = g
    d_ref[...] = -ADAM_LR * (m_hat / (jnp.sqrt(v_hat) + ADAM_EPS) + ADAM_WD * w_ref[...])
    nm_ref[...] = nm
    nv_ref[...] = nv


def _adamw(parts, w, m, v, name):
    rows, cols = w.shape
    t = _pick(rows, (512, 256, 128)) if rows > 512 else rows

    def body(p_ref, *refs):
        _adamw_update(p_ref[...], *refs)

    row = pl.BlockSpec((t, cols), lambda i: (i, 0))
    out = SDS((rows, cols), F32)
    return pl.pallas_call(
        body, name=name, out_shape=(out,) * 4, grid=(rows // t,),
        in_specs=[pl.BlockSpec((N_DEV, t, cols), lambda i: (0, i, 0)), row, row, row], out_specs=(row,) * 4,
        compiler_params=_params(),
    )(parts, w, m, v)


def _adamw_packed(sources, w, m, v, name):
    rows, cols = w.shape
    t = ADAMW_ROW_BLOCK
    nb = rows // DEPTH // t
    (src0, first0), (src1, first1) = sources
    assert first0 % t == 0 and first1 % t == 0 and rows % (DEPTH * t) == 0

    def body(p0_ref, p1_ref, *refs):
        layer = pl.program_id(0)
        _adamw_update(jnp.where(layer == 0, p0_ref[...], p1_ref[...]), *refs)

    spec0 = pl.BlockSpec((N_DEV, t, cols), lambda l, i: (0, first0 // t + i * (1 - l) + (nb - 1) * l, 0))
    spec1 = pl.BlockSpec((N_DEV, t, cols), lambda l, i: (0, first1 // t + i * l, 0))
    row = pl.BlockSpec((t, cols), lambda l, i: (l * nb + i, 0))
    out = SDS((rows, cols), F32)
    return pl.pallas_call(body, name=name, out_shape=(out,) * 4, grid=(DEPTH, nb), in_specs=[spec0, spec1, row, row, row],
                          out_specs=(row,) * 4, compiler_params=_params())(src0, src1, w, m, v)


_COMM_SEMAPHORES = [pltpu.SemaphoreType.DMA((N_DEV - 1,)), pltpu.SemaphoreType.DMA((N_DEV - 1,)), pltpu.SemaphoreType.DMA]


def _gather_steps(x_ref, out_ref, send_sems, recv_sems, local_sem):
    x_, y_, c_ = lax.axis_index("x"), lax.axis_index("y"), lax.axis_index("c")
    me, sibling = (x_, y_, c_), (x_, y_, 1 - c_)
    chips = [(1 - x_, y_), (x_, 1 - y_), (1 - x_, 1 - y_)]

    def slot(px, py, pc):
        return out_ref.at[4 * px + 2 * py + pc]

    def copy(kk, block, to, src=None):
        return pltpu.make_async_remote_copy(
            src_ref=slot(*block) if src is None else src, dst_ref=slot(*block),
            send_sem=send_sems.at[kk], recv_sem=recv_sems.at[kk], device_id=to, device_id_type=MESH)

    def mine():
        return pltpu.make_async_copy(x_ref, slot(*me), local_sem)

    def first():
        return [copy(0, me, sibling, src=x_ref)] + [copy(1 + j, me, (*chip, c_), src=x_ref) for j, chip in enumerate(chips)]

    def passed():
        return [copy(4 + j, (*chip, c_), sibling) for j, chip in enumerate(chips)]

    def start():
        mine().start()
        for cp in first():
            cp.start()

    def forward():
        for j, (chip, cp) in enumerate(zip(chips, passed())):
            copy(1 + j, (*chip, c_), me).wait_recv()
            cp.start()

    def finish():
        copy(0, sibling, me).wait_recv()
        for j, chip in enumerate(chips):
            copy(4 + j, (*chip, 1 - c_), me).wait_recv()
        for cp in first() + passed():
            cp.wait_send()
        mine().wait()

    return start, forward, finish


def _exchange_steps(x_ref, out_ref, send_sems, recv_sems, local_sem):
    x_, y_, c_ = lax.axis_index("x"), lax.axis_index("y"), lax.axis_index("c")
    me = 4 * x_ + 2 * y_ + c_

    def mine():
        return pltpu.make_async_copy(x_ref.at[me], out_ref.at[me], local_sem)

    def copies():
        out = []
        for r in range(1, N_DEV):
            px = 1 - x_ if r & 4 else x_
            py = 1 - y_ if r & 2 else y_
            pc = 1 - c_ if r & 1 else c_
            out.append(pltpu.make_async_remote_copy(
                src_ref=x_ref.at[4 * px + 2 * py + pc], dst_ref=out_ref.at[me],
                send_sem=send_sems.at[r - 1], recv_sem=recv_sems.at[r - 1], device_id=(px, py, pc), device_id_type=MESH))
        return out

    def start():
        mine().start()
        for cp in copies():
            cp.start()

    def finish():
        for cp in copies():
            cp.wait_recv()
        for cp in copies():
            cp.wait_send()
        mine().wait()

    return start, finish


def _hosted_call(body, exchange, *, name, out_shape, grid, in_specs, out_specs, args, scratch_shapes=()):
    if exchange is None:
        return pl.pallas_call(body, name=name, out_shape=out_shape, grid=grid, in_specs=in_specs, out_specs=out_specs,
                              scratch_shapes=list(scratch_shapes), compiler_params=_params())(*args)
    kind, payload, when = exchange
    n_in, n_out, n_scratch = len(in_specs), len(out_specs), len(scratch_shapes)
    result = SDS((N_DEV,) + payload.shape, payload.dtype) if kind == "gather" else SDS(payload.shape, payload.dtype)

    def hosting(*refs):
        ins, pay_ref = refs[:n_in], refs[n_in]
        outs, res_ref = refs[n_in + 1:n_in + 1 + n_out], refs[n_in + 1 + n_out]
        rest = refs[n_in + 2 + n_out:]
        scratch, sems = rest[:n_scratch], rest[n_scratch:]
        first, middle, last = when()
        if kind == "gather":
            start, forward, finish = _gather_steps(pay_ref, res_ref, *sems)
            pl.when(first)(start)
            pl.when(middle)(forward)
        else:
            start, finish = _exchange_steps(pay_ref, res_ref, *sems)
            pl.when(first)(start)
        body(*ins, *outs, *scratch)
        pl.when(last)(finish)

    hbm = pl.BlockSpec(memory_space=pl.ANY)
    return pl.pallas_call(
        hosting, name=name, out_shape=tuple(out_shape) + (result,), grid=grid, in_specs=list(in_specs) + [hbm],
        out_specs=tuple(out_specs) + (hbm,), scratch_shapes=list(scratch_shapes) + _COMM_SEMAPHORES,
        compiler_params=_params(has_side_effects=True),
    )(*args, payload)


def _all_gather(x, name):
    def body(x_ref, out_ref, send_sems, recv_sems, local_sem):
        for step in _gather_steps(x_ref, out_ref, send_sems, recv_sems, local_sem):
            step()

    hbm = pl.BlockSpec(memory_space=pl.ANY)
    return pl.pallas_call(body, name=name, out_shape=SDS((N_DEV,) + x.shape, x.dtype), in_specs=[hbm], out_specs=hbm,
                          scratch_shapes=_COMM_SEMAPHORES, compiler_params=pltpu.CompilerParams(has_side_effects=True))(x)


def _all_to_all(x, name):
    def body(x_ref, out_ref, send_sems, recv_sems, local_sem):
        for step in _exchange_steps(x_ref, out_ref, send_sems, recv_sems, local_sem):
            step()

    hbm = pl.BlockSpec(memory_space=pl.ANY)
    return pl.pallas_call(body, name=name, out_shape=SDS(x.shape, x.dtype), in_specs=[hbm], out_specs=hbm,
                          scratch_shapes=_COMM_SEMAPHORES, compiler_params=pltpu.CompilerParams(has_side_effects=True))(x)


BIG = ("w_in", "cv_pw_w", "w_out", "x_wq", "x_wk", "x_wv", "x_wo", "ffn_w_up", "ffn_w_down")
_MIXER = (("w_in", 352), ("w_out", 128))
_CROSS = (("x_wq", 128), ("x_wk", 128), ("x_wv", 128), ("x_wo", 128))
GROUPS = {
    "a": tuple((n, 0, r) for n, r in _MIXER),
    "m1": tuple((n, 1, r) for n, r in _MIXER),
    "bx": tuple((n, 0, r) for n, r in _CROSS),
    "cx": tuple((n, 1, r) for n, r in _CROSS),
    "bf": (("ffn_w_down", 0, 352),),
    "cf": (("ffn_w_down", 1, 352),),
}
GRADIENT_GROUPS = ("a", "m1", "bf", "bx", "cf", "cx")
TRANSPOSED = ("w_in", "ffn_w_up")
PW_ROWS = 16
ADAMW_ROW_BLOCK = 32


def _group_rows(group):
    out, first = {}, 0
    for n, l, r in GROUPS[group]:
        out[(n, l)] = (first, r)
        first += r
    return out


def _where_is(name, layer):
    for group in GRADIENT_GROUPS:
        rows = _group_rows(group)
        if (name, layer) in rows:
            return (group,) + rows[(name, layer)]
    raise KeyError((name, layer))


def _pack_weights(group, wts):
    pieces = []
    for n, l, _ in GROUPS[group]:
        w = wts[n][l].astype(BF16)
        pieces.append(w.T if n in TRANSPOSED else w)
    if group == "a":
        pieces.append(wts["cv_pw_w"].astype(BF16).reshape(PW_ROWS, PAYLOAD_COLS))
    return jnp.concatenate(pieces, axis=0)


def _unpack_weights(group, gathered):
    return {key: gathered[:, first:first + r, :].reshape(N_DEV * r, PAYLOAD_COLS)
            for key, (first, r) in _group_rows(group).items()}


def _pack_grads(group, grads):
    pieces = []
    for n, l, r in GROUPS[group]:
        g = grads[n][l]
        parts = g if isinstance(g, tuple) else (g,)
        pieces.append(jnp.concatenate([p.reshape(-1, r, PAYLOAD_COLS) for p in parts], axis=0))
    if group == "a":
        pieces.append(_to_shards("cv_pw_w", jnp.stack(grads["cv_pw_w"])).reshape(N_DEV, PW_ROWS, PAYLOAD_COLS))
    return jnp.concatenate(pieces, axis=1)


COL_SHARDED = ("w_in", "ffn_w_up", "cv_w", "ffn_conv_w")
SMALL_SHARDED = ("cv_w", "ffn_conv_w")
REPLICATED = ("mix_norm_pre", "cv_b", "cv_ln_g", "cv_ln_b", "cv_pw_b", "mix_norm_post", "x_norm_pre", "mem_norm",
              "x_norm_post", "ffn_norm_pre", "ffn_conv_b", "ffn_norm_post")
WEIGHTS = ("mix_norm_pre", "w_in", "cv_w", "cv_b", "cv_ln_g", "cv_ln_b", "cv_pw_w", "cv_pw_b", "w_out", "mix_norm_post",
           "x_norm_pre", "mem_norm", "x_wq", "x_wk", "x_wv", "x_wo", "x_norm_post", "ffn_norm_pre", "ffn_w_up",
           "ffn_conv_w", "ffn_conv_b", "ffn_w_down", "ffn_norm_post")
PAYLOAD_COLS = 1024


PAYLOAD_ROW_TILE = 16


def _pad_rows(flat, cols):
    n = flat.shape[-1]
    rows = -(-n // (cols * PAYLOAD_ROW_TILE)) * PAYLOAD_ROW_TILE
    pad = rows * cols - n
    if pad:
        flat = jnp.concatenate([flat, jnp.zeros(flat.shape[:-1] + (pad,), flat.dtype)], axis=-1)
    return flat.reshape(flat.shape[:-1] + (rows, cols))


def _unshard(name, parts):
    n, depth, r, c = parts.shape
    if name in COL_SHARDED:
        return parts.transpose(1, 2, 0, 3).reshape(depth, r, n * c)
    return parts.transpose(1, 0, 2, 3).reshape(depth, n * r, c)


def _to_shards(name, full):
    depth, r, c = full.shape
    if name in COL_SHARDED:
        return full.reshape(depth, r, N_DEV, c // N_DEV).transpose(2, 0, 1, 3).reshape(N_DEV, -1)
    return full.reshape(depth, N_DEV, r // N_DEV, c).transpose(1, 0, 2, 3).reshape(N_DEV, -1)


def _heads_major(x, h):
    return x.reshape(x.shape[0], h, HEAD_DIM).transpose(1, 0, 2)


def _tokens_major(x):
    return x.transpose(1, 0, 2).reshape(x.shape[1], -1)


def _ffn_halves(p):
    w, b = p["ffn_conv_w"], p["ffn_conv_b"]
    return w[:, :D_FF], w[:, D_FF:], b[:, :D_FF], b[:, D_FF:]


def _layer_fwd(l, h, hn, p, mem, cos, sin, g_next, payload, unpack, ffn_shards, ffn_payload):
    p = dict(p)
    sv = {"h0": h, "hn0": hn}
    u = _mm(hn, p["w_in"], "nt", F32, f"l{l}_in_proj")
    sv["u"] = u
    sb = _heads_major(u[:, :3 * SB_WIDTH].astype(BF16), 3 * SB_HEADS)
    sb_q, sb_k, sb_v = sb[:SB_HEADS], sb[SB_HEADS:2 * SB_HEADS], sb[2 * SB_HEADS:]
    a_out, sb_tot, sb_first, gathered = _sb_fwd(sb_q, sb_k, sb_v, payload, f"l{l}_sb_fwd")
    p.update(unpack(gathered))
    sv.update(sb_q=sb_q, sb_k=sb_k, sb_v=sb_v, sb_tot=sb_tot, sb_first=sb_first, p=p)

    cv_s, cv_c = _cv_fwd(u, p["cv_w"], p["cv_b"], p["cv_ln_g"], p["cv_ln_b"], f"l{l}_cv_fwd")
    b_out = _mm(cv_s, p["cv_pw_w"], "nn", BF16, f"l{l}_cv_pw", bias=p["cv_pw_b"])
    sv.update(cv_s=cv_s, cv_c=cv_c)

    qk = _rope_fwd(u, cos, sin, f"l{l}_rope_fwd")
    up_t = ffn_shards[0].astype(BF16).T
    half_rows = up_t.shape[0] // 2
    carried = (up_t[:half_rows], up_t[half_rows:], ffn_shards[1].astype(BF16))
    outs, lses, got = [], [], []
    for b, (_, dil) in enumerate(DL_PATTERN):
        o, lse, gathered = _dl_fwd(qk, u, dil, f"l{l}_dl{b}_fwd", carried[b])
        outs.append(o)
        lses.append(lse)
        got.append(gathered)
    up_blocks = jnp.concatenate(got[:2], axis=1)
    half = N_DEV // 2
    p["ffn_w_up"] = (up_blocks[:half].reshape(-1, PAYLOAD_COLS), up_blocks[half:].reshape(-1, PAYLOAD_COLS))
    p["ffn_w_down"] = got[2].reshape(-1, PAYLOAD_COLS)
    c_out, c_out_f32, w1, w2, w3 = _dl_mix_fwd(outs, lses, f"l{l}_dl_mix")
    sv.update(dl_qk=qk, dl_lse=lses, dl_o=c_out_f32, dl_w=(w1, w2, w3))

    mix = jnp.concatenate([_tokens_major(a_out), b_out, c_out], axis=-1)
    y = _mm(mix, p["w_out"], "nn", F32, f"l{l}_out_proj")
    h1, hn1 = _res_norm_fwd(h, y, p["mix_norm_post"], p["x_norm_pre"], f"l{l}_mix_post")
    sv.update(mix=mix, y_mix=y, h1=h1, hn1=hn1)

    xq = _mm(hn1, p["x_wq"], "nn", BF16, f"l{l}_xq")
    memn = _rms_fwd(mem, p["mem_norm"], f"l{l}_mem_norm")
    xk = _mm(memn, p["x_wk"], "nn", BF16, f"l{l}_xk")
    xv = _mm(memn, p["x_wv"], "nn", BF16, f"l{l}_xv")
    xo = _xattn_fwd(xq, xk, xv, f"l{l}_xattn_fwd")
    y = _mm(xo, p["x_wo"], "nn", F32, f"l{l}_xo_proj")
    h2, hn2 = _res_norm_fwd(h1, y, p["x_norm_post"], p["ffn_norm_pre"], f"l{l}_x_post")
    sv.update(xq=xq, xk=xk, xv=xv, xo=xo, memn=memn, y_x=y, h2=h2, hn2=hn2)

    up_g = _mm(hn2, p["ffn_w_up"][0], "nt", F32, f"l{l}_ffn_up_gate")
    up_v = _mm(hn2, p["ffn_w_up"][1], "nt", F32, f"l{l}_ffn_up_val")
    act, *ffn_gathered = _ffn_act_fwd(up_g, up_v, *_ffn_halves(p), f"l{l}_ffn_act", ffn_payload)
    y = _mm(act, p["ffn_w_down"], "nn", F32, f"l{l}_ffn_down")
    h3, hn3 = _res_norm_fwd(h2, y, p["ffn_norm_post"], g_next, f"l{l}_ffn_post")
    sv.update(up_g=up_g, up_v=up_v, act=act, y_ffn=y)
    return h3, hn3, sv, (ffn_gathered[0] if ffn_gathered else None)


def _layer_bwd(l, dh, dy, sv, mem, cos, sin, prev_post, ffn_payload, pack):
    p = sv["p"]
    gr = {}
    received = {}
    dact = _mm(dy, p["ffn_w_down"], "nt", F32, f"l{l}_d_act")
    gr["ffn_w_down"] = _mm(sv["act"], dy, "tn", BF16, f"l{l}_dw_down")
    dup_g, dup_v, dwg, dwv, dbg, dbv, *got = _ffn_act_bwd(sv["up_g"], sv["up_v"], dact, *_ffn_halves(p), f"l{l}_ffn_act_bwd",
                                                         ffn_payload)
    if got:
        received["ffn_payload"] = got[0]
    gr["ffn_conv_w"] = jnp.concatenate([dwg, dwv], axis=1)
    gr["ffn_conv_b"] = jnp.concatenate([dbg, dbv], axis=1)
    dhn = (_mm(dup_g, p["ffn_w_up"][0], "nn", F32, f"l{l}_d_hn2_gate"), _mm(dup_v, p["ffn_w_up"][1], "nn", F32, f"l{l}_d_hn2_val"))
    gr["ffn_w_up"] = (_mm(dup_g, sv["hn2"], "tn", BF16, f"l{l}_dw_up_gate"), _mm(dup_v, sv["hn2"], "tn", BF16, f"l{l}_dw_up_val"))
    dh, dy, gr["ffn_norm_pre"], gr["x_norm_post"] = _norm_bwd(
        dh, (sv["h2"], p["ffn_norm_pre"], dhn), (sv["y_x"], p["x_norm_post"]), f"l{l}_x_post_bwd")

    do = _mm(dy, p["x_wo"], "nt", BF16, f"l{l}_d_xo")
    gr["x_wo"] = _mm(sv["xo"], dy, "tn", BF16, f"l{l}_dw_xo")
    dq, dk, dv = _xattn_bwd(sv["xq"], sv["xk"], sv["xv"], do, f"l{l}_xattn_bwd")
    dhn = _mm(dq, p["x_wq"], "nt", F32, f"l{l}_d_hn1")
    gr["x_wq"] = _mm(sv["hn1"], dq, "tn", BF16, f"l{l}_dw_xq")
    gr["x_wk"] = _mm(sv["memn"], dk, "tn", BF16, f"l{l}_dw_xk")
    gr["x_wv"] = _mm(sv["memn"], dv, "tn", BF16, f"l{l}_dw_xv")
    dmemn = _mm(dk, p["x_wk"], "nt", F32, f"l{l}_d_memn_k") + _mm(dv, p["x_wv"], "nt", F32, f"l{l}_d_memn_v")
    gr["mem_norm"] = _rms_gain_grad(mem, p["mem_norm"], dmemn, f"l{l}_mem_norm_bwd")
    dh, dy, gr["x_norm_pre"], gr["mix_norm_post"] = _norm_bwd(
        dh, (sv["h1"], p["x_norm_pre"], dhn), (sv["y_mix"], p["mix_norm_post"]), f"l{l}_mix_post_bwd")

    dmix = _mm(dy, p["w_out"], "nt", F32, f"l{l}_d_mix")
    gr["w_out"] = _mm(sv["mix"], dy, "tn", BF16, f"l{l}_dw_out")
    do_a = _heads_major(dmix[:, :SB_WIDTH].astype(BF16), SB_HEADS)
    dq, dk, dv, received["ffn"] = _sb_bwd(sv["sb_q"], sv["sb_k"], sv["sb_v"], do_a, sv["sb_tot"], sv["sb_first"],
                                          pack("ffn", gr), f"l{l}_sb_bwd")
    du_sb = _tokens_major(jnp.concatenate([dq, dk, dv], axis=0))

    db_out = dmix[:, SB_WIDTH:SB_WIDTH + CV_WIDTH]
    ds = _mm(db_out, p["cv_pw_w"], "nt", F32, f"l{l}_d_cv_s")
    gr["cv_pw_w"] = _mm(sv["cv_s"], db_out, "tn", BF16, f"l{l}_dw_cv_pw")
    du_cv, dcvw, gr["cv_b"], gr["cv_ln_g"], gr["cv_ln_b"], gr["cv_pw_b"] = _cv_bwd(
        sv["u"], sv["cv_c"], ds, db_out, p["cv_w"], p["cv_ln_g"], p["cv_ln_b"], f"l{l}_cv_bwd")
    gr["cv_w"] = dcvw[:CV_KERNEL]

    dqs, dks, dvs = [], [], []
    up_rows = jnp.concatenate([g.reshape(N_DEV // 2, -1, PAYLOAD_COLS) for g in gr["ffn_w_up"]], axis=0)
    half_rows = up_rows.shape[1] // 2
    carried = {"up0": up_rows[:, :half_rows], "up1": up_rows[:, half_rows:], "cross": pack("cross", gr)}
    for b, ((_, dil), what) in enumerate(zip(DL_PATTERN, carried)):
        dq, dk, dv, received[what] = _dl_bwd(sv["dl_qk"], sv["u"], dmix, sv["dl_o"], sv["dl_w"][b], sv["dl_lse"][b], dil,
                                             f"l{l}_dl{b}_bwd", carried[what])
        dqs.append(dq)
        dks.append(dk)
        dvs.append(dv)
    du_dl = _rope_bwd(dqs, dks, dvs, cos, sin, f"l{l}_rope_bwd")

    du = jnp.concatenate([du_sb, du_cv, du_dl], axis=-1)
    dhn = _mm(du, p["w_in"], "nn", F32, f"l{l}_d_hn0")
    gr["w_in"] = _mm(du, sv["hn0"], "tn", BF16, f"l{l}_dw_in")
    dh, dy, gr["mix_norm_pre"], dg_prev = _norm_bwd(dh, (sv["h0"], p["mix_norm_pre"], dhn), prev_post, f"l{l}_in_bwd")
    return dh, dy, gr, dg_prev, received


def kernel(x, mem, positions, mix_norm_pre, w_in, cv_w, cv_b, cv_ln_g, cv_ln_b, cv_pw_w, cv_pw_b, w_out, mix_norm_post, x_norm_pre, mem_norm, x_wq, x_wk, x_wv, x_wo, x_norm_post, ffn_norm_pre, ffn_w_up, ffn_conv_w, ffn_conv_b, ffn_w_down, ffn_norm_post, loss_target, m_mix_norm_pre, m_w_in, m_cv_w, m_cv_b, m_cv_ln_g, m_cv_ln_b, m_cv_pw_w, m_cv_pw_b, m_w_out, m_mix_norm_post, m_x_norm_pre, m_mem_norm, m_x_wq, m_x_wk, m_x_wv, m_x_wo, m_x_norm_post, m_ffn_norm_pre, m_ffn_w_up, m_ffn_conv_w, m_ffn_conv_b, m_ffn_w_down, m_ffn_norm_post, v_mix_norm_pre, v_w_in, v_cv_w, v_cv_b, v_cv_ln_g, v_cv_ln_b, v_cv_pw_w, v_cv_pw_b, v_w_out, v_mix_norm_post, v_x_norm_pre, v_mem_norm, v_x_wq, v_x_wk, v_x_wv, v_x_wo, v_x_norm_post, v_ffn_norm_pre, v_ffn_w_up, v_ffn_conv_w, v_ffn_conv_b, v_ffn_w_down, v_ffn_norm_post):
    args = locals()
    wts = {n: args[n] for n in WEIGHTS}
    mom = {n: args["m_" + n] for n in WEIGHTS}
    var = {n: args["v_" + n] for n in WEIGHTS}

    x2, mem2, target = x[0], mem[0], loss_target[0]

    gathered_a = _all_gather(_pack_weights("a", wts), "weights_all_gather")
    small_payload = _pad_rows(jnp.concatenate([wts[n].reshape(-1) for n in SMALL_SHARDED]), PAYLOAD_COLS)
    small = _all_gather(small_payload, "small_weights_all_gather").reshape(N_DEV, -1)
    small_full = {}
    off = 0
    for n in SMALL_SHARDED:
        size = wts[n].size
        small_full[n] = _unshard(n, small[:, off:off + size].reshape((N_DEV,) + wts[n].shape))
        off += size
    pw_first = sum(r for _, _, r in GROUPS["a"])
    pw_full = _unshard("cv_pw_w", gathered_a[:, pw_first:, :].reshape((N_DEV,) + wts["cv_pw_w"].shape))

    def mixer_params(l, unpacked):
        p = {n: wts[n][l][None, :] for n in REPLICATED}
        p.update({n: small_full[n][l] for n in SMALL_SHARDED})
        p.update(cv_pw_w=pw_full[l], w_in=unpacked[("w_in", l)], w_out=unpacked[("w_out", l)])
        return p

    def of_layer(group, l):
        return lambda gathered: {n: w for (n, ll), w in _unpack_weights(group, gathered).items() if ll == l}

    pos = positions[0].astype(F32)
    half = HEAD_DIM // 2
    inv_freq = ROPE_THETA ** (-jnp.arange(half, dtype=F32) / half)
    ang = pos[:, None] * inv_freq
    cos = jnp.tile(jnp.cos(ang), (1, LANES // half))
    sin = jnp.tile(jnp.sin(ang), (1, LANES // half))

    p0 = mixer_params(0, _unpack_weights("a", gathered_a))
    hn = _rms_fwd(x2, p0["mix_norm_pre"], "l0_in_norm")
    ffn_shards = [(wts["ffn_w_up"][l], wts["ffn_w_down"][l]) for l in range(DEPTH)]
    h, hn, sv0, gathered_m1 = _layer_fwd(0, x2, hn, p0, mem2, cos, sin, wts["mix_norm_pre"][1][None, :],
                                         _pack_weights("bx", wts), of_layer("bx", 0), ffn_shards[0], _pack_weights("m1", wts))
    p1 = mixer_params(1, _unpack_weights("m1", gathered_m1))
    h, _, sv1, _ = _layer_fwd(1, h, hn, p1, mem2, cos, sin, None, _pack_weights("cx", wts), of_layer("cx", 1), ffn_shards[1],
                              None)
    loss_part, dh = _loss_fwd(h, target, "loss")
    loss = lax.psum(loss_part[0, 0], ("x", "y", "c"))

    grads = {n: [None] * DEPTH for n in WEIGHTS}
    dh, dy, _, grads["ffn_norm_post"][1] = _norm_bwd(dh, None, (sv1["y_ffn"], sv1["p"]["ffn_norm_post"]), "last_post_bwd")

    def packer(l, groups):
        return lambda which, gr: _pack_grads(groups[which], {n: {l: g} for n, g in gr.items()})

    dh, dy, gr, grads["ffn_norm_post"][0], got1 = _layer_bwd(
        1, dh, dy, sv1, mem2, cos, sin, (sv0["y_ffn"], sv0["p"]["ffn_norm_post"]), None, packer(1, {"ffn": "cf", "cross": "cx"}))
    for n, g in gr.items():
        grads[n][1] = g
    dh, _, gr, _, got0 = _layer_bwd(0, dh, dy, sv0, mem2, cos, sin, None, _pack_grads("m1", grads),
                                    packer(0, {"ffn": "bf", "cross": "bx"}))
    for n, g in gr.items():
        grads[n][0] = g
    grad_x = dh

    received_a = _all_to_all(_pack_grads("a", grads), "grads_all_to_all")
    received = {"a": received_a, "m1": got0["ffn_payload"], "bf": got0["ffn"], "bx": got0["cross"],
                "cf": got1["ffn"], "cx": got1["cross"]}
    small_rows = jnp.concatenate([_to_shards(n, jnp.stack(grads[n])) for n in SMALL_SHARDED], axis=1)
    rep_flat = jnp.concatenate([jnp.stack([g.reshape(-1) for g in grads[n]]).reshape(-1) for n in REPLICATED])
    rep_rows = jnp.broadcast_to(rep_flat[None], (N_DEV, rep_flat.shape[0]))
    f32_rows = _pad_rows(jnp.concatenate([small_rows, rep_rows], axis=1), PAYLOAD_COLS)
    small_parts = _all_to_all(f32_rows, "small_grads_all_to_all")

    res = {}
    for n in BIG:
        shape = wts[n].shape
        two_d = (shape[0] * shape[1], shape[2])
        operands = (wts[n].reshape(two_d), mom[n].reshape(two_d), var[n].reshape(two_d))
        if n == "cv_pw_w":
            outs = _adamw(received_a[:, pw_first:, :].reshape((N_DEV,) + two_d), *operands, f"adamw_{n}")
        elif n in TRANSPOSED:
            layers = []
            for l, got in enumerate((got0, got1)):
                if n == "ffn_w_up":
                    layers.append(jnp.concatenate([got["up0"], got["up1"]], axis=1))
                else:
                    group, first, r = _where_is(n, l)
                    layers.append(received[group][:, first:first + r, :])
            parts = jnp.stack(layers, axis=1).transpose(0, 1, 3, 2).reshape((N_DEV,) + two_d)
            outs = _adamw(parts, *operands, f"adamw_{n}")
        else:
            sources = []
            for l in range(DEPTH):
                group, first, _ = _where_is(n, l)
                sources.append((received[group], first))
            outs = _adamw_packed(sources, *operands, f"adamw_{n}")
        res[n] = [o.reshape(shape) for o in outs]
    small_names = SMALL_SHARDED + REPLICATED
    flat_w = _pad_rows(jnp.concatenate([wts[n].reshape(-1) for n in small_names]), PAYLOAD_COLS)
    flat_m = _pad_rows(jnp.concatenate([mom[n].reshape(-1) for n in small_names]), PAYLOAD_COLS)
    flat_v = _pad_rows(jnp.concatenate([var[n].reshape(-1) for n in small_names]), PAYLOAD_COLS)
    outs = _adamw(small_parts, flat_w, flat_m, flat_v, "adamw_small")
    outs = [o.reshape(-1) for o in outs]
    off = 0
    for n in small_names:
        size = wts[n].size
        res[n] = [o[off:off + size].reshape(wts[n].shape) for o in outs]
        off += size

    result = [loss, grad_x[None]]
    for kind in range(4):
        result += [res[n][kind] for n in WEIGHTS]
    return tuple(result)
```

```python
import functools
import math

import jax
import jax.numpy as jnp
from jax import lax
from jax.experimental import pallas as pl
from jax.experimental.pallas import tpu as pltpu

F32, BF16 = jnp.float32, jnp.bfloat16
SDS = jax.ShapeDtypeStruct

D_MODEL = 1024
SEQ = 4096
DEPTH = 2
HEAD_DIM = 64
SB_HEADS = 4
SB_WIDTH = 256
CV_WIDTH = 256
CV_KERNEL = 31
DL_HEADS = 8
DL_WIDTH = 512
IN_WIDTH = 2816
DL_PATTERN = ((128, 1), (512, 4), (2048, 16))
BLOCK = 128
ROPE_THETA = 10000.0
N_MEM = 256
X_HEADS = 4
X_HEAD_DIM = 256
D_FF = 2816
EPS = 1e-6
N_DEV = 8
LANES = 128

ADAM_LR = 0.001
ADAM_B1 = 0.9
ADAM_B2 = 0.999
ADAM_EPS = 1e-08
ADAM_WD = 0.01
ADAM_STEP = 10

VMEM_LIMIT_BYTES = 56 * 1024 * 1024
MESH = pl.DeviceIdType.MESH
NEG = -1e30


def _params(**kw):
    return pltpu.CompilerParams(vmem_limit_bytes=VMEM_LIMIT_BYTES, **kw)


def _pick(n, cands):
    for c in cands:
        if n % c == 0:
            return c
    return n


def _mm(a, b, mode, out_dtype, name, bias=None):
    if mode == "nn":
        (m, k), (k2, n) = a.shape, b.shape
    elif mode == "nt":
        (m, k), (n, k2) = a.shape, b.shape
    else:
        (k, m), (k2, n) = a.shape, b.shape
    assert k == k2, (a.shape, b.shape, mode)
    tm = _pick(m, (1024, 1408, 512, 256, 128))
    tn = _pick(n, (1024, 1408, 512, 256, 128))
    tk = k if k <= 2048 else _pick(k, (2048, 1408, 1024, 512))
    nk = k // tk
    dims = {"nn": ((1,), (0,)), "nt": ((1,), (1,)), "tn": ((0,), (0,))}[mode]

    def body(*refs):
        refs = list(refs)
        acc_ref = refs.pop() if nk > 1 else None
        a_ref, b_ref = refs[0], refs[1]
        bias_ref = refs[2] if bias is not None else None
        o_ref = refs[-1]
        p = lax.dot_general(a_ref[...].astype(BF16), b_ref[...].astype(BF16), (dims, ((), ())),
                            preferred_element_type=F32)

        def finish(v):
            if bias_ref is not None:
                v = v + bias_ref[...]
            o_ref[...] = v.astype(out_dtype)

        if nk == 1:
            finish(p)
        else:
            kk = pl.program_id(2)

            @pl.when(kk == 0)
            def _():
                acc_ref[...] = p

            @pl.when(kk > 0)
            def _():
                acc_ref[...] += p

            @pl.when(kk == nk - 1)
            def _():
                finish(acc_ref[...])

    a_spec = pl.BlockSpec((tk, tm), lambda i, j, kk: (kk, i)) if mode == "tn" else pl.BlockSpec((tm, tk), lambda i, j, kk: (i, kk))
    b_spec = pl.BlockSpec((tn, tk), lambda i, j, kk: (j, kk)) if mode == "nt" else pl.BlockSpec((tk, tn), lambda i, j, kk: (kk, j))
    in_specs = [a_spec, b_spec]
    args = [a, b]
    if bias is not None:
        in_specs.append(pl.BlockSpec((1, tn), lambda i, j, kk: (0, j)))
        args.append(bias)
    return pl.pallas_call(
        body, name=name, out_shape=SDS((m, n), out_dtype), grid=(m // tm, n // tn, nk),
        in_specs=in_specs, out_specs=pl.BlockSpec((tm, tn), lambda i, j, kk: (i, j)),
        scratch_shapes=[pltpu.VMEM((tm, tn), F32)] if nk > 1 else [], compiler_params=_params(),
    )(*args)


def _rms(x, g):
    r = lax.rsqrt(jnp.mean(x * x, axis=-1, keepdims=True) + EPS)
    return x * r * g


def _rms_bwd(x, g, dy):
    r = lax.rsqrt(jnp.mean(x * x, axis=-1, keepdims=True) + EPS)
    xh = x * r
    dyg = dy * g
    dx = r * (dyg - xh * jnp.mean(dyg * xh, axis=-1, keepdims=True))
    return dx, dy * xh


def _rms_fwd(x, g, name):
    rows, d = x.shape
    t = min(rows, 512)

    def body(x_ref, g_ref, o_ref):
        o_ref[...] = _rms(x_ref[...], g_ref[...]).astype(BF16)

    return pl.pallas_call(
        body, name=name, out_shape=SDS((rows, d), BF16), grid=(rows // t,),
        in_specs=[pl.BlockSpec((t, d), lambda i: (i, 0)), pl.BlockSpec((1, d), lambda i: (0, 0))],
        out_specs=pl.BlockSpec((t, d), lambda i: (i, 0)), compiler_params=_params(),
    )(x, g)


def _res_norm_fwd(h, y, g_post, g_next, name):
    rows, d = h.shape
    t = 512
    has_next = g_next is not None

    def body(*refs):
        if has_next:
            h_ref, y_ref, gp_ref, gn_ref, h1_ref, hn_ref = refs
        else:
            h_ref, y_ref, gp_ref, h1_ref = refs
        h1 = h_ref[...] + _rms(y_ref[...], gp_ref[...])
        h1_ref[...] = h1
        if has_next:
            hn_ref[...] = _rms(h1, gn_ref[...]).astype(BF16)

    row = pl.BlockSpec((t, d), lambda i: (i, 0))
    vec = pl.BlockSpec((1, d), lambda i: (0, 0))
    in_specs = [row, row, vec] + ([vec] if has_next else [])
    args = [h, y, g_post] + ([g_next] if has_next else [])
    out_shape = [SDS((rows, d), F32)] + ([SDS((rows, d), BF16)] if has_next else [])
    out_specs = [row] + ([row] if has_next else [])
    res = pl.pallas_call(body, name=name, out_shape=out_shape, grid=(rows // t,), in_specs=in_specs,
                         out_specs=out_specs, compiler_params=_params())(*args)
    return (res[0], res[1]) if has_next else (res[0], None)


def _norm_bwd(dh, pre, post, name):
    rows, d = dh.shape
    t = 512
    has_pre, has_post = pre is not None, post is not None
    if has_pre:
        dhns = pre[2] if isinstance(pre[2], tuple) else (pre[2],)
        pre = (pre[0], pre[1]) + dhns

    def body(*refs):
        refs = list(refs)
        dh_ref = refs.pop(0)
        if has_pre:
            h_ref, gpre_ref = refs.pop(0), refs.pop(0)
            dhn_refs = [refs.pop(0) for _ in dhns]
        if has_post:
            y_ref, gpost_ref = refs.pop(0), refs.pop(0)
        dht_ref = refs.pop(0)
        if has_post:
            dy_ref = refs.pop(0)
        if has_pre:
            dgpre_ref = refs.pop(0)
        if has_post:
            dgpost_ref = refs.pop(0)
        i = pl.program_id(0)
        dht = dh_ref[...]
        if has_pre:
            dhn = dhn_refs[0][...]
            for r in dhn_refs[1:]:
                dhn = dhn + r[...]
            dx, dgr = _rms_bwd(h_ref[...], gpre_ref[...], dhn)
            dht = dht + dx

            @pl.when(i == 0)
            def _():
                dgpre_ref[...] = jnp.zeros_like(dgpre_ref)

            dgpre_ref[...] += jnp.sum(dgr, axis=0, keepdims=True)
        dht_ref[...] = dht
        if has_post:
            dy, dgr = _rms_bwd(y_ref[...], gpost_ref[...], dht)
            dy_ref[...] = dy.astype(BF16)

            @pl.when(i == 0)
            def _():
                dgpost_ref[...] = jnp.zeros_like(dgpost_ref)

            dgpost_ref[...] += jnp.sum(dgr, axis=0, keepdims=True)

    row = pl.BlockSpec((t, d), lambda i: (i, 0))
    vec = pl.BlockSpec((1, d), lambda i: (0, 0))
    in_specs, args = [row], [dh]
    if has_pre:
        in_specs += [row, vec] + [row] * len(dhns)
        args += list(pre)
    if has_post:
        in_specs += [row, vec]
        args += list(post)
    out_shape, out_specs = [SDS((rows, d), F32)], [row]
    if has_post:
        out_shape.append(SDS((rows, d), BF16))
        out_specs.append(row)
    if has_pre:
        out_shape.append(SDS((1, d), F32))
        out_specs.append(vec)
    if has_post:
        out_shape.append(SDS((1, d), F32))
        out_specs.append(vec)
    res = list(pl.pallas_call(body, name=name, out_shape=out_shape, grid=(rows // t,), in_specs=in_specs,
                              out_specs=out_specs, compiler_params=_params())(*args))
    dht = res.pop(0)
    dy = res.pop(0) if has_post else None
    dgpre = res.pop(0) if has_pre else None
    dgpost = res.pop(0) if has_post else None
    return dht, dy, dgpre, dgpost


def _rms_gain_grad(x, g, dy, name):
    rows, d = x.shape

    def body(x_ref, g_ref, dy_ref, dg_ref):
        _, dgr = _rms_bwd(x_ref[...], g_ref[...], dy_ref[...])
        dg_ref[...] = jnp.sum(dgr, axis=0, keepdims=True)

    return pl.pallas_call(body, name=name, out_shape=SDS((1, d), F32), compiler_params=_params())(x, g, dy)


def _loss_fwd(h, target, name):
    rows, d = h.shape
    t = 512

    def body(h_ref, t_ref, loss_ref, dh_ref):
        i = pl.program_id(0)
        err = h_ref[...] - t_ref[...]
        dh_ref[...] = err * (1.0 / d)

        @pl.when(i == 0)
        def _():
            loss_ref[...] = jnp.zeros_like(loss_ref)

        part = jnp.sum(jnp.sum(err * err, axis=1, keepdims=True), axis=0, keepdims=True) * (0.5 / d)
        loss_ref[...] += jnp.broadcast_to(part, loss_ref.shape)

    row = pl.BlockSpec((t, d), lambda i: (i, 0))
    return pl.pallas_call(
        body, name=name, out_shape=(SDS((1, LANES), F32), SDS((rows, d), F32)), grid=(rows // t,),
        in_specs=[row, row], out_specs=(pl.BlockSpec((1, LANES), lambda i: (0, 0)), row), compiler_params=_params(),
    )(h, target)


def _rot_half(x, sign):
    w = x.shape[-1]
    lane = lax.broadcasted_iota(jnp.int32, x.shape, 1)
    first = (lane % HEAD_DIM) < (HEAD_DIM // 2)
    return jnp.where(first, -sign * pltpu.roll(x, w - HEAD_DIM // 2, axis=1), sign * pltpu.roll(x, HEAD_DIM // 2, axis=1))


def _rope_fwd(u, cos, sin, name):
    rows = u.shape[0]
    t, cw = 512, 256
    first_col = (3 * SB_WIDTH + 2 * CV_WIDTH) // cw

    def body(u_ref, c_ref, s_ref, o_ref):
        x = u_ref[...]
        c = jnp.tile(c_ref[...], (1, cw // LANES))
        s = jnp.tile(s_ref[...], (1, cw // LANES))
        o_ref[...] = x * c + _rot_half(x, 1.0) * s

    tab = pl.BlockSpec((t, LANES), lambda i, j: (i, 0))
    return pl.pallas_call(
        body, name=name, out_shape=SDS((rows, 2 * DL_WIDTH), F32), grid=(rows // t, 2 * DL_WIDTH // cw),
        in_specs=[pl.BlockSpec((t, cw), lambda i, j: (i, first_col + j)), tab, tab],
        out_specs=pl.BlockSpec((t, cw), lambda i, j: (i, j)), compiler_params=_params(),
    )(u, cos, sin)


def _rope_bwd(dqs, dks, dvs, cos, sin, name):
    rows = dqs[0].shape[0]
    t, w = 256, DL_WIDTH

    def body(*refs):
        c = jnp.tile(refs[9][...], (1, w // LANES))
        s = jnp.tile(refs[10][...], (1, w // LANES))
        o_ref = refs[11]
        dq = refs[0][...] + refs[1][...] + refs[2][...]
        dk = refs[3][...] + refs[4][...] + refs[5][...]
        dv = refs[6][...] + refs[7][...] + refs[8][...]
        o_ref[:, 0:w] = (dq * c + _rot_half(dq, -1.0) * s).astype(BF16)
        o_ref[:, w:2 * w] = (dk * c + _rot_half(dk, -1.0) * s).astype(BF16)
        o_ref[:, 2 * w:3 * w] = dv.astype(BF16)

    row = pl.BlockSpec((t, w), lambda i: (i, 0))
    tab = pl.BlockSpec((t, LANES), lambda i: (i, 0))
    return pl.pallas_call(
        body, name=name, out_shape=SDS((rows, 3 * w), BF16), grid=(rows // t,), in_specs=[row] * 9 + [tab, tab],
        out_specs=pl.BlockSpec((t, 3 * w), lambda i: (i, 0)), compiler_params=_params(),
    )(*dqs, *dks, *dvs, cos, sin)


SB_TILE = 256
SB_ZERO_AFTER = 110.0
SB_FIRST_BLOCK = (8, LANES)


def _softplus(z):
    return jnp.maximum(z, 0.0) + jnp.log(1.0 + jnp.exp(-jnp.abs(z)))


def _split_dot(x, tri, passes):
    acc = None
    rem = x
    for _ in range(passes):
        part = rem.astype(BF16)
        rem = rem - part.astype(F32)
        d = jnp.dot(part, tri, preferred_element_type=F32)
        acc = d if acc is None else acc + d
    return acc


def _tri(t, rel):
    j = lax.broadcasted_iota(jnp.int32, (t, t), 0)
    s = lax.broadcasted_iota(jnp.int32, (t, t), 1)
    return rel(j, s).astype(BF16)


def _sb_masks(t, i):
    row = lax.broadcasted_iota(jnp.int32, (t, t), 0)
    col = lax.broadcasted_iota(jnp.int32, (t, t), 1)
    return col < row, (row >= 0) & (i >= 1)


def _sb_fwd(q, k, v, payload, name):
    h, s_len, hd = q.shape
    t = SB_TILE
    nq = s_len // t
    scale = hd ** -0.5

    def body(q_ref, k_ref, v_ref, pay_ref, o_ref, tot_ref, first_ref, gathered_ref, send_sems, recv_sems, local_sem):
        hh, i = pl.program_id(0), pl.program_id(1)
        start, forward, finish = _gather_steps(pay_ref, gathered_ref, send_sems, recv_sems, local_sem)
        pl.when((hh == 0) & (i == 0))(start)
        pl.when((hh == h - 1) & (i == nq - 1))(forward)
        qv = q_ref[0] * scale
        upper = _tri(t, lambda j, s: j > s)

        def tiles(js, carry, masks=(None, None)):
            acc, run = carry
            starts = [pl.multiple_of(j * t, t) for j in js]
            zs = [lax.dot_general(qv, k_ref[0, pl.ds(st, t), :], (((1,), (1,)), ((), ())), preferred_element_type=F32)
                  for st in starts]
            sps = [_softplus(z) for z in zs]
            sps = [sp if m is None else jnp.where(m, sp, 0.0) for sp, m in zip(sps, masks)]
            laters = [_split_dot(sp, upper, 2) for sp in sps]
            for st, z, sp, later, m in zip(starts, zs, sps, laters, masks):
                a = jnp.exp((z - sp) - (run + later))
                if m is not None:
                    a = jnp.where(m, a, 0.0)
                acc = acc + jnp.dot(a.astype(BF16), v_ref[0, pl.ds(st, t), :], preferred_element_type=F32)
                run = run + jnp.sum(sp, axis=1, keepdims=True)
            return acc, run

        def live(carry):
            return jnp.min(carry[1]) < SB_ZERO_AFTER

        below, whole = _sb_masks(t, i)
        top = jnp.maximum(i - 1, 0)
        carry = tiles([i, top], (jnp.zeros((t, hd), F32), jnp.zeros((t, 1), F32)), (below, whole))

        def pair(state):
            pp, carry = state
            j = top - 1 - 2 * pp
            return pp + 1, tiles([j, j - 1], carry)

        pairs, carry = lax.while_loop(lambda st: (st[0] < top // 2) & live(st[1]), pair, (0, carry))
        last = ((top % 2 == 1) & (pairs == top // 2) & live(carry)).astype(jnp.int32)
        acc, run = lax.fori_loop(0, last, lambda _, c: tiles([0], c), carry)
        o_ref[0] = acc.astype(BF16)
        tot_ref[0] = run
        first_ref[...] = jnp.full(first_ref.shape, top - 2 * pairs - last, jnp.int32).astype(F32)
        pl.when((hh == h - 1) & (i == nq - 1))(finish)

    full = pl.BlockSpec((1, s_len, hd), lambda hh, i: (hh, 0, 0))
    tile = pl.BlockSpec((1, t, hd), lambda hh, i: (hh, i, 0))
    hbm = pl.BlockSpec(memory_space=pl.ANY)
    return pl.pallas_call(
        body, name=name,
        out_shape=(SDS((h, s_len, hd), BF16), SDS((h, s_len, 1), F32), SDS((h, nq) + SB_FIRST_BLOCK, F32),
                   SDS((N_DEV,) + payload.shape, payload.dtype)),
        grid=(h, nq), in_specs=[tile, full, full, hbm],
        out_specs=(tile, pl.BlockSpec((1, t, 1), lambda hh, i: (hh, i, 0)),
                   pl.BlockSpec((1, 1) + SB_FIRST_BLOCK, lambda hh, i: (hh, i, 0, 0)), hbm),
        scratch_shapes=_COMM_SEMAPHORES, compiler_params=_params(has_side_effects=True),
    )(q, k, v, payload)


def _sb_bwd(q, k, v, do, tot, first, payload, name):
    h, s_len, hd = q.shape
    t = SB_TILE
    nq = s_len // t
    scale = hd ** -0.5

    def body(q_ref, k_ref, v_ref, do_ref, tot_ref, first_ref, pay_ref, dq_ref, dk_ref, dv_ref, received_ref, dk_acc, dv_acc,
             send_sems, recv_sems, local_sem):
        hh, i = pl.program_id(0), pl.program_id(1)
        start, finish = _exchange_steps(pay_ref, received_ref, send_sems, recv_sems, local_sem)
        pl.when((hh == 0) & (i == 0))(start)

        @pl.when(i == 0)
        def _():
            dk_acc[...] = jnp.zeros_like(dk_acc)
            dv_acc[...] = jnp.zeros_like(dv_acc)

        qv = q_ref[0] * scale
        dov = do_ref[0]
        total = tot_ref[0]
        upto = _tri(t, lambda j, s: j <= s)
        before = _tri(t, lambda j, s: j < s)
        nt_dims = (((1,), (1,)), ((), ()))
        tn_dims = (((0,), (0,)), ((), ()))

        def tiles(js, carry, masks=(None, None)):
            dq, run_sp, run_g = carry
            starts = [pl.multiple_of(j * t, t) for j in js]
            zs = [lax.dot_general(qv, k_ref[0, pl.ds(st, t), :], nt_dims, preferred_element_type=F32) for st in starts]
            das = [lax.dot_general(dov, v_ref[0, pl.ds(st, t), :], nt_dims, preferred_element_type=F32) for st in starts]
            sps = [_softplus(z) for z in zs]
            log_sigs = [z - sp for z, sp in zip(zs, sps)]
            sps = [sp if m is None else jnp.where(m, sp, 0.0) for sp, m in zip(sps, masks)]
            pres = [_split_dot(sp, upto, 2) for sp in sps]
            a_s, gs = [], []
            for sp, log_sig, pre, da, m in zip(sps, log_sigs, pres, das, masks):
                a = jnp.exp(log_sig - (total - (run_sp + pre)))
                if m is not None:
                    a = jnp.where(m, a, 0.0)
                a_s.append(a)
                gs.append(a * da)
                run_sp = run_sp + jnp.sum(sp, axis=1, keepdims=True)
            g_pres = [_split_dot(g, before, 3) for g in gs]
            for st, a, g, g_pre, log_sig, m in zip(starts, a_s, gs, g_pres, log_sigs, masks):
                sig = jnp.exp(log_sig)
                dz = g * (1.0 - sig) - sig * (run_g + g_pre)
                if m is not None:
                    dz = jnp.where(m, dz, 0.0)
                dz = dz.astype(BF16)
                dq = dq + jnp.dot(dz, k_ref[0, pl.ds(st, t), :], preferred_element_type=F32)
                dk_acc[pl.ds(st, t), :] += lax.dot_general(dz, qv, tn_dims, preferred_element_type=F32)
                dv_acc[pl.ds(st, t), :] += lax.dot_general(a.astype(BF16), dov, tn_dims, preferred_element_type=F32)
                run_g = run_g + jnp.sum(g, axis=1, keepdims=True)
            return dq, run_sp, run_g

        zero = jnp.zeros((t, 1), F32)
        top = jnp.maximum(i - 1, 0)
        first = jnp.clip(first_ref[0, 0, 0, 0].astype(jnp.int32), 0, top)
        count = top - first
        carry = lax.fori_loop(0, count // 2, lambda pp, c: tiles([first + 2 * pp, first + 2 * pp + 1], c),
                              (jnp.zeros((t, hd), F32), zero, zero))
        carry = lax.fori_loop(0, count % 2, lambda _, c: tiles([top - 1], c), carry)
        below, whole = _sb_masks(t, i)
        dq, _, _ = tiles([top, i], carry, (whole, below))
        dq_ref[0] = (dq * scale).astype(BF16)

        @pl.when(i == nq - 1)
        def _():
            dk_ref[0] = dk_acc[...].astype(BF16)
            dv_ref[0] = dv_acc[...].astype(BF16)

        pl.when((hh == h - 1) & (i == nq - 1))(finish)

    full = pl.BlockSpec((1, s_len, hd), lambda hh, i: (hh, 0, 0))
    tile = pl.BlockSpec((1, t, hd), lambda hh, i: (hh, i, 0))
    hbm = pl.BlockSpec(memory_space=pl.ANY)
    out = SDS((h, s_len, hd), BF16)
    return pl.pallas_call(
        body, name=name, out_shape=(out, out, out, SDS(payload.shape, payload.dtype)), grid=(h, nq),
        in_specs=[tile, full, full, tile, pl.BlockSpec((1, t, 1), lambda hh, i: (hh, i, 0)),
                  pl.BlockSpec((1, 1) + SB_FIRST_BLOCK, lambda hh, i: (hh, i, 0, 0)), hbm],
        out_specs=(tile, full, full, hbm),
        scratch_shapes=[pltpu.VMEM((s_len, hd), F32), pltpu.VMEM((s_len, hd), F32)] + _COMM_SEMAPHORES,
        compiler_params=_params(has_side_effects=True),
    )(q, k, v, do, tot, first, payload)


def _dl_scores(qv, kk, n):
    s = lax.dot_general(qv, kk, (((1,), (1,)), ((), ())), preferred_element_type=F32) * (HEAD_DIM ** -0.5)
    r = lax.broadcasted_iota(jnp.int32, s.shape, 0)
    c = lax.broadcasted_iota(jnp.int32, s.shape, 1)
    valid = (c >= r) & (c - r <= BLOCK) & ((n > 0) | (c >= BLOCK))
    return jnp.where(valid, s, NEG)


DL_UNROLL = 4
DL_PAIR = 2 * HEAD_DIM
DL_Q_BLOCK0 = 0
DL_K_BLOCK0 = DL_WIDTH // DL_PAIR
DL_V_BLOCK0 = (IN_WIDTH - DL_WIDTH) // DL_PAIR
DL_DO_BLOCK0 = (SB_WIDTH + CV_WIDTH) // DL_PAIR


def _dl_rows(idx, nb, dil):
    r, n = idx // nb, idx % nb
    cur = pl.ds(r + n * (BLOCK * dil), BLOCK, stride=dil)
    prev = pl.ds(r + jnp.maximum(n - 1, 0) * (BLOCK * dil), BLOCK, stride=dil)
    return n, cur, prev


def _dl_window(ref, cur, prev):
    return jnp.concatenate([ref[prev, :], ref[cur, :]], axis=0).astype(BF16)


def _head_lanes():
    first = lax.broadcasted_iota(jnp.int32, (BLOCK, DL_PAIR), 1) < HEAD_DIM
    return first, jnp.logical_not(first)


def _dl_fwd(qk, u, dil, name, payload=None):
    s_len = qk.shape[0]
    nb = s_len // dil // BLOCK

    def body(q_ref, k_ref, v_ref, o_ref, lse_ref):
        heads = _head_lanes()

        def step(idx, _):
            n, cur, prev = _dl_rows(idx, nb, dil)
            q = q_ref[cur, :]
            kk = _dl_window(k_ref, cur, prev)
            vv = _dl_window(v_ref, cur, prev)
            o, lse = None, None
            for lanes in heads:
                s = _dl_scores(jnp.where(lanes, q, 0.0).astype(BF16), kk, n)
                m = jnp.max(s, axis=-1, keepdims=True)
                p = jnp.exp(s - m)
                den = jnp.sum(p, axis=-1, keepdims=True)
                o_h = jnp.dot((p / den).astype(BF16), vv, preferred_element_type=F32)
                lse_h = jnp.broadcast_to(m + jnp.log(den), (BLOCK, DL_PAIR))
                o = o_h if o is None else jnp.where(heads[0], o, o_h)
                lse = lse_h if lse is None else jnp.where(heads[0], lse, lse_h)
            o_ref[cur, :] = o
            lse_ref[cur, :] = lse
            return 0

        lax.fori_loop(0, s_len // BLOCK, step, 0, unroll=DL_UNROLL)

    col = lambda first: pl.BlockSpec((s_len, DL_PAIR), lambda i: (0, first + i))
    out = SDS((s_len, DL_WIDTH), F32)
    steps = DL_WIDTH // DL_PAIR
    if payload is not None:
        step = lambda: pl.program_id(0)
        payload = ("gather", payload, lambda: (step() == 0, step() == steps - 1, step() == steps - 1))
    return _hosted_call(body, payload, name=name, out_shape=(out, out), grid=(steps,),
                        in_specs=[col(DL_Q_BLOCK0), col(DL_K_BLOCK0), col(DL_V_BLOCK0)], out_specs=(col(0), col(0)),
                        args=(qk, qk, u))


def _dl_bwd(qk, u, dmix, o_mix, wt, lse, dil, name, payload=None):
    s_len = qk.shape[0]
    nb = s_len // dil // BLOCK
    scale = HEAD_DIM ** -0.5
    nt_dims = (((1,), (1,)), ((), ()))
    tn_dims = (((0,), (0,)), ((), ()))

    def body(q_ref, k_ref, v_ref, do_ref, om_ref, wt_ref, lse_ref, dq_ref, dk_ref, dv_ref):
        dk_ref[...] = jnp.zeros_like(dk_ref)
        dv_ref[...] = jnp.zeros_like(dv_ref)
        heads = _head_lanes()

        def step(idx, _):
            n, cur, prev = _dl_rows(idx, nb, dil)
            q = q_ref[cur, :]
            kk = _dl_window(k_ref, cur, prev)
            vv = _dl_window(v_ref, cur, prev)
            dov = do_ref[cur, :]
            d_lanes = dov * om_ref[cur, :]
            w_lanes = wt_ref[cur, :]
            lse_lanes = lse_ref[cur, :]
            dq, dkk, dvv = None, None, None
            for lanes in heads:
                qm = jnp.where(lanes, q, 0.0).astype(BF16)
                s = _dl_scores(qm, kk, n)
                p = jnp.exp(s - jnp.max(jnp.where(lanes, lse_lanes, NEG), axis=-1, keepdims=True))
                w = jnp.max(jnp.where(lanes, w_lanes, 0.0), axis=-1, keepdims=True)
                d_all = jnp.sum(jnp.where(lanes, d_lanes, 0.0), axis=-1, keepdims=True)
                do_n = jnp.where(lanes, dov * w, 0.0).astype(BF16)
                dp = lax.dot_general(do_n, vv, nt_dims, preferred_element_type=F32)
                ds = (p * (dp - w * d_all) * scale).astype(BF16)
                dq_h = jnp.dot(ds, kk, preferred_element_type=F32)
                dkk_h = lax.dot_general(ds, qm, tn_dims, preferred_element_type=F32)
                dvv_h = lax.dot_general(p.astype(BF16), do_n, tn_dims, preferred_element_type=F32)
                dq = dq_h if dq is None else jnp.where(heads[0], dq, dq_h)
                dkk = dkk_h if dkk is None else dkk + dkk_h
                dvv = dvv_h if dvv is None else dvv + dvv_h
            dq_ref[cur, :] = dq
            dk_ref[prev, :] += dkk[:BLOCK]
            dv_ref[prev, :] += dvv[:BLOCK]
            dk_ref[cur, :] += dkk[BLOCK:]
            dv_ref[cur, :] += dvv[BLOCK:]
            return 0

        lax.fori_loop(0, s_len // BLOCK, step, 0, unroll=DL_UNROLL)

    col = lambda first: pl.BlockSpec((s_len, DL_PAIR), lambda i: (0, first + i))
    out = SDS((s_len, DL_WIDTH), F32)
    steps = DL_WIDTH // DL_PAIR
    if payload is not None:
        step = lambda: pl.program_id(0)
        payload = ("all_to_all", payload, lambda: (step() == 0, None, step() == steps - 1))
    return _hosted_call(
        body, payload, name=name, out_shape=(out, out, out), grid=(steps,),
        in_specs=[col(DL_Q_BLOCK0), col(DL_K_BLOCK0), col(DL_V_BLOCK0), col(DL_DO_BLOCK0), col(0), col(0), col(0)],
        out_specs=(col(0), col(0), col(0)), args=(qk, qk, u, dmix, o_mix, wt, lse))


def _dl_mix_fwd(outs, lses, name):
    rows, w = outs[0].shape
    t = 256

    def body(o1, o2, o3, l1, l2, l3, ob_ref, of_ref, w1, w2, w3):
        a, b, c = l1[...], l2[...], l3[...]
        m = jnp.maximum(jnp.maximum(a, b), c)
        ea, eb, ec = jnp.exp(a - m), jnp.exp(b - m), jnp.exp(c - m)
        den = ea + eb + ec
        wa, wb, wc = ea / den, eb / den, ec / den
        o = wa * o1[...] + wb * o2[...] + wc * o3[...]
        ob_ref[...] = o.astype(BF16)
        of_ref[...] = o
        w1[...] = wa
        w2[...] = wb
        w3[...] = wc

    row = pl.BlockSpec((t, w), lambda i: (i, 0))
    f = SDS((rows, w), F32)
    return pl.pallas_call(body, name=name, out_shape=(SDS((rows, w), BF16), f, f, f, f), grid=(rows // t,),
                          in_specs=[row] * 6, out_specs=(row,) * 5, compiler_params=_params())(*outs, *lses)


def _x_probs(qh, kh):
    s = lax.dot_general(qh, kh, (((1,), (1,)), ((), ())), preferred_element_type=F32) * (X_HEAD_DIM ** -0.5)
    e = jnp.exp(s - jnp.max(s, axis=-1, keepdims=True))
    return e / jnp.sum(e, axis=-1, keepdims=True)


def _xattn_fwd(q, k, v, name):
    rows, d = q.shape
    t = 512

    def body(q_ref, k_ref, v_ref, o_ref):
        for hh in range(X_HEADS):
            cols = slice(hh * X_HEAD_DIM, (hh + 1) * X_HEAD_DIM)
            p = _x_probs(q_ref[:, cols], k_ref[:, cols])
            o_ref[:, cols] = jnp.dot(p.astype(BF16), v_ref[:, cols], preferred_element_type=F32).astype(BF16)

    row = pl.BlockSpec((t, d), lambda i: (i, 0))
    mem = pl.BlockSpec((N_MEM, d), lambda i: (0, 0))
    return pl.pallas_call(body, name=name, out_shape=SDS((rows, d), BF16), grid=(rows // t,), in_specs=[row, mem, mem],
                          out_specs=row, compiler_params=_params())(q, k, v)


def _xattn_bwd(q, k, v, do, name):
    rows, d = q.shape
    t = 512
    scale = X_HEAD_DIM ** -0.5

    def body(q_ref, k_ref, v_ref, do_ref, dq_ref, dk_ref, dv_ref):
        @pl.when(pl.program_id(0) == 0)
        def _():
            dk_ref[...] = jnp.zeros_like(dk_ref)
            dv_ref[...] = jnp.zeros_like(dv_ref)

        for hh in range(X_HEADS):
            cols = slice(hh * X_HEAD_DIM, (hh + 1) * X_HEAD_DIM)
            qh, kh, vh, doh = q_ref[:, cols], k_ref[:, cols], v_ref[:, cols], do_ref[:, cols]
            p = _x_probs(qh, kh)
            dp = lax.dot_general(doh, vh, (((1,), (1,)), ((), ())), preferred_element_type=F32)
            ds = (p * (dp - jnp.sum(p * dp, axis=-1, keepdims=True)) * scale).astype(BF16)
            dq_ref[:, cols] = jnp.dot(ds, kh, preferred_element_type=F32).astype(BF16)
            dk_ref[:, cols] += lax.dot_general(ds, qh, (((0,), (0,)), ((), ())), preferred_element_type=F32)
            dv_ref[:, cols] += lax.dot_general(p.astype(BF16), doh, (((0,), (0,)), ((), ())), preferred_element_type=F32)

    row = pl.BlockSpec((t, d), lambda i: (i, 0))
    mem = pl.BlockSpec((N_MEM, d), lambda i: (0, 0))
    return pl.pallas_call(
        body, name=name, out_shape=(SDS((rows, d), BF16), SDS((N_MEM, d), F32), SDS((N_MEM, d), F32)), grid=(rows // t,),
        in_specs=[row, mem, mem, row], out_specs=(row, mem, mem), compiler_params=_params(),
    )(q, k, v, do)


CV_TILE = 256
CV_HALO = 32
CV_LEAD = CV_HALO - (CV_KERNEL - 1)


def _shifted(win, off, rows):
    n = win.shape[0]
    return pltpu.roll(win, (n - off) % n, axis=0)[:rows]


def _glu(val, gate):
    return val * jax.nn.sigmoid(gate)


def _ln_parts(c):
    mu = jnp.mean(c, axis=-1, keepdims=True)
    xc = c - mu
    rstd = lax.rsqrt(jnp.mean(xc * xc, axis=-1, keepdims=True) + EPS)
    return xc * rstd, rstd


def _cv_fwd(u, cv_w, cv_b, ln_g, ln_b, name):
    rows = u.shape[0]
    t, w = CV_TILE, CV_WIDTH
    val_col = 3 * SB_WIDTH // w
    ratio = t // CV_HALO

    def body(val_ref, gate_ref, pval_ref, pgate_ref, w_ref, b_ref, g_ref, beta_ref, s_ref, c_ref):
        i = pl.program_id(0)
        hist = jnp.where(i > 0, _glu(pval_ref[...], pgate_ref[...]), 0.0)
        win = jnp.concatenate([hist, _glu(val_ref[...], gate_ref[...])], axis=0)
        acc = jnp.broadcast_to(b_ref[...], (t, w))
        for kk in range(CV_KERNEL):
            acc = acc + _shifted(win, CV_LEAD + kk, t) * w_ref[kk:kk + 1, :]
        c_ref[...] = acc
        n, _ = _ln_parts(acc)
        y = n * g_ref[...] + beta_ref[...]
        s_ref[...] = (y * jax.nn.sigmoid(y)).astype(BF16)

    cur = lambda col: pl.BlockSpec((t, w), lambda i: (i, col))
    prev = lambda col: pl.BlockSpec((CV_HALO, w), lambda i: (jnp.maximum(i * ratio - 1, 0), col))
    vec = pl.BlockSpec((1, w), lambda i: (0, 0))
    return pl.pallas_call(
        body, name=name, out_shape=(SDS((rows, w), BF16), SDS((rows, w), F32)), grid=(rows // t,),
        in_specs=[cur(val_col), cur(val_col + 1), prev(val_col), prev(val_col + 1),
                  pl.BlockSpec((CV_KERNEL, w), lambda i: (0, 0)), vec, vec, vec],
        out_specs=(pl.BlockSpec((t, w), lambda i: (i, 0)),) * 2, compiler_params=_params(),
    )(u, u, u, u, cv_w, cv_b, ln_g, ln_b)


def _cv_bwd(u, c, ds, db_out, cv_w, ln_g, ln_b, name):
    rows = u.shape[0]
    t, w = CV_TILE, CV_WIDTH
    val_col = 3 * SB_WIDTH // w
    ratio = t // CV_HALO
    nt = rows // t

    def conv_out_grad(c_v, ds_v, g_v, beta_v):
        n, rstd = _ln_parts(c_v)
        y = n * g_v + beta_v
        sig = jax.nn.sigmoid(y)
        dy = ds_v * (sig * (1.0 + y * (1.0 - sig)))
        dn = dy * g_v
        dc = rstd * (dn - jnp.mean(dn, axis=-1, keepdims=True) - n * jnp.mean(dn * n, axis=-1, keepdims=True))
        return dc, dy, n

    def body(val_ref, gate_ref, pval_ref, pgate_ref, c_ref, nc_ref, ds_ref, nds_ref, dbo_ref, w_ref, g_ref, beta_ref,
             dvg_ref, dw_ref, db_ref, dg_ref, dbeta_ref, dpwb_ref):
        i = pl.program_id(0)

        @pl.when(i == 0)
        def _():
            for r in (dw_ref, db_ref, dg_ref, dbeta_ref, dpwb_ref):
                r[...] = jnp.zeros_like(r)

        g_v, beta_v = g_ref[...], beta_ref[...]
        dc, dy, n = conv_out_grad(c_ref[...], ds_ref[...], g_v, beta_v)
        dc_next, _, _ = conv_out_grad(nc_ref[...], nds_ref[...], g_v, beta_v)
        dc_next = jnp.where(i < nt - 1, dc_next, 0.0)
        dg_ref[...] += jnp.sum(dy * n, axis=0, keepdims=True)
        dbeta_ref[...] += jnp.sum(dy, axis=0, keepdims=True)
        db_ref[...] += jnp.sum(dc, axis=0, keepdims=True)
        dpwb_ref[...] += jnp.sum(dbo_ref[...], axis=0, keepdims=True)

        val, gate = val_ref[...], gate_ref[...]
        hist = jnp.where(i > 0, _glu(pval_ref[...], pgate_ref[...]), 0.0)
        win = jnp.concatenate([hist, _glu(val, gate)], axis=0)
        dc_ext = jnp.concatenate([dc, dc_next], axis=0)
        dglu = jnp.zeros((t, w), F32)
        for kk in range(CV_KERNEL):
            dw_ref[kk:kk + 1, :] += jnp.sum(dc * _shifted(win, CV_LEAD + kk, t), axis=0, keepdims=True)
            dglu = dglu + _shifted(dc_ext, CV_KERNEL - 1 - kk, t) * w_ref[kk:kk + 1, :]
        sig = jax.nn.sigmoid(gate)
        dvg_ref[:, 0:w] = (dglu * sig).astype(BF16)
        dvg_ref[:, w:2 * w] = (dglu * val * sig * (1.0 - sig)).astype(BF16)

    cur = lambda col: pl.BlockSpec((t, w), lambda i: (i, col))
    prev = lambda col: pl.BlockSpec((CV_HALO, w), lambda i: (jnp.maximum(i * ratio - 1, 0), col))
    nxt = pl.BlockSpec((CV_HALO, w), lambda i: (jnp.minimum((i + 1) * ratio, rows // CV_HALO - 1), 0))
    vec = pl.BlockSpec((1, w), lambda i: (0, 0))
    return pl.pallas_call(
        body, name=name,
        out_shape=(SDS((rows, 2 * w), BF16), SDS((CV_HALO, w), F32), SDS((1, w), F32), SDS((1, w), F32), SDS((1, w), F32),
                   SDS((1, w), F32)),
        grid=(nt,),
        in_specs=[cur(val_col), cur(val_col + 1), prev(val_col), prev(val_col + 1), cur(0), nxt, cur(0), nxt, cur(0),
                  pl.BlockSpec((CV_KERNEL, w), lambda i: (0, 0)), vec, vec],
        out_specs=(pl.BlockSpec((t, 2 * w), lambda i: (i, 0)), pl.BlockSpec((CV_HALO, w), lambda i: (0, 0)), vec, vec, vec, vec),
        compiler_params=_params(),
    )(u, u, u, u, c, c, ds, ds, db_out, cv_w, ln_g, ln_b)


FFN_TILE = 512
FFN_CHUNK = 64
FFN_COLS = 256
FFN_HALO = 8
FFN_KERNEL = 3
N_FF_BLOCKS = D_FF // FFN_COLS


def _conv3(prev8, cur, w_ref, b_ref, first):
    t = cur.shape[0]
    win = jnp.concatenate([jnp.where(first, 0.0, prev8), cur], axis=0)
    return (b_ref[...] + _shifted(win, FFN_HALO - 2, t) * w_ref[0:1, :] + _shifted(win, FFN_HALO - 1, t) * w_ref[1:2, :]
            + cur * w_ref[2:3, :])


def _gelu_gate(gate, val):
    return jax.nn.gelu(gate, approximate=True) * val


GELU_C0 = math.sqrt(2.0 / math.pi)
GELU_C1 = 0.044715


def _gelu_gate_bwd(gate, val, dout):
    sq = gate * gate
    th = jnp.tanh(GELU_C0 * gate * (1.0 + GELU_C1 * sq))
    half_cdf = 0.5 * (1.0 + th)
    slope = half_cdf + 0.5 * gate * (1.0 - th * th) * (GELU_C0 * (1.0 + 3.0 * GELU_C1 * sq))
    return dout * val * slope, dout * (gate * half_cdf)


def _ffn_specs(t):
    ratio = t // FFN_HALO
    cur = pl.BlockSpec((t, FFN_COLS), lambda j, i: (i, j))
    prev = pl.BlockSpec((FFN_HALO, FFN_COLS), lambda j, i: (jnp.maximum(i * ratio - 1, 0), j))
    wsp = pl.BlockSpec((FFN_KERNEL, FFN_COLS), lambda j, i: (0, j))
    bsp = pl.BlockSpec((1, FFN_COLS), lambda j, i: (0, j))
    return cur, prev, wsp, bsp


def _ffn_host_steps(row_tiles):
    def when():
        j, i = pl.program_id(0), pl.program_id(1)
        return (j == 0) & (i == 0), (j == (3 * N_FF_BLOCKS) // 4) & (i == 0), (j == N_FF_BLOCKS - 1) & (i == row_tiles - 1)
    return when


def _ffn_act_fwd(up_g, up_v, w_g, w_v, b_g, b_v, name, payload=None):
    rows = up_g.shape[0]
    t = FFN_TILE
    cur, prev, wsp, bsp = _ffn_specs(t)

    def body(g_ref, v_ref, pg_ref, pv_ref, wg_ref, wv_ref, bg_ref, bv_ref, o_ref):
        first = pl.program_id(1) == 0
        gate = _conv3(pg_ref[...], g_ref[...], wg_ref, bg_ref, first)
        val = _conv3(pv_ref[...], v_ref[...], wv_ref, bv_ref, first)
        o_ref[...] = _gelu_gate(gate, val).astype(BF16)

    if payload is not None:
        payload = ("gather", payload, _ffn_host_steps(rows // t))
    return _hosted_call(
        body, payload, name=name, out_shape=(SDS((rows, D_FF), BF16),), grid=(N_FF_BLOCKS, rows // t),
        in_specs=[cur, cur, prev, prev, wsp, wsp, bsp, bsp], out_specs=(cur,), args=(up_g, up_v, up_g, up_v, w_g, w_v, b_g, b_v))


def _ffn_act_bwd(up_g, up_v, dact, w_g, w_v, b_g, b_v, name, payload=None):
    rows = up_g.shape[0]
    t = FFN_TILE
    ch = FFN_CHUNK
    che = ch + FFN_HALO
    ratio = t // FFN_HALO
    nt = rows // t
    cur, prev, wsp, bsp = _ffn_specs(t)
    nxt = pl.BlockSpec((FFN_HALO, FFN_COLS), lambda j, i: (jnp.minimum((i + 1) * ratio, rows // FFN_HALO - 1), j))

    def body(g_ref, v_ref, pg_ref, pv_ref, ng_ref, nv_ref, da_ref, nda_ref, wg_ref, wv_ref, bg_ref, bv_ref,
             dug_ref, duv_ref, dwg_ref, dwv_ref, dbg_ref, dbv_ref, win_g, win_v, da_win):
        i = pl.program_id(1)
        first = i == 0

        @pl.when(first)
        def _():
            for r in (dwg_ref, dwv_ref, dbg_ref, dbv_ref):
                r[...] = jnp.zeros_like(r)

        for win, pre, x, nx in ((win_g, pg_ref, g_ref, ng_ref), (win_v, pv_ref, v_ref, nv_ref)):
            win[0:FFN_HALO, :] = jnp.where(first, 0.0, pre[...])
            win[FFN_HALO:FFN_HALO + t, :] = x[...]
            win[FFN_HALO + t:, :] = nx[...]
        da_win[0:t, :] = da_ref[...]
        da_win[t:, :] = jnp.where(i < nt - 1, nda_ref[...], 0.0)
        halves = ((win_g, wg_ref, bg_ref, dug_ref), (win_v, wv_ref, bv_ref, duv_ref))

        def chunk(c, sums):
            base = pl.multiple_of(c * ch, ch)
            taps, convs = [], []
            for win, w_ref, b_ref, _ in halves:
                w = win[pl.ds(base, ch + 2 * FFN_HALO), :]
                shifted = [_shifted(w, FFN_HALO - 2 + kk, che) for kk in range(FFN_KERNEL)]
                taps.append(shifted)
                convs.append(b_ref[...] + sum(s * w_ref[kk:kk + 1, :] for kk, s in enumerate(shifted)))
            dconvs = _gelu_gate_bwd(*convs, da_win[pl.ds(base, che), :])
            new = []
            for dc_ext, shifted, (_, w_ref, _, du_ref), (dw, db) in zip(dconvs, taps, halves, sums):
                dc = dc_ext[:ch]
                du_ref[pl.ds(base, ch), :] = (dc * w_ref[2:3, :] + _shifted(dc_ext, 1, ch) * w_ref[1:2, :]
                                              + _shifted(dc_ext, 2, ch) * w_ref[0:1, :]).astype(BF16)
                dw = [dw[kk] + jnp.sum(dc * shifted[kk][:ch], axis=0, keepdims=True) for kk in range(FFN_KERNEL)]
                new.append((dw, db + jnp.sum(dc, axis=0, keepdims=True)))
            return new

        zero = jnp.zeros((1, FFN_COLS), F32)
        sums = lax.fori_loop(0, t // ch, chunk, [([zero] * FFN_KERNEL, zero)] * 2, unroll=2)
        for (dw, db), dw_ref, db_ref in zip(sums, (dwg_ref, dwv_ref), (dbg_ref, dbv_ref)):
            for kk in range(FFN_KERNEL):
                dw_ref[kk:kk + 1, :] += dw[kk]
            db_ref[...] += db

    big, wshape, bshape = SDS((rows, D_FF), BF16), SDS((FFN_KERNEL, D_FF), F32), SDS((1, D_FF), F32)
    if payload is not None:
        payload = ("all_to_all", payload, _ffn_host_steps(nt))
    window = pltpu.VMEM((t + 2 * FFN_HALO, FFN_COLS), F32)
    return _hosted_call(
        body, payload, name=name, out_shape=(big, big, wshape, wshape, bshape, bshape), grid=(N_FF_BLOCKS, nt),
        in_specs=[cur, cur, prev, prev, nxt, nxt, cur, nxt, wsp, wsp, bsp, bsp], out_specs=(cur, cur, wsp, wsp, bsp, bsp),
        scratch_shapes=(window, window, pltpu.VMEM((t + FFN_HALO, FFN_COLS), F32)),
        args=(up_g, up_v, up_g, up_v, up_g, up_v, dact, dact, w_g, w_v, b_g, b_v))


def _adamw_update(parts, w_ref, m_ref, v_ref, g_ref, d_ref, nm_ref, nv_ref):
    g = parts[0].astype(F32)
    for s in range(1, N_DEV):
        g = g + parts[s].astype(F32)
    nm = ADAM_B1 * m_ref[...] + (1.0 - ADAM_B1) * g
    nv = ADAM_B2 * v_ref[...] + (1.0 - ADAM_B2) * jnp.square(g)
    m_hat = nm / (1.0 - ADAM_B1 ** ADAM_STEP)
    v_hat = nv / (1.0 - ADAM_B2 ** ADAM_STEP)
    g_ref[...] = g
    d_ref[...] = -ADAM_LR * (m_hat / (jnp.sqrt(v_hat) + ADAM_EPS) + ADAM_WD * w_ref[...])
    nm_ref[...] = nm
    nv_ref[...] = nv


def _adamw(parts, w, m, v, name):
    rows, cols = w.shape
    t = _pick(rows, (512, 256, 128)) if rows > 512 else rows

    def body(p_ref, *refs):
        _adamw_update(p_ref[...], *refs)

    row = pl.BlockSpec((t, cols), lambda i: (i, 0))
    out = SDS((rows, cols), F32)
    return pl.pallas_call(
        body, name=name, out_shape=(out,) * 4, grid=(rows // t,),
        in_specs=[pl.BlockSpec((N_DEV, t, cols), lambda i: (0, i, 0)), row, row, row], out_specs=(row,) * 4,
        compiler_params=_params(),
    )(parts, w, m, v)


def _adamw_packed(sources, w, m, v, name):
    rows, cols = w.shape
    t = ADAMW_ROW_BLOCK
    nb = rows // DEPTH // t
    (src0, first0), (src1, first1) = sources
    assert first0 % t == 0 and first1 % t == 0 and rows % (DEPTH * t) == 0

    def body(p0_ref, p1_ref, *refs):
        layer = pl.program_id(0)
        _adamw_update(jnp.where(layer == 0, p0_ref[...], p1_ref[...]), *refs)

    spec0 = pl.BlockSpec((N_DEV, t, cols), lambda l, i: (0, first0 // t + i * (1 - l) + (nb - 1) * l, 0))
    spec1 = pl.BlockSpec((N_DEV, t, cols), lambda l, i: (0, first1 // t + i * l, 0))
    row = pl.BlockSpec((t, cols), lambda l, i: (l * nb + i, 0))
    out = SDS((rows, cols), F32)
    return pl.pallas_call(body, name=name, out_shape=(out,) * 4, grid=(DEPTH, nb), in_specs=[spec0, spec1, row, row, row],
                          out_specs=(row,) * 4, compiler_params=_params())(src0, src1, w, m, v)


_COMM_SEMAPHORES = [pltpu.SemaphoreType.DMA((N_DEV - 1,)), pltpu.SemaphoreType.DMA((N_DEV - 1,)), pltpu.SemaphoreType.DMA]


def _gather_steps(x_ref, out_ref, send_sems, recv_sems, local_sem):
    x_, y_, c_ = lax.axis_index("x"), lax.axis_index("y"), lax.axis_index("c")
    me, sibling = (x_, y_, c_), (x_, y_, 1 - c_)
    chips = [(1 - x_, y_), (x_, 1 - y_), (1 - x_, 1 - y_)]

    def slot(px, py, pc):
        return out_ref.at[4 * px + 2 * py + pc]

    def copy(kk, block, to, src=None):
        return pltpu.make_async_remote_copy(
            src_ref=slot(*block) if src is None else src, dst_ref=slot(*block),
            send_sem=send_sems.at[kk], recv_sem=recv_sems.at[kk], device_id=to, device_id_type=MESH)

    def mine():
        return pltpu.make_async_copy(x_ref, slot(*me), local_sem)

    def first():
        return [copy(0, me, sibling, src=x_ref)] + [copy(1 + j, me, (*chip, c_), src=x_ref) for j, chip in enumerate(chips)]

    def passed():
        return [copy(4 + j, (*chip, c_), sibling) for j, chip in enumerate(chips)]

    def start():
        mine().start()
        for cp in first():
            cp.start()

    def forward():
        for j, (chip, cp) in enumerate(zip(chips, passed())):
            copy(1 + j, (*chip, c_), me).wait_recv()
            cp.start()

    def finish():
        copy(0, sibling, me).wait_recv()
        for j, chip in enumerate(chips):
            copy(4 + j, (*chip, 1 - c_), me).wait_recv()
        for cp in first() + passed():
            cp.wait_send()
        mine().wait()

    return start, forward, finish


def _exchange_steps(x_ref, out_ref, send_sems, recv_sems, local_sem):
    x_, y_, c_ = lax.axis_index("x"), lax.axis_index("y"), lax.axis_index("c")
    me = 4 * x_ + 2 * y_ + c_

    def mine():
        return pltpu.make_async_copy(x_ref.at[me], out_ref.at[me], local_sem)

    def copies():
        out = []
        for r in range(1, N_DEV):
            px = 1 - x_ if r & 4 else x_
            py = 1 - y_ if r & 2 else y_
            pc = 1 - c_ if r & 1 else c_
            out.append(pltpu.make_async_remote_copy(
                src_ref=x_ref.at[4 * px + 2 * py + pc], dst_ref=out_ref.at[me],
                send_sem=send_sems.at[r - 1], recv_sem=recv_sems.at[r - 1], device_id=(px, py, pc), device_id_type=MESH))
        return out

    def start():
        mine().start()
        for cp in copies():
            cp.start()

    def finish():
        for cp in copies():
            cp.wait_recv()
        for cp in copies():
            cp.wait_send()
        mine().wait()

    return start, finish


def _hosted_call(body, exchange, *, name, out_shape, grid, in_specs, out_specs, args, scratch_shapes=()):
    if exchange is None:
        return pl.pallas_call(body, name=name, out_shape=out_shape, grid=grid, in_specs=in_specs, out_specs=out_specs,
                              scratch_shapes=list(scratch_shapes), compiler_params=_params())(*args)
    kind, payload, when = exchange
    n_in, n_out, n_scratch = len(in_specs), len(out_specs), len(scratch_shapes)
    result = SDS((N_DEV,) + payload.shape, payload.dtype) if kind == "gather" else SDS(payload.shape, payload.dtype)

    def hosting(*refs):
        ins, pay_ref = refs[:n_in], refs[n_in]
        outs, res_ref = refs[n_in + 1:n_in + 1 + n_out], refs[n_in + 1 + n_out]
        rest = refs[n_in + 2 + n_out:]
        scratch, sems = rest[:n_scratch], rest[n_scratch:]
        first, middle, last = when()
        if kind == "gather":
            start, forward, finish = _gather_steps(pay_ref, res_ref, *sems)
            pl.when(first)(start)
            pl.when(middle)(forward)
        else:
            start, finish = _exchange_steps(pay_ref, res_ref, *sems)
            pl.when(first)(start)
        body(*ins, *outs, *scratch)
        pl.when(last)(finish)

    hbm = pl.BlockSpec(memory_space=pl.ANY)
    return pl.pallas_call(
        hosting, name=name, out_shape=tuple(out_shape) + (result,), grid=grid, in_specs=list(in_specs) + [hbm],
        out_specs=tuple(out_specs) + (hbm,), scratch_shapes=list(scratch_shapes) + _COMM_SEMAPHORES,
        compiler_params=_params(has_side_effects=True),
    )(*args, payload)


def _exchange_alone(xs, make_steps, out_shapes, name):
    n = len(xs)

    def body(*refs):
        sems = refs[2 * n:]
        steps = [make_steps(refs[k], refs[n + k], *sems[3 * k:3 * k + 3]) for k in range(n)]
        for stage in zip(*steps):
            for step in stage:
                step()

    hbm = pl.BlockSpec(memory_space=pl.ANY)
    return pl.pallas_call(body, name=name, out_shape=tuple(out_shapes), in_specs=[hbm] * n, out_specs=(hbm,) * n,
                          scratch_shapes=_COMM_SEMAPHORES * n, compiler_params=pltpu.CompilerParams(has_side_effects=True))(*xs)


def _all_gather(xs, name):
    return _exchange_alone(xs, _gather_steps, [SDS((N_DEV,) + x.shape, x.dtype) for x in xs], name)


def _all_to_all(xs, name):
    return _exchange_alone(xs, _exchange_steps, [SDS(x.shape, x.dtype) for x in xs], name)


BIG = ("w_in", "cv_pw_w", "w_out", "x_wq", "x_wk", "x_wv", "x_wo", "ffn_w_up", "ffn_w_down")
_MIXER = (("w_in", 352), ("w_out", 128))
_CROSS = (("x_wq", 128), ("x_wk", 128), ("x_wv", 128), ("x_wo", 128))
GROUPS = {
    "a": tuple((n, 0, r) for n, r in _MIXER),
    "m1": tuple((n, 1, r) for n, r in _MIXER),
    "bx": tuple((n, 0, r) for n, r in _CROSS),
    "cx": tuple((n, 1, r) for n, r in _CROSS),
    "bf": (("ffn_w_down", 0, 352),),
    "cf": (("ffn_w_down", 1, 352),),
}
GRADIENT_GROUPS = ("a", "m1", "bf", "bx", "cf", "cx")
TRANSPOSED = ("w_in", "ffn_w_up")
PW_ROWS = 16
ADAMW_ROW_BLOCK = 32


def _group_rows(group):
    out, first = {}, 0
    for n, l, r in GROUPS[group]:
        out[(n, l)] = (first, r)
        first += r
    return out


def _where_is(name, layer):
    for group in GRADIENT_GROUPS:
        rows = _group_rows(group)
        if (name, layer) in rows:
            return (group,) + rows[(name, layer)]
    raise KeyError((name, layer))


def _pack_weights(group, wts):
    pieces = []
    for n, l, _ in GROUPS[group]:
        w = wts[n][l].astype(BF16)
        pieces.append(w.T if n in TRANSPOSED else w)
    if group == "a":
        pieces.append(wts["cv_pw_w"].astype(BF16).reshape(PW_ROWS, PAYLOAD_COLS))
    return jnp.concatenate(pieces, axis=0)


def _unpack_weights(group, gathered):
    return {key: gathered[:, first:first + r, :].reshape(N_DEV * r, PAYLOAD_COLS)
            for key, (first, r) in _group_rows(group).items()}


def _pack_grads(group, grads):
    pieces = []
    for n, l, r in GROUPS[group]:
        g = grads[n][l]
        parts = g if isinstance(g, tuple) else (g,)
        pieces.append(jnp.concatenate([p.reshape(-1, r, PAYLOAD_COLS) for p in parts], axis=0))
    if group == "a":
        pieces.append(_to_shards("cv_pw_w", jnp.stack(grads["cv_pw_w"])).reshape(N_DEV, PW_ROWS, PAYLOAD_COLS))
    return jnp.concatenate(pieces, axis=1)


COL_SHARDED = ("w_in", "ffn_w_up", "cv_w", "ffn_conv_w")
SMALL_SHARDED = ("cv_w", "ffn_conv_w")
REPLICATED = ("mix_norm_pre", "cv_b", "cv_ln_g", "cv_ln_b", "cv_pw_b", "mix_norm_post", "x_norm_pre", "mem_norm",
              "x_norm_post", "ffn_norm_pre", "ffn_conv_b", "ffn_norm_post")
WEIGHTS = ("mix_norm_pre", "w_in", "cv_w", "cv_b", "cv_ln_g", "cv_ln_b", "cv_pw_w", "cv_pw_b", "w_out", "mix_norm_post",
           "x_norm_pre", "mem_norm", "x_wq", "x_wk", "x_wv", "x_wo", "x_norm_post", "ffn_norm_pre", "ffn_w_up",
           "ffn_conv_w", "ffn_conv_b", "ffn_w_down", "ffn_norm_post")
PAYLOAD_COLS = 1024


PAYLOAD_ROW_TILE = 16


def _pad_rows(flat, cols):
    n = flat.shape[-1]
    rows = -(-n // (cols * PAYLOAD_ROW_TILE)) * PAYLOAD_ROW_TILE
    pad = rows * cols - n
    if pad:
        flat = jnp.concatenate([flat, jnp.zeros(flat.shape[:-1] + (pad,), flat.dtype)], axis=-1)
    return flat.reshape(flat.shape[:-1] + (rows, cols))


def _unshard(name, parts):
    n, depth, r, c = parts.shape
    if name in COL_SHARDED:
        return parts.transpose(1, 2, 0, 3).reshape(depth, r, n * c)
    return parts.transpose(1, 0, 2, 3).reshape(depth, n * r, c)


def _to_shards(name, full):
    depth, r, c = full.shape
    if name in COL_SHARDED:
        return full.reshape(depth, r, N_DEV, c // N_DEV).transpose(2, 0, 1, 3).reshape(N_DEV, -1)
    return full.reshape(depth, N_DEV, r // N_DEV, c).transpose(1, 0, 2, 3).reshape(N_DEV, -1)


def _heads_major(x, h):
    return x.reshape(x.shape[0], h, HEAD_DIM).transpose(1, 0, 2)


def _tokens_major(x):
    return x.transpose(1, 0, 2).reshape(x.shape[1], -1)


def _ffn_halves(p):
    w, b = p["ffn_conv_w"], p["ffn_conv_b"]
    return w[:, :D_FF], w[:, D_FF:], b[:, :D_FF], b[:, D_FF:]


def _layer_fwd(l, h, hn, p, mem, cos, sin, g_next, payload, unpack, ffn_shards, ffn_payload):
    p = dict(p)
    sv = {"h0": h, "hn0": hn}
    u = _mm(hn, p["w_in"], "nt", F32, f"l{l}_in_proj")
    sv["u"] = u
    sb = _heads_major(u[:, :3 * SB_WIDTH].astype(BF16), 3 * SB_HEADS)
    sb_q, sb_k, sb_v = sb[:SB_HEADS], sb[SB_HEADS:2 * SB_HEADS], sb[2 * SB_HEADS:]
    a_out, sb_tot, sb_first, gathered = _sb_fwd(sb_q, sb_k, sb_v, payload, f"l{l}_sb_fwd")
    p.update(unpack(gathered))
    sv.update(sb_q=sb_q, sb_k=sb_k, sb_v=sb_v, sb_tot=sb_tot, sb_first=sb_first, p=p)

    cv_s, cv_c = _cv_fwd(u, p["cv_w"], p["cv_b"], p["cv_ln_g"], p["cv_ln_b"], f"l{l}_cv_fwd")
    b_out = _mm(cv_s, p["cv_pw_w"], "nn", BF16, f"l{l}_cv_pw", bias=p["cv_pw_b"])
    sv.update(cv_s=cv_s, cv_c=cv_c)

    qk = _rope_fwd(u, cos, sin, f"l{l}_rope_fwd")
    up_t = ffn_shards[0].astype(BF16).T
    half_rows = up_t.shape[0] // 2
    carried = (up_t[:half_rows], up_t[half_rows:], ffn_shards[1].astype(BF16))
    outs, lses, got = [], [], []
    for b, (_, dil) in enumerate(DL_PATTERN):
        o, lse, gathered = _dl_fwd(qk, u, dil, f"l{l}_dl{b}_fwd", carried[b])
        outs.append(o)
        lses.append(lse)
        got.append(gathered)
    up_blocks = jnp.concatenate(got[:2], axis=1)
    half = N_DEV // 2
    p["ffn_w_up"] = (up_blocks[:half].reshape(-1, PAYLOAD_COLS), up_blocks[half:].reshape(-1, PAYLOAD_COLS))
    p["ffn_w_down"] = got[2].reshape(-1, PAYLOAD_COLS)
    c_out, c_out_f32, w1, w2, w3 = _dl_mix_fwd(outs, lses, f"l{l}_dl_mix")
    sv.update(dl_qk=qk, dl_lse=lses, dl_o=c_out_f32, dl_w=(w1, w2, w3))

    mix = jnp.concatenate([_tokens_major(a_out), b_out, c_out], axis=-1)
    y = _mm(mix, p["w_out"], "nn", F32, f"l{l}_out_proj")
    h1, hn1 = _res_norm_fwd(h, y, p["mix_norm_post"], p["x_norm_pre"], f"l{l}_mix_post")
    sv.update(mix=mix, y_mix=y, h1=h1, hn1=hn1)

    xq = _mm(hn1, p["x_wq"], "nn", BF16, f"l{l}_xq")
    memn = _rms_fwd(mem, p["mem_norm"], f"l{l}_mem_norm")
    xk = _mm(memn, p["x_wk"], "nn", BF16, f"l{l}_xk")
    xv = _mm(memn, p["x_wv"], "nn", BF16, f"l{l}_xv")
    xo = _xattn_fwd(xq, xk, xv, f"l{l}_xattn_fwd")
    y = _mm(xo, p["x_wo"], "nn", F32, f"l{l}_xo_proj")
    h2, hn2 = _res_norm_fwd(h1, y, p["x_norm_post"], p["ffn_norm_pre"], f"l{l}_x_post")
    sv.update(xq=xq, xk=xk, xv=xv, xo=xo, memn=memn, y_x=y, h2=h2, hn2=hn2)

    up_g = _mm(hn2, p["ffn_w_up"][0], "nt", F32, f"l{l}_ffn_up_gate")
    up_v = _mm(hn2, p["ffn_w_up"][1], "nt", F32, f"l{l}_ffn_up_val")
    act, *ffn_gathered = _ffn_act_fwd(up_g, up_v, *_ffn_halves(p), f"l{l}_ffn_act", ffn_payload)
    y = _mm(act, p["ffn_w_down"], "nn", F32, f"l{l}_ffn_down")
    h3, hn3 = _res_norm_fwd(h2, y, p["ffn_norm_post"], g_next, f"l{l}_ffn_post")
    sv.update(up_g=up_g, up_v=up_v, act=act, y_ffn=y)
    return h3, hn3, sv, (ffn_gathered[0] if ffn_gathered else None)


def _layer_bwd(l, dh, dy, sv, mem, cos, sin, prev_post, ffn_payload, pack):
    p = sv["p"]
    gr = {}
    received = {}
    dact = _mm(dy, p["ffn_w_down"], "nt", F32, f"l{l}_d_act")
    gr["ffn_w_down"] = _mm(sv["act"], dy, "tn", BF16, f"l{l}_dw_down")
    dup_g, dup_v, dwg, dwv, dbg, dbv, *got = _ffn_act_bwd(sv["up_g"], sv["up_v"], dact, *_ffn_halves(p), f"l{l}_ffn_act_bwd",
                                                         ffn_payload)
    if got:
        received["ffn_payload"] = got[0]
    gr["ffn_conv_w"] = jnp.concatenate([dwg, dwv], axis=1)
    gr["ffn_conv_b"] = jnp.concatenate([dbg, dbv], axis=1)
    dhn = (_mm(dup_g, p["ffn_w_up"][0], "nn", F32, f"l{l}_d_hn2_gate"), _mm(dup_v, p["ffn_w_up"][1], "nn", F32, f"l{l}_d_hn2_val"))
    gr["ffn_w_up"] = (_mm(dup_g, sv["hn2"], "tn", BF16, f"l{l}_dw_up_gate"), _mm(dup_v, sv["hn2"], "tn", BF16, f"l{l}_dw_up_val"))
    dh, dy, gr["ffn_norm_pre"], gr["x_norm_post"] = _norm_bwd(
        dh, (sv["h2"], p["ffn_norm_pre"], dhn), (sv["y_x"], p["x_norm_post"]), f"l{l}_x_post_bwd")

    do = _mm(dy, p["x_wo"], "nt", BF16, f"l{l}_d_xo")
    gr["x_wo"] = _mm(sv["xo"], dy, "tn", BF16, f"l{l}_dw_xo")
    dq, dk, dv = _xattn_bwd(sv["xq"], sv["xk"], sv["xv"], do, f"l{l}_xattn_bwd")
    dhn = _mm(dq, p["x_wq"], "nt", F32, f"l{l}_d_hn1")
    gr["x_wq"] = _mm(sv["hn1"], dq, "tn", BF16, f"l{l}_dw_xq")
    gr["x_wk"] = _mm(sv["memn"], dk, "tn", BF16, f"l{l}_dw_xk")
    gr["x_wv"] = _mm(sv["memn"], dv, "tn", BF16, f"l{l}_dw_xv")
    dmemn = _mm(dk, p["x_wk"], "nt", F32, f"l{l}_d_memn_k") + _mm(dv, p["x_wv"], "nt", F32, f"l{l}_d_memn_v")
    gr["mem_norm"] = _rms_gain_grad(mem, p["mem_norm"], dmemn, f"l{l}_mem_norm_bwd")
    dh, dy, gr["x_norm_pre"], gr["mix_norm_post"] = _norm_bwd(
        dh, (sv["h1"], p["x_norm_pre"], dhn), (sv["y_mix"], p["mix_norm_post"]), f"l{l}_mix_post_bwd")

    dmix = _mm(dy, p["w_out"], "nt", F32, f"l{l}_d_mix")
    gr["w_out"] = _mm(sv["mix"], dy, "tn", BF16, f"l{l}_dw_out")
    do_a = _heads_major(dmix[:, :SB_WIDTH].astype(BF16), SB_HEADS)
    dq, dk, dv, received["ffn"] = _sb_bwd(sv["sb_q"], sv["sb_k"], sv["sb_v"], do_a, sv["sb_tot"], sv["sb_first"],
                                          pack("ffn", gr), f"l{l}_sb_bwd")
    du_sb = _tokens_major(jnp.concatenate([dq, dk, dv], axis=0))

    db_out = dmix[:, SB_WIDTH:SB_WIDTH + CV_WIDTH]
    ds = _mm(db_out, p["cv_pw_w"], "nt", F32, f"l{l}_d_cv_s")
    gr["cv_pw_w"] = _mm(sv["cv_s"], db_out, "tn", BF16, f"l{l}_dw_cv_pw")
    du_cv, dcvw, gr["cv_b"], gr["cv_ln_g"], gr["cv_ln_b"], gr["cv_pw_b"] = _cv_bwd(
        sv["u"], sv["cv_c"], ds, db_out, p["cv_w"], p["cv_ln_g"], p["cv_ln_b"], f"l{l}_cv_bwd")
    gr["cv_w"] = dcvw[:CV_KERNEL]

    dqs, dks, dvs = [], [], []
    up_rows = jnp.concatenate([g.reshape(N_DEV // 2, -1, PAYLOAD_COLS) for g in gr["ffn_w_up"]], axis=0)
    half_rows = up_rows.shape[1] // 2
    carried = {"up0": up_rows[:, :half_rows], "up1": up_rows[:, half_rows:], "cross": pack("cross", gr)}
    for b, ((_, dil), what) in enumerate(zip(DL_PATTERN, carried)):
        dq, dk, dv, received[what] = _dl_bwd(sv["dl_qk"], sv["u"], dmix, sv["dl_o"], sv["dl_w"][b], sv["dl_lse"][b], dil,
                                             f"l{l}_dl{b}_bwd", carried[what])
        dqs.append(dq)
        dks.append(dk)
        dvs.append(dv)
    du_dl = _rope_bwd(dqs, dks, dvs, cos, sin, f"l{l}_rope_bwd")

    du = jnp.concatenate([du_sb, du_cv, du_dl], axis=-1)
    dhn = _mm(du, p["w_in"], "nn", F32, f"l{l}_d_hn0")
    gr["w_in"] = _mm(du, sv["hn0"], "tn", BF16, f"l{l}_dw_in")
    dh, dy, gr["mix_norm_pre"], dg_prev = _norm_bwd(dh, (sv["h0"], p["mix_norm_pre"], dhn), prev_post, f"l{l}_in_bwd")
    return dh, dy, gr, dg_prev, received


def kernel(x, mem, positions, mix_norm_pre, w_in, cv_w, cv_b, cv_ln_g, cv_ln_b, cv_pw_w, cv_pw_b, w_out, mix_norm_post, x_norm_pre, mem_norm, x_wq, x_wk, x_wv, x_wo, x_norm_post, ffn_norm_pre, ffn_w_up, ffn_conv_w, ffn_conv_b, ffn_w_down, ffn_norm_post, loss_target, m_mix_norm_pre, m_w_in, m_cv_w, m_cv_b, m_cv_ln_g, m_cv_ln_b, m_cv_pw_w, m_cv_pw_b, m_w_out, m_mix_norm_post, m_x_norm_pre, m_mem_norm, m_x_wq, m_x_wk, m_x_wv, m_x_wo, m_x_norm_post, m_ffn_norm_pre, m_ffn_w_up, m_ffn_conv_w, m_ffn_conv_b, m_ffn_w_down, m_ffn_norm_post, v_mix_norm_pre, v_w_in, v_cv_w, v_cv_b, v_cv_ln_g, v_cv_ln_b, v_cv_pw_w, v_cv_pw_b, v_w_out, v_mix_norm_post, v_x_norm_pre, v_mem_norm, v_x_wq, v_x_wk, v_x_wv, v_x_wo, v_x_norm_post, v_ffn_norm_pre, v_ffn_w_up, v_ffn_conv_w, v_ffn_conv_b, v_ffn_w_down, v_ffn_norm_post):
    args = locals()
    wts = {n: args[n] for n in WEIGHTS}
    mom = {n: args["m_" + n] for n in WEIGHTS}
    var = {n: args["v_" + n] for n in WEIGHTS}

    x2, mem2, target = x[0], mem[0], loss_target[0]

    small_payload = _pad_rows(jnp.concatenate([wts[n].reshape(-1) for n in SMALL_SHARDED]), PAYLOAD_COLS)
    gathered_a, small = _all_gather([_pack_weights("a", wts), small_payload], "weights_all_gather")
    small = small.reshape(N_DEV, -1)
    small_full = {}
    off = 0
    for n in SMALL_SHARDED:
        size = wts[n].size
        small_full[n] = _unshard(n, small[:, off:off + size].reshape((N_DEV,) + wts[n].shape))
        off += size
    pw_first = sum(r for _, _, r in GROUPS["a"])
    pw_full = _unshard("cv_pw_w", gathered_a[:, pw_first:, :].reshape((N_DEV,) + wts["cv_pw_w"].shape))

    def mixer_params(l, unpacked):
        p = {n: wts[n][l][None, :] for n in REPLICATED}
        p.update({n: small_full[n][l] for n in SMALL_SHARDED})
        p.update(cv_pw_w=pw_full[l], w_in=unpacked[("w_in", l)], w_out=unpacked[("w_out", l)])
        return p

    def of_layer(group, l):
        return lambda gathered: {n: w for (n, ll), w in _unpack_weights(group, gathered).items() if ll == l}

    pos = positions[0].astype(F32)
    half = HEAD_DIM // 2
    inv_freq = ROPE_THETA ** (-jnp.arange(half, dtype=F32) / half)
    ang = pos[:, None] * inv_freq
    cos = jnp.tile(jnp.cos(ang), (1, LANES // half))
    sin = jnp.tile(jnp.sin(ang), (1, LANES // half))

    p0 = mixer_params(0, _unpack_weights("a", gathered_a))
    hn = _rms_fwd(x2, p0["mix_norm_pre"], "l0_in_norm")
    ffn_shards = [(wts["ffn_w_up"][l], wts["ffn_w_down"][l]) for l in range(DEPTH)]
    h, hn, sv0, gathered_m1 = _layer_fwd(0, x2, hn, p0, mem2, cos, sin, wts["mix_norm_pre"][1][None, :],
                                         _pack_weights("bx", wts), of_layer("bx", 0), ffn_shards[0], _pack_weights("m1", wts))
    p1 = mixer_params(1, _unpack_weights("m1", gathered_m1))
    h, _, sv1, _ = _layer_fwd(1, h, hn, p1, mem2, cos, sin, None, _pack_weights("cx", wts), of_layer("cx", 1), ffn_shards[1],
                              None)
    loss_part, dh = _loss_fwd(h, target, "loss")
    loss = lax.psum(loss_part[0, 0], ("x", "y", "c"))

    grads = {n: [None] * DEPTH for n in WEIGHTS}
    dh, dy, _, grads["ffn_norm_post"][1] = _norm_bwd(dh, None, (sv1["y_ffn"], sv1["p"]["ffn_norm_post"]), "last_post_bwd")

    def packer(l, groups):
        return lambda which, gr: _pack_grads(groups[which], {n: {l: g} for n, g in gr.items()})

    dh, dy, gr, grads["ffn_norm_post"][0], got1 = _layer_bwd(
        1, dh, dy, sv1, mem2, cos, sin, (sv0["y_ffn"], sv0["p"]["ffn_norm_post"]), None, packer(1, {"ffn": "cf", "cross": "cx"}))
    for n, g in gr.items():
        grads[n][1] = g
    dh, _, gr, _, got0 = _layer_bwd(0, dh, dy, sv0, mem2, cos, sin, None, _pack_grads("m1", grads),
                                    packer(0, {"ffn": "bf", "cross": "bx"}))
    for n, g in gr.items():
        grads[n][0] = g
    grad_x = dh

    small_rows = jnp.concatenate([_to_shards(n, jnp.stack(grads[n])) for n in SMALL_SHARDED], axis=1)
    rep_flat = jnp.concatenate([jnp.stack([g.reshape(-1) for g in grads[n]]).reshape(-1) for n in REPLICATED])
    rep_rows = jnp.broadcast_to(rep_flat[None], (N_DEV, rep_flat.shape[0]))
    f32_rows = _pad_rows(jnp.concatenate([small_rows, rep_rows], axis=1), PAYLOAD_COLS)
    received_a, small_parts = _all_to_all([_pack_grads("a", grads), f32_rows], "grads_all_to_all")
    received = {"a": received_a, "m1": got0["ffn_payload"], "bf": got0["ffn"], "bx": got0["cross"],
                "cf": got1["ffn"], "cx": got1["cross"]}

    res = {}
    for n in BIG:
        shape = wts[n].shape
        two_d = (shape[0] * shape[1], shape[2])
        operands = (wts[n].reshape(two_d), mom[n].reshape(two_d), var[n].reshape(two_d))
        if n == "cv_pw_w":
            outs = _adamw(received_a[:, pw_first:, :].reshape((N_DEV,) + two_d), *operands, f"adamw_{n}")
        elif n in TRANSPOSED:
            layers = []
            for l, got in enumerate((got0, got1)):
                if n == "ffn_w_up":
                    layers.append(jnp.concatenate([got["up0"], got["up1"]], axis=1))
                else:
                    group, first, r = _where_is(n, l)
                    layers.append(received[group][:, first:first + r, :])
            parts = jnp.stack(layers, axis=1).transpose(0, 1, 3, 2).reshape((N_DEV,) + two_d)
            outs = _adamw(parts, *operands, f"adamw_{n}")
        else:
            sources = []
            for l in range(DEPTH):
                group, first, _ = _where_is(n, l)
                sources.append((received[group], first))
            outs = _adamw_packed(sources, *operands, f"adamw_{n}")
        res[n] = [o.reshape(shape) for o in outs]
    small_names = SMALL_SHARDED + REPLICATED
    flat_w = _pad_rows(jnp.concatenate([wts[n].reshape(-1) for n in small_names]), PAYLOAD_COLS)
    flat_m = _pad_rows(jnp.concatenate([mom[n].reshape(-1) for n in small_names]), PAYLOAD_COLS)
    flat_v = _pad_rows(jnp.concatenate([var[n].reshape(-1) for n in small_names]), PAYLOAD_COLS)
    outs = _adamw(small_parts, flat_w, flat_m, flat_v, "adamw_small")
    outs = [o.reshape(-1) for o in outs]
    off = 0
    for n in small_names:
        size = wts[n].size
        res[n] = [o[off:off + size].reshape(wts[n].shape) for o in outs]
        off += size

    result = [loss, grad_x[None]]
    for kind in range(4):
        result += [res[n][kind] for n in WEIGHTS]
    return tuple(result)
```

```python
import functools
import math

import jax
import jax.numpy as jnp
from jax import lax
from jax.experimental import pallas as pl
from jax.experimental.pallas import tpu as pltpu

F32, BF16 = jnp.float32, jnp.bfloat16
SDS = jax.ShapeDtypeStruct

D_MODEL = 1024
SEQ = 4096
DEPTH = 2
HEAD_DIM = 64
SB_HEADS = 4
SB_WIDTH = 256
CV_WIDTH = 256
CV_KERNEL = 31
DL_HEADS = 8
DL_WIDTH = 512
IN_WIDTH = 2816
DL_PATTERN = ((128, 1), (512, 4), (2048, 16))
BLOCK = 128
ROPE_THETA = 10000.0
N_MEM = 256
X_HEADS = 4
X_HEAD_DIM = 256
D_FF = 2816
EPS = 1e-6
N_DEV = 8
LANES = 128

ADAM_LR = 0.001
ADAM_B1 = 0.9
ADAM_B2 = 0.999
ADAM_EPS = 1e-08
ADAM_WD = 0.01
ADAM_STEP = 10

VMEM_LIMIT_BYTES = 56 * 1024 * 1024
MESH = pl.DeviceIdType.MESH
NEG = -1e30


def _params(**kw):
    return pltpu.CompilerParams(vmem_limit_bytes=VMEM_LIMIT_BYTES, **kw)


def _pick(n, cands):
    for c in cands:
        if n % c == 0:
            return c
    return n


def _mm(a, b, mode, out_dtype, name, bias=None):
    if mode == "nn":
        (m, k), (k2, n) = a.shape, b.shape
    elif mode == "nt":
        (m, k), (n, k2) = a.shape, b.shape
    else:
        (k, m), (k2, n) = a.shape, b.shape
    assert k == k2, (a.shape, b.shape, mode)
    tm = _pick(m, (1024, 1408, 512, 256, 128))
    tn = _pick(n, (1024, 1408, 512, 256, 128))
    tk = k if k <= 2048 else _pick(k, (2048, 1408, 1024, 512))
    nk = k // tk
    dims = {"nn": ((1,), (0,)), "nt": ((1,), (1,)), "tn": ((0,), (0,))}[mode]

    def body(*refs):
        refs = list(refs)
        acc_ref = refs.pop() if nk > 1 else None
        a_ref, b_ref = refs[0], refs[1]
        bias_ref = refs[2] if bias is not None else None
        o_ref = refs[-1]
        p = lax.dot_general(a_ref[...].astype(BF16), b_ref[...].astype(BF16), (dims, ((), ())),
                            preferred_element_type=F32)

        def finish(v):
            if bias_ref is not None:
                v = v + bias_ref[...]
            o_ref[...] = v.astype(out_dtype)

        if nk == 1:
            finish(p)
        else:
            kk = pl.program_id(2)

            @pl.when(kk == 0)
            def _():
                acc_ref[...] = p

            @pl.when(kk > 0)
            def _():
                acc_ref[...] += p

            @pl.when(kk == nk - 1)
            def _():
                finish(acc_ref[...])

    a_spec = pl.BlockSpec((tk, tm), lambda i, j, kk: (kk, i)) if mode == "tn" else pl.BlockSpec((tm, tk), lambda i, j, kk: (i, kk))
    b_spec = pl.BlockSpec((tn, tk), lambda i, j, kk: (j, kk)) if mode == "nt" else pl.BlockSpec((tk, tn), lambda i, j, kk: (kk, j))
    in_specs = [a_spec, b_spec]
    args = [a, b]
    if bias is not None:
        in_specs.append(pl.BlockSpec((1, tn), lambda i, j, kk: (0, j)))
        args.append(bias)
    return pl.pallas_call(
        body, name=name, out_shape=SDS((m, n), out_dtype), grid=(m // tm, n // tn, nk),
        in_specs=in_specs, out_specs=pl.BlockSpec((tm, tn), lambda i, j, kk: (i, j)),
        scratch_shapes=[pltpu.VMEM((tm, tn), F32)] if nk > 1 else [], compiler_params=_params(),
    )(*args)


def _rms(x, g):
    r = lax.rsqrt(jnp.mean(x * x, axis=-1, keepdims=True) + EPS)
    return x * r * g


def _rms_bwd(x, g, dy):
    r = lax.rsqrt(jnp.mean(x * x, axis=-1, keepdims=True) + EPS)
    xh = x * r
    dyg = dy * g
    dx = r * (dyg - xh * jnp.mean(dyg * xh, axis=-1, keepdims=True))
    return dx, dy * xh


def _rms_fwd(x, g, name):
    rows, d = x.shape
    t = min(rows, 512)

    def body(x_ref, g_ref, o_ref):
        o_ref[...] = _rms(x_ref[...], g_ref[...]).astype(BF16)

    return pl.pallas_call(
        body, name=name, out_shape=SDS((rows, d), BF16), grid=(rows // t,),
        in_specs=[pl.BlockSpec((t, d), lambda i: (i, 0)), pl.BlockSpec((1, d), lambda i: (0, 0))],
        out_specs=pl.BlockSpec((t, d), lambda i: (i, 0)), compiler_params=_params(),
    )(x, g)


def _res_norm_fwd(h, y, g_post, g_next, name):
    rows, d = h.shape
    t = 512
    has_next = g_next is not None

    def body(*refs):
        if has_next:
            h_ref, y_ref, gp_ref, gn_ref, h1_ref, hn_ref = refs
        else:
            h_ref, y_ref, gp_ref, h1_ref = refs
        h1 = h_ref[...] + _rms(y_ref[...], gp_ref[...])
        h1_ref[...] = h1
        if has_next:
            hn_ref[...] = _rms(h1, gn_ref[...]).astype(BF16)

    row = pl.BlockSpec((t, d), lambda i: (i, 0))
    vec = pl.BlockSpec((1, d), lambda i: (0, 0))
    in_specs = [row, row, vec] + ([vec] if has_next else [])
    args = [h, y, g_post] + ([g_next] if has_next else [])
    out_shape = [SDS((rows, d), F32)] + ([SDS((rows, d), BF16)] if has_next else [])
    out_specs = [row] + ([row] if has_next else [])
    res = pl.pallas_call(body, name=name, out_shape=out_shape, grid=(rows // t,), in_specs=in_specs,
                         out_specs=out_specs, compiler_params=_params())(*args)
    return (res[0], res[1]) if has_next else (res[0], None)


def _norm_bwd(dh, pre, post, name):
    rows, d = dh.shape
    t = 512
    has_pre, has_post = pre is not None, post is not None
    if has_pre:
        dhns = pre[2] if isinstance(pre[2], tuple) else (pre[2],)
        pre = (pre[0], pre[1]) + dhns

    def body(*refs):
        refs = list(refs)
        dh_ref = refs.pop(0)
        if has_pre:
            h_ref, gpre_ref = refs.pop(0), refs.pop(0)
            dhn_refs = [refs.pop(0) for _ in dhns]
        if has_post:
            y_ref, gpost_ref = refs.pop(0), refs.pop(0)
        dht_ref = refs.pop(0)
        if has_post:
            dy_ref = refs.pop(0)
        if has_pre:
            dgpre_ref = refs.pop(0)
        if has_post:
            dgpost_ref = refs.pop(0)
        i = pl.program_id(0)
        dht = dh_ref[...]
        if has_pre:
            dhn = dhn_refs[0][...]
            for r in dhn_refs[1:]:
                dhn = dhn + r[...]
            dx, dgr = _rms_bwd(h_ref[...], gpre_ref[...], dhn)
            dht = dht + dx

            @pl.when(i == 0)
            def _():
                dgpre_ref[...] = jnp.zeros_like(dgpre_ref)

            dgpre_ref[...] += jnp.sum(dgr, axis=0, keepdims=True)
        dht_ref[...] = dht
        if has_post:
            dy, dgr = _rms_bwd(y_ref[...], gpost_ref[...], dht)
            dy_ref[...] = dy.astype(BF16)

            @pl.when(i == 0)
            def _():
                dgpost_ref[...] = jnp.zeros_like(dgpost_ref)

            dgpost_ref[...] += jnp.sum(dgr, axis=0, keepdims=True)

    row = pl.BlockSpec((t, d), lambda i: (i, 0))
    vec = pl.BlockSpec((1, d), lambda i: (0, 0))
    in_specs, args = [row], [dh]
    if has_pre:
        in_specs += [row, vec] + [row] * len(dhns)
        args += list(pre)
    if has_post:
        in_specs += [row, vec]
        args += list(post)
    out_shape, out_specs = [SDS((rows, d), F32)], [row]
    if has_post:
        out_shape.append(SDS((rows, d), BF16))
        out_specs.append(row)
    if has_pre:
        out_shape.append(SDS((1, d), F32))
        out_specs.append(vec)
    if has_post:
        out_shape.append(SDS((1, d), F32))
        out_specs.append(vec)
    res = list(pl.pallas_call(body, name=name, out_shape=out_shape, grid=(rows // t,), in_specs=in_specs,
                              out_specs=out_specs, compiler_params=_params())(*args))
    dht = res.pop(0)
    dy = res.pop(0) if has_post else None
    dgpre = res.pop(0) if has_pre else None
    dgpost = res.pop(0) if has_post else None
    return dht, dy, dgpre, dgpost


def _rms_gain_grad(x, g, dy, name):
    rows, d = x.shape

    def body(x_ref, g_ref, dy_ref, dg_ref):
        _, dgr = _rms_bwd(x_ref[...], g_ref[...], dy_ref[...])
        dg_ref[...] = jnp.sum(dgr, axis=0, keepdims=True)

    return pl.pallas_call(body, name=name, out_shape=SDS((1, d), F32), compiler_params=_params())(x, g, dy)


def _loss_fwd(h, target, name):
    rows, d = h.shape
    t = 512

    def body(h_ref, t_ref, loss_ref, dh_ref):
        i = pl.program_id(0)
        err = h_ref[...] - t_ref[...]
        dh_ref[...] = err * (1.0 / d)

        @pl.when(i == 0)
        def _():
            loss_ref[...] = jnp.zeros_like(loss_ref)

        part = jnp.sum(jnp.sum(err * err, axis=1, keepdims=True), axis=0, keepdims=True) * (0.5 / d)
        loss_ref[...] += jnp.broadcast_to(part, loss_ref.shape)

    row = pl.BlockSpec((t, d), lambda i: (i, 0))
    return pl.pallas_call(
        body, name=name, out_shape=(SDS((1, LANES), F32), SDS((rows, d), F32)), grid=(rows // t,),
        in_specs=[row, row], out_specs=(pl.BlockSpec((1, LANES), lambda i: (0, 0)), row), compiler_params=_params(),
    )(h, target)


def _rot_half(x, sign):
    w = x.shape[-1]
    lane = lax.broadcasted_iota(jnp.int32, x.shape, 1)
    first = (lane % HEAD_DIM) < (HEAD_DIM // 2)
    return jnp.where(first, -sign * pltpu.roll(x, w - HEAD_DIM // 2, axis=1), sign * pltpu.roll(x, HEAD_DIM // 2, axis=1))


def _rope_fwd(u, cos, sin, name, payload=None):
    rows = u.shape[0]
    t, cw = 512, 256
    first_col = (3 * SB_WIDTH + 2 * CV_WIDTH) // cw

    def body(u_ref, c_ref, s_ref, o_ref):
        x = u_ref[...]
        c = jnp.tile(c_ref[...], (1, cw // LANES))
        s = jnp.tile(s_ref[...], (1, cw // LANES))
        o_ref[...] = x * c + _rot_half(x, 1.0) * s

    tab = pl.BlockSpec((t, LANES), lambda i, j: (i, 0))
    grid = (rows // t, 2 * DL_WIDTH // cw)
    if payload is not None:
        def when():
            first = (pl.program_id(0) == 0) & (pl.program_id(1) == 0)
            last = (pl.program_id(0) == grid[0] - 1) & (pl.program_id(1) == grid[1] - 1)
            return first, last, last
        payload = ("gather", payload, when)
    return _hosted_call(
        body, payload, name=name, out_shape=(SDS((rows, 2 * DL_WIDTH), F32),), grid=grid,
        in_specs=[pl.BlockSpec((t, cw), lambda i, j: (i, first_col + j)), tab, tab],
        out_specs=(pl.BlockSpec((t, cw), lambda i, j: (i, j)),), args=(u, cos, sin))


def _rope_bwd(dqs, dks, dvs, cos, sin, name):
    rows = dqs[0].shape[0]
    t, w = 256, DL_WIDTH

    def body(*refs):
        c = jnp.tile(refs[9][...], (1, w // LANES))
        s = jnp.tile(refs[10][...], (1, w // LANES))
        o_ref = refs[11]
        dq = refs[0][...] + refs[1][...] + refs[2][...]
        dk = refs[3][...] + refs[4][...] + refs[5][...]
        dv = refs[6][...] + refs[7][...] + refs[8][...]
        o_ref[:, 0:w] = (dq * c + _rot_half(dq, -1.0) * s).astype(BF16)
        o_ref[:, w:2 * w] = (dk * c + _rot_half(dk, -1.0) * s).astype(BF16)
        o_ref[:, 2 * w:3 * w] = dv.astype(BF16)

    row = pl.BlockSpec((t, w), lambda i: (i, 0))
    tab = pl.BlockSpec((t, LANES), lambda i: (i, 0))
    return pl.pallas_call(
        body, name=name, out_shape=SDS((rows, 3 * w), BF16), grid=(rows // t,), in_specs=[row] * 9 + [tab, tab],
        out_specs=pl.BlockSpec((t, 3 * w), lambda i: (i, 0)), compiler_params=_params(),
    )(*dqs, *dks, *dvs, cos, sin)


SB_TILE = 256
SB_ZERO_AFTER = 110.0
SB_FIRST_BLOCK = (8, LANES)


def _softplus(z):
    return jnp.maximum(z, 0.0) + jnp.log(1.0 + jnp.exp(-jnp.abs(z)))


def _split_dot(x, tri, passes):
    acc = None
    rem = x
    for _ in range(passes):
        part = rem.astype(BF16)
        rem = rem - part.astype(F32)
        d = jnp.dot(part, tri, preferred_element_type=F32)
        acc = d if acc is None else acc + d
    return acc


def _tri(t, rel):
    j = lax.broadcasted_iota(jnp.int32, (t, t), 0)
    s = lax.broadcasted_iota(jnp.int32, (t, t), 1)
    return rel(j, s).astype(BF16)


def _sb_masks(t, i):
    row = lax.broadcasted_iota(jnp.int32, (t, t), 0)
    col = lax.broadcasted_iota(jnp.int32, (t, t), 1)
    return col < row, (row >= 0) & (i >= 1)


def _sb_fwd(q, k, v, payload, name):
    h, s_len, hd = q.shape
    t = SB_TILE
    nq = s_len // t
    scale = hd ** -0.5

    def body(q_ref, k_ref, v_ref, pay_ref, o_ref, tot_ref, first_ref, gathered_ref, send_sems, recv_sems, local_sem):
        hh, i = pl.program_id(0), pl.program_id(1)
        start, forward, finish = _gather_steps(pay_ref, gathered_ref, send_sems, recv_sems, local_sem)
        pl.when((hh == 0) & (i == 0))(start)
        pl.when((hh == h - 1) & (i == nq - 1))(forward)
        qv = q_ref[0] * scale
        upper = _tri(t, lambda j, s: j > s)

        def tiles(js, carry, masks=(None, None)):
            acc, run = carry
            starts = [pl.multiple_of(j * t, t) for j in js]
            zs = [lax.dot_general(qv, k_ref[0, pl.ds(st, t), :], (((1,), (1,)), ((), ())), preferred_element_type=F32)
                  for st in starts]
            sps = [_softplus(z) for z in zs]
            sps = [sp if m is None else jnp.where(m, sp, 0.0) for sp, m in zip(sps, masks)]
            laters = [_split_dot(sp, upper, 2) for sp in sps]
            for st, z, sp, later, m in zip(starts, zs, sps, laters, masks):
                a = jnp.exp((z - sp) - (run + later))
                if m is not None:
                    a = jnp.where(m, a, 0.0)
                acc = acc + jnp.dot(a.astype(BF16), v_ref[0, pl.ds(st, t), :], preferred_element_type=F32)
                run = run + jnp.sum(sp, axis=1, keepdims=True)
            return acc, run

        def live(carry):
            return jnp.min(carry[1]) < SB_ZERO_AFTER

        below, whole = _sb_masks(t, i)
        top = jnp.maximum(i - 1, 0)
        carry = tiles([i, top], (jnp.zeros((t, hd), F32), jnp.zeros((t, 1), F32)), (below, whole))

        def pair(state):
            pp, carry = state
            j = top - 1 - 2 * pp
            return pp + 1, tiles([j, j - 1], carry)

        pairs, carry = lax.while_loop(lambda st: (st[0] < top // 2) & live(st[1]), pair, (0, carry))
        last = ((top % 2 == 1) & (pairs == top // 2) & live(carry)).astype(jnp.int32)
        acc, run = lax.fori_loop(0, last, lambda _, c: tiles([0], c), carry)
        o_ref[0] = acc.astype(BF16)
        tot_ref[0] = run
        first_ref[...] = jnp.full(first_ref.shape, top - 2 * pairs - last, jnp.int32).astype(F32)
        pl.when((hh == h - 1) & (i == nq - 1))(finish)

    full = pl.BlockSpec((1, s_len, hd), lambda hh, i: (hh, 0, 0))
    tile = pl.BlockSpec((1, t, hd), lambda hh, i: (hh, i, 0))
    hbm = pl.BlockSpec(memory_space=pl.ANY)
    return pl.pallas_call(
        body, name=name,
        out_shape=(SDS((h, s_len, hd), BF16), SDS((h, s_len, 1), F32), SDS((h, nq) + SB_FIRST_BLOCK, F32),
                   SDS((N_DEV,) + payload.shape, payload.dtype)),
        grid=(h, nq), in_specs=[tile, full, full, hbm],
        out_specs=(tile, pl.BlockSpec((1, t, 1), lambda hh, i: (hh, i, 0)),
                   pl.BlockSpec((1, 1) + SB_FIRST_BLOCK, lambda hh, i: (hh, i, 0, 0)), hbm),
        scratch_shapes=_COMM_SEMAPHORES, compiler_params=_params(has_side_effects=True),
    )(q, k, v, payload)


def _sb_bwd(q, k, v, do, tot, first, payload, name):
    h, s_len, hd = q.shape
    t = SB_TILE
    nq = s_len // t
    scale = hd ** -0.5

    def body(q_ref, k_ref, v_ref, do_ref, tot_ref, first_ref, pay_ref, dq_ref, dk_ref, dv_ref, received_ref, dk_acc, dv_acc,
             send_sems, recv_sems, local_sem):
        hh, i = pl.program_id(0), pl.program_id(1)
        start, finish = _exchange_steps(pay_ref, received_ref, send_sems, recv_sems, local_sem)
        pl.when((hh == 0) & (i == 0))(start)

        @pl.when(i == 0)
        def _():
            dk_acc[...] = jnp.zeros_like(dk_acc)
            dv_acc[...] = jnp.zeros_like(dv_acc)

        qv = q_ref[0] * scale
        dov = do_ref[0]
        total = tot_ref[0]
        upto = _tri(t, lambda j, s: j <= s)
        before = _tri(t, lambda j, s: j < s)
        nt_dims = (((1,), (1,)), ((), ()))
        tn_dims = (((0,), (0,)), ((), ()))

        def tiles(js, carry, masks=(None, None)):
            dq, run_sp, run_g = carry
            starts = [pl.multiple_of(j * t, t) for j in js]
            zs = [lax.dot_general(qv, k_ref[0, pl.ds(st, t), :], nt_dims, preferred_element_type=F32) for st in starts]
            das = [lax.dot_general(dov, v_ref[0, pl.ds(st, t), :], nt_dims, preferred_element_type=F32) for st in starts]
            sps = [_softplus(z) for z in zs]
            log_sigs = [z - sp for z, sp in zip(zs, sps)]
            sps = [sp if m is None else jnp.where(m, sp, 0.0) for sp, m in zip(sps, masks)]
            pres = [_split_dot(sp, upto, 2) for sp in sps]
            a_s, gs = [], []
            for sp, log_sig, pre, da, m in zip(sps, log_sigs, pres, das, masks):
                a = jnp.exp(log_sig - (total - (run_sp + pre)))
                if m is not None:
                    a = jnp.where(m, a, 0.0)
                a_s.append(a)
                gs.append(a * da)
                run_sp = run_sp + jnp.sum(sp, axis=1, keepdims=True)
            g_pres = [_split_dot(g, before, 3) for g in gs]
            for st, a, g, g_pre, log_sig, m in zip(starts, a_s, gs, g_pres, log_sigs, masks):
                sig = jnp.exp(log_sig)
                dz = g * (1.0 - sig) - sig * (run_g + g_pre)
                if m is not None:
                    dz = jnp.where(m, dz, 0.0)
                dz = dz.astype(BF16)
                dq = dq + jnp.dot(dz, k_ref[0, pl.ds(st, t), :], preferred_element_type=F32)
                dk_acc[pl.ds(st, t), :] += lax.dot_general(dz, qv, tn_dims, preferred_element_type=F32)
                dv_acc[pl.ds(st, t), :] += lax.dot_general(a.astype(BF16), dov, tn_dims, preferred_element_type=F32)
                run_g = run_g + jnp.sum(g, axis=1, keepdims=True)
            return dq, run_sp, run_g

        zero = jnp.zeros((t, 1), F32)
        top = jnp.maximum(i - 1, 0)
        first = jnp.clip(first_ref[0, 0, 0, 0].astype(jnp.int32), 0, top)
        count = top - first
        carry = lax.fori_loop(0, count // 2, lambda pp, c: tiles([first + 2 * pp, first + 2 * pp + 1], c),
                              (jnp.zeros((t, hd), F32), zero, zero))
        carry = lax.fori_loop(0, count % 2, lambda _, c: tiles([top - 1], c), carry)
        below, whole = _sb_masks(t, i)
        dq, _, _ = tiles([top, i], carry, (whole, below))
        dq_ref[0] = (dq * scale).astype(BF16)

        @pl.when(i == nq - 1)
        def _():
            dk_ref[0] = dk_acc[...].astype(BF16)
            dv_ref[0] = dv_acc[...].astype(BF16)

        pl.when((hh == h - 1) & (i == nq - 1))(finish)

    full = pl.BlockSpec((1, s_len, hd), lambda hh, i: (hh, 0, 0))
    tile = pl.BlockSpec((1, t, hd), lambda hh, i: (hh, i, 0))
    hbm = pl.BlockSpec(memory_space=pl.ANY)
    out = SDS((h, s_len, hd), BF16)
    return pl.pallas_call(
        body, name=name, out_shape=(out, out, out, SDS(payload.shape, payload.dtype)), grid=(h, nq),
        in_specs=[tile, full, full, tile, pl.BlockSpec((1, t, 1), lambda hh, i: (hh, i, 0)),
                  pl.BlockSpec((1, 1) + SB_FIRST_BLOCK, lambda hh, i: (hh, i, 0, 0)), hbm],
        out_specs=(tile, full, full, hbm),
        scratch_shapes=[pltpu.VMEM((s_len, hd), F32), pltpu.VMEM((s_len, hd), F32)] + _COMM_SEMAPHORES,
        compiler_params=_params(has_side_effects=True),
    )(q, k, v, do, tot, first, payload)


def _dl_scores(qv, kk, n):
    s = lax.dot_general(qv, kk, (((1,), (1,)), ((), ())), preferred_element_type=F32) * (HEAD_DIM ** -0.5)
    r = lax.broadcasted_iota(jnp.int32, s.shape, 0)
    c = lax.broadcasted_iota(jnp.int32, s.shape, 1)
    valid = (c >= r) & (c - r <= BLOCK) & ((n > 0) | (c >= BLOCK))
    return jnp.where(valid, s, NEG)


DL_UNROLL = 4
DL_FWD_UNROLL = 8
DL_PAIR = 2 * HEAD_DIM
DL_Q_BLOCK0 = 0
DL_K_BLOCK0 = DL_WIDTH // DL_PAIR
DL_V_BLOCK0 = (IN_WIDTH - DL_WIDTH) // DL_PAIR
DL_DO_BLOCK0 = (SB_WIDTH + CV_WIDTH) // DL_PAIR


def _dl_rows(idx, nb, dil):
    r, n = idx // nb, idx % nb
    cur = pl.ds(r + n * (BLOCK * dil), BLOCK, stride=dil)
    prev = pl.ds(r + jnp.maximum(n - 1, 0) * (BLOCK * dil), BLOCK, stride=dil)
    return n, cur, prev


def _dl_window(ref, cur, prev):
    return jnp.concatenate([ref[prev, :], ref[cur, :]], axis=0).astype(BF16)


def _head_lanes():
    first = lax.broadcasted_iota(jnp.int32, (BLOCK, DL_PAIR), 1) < HEAD_DIM
    return first, jnp.logical_not(first)


def _dl_fwd(qk, u, dil, name, payload=None):
    s_len = qk.shape[0]
    nb = s_len // dil // BLOCK

    def body(q_ref, k_ref, v_ref, o_ref, lse_ref):
        heads = _head_lanes()

        def step(idx, _):
            n, cur, prev = _dl_rows(idx, nb, dil)
            q = q_ref[cur, :]
            kk = _dl_window(k_ref, cur, prev)
            vv = _dl_window(v_ref, cur, prev)
            o, lse = None, None
            for lanes in heads:
                s = _dl_scores(jnp.where(lanes, q, 0.0).astype(BF16), kk, n)
                m = jnp.max(s, axis=-1, keepdims=True)
                p = jnp.exp(s - m)
                den = jnp.sum(p, axis=-1, keepdims=True)
                o_h = jnp.dot((p / den).astype(BF16), vv, preferred_element_type=F32)
                lse_h = jnp.broadcast_to(m + jnp.log(den), (BLOCK, DL_PAIR))
                o = o_h if o is None else jnp.where(heads[0], o, o_h)
                lse = lse_h if lse is None else jnp.where(heads[0], lse, lse_h)
            o_ref[cur, :] = o
            lse_ref[cur, :] = lse
            return 0

        lax.fori_loop(0, s_len // BLOCK, step, 0, unroll=DL_FWD_UNROLL)

    col = lambda first: pl.BlockSpec((s_len, DL_PAIR), lambda i: (0, first + i))
    out = SDS((s_len, DL_WIDTH), F32)
    steps = DL_WIDTH // DL_PAIR
    if payload is not None:
        step = lambda: pl.program_id(0)
        payload = ("gather", payload, lambda: (step() == 0, step() == steps - 1, step() == steps - 1))
    return _hosted_call(body, payload, name=name, out_shape=(out, out), grid=(steps,),
                        in_specs=[col(DL_Q_BLOCK0), col(DL_K_BLOCK0), col(DL_V_BLOCK0)], out_specs=(col(0), col(0)),
                        args=(qk, qk, u))


def _dl_bwd(qk, u, dmix, o_mix, wt, lse, dil, name, payload=None):
    s_len = qk.shape[0]
    nb = s_len // dil // BLOCK
    scale = HEAD_DIM ** -0.5
    nt_dims = (((1,), (1,)), ((), ()))
    tn_dims = (((0,), (0,)), ((), ()))

    def body(q_ref, k_ref, v_ref, do_ref, om_ref, wt_ref, lse_ref, dq_ref, dk_ref, dv_ref):
        dk_ref[...] = jnp.zeros_like(dk_ref)
        dv_ref[...] = jnp.zeros_like(dv_ref)
        heads = _head_lanes()

        def step(idx, _):
            n, cur, prev = _dl_rows(idx, nb, dil)
            q = q_ref[cur, :]
            kk = _dl_window(k_ref, cur, prev)
            vv = _dl_window(v_ref, cur, prev)
            dov = do_ref[cur, :]
            d_lanes = dov * om_ref[cur, :]
            w_lanes = wt_ref[cur, :]
            lse_lanes = lse_ref[cur, :]
            dq, dkk, dvv = None, None, None
            for lanes in heads:
                qm = jnp.where(lanes, q, 0.0).astype(BF16)
                s = _dl_scores(qm, kk, n)
                p = jnp.exp(s - jnp.max(jnp.where(lanes, lse_lanes, NEG), axis=-1, keepdims=True))
                w = jnp.max(jnp.where(lanes, w_lanes, 0.0), axis=-1, keepdims=True)
                d_all = jnp.sum(jnp.where(lanes, d_lanes, 0.0), axis=-1, keepdims=True)
                do_n = jnp.where(lanes, dov * w, 0.0).astype(BF16)
                dp = lax.dot_general(do_n, vv, nt_dims, preferred_element_type=F32)
                ds = (p * (dp - w * d_all) * scale).astype(BF16)
                dq_h = jnp.dot(ds, kk, preferred_element_type=F32)
                dkk_h = lax.dot_general(ds, qm, tn_dims, preferred_element_type=F32)
                dvv_h = lax.dot_general(p.astype(BF16), do_n, tn_dims, preferred_element_type=F32)
                dq = dq_h if dq is None else jnp.where(heads[0], dq, dq_h)
                dkk = dkk_h if dkk is None else dkk + dkk_h
                dvv = dvv_h if dvv is None else dvv + dvv_h
            dq_ref[cur, :] = dq
            dk_ref[prev, :] += dkk[:BLOCK]
            dv_ref[prev, :] += dvv[:BLOCK]
            dk_ref[cur, :] += dkk[BLOCK:]
            dv_ref[cur, :] += dvv[BLOCK:]
            return 0

        lax.fori_loop(0, s_len // BLOCK, step, 0, unroll=DL_UNROLL)

    col = lambda first: pl.BlockSpec((s_len, DL_PAIR), lambda i: (0, first + i))
    out = SDS((s_len, DL_WIDTH), F32)
    steps = DL_WIDTH // DL_PAIR
    if payload is not None:
        step = lambda: pl.program_id(0)
        payload = ("all_to_all", payload, lambda: (step() == 0, None, step() == steps - 1))
    return _hosted_call(
        body, payload, name=name, out_shape=(out, out, out), grid=(steps,),
        in_specs=[col(DL_Q_BLOCK0), col(DL_K_BLOCK0), col(DL_V_BLOCK0), col(DL_DO_BLOCK0), col(0), col(0), col(0)],
        out_specs=(col(0), col(0), col(0)), args=(qk, qk, u, dmix, o_mix, wt, lse))


def _dl_mix_fwd(outs, lses, name):
    rows, w = outs[0].shape
    t = 256

    def body(o1, o2, o3, l1, l2, l3, ob_ref, of_ref, w1, w2, w3):
        a, b, c = l1[...], l2[...], l3[...]
        m = jnp.maximum(jnp.maximum(a, b), c)
        ea, eb, ec = jnp.exp(a - m), jnp.exp(b - m), jnp.exp(c - m)
        den = ea + eb + ec
        wa, wb, wc = ea / den, eb / den, ec / den
        o = wa * o1[...] + wb * o2[...] + wc * o3[...]
        ob_ref[...] = o.astype(BF16)
        of_ref[...] = o
        w1[...] = wa
        w2[...] = wb
        w3[...] = wc

    row = pl.BlockSpec((t, w), lambda i: (i, 0))
    f = SDS((rows, w), F32)
    return pl.pallas_call(body, name=name, out_shape=(SDS((rows, w), BF16), f, f, f, f), grid=(rows // t,),
                          in_specs=[row] * 6, out_specs=(row,) * 5, compiler_params=_params())(*outs, *lses)


def _x_probs(qh, kh):
    s = lax.dot_general(qh, kh, (((1,), (1,)), ((), ())), preferred_element_type=F32) * (X_HEAD_DIM ** -0.5)
    e = jnp.exp(s - jnp.max(s, axis=-1, keepdims=True))
    return e / jnp.sum(e, axis=-1, keepdims=True)


def _xattn_fwd(q, k, v, name):
    rows, d = q.shape
    t = 512

    def body(q_ref, k_ref, v_ref, o_ref):
        for hh in range(X_HEADS):
            cols = slice(hh * X_HEAD_DIM, (hh + 1) * X_HEAD_DIM)
            p = _x_probs(q_ref[:, cols], k_ref[:, cols])
            o_ref[:, cols] = jnp.dot(p.astype(BF16), v_ref[:, cols], preferred_element_type=F32).astype(BF16)

    row = pl.BlockSpec((t, d), lambda i: (i, 0))
    mem = pl.BlockSpec((N_MEM, d), lambda i: (0, 0))
    return pl.pallas_call(body, name=name, out_shape=SDS((rows, d), BF16), grid=(rows // t,), in_specs=[row, mem, mem],
                          out_specs=row, compiler_params=_params())(q, k, v)


def _xattn_bwd(q, k, v, do, name):
    rows, d = q.shape
    t = 512
    scale = X_HEAD_DIM ** -0.5

    def body(q_ref, k_ref, v_ref, do_ref, dq_ref, dk_ref, dv_ref):
        @pl.when(pl.program_id(0) == 0)
        def _():
            dk_ref[...] = jnp.zeros_like(dk_ref)
            dv_ref[...] = jnp.zeros_like(dv_ref)

        for hh in range(X_HEADS):
            cols = slice(hh * X_HEAD_DIM, (hh + 1) * X_HEAD_DIM)
            qh, kh, vh, doh = q_ref[:, cols], k_ref[:, cols], v_ref[:, cols], do_ref[:, cols]
            p = _x_probs(qh, kh)
            dp = lax.dot_general(doh, vh, (((1,), (1,)), ((), ())), preferred_element_type=F32)
            ds = (p * (dp - jnp.sum(p * dp, axis=-1, keepdims=True)) * scale).astype(BF16)
            dq_ref[:, cols] = jnp.dot(ds, kh, preferred_element_type=F32).astype(BF16)
            dk_ref[:, cols] += lax.dot_general(ds, qh, (((0,), (0,)), ((), ())), preferred_element_type=F32)
            dv_ref[:, cols] += lax.dot_general(p.astype(BF16), doh, (((0,), (0,)), ((), ())), preferred_element_type=F32)

    row = pl.BlockSpec((t, d), lambda i: (i, 0))
    mem = pl.BlockSpec((N_MEM, d), lambda i: (0, 0))
    return pl.pallas_call(
        body, name=name, out_shape=(SDS((rows, d), BF16), SDS((N_MEM, d), F32), SDS((N_MEM, d), F32)), grid=(rows // t,),
        in_specs=[row, mem, mem, row], out_specs=(row, mem, mem), compiler_params=_params(),
    )(q, k, v, do)


CV_TILE = 256
CV_HALO = 32
CV_LEAD = CV_HALO - (CV_KERNEL - 1)


def _shifted(win, off, rows):
    n = win.shape[0]
    return pltpu.roll(win, (n - off) % n, axis=0)[:rows]


def _glu(val, gate):
    return val * jax.nn.sigmoid(gate)


def _ln_parts(c):
    mu = jnp.mean(c, axis=-1, keepdims=True)
    xc = c - mu
    rstd = lax.rsqrt(jnp.mean(xc * xc, axis=-1, keepdims=True) + EPS)
    return xc * rstd, rstd


def _cv_fwd(u, cv_w, cv_b, ln_g, ln_b, name):
    rows = u.shape[0]
    t, w = CV_TILE, CV_WIDTH
    val_col = 3 * SB_WIDTH // w
    ratio = t // CV_HALO

    def body(val_ref, gate_ref, pval_ref, pgate_ref, w_ref, b_ref, g_ref, beta_ref, s_ref, c_ref):
        i = pl.program_id(0)
        hist = jnp.where(i > 0, _glu(pval_ref[...], pgate_ref[...]), 0.0)
        win = jnp.concatenate([hist, _glu(val_ref[...], gate_ref[...])], axis=0)
        acc = jnp.broadcast_to(b_ref[...], (t, w))
        for kk in range(CV_KERNEL):
            acc = acc + _shifted(win, CV_LEAD + kk, t) * w_ref[kk:kk + 1, :]
        c_ref[...] = acc
        n, _ = _ln_parts(acc)
        y = n * g_ref[...] + beta_ref[...]
        s_ref[...] = (y * jax.nn.sigmoid(y)).astype(BF16)

    cur = lambda col: pl.BlockSpec((t, w), lambda i: (i, col))
    prev = lambda col: pl.BlockSpec((CV_HALO, w), lambda i: (jnp.maximum(i * ratio - 1, 0), col))
    vec = pl.BlockSpec((1, w), lambda i: (0, 0))
    return pl.pallas_call(
        body, name=name, out_shape=(SDS((rows, w), BF16), SDS((rows, w), F32)), grid=(rows // t,),
        in_specs=[cur(val_col), cur(val_col + 1), prev(val_col), prev(val_col + 1),
                  pl.BlockSpec((CV_KERNEL, w), lambda i: (0, 0)), vec, vec, vec],
        out_specs=(pl.BlockSpec((t, w), lambda i: (i, 0)),) * 2, compiler_params=_params(),
    )(u, u, u, u, cv_w, cv_b, ln_g, ln_b)


def _cv_bwd(u, c, ds, db_out, cv_w, ln_g, ln_b, name, payload=None):
    rows = u.shape[0]
    t, w = CV_TILE, CV_WIDTH
    val_col = 3 * SB_WIDTH // w
    ratio = t // CV_HALO
    nt = rows // t

    def conv_out_grad(c_v, ds_v, g_v, beta_v):
        n, rstd = _ln_parts(c_v)
        y = n * g_v + beta_v
        sig = jax.nn.sigmoid(y)
        dy = ds_v * (sig * (1.0 + y * (1.0 - sig)))
        dn = dy * g_v
        dc = rstd * (dn - jnp.mean(dn, axis=-1, keepdims=True) - n * jnp.mean(dn * n, axis=-1, keepdims=True))
        return dc, dy, n

    def body(val_ref, gate_ref, pval_ref, pgate_ref, c_ref, nc_ref, ds_ref, nds_ref, dbo_ref, w_ref, g_ref, beta_ref,
             dvg_ref, dw_ref, db_ref, dg_ref, dbeta_ref, dpwb_ref):
        i = pl.program_id(0)

        @pl.when(i == 0)
        def _():
            for r in (dw_ref, db_ref, dg_ref, dbeta_ref, dpwb_ref):
                r[...] = jnp.zeros_like(r)

        g_v, beta_v = g_ref[...], beta_ref[...]
        dc, dy, n = conv_out_grad(c_ref[...], ds_ref[...], g_v, beta_v)
        dc_next, _, _ = conv_out_grad(nc_ref[...], nds_ref[...], g_v, beta_v)
        dc_next = jnp.where(i < nt - 1, dc_next, 0.0)
        dg_ref[...] += jnp.sum(dy * n, axis=0, keepdims=True)
        dbeta_ref[...] += jnp.sum(dy, axis=0, keepdims=True)
        db_ref[...] += jnp.sum(dc, axis=0, keepdims=True)
        dpwb_ref[...] += jnp.sum(dbo_ref[...], axis=0, keepdims=True)

        val, gate = val_ref[...], gate_ref[...]
        hist = jnp.where(i > 0, _glu(pval_ref[...], pgate_ref[...]), 0.0)
        win = jnp.concatenate([hist, _glu(val, gate)], axis=0)
        dc_ext = jnp.concatenate([dc, dc_next], axis=0)
        dglu = jnp.zeros((t, w), F32)
        for kk in range(CV_KERNEL):
            dw_ref[kk:kk + 1, :] += jnp.sum(dc * _shifted(win, CV_LEAD + kk, t), axis=0, keepdims=True)
            dglu = dglu + _shifted(dc_ext, CV_KERNEL - 1 - kk, t) * w_ref[kk:kk + 1, :]
        sig = jax.nn.sigmoid(gate)
        dvg_ref[:, 0:w] = (dglu * sig).astype(BF16)
        dvg_ref[:, w:2 * w] = (dglu * val * sig * (1.0 - sig)).astype(BF16)

    cur = lambda col: pl.BlockSpec((t, w), lambda i: (i, col))
    prev = lambda col: pl.BlockSpec((CV_HALO, w), lambda i: (jnp.maximum(i * ratio - 1, 0), col))
    nxt = pl.BlockSpec((CV_HALO, w), lambda i: (jnp.minimum((i + 1) * ratio, rows // CV_HALO - 1), 0))
    vec = pl.BlockSpec((1, w), lambda i: (0, 0))
    if payload is not None:
        payload = ("all_to_all", payload, lambda: (pl.program_id(0) == 0, None, pl.program_id(0) == nt - 1))
    return _hosted_call(
        body, payload, name=name,
        out_shape=(SDS((rows, 2 * w), BF16), SDS((CV_HALO, w), F32), SDS((1, w), F32), SDS((1, w), F32), SDS((1, w), F32),
                   SDS((1, w), F32)),
        grid=(nt,),
        in_specs=[cur(val_col), cur(val_col + 1), prev(val_col), prev(val_col + 1), cur(0), nxt, cur(0), nxt, cur(0),
                  pl.BlockSpec((CV_KERNEL, w), lambda i: (0, 0)), vec, vec],
        out_specs=(pl.BlockSpec((t, 2 * w), lambda i: (i, 0)), pl.BlockSpec((CV_HALO, w), lambda i: (0, 0)), vec, vec, vec, vec),
        args=(u, u, u, u, c, c, ds, ds, db_out, cv_w, ln_g, ln_b))


FFN_TILE = 512
FFN_CHUNK = 64
FFN_COLS = 256
FFN_HALO = 8
FFN_KERNEL = 3
N_FF_BLOCKS = D_FF // FFN_COLS


def _conv3(prev8, cur, w_ref, b_ref, first):
    t = cur.shape[0]
    win = jnp.concatenate([jnp.where(first, 0.0, prev8), cur], axis=0)
    return (b_ref[...] + _shifted(win, FFN_HALO - 2, t) * w_ref[0:1, :] + _shifted(win, FFN_HALO - 1, t) * w_ref[1:2, :]
            + cur * w_ref[2:3, :])


def _gelu_gate(gate, val):
    return jax.nn.gelu(gate, approximate=True) * val


GELU_C0 = math.sqrt(2.0 / math.pi)
GELU_C1 = 0.044715


def _gelu_gate_bwd(gate, val, dout):
    sq = gate * gate
    th = jnp.tanh(GELU_C0 * gate * (1.0 + GELU_C1 * sq))
    half_cdf = 0.5 * (1.0 + th)
    slope = half_cdf + 0.5 * gate * (1.0 - th * th) * (GELU_C0 * (1.0 + 3.0 * GELU_C1 * sq))
    return dout * val * slope, dout * (gate * half_cdf)


def _ffn_specs(t):
    ratio = t // FFN_HALO
    cur = pl.BlockSpec((t, FFN_COLS), lambda j, i: (i, j))
    prev = pl.BlockSpec((FFN_HALO, FFN_COLS), lambda j, i: (jnp.maximum(i * ratio - 1, 0), j))
    wsp = pl.BlockSpec((FFN_KERNEL, FFN_COLS), lambda j, i: (0, j))
    bsp = pl.BlockSpec((1, FFN_COLS), lambda j, i: (0, j))
    return cur, prev, wsp, bsp


def _ffn_host_steps(row_tiles):
    def when():
        j, i = pl.program_id(0), pl.program_id(1)
        return (j == 0) & (i == 0), (j == (3 * N_FF_BLOCKS) // 4) & (i == 0), (j == N_FF_BLOCKS - 1) & (i == row_tiles - 1)
    return when


def _ffn_act_fwd(up_g, up_v, w_g, w_v, b_g, b_v, name, payload=None):
    rows = up_g.shape[0]
    t = FFN_TILE
    cur, prev, wsp, bsp = _ffn_specs(t)

    def body(g_ref, v_ref, pg_ref, pv_ref, wg_ref, wv_ref, bg_ref, bv_ref, o_ref):
        first = pl.program_id(1) == 0
        gate = _conv3(pg_ref[...], g_ref[...], wg_ref, bg_ref, first)
        val = _conv3(pv_ref[...], v_ref[...], wv_ref, bv_ref, first)
        o_ref[...] = _gelu_gate(gate, val).astype(BF16)

    if payload is not None:
        payload = ("gather", payload, _ffn_host_steps(rows // t))
    return _hosted_call(
        body, payload, name=name, out_shape=(SDS((rows, D_FF), BF16),), grid=(N_FF_BLOCKS, rows // t),
        in_specs=[cur, cur, prev, prev, wsp, wsp, bsp, bsp], out_specs=(cur,), args=(up_g, up_v, up_g, up_v, w_g, w_v, b_g, b_v))


def _ffn_act_bwd(up_g, up_v, dact, w_g, w_v, b_g, b_v, name, payload=None):
    rows = up_g.shape[0]
    t = FFN_TILE
    ch = FFN_CHUNK
    che = ch + FFN_HALO
    ratio = t // FFN_HALO
    nt = rows // t
    cur, prev, wsp, bsp = _ffn_specs(t)
    nxt = pl.BlockSpec((FFN_HALO, FFN_COLS), lambda j, i: (jnp.minimum((i + 1) * ratio, rows // FFN_HALO - 1), j))

    def body(g_ref, v_ref, pg_ref, pv_ref, ng_ref, nv_ref, da_ref, nda_ref, wg_ref, wv_ref, bg_ref, bv_ref,
             dug_ref, duv_ref, dwg_ref, dwv_ref, dbg_ref, dbv_ref, win_g, win_v, da_win):
        i = pl.program_id(1)
        first = i == 0

        @pl.when(first)
        def _():
            for r in (dwg_ref, dwv_ref, dbg_ref, dbv_ref):
                r[...] = jnp.zeros_like(r)

        for win, pre, x, nx in ((win_g, pg_ref, g_ref, ng_ref), (win_v, pv_ref, v_ref, nv_ref)):
            win[0:FFN_HALO, :] = jnp.where(first, 0.0, pre[...])
            win[FFN_HALO:FFN_HALO + t, :] = x[...]
            win[FFN_HALO + t:, :] = nx[...]
        da_win[0:t, :] = da_ref[...]
        da_win[t:, :] = jnp.where(i < nt - 1, nda_ref[...], 0.0)
        halves = ((win_g, wg_ref, bg_ref, dug_ref), (win_v, wv_ref, bv_ref, duv_ref))

        def chunk(c, sums):
            base = pl.multiple_of(c * ch, ch)
            taps, convs = [], []
            for win, w_ref, b_ref, _ in halves:
                w = win[pl.ds(base, ch + 2 * FFN_HALO), :]
                shifted = [_shifted(w, FFN_HALO - 2 + kk, che) for kk in range(FFN_KERNEL)]
                taps.append(shifted)
                convs.append(b_ref[...] + sum(s * w_ref[kk:kk + 1, :] for kk, s in enumerate(shifted)))
            dconvs = _gelu_gate_bwd(*convs, da_win[pl.ds(base, che), :])
            new = []
            for dc_ext, shifted, (_, w_ref, _, du_ref), (dw, db) in zip(dconvs, taps, halves, sums):
                dc = dc_ext[:ch]
                du_ref[pl.ds(base, ch), :] = (dc * w_ref[2:3, :] + _shifted(dc_ext, 1, ch) * w_ref[1:2, :]
                                              + _shifted(dc_ext, 2, ch) * w_ref[0:1, :]).astype(BF16)
                dw = [dw[kk] + jnp.sum(dc * shifted[kk][:ch], axis=0, keepdims=True) for kk in range(FFN_KERNEL)]
                new.append((dw, db + jnp.sum(dc, axis=0, keepdims=True)))
            return new

        zero = jnp.zeros((1, FFN_COLS), F32)
        sums = lax.fori_loop(0, t // ch, chunk, [([zero] * FFN_KERNEL, zero)] * 2, unroll=2)
        for (dw, db), dw_ref, db_ref in zip(sums, (dwg_ref, dwv_ref), (dbg_ref, dbv_ref)):
            for kk in range(FFN_KERNEL):
                dw_ref[kk:kk + 1, :] += dw[kk]
            db_ref[...] += db

    big, wshape, bshape = SDS((rows, D_FF), BF16), SDS((FFN_KERNEL, D_FF), F32), SDS((1, D_FF), F32)
    if payload is not None:
        payload = ("all_to_all", payload, _ffn_host_steps(nt))
    window = pltpu.VMEM((t + 2 * FFN_HALO, FFN_COLS), F32)
    return _hosted_call(
        body, payload, name=name, out_shape=(big, big, wshape, wshape, bshape, bshape), grid=(N_FF_BLOCKS, nt),
        in_specs=[cur, cur, prev, prev, nxt, nxt, cur, nxt, wsp, wsp, bsp, bsp], out_specs=(cur, cur, wsp, wsp, bsp, bsp),
        scratch_shapes=(window, window, pltpu.VMEM((t + FFN_HALO, FFN_COLS), F32)),
        args=(up_g, up_v, up_g, up_v, up_g, up_v, dact, dact, w_g, w_v, b_g, b_v))


def _adamw_update(parts, w_ref, m_ref, v_ref, g_ref, d_ref, nm_ref, nv_ref):
    g = parts[0].astype(F32)
    for s in range(1, N_DEV):
        g = g + parts[s].astype(F32)
    nm = ADAM_B1 * m_ref[...] + (1.0 - ADAM_B1) * g
    nv = ADAM_B2 * v_ref[...] + (1.0 - ADAM_B2) * jnp.square(g)
    m_hat = nm / (1.0 - ADAM_B1 ** ADAM_STEP)
    v_hat = nv / (1.0 - ADAM_B2 ** ADAM_STEP)
    g_ref[...] = g
    d_ref[...] = -ADAM_LR * (m_hat / (jnp.sqrt(v_hat) + ADAM_EPS) + ADAM_WD * w_ref[...])
    nm_ref[...] = nm
    nv_ref[...] = nv


def _adamw(parts, w, m, v, name):
    rows, cols = w.shape
    t = _pick(rows, (512, 256, 128)) if rows > 512 else rows

    def body(p_ref, *refs):
        _adamw_update(p_ref[...], *refs)

    row = pl.BlockSpec((t, cols), lambda i: (i, 0))
    out = SDS((rows, cols), F32)
    return pl.pallas_call(
        body, name=name, out_shape=(out,) * 4, grid=(rows // t,),
        in_specs=[pl.BlockSpec((N_DEV, t, cols), lambda i: (0, i, 0)), row, row, row], out_specs=(row,) * 4,
        compiler_params=_params(),
    )(parts, w, m, v)


def _adamw_packed(sources, w, m, v, name):
    rows, cols = w.shape
    t = ADAMW_ROW_BLOCK
    nb = rows // DEPTH // t
    (src0, first0), (src1, first1) = sources
    assert first0 % t == 0 and first1 % t == 0 and rows % (DEPTH * t) == 0

    def body(p0_ref, p1_ref, *refs):
        layer = pl.program_id(0)
        _adamw_update(jnp.where(layer == 0, p0_ref[...], p1_ref[...]), *refs)

    spec0 = pl.BlockSpec((N_DEV, t, cols), lambda l, i: (0, first0 // t + i * (1 - l) + (nb - 1) * l, 0))
    spec1 = pl.BlockSpec((N_DEV, t, cols), lambda l, i: (0, first1 // t + i * l, 0))
    row = pl.BlockSpec((t, cols), lambda l, i: (l * nb + i, 0))
    out = SDS((rows, cols), F32)
    return pl.pallas_call(body, name=name, out_shape=(out,) * 4, grid=(DEPTH, nb), in_specs=[spec0, spec1, row, row, row],
                          out_specs=(row,) * 4, compiler_params=_params())(src0, src1, w, m, v)


_COMM_SEMAPHORES = [pltpu.SemaphoreType.DMA((N_DEV - 1,)), pltpu.SemaphoreType.DMA((N_DEV - 1,)), pltpu.SemaphoreType.DMA]


def _gather_steps(x_ref, out_ref, send_sems, recv_sems, local_sem):
    x_, y_, c_ = lax.axis_index("x"), lax.axis_index("y"), lax.axis_index("c")
    me, sibling = (x_, y_, c_), (x_, y_, 1 - c_)
    chips = [(1 - x_, y_), (x_, 1 - y_), (1 - x_, 1 - y_)]

    def slot(px, py, pc):
        return out_ref.at[4 * px + 2 * py + pc]

    def copy(kk, block, to, src=None):
        return pltpu.make_async_remote_copy(
            src_ref=slot(*block) if src is None else src, dst_ref=slot(*block),
            send_sem=send_sems.at[kk], recv_sem=recv_sems.at[kk], device_id=to, device_id_type=MESH)

    def mine():
        return pltpu.make_async_copy(x_ref, slot(*me), local_sem)

    def first():
        return [copy(0, me, sibling, src=x_ref)] + [copy(1 + j, me, (*chip, c_), src=x_ref) for j, chip in enumerate(chips)]

    def passed():
        return [copy(4 + j, (*chip, c_), sibling) for j, chip in enumerate(chips)]

    def start():
        mine().start()
        for cp in first():
            cp.start()

    def forward():
        for j, (chip, cp) in enumerate(zip(chips, passed())):
            copy(1 + j, (*chip, c_), me).wait_recv()
            cp.start()

    def finish():
        copy(0, sibling, me).wait_recv()
        for j, chip in enumerate(chips):
            copy(4 + j, (*chip, 1 - c_), me).wait_recv()
        for cp in first() + passed():
            cp.wait_send()
        mine().wait()

    return start, forward, finish


def _exchange_steps(x_ref, out_ref, send_sems, recv_sems, local_sem):
    x_, y_, c_ = lax.axis_index("x"), lax.axis_index("y"), lax.axis_index("c")
    me = 4 * x_ + 2 * y_ + c_

    def mine():
        return pltpu.make_async_copy(x_ref.at[me], out_ref.at[me], local_sem)

    def copies():
        out = []
        for r in range(1, N_DEV):
            px = 1 - x_ if r & 4 else x_
            py = 1 - y_ if r & 2 else y_
            pc = 1 - c_ if r & 1 else c_
            out.append(pltpu.make_async_remote_copy(
                src_ref=x_ref.at[4 * px + 2 * py + pc], dst_ref=out_ref.at[me],
                send_sem=send_sems.at[r - 1], recv_sem=recv_sems.at[r - 1], device_id=(px, py, pc), device_id_type=MESH))
        return out

    def start():
        mine().start()
        for cp in copies():
            cp.start()

    def finish():
        for cp in copies():
            cp.wait_recv()
        for cp in copies():
            cp.wait_send()
        mine().wait()

    return start, finish


def _hosted_call(body, exchange, *, name, out_shape, grid, in_specs, out_specs, args, scratch_shapes=()):
    if exchange is None:
        return pl.pallas_call(body, name=name, out_shape=out_shape, grid=grid, in_specs=in_specs, out_specs=out_specs,
                              scratch_shapes=list(scratch_shapes), compiler_params=_params())(*args)
    kind, payload, when = exchange
    n_in, n_out, n_scratch = len(in_specs), len(out_specs), len(scratch_shapes)
    result = SDS((N_DEV,) + payload.shape, payload.dtype) if kind == "gather" else SDS(payload.shape, payload.dtype)

    def hosting(*refs):
        ins, pay_ref = refs[:n_in], refs[n_in]
        outs, res_ref = refs[n_in + 1:n_in + 1 + n_out], refs[n_in + 1 + n_out]
        rest = refs[n_in + 2 + n_out:]
        scratch, sems = rest[:n_scratch], rest[n_scratch:]
        first, middle, last = when()
        if kind == "gather":
            start, forward, finish = _gather_steps(pay_ref, res_ref, *sems)
            pl.when(first)(start)
            pl.when(middle)(forward)
        else:
            start, finish = _exchange_steps(pay_ref, res_ref, *sems)
            pl.when(first)(start)
        body(*ins, *outs, *scratch)
        pl.when(last)(finish)

    hbm = pl.BlockSpec(memory_space=pl.ANY)
    return pl.pallas_call(
        hosting, name=name, out_shape=tuple(out_shape) + (result,), grid=grid, in_specs=list(in_specs) + [hbm],
        out_specs=tuple(out_specs) + (hbm,), scratch_shapes=list(scratch_shapes) + _COMM_SEMAPHORES,
        compiler_params=_params(has_side_effects=True),
    )(*args, payload)


def _exchange_alone(xs, make_steps, out_shapes, name):
    n = len(xs)

    def body(*refs):
        sems = refs[2 * n:]
        steps = [make_steps(refs[k], refs[n + k], *sems[3 * k:3 * k + 3]) for k in range(n)]
        for stage in zip(*steps):
            for step in stage:
                step()

    hbm = pl.BlockSpec(memory_space=pl.ANY)
    return pl.pallas_call(body, name=name, out_shape=tuple(out_shapes), in_specs=[hbm] * n, out_specs=(hbm,) * n,
                          scratch_shapes=_COMM_SEMAPHORES * n, compiler_params=pltpu.CompilerParams(has_side_effects=True))(*xs)


def _all_gather(xs, name):
    return _exchange_alone(xs, _gather_steps, [SDS((N_DEV,) + x.shape, x.dtype) for x in xs], name)


def _all_to_all(xs, name):
    return _exchange_alone(xs, _exchange_steps, [SDS(x.shape, x.dtype) for x in xs], name)


BIG = ("w_in", "cv_pw_w", "w_out", "x_wq", "x_wk", "x_wv", "x_wo", "ffn_w_up", "ffn_w_down")
_MIXER = (("w_in", 352), ("w_out", 128))
_CROSS = (("x_wq", 128), ("x_wk", 128), ("x_wv", 128), ("x_wo", 128))
GROUPS = {
    "a": (("w_in", 0, 352),),
    "o0": (("w_out", 0, 128),),
    "m1": tuple((n, 1, r) for n, r in _MIXER),
    "bx": tuple((n, 0, r) for n, r in _CROSS),
    "cx": tuple((n, 1, r) for n, r in _CROSS),
    "bf": (("ffn_w_down", 0, 352),),
    "cf": (("ffn_w_down", 1, 352),),
}
GRADIENT_GROUPS = ("a", "o0", "m1", "bf", "bx", "cf", "cx")
PW_GROUP = "o0"
TRANSPOSED = ("w_in", "ffn_w_up")
PW_ROWS = 16
ADAMW_ROW_BLOCK = 32


def _group_rows(group):
    out, first = {}, 0
    for n, l, r in GROUPS[group]:
        out[(n, l)] = (first, r)
        first += r
    return out


def _where_is(name, layer):
    for group in GRADIENT_GROUPS:
        rows = _group_rows(group)
        if (name, layer) in rows:
            return (group,) + rows[(name, layer)]
    raise KeyError((name, layer))


def _pack_weights(group, wts):
    pieces = []
    for n, l, _ in GROUPS[group]:
        w = wts[n][l].astype(BF16)
        pieces.append(w.T if n in TRANSPOSED else w)
    if group == PW_GROUP:
        pieces.append(wts["cv_pw_w"].astype(BF16).reshape(PW_ROWS, PAYLOAD_COLS))
    return jnp.concatenate(pieces, axis=0)


def _unpack_weights(group, gathered):
    return {key: gathered[:, first:first + r, :].reshape(N_DEV * r, PAYLOAD_COLS)
            for key, (first, r) in _group_rows(group).items()}


def _pack_grads(group, grads):
    pieces = []
    for n, l, r in GROUPS[group]:
        g = grads[n][l]
        parts = g if isinstance(g, tuple) else (g,)
        pieces.append(jnp.concatenate([p.reshape(-1, r, PAYLOAD_COLS) for p in parts], axis=0))
    if group == PW_GROUP:
        pieces.append(_to_shards("cv_pw_w", jnp.stack(grads["cv_pw_w"])).reshape(N_DEV, PW_ROWS, PAYLOAD_COLS))
    return jnp.concatenate(pieces, axis=1)


COL_SHARDED = ("w_in", "ffn_w_up", "cv_w", "ffn_conv_w")
SMALL_SHARDED = ("cv_w", "ffn_conv_w")
REPLICATED = ("mix_norm_pre", "cv_b", "cv_ln_g", "cv_ln_b", "cv_pw_b", "mix_norm_post", "x_norm_pre", "mem_norm",
              "x_norm_post", "ffn_norm_pre", "ffn_conv_b", "ffn_norm_post")
WEIGHTS = ("mix_norm_pre", "w_in", "cv_w", "cv_b", "cv_ln_g", "cv_ln_b", "cv_pw_w", "cv_pw_b", "w_out", "mix_norm_post",
           "x_norm_pre", "mem_norm", "x_wq", "x_wk", "x_wv", "x_wo", "x_norm_post", "ffn_norm_pre", "ffn_w_up",
           "ffn_conv_w", "ffn_conv_b", "ffn_w_down", "ffn_norm_post")
PAYLOAD_COLS = 1024


PAYLOAD_ROW_TILE = 16


def _pad_rows(flat, cols):
    n = flat.shape[-1]
    rows = -(-n // (cols * PAYLOAD_ROW_TILE)) * PAYLOAD_ROW_TILE
    pad = rows * cols - n
    if pad:
        flat = jnp.concatenate([flat, jnp.zeros(flat.shape[:-1] + (pad,), flat.dtype)], axis=-1)
    return flat.reshape(flat.shape[:-1] + (rows, cols))


def _unshard(name, parts):
    n, depth, r, c = parts.shape
    if name in COL_SHARDED:
        return parts.transpose(1, 2, 0, 3).reshape(depth, r, n * c)
    return parts.transpose(1, 0, 2, 3).reshape(depth, n * r, c)


def _to_shards(name, full):
    depth, r, c = full.shape
    if name in COL_SHARDED:
        return full.reshape(depth, r, N_DEV, c // N_DEV).transpose(2, 0, 1, 3).reshape(N_DEV, -1)
    return full.reshape(depth, N_DEV, r // N_DEV, c).transpose(1, 0, 2, 3).reshape(N_DEV, -1)


def _heads_major(x, h):
    return x.reshape(x.shape[0], h, HEAD_DIM).transpose(1, 0, 2)


def _tokens_major(x):
    return x.transpose(1, 0, 2).reshape(x.shape[1], -1)


def _ffn_halves(p):
    w, b = p["ffn_conv_w"], p["ffn_conv_b"]
    return w[:, :D_FF], w[:, D_FF:], b[:, :D_FF], b[:, D_FF:]


def _layer_fwd(l, h, hn, p, mem, cos, sin, g_next, payload, unpack, ffn_shards, ffn_payload, rope_payload=None,
               rope_unpack=None):
    p = dict(p)
    sv = {"h0": h, "hn0": hn}
    u = _mm(hn, p["w_in"], "nt", F32, f"l{l}_in_proj")
    sv["u"] = u
    sb = _heads_major(u[:, :3 * SB_WIDTH].astype(BF16), 3 * SB_HEADS)
    sb_q, sb_k, sb_v = sb[:SB_HEADS], sb[SB_HEADS:2 * SB_HEADS], sb[2 * SB_HEADS:]
    a_out, sb_tot, sb_first, gathered = _sb_fwd(sb_q, sb_k, sb_v, payload, f"l{l}_sb_fwd")
    p.update(unpack(gathered))
    sv.update(sb_q=sb_q, sb_k=sb_k, sb_v=sb_v, sb_tot=sb_tot, sb_first=sb_first, p=p)

    qk, *rope_gathered = _rope_fwd(u, cos, sin, f"l{l}_rope_fwd", rope_payload)
    if rope_gathered:
        p.update(rope_unpack(rope_gathered[0]))
        sv["rope_gathered"] = rope_gathered[0]
    cv_s, cv_c = _cv_fwd(u, p["cv_w"], p["cv_b"], p["cv_ln_g"], p["cv_ln_b"], f"l{l}_cv_fwd")
    b_out = _mm(cv_s, p["cv_pw_w"], "nn", BF16, f"l{l}_cv_pw", bias=p["cv_pw_b"])
    sv.update(cv_s=cv_s, cv_c=cv_c)

    up_t = ffn_shards[0].astype(BF16).T
    half_rows = up_t.shape[0] // 2
    carried = (up_t[:half_rows], up_t[half_rows:], ffn_shards[1].astype(BF16))
    outs, lses, got = [], [], []
    for b, (_, dil) in enumerate(DL_PATTERN):
        o, lse, gathered = _dl_fwd(qk, u, dil, f"l{l}_dl{b}_fwd", carried[b])
        outs.append(o)
        lses.append(lse)
        got.append(gathered)
    up_blocks = jnp.concatenate(got[:2], axis=1)
    half = N_DEV // 2
    p["ffn_w_up"] = (up_blocks[:half].reshape(-1, PAYLOAD_COLS), up_blocks[half:].reshape(-1, PAYLOAD_COLS))
    p["ffn_w_down"] = got[2].reshape(-1, PAYLOAD_COLS)
    c_out, c_out_f32, w1, w2, w3 = _dl_mix_fwd(outs, lses, f"l{l}_dl_mix")
    sv.update(dl_qk=qk, dl_lse=lses, dl_o=c_out_f32, dl_w=(w1, w2, w3))

    mix = jnp.concatenate([_tokens_major(a_out), b_out, c_out], axis=-1)
    y = _mm(mix, p["w_out"], "nn", F32, f"l{l}_out_proj")
    h1, hn1 = _res_norm_fwd(h, y, p["mix_norm_post"], p["x_norm_pre"], f"l{l}_mix_post")
    sv.update(mix=mix, y_mix=y, h1=h1, hn1=hn1)

    xq = _mm(hn1, p["x_wq"], "nn", BF16, f"l{l}_xq")
    memn = _rms_fwd(mem, p["mem_norm"], f"l{l}_mem_norm")
    xk = _mm(memn, p["x_wk"], "nn", BF16, f"l{l}_xk")
    xv = _mm(memn, p["x_wv"], "nn", BF16, f"l{l}_xv")
    xo = _xattn_fwd(xq, xk, xv, f"l{l}_xattn_fwd")
    y = _mm(xo, p["x_wo"], "nn", F32, f"l{l}_xo_proj")
    h2, hn2 = _res_norm_fwd(h1, y, p["x_norm_post"], p["ffn_norm_pre"], f"l{l}_x_post")
    sv.update(xq=xq, xk=xk, xv=xv, xo=xo, memn=memn, y_x=y, h2=h2, hn2=hn2)

    up_g = _mm(hn2, p["ffn_w_up"][0], "nt", F32, f"l{l}_ffn_up_gate")
    up_v = _mm(hn2, p["ffn_w_up"][1], "nt", F32, f"l{l}_ffn_up_val")
    act, *ffn_gathered = _ffn_act_fwd(up_g, up_v, *_ffn_halves(p), f"l{l}_ffn_act", ffn_payload)
    y = _mm(act, p["ffn_w_down"], "nn", F32, f"l{l}_ffn_down")
    h3, hn3 = _res_norm_fwd(h2, y, p["ffn_norm_post"], g_next, f"l{l}_ffn_post")
    sv.update(up_g=up_g, up_v=up_v, act=act, y_ffn=y)
    return h3, hn3, sv, (ffn_gathered[0] if ffn_gathered else None)


def _layer_bwd(l, dh, dy, sv, mem, cos, sin, prev_post, ffn_payload, pack):
    p = sv["p"]
    gr = {}
    received = {}
    dact = _mm(dy, p["ffn_w_down"], "nt", F32, f"l{l}_d_act")
    gr["ffn_w_down"] = _mm(sv["act"], dy, "tn", BF16, f"l{l}_dw_down")
    dup_g, dup_v, dwg, dwv, dbg, dbv, *got = _ffn_act_bwd(sv["up_g"], sv["up_v"], dact, *_ffn_halves(p), f"l{l}_ffn_act_bwd",
                                                         ffn_payload)
    if got:
        received["ffn_payload"] = got[0]
    gr["ffn_conv_w"] = jnp.concatenate([dwg, dwv], axis=1)
    gr["ffn_conv_b"] = jnp.concatenate([dbg, dbv], axis=1)
    dhn = (_mm(dup_g, p["ffn_w_up"][0], "nn", F32, f"l{l}_d_hn2_gate"), _mm(dup_v, p["ffn_w_up"][1], "nn", F32, f"l{l}_d_hn2_val"))
    gr["ffn_w_up"] = (_mm(dup_g, sv["hn2"], "tn", BF16, f"l{l}_dw_up_gate"), _mm(dup_v, sv["hn2"], "tn", BF16, f"l{l}_dw_up_val"))
    dh, dy, gr["ffn_norm_pre"], gr["x_norm_post"] = _norm_bwd(
        dh, (sv["h2"], p["ffn_norm_pre"], dhn), (sv["y_x"], p["x_norm_post"]), f"l{l}_x_post_bwd")

    do = _mm(dy, p["x_wo"], "nt", BF16, f"l{l}_d_xo")
    gr["x_wo"] = _mm(sv["xo"], dy, "tn", BF16, f"l{l}_dw_xo")
    dq, dk, dv = _xattn_bwd(sv["xq"], sv["xk"], sv["xv"], do, f"l{l}_xattn_bwd")
    dhn = _mm(dq, p["x_wq"], "nt", F32, f"l{l}_d_hn1")
    gr["x_wq"] = _mm(sv["hn1"], dq, "tn", BF16, f"l{l}_dw_xq")
    gr["x_wk"] = _mm(sv["memn"], dk, "tn", BF16, f"l{l}_dw_xk")
    gr["x_wv"] = _mm(sv["memn"], dv, "tn", BF16, f"l{l}_dw_xv")
    dmemn = _mm(dk, p["x_wk"], "nt", F32, f"l{l}_d_memn_k") + _mm(dv, p["x_wv"], "nt", F32, f"l{l}_d_memn_v")
    gr["mem_norm"] = _rms_gain_grad(mem, p["mem_norm"], dmemn, f"l{l}_mem_norm_bwd")
    dh, dy, gr["x_norm_pre"], gr["mix_norm_post"] = _norm_bwd(
        dh, (sv["h1"], p["x_norm_pre"], dhn), (sv["y_mix"], p["mix_norm_post"]), f"l{l}_mix_post_bwd")

    dmix = _mm(dy, p["w_out"], "nt", F32, f"l{l}_d_mix")
    gr["w_out"] = _mm(sv["mix"], dy, "tn", BF16, f"l{l}_dw_out")
    do_a = _heads_major(dmix[:, :SB_WIDTH].astype(BF16), SB_HEADS)
    dq, dk, dv, received["ffn"] = _sb_bwd(sv["sb_q"], sv["sb_k"], sv["sb_v"], do_a, sv["sb_tot"], sv["sb_first"],
                                          pack("ffn", gr), f"l{l}_sb_bwd")
    du_sb = _tokens_major(jnp.concatenate([dq, dk, dv], axis=0))

    db_out = dmix[:, SB_WIDTH:SB_WIDTH + CV_WIDTH]
    ds = _mm(db_out, p["cv_pw_w"], "nt", F32, f"l{l}_d_cv_s")
    gr["cv_pw_w"] = _mm(sv["cv_s"], db_out, "tn", BF16, f"l{l}_dw_cv_pw")
    du_cv, dcvw, gr["cv_b"], gr["cv_ln_g"], gr["cv_ln_b"], gr["cv_pw_b"], *got = _cv_bwd(
        sv["u"], sv["cv_c"], ds, db_out, p["cv_w"], p["cv_ln_g"], p["cv_ln_b"], f"l{l}_cv_bwd", pack("out", gr))
    if got:
        received["out"] = got[0]
    gr["cv_w"] = dcvw[:CV_KERNEL]

    dqs, dks, dvs = [], [], []
    up_rows = jnp.concatenate([g.reshape(N_DEV // 2, -1, PAYLOAD_COLS) for g in gr["ffn_w_up"]], axis=0)
    half_rows = up_rows.shape[1] // 2
    carried = {"up0": up_rows[:, :half_rows], "up1": up_rows[:, half_rows:], "cross": pack("cross", gr)}
    for b, ((_, dil), what) in enumerate(zip(DL_PATTERN, carried)):
        dq, dk, dv, received[what] = _dl_bwd(sv["dl_qk"], sv["u"], dmix, sv["dl_o"], sv["dl_w"][b], sv["dl_lse"][b], dil,
                                             f"l{l}_dl{b}_bwd", carried[what])
        dqs.append(dq)
        dks.append(dk)
        dvs.append(dv)
    du_dl = _rope_bwd(dqs, dks, dvs, cos, sin, f"l{l}_rope_bwd")

    du = jnp.concatenate([du_sb, du_cv, du_dl], axis=-1)
    dhn = _mm(du, p["w_in"], "nn", F32, f"l{l}_d_hn0")
    gr["w_in"] = _mm(du, sv["hn0"], "tn", BF16, f"l{l}_dw_in")
    dh, dy, gr["mix_norm_pre"], dg_prev = _norm_bwd(dh, (sv["h0"], p["mix_norm_pre"], dhn), prev_post, f"l{l}_in_bwd")
    return dh, dy, gr, dg_prev, received


def kernel(x, mem, positions, mix_norm_pre, w_in, cv_w, cv_b, cv_ln_g, cv_ln_b, cv_pw_w, cv_pw_b, w_out, mix_norm_post, x_norm_pre, mem_norm, x_wq, x_wk, x_wv, x_wo, x_norm_post, ffn_norm_pre, ffn_w_up, ffn_conv_w, ffn_conv_b, ffn_w_down, ffn_norm_post, loss_target, m_mix_norm_pre, m_w_in, m_cv_w, m_cv_b, m_cv_ln_g, m_cv_ln_b, m_cv_pw_w, m_cv_pw_b, m_w_out, m_mix_norm_post, m_x_norm_pre, m_mem_norm, m_x_wq, m_x_wk, m_x_wv, m_x_wo, m_x_norm_post, m_ffn_norm_pre, m_ffn_w_up, m_ffn_conv_w, m_ffn_conv_b, m_ffn_w_down, m_ffn_norm_post, v_mix_norm_pre, v_w_in, v_cv_w, v_cv_b, v_cv_ln_g, v_cv_ln_b, v_cv_pw_w, v_cv_pw_b, v_w_out, v_mix_norm_post, v_x_norm_pre, v_mem_norm, v_x_wq, v_x_wk, v_x_wv, v_x_wo, v_x_norm_post, v_ffn_norm_pre, v_ffn_w_up, v_ffn_conv_w, v_ffn_conv_b, v_ffn_w_down, v_ffn_norm_post):
    args = locals()
    wts = {n: args[n] for n in WEIGHTS}
    mom = {n: args["m_" + n] for n in WEIGHTS}
    var = {n: args["v_" + n] for n in WEIGHTS}

    x2, mem2, target = x[0], mem[0], loss_target[0]

    small_payload = _pad_rows(jnp.concatenate([wts[n].reshape(-1) for n in SMALL_SHARDED]), PAYLOAD_COLS)
    gathered_a, small = _all_gather([_pack_weights("a", wts), small_payload], "weights_all_gather")
    small = small.reshape(N_DEV, -1)
    small_full = {}
    off = 0
    for n in SMALL_SHARDED:
        size = wts[n].size
        small_full[n] = _unshard(n, small[:, off:off + size].reshape((N_DEV,) + wts[n].shape))
        off += size
    pw_first = sum(r for _, _, r in GROUPS[PW_GROUP])

    def pw_of(gathered_pw_group):
        return _unshard("cv_pw_w", gathered_pw_group[:, pw_first:, :].reshape((N_DEV,) + wts["cv_pw_w"].shape))

    def small_params(l):
        p = {n: wts[n][l][None, :] for n in REPLICATED}
        p.update({n: small_full[n][l] for n in SMALL_SHARDED})
        return p

    def out_and_pw(gathered_pw_group):
        return {"w_out": _unpack_weights(PW_GROUP, gathered_pw_group)[("w_out", 0)], "cv_pw_w": pw_of(gathered_pw_group)[0]}

    def of_layer(group, l):
        return lambda gathered: {n: w for (n, ll), w in _unpack_weights(group, gathered).items() if ll == l}

    pos = positions[0].astype(F32)
    half = HEAD_DIM // 2
    inv_freq = ROPE_THETA ** (-jnp.arange(half, dtype=F32) / half)
    ang = pos[:, None] * inv_freq
    cos = jnp.tile(jnp.cos(ang), (1, LANES // half))
    sin = jnp.tile(jnp.sin(ang), (1, LANES // half))

    p0 = small_params(0)
    p0["w_in"] = _unpack_weights("a", gathered_a)[("w_in", 0)]
    hn = _rms_fwd(x2, p0["mix_norm_pre"], "l0_in_norm")
    ffn_shards = [(wts["ffn_w_up"][l], wts["ffn_w_down"][l]) for l in range(DEPTH)]
    h, hn, sv0, gathered_m1 = _layer_fwd(0, x2, hn, p0, mem2, cos, sin, wts["mix_norm_pre"][1][None, :],
                                         _pack_weights("bx", wts), of_layer("bx", 0), ffn_shards[0], _pack_weights("m1", wts),
                                         _pack_weights(PW_GROUP, wts), out_and_pw)
    p1 = small_params(1)
    mixer1 = _unpack_weights("m1", gathered_m1)
    p1.update(w_in=mixer1[("w_in", 1)], w_out=mixer1[("w_out", 1)], cv_pw_w=pw_of(sv0["rope_gathered"])[1])
    h, _, sv1, _ = _layer_fwd(1, h, hn, p1, mem2, cos, sin, None, _pack_weights("cx", wts), of_layer("cx", 1), ffn_shards[1],
                              None)
    loss_part, dh = _loss_fwd(h, target, "loss")
    loss = lax.psum(loss_part[0, 0], ("x", "y", "c"))

    grads = {n: [None] * DEPTH for n in WEIGHTS}
    dh, dy, _, grads["ffn_norm_post"][1] = _norm_bwd(dh, None, (sv1["y_ffn"], sv1["p"]["ffn_norm_post"]), "last_post_bwd")

    def packer(l, groups):
        def pack(which, gr):
            if which not in groups:
                return None
            if groups[which] == PW_GROUP:
                return _pack_grads(PW_GROUP, {"w_out": {0: gr["w_out"]}, "cv_pw_w": [gr["cv_pw_w"], grads["cv_pw_w"][1]]})
            return _pack_grads(groups[which], {n: {l: g} for n, g in gr.items()})
        return pack

    dh, dy, gr, grads["ffn_norm_post"][0], got1 = _layer_bwd(
        1, dh, dy, sv1, mem2, cos, sin, (sv0["y_ffn"], sv0["p"]["ffn_norm_post"]), None, packer(1, {"ffn": "cf", "cross": "cx"}))
    for n, g in gr.items():
        grads[n][1] = g
    dh, _, gr, _, got0 = _layer_bwd(0, dh, dy, sv0, mem2, cos, sin, None, _pack_grads("m1", grads),
                                    packer(0, {"ffn": "bf", "cross": "bx", "out": PW_GROUP}))
    for n, g in gr.items():
        grads[n][0] = g
    grad_x = dh

    small_rows = jnp.concatenate([_to_shards(n, jnp.stack(grads[n])) for n in SMALL_SHARDED], axis=1)
    rep_flat = jnp.concatenate([jnp.stack([g.reshape(-1) for g in grads[n]]).reshape(-1) for n in REPLICATED])
    rep_rows = jnp.broadcast_to(rep_flat[None], (N_DEV, rep_flat.shape[0]))
    f32_rows = _pad_rows(jnp.concatenate([small_rows, rep_rows], axis=1), PAYLOAD_COLS)
    received_a, small_parts = _all_to_all([_pack_grads("a", grads), f32_rows], "grads_all_to_all")
    received = {"a": received_a, PW_GROUP: got0["out"], "m1": got0["ffn_payload"], "bf": got0["ffn"], "bx": got0["cross"],
                "cf": got1["ffn"], "cx": got1["cross"]}

    res = {}
    for n in BIG:
        shape = wts[n].shape
        two_d = (shape[0] * shape[1], shape[2])
        operands = (wts[n].reshape(two_d), mom[n].reshape(two_d), var[n].reshape(two_d))
        if n == "cv_pw_w":
            outs = _adamw(received[PW_GROUP][:, pw_first:, :].reshape((N_DEV,) + two_d), *operands, f"adamw_{n}")
        elif n in TRANSPOSED:
            layers = []
            for l, got in enumerate((got0, got1)):
                if n == "ffn_w_up":
                    layers.append(jnp.concatenate([got["up0"], got["up1"]], axis=1))
                else:
                    group, first, r = _where_is(n, l)
                    layers.append(received[group][:, first:first + r, :])
            parts = jnp.stack(layers, axis=1).transpose(0, 1, 3, 2).reshape((N_DEV,) + two_d)
            outs = _adamw(parts, *operands, f"adamw_{n}")
        else:
            sources = []
            for l in range(DEPTH):
                group, first, _ = _where_is(n, l)
                sources.append((received[group], first))
            outs = _adamw_packed(sources, *operands, f"adamw_{n}")
        res[n] = [o.reshape(shape) for o in outs]
    small_names = SMALL_SHARDED + REPLICATED
    flat_w = _pad_rows(jnp.concatenate([wts[n].reshape(-1) for n in small_names]), PAYLOAD_COLS)
    flat_m = _pad_rows(jnp.concatenate([mom[n].reshape(-1) for n in small_names]), PAYLOAD_COLS)
    flat_v = _pad_rows(jnp.concatenate([var[n].reshape(-1) for n in small_names]), PAYLOAD_COLS)
    outs = _adamw(small_parts, flat_w, flat_m, flat_v, "adamw_small")
    outs = [o.reshape(-1) for o in outs]
    off = 0
    for n in small_names:
        size = wts[n].size
        res[n] = [o[off:off + size].reshape(wts[n].shape) for o in outs]
        off += size

    result = [loss, grad_x[None]]
    for kind in range(4):
        result += [res[n][kind] for n in WEIGHTS]
    return tuple(result)
```

```python
import functools
import math

import jax
import jax.numpy as jnp
from jax import lax
from jax.experimental import pallas as pl
from jax.experimental.pallas import tpu as pltpu

F32, BF16 = jnp.float32, jnp.bfloat16
SDS = jax.ShapeDtypeStruct

D_MODEL = 1024
SEQ = 4096
DEPTH = 2
HEAD_DIM = 64
SB_HEADS = 4
SB_WIDTH = 256
CV_WIDTH = 256
CV_KERNEL = 31
DL_HEADS = 8
DL_WIDTH = 512
IN_WIDTH = 2816
DL_PATTERN = ((128, 1), (512, 4), (2048, 16))
BLOCK = 128
ROPE_THETA = 10000.0
N_MEM = 256
X_HEADS = 4
X_HEAD_DIM = 256
D_FF = 2816
EPS = 1e-6
N_DEV = 8
LANES = 128

ADAM_LR = 0.001
ADAM_B1 = 0.9
ADAM_B2 = 0.999
ADAM_EPS = 1e-08
ADAM_WD = 0.01
ADAM_STEP = 10

VMEM_LIMIT_BYTES = 56 * 1024 * 1024
MESH = pl.DeviceIdType.MESH
NEG = -1e30


def _params(**kw):
    return pltpu.CompilerParams(vmem_limit_bytes=VMEM_LIMIT_BYTES, **kw)


def _pick(n, cands):
    for c in cands:
        if n % c == 0:
            return c
    return n


def _mm(a, b, mode, out_dtype, name, bias=None):
    if mode == "nn":
        (m, k), (k2, n) = a.shape, b.shape
    elif mode == "nt":
        (m, k), (n, k2) = a.shape, b.shape
    else:
        (k, m), (k2, n) = a.shape, b.shape
    assert k == k2, (a.shape, b.shape, mode)
    tm = _pick(m, (1024, 1408, 512, 256, 128))
    tn = _pick(n, (1024, 1408, 512, 256, 128))
    tk = k if k <= 2048 else _pick(k, (2048, 1408, 1024, 512))
    nk = k // tk
    dims = {"nn": ((1,), (0,)), "nt": ((1,), (1,)), "tn": ((0,), (0,))}[mode]

    def body(*refs):
        refs = list(refs)
        acc_ref = refs.pop() if nk > 1 else None
        a_ref, b_ref = refs[0], refs[1]
        bias_ref = refs[2] if bias is not None else None
        o_ref = refs[-1]
        p = lax.dot_general(a_ref[...].astype(BF16), b_ref[...].astype(BF16), (dims, ((), ())),
                            preferred_element_type=F32)

        def finish(v):
            if bias_ref is not None:
                v = v + bias_ref[...]
            o_ref[...] = v.astype(out_dtype)

        if nk == 1:
            finish(p)
        else:
            kk = pl.program_id(2)

            @pl.when(kk == 0)
            def _():
                acc_ref[...] = p

            @pl.when(kk > 0)
            def _():
                acc_ref[...] += p

            @pl.when(kk == nk - 1)
            def _():
                finish(acc_ref[...])

    a_spec = pl.BlockSpec((tk, tm), lambda i, j, kk: (kk, i)) if mode == "tn" else pl.BlockSpec((tm, tk), lambda i, j, kk: (i, kk))
    b_spec = pl.BlockSpec((tn, tk), lambda i, j, kk: (j, kk)) if mode == "nt" else pl.BlockSpec((tk, tn), lambda i, j, kk: (kk, j))
    in_specs = [a_spec, b_spec]
    args = [a, b]
    if bias is not None:
        in_specs.append(pl.BlockSpec((1, tn), lambda i, j, kk: (0, j)))
        args.append(bias)
    return pl.pallas_call(
        body, name=name, out_shape=SDS((m, n), out_dtype), grid=(m // tm, n // tn, nk),
        in_specs=in_specs, out_specs=pl.BlockSpec((tm, tn), lambda i, j, kk: (i, j)),
        scratch_shapes=[pltpu.VMEM((tm, tn), F32)] if nk > 1 else [], compiler_params=_params(),
    )(*args)


def _rms(x, g):
    r = lax.rsqrt(jnp.mean(x * x, axis=-1, keepdims=True) + EPS)
    return x * r * g


def _rms_bwd(x, g, dy):
    r = lax.rsqrt(jnp.mean(x * x, axis=-1, keepdims=True) + EPS)
    xh = x * r
    dyg = dy * g
    dx = r * (dyg - xh * jnp.mean(dyg * xh, axis=-1, keepdims=True))
    return dx, dy * xh


def _rms_fwd(x, g, name):
    rows, d = x.shape
    t = min(rows, 512)

    def body(x_ref, g_ref, o_ref):
        o_ref[...] = _rms(x_ref[...], g_ref[...]).astype(BF16)

    return pl.pallas_call(
        body, name=name, out_shape=SDS((rows, d), BF16), grid=(rows // t,),
        in_specs=[pl.BlockSpec((t, d), lambda i: (i, 0)), pl.BlockSpec((1, d), lambda i: (0, 0))],
        out_specs=pl.BlockSpec((t, d), lambda i: (i, 0)), compiler_params=_params(),
    )(x, g)


def _res_norm_fwd(h, y, g_post, g_next, name):
    rows, d = h.shape
    t = 512
    has_next = g_next is not None

    def body(*refs):
        if has_next:
            h_ref, y_ref, gp_ref, gn_ref, h1_ref, hn_ref = refs
        else:
            h_ref, y_ref, gp_ref, h1_ref = refs
        h1 = h_ref[...] + _rms(y_ref[...], gp_ref[...])
        h1_ref[...] = h1
        if has_next:
            hn_ref[...] = _rms(h1, gn_ref[...]).astype(BF16)

    row = pl.BlockSpec((t, d), lambda i: (i, 0))
    vec = pl.BlockSpec((1, d), lambda i: (0, 0))
    in_specs = [row, row, vec] + ([vec] if has_next else [])
    args = [h, y, g_post] + ([g_next] if has_next else [])
    out_shape = [SDS((rows, d), F32)] + ([SDS((rows, d), BF16)] if has_next else [])
    out_specs = [row] + ([row] if has_next else [])
    res = pl.pallas_call(body, name=name, out_shape=out_shape, grid=(rows // t,), in_specs=in_specs,
                         out_specs=out_specs, compiler_params=_params())(*args)
    return (res[0], res[1]) if has_next else (res[0], None)


def _norm_bwd(dh, pre, post, name):
    rows, d = dh.shape
    t = 512
    has_pre, has_post = pre is not None, post is not None
    if has_pre:
        dhns = pre[2] if isinstance(pre[2], tuple) else (pre[2],)
        pre = (pre[0], pre[1]) + dhns

    def body(*refs):
        refs = list(refs)
        dh_ref = refs.pop(0)
        if has_pre:
            h_ref, gpre_ref = refs.pop(0), refs.pop(0)
            dhn_refs = [refs.pop(0) for _ in dhns]
        if has_post:
            y_ref, gpost_ref = refs.pop(0), refs.pop(0)
        dht_ref = refs.pop(0)
        if has_post:
            dy_ref = refs.pop(0)
        if has_pre:
            dgpre_ref = refs.pop(0)
        if has_post:
            dgpost_ref = refs.pop(0)
        i = pl.program_id(0)
        dht = dh_ref[...]
        if has_pre:
            dhn = dhn_refs[0][...]
            for r in dhn_refs[1:]:
                dhn = dhn + r[...]
            dx, dgr = _rms_bwd(h_ref[...], gpre_ref[...], dhn)
            dht = dht + dx

            @pl.when(i == 0)
            def _():
                dgpre_ref[...] = jnp.zeros_like(dgpre_ref)

            dgpre_ref[...] += jnp.sum(dgr, axis=0, keepdims=True)
        dht_ref[...] = dht
        if has_post:
            dy, dgr = _rms_bwd(y_ref[...], gpost_ref[...], dht)
            dy_ref[...] = dy.astype(BF16)

            @pl.when(i == 0)
            def _():
                dgpost_ref[...] = jnp.zeros_like(dgpost_ref)

            dgpost_ref[...] += jnp.sum(dgr, axis=0, keepdims=True)

    row = pl.BlockSpec((t, d), lambda i: (i, 0))
    vec = pl.BlockSpec((1, d), lambda i: (0, 0))
    in_specs, args = [row], [dh]
    if has_pre:
        in_specs += [row, vec] + [row] * len(dhns)
        args += list(pre)
    if has_post:
        in_specs += [row, vec]
        args += list(post)
    out_shape, out_specs = [SDS((rows, d), F32)], [row]
    if has_post:
        out_shape.append(SDS((rows, d), BF16))
        out_specs.append(row)
    if has_pre:
        out_shape.append(SDS((1, d), F32))
        out_specs.append(vec)
    if has_post:
        out_shape.append(SDS((1, d), F32))
        out_specs.append(vec)
    res = list(pl.pallas_call(body, name=name, out_shape=out_shape, grid=(rows // t,), in_specs=in_specs,
                              out_specs=out_specs, compiler_params=_params())(*args))
    dht = res.pop(0)
    dy = res.pop(0) if has_post else None
    dgpre = res.pop(0) if has_pre else None
    dgpost = res.pop(0) if has_post else None
    return dht, dy, dgpre, dgpost


def _rms_gain_grad(x, g, dy, name):
    rows, d = x.shape

    def body(x_ref, g_ref, dy_ref, dg_ref):
        _, dgr = _rms_bwd(x_ref[...], g_ref[...], dy_ref[...])
        dg_ref[...] = jnp.sum(dgr, axis=0, keepdims=True)

    return pl.pallas_call(body, name=name, out_shape=SDS((1, d), F32), compiler_params=_params())(x, g, dy)


def _loss_fwd(h, target, name):
    rows, d = h.shape
    t = 512

    def body(h_ref, t_ref, loss_ref, dh_ref):
        i = pl.program_id(0)
        err = h_ref[...] - t_ref[...]
        dh_ref[...] = err * (1.0 / d)

        @pl.when(i == 0)
        def _():
            loss_ref[...] = jnp.zeros_like(loss_ref)

        part = jnp.sum(jnp.sum(err * err, axis=1, keepdims=True), axis=0, keepdims=True) * (0.5 / d)
        loss_ref[...] += jnp.broadcast_to(part, loss_ref.shape)

    row = pl.BlockSpec((t, d), lambda i: (i, 0))
    return pl.pallas_call(
        body, name=name, out_shape=(SDS((1, LANES), F32), SDS((rows, d), F32)), grid=(rows // t,),
        in_specs=[row, row], out_specs=(pl.BlockSpec((1, LANES), lambda i: (0, 0)), row), compiler_params=_params(),
    )(h, target)


def _rot_half(x, sign):
    w = x.shape[-1]
    lane = lax.broadcasted_iota(jnp.int32, x.shape, 1)
    first = (lane % HEAD_DIM) < (HEAD_DIM // 2)
    return jnp.where(first, -sign * pltpu.roll(x, w - HEAD_DIM // 2, axis=1), sign * pltpu.roll(x, HEAD_DIM // 2, axis=1))


def _rope_fwd(u, cos, sin, name, payload=None):
    rows = u.shape[0]
    t, cw = 512, 256
    first_col = (3 * SB_WIDTH + 2 * CV_WIDTH) // cw

    def body(u_ref, c_ref, s_ref, o_ref):
        x = u_ref[...]
        c = jnp.tile(c_ref[...], (1, cw // LANES))
        s = jnp.tile(s_ref[...], (1, cw // LANES))
        o_ref[...] = x * c + _rot_half(x, 1.0) * s

    tab = pl.BlockSpec((t, LANES), lambda i, j: (i, 0))
    grid = (rows // t, 2 * DL_WIDTH // cw)
    if payload is not None:
        def when():
            first = (pl.program_id(0) == 0) & (pl.program_id(1) == 0)
            last = (pl.program_id(0) == grid[0] - 1) & (pl.program_id(1) == grid[1] - 1)
            return first, last, last
        payload = ("gather", payload, when)
    return _hosted_call(
        body, payload, name=name, out_shape=(SDS((rows, 2 * DL_WIDTH), F32),), grid=grid,
        in_specs=[pl.BlockSpec((t, cw), lambda i, j: (i, first_col + j)), tab, tab],
        out_specs=(pl.BlockSpec((t, cw), lambda i, j: (i, j)),), args=(u, cos, sin))


def _rope_bwd(dqs, dks, dvs, cos, sin, name):
    rows = dqs[0].shape[0]
    t, w = 256, DL_WIDTH

    def body(*refs):
        c = jnp.tile(refs[9][...], (1, w // LANES))
        s = jnp.tile(refs[10][...], (1, w // LANES))
        o_ref = refs[11]
        dq = refs[0][...] + refs[1][...] + refs[2][...]
        dk = refs[3][...] + refs[4][...] + refs[5][...]
        dv = refs[6][...] + refs[7][...] + refs[8][...]
        o_ref[:, 0:w] = (dq * c + _rot_half(dq, -1.0) * s).astype(BF16)
        o_ref[:, w:2 * w] = (dk * c + _rot_half(dk, -1.0) * s).astype(BF16)
        o_ref[:, 2 * w:3 * w] = dv.astype(BF16)

    row = pl.BlockSpec((t, w), lambda i: (i, 0))
    tab = pl.BlockSpec((t, LANES), lambda i: (i, 0))
    return pl.pallas_call(
        body, name=name, out_shape=SDS((rows, 3 * w), BF16), grid=(rows // t,), in_specs=[row] * 9 + [tab, tab],
        out_specs=pl.BlockSpec((t, 3 * w), lambda i: (i, 0)), compiler_params=_params(),
    )(*dqs, *dks, *dvs, cos, sin)


SB_TILE = 256
SB_ZERO_AFTER = 110.0
SB_FIRST_BLOCK = (8, LANES)


def _softplus(z):
    return jnp.maximum(z, 0.0) + jnp.log(1.0 + jnp.exp(-jnp.abs(z)))


def _split_dot(x, tri, passes):
    acc = None
    rem = x
    for _ in range(passes):
        part = rem.astype(BF16)
        rem = rem - part.astype(F32)
        d = jnp.dot(part, tri, preferred_element_type=F32)
        acc = d if acc is None else acc + d
    return acc


def _tri(t, rel):
    j = lax.broadcasted_iota(jnp.int32, (t, t), 0)
    s = lax.broadcasted_iota(jnp.int32, (t, t), 1)
    return rel(j, s).astype(BF16)


def _sb_masks(t, i):
    row = lax.broadcasted_iota(jnp.int32, (t, t), 0)
    col = lax.broadcasted_iota(jnp.int32, (t, t), 1)
    return col < row, (row >= 0) & (i >= 1)


def _sb_fwd(q, k, v, payload, name):
    h, s_len, hd = q.shape
    t = SB_TILE
    nq = s_len // t
    scale = hd ** -0.5

    def body(q_ref, k_ref, v_ref, pay_ref, o_ref, tot_ref, first_ref, gathered_ref, send_sems, recv_sems, local_sem):
        hh, i = pl.program_id(0), pl.program_id(1)
        start, forward, finish = _gather_steps(pay_ref, gathered_ref, send_sems, recv_sems, local_sem)
        pl.when((hh == 0) & (i == 0))(start)
        pl.when((hh == h - 1) & (i == nq - 1))(forward)
        qv = q_ref[0] * scale
        upper = _tri(t, lambda j, s: j > s)

        def tiles(js, carry, masks=(None, None)):
            acc, run = carry
            starts = [pl.multiple_of(j * t, t) for j in js]
            zs = [lax.dot_general(qv, k_ref[0, pl.ds(st, t), :], (((1,), (1,)), ((), ())), preferred_element_type=F32)
                  for st in starts]
            sps = [_softplus(z) for z in zs]
            sps = [sp if m is None else jnp.where(m, sp, 0.0) for sp, m in zip(sps, masks)]
            laters = [_split_dot(sp, upper, 2) for sp in sps]
            for st, z, sp, later, m in zip(starts, zs, sps, laters, masks):
                a = jnp.exp((z - sp) - (run + later))
                if m is not None:
                    a = jnp.where(m, a, 0.0)
                acc = acc + jnp.dot(a.astype(BF16), v_ref[0, pl.ds(st, t), :], preferred_element_type=F32)
                run = run + jnp.sum(sp, axis=1, keepdims=True)
            return acc, run

        def live(carry):
            return jnp.min(carry[1]) < SB_ZERO_AFTER

        below, whole = _sb_masks(t, i)
        top = jnp.maximum(i - 1, 0)
        carry = tiles([i, top], (jnp.zeros((t, hd), F32), jnp.zeros((t, 1), F32)), (below, whole))

        def pair(state):
            pp, carry = state
            j = top - 1 - 2 * pp
            return pp + 1, tiles([j, j - 1], carry)

        pairs, carry = lax.while_loop(lambda st: (st[0] < top // 2) & live(st[1]), pair, (0, carry))
        last = ((top % 2 == 1) & (pairs == top // 2) & live(carry)).astype(jnp.int32)
        acc, run = lax.fori_loop(0, last, lambda _, c: tiles([0], c), carry)
        o_ref[0] = acc.astype(BF16)
        tot_ref[0] = run
        first_ref[...] = jnp.full(first_ref.shape, top - 2 * pairs - last, jnp.int32).astype(F32)
        pl.when((hh == h - 1) & (i == nq - 1))(finish)

    full = pl.BlockSpec((1, s_len, hd), lambda hh, i: (hh, 0, 0))
    tile = pl.BlockSpec((1, t, hd), lambda hh, i: (hh, i, 0))
    hbm = pl.BlockSpec(memory_space=pl.ANY)
    return pl.pallas_call(
        body, name=name,
        out_shape=(SDS((h, s_len, hd), BF16), SDS((h, s_len, 1), F32), SDS((h, nq) + SB_FIRST_BLOCK, F32),
                   SDS((N_DEV,) + payload.shape, payload.dtype)),
        grid=(h, nq), in_specs=[tile, full, full, hbm],
        out_specs=(tile, pl.BlockSpec((1, t, 1), lambda hh, i: (hh, i, 0)),
                   pl.BlockSpec((1, 1) + SB_FIRST_BLOCK, lambda hh, i: (hh, i, 0, 0)), hbm),
        scratch_shapes=_COMM_SEMAPHORES, compiler_params=_params(has_side_effects=True),
    )(q, k, v, payload)


def _sb_bwd(q, k, v, do, tot, first, payload, name):
    h, s_len, hd = q.shape
    t = SB_TILE
    nq = s_len // t
    scale = hd ** -0.5

    def body(q_ref, k_ref, v_ref, do_ref, tot_ref, first_ref, pay_ref, dq_ref, dk_ref, dv_ref, received_ref, dk_acc, dv_acc,
             send_sems, recv_sems, local_sem):
        hh, i = pl.program_id(0), pl.program_id(1)
        start, finish = _exchange_steps(pay_ref, received_ref, send_sems, recv_sems, local_sem)
        pl.when((hh == 0) & (i == 0))(start)

        @pl.when(i == 0)
        def _():
            dk_acc[...] = jnp.zeros_like(dk_acc)
            dv_acc[...] = jnp.zeros_like(dv_acc)

        qv = q_ref[0] * scale
        dov = do_ref[0]
        total = tot_ref[0]
        upto = _tri(t, lambda j, s: j <= s)
        before = _tri(t, lambda j, s: j < s)
        nt_dims = (((1,), (1,)), ((), ()))
        tn_dims = (((0,), (0,)), ((), ()))

        def tiles(js, carry, masks=(None, None)):
            dq, run_sp, run_g = carry
            starts = [pl.multiple_of(j * t, t) for j in js]
            zs = [lax.dot_general(qv, k_ref[0, pl.ds(st, t), :], nt_dims, preferred_element_type=F32) for st in starts]
            das = [lax.dot_general(dov, v_ref[0, pl.ds(st, t), :], nt_dims, preferred_element_type=F32) for st in starts]
            sps = [_softplus(z) for z in zs]
            log_sigs = [z - sp for z, sp in zip(zs, sps)]
            sps = [sp if m is None else jnp.where(m, sp, 0.0) for sp, m in zip(sps, masks)]
            pres = [_split_dot(sp, upto, 2) for sp in sps]
            a_s, gs = [], []
            for sp, log_sig, pre, da, m in zip(sps, log_sigs, pres, das, masks):
                a = jnp.exp(log_sig - (total - (run_sp + pre)))
                if m is not None:
                    a = jnp.where(m, a, 0.0)
                a_s.append(a)
                gs.append(a * da)
                run_sp = run_sp + jnp.sum(sp, axis=1, keepdims=True)
            g_pres = [_split_dot(g, before, 3) for g in gs]
            for st, a, g, g_pre, log_sig, m in zip(starts, a_s, gs, g_pres, log_sigs, masks):
                sig = jnp.exp(log_sig)
                dz = g * (1.0 - sig) - sig * (run_g + g_pre)
                if m is not None:
                    dz = jnp.where(m, dz, 0.0)
                dz = dz.astype(BF16)
                dq = dq + jnp.dot(dz, k_ref[0, pl.ds(st, t), :], preferred_element_type=F32)
                dk_acc[pl.ds(st, t), :] += lax.dot_general(dz, qv, tn_dims, preferred_element_type=F32)
                dv_acc[pl.ds(st, t), :] += lax.dot_general(a.astype(BF16), dov, tn_dims, preferred_element_type=F32)
                run_g = run_g + jnp.sum(g, axis=1, keepdims=True)
            return dq, run_sp, run_g

        zero = jnp.zeros((t, 1), F32)
        top = jnp.maximum(i - 1, 0)
        first = jnp.clip(first_ref[0, 0, 0, 0].astype(jnp.int32), 0, top)
        count = top - first
        carry = lax.fori_loop(0, count // 2, lambda pp, c: tiles([first + 2 * pp, first + 2 * pp + 1], c),
                              (jnp.zeros((t, hd), F32), zero, zero))
        carry = lax.fori_loop(0, count % 2, lambda _, c: tiles([top - 1], c), carry)
        below, whole = _sb_masks(t, i)
        dq, _, _ = tiles([top, i], carry, (whole, below))
        dq_ref[0] = (dq * scale).astype(BF16)

        @pl.when(i == nq - 1)
        def _():
            dk_ref[0] = dk_acc[...].astype(BF16)
            dv_ref[0] = dv_acc[...].astype(BF16)

        pl.when((hh == h - 1) & (i == nq - 1))(finish)

    full = pl.BlockSpec((1, s_len, hd), lambda hh, i: (hh, 0, 0))
    tile = pl.BlockSpec((1, t, hd), lambda hh, i: (hh, i, 0))
    hbm = pl.BlockSpec(memory_space=pl.ANY)
    out = SDS((h, s_len, hd), BF16)
    return pl.pallas_call(
        body, name=name, out_shape=(out, out, out, SDS(payload.shape, payload.dtype)), grid=(h, nq),
        in_specs=[tile, full, full, tile, pl.BlockSpec((1, t, 1), lambda hh, i: (hh, i, 0)),
                  pl.BlockSpec((1, 1) + SB_FIRST_BLOCK, lambda hh, i: (hh, i, 0, 0)), hbm],
        out_specs=(tile, full, full, hbm),
        scratch_shapes=[pltpu.VMEM((s_len, hd), F32), pltpu.VMEM((s_len, hd), F32)] + _COMM_SEMAPHORES,
        compiler_params=_params(has_side_effects=True),
    )(q, k, v, do, tot, first, payload)


def _dl_scores(qv, kk, n):
    s = lax.dot_general(qv, kk, (((1,), (1,)), ((), ())), preferred_element_type=F32) * (HEAD_DIM ** -0.5)
    r = lax.broadcasted_iota(jnp.int32, s.shape, 0)
    c = lax.broadcasted_iota(jnp.int32, s.shape, 1)
    valid = (c >= r) & (c - r <= BLOCK) & ((n > 0) | (c >= BLOCK))
    return jnp.where(valid, s, NEG)


DL_UNROLL = 8
DL_FWD_UNROLL = 8
DL_PAIR = 2 * HEAD_DIM
DL_Q_BLOCK0 = 0
DL_K_BLOCK0 = DL_WIDTH // DL_PAIR
DL_V_BLOCK0 = (IN_WIDTH - DL_WIDTH) // DL_PAIR
DL_DO_BLOCK0 = (SB_WIDTH + CV_WIDTH) // DL_PAIR


def _dl_rows(idx, nb, dil):
    r, n = idx // nb, idx % nb
    cur = pl.ds(r + n * (BLOCK * dil), BLOCK, stride=dil)
    prev = pl.ds(r + jnp.maximum(n - 1, 0) * (BLOCK * dil), BLOCK, stride=dil)
    return n, cur, prev


def _dl_window(ref, cur, prev):
    return jnp.concatenate([ref[prev, :], ref[cur, :]], axis=0).astype(BF16)


def _head_lanes():
    first = lax.broadcasted_iota(jnp.int32, (BLOCK, DL_PAIR), 1) < HEAD_DIM
    return first, jnp.logical_not(first)


def _dl_fwd(qk, u, dil, name, payload=None):
    s_len = qk.shape[0]
    nb = s_len // dil // BLOCK

    def body(q_ref, k_ref, v_ref, o_ref, lse_ref):
        heads = _head_lanes()

        def step(idx, _):
            n, cur, prev = _dl_rows(idx, nb, dil)
            q = q_ref[cur, :]
            kk = _dl_window(k_ref, cur, prev)
            vv = _dl_window(v_ref, cur, prev)
            o, lse = None, None
            for lanes in heads:
                s = _dl_scores(jnp.where(lanes, q, 0.0).astype(BF16), kk, n)
                m = jnp.max(s, axis=-1, keepdims=True)
                p = jnp.exp(s - m)
                den = jnp.sum(p, axis=-1, keepdims=True)
                o_h = jnp.dot((p / den).astype(BF16), vv, preferred_element_type=F32)
                lse_h = jnp.broadcast_to(m + jnp.log(den), (BLOCK, DL_PAIR))
                o = o_h if o is None else jnp.where(heads[0], o, o_h)
                lse = lse_h if lse is None else jnp.where(heads[0], lse, lse_h)
            o_ref[cur, :] = o
            lse_ref[cur, :] = lse
            return 0

        lax.fori_loop(0, s_len // BLOCK, step, 0, unroll=DL_FWD_UNROLL)

    col = lambda first: pl.BlockSpec((s_len, DL_PAIR), lambda i: (0, first + i))
    out = SDS((s_len, DL_WIDTH), F32)
    steps = DL_WIDTH // DL_PAIR
    if payload is not None:
        step = lambda: pl.program_id(0)
        payload = ("gather", payload, lambda: (step() == 0, step() == steps - 1, step() == steps - 1))
    return _hosted_call(body, payload, name=name, out_shape=(out, out), grid=(steps,),
                        in_specs=[col(DL_Q_BLOCK0), col(DL_K_BLOCK0), col(DL_V_BLOCK0)], out_specs=(col(0), col(0)),
                        args=(qk, qk, u))


def _dl_bwd(qk, u, dmix, o_mix, wt, lse, dil, name, payload=None):
    s_len = qk.shape[0]
    nb = s_len // dil // BLOCK
    scale = HEAD_DIM ** -0.5
    nt_dims = (((1,), (1,)), ((), ()))
    tn_dims = (((0,), (0,)), ((), ()))

    def body(q_ref, k_ref, v_ref, do_ref, om_ref, wt_ref, lse_ref, dq_ref, dk_ref, dv_ref):
        dk_ref[...] = jnp.zeros_like(dk_ref)
        dv_ref[...] = jnp.zeros_like(dv_ref)
        heads = _head_lanes()

        def step(idx, _):
            n, cur, prev = _dl_rows(idx, nb, dil)
            q = q_ref[cur, :]
            kk = _dl_window(k_ref, cur, prev)
            vv = _dl_window(v_ref, cur, prev)
            dov = do_ref[cur, :]
            d_lanes = dov * om_ref[cur, :]
            w_lanes = wt_ref[cur, :]
            lse_lanes = lse_ref[cur, :]
            dq, dkk, dvv = None, None, None
            for lanes in heads:
                qm = jnp.where(lanes, q, 0.0).astype(BF16)
                s = _dl_scores(qm, kk, n)
                p = jnp.exp(s - jnp.max(jnp.where(lanes, lse_lanes, NEG), axis=-1, keepdims=True))
                w = jnp.max(jnp.where(lanes, w_lanes, 0.0), axis=-1, keepdims=True)
                d_all = jnp.sum(jnp.where(lanes, d_lanes, 0.0), axis=-1, keepdims=True)
                do_n = jnp.where(lanes, dov * w, 0.0).astype(BF16)
                dp = lax.dot_general(do_n, vv, nt_dims, preferred_element_type=F32)
                ds = (p * (dp - w * d_all) * scale).astype(BF16)
                dq_h = jnp.dot(ds, kk, preferred_element_type=F32)
                dkk_h = lax.dot_general(ds, qm, tn_dims, preferred_element_type=F32)
                dvv_h = lax.dot_general(p.astype(BF16), do_n, tn_dims, preferred_element_type=F32)
                dq = dq_h if dq is None else jnp.where(heads[0], dq, dq_h)
                dkk = dkk_h if dkk is None else dkk + dkk_h
                dvv = dvv_h if dvv is None else dvv + dvv_h
            dq_ref[cur, :] = dq
            dk_ref[prev, :] += dkk[:BLOCK]
            dv_ref[prev, :] += dvv[:BLOCK]
            dk_ref[cur, :] += dkk[BLOCK:]
            dv_ref[cur, :] += dvv[BLOCK:]
            return 0

        lax.fori_loop(0, s_len // BLOCK, step, 0, unroll=DL_UNROLL)

    col = lambda first: pl.BlockSpec((s_len, DL_PAIR), lambda i: (0, first + i))
    out = SDS((s_len, DL_WIDTH), F32)
    steps = DL_WIDTH // DL_PAIR
    if payload is not None:
        step = lambda: pl.program_id(0)
        payload = ("all_to_all", payload, lambda: (step() == 0, None, step() == steps - 1))
    return _hosted_call(
        body, payload, name=name, out_shape=(out, out, out), grid=(steps,),
        in_specs=[col(DL_Q_BLOCK0), col(DL_K_BLOCK0), col(DL_V_BLOCK0), col(DL_DO_BLOCK0), col(0), col(0), col(0)],
        out_specs=(col(0), col(0), col(0)), args=(qk, qk, u, dmix, o_mix, wt, lse))


def _dl_mix_fwd(outs, lses, name):
    rows, w = outs[0].shape
    t = 256

    def body(o1, o2, o3, l1, l2, l3, ob_ref, of_ref, w1, w2, w3):
        a, b, c = l1[...], l2[...], l3[...]
        m = jnp.maximum(jnp.maximum(a, b), c)
        ea, eb, ec = jnp.exp(a - m), jnp.exp(b - m), jnp.exp(c - m)
        den = ea + eb + ec
        wa, wb, wc = ea / den, eb / den, ec / den
        o = wa * o1[...] + wb * o2[...] + wc * o3[...]
        ob_ref[...] = o.astype(BF16)
        of_ref[...] = o
        w1[...] = wa
        w2[...] = wb
        w3[...] = wc

    row = pl.BlockSpec((t, w), lambda i: (i, 0))
    f = SDS((rows, w), F32)
    return pl.pallas_call(body, name=name, out_shape=(SDS((rows, w), BF16), f, f, f, f), grid=(rows // t,),
                          in_specs=[row] * 6, out_specs=(row,) * 5, compiler_params=_params())(*outs, *lses)


def _x_probs(qh, kh):
    s = lax.dot_general(qh, kh, (((1,), (1,)), ((), ())), preferred_element_type=F32) * (X_HEAD_DIM ** -0.5)
    e = jnp.exp(s - jnp.max(s, axis=-1, keepdims=True))
    return e / jnp.sum(e, axis=-1, keepdims=True)


def _xattn_fwd(q, k, v, name):
    rows, d = q.shape
    t = 512

    def body(q_ref, k_ref, v_ref, o_ref):
        for hh in range(X_HEADS):
            cols = slice(hh * X_HEAD_DIM, (hh + 1) * X_HEAD_DIM)
            p = _x_probs(q_ref[:, cols], k_ref[:, cols])
            o_ref[:, cols] = jnp.dot(p.astype(BF16), v_ref[:, cols], preferred_element_type=F32).astype(BF16)

    row = pl.BlockSpec((t, d), lambda i: (i, 0))
    mem = pl.BlockSpec((N_MEM, d), lambda i: (0, 0))
    return pl.pallas_call(body, name=name, out_shape=SDS((rows, d), BF16), grid=(rows // t,), in_specs=[row, mem, mem],
                          out_specs=row, compiler_params=_params())(q, k, v)


def _xattn_bwd(q, k, v, do, name):
    rows, d = q.shape
    t = 512
    scale = X_HEAD_DIM ** -0.5

    def body(q_ref, k_ref, v_ref, do_ref, dq_ref, dk_ref, dv_ref):
        @pl.when(pl.program_id(0) == 0)
        def _():
            dk_ref[...] = jnp.zeros_like(dk_ref)
            dv_ref[...] = jnp.zeros_like(dv_ref)

        for hh in range(X_HEADS):
            cols = slice(hh * X_HEAD_DIM, (hh + 1) * X_HEAD_DIM)
            qh, kh, vh, doh = q_ref[:, cols], k_ref[:, cols], v_ref[:, cols], do_ref[:, cols]
            p = _x_probs(qh, kh)
            dp = lax.dot_general(doh, vh, (((1,), (1,)), ((), ())), preferred_element_type=F32)
            ds = (p * (dp - jnp.sum(p * dp, axis=-1, keepdims=True)) * scale).astype(BF16)
            dq_ref[:, cols] = jnp.dot(ds, kh, preferred_element_type=F32).astype(BF16)
            dk_ref[:, cols] += lax.dot_general(ds, qh, (((0,), (0,)), ((), ())), preferred_element_type=F32)
            dv_ref[:, cols] += lax.dot_general(p.astype(BF16), doh, (((0,), (0,)), ((), ())), preferred_element_type=F32)

    row = pl.BlockSpec((t, d), lambda i: (i, 0))
    mem = pl.BlockSpec((N_MEM, d), lambda i: (0, 0))
    return pl.pallas_call(
        body, name=name, out_shape=(SDS((rows, d), BF16), SDS((N_MEM, d), F32), SDS((N_MEM, d), F32)), grid=(rows // t,),
        in_specs=[row, mem, mem, row], out_specs=(row, mem, mem), compiler_params=_params(),
    )(q, k, v, do)


CV_TILE = 256
CV_HALO = 32
CV_LEAD = CV_HALO - (CV_KERNEL - 1)


def _shifted(win, off, rows):
    n = win.shape[0]
    return pltpu.roll(win, (n - off) % n, axis=0)[:rows]


def _glu(val, gate):
    return val * jax.nn.sigmoid(gate)


def _ln_parts(c):
    mu = jnp.mean(c, axis=-1, keepdims=True)
    xc = c - mu
    rstd = lax.rsqrt(jnp.mean(xc * xc, axis=-1, keepdims=True) + EPS)
    return xc * rstd, rstd


def _cv_fwd(u, cv_w, cv_b, ln_g, ln_b, name):
    rows = u.shape[0]
    t, w = CV_TILE, CV_WIDTH
    val_col = 3 * SB_WIDTH // w
    ratio = t // CV_HALO

    def body(val_ref, gate_ref, pval_ref, pgate_ref, w_ref, b_ref, g_ref, beta_ref, s_ref, c_ref):
        i = pl.program_id(0)
        hist = jnp.where(i > 0, _glu(pval_ref[...], pgate_ref[...]), 0.0)
        win = jnp.concatenate([hist, _glu(val_ref[...], gate_ref[...])], axis=0)
        acc = jnp.broadcast_to(b_ref[...], (t, w))
        for kk in range(CV_KERNEL):
            acc = acc + _shifted(win, CV_LEAD + kk, t) * w_ref[kk:kk + 1, :]
        c_ref[...] = acc
        n, _ = _ln_parts(acc)
        y = n * g_ref[...] + beta_ref[...]
        s_ref[...] = (y * jax.nn.sigmoid(y)).astype(BF16)

    cur = lambda col: pl.BlockSpec((t, w), lambda i: (i, col))
    prev = lambda col: pl.BlockSpec((CV_HALO, w), lambda i: (jnp.maximum(i * ratio - 1, 0), col))
    vec = pl.BlockSpec((1, w), lambda i: (0, 0))
    return pl.pallas_call(
        body, name=name, out_shape=(SDS((rows, w), BF16), SDS((rows, w), F32)), grid=(rows // t,),
        in_specs=[cur(val_col), cur(val_col + 1), prev(val_col), prev(val_col + 1),
                  pl.BlockSpec((CV_KERNEL, w), lambda i: (0, 0)), vec, vec, vec],
        out_specs=(pl.BlockSpec((t, w), lambda i: (i, 0)),) * 2, compiler_params=_params(),
    )(u, u, u, u, cv_w, cv_b, ln_g, ln_b)


def _cv_bwd(u, c, ds, db_out, cv_w, ln_g, ln_b, name, payload=None):
    rows = u.shape[0]
    t, w = CV_TILE, CV_WIDTH
    val_col = 3 * SB_WIDTH // w
    ratio = t // CV_HALO
    nt = rows // t

    def conv_out_grad(c_v, ds_v, g_v, beta_v):
        n, rstd = _ln_parts(c_v)
        y = n * g_v + beta_v
        sig = jax.nn.sigmoid(y)
        dy = ds_v * (sig * (1.0 + y * (1.0 - sig)))
        dn = dy * g_v
        dc = rstd * (dn - jnp.mean(dn, axis=-1, keepdims=True) - n * jnp.mean(dn * n, axis=-1, keepdims=True))
        return dc, dy, n

    def body(val_ref, gate_ref, pval_ref, pgate_ref, c_ref, nc_ref, ds_ref, nds_ref, dbo_ref, w_ref, g_ref, beta_ref,
             dvg_ref, dw_ref, db_ref, dg_ref, dbeta_ref, dpwb_ref):
        i = pl.program_id(0)

        @pl.when(i == 0)
        def _():
            for r in (dw_ref, db_ref, dg_ref, dbeta_ref, dpwb_ref):
                r[...] = jnp.zeros_like(r)

        g_v, beta_v = g_ref[...], beta_ref[...]
        dc, dy, n = conv_out_grad(c_ref[...], ds_ref[...], g_v, beta_v)
        dc_next, _, _ = conv_out_grad(nc_ref[...], nds_ref[...], g_v, beta_v)
        dc_next = jnp.where(i < nt - 1, dc_next, 0.0)
        dg_ref[...] += jnp.sum(dy * n, axis=0, keepdims=True)
        dbeta_ref[...] += jnp.sum(dy, axis=0, keepdims=True)
        db_ref[...] += jnp.sum(dc, axis=0, keepdims=True)
        dpwb_ref[...] += jnp.sum(dbo_ref[...], axis=0, keepdims=True)

        val, gate = val_ref[...], gate_ref[...]
        hist = jnp.where(i > 0, _glu(pval_ref[...], pgate_ref[...]), 0.0)
        win = jnp.concatenate([hist, _glu(val, gate)], axis=0)
        dc_ext = jnp.concatenate([dc, dc_next], axis=0)
        dglu = jnp.zeros((t, w), F32)
        for kk in range(CV_KERNEL):
            dw_ref[kk:kk + 1, :] += jnp.sum(dc * _shifted(win, CV_LEAD + kk, t), axis=0, keepdims=True)
            dglu = dglu + _shifted(dc_ext, CV_KERNEL - 1 - kk, t) * w_ref[kk:kk + 1, :]
        sig = jax.nn.sigmoid(gate)
        dvg_ref[:, 0:w] = (dglu * sig).astype(BF16)
        dvg_ref[:, w:2 * w] = (dglu * val * sig * (1.0 - sig)).astype(BF16)

    cur = lambda col: pl.BlockSpec((t, w), lambda i: (i, col))
    prev = lambda col: pl.BlockSpec((CV_HALO, w), lambda i: (jnp.maximum(i * ratio - 1, 0), col))
    nxt = pl.BlockSpec((CV_HALO, w), lambda i: (jnp.minimum((i + 1) * ratio, rows // CV_HALO - 1), 0))
    vec = pl.BlockSpec((1, w), lambda i: (0, 0))
    if payload is not None:
        payload = ("all_to_all", payload, lambda: (pl.program_id(0) == 0, None, pl.program_id(0) == nt - 1))
    return _hosted_call(
        body, payload, name=name,
        out_shape=(SDS((rows, 2 * w), BF16), SDS((CV_HALO, w), F32), SDS((1, w), F32), SDS((1, w), F32), SDS((1, w), F32),
                   SDS((1, w), F32)),
        grid=(nt,),
        in_specs=[cur(val_col), cur(val_col + 1), prev(val_col), prev(val_col + 1), cur(0), nxt, cur(0), nxt, cur(0),
                  pl.BlockSpec((CV_KERNEL, w), lambda i: (0, 0)), vec, vec],
        out_specs=(pl.BlockSpec((t, 2 * w), lambda i: (i, 0)), pl.BlockSpec((CV_HALO, w), lambda i: (0, 0)), vec, vec, vec, vec),
        args=(u, u, u, u, c, c, ds, ds, db_out, cv_w, ln_g, ln_b))


FFN_TILE = 512
FFN_CHUNK = 128
FFN_COLS = 256
FFN_HALO = 8
FFN_KERNEL = 3
N_FF_BLOCKS = D_FF // FFN_COLS


def _conv3(prev8, cur, w_ref, b_ref, first):
    t = cur.shape[0]
    win = jnp.concatenate([jnp.where(first, 0.0, prev8), cur], axis=0)
    return (b_ref[...] + _shifted(win, FFN_HALO - 2, t) * w_ref[0:1, :] + _shifted(win, FFN_HALO - 1, t) * w_ref[1:2, :]
            + cur * w_ref[2:3, :])


def _gelu_gate(gate, val):
    return jax.nn.gelu(gate, approximate=True) * val


GELU_C0 = math.sqrt(2.0 / math.pi)
GELU_C1 = 0.044715


def _gelu_gate_bwd(gate, val, dout):
    sq = gate * gate
    th = jnp.tanh(GELU_C0 * gate * (1.0 + GELU_C1 * sq))
    half_cdf = 0.5 * (1.0 + th)
    slope = half_cdf + 0.5 * gate * (1.0 - th * th) * (GELU_C0 * (1.0 + 3.0 * GELU_C1 * sq))
    return dout * val * slope, dout * (gate * half_cdf)


def _ffn_specs(t):
    ratio = t // FFN_HALO
    cur = pl.BlockSpec((t, FFN_COLS), lambda j, i: (i, j))
    prev = pl.BlockSpec((FFN_HALO, FFN_COLS), lambda j, i: (jnp.maximum(i * ratio - 1, 0), j))
    wsp = pl.BlockSpec((FFN_KERNEL, FFN_COLS), lambda j, i: (0, j))
    bsp = pl.BlockSpec((1, FFN_COLS), lambda j, i: (0, j))
    return cur, prev, wsp, bsp


def _ffn_host_steps(row_tiles):
    def when():
        j, i = pl.program_id(0), pl.program_id(1)
        return (j == 0) & (i == 0), (j == (3 * N_FF_BLOCKS) // 4) & (i == 0), (j == N_FF_BLOCKS - 1) & (i == row_tiles - 1)
    return when


def _ffn_act_fwd(up_g, up_v, w_g, w_v, b_g, b_v, name, payload=None):
    rows = up_g.shape[0]
    t = FFN_TILE
    cur, prev, wsp, bsp = _ffn_specs(t)

    def body(g_ref, v_ref, pg_ref, pv_ref, wg_ref, wv_ref, bg_ref, bv_ref, o_ref):
        first = pl.program_id(1) == 0
        gate = _conv3(pg_ref[...], g_ref[...], wg_ref, bg_ref, first)
        val = _conv3(pv_ref[...], v_ref[...], wv_ref, bv_ref, first)
        o_ref[...] = _gelu_gate(gate, val).astype(BF16)

    if payload is not None:
        payload = ("gather", payload, _ffn_host_steps(rows // t))
    return _hosted_call(
        body, payload, name=name, out_shape=(SDS((rows, D_FF), BF16),), grid=(N_FF_BLOCKS, rows // t),
        in_specs=[cur, cur, prev, prev, wsp, wsp, bsp, bsp], out_specs=(cur,), args=(up_g, up_v, up_g, up_v, w_g, w_v, b_g, b_v))


def _ffn_act_bwd(up_g, up_v, dact, w_g, w_v, b_g, b_v, name, payload=None):
    rows = up_g.shape[0]
    t = FFN_TILE
    ch = FFN_CHUNK
    che = ch + FFN_HALO
    ratio = t // FFN_HALO
    nt = rows // t
    cur, prev, wsp, bsp = _ffn_specs(t)
    nxt = pl.BlockSpec((FFN_HALO, FFN_COLS), lambda j, i: (jnp.minimum((i + 1) * ratio, rows // FFN_HALO - 1), j))

    def body(g_ref, v_ref, pg_ref, pv_ref, ng_ref, nv_ref, da_ref, nda_ref, wg_ref, wv_ref, bg_ref, bv_ref,
             dug_ref, duv_ref, dwg_ref, dwv_ref, dbg_ref, dbv_ref, win_g, win_v, da_win):
        i = pl.program_id(1)
        first = i == 0

        @pl.when(first)
        def _():
            for r in (dwg_ref, dwv_ref, dbg_ref, dbv_ref):
                r[...] = jnp.zeros_like(r)

        for win, pre, x, nx in ((win_g, pg_ref, g_ref, ng_ref), (win_v, pv_ref, v_ref, nv_ref)):
            win[0:FFN_HALO, :] = jnp.where(first, 0.0, pre[...])
            win[FFN_HALO:FFN_HALO + t, :] = x[...]
            win[FFN_HALO + t:, :] = nx[...]
        da_win[0:t, :] = da_ref[...]
        da_win[t:, :] = jnp.where(i < nt - 1, nda_ref[...], 0.0)
        halves = ((win_g, wg_ref, bg_ref, dug_ref), (win_v, wv_ref, bv_ref, duv_ref))

        def chunk(c, sums):
            base = pl.multiple_of(c * ch, ch)
            taps, convs = [], []
            for win, w_ref, b_ref, _ in halves:
                w = win[pl.ds(base, ch + 2 * FFN_HALO), :]
                shifted = [_shifted(w, FFN_HALO - 2 + kk, che) for kk in range(FFN_KERNEL)]
                taps.append(shifted)
                convs.append(b_ref[...] + sum(s * w_ref[kk:kk + 1, :] for kk, s in enumerate(shifted)))
            dconvs = _gelu_gate_bwd(*convs, da_win[pl.ds(base, che), :])
            new = []
            for dc_ext, shifted, (_, w_ref, _, du_ref), (dw, db) in zip(dconvs, taps, halves, sums):
                dc = dc_ext[:ch]
                du_ref[pl.ds(base, ch), :] = (dc * w_ref[2:3, :] + _shifted(dc_ext, 1, ch) * w_ref[1:2, :]
                                              + _shifted(dc_ext, 2, ch) * w_ref[0:1, :]).astype(BF16)
                dw = [dw[kk] + jnp.sum(dc * shifted[kk][:ch], axis=0, keepdims=True) for kk in range(FFN_KERNEL)]
                new.append((dw, db + jnp.sum(dc, axis=0, keepdims=True)))
            return new

        zero = jnp.zeros((1, FFN_COLS), F32)
        sums = lax.fori_loop(0, t // ch, chunk, [([zero] * FFN_KERNEL, zero)] * 2, unroll=2)
        for (dw, db), dw_ref, db_ref in zip(sums, (dwg_ref, dwv_ref), (dbg_ref, dbv_ref)):
            for kk in range(FFN_KERNEL):
                dw_ref[kk:kk + 1, :] += dw[kk]
            db_ref[...] += db

    big, wshape, bshape = SDS((rows, D_FF), BF16), SDS((FFN_KERNEL, D_FF), F32), SDS((1, D_FF), F32)
    if payload is not None:
        payload = ("all_to_all", payload, _ffn_host_steps(nt))
    window = pltpu.VMEM((t + 2 * FFN_HALO, FFN_COLS), F32)
    return _hosted_call(
        body, payload, name=name, out_shape=(big, big, wshape, wshape, bshape, bshape), grid=(N_FF_BLOCKS, nt),
        in_specs=[cur, cur, prev, prev, nxt, nxt, cur, nxt, wsp, wsp, bsp, bsp], out_specs=(cur, cur, wsp, wsp, bsp, bsp),
        scratch_shapes=(window, window, pltpu.VMEM((t + FFN_HALO, FFN_COLS), F32)),
        args=(up_g, up_v, up_g, up_v, up_g, up_v, dact, dact, w_g, w_v, b_g, b_v))


def _adamw_update(parts, w_ref, m_ref, v_ref, g_ref, d_ref, nm_ref, nv_ref):
    g = parts[0].astype(F32)
    for s in range(1, N_DEV):
        g = g + parts[s].astype(F32)
    nm = ADAM_B1 * m_ref[...] + (1.0 - ADAM_B1) * g
    nv = ADAM_B2 * v_ref[...] + (1.0 - ADAM_B2) * jnp.square(g)
    m_hat = nm / (1.0 - ADAM_B1 ** ADAM_STEP)
    v_hat = nv / (1.0 - ADAM_B2 ** ADAM_STEP)
    g_ref[...] = g
    d_ref[...] = -ADAM_LR * (m_hat / (jnp.sqrt(v_hat) + ADAM_EPS) + ADAM_WD * w_ref[...])
    nm_ref[...] = nm
    nv_ref[...] = nv


def _adamw(parts, w, m, v, name):
    rows, cols = w.shape
    t = _pick(rows, (512, 256, 128)) if rows > 512 else rows

    def body(p_ref, *refs):
        _adamw_update(p_ref[...], *refs)

    row = pl.BlockSpec((t, cols), lambda i: (i, 0))
    out = SDS((rows, cols), F32)
    return pl.pallas_call(
        body, name=name, out_shape=(out,) * 4, grid=(rows // t,),
        in_specs=[pl.BlockSpec((N_DEV, t, cols), lambda i: (0, i, 0)), row, row, row], out_specs=(row,) * 4,
        compiler_params=_params(),
    )(parts, w, m, v)


def _adamw_packed(sources, w, m, v, name):
    rows, cols = w.shape
    t = ADAMW_ROW_BLOCK
    nb = rows // DEPTH // t
    (src0, first0), (src1, first1) = sources
    assert first0 % t == 0 and first1 % t == 0 and rows % (DEPTH * t) == 0

    def body(p0_ref, p1_ref, *refs):
        layer = pl.program_id(0)
        _adamw_update(jnp.where(layer == 0, p0_ref[...], p1_ref[...]), *refs)

    spec0 = pl.BlockSpec((N_DEV, t, cols), lambda l, i: (0, first0 // t + i * (1 - l) + (nb - 1) * l, 0))
    spec1 = pl.BlockSpec((N_DEV, t, cols), lambda l, i: (0, first1 // t + i * l, 0))
    row = pl.BlockSpec((t, cols), lambda l, i: (l * nb + i, 0))
    out = SDS((rows, cols), F32)
    return pl.pallas_call(body, name=name, out_shape=(out,) * 4, grid=(DEPTH, nb), in_specs=[spec0, spec1, row, row, row],
                          out_specs=(row,) * 4, compiler_params=_params())(src0, src1, w, m, v)


_COMM_SEMAPHORES = [pltpu.SemaphoreType.DMA((N_DEV - 1,)), pltpu.SemaphoreType.DMA((N_DEV - 1,)), pltpu.SemaphoreType.DMA]


def _gather_steps(x_ref, out_ref, send_sems, recv_sems, local_sem):
    x_, y_, c_ = lax.axis_index("x"), lax.axis_index("y"), lax.axis_index("c")
    me, sibling = (x_, y_, c_), (x_, y_, 1 - c_)
    chips = [(1 - x_, y_), (x_, 1 - y_), (1 - x_, 1 - y_)]

    def slot(px, py, pc):
        return out_ref.at[4 * px + 2 * py + pc]

    def copy(kk, block, to, src=None):
        return pltpu.make_async_remote_copy(
            src_ref=slot(*block) if src is None else src, dst_ref=slot(*block),
            send_sem=send_sems.at[kk], recv_sem=recv_sems.at[kk], device_id=to, device_id_type=MESH)

    def mine():
        return pltpu.make_async_copy(x_ref, slot(*me), local_sem)

    def first():
        return [copy(0, me, sibling, src=x_ref)] + [copy(1 + j, me, (*chip, c_), src=x_ref) for j, chip in enumerate(chips)]

    def passed():
        return [copy(4 + j, (*chip, c_), sibling) for j, chip in enumerate(chips)]

    def start():
        mine().start()
        for cp in first():
            cp.start()

    def forward():
        for j, (chip, cp) in enumerate(zip(chips, passed())):
            copy(1 + j, (*chip, c_), me).wait_recv()
            cp.start()

    def finish():
        copy(0, sibling, me).wait_recv()
        for j, chip in enumerate(chips):
            copy(4 + j, (*chip, 1 - c_), me).wait_recv()
        for cp in first() + passed():
            cp.wait_send()
        mine().wait()

    return start, forward, finish


def _exchange_steps(x_ref, out_ref, send_sems, recv_sems, local_sem):
    x_, y_, c_ = lax.axis_index("x"), lax.axis_index("y"), lax.axis_index("c")
    me = 4 * x_ + 2 * y_ + c_

    def mine():
        return pltpu.make_async_copy(x_ref.at[me], out_ref.at[me], local_sem)

    def copies():
        out = []
        for r in range(1, N_DEV):
            px = 1 - x_ if r & 4 else x_
            py = 1 - y_ if r & 2 else y_
            pc = 1 - c_ if r & 1 else c_
            out.append(pltpu.make_async_remote_copy(
                src_ref=x_ref.at[4 * px + 2 * py + pc], dst_ref=out_ref.at[me],
                send_sem=send_sems.at[r - 1], recv_sem=recv_sems.at[r - 1], device_id=(px, py, pc), device_id_type=MESH))
        return out

    def start():
        mine().start()
        for cp in copies():
            cp.start()

    def finish():
        for cp in copies():
            cp.wait_recv()
        for cp in copies():
            cp.wait_send()
        mine().wait()

    return start, finish


def _hosted_call(body, exchange, *, name, out_shape, grid, in_specs, out_specs, args, scratch_shapes=()):
    if exchange is None:
        return pl.pallas_call(body, name=name, out_shape=out_shape, grid=grid, in_specs=in_specs, out_specs=out_specs,
                              scratch_shapes=list(scratch_shapes), compiler_params=_params())(*args)
    kind, payload, when = exchange
    n_in, n_out, n_scratch = len(in_specs), len(out_specs), len(scratch_shapes)
    result = SDS((N_DEV,) + payload.shape, payload.dtype) if kind == "gather" else SDS(payload.shape, payload.dtype)

    def hosting(*refs):
        ins, pay_ref = refs[:n_in], refs[n_in]
        outs, res_ref = refs[n_in + 1:n_in + 1 + n_out], refs[n_in + 1 + n_out]
        rest = refs[n_in + 2 + n_out:]
        scratch, sems = rest[:n_scratch], rest[n_scratch:]
        first, middle, last = when()
        if kind == "gather":
            start, forward, finish = _gather_steps(pay_ref, res_ref, *sems)
            pl.when(first)(start)
            pl.when(middle)(forward)
        else:
            start, finish = _exchange_steps(pay_ref, res_ref, *sems)
            pl.when(first)(start)
        body(*ins, *outs, *scratch)
        pl.when(last)(finish)

    hbm = pl.BlockSpec(memory_space=pl.ANY)
    return pl.pallas_call(
        hosting, name=name, out_shape=tuple(out_shape) + (result,), grid=grid, in_specs=list(in_specs) + [hbm],
        out_specs=tuple(out_specs) + (hbm,), scratch_shapes=list(scratch_shapes) + _COMM_SEMAPHORES,
        compiler_params=_params(has_side_effects=True),
    )(*args, payload)


def _exchange_alone(xs, make_steps, out_shapes, name):
    n = len(xs)

    def body(*refs):
        sems = refs[2 * n:]
        steps = [make_steps(refs[k], refs[n + k], *sems[3 * k:3 * k + 3]) for k in range(n)]
        for stage in zip(*steps):
            for step in stage:
                step()

    hbm = pl.BlockSpec(memory_space=pl.ANY)
    return pl.pallas_call(body, name=name, out_shape=tuple(out_shapes), in_specs=[hbm] * n, out_specs=(hbm,) * n,
                          scratch_shapes=_COMM_SEMAPHORES * n, compiler_params=pltpu.CompilerParams(has_side_effects=True))(*xs)


def _all_gather(xs, name):
    return _exchange_alone(xs, _gather_steps, [SDS((N_DEV,) + x.shape, x.dtype) for x in xs], name)


def _all_to_all(xs, name):
    return _exchange_alone(xs, _exchange_steps, [SDS(x.shape, x.dtype) for x in xs], name)


BIG = ("w_in", "cv_pw_w", "w_out", "x_wq", "x_wk", "x_wv", "x_wo", "ffn_w_up", "ffn_w_down")
_MIXER = (("w_in", 352), ("w_out", 128))
_CROSS = (("x_wq", 128), ("x_wk", 128), ("x_wv", 128), ("x_wo", 128))
GROUPS = {
    "a": (("w_in", 0, 352),),
    "o0": (("w_out", 0, 128),),
    "m1": tuple((n, 1, r) for n, r in _MIXER),
    "bx": tuple((n, 0, r) for n, r in _CROSS),
    "cx": tuple((n, 1, r) for n, r in _CROSS),
    "bf": (("ffn_w_down", 0, 352),),
    "cf": (("ffn_w_down", 1, 352),),
}
GRADIENT_GROUPS = ("a", "o0", "m1", "bf", "bx", "cf", "cx")
PW_GROUP = "o0"
TRANSPOSED = ("w_in", "ffn_w_up")
PW_ROWS = 16
ADAMW_ROW_BLOCK = 32


def _group_rows(group):
    out, first = {}, 0
    for n, l, r in GROUPS[group]:
        out[(n, l)] = (first, r)
        first += r
    return out


def _where_is(name, layer):
    for group in GRADIENT_GROUPS:
        rows = _group_rows(group)
        if (name, layer) in rows:
            return (group,) + rows[(name, layer)]
    raise KeyError((name, layer))


def _pack_weights(group, wts):
    pieces = []
    for n, l, _ in GROUPS[group]:
        w = wts[n][l].astype(BF16)
        pieces.append(w.T if n in TRANSPOSED else w)
    if group == PW_GROUP:
        pieces.append(wts["cv_pw_w"].astype(BF16).reshape(PW_ROWS, PAYLOAD_COLS))
    return jnp.concatenate(pieces, axis=0)


def _unpack_weights(group, gathered):
    return {key: gathered[:, first:first + r, :].reshape(N_DEV * r, PAYLOAD_COLS)
            for key, (first, r) in _group_rows(group).items()}


def _pack_grads(group, grads):
    pieces = []
    for n, l, r in GROUPS[group]:
        g = grads[n][l]
        parts = g if isinstance(g, tuple) else (g,)
        pieces.append(jnp.concatenate([p.reshape(-1, r, PAYLOAD_COLS) for p in parts], axis=0))
    if group == PW_GROUP:
        pieces.append(_to_shards("cv_pw_w", jnp.stack(grads["cv_pw_w"])).reshape(N_DEV, PW_ROWS, PAYLOAD_COLS))
    return jnp.concatenate(pieces, axis=1)


COL_SHARDED = ("w_in", "ffn_w_up", "cv_w", "ffn_conv_w")
SMALL_SHARDED = ("cv_w", "ffn_conv_w")
REPLICATED = ("mix_norm_pre", "cv_b", "cv_ln_g", "cv_ln_b", "cv_pw_b", "mix_norm_post", "x_norm_pre", "mem_norm",
              "x_norm_post", "ffn_norm_pre", "ffn_conv_b", "ffn_norm_post")
WEIGHTS = ("mix_norm_pre", "w_in", "cv_w", "cv_b", "cv_ln_g", "cv_ln_b", "cv_pw_w", "cv_pw_b", "w_out", "mix_norm_post",
           "x_norm_pre", "mem_norm", "x_wq", "x_wk", "x_wv", "x_wo", "x_norm_post", "ffn_norm_pre", "ffn_w_up",
           "ffn_conv_w", "ffn_conv_b", "ffn_w_down", "ffn_norm_post")
PAYLOAD_COLS = 1024


PAYLOAD_ROW_TILE = 16


def _pad_rows(flat, cols):
    n = flat.shape[-1]
    rows = -(-n // (cols * PAYLOAD_ROW_TILE)) * PAYLOAD_ROW_TILE
    pad = rows * cols - n
    if pad:
        flat = jnp.concatenate([flat, jnp.zeros(flat.shape[:-1] + (pad,), flat.dtype)], axis=-1)
    return flat.reshape(flat.shape[:-1] + (rows, cols))


def _unshard(name, parts):
    n, depth, r, c = parts.shape
    if name in COL_SHARDED:
        return parts.transpose(1, 2, 0, 3).reshape(depth, r, n * c)
    return parts.transpose(1, 0, 2, 3).reshape(depth, n * r, c)


def _to_shards(name, full):
    depth, r, c = full.shape
    if name in COL_SHARDED:
        return full.reshape(depth, r, N_DEV, c // N_DEV).transpose(2, 0, 1, 3).reshape(N_DEV, -1)
    return full.reshape(depth, N_DEV, r // N_DEV, c).transpose(1, 0, 2, 3).reshape(N_DEV, -1)


def _heads_major(x, h):
    return x.reshape(x.shape[0], h, HEAD_DIM).transpose(1, 0, 2)


def _tokens_major(x):
    return x.transpose(1, 0, 2).reshape(x.shape[1], -1)


def _ffn_halves(p):
    w, b = p["ffn_conv_w"], p["ffn_conv_b"]
    return w[:, :D_FF], w[:, D_FF:], b[:, :D_FF], b[:, D_FF:]


def _layer_fwd(l, h, hn, p, mem, cos, sin, g_next, payload, unpack, ffn_shards, ffn_payload, rope_payload=None,
               rope_unpack=None):
    p = dict(p)
    sv = {"h0": h, "hn0": hn}
    u = _mm(hn, p["w_in"], "nt", F32, f"l{l}_in_proj")
    sv["u"] = u
    sb = _heads_major(u[:, :3 * SB_WIDTH].astype(BF16), 3 * SB_HEADS)
    sb_q, sb_k, sb_v = sb[:SB_HEADS], sb[SB_HEADS:2 * SB_HEADS], sb[2 * SB_HEADS:]
    a_out, sb_tot, sb_first, gathered = _sb_fwd(sb_q, sb_k, sb_v, payload, f"l{l}_sb_fwd")
    p.update(unpack(gathered))
    sv.update(sb_q=sb_q, sb_k=sb_k, sb_v=sb_v, sb_tot=sb_tot, sb_first=sb_first, p=p)

    qk, *rope_gathered = _rope_fwd(u, cos, sin, f"l{l}_rope_fwd", rope_payload)
    if rope_gathered:
        p.update(rope_unpack(rope_gathered[0]))
        sv["rope_gathered"] = rope_gathered[0]
    cv_s, cv_c = _cv_fwd(u, p["cv_w"], p["cv_b"], p["cv_ln_g"], p["cv_ln_b"], f"l{l}_cv_fwd")
    b_out = _mm(cv_s, p["cv_pw_w"], "nn", BF16, f"l{l}_cv_pw", bias=p["cv_pw_b"])
    sv.update(cv_s=cv_s, cv_c=cv_c)

    up_t = ffn_shards[0].astype(BF16).T
    half_rows = up_t.shape[0] // 2
    carried = (up_t[:half_rows], up_t[half_rows:], ffn_shards[1].astype(BF16))
    outs, lses, got = [], [], []
    for b, (_, dil) in enumerate(DL_PATTERN):
        o, lse, gathered = _dl_fwd(qk, u, dil, f"l{l}_dl{b}_fwd", carried[b])
        outs.append(o)
        lses.append(lse)
        got.append(gathered)
    up_blocks = jnp.concatenate(got[:2], axis=1)
    half = N_DEV // 2
    p["ffn_w_up"] = (up_blocks[:half].reshape(-1, PAYLOAD_COLS), up_blocks[half:].reshape(-1, PAYLOAD_COLS))
    p["ffn_w_down"] = got[2].reshape(-1, PAYLOAD_COLS)
    c_out, c_out_f32, w1, w2, w3 = _dl_mix_fwd(outs, lses, f"l{l}_dl_mix")
    sv.update(dl_qk=qk, dl_lse=lses, dl_o=c_out_f32, dl_w=(w1, w2, w3))

    mix = jnp.concatenate([_tokens_major(a_out), b_out, c_out], axis=-1)
    y = _mm(mix, p["w_out"], "nn", F32, f"l{l}_out_proj")
    h1, hn1 = _res_norm_fwd(h, y, p["mix_norm_post"], p["x_norm_pre"], f"l{l}_mix_post")
    sv.update(mix=mix, y_mix=y, h1=h1, hn1=hn1)

    xq = _mm(hn1, p["x_wq"], "nn", BF16, f"l{l}_xq")
    memn = _rms_fwd(mem, p["mem_norm"], f"l{l}_mem_norm")
    xk = _mm(memn, p["x_wk"], "nn", BF16, f"l{l}_xk")
    xv = _mm(memn, p["x_wv"], "nn", BF16, f"l{l}_xv")
    xo = _xattn_fwd(xq, xk, xv, f"l{l}_xattn_fwd")
    y = _mm(xo, p["x_wo"], "nn", F32, f"l{l}_xo_proj")
    h2, hn2 = _res_norm_fwd(h1, y, p["x_norm_post"], p["ffn_norm_pre"], f"l{l}_x_post")
    sv.update(xq=xq, xk=xk, xv=xv, xo=xo, memn=memn, y_x=y, h2=h2, hn2=hn2)

    up_g = _mm(hn2, p["ffn_w_up"][0], "nt", F32, f"l{l}_ffn_up_gate")
    up_v = _mm(hn2, p["ffn_w_up"][1], "nt", F32, f"l{l}_ffn_up_val")
    act, *ffn_gathered = _ffn_act_fwd(up_g, up_v, *_ffn_halves(p), f"l{l}_ffn_act", ffn_payload)
    y = _mm(act, p["ffn_w_down"], "nn", F32, f"l{l}_ffn_down")
    h3, hn3 = _res_norm_fwd(h2, y, p["ffn_norm_post"], g_next, f"l{l}_ffn_post")
    sv.update(up_g=up_g, up_v=up_v, act=act, y_ffn=y)
    return h3, hn3, sv, (ffn_gathered[0] if ffn_gathered else None)


def _layer_bwd(l, dh, dy, sv, mem, cos, sin, prev_post, ffn_payload, pack):
    p = sv["p"]
    gr = {}
    received = {}
    dact = _mm(dy, p["ffn_w_down"], "nt", F32, f"l{l}_d_act")
    gr["ffn_w_down"] = _mm(sv["act"], dy, "tn", BF16, f"l{l}_dw_down")
    dup_g, dup_v, dwg, dwv, dbg, dbv, *got = _ffn_act_bwd(sv["up_g"], sv["up_v"], dact, *_ffn_halves(p), f"l{l}_ffn_act_bwd",
                                                         ffn_payload)
    if got:
        received["ffn_payload"] = got[0]
    gr["ffn_conv_w"] = jnp.concatenate([dwg, dwv], axis=1)
    gr["ffn_conv_b"] = jnp.concatenate([dbg, dbv], axis=1)
    dhn = (_mm(dup_g, p["ffn_w_up"][0], "nn", F32, f"l{l}_d_hn2_gate"), _mm(dup_v, p["ffn_w_up"][1], "nn", F32, f"l{l}_d_hn2_val"))
    gr["ffn_w_up"] = (_mm(dup_g, sv["hn2"], "tn", BF16, f"l{l}_dw_up_gate"), _mm(dup_v, sv["hn2"], "tn", BF16, f"l{l}_dw_up_val"))
    dh, dy, gr["ffn_norm_pre"], gr["x_norm_post"] = _norm_bwd(
        dh, (sv["h2"], p["ffn_norm_pre"], dhn), (sv["y_x"], p["x_norm_post"]), f"l{l}_x_post_bwd")

    do = _mm(dy, p["x_wo"], "nt", BF16, f"l{l}_d_xo")
    gr["x_wo"] = _mm(sv["xo"], dy, "tn", BF16, f"l{l}_dw_xo")
    dq, dk, dv = _xattn_bwd(sv["xq"], sv["xk"], sv["xv"], do, f"l{l}_xattn_bwd")
    dhn = _mm(dq, p["x_wq"], "nt", F32, f"l{l}_d_hn1")
    gr["x_wq"] = _mm(sv["hn1"], dq, "tn", BF16, f"l{l}_dw_xq")
    gr["x_wk"] = _mm(sv["memn"], dk, "tn", BF16, f"l{l}_dw_xk")
    gr["x_wv"] = _mm(sv["memn"], dv, "tn", BF16, f"l{l}_dw_xv")
    dmemn = _mm(dk, p["x_wk"], "nt", F32, f"l{l}_d_memn_k") + _mm(dv, p["x_wv"], "nt", F32, f"l{l}_d_memn_v")
    gr["mem_norm"] = _rms_gain_grad(mem, p["mem_norm"], dmemn, f"l{l}_mem_norm_bwd")
    dh, dy, gr["x_norm_pre"], gr["mix_norm_post"] = _norm_bwd(
        dh, (sv["h1"], p["x_norm_pre"], dhn), (sv["y_mix"], p["mix_norm_post"]), f"l{l}_mix_post_bwd")

    dmix = _mm(dy, p["w_out"], "nt", F32, f"l{l}_d_mix")
    gr["w_out"] = _mm(sv["mix"], dy, "tn", BF16, f"l{l}_dw_out")
    do_a = _heads_major(dmix[:, :SB_WIDTH].astype(BF16), SB_HEADS)
    dq, dk, dv, received["ffn"] = _sb_bwd(sv["sb_q"], sv["sb_k"], sv["sb_v"], do_a, sv["sb_tot"], sv["sb_first"],
                                          pack("ffn", gr), f"l{l}_sb_bwd")
    du_sb = _tokens_major(jnp.concatenate([dq, dk, dv], axis=0))

    db_out = dmix[:, SB_WIDTH:SB_WIDTH + CV_WIDTH]
    ds = _mm(db_out, p["cv_pw_w"], "nt", F32, f"l{l}_d_cv_s")
    gr["cv_pw_w"] = _mm(sv["cv_s"], db_out, "tn", BF16, f"l{l}_dw_cv_pw")
    du_cv, dcvw, gr["cv_b"], gr["cv_ln_g"], gr["cv_ln_b"], gr["cv_pw_b"], *got = _cv_bwd(
        sv["u"], sv["cv_c"], ds, db_out, p["cv_w"], p["cv_ln_g"], p["cv_ln_b"], f"l{l}_cv_bwd", pack("out", gr))
    if got:
        received["out"] = got[0]
    gr["cv_w"] = dcvw[:CV_KERNEL]

    dqs, dks, dvs = [], [], []
    up_rows = jnp.concatenate([g.reshape(N_DEV // 2, -1, PAYLOAD_COLS) for g in gr["ffn_w_up"]], axis=0)
    half_rows = up_rows.shape[1] // 2
    carried = {"up0": up_rows[:, :half_rows], "up1": up_rows[:, half_rows:], "cross": pack("cross", gr)}
    for b, ((_, dil), what) in enumerate(zip(DL_PATTERN, carried)):
        dq, dk, dv, received[what] = _dl_bwd(sv["dl_qk"], sv["u"], dmix, sv["dl_o"], sv["dl_w"][b], sv["dl_lse"][b], dil,
                                             f"l{l}_dl{b}_bwd", carried[what])
        dqs.append(dq)
        dks.append(dk)
        dvs.append(dv)
    du_dl = _rope_bwd(dqs, dks, dvs, cos, sin, f"l{l}_rope_bwd")

    du = jnp.concatenate([du_sb, du_cv, du_dl], axis=-1)
    dhn = _mm(du, p["w_in"], "nn", F32, f"l{l}_d_hn0")
    gr["w_in"] = _mm(du, sv["hn0"], "tn", BF16, f"l{l}_dw_in")
    dh, dy, gr["mix_norm_pre"], dg_prev = _norm_bwd(dh, (sv["h0"], p["mix_norm_pre"], dhn), prev_post, f"l{l}_in_bwd")
    return dh, dy, gr, dg_prev, received


def kernel(x, mem, positions, mix_norm_pre, w_in, cv_w, cv_b, cv_ln_g, cv_ln_b, cv_pw_w, cv_pw_b, w_out, mix_norm_post, x_norm_pre, mem_norm, x_wq, x_wk, x_wv, x_wo, x_norm_post, ffn_norm_pre, ffn_w_up, ffn_conv_w, ffn_conv_b, ffn_w_down, ffn_norm_post, loss_target, m_mix_norm_pre, m_w_in, m_cv_w, m_cv_b, m_cv_ln_g, m_cv_ln_b, m_cv_pw_w, m_cv_pw_b, m_w_out, m_mix_norm_post, m_x_norm_pre, m_mem_norm, m_x_wq, m_x_wk, m_x_wv, m_x_wo, m_x_norm_post, m_ffn_norm_pre, m_ffn_w_up, m_ffn_conv_w, m_ffn_conv_b, m_ffn_w_down, m_ffn_norm_post, v_mix_norm_pre, v_w_in, v_cv_w, v_cv_b, v_cv_ln_g, v_cv_ln_b, v_cv_pw_w, v_cv_pw_b, v_w_out, v_mix_norm_post, v_x_norm_pre, v_mem_norm, v_x_wq, v_x_wk, v_x_wv, v_x_wo, v_x_norm_post, v_ffn_norm_pre, v_ffn_w_up, v_ffn_conv_w, v_ffn_conv_b, v_ffn_w_down, v_ffn_norm_post):
    args = locals()
    wts = {n: args[n] for n in WEIGHTS}
    mom = {n: args["m_" + n] for n in WEIGHTS}
    var = {n: args["v_" + n] for n in WEIGHTS}

    x2, mem2, target = x[0], mem[0], loss_target[0]

    small_payload = _pad_rows(jnp.concatenate([wts[n].reshape(-1) for n in SMALL_SHARDED]), PAYLOAD_COLS)
    gathered_a, small = _all_gather([_pack_weights("a", wts), small_payload], "weights_all_gather")
    small = small.reshape(N_DEV, -1)
    small_full = {}
    off = 0
    for n in SMALL_SHARDED:
        size = wts[n].size
        small_full[n] = _unshard(n, small[:, off:off + size].reshape((N_DEV,) + wts[n].shape))
        off += size
    pw_first = sum(r for _, _, r in GROUPS[PW_GROUP])

    def pw_of(gathered_pw_group):
        return _unshard("cv_pw_w", gathered_pw_group[:, pw_first:, :].reshape((N_DEV,) + wts["cv_pw_w"].shape))

    def small_params(l):
        p = {n: wts[n][l][None, :] for n in REPLICATED}
        p.update({n: small_full[n][l] for n in SMALL_SHARDED})
        return p

    def out_and_pw(gathered_pw_group):
        return {"w_out": _unpack_weights(PW_GROUP, gathered_pw_group)[("w_out", 0)], "cv_pw_w": pw_of(gathered_pw_group)[0]}

    def of_layer(group, l):
        return lambda gathered: {n: w for (n, ll), w in _unpack_weights(group, gathered).items() if ll == l}

    pos = positions[0].astype(F32)
    half = HEAD_DIM // 2
    inv_freq = ROPE_THETA ** (-jnp.arange(half, dtype=F32) / half)
    ang = pos[:, None] * inv_freq
    cos = jnp.tile(jnp.cos(ang), (1, LANES // half))
    sin = jnp.tile(jnp.sin(ang), (1, LANES // half))

    p0 = small_params(0)
    p0["w_in"] = _unpack_weights("a", gathered_a)[("w_in", 0)]
    hn = _rms_fwd(x2, p0["mix_norm_pre"], "l0_in_norm")
    ffn_shards = [(wts["ffn_w_up"][l], wts["ffn_w_down"][l]) for l in range(DEPTH)]
    h, hn, sv0, gathered_m1 = _layer_fwd(0, x2, hn, p0, mem2, cos, sin, wts["mix_norm_pre"][1][None, :],
                                         _pack_weights("bx", wts), of_layer("bx", 0), ffn_shards[0], _pack_weights("m1", wts),
                                         _pack_weights(PW_GROUP, wts), out_and_pw)
    p1 = small_params(1)
    mixer1 = _unpack_weights("m1", gathered_m1)
    p1.update(w_in=mixer1[("w_in", 1)], w_out=mixer1[("w_out", 1)], cv_pw_w=pw_of(sv0["rope_gathered"])[1])
    h, _, sv1, _ = _layer_fwd(1, h, hn, p1, mem2, cos, sin, None, _pack_weights("cx", wts), of_layer("cx", 1), ffn_shards[1],
                              None)
    loss_part, dh = _loss_fwd(h, target, "loss")
    loss = lax.psum(loss_part[0, 0], ("x", "y", "c"))

    grads = {n: [None] * DEPTH for n in WEIGHTS}
    dh, dy, _, grads["ffn_norm_post"][1] = _norm_bwd(dh, None, (sv1["y_ffn"], sv1["p"]["ffn_norm_post"]), "last_post_bwd")

    def packer(l, groups):
        def pack(which, gr):
            if which not in groups:
                return None
            if groups[which] == PW_GROUP:
                return _pack_grads(PW_GROUP, {"w_out": {0: gr["w_out"]}, "cv_pw_w": [gr["cv_pw_w"], grads["cv_pw_w"][1]]})
            return _pack_grads(groups[which], {n: {l: g} for n, g in gr.items()})
        return pack

    dh, dy, gr, grads["ffn_norm_post"][0], got1 = _layer_bwd(
        1, dh, dy, sv1, mem2, cos, sin, (sv0["y_ffn"], sv0["p"]["ffn_norm_post"]), None, packer(1, {"ffn": "cf", "cross": "cx"}))
    for n, g in gr.items():
        grads[n][1] = g
    dh, _, gr, _, got0 = _layer_bwd(0, dh, dy, sv0, mem2, cos, sin, None, _pack_grads("m1", grads),
                                    packer(0, {"ffn": "bf", "cross": "bx", "out": PW_GROUP}))
    for n, g in gr.items():
        grads[n][0] = g
    grad_x = dh

    small_rows = jnp.concatenate([_to_shards(n, jnp.stack(grads[n])) for n in SMALL_SHARDED], axis=1)
    rep_flat = jnp.concatenate([jnp.stack([g.reshape(-1) for g in grads[n]]).reshape(-1) for n in REPLICATED])
    rep_rows = jnp.broadcast_to(rep_flat[None], (N_DEV, rep_flat.shape[0]))
    f32_rows = _pad_rows(jnp.concatenate([small_rows, rep_rows], axis=1), PAYLOAD_COLS)
    received_a, small_parts = _all_to_all([_pack_grads("a", grads), f32_rows], "grads_all_to_all")
    received = {"a": received_a, PW_GROUP: got0["out"], "m1": got0["ffn_payload"], "bf": got0["ffn"], "bx": got0["cross"],
                "cf": got1["ffn"], "cx": got1["cross"]}

    res = {}
    for n in BIG:
        shape = wts[n].shape
        two_d = (shape[0] * shape[1], shape[2])
        operands = (wts[n].reshape(two_d), mom[n].reshape(two_d), var[n].reshape(two_d))
        if n == "cv_pw_w":
            outs = _adamw(received[PW_GROUP][:, pw_first:, :].reshape((N_DEV,) + two_d), *operands, f"adamw_{n}")
        elif n in TRANSPOSED:
            layers = []
            for l, got in enumerate((got0, got1)):
                if n == "ffn_w_up":
                    layers.append(jnp.concatenate([got["up0"], got["up1"]], axis=1))
                else:
                    group, first, r = _where_is(n, l)
                    layers.append(received[group][:, first:first + r, :])
            parts = jnp.stack(layers, axis=1).transpose(0, 1, 3, 2).reshape((N_DEV,) + two_d)
            outs = _adamw(parts, *operands, f"adamw_{n}")
        else:
            sources = []
            for l in range(DEPTH):
                group, first, _ = _where_is(n, l)
                sources.append((received[group], first))
            outs = _adamw_packed(sources, *operands, f"adamw_{n}")
        res[n] = [o.reshape(shape) for o in outs]
    small_names = SMALL_SHARDED + REPLICATED
    flat_w = _pad_rows(jnp.concatenate([wts[n].reshape(-1) for n in small_names]), PAYLOAD_COLS)
    flat_m = _pad_rows(jnp.concatenate([mom[n].reshape(-1) for n in small_names]), PAYLOAD_COLS)
    flat_v = _pad_rows(jnp.concatenate([var[n].reshape(-1) for n in small_names]), PAYLOAD_COLS)
    outs = _adamw(small_parts, flat_w, flat_m, flat_v, "adamw_small")
    outs = [o.reshape(-1) for o in outs]
    off = 0
    for n in small_names:
        size = wts[n].size
        res[n] = [o[off:off + size].reshape(wts[n].shape) for o in outs]
        off += size

    result = [loss, grad_x[None]]
    for kind in range(4):
        result += [res[n][kind] for n in WEIGHTS]
    return tuple(result)
```

```python
import functools
import math

import jax
import jax.numpy as jnp
from jax import lax
from jax.experimental import pallas as pl
from jax.experimental.pallas import tpu as pltpu

F32, BF16 = jnp.float32, jnp.bfloat16
SDS = jax.ShapeDtypeStruct

D_MODEL = 1024
SEQ = 4096
DEPTH = 2
HEAD_DIM = 64
SB_HEADS = 4
SB_WIDTH = 256
CV_WIDTH = 256
CV_KERNEL = 31
DL_HEADS = 8
DL_WIDTH = 512
IN_WIDTH = 2816
DL_PATTERN = ((128, 1), (512, 4), (2048, 16))
BLOCK = 128
ROPE_THETA = 10000.0
N_MEM = 256
X_HEADS = 4
X_HEAD_DIM = 256
D_FF = 2816
EPS = 1e-6
N_DEV = 8
LANES = 128

ADAM_LR = 0.001
ADAM_B1 = 0.9
ADAM_B2 = 0.999
ADAM_EPS = 1e-08
ADAM_WD = 0.01
ADAM_STEP = 10

VMEM_LIMIT_BYTES = 56 * 1024 * 1024
MESH = pl.DeviceIdType.MESH
NEG = -1e30


def _params(**kw):
    return pltpu.CompilerParams(vmem_limit_bytes=VMEM_LIMIT_BYTES, **kw)


def _pick(n, cands):
    for c in cands:
        if n % c == 0:
            return c
    return n


def _mm(a, b, mode, out_dtype, name, bias=None):
    if mode == "nn":
        (m, k), (k2, n) = a.shape, b.shape
    elif mode == "nt":
        (m, k), (n, k2) = a.shape, b.shape
    else:
        (k, m), (k2, n) = a.shape, b.shape
    assert k == k2, (a.shape, b.shape, mode)
    tm = _pick(m, (1024, 1408, 512, 256, 128))
    tn = _pick(n, (1024, 1408, 512, 256, 128))
    tk = k if k <= 2048 else _pick(k, (2048, 1408, 1024, 512))
    nk = k // tk
    dims = {"nn": ((1,), (0,)), "nt": ((1,), (1,)), "tn": ((0,), (0,))}[mode]

    def body(*refs):
        refs = list(refs)
        acc_ref = refs.pop() if nk > 1 else None
        a_ref, b_ref = refs[0], refs[1]
        bias_ref = refs[2] if bias is not None else None
        o_ref = refs[-1]
        p = lax.dot_general(a_ref[...].astype(BF16), b_ref[...].astype(BF16), (dims, ((), ())),
                            preferred_element_type=F32)

        def finish(v):
            if bias_ref is not None:
                v = v + bias_ref[...]
            o_ref[...] = v.astype(out_dtype)

        if nk == 1:
            finish(p)
        else:
            kk = pl.program_id(2)

            @pl.when(kk == 0)
            def _():
                acc_ref[...] = p

            @pl.when(kk > 0)
            def _():
                acc_ref[...] += p

            @pl.when(kk == nk - 1)
            def _():
                finish(acc_ref[...])

    a_spec = pl.BlockSpec((tk, tm), lambda i, j, kk: (kk, i)) if mode == "tn" else pl.BlockSpec((tm, tk), lambda i, j, kk: (i, kk))
    b_spec = pl.BlockSpec((tn, tk), lambda i, j, kk: (j, kk)) if mode == "nt" else pl.BlockSpec((tk, tn), lambda i, j, kk: (kk, j))
    in_specs = [a_spec, b_spec]
    args = [a, b]
    if bias is not None:
        in_specs.append(pl.BlockSpec((1, tn), lambda i, j, kk: (0, j)))
        args.append(bias)
    return pl.pallas_call(
        body, name=name, out_shape=SDS((m, n), out_dtype), grid=(m // tm, n // tn, nk),
        in_specs=in_specs, out_specs=pl.BlockSpec((tm, tn), lambda i, j, kk: (i, j)),
        scratch_shapes=[pltpu.VMEM((tm, tn), F32)] if nk > 1 else [], compiler_params=_params(),
    )(*args)


def _rms(x, g):
    r = lax.rsqrt(jnp.mean(x * x, axis=-1, keepdims=True) + EPS)
    return x * r * g


def _rms_bwd(x, g, dy):
    r = lax.rsqrt(jnp.mean(x * x, axis=-1, keepdims=True) + EPS)
    xh = x * r
    dyg = dy * g
    dx = r * (dyg - xh * jnp.mean(dyg * xh, axis=-1, keepdims=True))
    return dx, dy * xh


def _rms_fwd(x, g, name):
    rows, d = x.shape
    t = min(rows, 512)

    def body(x_ref, g_ref, o_ref):
        o_ref[...] = _rms(x_ref[...], g_ref[...]).astype(BF16)

    return pl.pallas_call(
        body, name=name, out_shape=SDS((rows, d), BF16), grid=(rows // t,),
        in_specs=[pl.BlockSpec((t, d), lambda i: (i, 0)), pl.BlockSpec((1, d), lambda i: (0, 0))],
        out_specs=pl.BlockSpec((t, d), lambda i: (i, 0)), compiler_params=_params(),
    )(x, g)


def _res_norm_fwd(h, y, g_post, g_next, name):
    rows, d = h.shape
    t = 512
    has_next = g_next is not None

    def body(*refs):
        if has_next:
            h_ref, y_ref, gp_ref, gn_ref, h1_ref, hn_ref = refs
        else:
            h_ref, y_ref, gp_ref, h1_ref = refs
        h1 = h_ref[...] + _rms(y_ref[...], gp_ref[...])
        h1_ref[...] = h1
        if has_next:
            hn_ref[...] = _rms(h1, gn_ref[...]).astype(BF16)

    row = pl.BlockSpec((t, d), lambda i: (i, 0))
    vec = pl.BlockSpec((1, d), lambda i: (0, 0))
    in_specs = [row, row, vec] + ([vec] if has_next else [])
    args = [h, y, g_post] + ([g_next] if has_next else [])
    out_shape = [SDS((rows, d), F32)] + ([SDS((rows, d), BF16)] if has_next else [])
    out_specs = [row] + ([row] if has_next else [])
    res = pl.pallas_call(body, name=name, out_shape=out_shape, grid=(rows // t,), in_specs=in_specs,
                         out_specs=out_specs, compiler_params=_params())(*args)
    return (res[0], res[1]) if has_next else (res[0], None)


def _norm_bwd(dh, pre, post, name):
    rows, d = dh.shape
    t = 512
    has_pre, has_post = pre is not None, post is not None
    if has_pre:
        dhns = pre[2] if isinstance(pre[2], tuple) else (pre[2],)
        pre = (pre[0], pre[1]) + dhns

    def body(*refs):
        refs = list(refs)
        dh_ref = refs.pop(0)
        if has_pre:
            h_ref, gpre_ref = refs.pop(0), refs.pop(0)
            dhn_refs = [refs.pop(0) for _ in dhns]
        if has_post:
            y_ref, gpost_ref = refs.pop(0), refs.pop(0)
        dht_ref = refs.pop(0)
        if has_post:
            dy_ref = refs.pop(0)
        if has_pre:
            dgpre_ref = refs.pop(0)
        if has_post:
            dgpost_ref = refs.pop(0)
        i = pl.program_id(0)
        dht = dh_ref[...]
        if has_pre:
            dhn = dhn_refs[0][...]
            for r in dhn_refs[1:]:
                dhn = dhn + r[...]
            dx, dgr = _rms_bwd(h_ref[...], gpre_ref[...], dhn)
            dht = dht + dx

            @pl.when(i == 0)
            def _():
                dgpre_ref[...] = jnp.zeros_like(dgpre_ref)

            dgpre_ref[...] += jnp.sum(dgr, axis=0, keepdims=True)
        dht_ref[...] = dht
        if has_post:
            dy, dgr = _rms_bwd(y_ref[...], gpost_ref[...], dht)
            dy_ref[...] = dy.astype(BF16)

            @pl.when(i == 0)
            def _():
                dgpost_ref[...] = jnp.zeros_like(dgpost_ref)

            dgpost_ref[...] += jnp.sum(dgr, axis=0, keepdims=True)

    row = pl.BlockSpec((t, d), lambda i: (i, 0))
    vec = pl.BlockSpec((1, d), lambda i: (0, 0))
    in_specs, args = [row], [dh]
    if has_pre:
        in_specs += [row, vec] + [row] * len(dhns)
        args += list(pre)
    if has_post:
        in_specs += [row, vec]
        args += list(post)
    out_shape, out_specs = [SDS((rows, d), F32)], [row]
    if has_post:
        out_shape.append(SDS((rows, d), BF16))
        out_specs.append(row)
    if has_pre:
        out_shape.append(SDS((1, d), F32))
        out_specs.append(vec)
    if has_post:
        out_shape.append(SDS((1, d), F32))
        out_specs.append(vec)
    res = list(pl.pallas_call(body, name=name, out_shape=out_shape, grid=(rows // t,), in_specs=in_specs,
                              out_specs=out_specs, compiler_params=_params())(*args))
    dht = res.pop(0)
    dy = res.pop(0) if has_post else None
    dgpre = res.pop(0) if has_pre else None
    dgpost = res.pop(0) if has_post else None
    return dht, dy, dgpre, dgpost


def _rms_gain_grad(x, g, dy, name):
    rows, d = x.shape

    def body(x_ref, g_ref, dy_ref, dg_ref):
        _, dgr = _rms_bwd(x_ref[...], g_ref[...], dy_ref[...])
        dg_ref[...] = jnp.sum(dgr, axis=0, keepdims=True)

    return pl.pallas_call(body, name=name, out_shape=SDS((1, d), F32), compiler_params=_params())(x, g, dy)


def _loss_fwd(h, target, name):
    rows, d = h.shape
    t = 512

    def body(h_ref, t_ref, loss_ref, dh_ref):
        i = pl.program_id(0)
        err = h_ref[...] - t_ref[...]
        dh_ref[...] = err * (1.0 / d)

        @pl.when(i == 0)
        def _():
            loss_ref[...] = jnp.zeros_like(loss_ref)

        part = jnp.sum(jnp.sum(err * err, axis=1, keepdims=True), axis=0, keepdims=True) * (0.5 / d)
        loss_ref[...] += jnp.broadcast_to(part, loss_ref.shape)

    row = pl.BlockSpec((t, d), lambda i: (i, 0))
    return pl.pallas_call(
        body, name=name, out_shape=(SDS((1, LANES), F32), SDS((rows, d), F32)), grid=(rows // t,),
        in_specs=[row, row], out_specs=(pl.BlockSpec((1, LANES), lambda i: (0, 0)), row), compiler_params=_params(),
    )(h, target)


def _rot_half(x, sign):
    w = x.shape[-1]
    lane = lax.broadcasted_iota(jnp.int32, x.shape, 1)
    first = (lane % HEAD_DIM) < (HEAD_DIM // 2)
    return jnp.where(first, -sign * pltpu.roll(x, w - HEAD_DIM // 2, axis=1), sign * pltpu.roll(x, HEAD_DIM // 2, axis=1))


def _rope_fwd(u, cos, sin, name, payload=None):
    rows = u.shape[0]
    t, cw = 512, 256
    first_col = (3 * SB_WIDTH + 2 * CV_WIDTH) // cw

    def body(u_ref, c_ref, s_ref, o_ref):
        x = u_ref[...]
        c = jnp.tile(c_ref[...], (1, cw // LANES))
        s = jnp.tile(s_ref[...], (1, cw // LANES))
        o_ref[...] = x * c + _rot_half(x, 1.0) * s

    tab = pl.BlockSpec((t, LANES), lambda i, j: (i, 0))
    grid = (rows // t, 2 * DL_WIDTH // cw)
    if payload is not None:
        def when():
            first = (pl.program_id(0) == 0) & (pl.program_id(1) == 0)
            last = (pl.program_id(0) == grid[0] - 1) & (pl.program_id(1) == grid[1] - 1)
            return first, last, last
        payload = ("gather", payload, when)
    return _hosted_call(
        body, payload, name=name, out_shape=(SDS((rows, 2 * DL_WIDTH), F32),), grid=grid,
        in_specs=[pl.BlockSpec((t, cw), lambda i, j: (i, first_col + j)), tab, tab],
        out_specs=(pl.BlockSpec((t, cw), lambda i, j: (i, j)),), args=(u, cos, sin))


def _rope_bwd(dqs, dks, dvs, cos, sin, name):
    rows = dqs[0].shape[0]
    t, w = 256, DL_WIDTH

    def body(*refs):
        c = jnp.tile(refs[9][...], (1, w // LANES))
        s = jnp.tile(refs[10][...], (1, w // LANES))
        o_ref = refs[11]
        dq = refs[0][...] + refs[1][...] + refs[2][...]
        dk = refs[3][...] + refs[4][...] + refs[5][...]
        dv = refs[6][...] + refs[7][...] + refs[8][...]
        o_ref[:, 0:w] = (dq * c + _rot_half(dq, -1.0) * s).astype(BF16)
        o_ref[:, w:2 * w] = (dk * c + _rot_half(dk, -1.0) * s).astype(BF16)
        o_ref[:, 2 * w:3 * w] = dv.astype(BF16)

    row = pl.BlockSpec((t, w), lambda i: (i, 0))
    tab = pl.BlockSpec((t, LANES), lambda i: (i, 0))
    return pl.pallas_call(
        body, name=name, out_shape=SDS((rows, 3 * w), BF16), grid=(rows // t,), in_specs=[row] * 9 + [tab, tab],
        out_specs=pl.BlockSpec((t, 3 * w), lambda i: (i, 0)), compiler_params=_params(),
    )(*dqs, *dks, *dvs, cos, sin)


SB_TILE = 256
SB_ZERO_AFTER = 110.0
SB_FIRST_BLOCK = (8, LANES)


def _softplus(z):
    return jnp.maximum(z, 0.0) + jnp.log(1.0 + jnp.exp(-jnp.abs(z)))


def _split_dot(x, tri, passes):
    acc = None
    rem = x
    for _ in range(passes):
        part = rem.astype(BF16)
        rem = rem - part.astype(F32)
        d = jnp.dot(part, tri, preferred_element_type=F32)
        acc = d if acc is None else acc + d
    return acc


def _tri(t, rel):
    j = lax.broadcasted_iota(jnp.int32, (t, t), 0)
    s = lax.broadcasted_iota(jnp.int32, (t, t), 1)
    return rel(j, s).astype(BF16)


def _sb_masks(t, i):
    row = lax.broadcasted_iota(jnp.int32, (t, t), 0)
    col = lax.broadcasted_iota(jnp.int32, (t, t), 1)
    return col < row, (row >= 0) & (i >= 1)


def _sb_fwd(q, k, v, payload, name):
    h, s_len, hd = q.shape
    t = SB_TILE
    nq = s_len // t
    scale = hd ** -0.5

    def body(q_ref, k_ref, v_ref, pay_ref, o_ref, tot_ref, first_ref, gathered_ref, send_sems, recv_sems, local_sem):
        hh, i = pl.program_id(0), pl.program_id(1)
        start, forward, finish = _gather_steps(pay_ref, gathered_ref, send_sems, recv_sems, local_sem)
        pl.when((hh == 0) & (i == 0))(start)
        pl.when((hh == h - 1) & (i == nq - 1))(forward)
        qv = q_ref[0] * scale
        upper = _tri(t, lambda j, s: j > s)

        def tiles(js, carry, masks=(None, None)):
            acc, run = carry
            starts = [pl.multiple_of(j * t, t) for j in js]
            zs = [lax.dot_general(qv, k_ref[0, pl.ds(st, t), :], (((1,), (1,)), ((), ())), preferred_element_type=F32)
                  for st in starts]
            sps = [_softplus(z) for z in zs]
            sps = [sp if m is None else jnp.where(m, sp, 0.0) for sp, m in zip(sps, masks)]
            laters = [_split_dot(sp, upper, 2) for sp in sps]
            for st, z, sp, later, m in zip(starts, zs, sps, laters, masks):
                a = jnp.exp((z - sp) - (run + later))
                if m is not None:
                    a = jnp.where(m, a, 0.0)
                acc = acc + jnp.dot(a.astype(BF16), v_ref[0, pl.ds(st, t), :], preferred_element_type=F32)
                run = run + jnp.sum(sp, axis=1, keepdims=True)
            return acc, run

        def live(carry):
            return jnp.min(carry[1]) < SB_ZERO_AFTER

        below, whole = _sb_masks(t, i)
        top = jnp.maximum(i - 1, 0)
        carry = tiles([i, top], (jnp.zeros((t, hd), F32), jnp.zeros((t, 1), F32)), (below, whole))

        def pair(state):
            pp, carry = state
            j = top - 1 - 2 * pp
            return pp + 1, tiles([j, j - 1], carry)

        pairs, carry = lax.while_loop(lambda st: (st[0] < top // 2) & live(st[1]), pair, (0, carry))
        last = ((top % 2 == 1) & (pairs == top // 2) & live(carry)).astype(jnp.int32)
        acc, run = lax.fori_loop(0, last, lambda _, c: tiles([0], c), carry)
        o_ref[0] = acc.astype(BF16)
        tot_ref[0] = run
        first_ref[...] = jnp.full(first_ref.shape, top - 2 * pairs - last, jnp.int32).astype(F32)
        pl.when((hh == h - 1) & (i == nq - 1))(finish)

    full = pl.BlockSpec((1, s_len, hd), lambda hh, i: (hh, 0, 0))
    tile = pl.BlockSpec((1, t, hd), lambda hh, i: (hh, i, 0))
    hbm = pl.BlockSpec(memory_space=pl.ANY)
    return pl.pallas_call(
        body, name=name,
        out_shape=(SDS((h, s_len, hd), BF16), SDS((h, s_len, 1), F32), SDS((h, nq) + SB_FIRST_BLOCK, F32),
                   SDS((N_DEV,) + payload.shape, payload.dtype)),
        grid=(h, nq), in_specs=[tile, full, full, hbm],
        out_specs=(tile, pl.BlockSpec((1, t, 1), lambda hh, i: (hh, i, 0)),
                   pl.BlockSpec((1, 1) + SB_FIRST_BLOCK, lambda hh, i: (hh, i, 0, 0)), hbm),
        scratch_shapes=_COMM_SEMAPHORES, compiler_params=_params(has_side_effects=True),
    )(q, k, v, payload)


def _sb_bwd(q, k, v, do, tot, first, payload, name):
    h, s_len, hd = q.shape
    t = SB_TILE
    nq = s_len // t
    scale = hd ** -0.5

    def body(q_ref, k_ref, v_ref, do_ref, tot_ref, first_ref, pay_ref, dq_ref, dk_ref, dv_ref, received_ref, dk_acc, dv_acc,
             send_sems, recv_sems, local_sem):
        hh, i = pl.program_id(0), pl.program_id(1)
        start, finish = _exchange_steps(pay_ref, received_ref, send_sems, recv_sems, local_sem)
        pl.when((hh == 0) & (i == 0))(start)

        @pl.when(i == 0)
        def _():
            dk_acc[...] = jnp.zeros_like(dk_acc)
            dv_acc[...] = jnp.zeros_like(dv_acc)

        qv = q_ref[0] * scale
        dov = do_ref[0]
        total = tot_ref[0]
        upto = _tri(t, lambda j, s: j <= s)
        before = _tri(t, lambda j, s: j < s)
        nt_dims = (((1,), (1,)), ((), ()))
        tn_dims = (((0,), (0,)), ((), ()))

        def tiles(js, carry, masks=(None, None)):
            dq, run_sp, run_g = carry
            starts = [pl.multiple_of(j * t, t) for j in js]
            zs = [lax.dot_general(qv, k_ref[0, pl.ds(st, t), :], nt_dims, preferred_element_type=F32) for st in starts]
            das = [lax.dot_general(dov, v_ref[0, pl.ds(st, t), :], nt_dims, preferred_element_type=F32) for st in starts]
            sps = [_softplus(z) for z in zs]
            log_sigs = [z - sp for z, sp in zip(zs, sps)]
            sps = [sp if m is None else jnp.where(m, sp, 0.0) for sp, m in zip(sps, masks)]
            pres = [_split_dot(sp, upto, 2) for sp in sps]
            a_s, gs = [], []
            for sp, log_sig, pre, da, m in zip(sps, log_sigs, pres, das, masks):
                a = jnp.exp(log_sig - (total - (run_sp + pre)))
                if m is not None:
                    a = jnp.where(m, a, 0.0)
                a_s.append(a)
                gs.append(a * da)
                run_sp = run_sp + jnp.sum(sp, axis=1, keepdims=True)
            g_pres = [_split_dot(g, before, 3) for g in gs]
            for st, a, g, g_pre, log_sig, m in zip(starts, a_s, gs, g_pres, log_sigs, masks):
                sig = jnp.exp(log_sig)
                dz = g * (1.0 - sig) - sig * (run_g + g_pre)
                if m is not None:
                    dz = jnp.where(m, dz, 0.0)
                dz = dz.astype(BF16)
                dq = dq + jnp.dot(dz, k_ref[0, pl.ds(st, t), :], preferred_element_type=F32)
                dk_acc[pl.ds(st, t), :] += lax.dot_general(dz, qv, tn_dims, preferred_element_type=F32)
                dv_acc[pl.ds(st, t), :] += lax.dot_general(a.astype(BF16), dov, tn_dims, preferred_element_type=F32)
                run_g = run_g + jnp.sum(g, axis=1, keepdims=True)
            return dq, run_sp, run_g

        zero = jnp.zeros((t, 1), F32)
        top = jnp.maximum(i - 1, 0)
        first = jnp.clip(first_ref[0, 0, 0, 0].astype(jnp.int32), 0, top)
        count = top - first
        carry = lax.fori_loop(0, count // 2, lambda pp, c: tiles([first + 2 * pp, first + 2 * pp + 1], c),
                              (jnp.zeros((t, hd), F32), zero, zero))
        carry = lax.fori_loop(0, count % 2, lambda _, c: tiles([top - 1], c), carry)
        below, whole = _sb_masks(t, i)
        dq, _, _ = tiles([top, i], carry, (whole, below))
        dq_ref[0] = (dq * scale).astype(BF16)

        @pl.when(i == nq - 1)
        def _():
            dk_ref[0] = dk_acc[...].astype(BF16)
            dv_ref[0] = dv_acc[...].astype(BF16)

        pl.when((hh == h - 1) & (i == nq - 1))(finish)

    full = pl.BlockSpec((1, s_len, hd), lambda hh, i: (hh, 0, 0))
    tile = pl.BlockSpec((1, t, hd), lambda hh, i: (hh, i, 0))
    hbm = pl.BlockSpec(memory_space=pl.ANY)
    out = SDS((h, s_len, hd), BF16)
    return pl.pallas_call(
        body, name=name, out_shape=(out, out, out, SDS(payload.shape, payload.dtype)), grid=(h, nq),
        in_specs=[tile, full, full, tile, pl.BlockSpec((1, t, 1), lambda hh, i: (hh, i, 0)),
                  pl.BlockSpec((1, 1) + SB_FIRST_BLOCK, lambda hh, i: (hh, i, 0, 0)), hbm],
        out_specs=(tile, full, full, hbm),
        scratch_shapes=[pltpu.VMEM((s_len, hd), F32), pltpu.VMEM((s_len, hd), F32)] + _COMM_SEMAPHORES,
        compiler_params=_params(has_side_effects=True),
    )(q, k, v, do, tot, first, payload)


def _dl_scores(qv, kk, n):
    s = lax.dot_general(qv, kk, (((1,), (1,)), ((), ())), preferred_element_type=F32) * (HEAD_DIM ** -0.5)
    r = lax.broadcasted_iota(jnp.int32, s.shape, 0)
    c = lax.broadcasted_iota(jnp.int32, s.shape, 1)
    valid = (c >= r) & (c - r <= BLOCK) & ((n > 0) | (c >= BLOCK))
    return jnp.where(valid, s, NEG)


DL_UNROLL = 8
DL_FWD_UNROLL = 8
DL_PAIR = 2 * HEAD_DIM
DL_Q_BLOCK0 = 0
DL_K_BLOCK0 = DL_WIDTH // DL_PAIR
DL_V_BLOCK0 = (IN_WIDTH - DL_WIDTH) // DL_PAIR
DL_DO_BLOCK0 = (SB_WIDTH + CV_WIDTH) // DL_PAIR


def _dl_rows(idx, nb, dil):
    r, n = idx // nb, idx % nb
    cur = pl.ds(r + n * (BLOCK * dil), BLOCK, stride=dil)
    prev = pl.ds(r + jnp.maximum(n - 1, 0) * (BLOCK * dil), BLOCK, stride=dil)
    return n, cur, prev


def _dl_window(ref, cur, prev):
    return jnp.concatenate([ref[prev, :], ref[cur, :]], axis=0).astype(BF16)


def _head_lanes():
    first = lax.broadcasted_iota(jnp.int32, (BLOCK, DL_PAIR), 1) < HEAD_DIM
    return first, jnp.logical_not(first)


def _dl_fwd(qk, u, dil, name, payload=None):
    s_len = qk.shape[0]
    nb = s_len // dil // BLOCK

    def body(q_ref, k_ref, v_ref, o_ref, lse_ref):
        heads = _head_lanes()

        def step(idx, _):
            n, cur, prev = _dl_rows(idx, nb, dil)
            q = q_ref[cur, :]
            kk = _dl_window(k_ref, cur, prev)
            vv = _dl_window(v_ref, cur, prev)
            o, lse = None, None
            for lanes in heads:
                s = _dl_scores(jnp.where(lanes, q, 0.0).astype(BF16), kk, n)
                m = jnp.max(s, axis=-1, keepdims=True)
                p = jnp.exp(s - m)
                den = jnp.sum(p, axis=-1, keepdims=True)
                o_h = jnp.dot((p / den).astype(BF16), vv, preferred_element_type=F32)
                lse_h = jnp.broadcast_to(m + jnp.log(den), (BLOCK, DL_PAIR))
                o = o_h if o is None else jnp.where(heads[0], o, o_h)
                lse = lse_h if lse is None else jnp.where(heads[0], lse, lse_h)
            o_ref[cur, :] = o
            lse_ref[cur, :] = lse
            return 0

        lax.fori_loop(0, s_len // BLOCK, step, 0, unroll=DL_FWD_UNROLL)

    col = lambda first: pl.BlockSpec((s_len, DL_PAIR), lambda i: (0, first + i))
    out = SDS((s_len, DL_WIDTH), F32)
    steps = DL_WIDTH // DL_PAIR
    if payload is not None:
        step = lambda: pl.program_id(0)
        payload = ("gather", payload, lambda: (step() == 0, step() == steps - 1, step() == steps - 1))
    return _hosted_call(body, payload, name=name, out_shape=(out, out), grid=(steps,),
                        in_specs=[col(DL_Q_BLOCK0), col(DL_K_BLOCK0), col(DL_V_BLOCK0)], out_specs=(col(0), col(0)),
                        args=(qk, qk, u))


def _dl_bwd(qk, u, dmix, o_mix, wt, lse, dil, name, payload=None):
    s_len = qk.shape[0]
    nb = s_len // dil // BLOCK
    scale = HEAD_DIM ** -0.5
    nt_dims = (((1,), (1,)), ((), ()))
    tn_dims = (((0,), (0,)), ((), ()))

    def body(q_ref, k_ref, v_ref, do_ref, om_ref, wt_ref, lse_ref, dq_ref, dk_ref, dv_ref):
        dk_ref[...] = jnp.zeros_like(dk_ref)
        dv_ref[...] = jnp.zeros_like(dv_ref)
        heads = _head_lanes()

        def step(idx, _):
            n, cur, prev = _dl_rows(idx, nb, dil)
            q = q_ref[cur, :]
            kk = _dl_window(k_ref, cur, prev)
            vv = _dl_window(v_ref, cur, prev)
            dov = do_ref[cur, :]
            d_lanes = dov * om_ref[cur, :]
            w_lanes = wt_ref[cur, :]
            lse_lanes = lse_ref[cur, :]
            dq, dkk, dvv = None, None, None
            for lanes in heads:
                qm = jnp.where(lanes, q, 0.0).astype(BF16)
                s = _dl_scores(qm, kk, n)
                p = jnp.exp(s - jnp.max(jnp.where(lanes, lse_lanes, NEG), axis=-1, keepdims=True))
                w = jnp.max(jnp.where(lanes, w_lanes, 0.0), axis=-1, keepdims=True)
                d_all = jnp.sum(jnp.where(lanes, d_lanes, 0.0), axis=-1, keepdims=True)
                do_n = jnp.where(lanes, dov * w, 0.0).astype(BF16)
                dp = lax.dot_general(do_n, vv, nt_dims, preferred_element_type=F32)
                ds = (p * (dp - w * d_all) * scale).astype(BF16)
                dq_h = jnp.dot(ds, kk, preferred_element_type=F32)
                dkk_h = lax.dot_general(ds, qm, tn_dims, preferred_element_type=F32)
                dvv_h = lax.dot_general(p.astype(BF16), do_n, tn_dims, preferred_element_type=F32)
                dq = dq_h if dq is None else jnp.where(heads[0], dq, dq_h)
                dkk = dkk_h if dkk is None else dkk + dkk_h
                dvv = dvv_h if dvv is None else dvv + dvv_h
            dq_ref[cur, :] = dq
            dk_ref[prev, :] += dkk[:BLOCK]
            dv_ref[prev, :] += dvv[:BLOCK]
            dk_ref[cur, :] += dkk[BLOCK:]
            dv_ref[cur, :] += dvv[BLOCK:]
            return 0

        lax.fori_loop(0, s_len // BLOCK, step, 0, unroll=DL_UNROLL)

    col = lambda first: pl.BlockSpec((s_len, DL_PAIR), lambda i: (0, first + i))
    out = SDS((s_len, DL_WIDTH), F32)
    steps = DL_WIDTH // DL_PAIR
    if payload is not None:
        step = lambda: pl.program_id(0)
        payload = ("all_to_all", payload, lambda: (step() == 0, None, step() == steps - 1))
    return _hosted_call(
        body, payload, name=name, out_shape=(out, out, out), grid=(steps,),
        in_specs=[col(DL_Q_BLOCK0), col(DL_K_BLOCK0), col(DL_V_BLOCK0), col(DL_DO_BLOCK0), col(0), col(0), col(0)],
        out_specs=(col(0), col(0), col(0)), args=(qk, qk, u, dmix, o_mix, wt, lse))


def _dl_mix_fwd(outs, lses, name):
    rows, w = outs[0].shape
    t = 256

    def body(o1, o2, o3, l1, l2, l3, ob_ref, of_ref, w1, w2, w3):
        a, b, c = l1[...], l2[...], l3[...]
        m = jnp.maximum(jnp.maximum(a, b), c)
        ea, eb, ec = jnp.exp(a - m), jnp.exp(b - m), jnp.exp(c - m)
        den = ea + eb + ec
        wa, wb, wc = ea / den, eb / den, ec / den
        o = wa * o1[...] + wb * o2[...] + wc * o3[...]
        ob_ref[...] = o.astype(BF16)
        of_ref[...] = o
        w1[...] = wa
        w2[...] = wb
        w3[...] = wc

    row = pl.BlockSpec((t, w), lambda i: (i, 0))
    f = SDS((rows, w), F32)
    return pl.pallas_call(body, name=name, out_shape=(SDS((rows, w), BF16), f, f, f, f), grid=(rows // t,),
                          in_specs=[row] * 6, out_specs=(row,) * 5, compiler_params=_params())(*outs, *lses)


def _x_probs(qh, kh):
    s = lax.dot_general(qh, kh, (((1,), (1,)), ((), ())), preferred_element_type=F32) * (X_HEAD_DIM ** -0.5)
    e = jnp.exp(s - jnp.max(s, axis=-1, keepdims=True))
    return e / jnp.sum(e, axis=-1, keepdims=True)


def _xattn_fwd(q, k, v, name):
    rows, d = q.shape
    t = 512

    def body(q_ref, k_ref, v_ref, o_ref):
        for hh in range(X_HEADS):
            cols = slice(hh * X_HEAD_DIM, (hh + 1) * X_HEAD_DIM)
            p = _x_probs(q_ref[:, cols], k_ref[:, cols])
            o_ref[:, cols] = jnp.dot(p.astype(BF16), v_ref[:, cols], preferred_element_type=F32).astype(BF16)

    row = pl.BlockSpec((t, d), lambda i: (i, 0))
    mem = pl.BlockSpec((N_MEM, d), lambda i: (0, 0))
    return pl.pallas_call(body, name=name, out_shape=SDS((rows, d), BF16), grid=(rows // t,), in_specs=[row, mem, mem],
                          out_specs=row, compiler_params=_params())(q, k, v)


def _xattn_bwd(q, k, v, do, name):
    rows, d = q.shape
    t = 512
    scale = X_HEAD_DIM ** -0.5

    def body(q_ref, k_ref, v_ref, do_ref, dq_ref, dk_ref, dv_ref):
        @pl.when(pl.program_id(0) == 0)
        def _():
            dk_ref[...] = jnp.zeros_like(dk_ref)
            dv_ref[...] = jnp.zeros_like(dv_ref)

        for hh in range(X_HEADS):
            cols = slice(hh * X_HEAD_DIM, (hh + 1) * X_HEAD_DIM)
            qh, kh, vh, doh = q_ref[:, cols], k_ref[:, cols], v_ref[:, cols], do_ref[:, cols]
            p = _x_probs(qh, kh)
            dp = lax.dot_general(doh, vh, (((1,), (1,)), ((), ())), preferred_element_type=F32)
            ds = (p * (dp - jnp.sum(p * dp, axis=-1, keepdims=True)) * scale).astype(BF16)
            dq_ref[:, cols] = jnp.dot(ds, kh, preferred_element_type=F32).astype(BF16)
            dk_ref[:, cols] += lax.dot_general(ds, qh, (((0,), (0,)), ((), ())), preferred_element_type=F32)
            dv_ref[:, cols] += lax.dot_general(p.astype(BF16), doh, (((0,), (0,)), ((), ())), preferred_element_type=F32)

    row = pl.BlockSpec((t, d), lambda i: (i, 0))
    mem = pl.BlockSpec((N_MEM, d), lambda i: (0, 0))
    return pl.pallas_call(
        body, name=name, out_shape=(SDS((rows, d), BF16), SDS((N_MEM, d), F32), SDS((N_MEM, d), F32)), grid=(rows // t,),
        in_specs=[row, mem, mem, row], out_specs=(row, mem, mem), compiler_params=_params(),
    )(q, k, v, do)


CV_TILE = 256
CV_HALO = 32
CV_LEAD = CV_HALO - (CV_KERNEL - 1)


def _shifted(win, off, rows):
    n = win.shape[0]
    return pltpu.roll(win, (n - off) % n, axis=0)[:rows]


def _glu(val, gate):
    return val * jax.nn.sigmoid(gate)


def _ln_parts(c):
    mu = jnp.mean(c, axis=-1, keepdims=True)
    xc = c - mu
    rstd = lax.rsqrt(jnp.mean(xc * xc, axis=-1, keepdims=True) + EPS)
    return xc * rstd, rstd


def _cv_fwd(u, cv_w, cv_b, ln_g, ln_b, name):
    rows = u.shape[0]
    t, w = CV_TILE, CV_WIDTH
    val_col = 3 * SB_WIDTH // w
    ratio = t // CV_HALO

    def body(val_ref, gate_ref, pval_ref, pgate_ref, w_ref, b_ref, g_ref, beta_ref, s_ref, c_ref):
        i = pl.program_id(0)
        hist = jnp.where(i > 0, _glu(pval_ref[...], pgate_ref[...]), 0.0)
        win = jnp.concatenate([hist, _glu(val_ref[...], gate_ref[...])], axis=0)
        acc = jnp.broadcast_to(b_ref[...], (t, w))
        for kk in range(CV_KERNEL):
            acc = acc + _shifted(win, CV_LEAD + kk, t) * w_ref[kk:kk + 1, :]
        c_ref[...] = acc
        n, _ = _ln_parts(acc)
        y = n * g_ref[...] + beta_ref[...]
        s_ref[...] = (y * jax.nn.sigmoid(y)).astype(BF16)

    cur = lambda col: pl.BlockSpec((t, w), lambda i: (i, col))
    prev = lambda col: pl.BlockSpec((CV_HALO, w), lambda i: (jnp.maximum(i * ratio - 1, 0), col))
    vec = pl.BlockSpec((1, w), lambda i: (0, 0))
    return pl.pallas_call(
        body, name=name, out_shape=(SDS((rows, w), BF16), SDS((rows, w), F32)), grid=(rows // t,),
        in_specs=[cur(val_col), cur(val_col + 1), prev(val_col), prev(val_col + 1),
                  pl.BlockSpec((CV_KERNEL, w), lambda i: (0, 0)), vec, vec, vec],
        out_specs=(pl.BlockSpec((t, w), lambda i: (i, 0)),) * 2, compiler_params=_params(),
    )(u, u, u, u, cv_w, cv_b, ln_g, ln_b)


def _cv_bwd(u, c, ds, db_out, cv_w, ln_g, ln_b, name, payload=None):
    rows = u.shape[0]
    t, w = CV_TILE, CV_WIDTH
    val_col = 3 * SB_WIDTH // w
    ratio = t // CV_HALO
    nt = rows // t

    def conv_out_grad(c_v, ds_v, g_v, beta_v):
        n, rstd = _ln_parts(c_v)
        y = n * g_v + beta_v
        sig = jax.nn.sigmoid(y)
        dy = ds_v * (sig * (1.0 + y * (1.0 - sig)))
        dn = dy * g_v
        dc = rstd * (dn - jnp.mean(dn, axis=-1, keepdims=True) - n * jnp.mean(dn * n, axis=-1, keepdims=True))
        return dc, dy, n

    def body(val_ref, gate_ref, pval_ref, pgate_ref, c_ref, nc_ref, ds_ref, nds_ref, dbo_ref, w_ref, g_ref, beta_ref,
             dvg_ref, dw_ref, db_ref, dg_ref, dbeta_ref, dpwb_ref):
        i = pl.program_id(0)

        @pl.when(i == 0)
        def _():
            for r in (dw_ref, db_ref, dg_ref, dbeta_ref, dpwb_ref):
                r[...] = jnp.zeros_like(r)

        g_v, beta_v = g_ref[...], beta_ref[...]
        dc, dy, n = conv_out_grad(c_ref[...], ds_ref[...], g_v, beta_v)
        dc_next, _, _ = conv_out_grad(nc_ref[...], nds_ref[...], g_v, beta_v)
        dc_next = jnp.where(i < nt - 1, dc_next, 0.0)
        dg_ref[...] += jnp.sum(dy * n, axis=0, keepdims=True)
        dbeta_ref[...] += jnp.sum(dy, axis=0, keepdims=True)
        db_ref[...] += jnp.sum(dc, axis=0, keepdims=True)
        dpwb_ref[...] += jnp.sum(dbo_ref[...], axis=0, keepdims=True)

        val, gate = val_ref[...], gate_ref[...]
        hist = jnp.where(i > 0, _glu(pval_ref[...], pgate_ref[...]), 0.0)
        win = jnp.concatenate([hist, _glu(val, gate)], axis=0)
        dc_ext = jnp.concatenate([dc, dc_next], axis=0)
        dglu = jnp.zeros((t, w), F32)
        for kk in range(CV_KERNEL):
            dw_ref[kk:kk + 1, :] += jnp.sum(dc * _shifted(win, CV_LEAD + kk, t), axis=0, keepdims=True)
            dglu = dglu + _shifted(dc_ext, CV_KERNEL - 1 - kk, t) * w_ref[kk:kk + 1, :]
        sig = jax.nn.sigmoid(gate)
        dvg_ref[:, 0:w] = (dglu * sig).astype(BF16)
        dvg_ref[:, w:2 * w] = (dglu * val * sig * (1.0 - sig)).astype(BF16)

    cur = lambda col: pl.BlockSpec((t, w), lambda i: (i, col))
    prev = lambda col: pl.BlockSpec((CV_HALO, w), lambda i: (jnp.maximum(i * ratio - 1, 0), col))
    nxt = pl.BlockSpec((CV_HALO, w), lambda i: (jnp.minimum((i + 1) * ratio, rows // CV_HALO - 1), 0))
    vec = pl.BlockSpec((1, w), lambda i: (0, 0))
    if payload is not None:
        payload = ("all_to_all", payload, lambda: (pl.program_id(0) == 0, None, pl.program_id(0) == nt - 1))
    return _hosted_call(
        body, payload, name=name,
        out_shape=(SDS((rows, 2 * w), BF16), SDS((CV_HALO, w), F32), SDS((1, w), F32), SDS((1, w), F32), SDS((1, w), F32),
                   SDS((1, w), F32)),
        grid=(nt,),
        in_specs=[cur(val_col), cur(val_col + 1), prev(val_col), prev(val_col + 1), cur(0), nxt, cur(0), nxt, cur(0),
                  pl.BlockSpec((CV_KERNEL, w), lambda i: (0, 0)), vec, vec],
        out_specs=(pl.BlockSpec((t, 2 * w), lambda i: (i, 0)), pl.BlockSpec((CV_HALO, w), lambda i: (0, 0)), vec, vec, vec, vec),
        args=(u, u, u, u, c, c, ds, ds, db_out, cv_w, ln_g, ln_b))


FFN_TILE = 512
FFN_CHUNK = 256
FFN_COLS = 256
FFN_HALO = 8
FFN_KERNEL = 3
N_FF_BLOCKS = D_FF // FFN_COLS


def _conv3(prev8, cur, w_ref, b_ref, first):
    t = cur.shape[0]
    win = jnp.concatenate([jnp.where(first, 0.0, prev8), cur], axis=0)
    return (b_ref[...] + _shifted(win, FFN_HALO - 2, t) * w_ref[0:1, :] + _shifted(win, FFN_HALO - 1, t) * w_ref[1:2, :]
            + cur * w_ref[2:3, :])


def _gelu_gate(gate, val):
    return jax.nn.gelu(gate, approximate=True) * val


GELU_C0 = math.sqrt(2.0 / math.pi)
GELU_C1 = 0.044715


def _gelu_gate_bwd(gate, val, dout):
    sq = gate * gate
    th = jnp.tanh(GELU_C0 * gate * (1.0 + GELU_C1 * sq))
    half_cdf = 0.5 * (1.0 + th)
    slope = half_cdf + 0.5 * gate * (1.0 - th * th) * (GELU_C0 * (1.0 + 3.0 * GELU_C1 * sq))
    return dout * val * slope, dout * (gate * half_cdf)


def _ffn_specs(t):
    ratio = t // FFN_HALO
    cur = pl.BlockSpec((t, FFN_COLS), lambda j, i: (i, j))
    prev = pl.BlockSpec((FFN_HALO, FFN_COLS), lambda j, i: (jnp.maximum(i * ratio - 1, 0), j))
    wsp = pl.BlockSpec((FFN_KERNEL, FFN_COLS), lambda j, i: (0, j))
    bsp = pl.BlockSpec((1, FFN_COLS), lambda j, i: (0, j))
    return cur, prev, wsp, bsp


def _ffn_host_steps(row_tiles):
    def when():
        j, i = pl.program_id(0), pl.program_id(1)
        return (j == 0) & (i == 0), (j == (3 * N_FF_BLOCKS) // 4) & (i == 0), (j == N_FF_BLOCKS - 1) & (i == row_tiles - 1)
    return when


def _ffn_act_fwd(up_g, up_v, w_g, w_v, b_g, b_v, name, payload=None):
    rows = up_g.shape[0]
    t = FFN_TILE
    cur, prev, wsp, bsp = _ffn_specs(t)

    def body(g_ref, v_ref, pg_ref, pv_ref, wg_ref, wv_ref, bg_ref, bv_ref, o_ref):
        first = pl.program_id(1) == 0
        gate = _conv3(pg_ref[...], g_ref[...], wg_ref, bg_ref, first)
        val = _conv3(pv_ref[...], v_ref[...], wv_ref, bv_ref, first)
        o_ref[...] = _gelu_gate(gate, val).astype(BF16)

    if payload is not None:
        payload = ("gather", payload, _ffn_host_steps(rows // t))
    return _hosted_call(
        body, payload, name=name, out_shape=(SDS((rows, D_FF), BF16),), grid=(N_FF_BLOCKS, rows // t),
        in_specs=[cur, cur, prev, prev, wsp, wsp, bsp, bsp], out_specs=(cur,), args=(up_g, up_v, up_g, up_v, w_g, w_v, b_g, b_v))


def _ffn_act_bwd(up_g, up_v, dact, w_g, w_v, b_g, b_v, name, payload=None):
    rows = up_g.shape[0]
    t = FFN_TILE
    ch = FFN_CHUNK
    che = ch + FFN_HALO
    ratio = t // FFN_HALO
    nt = rows // t
    cur, prev, wsp, bsp = _ffn_specs(t)
    nxt = pl.BlockSpec((FFN_HALO, FFN_COLS), lambda j, i: (jnp.minimum((i + 1) * ratio, rows // FFN_HALO - 1), j))

    def body(g_ref, v_ref, pg_ref, pv_ref, ng_ref, nv_ref, da_ref, nda_ref, wg_ref, wv_ref, bg_ref, bv_ref,
             dug_ref, duv_ref, dwg_ref, dwv_ref, dbg_ref, dbv_ref, win_g, win_v, da_win):
        i = pl.program_id(1)
        first = i == 0

        @pl.when(first)
        def _():
            for r in (dwg_ref, dwv_ref, dbg_ref, dbv_ref):
                r[...] = jnp.zeros_like(r)

        for win, pre, x, nx in ((win_g, pg_ref, g_ref, ng_ref), (win_v, pv_ref, v_ref, nv_ref)):
            win[0:FFN_HALO, :] = jnp.where(first, 0.0, pre[...])
            win[FFN_HALO:FFN_HALO + t, :] = x[...]
            win[FFN_HALO + t:, :] = nx[...]
        da_win[0:t, :] = da_ref[...]
        da_win[t:, :] = jnp.where(i < nt - 1, nda_ref[...], 0.0)
        halves = ((win_g, wg_ref, bg_ref, dug_ref), (win_v, wv_ref, bv_ref, duv_ref))

        def chunk(c, sums):
            base = pl.multiple_of(c * ch, ch)
            taps, convs = [], []
            for win, w_ref, b_ref, _ in halves:
                w = win[pl.ds(base, ch + 2 * FFN_HALO), :]
                shifted = [_shifted(w, FFN_HALO - 2 + kk, che) for kk in range(FFN_KERNEL)]
                taps.append(shifted)
                convs.append(b_ref[...] + sum(s * w_ref[kk:kk + 1, :] for kk, s in enumerate(shifted)))
            dconvs = _gelu_gate_bwd(*convs, da_win[pl.ds(base, che), :])
            new = []
            for dc_ext, shifted, (_, w_ref, _, du_ref), (dw, db) in zip(dconvs, taps, halves, sums):
                dc = dc_ext[:ch]
                du_ref[pl.ds(base, ch), :] = (dc * w_ref[2:3, :] + _shifted(dc_ext, 1, ch) * w_ref[1:2, :]
                                              + _shifted(dc_ext, 2, ch) * w_ref[0:1, :]).astype(BF16)
                dw = [dw[kk] + jnp.sum(dc * shifted[kk][:ch], axis=0, keepdims=True) for kk in range(FFN_KERNEL)]
                new.append((dw, db + jnp.sum(dc, axis=0, keepdims=True)))
            return new

        zero = jnp.zeros((1, FFN_COLS), F32)
        sums = lax.fori_loop(0, t // ch, chunk, [([zero] * FFN_KERNEL, zero)] * 2, unroll=2)
        for (dw, db), dw_ref, db_ref in zip(sums, (dwg_ref, dwv_ref), (dbg_ref, dbv_ref)):
            for kk in range(FFN_KERNEL):
                dw_ref[kk:kk + 1, :] += dw[kk]
            db_ref[...] += db

    big, wshape, bshape = SDS((rows, D_FF), BF16), SDS((FFN_KERNEL, D_FF), F32), SDS((1, D_FF), F32)
    if payload is not None:
        payload = ("all_to_all", payload, _ffn_host_steps(nt))
    window = pltpu.VMEM((t + 2 * FFN_HALO, FFN_COLS), F32)
    return _hosted_call(
        body, payload, name=name, out_shape=(big, big, wshape, wshape, bshape, bshape), grid=(N_FF_BLOCKS, nt),
        in_specs=[cur, cur, prev, prev, nxt, nxt, cur, nxt, wsp, wsp, bsp, bsp], out_specs=(cur, cur, wsp, wsp, bsp, bsp),
        scratch_shapes=(window, window, pltpu.VMEM((t + FFN_HALO, FFN_COLS), F32)),
        args=(up_g, up_v, up_g, up_v, up_g, up_v, dact, dact, w_g, w_v, b_g, b_v))


def _adamw_update(parts, w_ref, m_ref, v_ref, g_ref, d_ref, nm_ref, nv_ref):
    g = parts[0].astype(F32)
    for s in range(1, N_DEV):
        g = g + parts[s].astype(F32)
    nm = ADAM_B1 * m_ref[...] + (1.0 - ADAM_B1) * g
    nv = ADAM_B2 * v_ref[...] + (1.0 - ADAM_B2) * jnp.square(g)
    m_hat = nm / (1.0 - ADAM_B1 ** ADAM_STEP)
    v_hat = nv / (1.0 - ADAM_B2 ** ADAM_STEP)
    g_ref[...] = g
    d_ref[...] = -ADAM_LR * (m_hat / (jnp.sqrt(v_hat) + ADAM_EPS) + ADAM_WD * w_ref[...])
    nm_ref[...] = nm
    nv_ref[...] = nv


def _adamw(parts, w, m, v, name):
    rows, cols = w.shape
    t = _pick(rows, (512, 256, 128)) if rows > 512 else rows

    def body(p_ref, *refs):
        _adamw_update(p_ref[...], *refs)

    row = pl.BlockSpec((t, cols), lambda i: (i, 0))
    out = SDS((rows, cols), F32)
    return pl.pallas_call(
        body, name=name, out_shape=(out,) * 4, grid=(rows // t,),
        in_specs=[pl.BlockSpec((N_DEV, t, cols), lambda i: (0, i, 0)), row, row, row], out_specs=(row,) * 4,
        compiler_params=_params(),
    )(parts, w, m, v)


def _adamw_packed(sources, w, m, v, name):
    rows, cols = w.shape
    t = ADAMW_ROW_BLOCK
    nb = rows // DEPTH // t
    (src0, first0), (src1, first1) = sources
    assert first0 % t == 0 and first1 % t == 0 and rows % (DEPTH * t) == 0

    def body(p0_ref, p1_ref, *refs):
        layer = pl.program_id(0)
        _adamw_update(jnp.where(layer == 0, p0_ref[...], p1_ref[...]), *refs)

    spec0 = pl.BlockSpec((N_DEV, t, cols), lambda l, i: (0, first0 // t + i * (1 - l) + (nb - 1) * l, 0))
    spec1 = pl.BlockSpec((N_DEV, t, cols), lambda l, i: (0, first1 // t + i * l, 0))
    row = pl.BlockSpec((t, cols), lambda l, i: (l * nb + i, 0))
    out = SDS((rows, cols), F32)
    return pl.pallas_call(body, name=name, out_shape=(out,) * 4, grid=(DEPTH, nb), in_specs=[spec0, spec1, row, row, row],
                          out_specs=(row,) * 4, compiler_params=_params())(src0, src1, w, m, v)


_COMM_SEMAPHORES = [pltpu.SemaphoreType.DMA((N_DEV - 1,)), pltpu.SemaphoreType.DMA((N_DEV - 1,)), pltpu.SemaphoreType.DMA]


def _gather_steps(x_ref, out_ref, send_sems, recv_sems, local_sem):
    x_, y_, c_ = lax.axis_index("x"), lax.axis_index("y"), lax.axis_index("c")
    me, sibling = (x_, y_, c_), (x_, y_, 1 - c_)
    chips = [(1 - x_, y_), (x_, 1 - y_), (1 - x_, 1 - y_)]

    def slot(px, py, pc):
        return out_ref.at[4 * px + 2 * py + pc]

    def copy(kk, block, to, src=None):
        return pltpu.make_async_remote_copy(
            src_ref=slot(*block) if src is None else src, dst_ref=slot(*block),
            send_sem=send_sems.at[kk], recv_sem=recv_sems.at[kk], device_id=to, device_id_type=MESH)

    def mine():
        return pltpu.make_async_copy(x_ref, slot(*me), local_sem)

    def first():
        return [copy(0, me, sibling, src=x_ref)] + [copy(1 + j, me, (*chip, c_), src=x_ref) for j, chip in enumerate(chips)]

    def passed():
        return [copy(4 + j, (*chip, c_), sibling) for j, chip in enumerate(chips)]

    def start():
        mine().start()
        for cp in first():
            cp.start()

    def forward():
        for j, (chip, cp) in enumerate(zip(chips, passed())):
            copy(1 + j, (*chip, c_), me).wait_recv()
            cp.start()

    def finish():
        copy(0, sibling, me).wait_recv()
        for j, chip in enumerate(chips):
            copy(4 + j, (*chip, 1 - c_), me).wait_recv()
        for cp in first() + passed():
            cp.wait_send()
        mine().wait()

    return start, forward, finish


def _exchange_steps(x_ref, out_ref, send_sems, recv_sems, local_sem):
    x_, y_, c_ = lax.axis_index("x"), lax.axis_index("y"), lax.axis_index("c")
    me = 4 * x_ + 2 * y_ + c_

    def mine():
        return pltpu.make_async_copy(x_ref.at[me], out_ref.at[me], local_sem)

    def copies():
        out = []
        for r in range(1, N_DEV):
            px = 1 - x_ if r & 4 else x_
            py = 1 - y_ if r & 2 else y_
            pc = 1 - c_ if r & 1 else c_
            out.append(pltpu.make_async_remote_copy(
                src_ref=x_ref.at[4 * px + 2 * py + pc], dst_ref=out_ref.at[me],
                send_sem=send_sems.at[r - 1], recv_sem=recv_sems.at[r - 1], device_id=(px, py, pc), device_id_type=MESH))
        return out

    def start():
        mine().start()
        for cp in copies():
            cp.start()

    def finish():
        for cp in copies():
            cp.wait_recv()
        for cp in copies():
            cp.wait_send()
        mine().wait()

    return start, finish


def _hosted_call(body, exchange, *, name, out_shape, grid, in_specs, out_specs, args, scratch_shapes=()):
    if exchange is None:
        return pl.pallas_call(body, name=name, out_shape=out_shape, grid=grid, in_specs=in_specs, out_specs=out_specs,
                              scratch_shapes=list(scratch_shapes), compiler_params=_params())(*args)
    kind, payload, when = exchange
    n_in, n_out, n_scratch = len(in_specs), len(out_specs), len(scratch_shapes)
    result = SDS((N_DEV,) + payload.shape, payload.dtype) if kind == "gather" else SDS(payload.shape, payload.dtype)

    def hosting(*refs):
        ins, pay_ref = refs[:n_in], refs[n_in]
        outs, res_ref = refs[n_in + 1:n_in + 1 + n_out], refs[n_in + 1 + n_out]
        rest = refs[n_in + 2 + n_out:]
        scratch, sems = rest[:n_scratch], rest[n_scratch:]
        first, middle, last = when()
        if kind == "gather":
            start, forward, finish = _gather_steps(pay_ref, res_ref, *sems)
            pl.when(first)(start)
            pl.when(middle)(forward)
        else:
            start, finish = _exchange_steps(pay_ref, res_ref, *sems)
            pl.when(first)(start)
        body(*ins, *outs, *scratch)
        pl.when(last)(finish)

    hbm = pl.BlockSpec(memory_space=pl.ANY)
    return pl.pallas_call(
        hosting, name=name, out_shape=tuple(out_shape) + (result,), grid=grid, in_specs=list(in_specs) + [hbm],
        out_specs=tuple(out_specs) + (hbm,), scratch_shapes=list(scratch_shapes) + _COMM_SEMAPHORES,
        compiler_params=_params(has_side_effects=True),
    )(*args, payload)


def _exchange_alone(xs, make_steps, out_shapes, name):
    n = len(xs)

    def body(*refs):
        sems = refs[2 * n:]
        steps = [make_steps(refs[k], refs[n + k], *sems[3 * k:3 * k + 3]) for k in range(n)]
        for stage in zip(*steps):
            for step in stage:
                step()

    hbm = pl.BlockSpec(memory_space=pl.ANY)
    return pl.pallas_call(body, name=name, out_shape=tuple(out_shapes), in_specs=[hbm] * n, out_specs=(hbm,) * n,
                          scratch_shapes=_COMM_SEMAPHORES * n, compiler_params=pltpu.CompilerParams(has_side_effects=True))(*xs)


def _all_gather(xs, name):
    return _exchange_alone(xs, _gather_steps, [SDS((N_DEV,) + x.shape, x.dtype) for x in xs], name)


def _all_to_all(xs, name):
    return _exchange_alone(xs, _exchange_steps, [SDS(x.shape, x.dtype) for x in xs], name)


BIG = ("w_in", "cv_pw_w", "w_out", "x_wq", "x_wk", "x_wv", "x_wo", "ffn_w_up", "ffn_w_down")
_MIXER = (("w_in", 352), ("w_out", 128))
_CROSS = (("x_wq", 128), ("x_wk", 128), ("x_wv", 128), ("x_wo", 128))
GROUPS = {
    "a": (("w_in", 0, 352),),
    "o0": (("w_out", 0, 128),),
    "m1": tuple((n, 1, r) for n, r in _MIXER),
    "bx": tuple((n, 0, r) for n, r in _CROSS),
    "cx": tuple((n, 1, r) for n, r in _CROSS),
    "bf": (("ffn_w_down", 0, 352),),
    "cf": (("ffn_w_down", 1, 352),),
}
GRADIENT_GROUPS = ("a", "o0", "m1", "bf", "bx", "cf", "cx")
PW_GROUP = "o0"
TRANSPOSED = ("w_in", "ffn_w_up")
PW_ROWS = 16
ADAMW_ROW_BLOCK = 32


def _group_rows(group):
    out, first = {}, 0
    for n, l, r in GROUPS[group]:
        out[(n, l)] = (first, r)
        first += r
    return out


def _where_is(name, layer):
    for group in GRADIENT_GROUPS:
        rows = _group_rows(group)
        if (name, layer) in rows:
            return (group,) + rows[(name, layer)]
    raise KeyError((name, layer))


def _pack_weights(group, wts):
    pieces = []
    for n, l, _ in GROUPS[group]:
        w = wts[n][l].astype(BF16)
        pieces.append(w.T if n in TRANSPOSED else w)
    if group == PW_GROUP:
        pieces.append(wts["cv_pw_w"].astype(BF16).reshape(PW_ROWS, PAYLOAD_COLS))
    return jnp.concatenate(pieces, axis=0)


def _unpack_weights(group, gathered):
    return {key: gathered[:, first:first + r, :].reshape(N_DEV * r, PAYLOAD_COLS)
            for key, (first, r) in _group_rows(group).items()}


def _pack_grads(group, grads):
    pieces = []
    for n, l, r in GROUPS[group]:
        g = grads[n][l]
        parts = g if isinstance(g, tuple) else (g,)
        pieces.append(jnp.concatenate([p.reshape(-1, r, PAYLOAD_COLS) for p in parts], axis=0))
    if group == PW_GROUP:
        pieces.append(_to_shards("cv_pw_w", jnp.stack(grads["cv_pw_w"])).reshape(N_DEV, PW_ROWS, PAYLOAD_COLS))
    return jnp.concatenate(pieces, axis=1)


COL_SHARDED = ("w_in", "ffn_w_up", "cv_w", "ffn_conv_w")
SMALL_SHARDED = ("cv_w", "ffn_conv_w")
REPLICATED = ("mix_norm_pre", "cv_b", "cv_ln_g", "cv_ln_b", "cv_pw_b", "mix_norm_post", "x_norm_pre", "mem_norm",
              "x_norm_post", "ffn_norm_pre", "ffn_conv_b", "ffn_norm_post")
WEIGHTS = ("mix_norm_pre", "w_in", "cv_w", "cv_b", "cv_ln_g", "cv_ln_b", "cv_pw_w", "cv_pw_b", "w_out", "mix_norm_post",
           "x_norm_pre", "mem_norm", "x_wq", "x_wk", "x_wv", "x_wo", "x_norm_post", "ffn_norm_pre", "ffn_w_up",
           "ffn_conv_w", "ffn_conv_b", "ffn_w_down", "ffn_norm_post")
PAYLOAD_COLS = 1024


PAYLOAD_ROW_TILE = 16


def _pad_rows(flat, cols):
    n = flat.shape[-1]
    rows = -(-n // (cols * PAYLOAD_ROW_TILE)) * PAYLOAD_ROW_TILE
    pad = rows * cols - n
    if pad:
        flat = jnp.concatenate([flat, jnp.zeros(flat.shape[:-1] + (pad,), flat.dtype)], axis=-1)
    return flat.reshape(flat.shape[:-1] + (rows, cols))


def _unshard(name, parts):
    n, depth, r, c = parts.shape
    if name in COL_SHARDED:
        return parts.transpose(1, 2, 0, 3).reshape(depth, r, n * c)
    return parts.transpose(1, 0, 2, 3).reshape(depth, n * r, c)


def _to_shards(name, full):
    depth, r, c = full.shape
    if name in COL_SHARDED:
        return full.reshape(depth, r, N_DEV, c // N_DEV).transpose(2, 0, 1, 3).reshape(N_DEV, -1)
    return full.reshape(depth, N_DEV, r // N_DEV, c).transpose(1, 0, 2, 3).reshape(N_DEV, -1)


def _heads_major(x, h):
    return x.reshape(x.shape[0], h, HEAD_DIM).transpose(1, 0, 2)


def _tokens_major(x):
    return x.transpose(1, 0, 2).reshape(x.shape[1], -1)


def _ffn_halves(p):
    w, b = p["ffn_conv_w"], p["ffn_conv_b"]
    return w[:, :D_FF], w[:, D_FF:], b[:, :D_FF], b[:, D_FF:]


def _layer_fwd(l, h, hn, p, mem, cos, sin, g_next, payload, unpack, ffn_shards, ffn_payload, rope_payload=None,
               rope_unpack=None):
    p = dict(p)
    sv = {"h0": h, "hn0": hn}
    u = _mm(hn, p["w_in"], "nt", F32, f"l{l}_in_proj")
    sv["u"] = u
    sb = _heads_major(u[:, :3 * SB_WIDTH].astype(BF16), 3 * SB_HEADS)
    sb_q, sb_k, sb_v = sb[:SB_HEADS], sb[SB_HEADS:2 * SB_HEADS], sb[2 * SB_HEADS:]
    a_out, sb_tot, sb_first, gathered = _sb_fwd(sb_q, sb_k, sb_v, payload, f"l{l}_sb_fwd")
    p.update(unpack(gathered))
    sv.update(sb_q=sb_q, sb_k=sb_k, sb_v=sb_v, sb_tot=sb_tot, sb_first=sb_first, p=p)

    qk, *rope_gathered = _rope_fwd(u, cos, sin, f"l{l}_rope_fwd", rope_payload)
    if rope_gathered:
        p.update(rope_unpack(rope_gathered[0]))
        sv["rope_gathered"] = rope_gathered[0]
    cv_s, cv_c = _cv_fwd(u, p["cv_w"], p["cv_b"], p["cv_ln_g"], p["cv_ln_b"], f"l{l}_cv_fwd")
    b_out = _mm(cv_s, p["cv_pw_w"], "nn", BF16, f"l{l}_cv_pw", bias=p["cv_pw_b"])
    sv.update(cv_s=cv_s, cv_c=cv_c)

    up_t = ffn_shards[0].astype(BF16).T
    half_rows = up_t.shape[0] // 2
    carried = (up_t[:half_rows], up_t[half_rows:], ffn_shards[1].astype(BF16))
    outs, lses, got = [], [], []
    for b, (_, dil) in enumerate(DL_PATTERN):
        o, lse, gathered = _dl_fwd(qk, u, dil, f"l{l}_dl{b}_fwd", carried[b])
        outs.append(o)
        lses.append(lse)
        got.append(gathered)
    up_blocks = jnp.concatenate(got[:2], axis=1)
    half = N_DEV // 2
    p["ffn_w_up"] = (up_blocks[:half].reshape(-1, PAYLOAD_COLS), up_blocks[half:].reshape(-1, PAYLOAD_COLS))
    p["ffn_w_down"] = got[2].reshape(-1, PAYLOAD_COLS)
    c_out, c_out_f32, w1, w2, w3 = _dl_mix_fwd(outs, lses, f"l{l}_dl_mix")
    sv.update(dl_qk=qk, dl_lse=lses, dl_o=c_out_f32, dl_w=(w1, w2, w3))

    mix = jnp.concatenate([_tokens_major(a_out), b_out, c_out], axis=-1)
    y = _mm(mix, p["w_out"], "nn", F32, f"l{l}_out_proj")
    h1, hn1 = _res_norm_fwd(h, y, p["mix_norm_post"], p["x_norm_pre"], f"l{l}_mix_post")
    sv.update(mix=mix, y_mix=y, h1=h1, hn1=hn1)

    xq = _mm(hn1, p["x_wq"], "nn", BF16, f"l{l}_xq")
    memn = _rms_fwd(mem, p["mem_norm"], f"l{l}_mem_norm")
    xk = _mm(memn, p["x_wk"], "nn", BF16, f"l{l}_xk")
    xv = _mm(memn, p["x_wv"], "nn", BF16, f"l{l}_xv")
    xo = _xattn_fwd(xq, xk, xv, f"l{l}_xattn_fwd")
    y = _mm(xo, p["x_wo"], "nn", F32, f"l{l}_xo_proj")
    h2, hn2 = _res_norm_fwd(h1, y, p["x_norm_post"], p["ffn_norm_pre"], f"l{l}_x_post")
    sv.update(xq=xq, xk=xk, xv=xv, xo=xo, memn=memn, y_x=y, h2=h2, hn2=hn2)

    up_g = _mm(hn2, p["ffn_w_up"][0], "nt", F32, f"l{l}_ffn_up_gate")
    up_v = _mm(hn2, p["ffn_w_up"][1], "nt", F32, f"l{l}_ffn_up_val")
    act, *ffn_gathered = _ffn_act_fwd(up_g, up_v, *_ffn_halves(p), f"l{l}_ffn_act", ffn_payload)
    y = _mm(act, p["ffn_w_down"], "nn", F32, f"l{l}_ffn_down")
    h3, hn3 = _res_norm_fwd(h2, y, p["ffn_norm_post"], g_next, f"l{l}_ffn_post")
    sv.update(up_g=up_g, up_v=up_v, act=act, y_ffn=y)
    return h3, hn3, sv, (ffn_gathered[0] if ffn_gathered else None)


def _layer_bwd(l, dh, dy, sv, mem, cos, sin, prev_post, ffn_payload, pack):
    p = sv["p"]
    gr = {}
    received = {}
    dact = _mm(dy, p["ffn_w_down"], "nt", F32, f"l{l}_d_act")
    gr["ffn_w_down"] = _mm(sv["act"], dy, "tn", BF16, f"l{l}_dw_down")
    dup_g, dup_v, dwg, dwv, dbg, dbv, *got = _ffn_act_bwd(sv["up_g"], sv["up_v"], dact, *_ffn_halves(p), f"l{l}_ffn_act_bwd",
                                                         ffn_payload)
    if got:
        received["ffn_payload"] = got[0]
    gr["ffn_conv_w"] = jnp.concatenate([dwg, dwv], axis=1)
    gr["ffn_conv_b"] = jnp.concatenate([dbg, dbv], axis=1)
    dhn = (_mm(dup_g, p["ffn_w_up"][0], "nn", F32, f"l{l}_d_hn2_gate"), _mm(dup_v, p["ffn_w_up"][1], "nn", F32, f"l{l}_d_hn2_val"))
    gr["ffn_w_up"] = (_mm(dup_g, sv["hn2"], "tn", BF16, f"l{l}_dw_up_gate"), _mm(dup_v, sv["hn2"], "tn", BF16, f"l{l}_dw_up_val"))
    dh, dy, gr["ffn_norm_pre"], gr["x_norm_post"] = _norm_bwd(
        dh, (sv["h2"], p["ffn_norm_pre"], dhn), (sv["y_x"], p["x_norm_post"]), f"l{l}_x_post_bwd")

    do = _mm(dy, p["x_wo"], "nt", BF16, f"l{l}_d_xo")
    gr["x_wo"] = _mm(sv["xo"], dy, "tn", BF16, f"l{l}_dw_xo")
    dq, dk, dv = _xattn_bwd(sv["xq"], sv["xk"], sv["xv"], do, f"l{l}_xattn_bwd")
    dhn = _mm(dq, p["x_wq"], "nt", F32, f"l{l}_d_hn1")
    gr["x_wq"] = _mm(sv["hn1"], dq, "tn", BF16, f"l{l}_dw_xq")
    gr["x_wk"] = _mm(sv["memn"], dk, "tn", BF16, f"l{l}_dw_xk")
    gr["x_wv"] = _mm(sv["memn"], dv, "tn", BF16, f"l{l}_dw_xv")
    dmemn = _mm(dk, p["x_wk"], "nt", F32, f"l{l}_d_memn_k") + _mm(dv, p["x_wv"], "nt", F32, f"l{l}_d_memn_v")
    gr["mem_norm"] = _rms_gain_grad(mem, p["mem_norm"], dmemn, f"l{l}_mem_norm_bwd")
    dh, dy, gr["x_norm_pre"], gr["mix_norm_post"] = _norm_bwd(
        dh, (sv["h1"], p["x_norm_pre"], dhn), (sv["y_mix"], p["mix_norm_post"]), f"l{l}_mix_post_bwd")

    dmix = _mm(dy, p["w_out"], "nt", F32, f"l{l}_d_mix")
    gr["w_out"] = _mm(sv["mix"], dy, "tn", BF16, f"l{l}_dw_out")
    do_a = _heads_major(dmix[:, :SB_WIDTH].astype(BF16), SB_HEADS)
    dq, dk, dv, received["ffn"] = _sb_bwd(sv["sb_q"], sv["sb_k"], sv["sb_v"], do_a, sv["sb_tot"], sv["sb_first"],
                                          pack("ffn", gr), f"l{l}_sb_bwd")
    du_sb = _tokens_major(jnp.concatenate([dq, dk, dv], axis=0))

    db_out = dmix[:, SB_WIDTH:SB_WIDTH + CV_WIDTH]
    ds = _mm(db_out, p["cv_pw_w"], "nt", F32, f"l{l}_d_cv_s")
    gr["cv_pw_w"] = _mm(sv["cv_s"], db_out, "tn", BF16, f"l{l}_dw_cv_pw")
    du_cv, dcvw, gr["cv_b"], gr["cv_ln_g"], gr["cv_ln_b"], gr["cv_pw_b"], *got = _cv_bwd(
        sv["u"], sv["cv_c"], ds, db_out, p["cv_w"], p["cv_ln_g"], p["cv_ln_b"], f"l{l}_cv_bwd", pack("out", gr))
    if got:
        received["out"] = got[0]
    gr["cv_w"] = dcvw[:CV_KERNEL]

    dqs, dks, dvs = [], [], []
    up_rows = jnp.concatenate([g.reshape(N_DEV // 2, -1, PAYLOAD_COLS) for g in gr["ffn_w_up"]], axis=0)
    half_rows = up_rows.shape[1] // 2
    carried = {"up0": up_rows[:, :half_rows], "up1": up_rows[:, half_rows:], "cross": pack("cross", gr)}
    for b, ((_, dil), what) in enumerate(zip(DL_PATTERN, carried)):
        dq, dk, dv, received[what] = _dl_bwd(sv["dl_qk"], sv["u"], dmix, sv["dl_o"], sv["dl_w"][b], sv["dl_lse"][b], dil,
                                             f"l{l}_dl{b}_bwd", carried[what])
        dqs.append(dq)
        dks.append(dk)
        dvs.append(dv)
    du_dl = _rope_bwd(dqs, dks, dvs, cos, sin, f"l{l}_rope_bwd")

    du = jnp.concatenate([du_sb, du_cv, du_dl], axis=-1)
    dhn = _mm(du, p["w_in"], "nn", F32, f"l{l}_d_hn0")
    gr["w_in"] = _mm(du, sv["hn0"], "tn", BF16, f"l{l}_dw_in")
    dh, dy, gr["mix_norm_pre"], dg_prev = _norm_bwd(dh, (sv["h0"], p["mix_norm_pre"], dhn), prev_post, f"l{l}_in_bwd")
    return dh, dy, gr, dg_prev, received


def kernel(x, mem, positions, mix_norm_pre, w_in, cv_w, cv_b, cv_ln_g, cv_ln_b, cv_pw_w, cv_pw_b, w_out, mix_norm_post, x_norm_pre, mem_norm, x_wq, x_wk, x_wv, x_wo, x_norm_post, ffn_norm_pre, ffn_w_up, ffn_conv_w, ffn_conv_b, ffn_w_down, ffn_norm_post, loss_target, m_mix_norm_pre, m_w_in, m_cv_w, m_cv_b, m_cv_ln_g, m_cv_ln_b, m_cv_pw_w, m_cv_pw_b, m_w_out, m_mix_norm_post, m_x_norm_pre, m_mem_norm, m_x_wq, m_x_wk, m_x_wv, m_x_wo, m_x_norm_post, m_ffn_norm_pre, m_ffn_w_up, m_ffn_conv_w, m_ffn_conv_b, m_ffn_w_down, m_ffn_norm_post, v_mix_norm_pre, v_w_in, v_cv_w, v_cv_b, v_cv_ln_g, v_cv_ln_b, v_cv_pw_w, v_cv_pw_b, v_w_out, v_mix_norm_post, v_x_norm_pre, v_mem_norm, v_x_wq, v_x_wk, v_x_wv, v_x_wo, v_x_norm_post, v_ffn_norm_pre, v_ffn_w_up, v_ffn_conv_w, v_ffn_conv_b, v_ffn_w_down, v_ffn_norm_post):
    args = locals()
    wts = {n: args[n] for n in WEIGHTS}
    mom = {n: args["m_" + n] for n in WEIGHTS}
    var = {n: args["v_" + n] for n in WEIGHTS}

    x2, mem2, target = x[0], mem[0], loss_target[0]

    small_payload = _pad_rows(jnp.concatenate([wts[n].reshape(-1) for n in SMALL_SHARDED]), PAYLOAD_COLS)
    gathered_a, small = _all_gather([_pack_weights("a", wts), small_payload], "weights_all_gather")
    small = small.reshape(N_DEV, -1)
    small_full = {}
    off = 0
    for n in SMALL_SHARDED:
        size = wts[n].size
        small_full[n] = _unshard(n, small[:, off:off + size].reshape((N_DEV,) + wts[n].shape))
        off += size
    pw_first = sum(r for _, _, r in GROUPS[PW_GROUP])

    def pw_of(gathered_pw_group):
        return _unshard("cv_pw_w", gathered_pw_group[:, pw_first:, :].reshape((N_DEV,) + wts["cv_pw_w"].shape))

    def small_params(l):
        p = {n: wts[n][l][None, :] for n in REPLICATED}
        p.update({n: small_full[n][l] for n in SMALL_SHARDED})
        return p

    def out_and_pw(gathered_pw_group):
        return {"w_out": _unpack_weights(PW_GROUP, gathered_pw_group)[("w_out", 0)], "cv_pw_w": pw_of(gathered_pw_group)[0]}

    def of_layer(group, l):
        return lambda gathered: {n: w for (n, ll), w in _unpack_weights(group, gathered).items() if ll == l}

    pos = positions[0].astype(F32)
    half = HEAD_DIM // 2
    inv_freq = ROPE_THETA ** (-jnp.arange(half, dtype=F32) / half)
    ang = pos[:, None] * inv_freq
    cos = jnp.tile(jnp.cos(ang), (1, LANES // half))
    sin = jnp.tile(jnp.sin(ang), (1, LANES // half))

    p0 = small_params(0)
    p0["w_in"] = _unpack_weights("a", gathered_a)[("w_in", 0)]
    hn = _rms_fwd(x2, p0["mix_norm_pre"], "l0_in_norm")
    ffn_shards = [(wts["ffn_w_up"][l], wts["ffn_w_down"][l]) for l in range(DEPTH)]
    h, hn, sv0, gathered_m1 = _layer_fwd(0, x2, hn, p0, mem2, cos, sin, wts["mix_norm_pre"][1][None, :],
                                         _pack_weights("bx", wts), of_layer("bx", 0), ffn_shards[0], _pack_weights("m1", wts),
                                         _pack_weights(PW_GROUP, wts), out_and_pw)
    p1 = small_params(1)
    mixer1 = _unpack_weights("m1", gathered_m1)
    p1.update(w_in=mixer1[("w_in", 1)], w_out=mixer1[("w_out", 1)], cv_pw_w=pw_of(sv0["rope_gathered"])[1])
    h, _, sv1, _ = _layer_fwd(1, h, hn, p1, mem2, cos, sin, None, _pack_weights("cx", wts), of_layer("cx", 1), ffn_shards[1],
                              None)
    loss_part, dh = _loss_fwd(h, target, "loss")
    loss = lax.psum(loss_part[0, 0], ("x", "y", "c"))

    grads = {n: [None] * DEPTH for n in WEIGHTS}
    dh, dy, _, grads["ffn_norm_post"][1] = _norm_bwd(dh, None, (sv1["y_ffn"], sv1["p"]["ffn_norm_post"]), "last_post_bwd")

    def packer(l, groups):
        def pack(which, gr):
            if which not in groups:
                return None
            if groups[which] == PW_GROUP:
                return _pack_grads(PW_GROUP, {"w_out": {0: gr["w_out"]}, "cv_pw_w": [gr["cv_pw_w"], grads["cv_pw_w"][1]]})
            return _pack_grads(groups[which], {n: {l: g} for n, g in gr.items()})
        return pack

    dh, dy, gr, grads["ffn_norm_post"][0], got1 = _layer_bwd(
        1, dh, dy, sv1, mem2, cos, sin, (sv0["y_ffn"], sv0["p"]["ffn_norm_post"]), None, packer(1, {"ffn": "cf", "cross": "cx"}))
    for n, g in gr.items():
        grads[n][1] = g
    dh, _, gr, _, got0 = _layer_bwd(0, dh, dy, sv0, mem2, cos, sin, None, _pack_grads("m1", grads),
                                    packer(0, {"ffn": "bf", "cross": "bx", "out": PW_GROUP}))
    for n, g in gr.items():
        grads[n][0] = g
    grad_x = dh

    small_rows = jnp.concatenate([_to_shards(n, jnp.stack(grads[n])) for n in SMALL_SHARDED], axis=1)
    rep_flat = jnp.concatenate([jnp.stack([g.reshape(-1) for g in grads[n]]).reshape(-1) for n in REPLICATED])
    rep_rows = jnp.broadcast_to(rep_flat[None], (N_DEV, rep_flat.shape[0]))
    f32_rows = _pad_rows(jnp.concatenate([small_rows, rep_rows], axis=1), PAYLOAD_COLS)
    received_a, small_parts = _all_to_all([_pack_grads("a", grads), f32_rows], "grads_all_to_all")
    received = {"a": received_a, PW_GROUP: got0["out"], "m1": got0["ffn_payload"], "bf": got0["ffn"], "bx": got0["cross"],
                "cf": got1["ffn"], "cx": got1["cross"]}

    res = {}
    for n in BIG:
        shape = wts[n].shape
        two_d = (shape[0] * shape[1], shape[2])
        operands = (wts[n].reshape(two_d), mom[n].reshape(two_d), var[n].reshape(two_d))
        if n == "cv_pw_w":
            outs = _adamw(received[PW_GROUP][:, pw_first:, :].reshape((N_DEV,) + two_d), *operands, f"adamw_{n}")
        elif n in TRANSPOSED:
            layers = []
            for l, got in enumerate((got0, got1)):
                if n == "ffn_w_up":
                    layers.append(jnp.concatenate([got["up0"], got["up1"]], axis=1))
                else:
                    group, first, r = _where_is(n, l)
                    layers.append(received[group][:, first:first + r, :])
            parts = jnp.stack(layers, axis=1).transpose(0, 1, 3, 2).reshape((N_DEV,) + two_d)
            outs = _adamw(parts, *operands, f"adamw_{n}")
        else:
            sources = []
            for l in range(DEPTH):
                group, first, _ = _where_is(n, l)
                sources.append((received[group], first))
            outs = _adamw_packed(sources, *operands, f"adamw_{n}")
        res[n] = [o.reshape(shape) for o in outs]
    small_names = SMALL_SHARDED + REPLICATED
    flat_w = _pad_rows(jnp.concatenate([wts[n].reshape(-1) for n in small_names]), PAYLOAD_COLS)
    flat_m = _pad_rows(jnp.concatenate([mom[n].reshape(-1) for n in small_names]), PAYLOAD_COLS)
    flat_v = _pad_rows(jnp.concatenate([var[n].reshape(-1) for n in small_names]), PAYLOAD_COLS)
    outs = _adamw(small_parts, flat_w, flat_m, flat_v, "adamw_small")
    outs = [o.reshape(-1) for o in outs]
    off = 0
    for n in small_names:
        size = wts[n].size
        res[n] = [o[off:off + size].reshape(wts[n].shape) for o in outs]
        off += size

    result = [loss, grad_x[None]]
    for kind in range(4):
        result += [res[n][kind] for n in WEIGHTS]
    return tuple(result)
```

```python
import functools
import math

import jax
import jax.numpy as jnp
from jax import lax
from jax.experimental import pallas as pl
from jax.experimental.pallas import tpu as pltpu

F32, BF16 = jnp.float32, jnp.bfloat16
SDS = jax.ShapeDtypeStruct

D_MODEL = 1024
SEQ = 4096
DEPTH = 2
HEAD_DIM = 64
SB_HEADS = 4
SB_WIDTH = 256
CV_WIDTH = 256
CV_KERNEL = 31
DL_HEADS = 8
DL_WIDTH = 512
IN_WIDTH = 2816
DL_PATTERN = ((128, 1), (512, 4), (2048, 16))
BLOCK = 128
ROPE_THETA = 10000.0
N_MEM = 256
X_HEADS = 4
X_HEAD_DIM = 256
D_FF = 2816
EPS = 1e-6
N_DEV = 8
LANES = 128

ADAM_LR = 0.001
ADAM_B1 = 0.9
ADAM_B2 = 0.999
ADAM_EPS = 1e-08
ADAM_WD = 0.01
ADAM_STEP = 10

VMEM_LIMIT_BYTES = 56 * 1024 * 1024
MESH = pl.DeviceIdType.MESH
NEG = -1e30


def _params(**kw):
    return pltpu.CompilerParams(vmem_limit_bytes=VMEM_LIMIT_BYTES, **kw)


def _pick(n, cands):
    for c in cands:
        if n % c == 0:
            return c
    return n


def _mm(a, b, mode, out_dtype, name, bias=None):
    if mode == "nn":
        (m, k), (k2, n) = a.shape, b.shape
    elif mode == "nt":
        (m, k), (n, k2) = a.shape, b.shape
    else:
        (k, m), (k2, n) = a.shape, b.shape
    assert k == k2, (a.shape, b.shape, mode)
    tm = _pick(m, (1024, 1408, 512, 256, 128))
    tn = _pick(n, (1024, 1408, 512, 256, 128))
    tk = k if k <= 2048 else _pick(k, (2048, 1408, 1024, 512))
    nk = k // tk
    dims = {"nn": ((1,), (0,)), "nt": ((1,), (1,)), "tn": ((0,), (0,))}[mode]

    def body(*refs):
        refs = list(refs)
        acc_ref = refs.pop() if nk > 1 else None
        a_ref, b_ref = refs[0], refs[1]
        bias_ref = refs[2] if bias is not None else None
        o_ref = refs[-1]
        p = lax.dot_general(a_ref[...].astype(BF16), b_ref[...].astype(BF16), (dims, ((), ())),
                            preferred_element_type=F32)

        def finish(v):
            if bias_ref is not None:
                v = v + bias_ref[...]
            o_ref[...] = v.astype(out_dtype)

        if nk == 1:
            finish(p)
        else:
            kk = pl.program_id(2)

            @pl.when(kk == 0)
            def _():
                acc_ref[...] = p

            @pl.when(kk > 0)
            def _():
                acc_ref[...] += p

            @pl.when(kk == nk - 1)
            def _():
                finish(acc_ref[...])

    a_spec = pl.BlockSpec((tk, tm), lambda i, j, kk: (kk, i)) if mode == "tn" else pl.BlockSpec((tm, tk), lambda i, j, kk: (i, kk))
    b_spec = pl.BlockSpec((tn, tk), lambda i, j, kk: (j, kk)) if mode == "nt" else pl.BlockSpec((tk, tn), lambda i, j, kk: (kk, j))
    in_specs = [a_spec, b_spec]
    args = [a, b]
    if bias is not None:
        in_specs.append(pl.BlockSpec((1, tn), lambda i, j, kk: (0, j)))
        args.append(bias)
    return pl.pallas_call(
        body, name=name, out_shape=SDS((m, n), out_dtype), grid=(m // tm, n // tn, nk),
        in_specs=in_specs, out_specs=pl.BlockSpec((tm, tn), lambda i, j, kk: (i, j)),
        scratch_shapes=[pltpu.VMEM((tm, tn), F32)] if nk > 1 else [], compiler_params=_params(),
    )(*args)


def _rms(x, g):
    r = lax.rsqrt(jnp.mean(x * x, axis=-1, keepdims=True) + EPS)
    return x * r * g


def _rms_bwd(x, g, dy):
    r = lax.rsqrt(jnp.mean(x * x, axis=-1, keepdims=True) + EPS)
    xh = x * r
    dyg = dy * g
    dx = r * (dyg - xh * jnp.mean(dyg * xh, axis=-1, keepdims=True))
    return dx, dy * xh


def _rms_fwd(x, g, name):
    rows, d = x.shape
    t = min(rows, 512)

    def body(x_ref, g_ref, o_ref):
        o_ref[...] = _rms(x_ref[...], g_ref[...]).astype(BF16)

    return pl.pallas_call(
        body, name=name, out_shape=SDS((rows, d), BF16), grid=(rows // t,),
        in_specs=[pl.BlockSpec((t, d), lambda i: (i, 0)), pl.BlockSpec((1, d), lambda i: (0, 0))],
        out_specs=pl.BlockSpec((t, d), lambda i: (i, 0)), compiler_params=_params(),
    )(x, g)


def _res_norm_fwd(h, y, g_post, g_next, name):
    rows, d = h.shape
    t = 512
    has_next = g_next is not None

    def body(*refs):
        if has_next:
            h_ref, y_ref, gp_ref, gn_ref, h1_ref, hn_ref = refs
        else:
            h_ref, y_ref, gp_ref, h1_ref = refs
        h1 = h_ref[...] + _rms(y_ref[...], gp_ref[...])
        h1_ref[...] = h1
        if has_next:
            hn_ref[...] = _rms(h1, gn_ref[...]).astype(BF16)

    row = pl.BlockSpec((t, d), lambda i: (i, 0))
    vec = pl.BlockSpec((1, d), lambda i: (0, 0))
    in_specs = [row, row, vec] + ([vec] if has_next else [])
    args = [h, y, g_post] + ([g_next] if has_next else [])
    out_shape = [SDS((rows, d), F32)] + ([SDS((rows, d), BF16)] if has_next else [])
    out_specs = [row] + ([row] if has_next else [])
    res = pl.pallas_call(body, name=name, out_shape=out_shape, grid=(rows // t,), in_specs=in_specs,
                         out_specs=out_specs, compiler_params=_params())(*args)
    return (res[0], res[1]) if has_next else (res[0], None)


def _norm_bwd(dh, pre, post, name):
    rows, d = dh.shape
    t = 512
    has_pre, has_post = pre is not None, post is not None
    if has_pre:
        dhns = pre[2] if isinstance(pre[2], tuple) else (pre[2],)
        pre = (pre[0], pre[1]) + dhns

    def body(*refs):
        refs = list(refs)
        dh_ref = refs.pop(0)
        if has_pre:
            h_ref, gpre_ref = refs.pop(0), refs.pop(0)
            dhn_refs = [refs.pop(0) for _ in dhns]
        if has_post:
            y_ref, gpost_ref = refs.pop(0), refs.pop(0)
        dht_ref = refs.pop(0)
        if has_post:
            dy_ref = refs.pop(0)
        if has_pre:
            dgpre_ref = refs.pop(0)
        if has_post:
            dgpost_ref = refs.pop(0)
        i = pl.program_id(0)
        dht = dh_ref[...]
        if has_pre:
            dhn = dhn_refs[0][...]
            for r in dhn_refs[1:]:
                dhn = dhn + r[...]
            dx, dgr = _rms_bwd(h_ref[...], gpre_ref[...], dhn)
            dht = dht + dx

            @pl.when(i == 0)
            def _():
                dgpre_ref[...] = jnp.zeros_like(dgpre_ref)

            dgpre_ref[...] += jnp.sum(dgr, axis=0, keepdims=True)
        dht_ref[...] = dht
        if has_post:
            dy, dgr = _rms_bwd(y_ref[...], gpost_ref[...], dht)
            dy_ref[...] = dy.astype(BF16)

            @pl.when(i == 0)
            def _():
                dgpost_ref[...] = jnp.zeros_like(dgpost_ref)

            dgpost_ref[...] += jnp.sum(dgr, axis=0, keepdims=True)

    row = pl.BlockSpec((t, d), lambda i: (i, 0))
    vec = pl.BlockSpec((1, d), lambda i: (0, 0))
    in_specs, args = [row], [dh]
    if has_pre:
        in_specs += [row, vec] + [row] * len(dhns)
        args += list(pre)
    if has_post:
        in_specs += [row, vec]
        args += list(post)
    out_shape, out_specs = [SDS((rows, d), F32)], [row]
    if has_post:
        out_shape.append(SDS((rows, d), BF16))
        out_specs.append(row)
    if has_pre:
        out_shape.append(SDS((1, d), F32))
        out_specs.append(vec)
    if has_post:
        out_shape.append(SDS((1, d), F32))
        out_specs.append(vec)
    res = list(pl.pallas_call(body, name=name, out_shape=out_shape, grid=(rows // t,), in_specs=in_specs,
                              out_specs=out_specs, compiler_params=_params())(*args))
    dht = res.pop(0)
    dy = res.pop(0) if has_post else None
    dgpre = res.pop(0) if has_pre else None
    dgpost = res.pop(0) if has_post else None
    return dht, dy, dgpre, dgpost


def _rms_gain_grad(x, g, dy, name):
    rows, d = x.shape

    def body(x_ref, g_ref, dy_ref, dg_ref):
        _, dgr = _rms_bwd(x_ref[...], g_ref[...], dy_ref[...])
        dg_ref[...] = jnp.sum(dgr, axis=0, keepdims=True)

    return pl.pallas_call(body, name=name, out_shape=SDS((1, d), F32), compiler_params=_params())(x, g, dy)


def _loss_fwd(h, target, name):
    rows, d = h.shape
    t = 512

    def body(h_ref, t_ref, loss_ref, dh_ref):
        i = pl.program_id(0)
        err = h_ref[...] - t_ref[...]
        dh_ref[...] = err * (1.0 / d)

        @pl.when(i == 0)
        def _():
            loss_ref[...] = jnp.zeros_like(loss_ref)

        part = jnp.sum(jnp.sum(err * err, axis=1, keepdims=True), axis=0, keepdims=True) * (0.5 / d)
        loss_ref[...] += jnp.broadcast_to(part, loss_ref.shape)

    row = pl.BlockSpec((t, d), lambda i: (i, 0))
    return pl.pallas_call(
        body, name=name, out_shape=(SDS((1, LANES), F32), SDS((rows, d), F32)), grid=(rows // t,),
        in_specs=[row, row], out_specs=(pl.BlockSpec((1, LANES), lambda i: (0, 0)), row), compiler_params=_params(),
    )(h, target)


def _rot_half(x, sign):
    w = x.shape[-1]
    lane = lax.broadcasted_iota(jnp.int32, x.shape, 1)
    first = (lane % HEAD_DIM) < (HEAD_DIM // 2)
    return jnp.where(first, -sign * pltpu.roll(x, w - HEAD_DIM // 2, axis=1), sign * pltpu.roll(x, HEAD_DIM // 2, axis=1))


def _rope_fwd(u, cos, sin, name, payload=None):
    rows = u.shape[0]
    t, cw = 512, 256
    first_col = (3 * SB_WIDTH + 2 * CV_WIDTH) // cw

    def body(u_ref, c_ref, s_ref, o_ref):
        x = u_ref[...]
        c = jnp.tile(c_ref[...], (1, cw // LANES))
        s = jnp.tile(s_ref[...], (1, cw // LANES))
        o_ref[...] = x * c + _rot_half(x, 1.0) * s

    tab = pl.BlockSpec((t, LANES), lambda i, j: (i, 0))
    grid = (rows // t, 2 * DL_WIDTH // cw)
    if payload is not None:
        def when():
            first = (pl.program_id(0) == 0) & (pl.program_id(1) == 0)
            last = (pl.program_id(0) == grid[0] - 1) & (pl.program_id(1) == grid[1] - 1)
            return first, last, last
        payload = ("gather", payload, when)
    return _hosted_call(
        body, payload, name=name, out_shape=(SDS((rows, 2 * DL_WIDTH), F32),), grid=grid,
        in_specs=[pl.BlockSpec((t, cw), lambda i, j: (i, first_col + j)), tab, tab],
        out_specs=(pl.BlockSpec((t, cw), lambda i, j: (i, j)),), args=(u, cos, sin))


def _rope_bwd(dqs, dks, dvs, cos, sin, name):
    rows = dqs[0].shape[0]
    t, w = 256, DL_WIDTH

    def body(*refs):
        c = jnp.tile(refs[9][...], (1, w // LANES))
        s = jnp.tile(refs[10][...], (1, w // LANES))
        o_ref = refs[11]
        dq = refs[0][...] + refs[1][...] + refs[2][...]
        dk = refs[3][...] + refs[4][...] + refs[5][...]
        dv = refs[6][...] + refs[7][...] + refs[8][...]
        o_ref[:, 0:w] = (dq * c + _rot_half(dq, -1.0) * s).astype(BF16)
        o_ref[:, w:2 * w] = (dk * c + _rot_half(dk, -1.0) * s).astype(BF16)
        o_ref[:, 2 * w:3 * w] = dv.astype(BF16)

    row = pl.BlockSpec((t, w), lambda i: (i, 0))
    tab = pl.BlockSpec((t, LANES), lambda i: (i, 0))
    return pl.pallas_call(
        body, name=name, out_shape=SDS((rows, 3 * w), BF16), grid=(rows // t,), in_specs=[row] * 9 + [tab, tab],
        out_specs=pl.BlockSpec((t, 3 * w), lambda i: (i, 0)), compiler_params=_params(),
    )(*dqs, *dks, *dvs, cos, sin)


SB_TILE = 256
SB_ZERO_AFTER = 110.0
SB_FIRST_BLOCK = (8, LANES)


def _softplus(z):
    return jnp.maximum(z, 0.0) + jnp.log(1.0 + jnp.exp(-jnp.abs(z)))


def _split_dot(x, tri, passes):
    acc = None
    rem = x
    for _ in range(passes):
        part = rem.astype(BF16)
        rem = rem - part.astype(F32)
        d = jnp.dot(part, tri, preferred_element_type=F32)
        acc = d if acc is None else acc + d
    return acc


def _tri(t, rel):
    j = lax.broadcasted_iota(jnp.int32, (t, t), 0)
    s = lax.broadcasted_iota(jnp.int32, (t, t), 1)
    return rel(j, s).astype(BF16)


def _sb_masks(t, i):
    row = lax.broadcasted_iota(jnp.int32, (t, t), 0)
    col = lax.broadcasted_iota(jnp.int32, (t, t), 1)
    return col < row, (row >= 0) & (i >= 1)


def _sb_fwd(q, k, v, payload, name):
    h, s_len, hd = q.shape
    t = SB_TILE
    nq = s_len // t
    scale = hd ** -0.5

    def body(q_ref, k_ref, v_ref, pay_ref, o_ref, tot_ref, first_ref, gathered_ref, send_sems, recv_sems, local_sem):
        hh, i = pl.program_id(0), pl.program_id(1)
        start, forward, finish = _gather_steps(pay_ref, gathered_ref, send_sems, recv_sems, local_sem)
        pl.when((hh == 0) & (i == 0))(start)
        pl.when((hh == h - 1) & (i == nq - 1))(forward)
        qv = q_ref[0] * scale
        upper = _tri(t, lambda j, s: j > s)

        def tiles(js, carry, masks=(None, None)):
            acc, run = carry
            starts = [pl.multiple_of(j * t, t) for j in js]
            zs = [lax.dot_general(qv, k_ref[0, pl.ds(st, t), :], (((1,), (1,)), ((), ())), preferred_element_type=F32)
                  for st in starts]
            sps = [_softplus(z) for z in zs]
            sps = [sp if m is None else jnp.where(m, sp, 0.0) for sp, m in zip(sps, masks)]
            laters = [_split_dot(sp, upper, 2) for sp in sps]
            for st, z, sp, later, m in zip(starts, zs, sps, laters, masks):
                a = jnp.exp((z - sp) - (run + later))
                if m is not None:
                    a = jnp.where(m, a, 0.0)
                acc = acc + jnp.dot(a.astype(BF16), v_ref[0, pl.ds(st, t), :], preferred_element_type=F32)
                run = run + jnp.sum(sp, axis=1, keepdims=True)
            return acc, run

        def live(carry):
            return jnp.min(carry[1]) < SB_ZERO_AFTER

        below, whole = _sb_masks(t, i)
        top = jnp.maximum(i - 1, 0)
        carry = tiles([i, top], (jnp.zeros((t, hd), F32), jnp.zeros((t, 1), F32)), (below, whole))

        def pair(state):
            pp, carry = state
            j = top - 1 - 2 * pp
            return pp + 1, tiles([j, j - 1], carry)

        pairs, carry = lax.while_loop(lambda st: (st[0] < top // 2) & live(st[1]), pair, (0, carry))
        last = ((top % 2 == 1) & (pairs == top // 2) & live(carry)).astype(jnp.int32)
        acc, run = lax.fori_loop(0, last, lambda _, c: tiles([0], c), carry)
        o_ref[0] = acc.astype(BF16)
        tot_ref[0] = run
        first_ref[...] = jnp.full(first_ref.shape, top - 2 * pairs - last, jnp.int32).astype(F32)
        pl.when((hh == h - 1) & (i == nq - 1))(finish)

    full = pl.BlockSpec((1, s_len, hd), lambda hh, i: (hh, 0, 0))
    tile = pl.BlockSpec((1, t, hd), lambda hh, i: (hh, i, 0))
    hbm = pl.BlockSpec(memory_space=pl.ANY)
    return pl.pallas_call(
        body, name=name,
        out_shape=(SDS((h, s_len, hd), BF16), SDS((h, s_len, 1), F32), SDS((h, nq) + SB_FIRST_BLOCK, F32),
                   SDS((N_DEV,) + payload.shape, payload.dtype)),
        grid=(h, nq), in_specs=[tile, full, full, hbm],
        out_specs=(tile, pl.BlockSpec((1, t, 1), lambda hh, i: (hh, i, 0)),
                   pl.BlockSpec((1, 1) + SB_FIRST_BLOCK, lambda hh, i: (hh, i, 0, 0)), hbm),
        scratch_shapes=_COMM_SEMAPHORES, compiler_params=_params(has_side_effects=True),
    )(q, k, v, payload)


def _sb_bwd(q, k, v, do, tot, first, payload, name):
    h, s_len, hd = q.shape
    t = SB_TILE
    nq = s_len // t
    scale = hd ** -0.5

    def body(q_ref, k_ref, v_ref, do_ref, tot_ref, first_ref, pay_ref, dq_ref, dk_ref, dv_ref, received_ref, dk_acc, dv_acc,
             send_sems, recv_sems, local_sem):
        hh, i = pl.program_id(0), pl.program_id(1)
        start, finish = _exchange_steps(pay_ref, received_ref, send_sems, recv_sems, local_sem)
        pl.when((hh == 0) & (i == 0))(start)

        @pl.when(i == 0)
        def _():
            dk_acc[...] = jnp.zeros_like(dk_acc)
            dv_acc[...] = jnp.zeros_like(dv_acc)

        qv = q_ref[0] * scale
        dov = do_ref[0]
        total = tot_ref[0]
        upto = _tri(t, lambda j, s: j <= s)
        before = _tri(t, lambda j, s: j < s)
        nt_dims = (((1,), (1,)), ((), ()))
        tn_dims = (((0,), (0,)), ((), ()))

        def tiles(js, carry, masks=(None, None)):
            dq, run_sp, run_g = carry
            starts = [pl.multiple_of(j * t, t) for j in js]
            zs = [lax.dot_general(qv, k_ref[0, pl.ds(st, t), :], nt_dims, preferred_element_type=F32) for st in starts]
            das = [lax.dot_general(dov, v_ref[0, pl.ds(st, t), :], nt_dims, preferred_element_type=F32) for st in starts]
            sps = [_softplus(z) for z in zs]
            log_sigs = [z - sp for z, sp in zip(zs, sps)]
            sps = [sp if m is None else jnp.where(m, sp, 0.0) for sp, m in zip(sps, masks)]
            pres = [_split_dot(sp, upto, 2) for sp in sps]
            a_s, gs = [], []
            for sp, log_sig, pre, da, m in zip(sps, log_sigs, pres, das, masks):
                a = jnp.exp(log_sig - (total - (run_sp + pre)))
                if m is not None:
                    a = jnp.where(m, a, 0.0)
                a_s.append(a)
                gs.append(a * da)
                run_sp = run_sp + jnp.sum(sp, axis=1, keepdims=True)
            g_pres = [_split_dot(g, before, 3) for g in gs]
            for st, a, g, g_pre, log_sig, m in zip(starts, a_s, gs, g_pres, log_sigs, masks):
                sig = jnp.exp(log_sig)
                dz = g * (1.0 - sig) - sig * (run_g + g_pre)
                if m is not None:
                    dz = jnp.where(m, dz, 0.0)
                dz = dz.astype(BF16)
                dq = dq + jnp.dot(dz, k_ref[0, pl.ds(st, t), :], preferred_element_type=F32)
                dk_acc[pl.ds(st, t), :] += lax.dot_general(dz, qv, tn_dims, preferred_element_type=F32)
                dv_acc[pl.ds(st, t), :] += lax.dot_general(a.astype(BF16), dov, tn_dims, preferred_element_type=F32)
                run_g = run_g + jnp.sum(g, axis=1, keepdims=True)
            return dq, run_sp, run_g

        zero = jnp.zeros((t, 1), F32)
        top = jnp.maximum(i - 1, 0)
        first = jnp.clip(first_ref[0, 0, 0, 0].astype(jnp.int32), 0, top)
        count = top - first
        carry = lax.fori_loop(0, count // 2, lambda pp, c: tiles([first + 2 * pp, first + 2 * pp + 1], c),
                              (jnp.zeros((t, hd), F32), zero, zero))
        carry = lax.fori_loop(0, count % 2, lambda _, c: tiles([top - 1], c), carry)
        below, whole = _sb_masks(t, i)
        dq, _, _ = tiles([top, i], carry, (whole, below))
        dq_ref[0] = (dq * scale).astype(BF16)

        @pl.when(i == nq - 1)
        def _():
            dk_ref[0] = dk_acc[...].astype(BF16)
            dv_ref[0] = dv_acc[...].astype(BF16)

        pl.when((hh == h - 1) & (i == nq - 1))(finish)

    full = pl.BlockSpec((1, s_len, hd), lambda hh, i: (hh, 0, 0))
    tile = pl.BlockSpec((1, t, hd), lambda hh, i: (hh, i, 0))
    hbm = pl.BlockSpec(memory_space=pl.ANY)
    out = SDS((h, s_len, hd), BF16)
    return pl.pallas_call(
        body, name=name, out_shape=(out, out, out, SDS(payload.shape, payload.dtype)), grid=(h, nq),
        in_specs=[tile, full, full, tile, pl.BlockSpec((1, t, 1), lambda hh, i: (hh, i, 0)),
                  pl.BlockSpec((1, 1) + SB_FIRST_BLOCK, lambda hh, i: (hh, i, 0, 0)), hbm],
        out_specs=(tile, full, full, hbm),
        scratch_shapes=[pltpu.VMEM((s_len, hd), F32), pltpu.VMEM((s_len, hd), F32)] + _COMM_SEMAPHORES,
        compiler_params=_params(has_side_effects=True),
    )(q, k, v, do, tot, first, payload)


def _dl_scores(qv, kk, n):
    s = lax.dot_general(qv, kk, (((1,), (1,)), ((), ())), preferred_element_type=F32) * (HEAD_DIM ** -0.5)
    r = lax.broadcasted_iota(jnp.int32, s.shape, 0)
    c = lax.broadcasted_iota(jnp.int32, s.shape, 1)
    valid = (c >= r) & (c - r <= BLOCK) & ((n > 0) | (c >= BLOCK))
    return jnp.where(valid, s, NEG)


DL_UNROLL = 8
DL_FWD_UNROLL = 8
DL_PAIR = 2 * HEAD_DIM
DL_Q_BLOCK0 = 0
DL_K_BLOCK0 = DL_WIDTH // DL_PAIR
DL_V_BLOCK0 = (IN_WIDTH - DL_WIDTH) // DL_PAIR
DL_DO_BLOCK0 = (SB_WIDTH + CV_WIDTH) // DL_PAIR


def _dl_rows(idx, nb, dil):
    r, n = idx // nb, idx % nb
    cur = pl.ds(r + n * (BLOCK * dil), BLOCK, stride=dil)
    prev = pl.ds(r + jnp.maximum(n - 1, 0) * (BLOCK * dil), BLOCK, stride=dil)
    return n, cur, prev


def _dl_window(ref, cur, prev):
    return jnp.concatenate([ref[prev, :], ref[cur, :]], axis=0).astype(BF16)


def _head_lanes():
    first = lax.broadcasted_iota(jnp.int32, (BLOCK, DL_PAIR), 1) < HEAD_DIM
    return first, jnp.logical_not(first)


def _dl_fwd(qk, u, dil, name, payload=None):
    s_len = qk.shape[0]
    nb = s_len // dil // BLOCK

    def body(q_ref, k_ref, v_ref, o_ref, lse_ref):
        heads = _head_lanes()

        def step(idx, _):
            n, cur, prev = _dl_rows(idx, nb, dil)
            q = q_ref[cur, :]
            kk = _dl_window(k_ref, cur, prev)
            vv = _dl_window(v_ref, cur, prev)
            o, lse = None, None
            for lanes in heads:
                s = _dl_scores(jnp.where(lanes, q, 0.0).astype(BF16), kk, n)
                m = jnp.max(s, axis=-1, keepdims=True)
                p = jnp.exp(s - m)
                den = jnp.sum(p, axis=-1, keepdims=True)
                o_h = jnp.dot((p / den).astype(BF16), vv, preferred_element_type=F32)
                lse_h = jnp.broadcast_to(m + jnp.log(den), (BLOCK, DL_PAIR))
                o = o_h if o is None else jnp.where(heads[0], o, o_h)
                lse = lse_h if lse is None else jnp.where(heads[0], lse, lse_h)
            o_ref[cur, :] = o
            lse_ref[cur, :] = lse
            return 0

        lax.fori_loop(0, s_len // BLOCK, step, 0, unroll=DL_FWD_UNROLL)

    col = lambda first: pl.BlockSpec((s_len, DL_PAIR), lambda i: (0, first + i))
    out = SDS((s_len, DL_WIDTH), F32)
    steps = DL_WIDTH // DL_PAIR
    if payload is not None:
        step = lambda: pl.program_id(0)
        payload = ("gather", payload, lambda: (step() == 0, step() == steps - 1, step() == steps - 1))
    return _hosted_call(body, payload, name=name, out_shape=(out, out), grid=(steps,),
                        in_specs=[col(DL_Q_BLOCK0), col(DL_K_BLOCK0), col(DL_V_BLOCK0)], out_specs=(col(0), col(0)),
                        args=(qk, qk, u))


def _dl_bwd(qk, u, dmix, o_mix, wt, lse, dil, name, payload=None):
    s_len = qk.shape[0]
    nb = s_len // dil // BLOCK
    scale = HEAD_DIM ** -0.5
    nt_dims = (((1,), (1,)), ((), ()))
    tn_dims = (((0,), (0,)), ((), ()))

    def body(q_ref, k_ref, v_ref, do_ref, om_ref, wt_ref, lse_ref, dq_ref, dk_ref, dv_ref):
        dk_ref[...] = jnp.zeros_like(dk_ref)
        dv_ref[...] = jnp.zeros_like(dv_ref)
        heads = _head_lanes()

        def step(idx, _):
            n, cur, prev = _dl_rows(idx, nb, dil)
            q = q_ref[cur, :]
            kk = _dl_window(k_ref, cur, prev)
            vv = _dl_window(v_ref, cur, prev)
            dov = do_ref[cur, :]
            d_lanes = dov * om_ref[cur, :]
            w_lanes = wt_ref[cur, :]
            lse_lanes = lse_ref[cur, :]
            dq, dkk, dvv = None, None, None
            for lanes in heads:
                qm = jnp.where(lanes, q, 0.0).astype(BF16)
                s = _dl_scores(qm, kk, n)
                p = jnp.exp(s - jnp.max(jnp.where(lanes, lse_lanes, NEG), axis=-1, keepdims=True))
                w = jnp.max(jnp.where(lanes, w_lanes, 0.0), axis=-1, keepdims=True)
                d_all = jnp.sum(jnp.where(lanes, d_lanes, 0.0), axis=-1, keepdims=True)
                do_n = jnp.where(lanes, dov * w, 0.0).astype(BF16)
                dp = lax.dot_general(do_n, vv, nt_dims, preferred_element_type=F32)
                ds = (p * (dp - w * d_all) * scale).astype(BF16)
                dq_h = jnp.dot(ds, kk, preferred_element_type=F32)
                dkk_h = lax.dot_general(ds, qm, tn_dims, preferred_element_type=F32)
                dvv_h = lax.dot_general(p.astype(BF16), do_n, tn_dims, preferred_element_type=F32)
                dq = dq_h if dq is None else jnp.where(heads[0], dq, dq_h)
                dkk = dkk_h if dkk is None else dkk + dkk_h
                dvv = dvv_h if dvv is None else dvv + dvv_h
            dq_ref[cur, :] = dq
            dk_ref[prev, :] += dkk[:BLOCK]
            dv_ref[prev, :] += dvv[:BLOCK]
            dk_ref[cur, :] += dkk[BLOCK:]
            dv_ref[cur, :] += dvv[BLOCK:]
            return 0

        lax.fori_loop(0, s_len // BLOCK, step, 0, unroll=DL_UNROLL)

    col = lambda first: pl.BlockSpec((s_len, DL_PAIR), lambda i: (0, first + i))
    out = SDS((s_len, DL_WIDTH), F32)
    steps = DL_WIDTH // DL_PAIR
    if payload is not None:
        step = lambda: pl.program_id(0)
        payload = ("all_to_all", payload, lambda: (step() == 0, None, step() == steps - 1))
    return _hosted_call(
        body, payload, name=name, out_shape=(out, out, out), grid=(steps,),
        in_specs=[col(DL_Q_BLOCK0), col(DL_K_BLOCK0), col(DL_V_BLOCK0), col(DL_DO_BLOCK0), col(0), col(0), col(0)],
        out_specs=(col(0), col(0), col(0)), args=(qk, qk, u, dmix, o_mix, wt, lse))


def _dl_mix_fwd(outs, lses, name):
    rows, w = outs[0].shape
    t = 256

    def body(o1, o2, o3, l1, l2, l3, ob_ref, of_ref, w1, w2, w3):
        a, b, c = l1[...], l2[...], l3[...]
        m = jnp.maximum(jnp.maximum(a, b), c)
        ea, eb, ec = jnp.exp(a - m), jnp.exp(b - m), jnp.exp(c - m)
        den = ea + eb + ec
        wa, wb, wc = ea / den, eb / den, ec / den
        o = wa * o1[...] + wb * o2[...] + wc * o3[...]
        ob_ref[...] = o.astype(BF16)
        of_ref[...] = o
        w1[...] = wa
        w2[...] = wb
        w3[...] = wc

    row = pl.BlockSpec((t, w), lambda i: (i, 0))
    f = SDS((rows, w), F32)
    return pl.pallas_call(body, name=name, out_shape=(SDS((rows, w), BF16), f, f, f, f), grid=(rows // t,),
                          in_specs=[row] * 6, out_specs=(row,) * 5, compiler_params=_params())(*outs, *lses)


def _x_probs(qh, kh):
    s = lax.dot_general(qh, kh, (((1,), (1,)), ((), ())), preferred_element_type=F32) * (X_HEAD_DIM ** -0.5)
    e = jnp.exp(s - jnp.max(s, axis=-1, keepdims=True))
    return e / jnp.sum(e, axis=-1, keepdims=True)


def _xattn_fwd(q, k, v, name):
    rows, d = q.shape
    t = 512

    def body(q_ref, k_ref, v_ref, o_ref):
        for hh in range(X_HEADS):
            cols = slice(hh * X_HEAD_DIM, (hh + 1) * X_HEAD_DIM)
            p = _x_probs(q_ref[:, cols], k_ref[:, cols])
            o_ref[:, cols] = jnp.dot(p.astype(BF16), v_ref[:, cols], preferred_element_type=F32).astype(BF16)

    row = pl.BlockSpec((t, d), lambda i: (i, 0))
    mem = pl.BlockSpec((N_MEM, d), lambda i: (0, 0))
    return pl.pallas_call(body, name=name, out_shape=SDS((rows, d), BF16), grid=(rows // t,), in_specs=[row, mem, mem],
                          out_specs=row, compiler_params=_params())(q, k, v)


def _xattn_bwd(q, k, v, do, name):
    rows, d = q.shape
    t = 512
    scale = X_HEAD_DIM ** -0.5

    def body(q_ref, k_ref, v_ref, do_ref, dq_ref, dk_ref, dv_ref):
        @pl.when(pl.program_id(0) == 0)
        def _():
            dk_ref[...] = jnp.zeros_like(dk_ref)
            dv_ref[...] = jnp.zeros_like(dv_ref)

        for hh in range(X_HEADS):
            cols = slice(hh * X_HEAD_DIM, (hh + 1) * X_HEAD_DIM)
            qh, kh, vh, doh = q_ref[:, cols], k_ref[:, cols], v_ref[:, cols], do_ref[:, cols]
            p = _x_probs(qh, kh)
            dp = lax.dot_general(doh, vh, (((1,), (1,)), ((), ())), preferred_element_type=F32)
            ds = (p * (dp - jnp.sum(p * dp, axis=-1, keepdims=True)) * scale).astype(BF16)
            dq_ref[:, cols] = jnp.dot(ds, kh, preferred_element_type=F32).astype(BF16)
            dk_ref[:, cols] += lax.dot_general(ds, qh, (((0,), (0,)), ((), ())), preferred_element_type=F32)
            dv_ref[:, cols] += lax.dot_general(p.astype(BF16), doh, (((0,), (0,)), ((), ())), preferred_element_type=F32)

    row = pl.BlockSpec((t, d), lambda i: (i, 0))
    mem = pl.BlockSpec((N_MEM, d), lambda i: (0, 0))
    return pl.pallas_call(
        body, name=name, out_shape=(SDS((rows, d), BF16), SDS((N_MEM, d), F32), SDS((N_MEM, d), F32)), grid=(rows // t,),
        in_specs=[row, mem, mem, row], out_specs=(row, mem, mem), compiler_params=_params(),
    )(q, k, v, do)


CV_TILE = 256
CV_HALO = 32
CV_LEAD = CV_HALO - (CV_KERNEL - 1)


def _shifted(win, off, rows):
    n = win.shape[0]
    return pltpu.roll(win, (n - off) % n, axis=0)[:rows]


def _glu(val, gate):
    return val * jax.nn.sigmoid(gate)


def _ln_parts(c):
    mu = jnp.mean(c, axis=-1, keepdims=True)
    xc = c - mu
    rstd = lax.rsqrt(jnp.mean(xc * xc, axis=-1, keepdims=True) + EPS)
    return xc * rstd, rstd


def _cv_fwd(u, cv_w, cv_b, ln_g, ln_b, name):
    rows = u.shape[0]
    t, w = CV_TILE, CV_WIDTH
    val_col = 3 * SB_WIDTH // w
    ratio = t // CV_HALO

    def body(val_ref, gate_ref, pval_ref, pgate_ref, w_ref, b_ref, g_ref, beta_ref, s_ref, c_ref):
        i = pl.program_id(0)
        hist = jnp.where(i > 0, _glu(pval_ref[...], pgate_ref[...]), 0.0)
        win = jnp.concatenate([hist, _glu(val_ref[...], gate_ref[...])], axis=0)
        acc = jnp.broadcast_to(b_ref[...], (t, w))
        for kk in range(CV_KERNEL):
            acc = acc + _shifted(win, CV_LEAD + kk, t) * w_ref[kk:kk + 1, :]
        c_ref[...] = acc
        n, _ = _ln_parts(acc)
        y = n * g_ref[...] + beta_ref[...]
        s_ref[...] = (y * jax.nn.sigmoid(y)).astype(BF16)

    cur = lambda col: pl.BlockSpec((t, w), lambda i: (i, col))
    prev = lambda col: pl.BlockSpec((CV_HALO, w), lambda i: (jnp.maximum(i * ratio - 1, 0), col))
    vec = pl.BlockSpec((1, w), lambda i: (0, 0))
    return pl.pallas_call(
        body, name=name, out_shape=(SDS((rows, w), BF16), SDS((rows, w), F32)), grid=(rows // t,),
        in_specs=[cur(val_col), cur(val_col + 1), prev(val_col), prev(val_col + 1),
                  pl.BlockSpec((CV_KERNEL, w), lambda i: (0, 0)), vec, vec, vec],
        out_specs=(pl.BlockSpec((t, w), lambda i: (i, 0)),) * 2, compiler_params=_params(),
    )(u, u, u, u, cv_w, cv_b, ln_g, ln_b)


def _cv_bwd(u, c, ds, db_out, cv_w, ln_g, ln_b, name, payload=None):
    rows = u.shape[0]
    t, w = CV_TILE, CV_WIDTH
    val_col = 3 * SB_WIDTH // w
    ratio = t // CV_HALO
    nt = rows // t

    def conv_out_grad(c_v, ds_v, g_v, beta_v):
        n, rstd = _ln_parts(c_v)
        y = n * g_v + beta_v
        sig = jax.nn.sigmoid(y)
        dy = ds_v * (sig * (1.0 + y * (1.0 - sig)))
        dn = dy * g_v
        dc = rstd * (dn - jnp.mean(dn, axis=-1, keepdims=True) - n * jnp.mean(dn * n, axis=-1, keepdims=True))
        return dc, dy, n

    def body(val_ref, gate_ref, pval_ref, pgate_ref, c_ref, nc_ref, ds_ref, nds_ref, dbo_ref, w_ref, g_ref, beta_ref,
             dvg_ref, dw_ref, db_ref, dg_ref, dbeta_ref, dpwb_ref):
        i = pl.program_id(0)

        @pl.when(i == 0)
        def _():
            for r in (dw_ref, db_ref, dg_ref, dbeta_ref, dpwb_ref):
                r[...] = jnp.zeros_like(r)

        g_v, beta_v = g_ref[...], beta_ref[...]
        dc, dy, n = conv_out_grad(c_ref[...], ds_ref[...], g_v, beta_v)
        dc_next, _, _ = conv_out_grad(nc_ref[...], nds_ref[...], g_v, beta_v)
        dc_next = jnp.where(i < nt - 1, dc_next, 0.0)
        dg_ref[...] += jnp.sum(dy * n, axis=0, keepdims=True)
        dbeta_ref[...] += jnp.sum(dy, axis=0, keepdims=True)
        db_ref[...] += jnp.sum(dc, axis=0, keepdims=True)
        dpwb_ref[...] += jnp.sum(dbo_ref[...], axis=0, keepdims=True)

        val, gate = val_ref[...], gate_ref[...]
        hist = jnp.where(i > 0, _glu(pval_ref[...], pgate_ref[...]), 0.0)
        win = jnp.concatenate([hist, _glu(val, gate)], axis=0)
        dc_ext = jnp.concatenate([dc, dc_next], axis=0)
        dglu = jnp.zeros((t, w), F32)
        for kk in range(CV_KERNEL):
            dw_ref[kk:kk + 1, :] += jnp.sum(dc * _shifted(win, CV_LEAD + kk, t), axis=0, keepdims=True)
            dglu = dglu + _shifted(dc_ext, CV_KERNEL - 1 - kk, t) * w_ref[kk:kk + 1, :]
        sig = jax.nn.sigmoid(gate)
        dvg_ref[:, 0:w] = (dglu * sig).astype(BF16)
        dvg_ref[:, w:2 * w] = (dglu * val * sig * (1.0 - sig)).astype(BF16)

    cur = lambda col: pl.BlockSpec((t, w), lambda i: (i, col))
    prev = lambda col: pl.BlockSpec((CV_HALO, w), lambda i: (jnp.maximum(i * ratio - 1, 0), col))
    nxt = pl.BlockSpec((CV_HALO, w), lambda i: (jnp.minimum((i + 1) * ratio, rows // CV_HALO - 1), 0))
    vec = pl.BlockSpec((1, w), lambda i: (0, 0))
    if payload is not None:
        payload = ("all_to_all", payload, lambda: (pl.program_id(0) == 0, None, pl.program_id(0) == nt - 1))
    return _hosted_call(
        body, payload, name=name,
        out_shape=(SDS((rows, 2 * w), BF16), SDS((CV_HALO, w), F32), SDS((1, w), F32), SDS((1, w), F32), SDS((1, w), F32),
                   SDS((1, w), F32)),
        grid=(nt,),
        in_specs=[cur(val_col), cur(val_col + 1), prev(val_col), prev(val_col + 1), cur(0), nxt, cur(0), nxt, cur(0),
                  pl.BlockSpec((CV_KERNEL, w), lambda i: (0, 0)), vec, vec],
        out_specs=(pl.BlockSpec((t, 2 * w), lambda i: (i, 0)), pl.BlockSpec((CV_HALO, w), lambda i: (0, 0)), vec, vec, vec, vec),
        args=(u, u, u, u, c, c, ds, ds, db_out, cv_w, ln_g, ln_b))


FFN_TILE = 1024
FFN_CHUNK = 256
FFN_COLS = 256
FFN_HALO = 8
FFN_KERNEL = 3
N_FF_BLOCKS = D_FF // FFN_COLS


def _conv3(prev8, cur, w_ref, b_ref, first):
    t = cur.shape[0]
    win = jnp.concatenate([jnp.where(first, 0.0, prev8), cur], axis=0)
    return (b_ref[...] + _shifted(win, FFN_HALO - 2, t) * w_ref[0:1, :] + _shifted(win, FFN_HALO - 1, t) * w_ref[1:2, :]
            + cur * w_ref[2:3, :])


def _gelu_gate(gate, val):
    return jax.nn.gelu(gate, approximate=True) * val


GELU_C0 = math.sqrt(2.0 / math.pi)
GELU_C1 = 0.044715


def _gelu_gate_bwd(gate, val, dout):
    sq = gate * gate
    th = jnp.tanh(GELU_C0 * gate * (1.0 + GELU_C1 * sq))
    half_cdf = 0.5 * (1.0 + th)
    slope = half_cdf + 0.5 * gate * (1.0 - th * th) * (GELU_C0 * (1.0 + 3.0 * GELU_C1 * sq))
    return dout * val * slope, dout * (gate * half_cdf)


def _ffn_specs(t):
    ratio = t // FFN_HALO
    cur = pl.BlockSpec((t, FFN_COLS), lambda j, i: (i, j))
    prev = pl.BlockSpec((FFN_HALO, FFN_COLS), lambda j, i: (jnp.maximum(i * ratio - 1, 0), j))
    wsp = pl.BlockSpec((FFN_KERNEL, FFN_COLS), lambda j, i: (0, j))
    bsp = pl.BlockSpec((1, FFN_COLS), lambda j, i: (0, j))
    return cur, prev, wsp, bsp


def _ffn_host_steps(row_tiles):
    def when():
        j, i = pl.program_id(0), pl.program_id(1)
        return (j == 0) & (i == 0), (j == (3 * N_FF_BLOCKS) // 4) & (i == 0), (j == N_FF_BLOCKS - 1) & (i == row_tiles - 1)
    return when


def _ffn_act_fwd(up_g, up_v, w_g, w_v, b_g, b_v, name, payload=None):
    rows = up_g.shape[0]
    t = FFN_TILE
    cur, prev, wsp, bsp = _ffn_specs(t)

    def body(g_ref, v_ref, pg_ref, pv_ref, wg_ref, wv_ref, bg_ref, bv_ref, o_ref):
        first = pl.program_id(1) == 0
        gate = _conv3(pg_ref[...], g_ref[...], wg_ref, bg_ref, first)
        val = _conv3(pv_ref[...], v_ref[...], wv_ref, bv_ref, first)
        o_ref[...] = _gelu_gate(gate, val).astype(BF16)

    if payload is not None:
        payload = ("gather", payload, _ffn_host_steps(rows // t))
    return _hosted_call(
        body, payload, name=name, out_shape=(SDS((rows, D_FF), BF16),), grid=(N_FF_BLOCKS, rows // t),
        in_specs=[cur, cur, prev, prev, wsp, wsp, bsp, bsp], out_specs=(cur,), args=(up_g, up_v, up_g, up_v, w_g, w_v, b_g, b_v))


def _ffn_act_bwd(up_g, up_v, dact, w_g, w_v, b_g, b_v, name, payload=None):
    rows = up_g.shape[0]
    t = FFN_TILE
    ch = FFN_CHUNK
    che = ch + FFN_HALO
    ratio = t // FFN_HALO
    nt = rows // t
    cur, prev, wsp, bsp = _ffn_specs(t)
    nxt = pl.BlockSpec((FFN_HALO, FFN_COLS), lambda j, i: (jnp.minimum((i + 1) * ratio, rows // FFN_HALO - 1), j))

    def body(g_ref, v_ref, pg_ref, pv_ref, ng_ref, nv_ref, da_ref, nda_ref, wg_ref, wv_ref, bg_ref, bv_ref,
             dug_ref, duv_ref, dwg_ref, dwv_ref, dbg_ref, dbv_ref, win_g, win_v, da_win):
        i = pl.program_id(1)
        first = i == 0

        @pl.when(first)
        def _():
            for r in (dwg_ref, dwv_ref, dbg_ref, dbv_ref):
                r[...] = jnp.zeros_like(r)

        for win, pre, x, nx in ((win_g, pg_ref, g_ref, ng_ref), (win_v, pv_ref, v_ref, nv_ref)):
            win[0:FFN_HALO, :] = jnp.where(first, 0.0, pre[...])
            win[FFN_HALO:FFN_HALO + t, :] = x[...]
            win[FFN_HALO + t:, :] = nx[...]
        da_win[0:t, :] = da_ref[...]
        da_win[t:, :] = jnp.where(i < nt - 1, nda_ref[...], 0.0)
        halves = ((win_g, wg_ref, bg_ref, dug_ref), (win_v, wv_ref, bv_ref, duv_ref))

        def chunk(c, sums):
            base = pl.multiple_of(c * ch, ch)
            taps, convs = [], []
            for win, w_ref, b_ref, _ in halves:
                w = win[pl.ds(base, ch + 2 * FFN_HALO), :]
                shifted = [_shifted(w, FFN_HALO - 2 + kk, che) for kk in range(FFN_KERNEL)]
                taps.append(shifted)
                convs.append(b_ref[...] + sum(s * w_ref[kk:kk + 1, :] for kk, s in enumerate(shifted)))
            dconvs = _gelu_gate_bwd(*convs, da_win[pl.ds(base, che), :])
            new = []
            for dc_ext, shifted, (_, w_ref, _, du_ref), (dw, db) in zip(dconvs, taps, halves, sums):
                dc = dc_ext[:ch]
                du_ref[pl.ds(base, ch), :] = (dc * w_ref[2:3, :] + _shifted(dc_ext, 1, ch) * w_ref[1:2, :]
                                              + _shifted(dc_ext, 2, ch) * w_ref[0:1, :]).astype(BF16)
                dw = [dw[kk] + jnp.sum(dc * shifted[kk][:ch], axis=0, keepdims=True) for kk in range(FFN_KERNEL)]
                new.append((dw, db + jnp.sum(dc, axis=0, keepdims=True)))
            return new

        zero = jnp.zeros((1, FFN_COLS), F32)
        sums = lax.fori_loop(0, t // ch, chunk, [([zero] * FFN_KERNEL, zero)] * 2, unroll=2)
        for (dw, db), dw_ref, db_ref in zip(sums, (dwg_ref, dwv_ref), (dbg_ref, dbv_ref)):
            for kk in range(FFN_KERNEL):
                dw_ref[kk:kk + 1, :] += dw[kk]
            db_ref[...] += db

    big, wshape, bshape = SDS((rows, D_FF), BF16), SDS((FFN_KERNEL, D_FF), F32), SDS((1, D_FF), F32)
    if payload is not None:
        payload = ("all_to_all", payload, _ffn_host_steps(nt))
    window = pltpu.VMEM((t + 2 * FFN_HALO, FFN_COLS), F32)
    return _hosted_call(
        body, payload, name=name, out_shape=(big, big, wshape, wshape, bshape, bshape), grid=(N_FF_BLOCKS, nt),
        in_specs=[cur, cur, prev, prev, nxt, nxt, cur, nxt, wsp, wsp, bsp, bsp], out_specs=(cur, cur, wsp, wsp, bsp, bsp),
        scratch_shapes=(window, window, pltpu.VMEM((t + FFN_HALO, FFN_COLS), F32)),
        args=(up_g, up_v, up_g, up_v, up_g, up_v, dact, dact, w_g, w_v, b_g, b_v))


def _adamw_update(parts, w_ref, m_ref, v_ref, g_ref, d_ref, nm_ref, nv_ref):
    g = parts[0].astype(F32)
    for s in range(1, N_DEV):
        g = g + parts[s].astype(F32)
    nm = ADAM_B1 * m_ref[...] + (1.0 - ADAM_B1) * g
    nv = ADAM_B2 * v_ref[...] + (1.0 - ADAM_B2) * jnp.square(g)
    m_hat = nm / (1.0 - ADAM_B1 ** ADAM_STEP)
    v_hat = nv / (1.0 - ADAM_B2 ** ADAM_STEP)
    g_ref[...] = g
    d_ref[...] = -ADAM_LR * (m_hat / (jnp.sqrt(v_hat) + ADAM_EPS) + ADAM_WD * w_ref[...])
    nm_ref[...] = nm
    nv_ref[...] = nv


def _adamw(parts, w, m, v, name):
    rows, cols = w.shape
    t = _pick(rows, (512, 256, 128)) if rows > 512 else rows

    def body(p_ref, *refs):
        _adamw_update(p_ref[...], *refs)

    row = pl.BlockSpec((t, cols), lambda i: (i, 0))
    out = SDS((rows, cols), F32)
    return pl.pallas_call(
        body, name=name, out_shape=(out,) * 4, grid=(rows // t,),
        in_specs=[pl.BlockSpec((N_DEV, t, cols), lambda i: (0, i, 0)), row, row, row], out_specs=(row,) * 4,
        compiler_params=_params(),
    )(parts, w, m, v)


def _adamw_packed(sources, w, m, v, name):
    rows, cols = w.shape
    t = ADAMW_ROW_BLOCK
    nb = rows // DEPTH // t
    (src0, first0), (src1, first1) = sources
    assert first0 % t == 0 and first1 % t == 0 and rows % (DEPTH * t) == 0

    def body(p0_ref, p1_ref, *refs):
        layer = pl.program_id(0)
        _adamw_update(jnp.where(layer == 0, p0_ref[...], p1_ref[...]), *refs)

    spec0 = pl.BlockSpec((N_DEV, t, cols), lambda l, i: (0, first0 // t + i * (1 - l) + (nb - 1) * l, 0))
    spec1 = pl.BlockSpec((N_DEV, t, cols), lambda l, i: (0, first1 // t + i * l, 0))
    row = pl.BlockSpec((t, cols), lambda l, i: (l * nb + i, 0))
    out = SDS((rows, cols), F32)
    return pl.pallas_call(body, name=name, out_shape=(out,) * 4, grid=(DEPTH, nb), in_specs=[spec0, spec1, row, row, row],
                          out_specs=(row,) * 4, compiler_params=_params())(src0, src1, w, m, v)


_COMM_SEMAPHORES = [pltpu.SemaphoreType.DMA((N_DEV - 1,)), pltpu.SemaphoreType.DMA((N_DEV - 1,)), pltpu.SemaphoreType.DMA]


def _gather_steps(x_ref, out_ref, send_sems, recv_sems, local_sem):
    x_, y_, c_ = lax.axis_index("x"), lax.axis_index("y"), lax.axis_index("c")
    me, sibling = (x_, y_, c_), (x_, y_, 1 - c_)
    chips = [(1 - x_, y_), (x_, 1 - y_), (1 - x_, 1 - y_)]

    def slot(px, py, pc):
        return out_ref.at[4 * px + 2 * py + pc]

    def copy(kk, block, to, src=None):
        return pltpu.make_async_remote_copy(
            src_ref=slot(*block) if src is None else src, dst_ref=slot(*block),
            send_sem=send_sems.at[kk], recv_sem=recv_sems.at[kk], device_id=to, device_id_type=MESH)

    def mine():
        return pltpu.make_async_copy(x_ref, slot(*me), local_sem)

    def first():
        return [copy(0, me, sibling, src=x_ref)] + [copy(1 + j, me, (*chip, c_), src=x_ref) for j, chip in enumerate(chips)]

    def passed():
        return [copy(4 + j, (*chip, c_), sibling) for j, chip in enumerate(chips)]

    def start():
        mine().start()
        for cp in first():
            cp.start()

    def forward():
        for j, (chip, cp) in enumerate(zip(chips, passed())):
            copy(1 + j, (*chip, c_), me).wait_recv()
            cp.start()

    def finish():
        copy(0, sibling, me).wait_recv()
        for j, chip in enumerate(chips):
            copy(4 + j, (*chip, 1 - c_), me).wait_recv()
        for cp in first() + passed():
            cp.wait_send()
        mine().wait()

    return start, forward, finish


def _exchange_steps(x_ref, out_ref, send_sems, recv_sems, local_sem):
    x_, y_, c_ = lax.axis_index("x"), lax.axis_index("y"), lax.axis_index("c")
    me = 4 * x_ + 2 * y_ + c_

    def mine():
        return pltpu.make_async_copy(x_ref.at[me], out_ref.at[me], local_sem)

    def copies():
        out = []
        for r in range(1, N_DEV):
            px = 1 - x_ if r & 4 else x_
            py = 1 - y_ if r & 2 else y_
            pc = 1 - c_ if r & 1 else c_
            out.append(pltpu.make_async_remote_copy(
                src_ref=x_ref.at[4 * px + 2 * py + pc], dst_ref=out_ref.at[me],
                send_sem=send_sems.at[r - 1], recv_sem=recv_sems.at[r - 1], device_id=(px, py, pc), device_id_type=MESH))
        return out

    def start():
        mine().start()
        for cp in copies():
            cp.start()

    def finish():
        for cp in copies():
            cp.wait_recv()
        for cp in copies():
            cp.wait_send()
        mine().wait()

    return start, finish


def _hosted_call(body, exchange, *, name, out_shape, grid, in_specs, out_specs, args, scratch_shapes=()):
    if exchange is None:
        return pl.pallas_call(body, name=name, out_shape=out_shape, grid=grid, in_specs=in_specs, out_specs=out_specs,
                              scratch_shapes=list(scratch_shapes), compiler_params=_params())(*args)
    kind, payload, when = exchange
    n_in, n_out, n_scratch = len(in_specs), len(out_specs), len(scratch_shapes)
    result = SDS((N_DEV,) + payload.shape, payload.dtype) if kind == "gather" else SDS(payload.shape, payload.dtype)

    def hosting(*refs):
        ins, pay_ref = refs[:n_in], refs[n_in]
        outs, res_ref = refs[n_in + 1:n_in + 1 + n_out], refs[n_in + 1 + n_out]
        rest = refs[n_in + 2 + n_out:]
        scratch, sems = rest[:n_scratch], rest[n_scratch:]
        first, middle, last = when()
        if kind == "gather":
            start, forward, finish = _gather_steps(pay_ref, res_ref, *sems)
            pl.when(first)(start)
            pl.when(middle)(forward)
        else:
            start, finish = _exchange_steps(pay_ref, res_ref, *sems)
            pl.when(first)(start)
        body(*ins, *outs, *scratch)
        pl.when(last)(finish)

    hbm = pl.BlockSpec(memory_space=pl.ANY)
    return pl.pallas_call(
        hosting, name=name, out_shape=tuple(out_shape) + (result,), grid=grid, in_specs=list(in_specs) + [hbm],
        out_specs=tuple(out_specs) + (hbm,), scratch_shapes=list(scratch_shapes) + _COMM_SEMAPHORES,
        compiler_params=_params(has_side_effects=True),
    )(*args, payload)


def _exchange_alone(xs, make_steps, out_shapes, name):
    n = len(xs)

    def body(*refs):
        sems = refs[2 * n:]
        steps = [make_steps(refs[k], refs[n + k], *sems[3 * k:3 * k + 3]) for k in range(n)]
        for stage in zip(*steps):
            for step in stage:
                step()

    hbm = pl.BlockSpec(memory_space=pl.ANY)
    return pl.pallas_call(body, name=name, out_shape=tuple(out_shapes), in_specs=[hbm] * n, out_specs=(hbm,) * n,
                          scratch_shapes=_COMM_SEMAPHORES * n, compiler_params=pltpu.CompilerParams(has_side_effects=True))(*xs)


def _all_gather(xs, name):
    return _exchange_alone(xs, _gather_steps, [SDS((N_DEV,) + x.shape, x.dtype) for x in xs], name)


def _all_to_all(xs, name):
    return _exchange_alone(xs, _exchange_steps, [SDS(x.shape, x.dtype) for x in xs], name)


BIG = ("w_in", "cv_pw_w", "w_out", "x_wq", "x_wk", "x_wv", "x_wo", "ffn_w_up", "ffn_w_down")
_MIXER = (("w_in", 352), ("w_out", 128))
_CROSS = (("x_wq", 128), ("x_wk", 128), ("x_wv", 128), ("x_wo", 128))
GROUPS = {
    "a": (("w_in", 0, 352),),
    "o0": (("w_out", 0, 128),),
    "m1": tuple((n, 1, r) for n, r in _MIXER),
    "bx": tuple((n, 0, r) for n, r in _CROSS),
    "cx": tuple((n, 1, r) for n, r in _CROSS),
    "bf": (("ffn_w_down", 0, 352),),
    "cf": (("ffn_w_down", 1, 352),),
}
GRADIENT_GROUPS = ("a", "o0", "m1", "bf", "bx", "cf", "cx")
PW_GROUP = "o0"
TRANSPOSED = ("w_in", "ffn_w_up")
PW_ROWS = 16
ADAMW_ROW_BLOCK = 32


def _group_rows(group):
    out, first = {}, 0
    for n, l, r in GROUPS[group]:
        out[(n, l)] = (first, r)
        first += r
    return out


def _where_is(name, layer):
    for group in GRADIENT_GROUPS:
        rows = _group_rows(group)
        if (name, layer) in rows:
            return (group,) + rows[(name, layer)]
    raise KeyError((name, layer))


def _pack_weights(group, wts):
    pieces = []
    for n, l, _ in GROUPS[group]:
        w = wts[n][l].astype(BF16)
        pieces.append(w.T if n in TRANSPOSED else w)
    if group == PW_GROUP:
        pieces.append(wts["cv_pw_w"].astype(BF16).reshape(PW_ROWS, PAYLOAD_COLS))
    return jnp.concatenate(pieces, axis=0)


def _unpack_weights(group, gathered):
    return {key: gathered[:, first:first + r, :].reshape(N_DEV * r, PAYLOAD_COLS)
            for key, (first, r) in _group_rows(group).items()}


def _pack_grads(group, grads):
    pieces = []
    for n, l, r in GROUPS[group]:
        g = grads[n][l]
        parts = g if isinstance(g, tuple) else (g,)
        pieces.append(jnp.concatenate([p.reshape(-1, r, PAYLOAD_COLS) for p in parts], axis=0))
    if group == PW_GROUP:
        pieces.append(_to_shards("cv_pw_w", jnp.stack(grads["cv_pw_w"])).reshape(N_DEV, PW_ROWS, PAYLOAD_COLS))
    return jnp.concatenate(pieces, axis=1)


COL_SHARDED = ("w_in", "ffn_w_up", "cv_w", "ffn_conv_w")
SMALL_SHARDED = ("cv_w", "ffn_conv_w")
REPLICATED = ("mix_norm_pre", "cv_b", "cv_ln_g", "cv_ln_b", "cv_pw_b", "mix_norm_post", "x_norm_pre", "mem_norm",
              "x_norm_post", "ffn_norm_pre", "ffn_conv_b", "ffn_norm_post")
WEIGHTS = ("mix_norm_pre", "w_in", "cv_w", "cv_b", "cv_ln_g", "cv_ln_b", "cv_pw_w", "cv_pw_b", "w_out", "mix_norm_post",
           "x_norm_pre", "mem_norm", "x_wq", "x_wk", "x_wv", "x_wo", "x_norm_post", "ffn_norm_pre", "ffn_w_up",
           "ffn_conv_w", "ffn_conv_b", "ffn_w_down", "ffn_norm_post")
PAYLOAD_COLS = 1024


PAYLOAD_ROW_TILE = 16


def _pad_rows(flat, cols):
    n = flat.shape[-1]
    rows = -(-n // (cols * PAYLOAD_ROW_TILE)) * PAYLOAD_ROW_TILE
    pad = rows * cols - n
    if pad:
        flat = jnp.concatenate([flat, jnp.zeros(flat.shape[:-1] + (pad,), flat.dtype)], axis=-1)
    return flat.reshape(flat.shape[:-1] + (rows, cols))


def _unshard(name, parts):
    n, depth, r, c = parts.shape
    if name in COL_SHARDED:
        return parts.transpose(1, 2, 0, 3).reshape(depth, r, n * c)
    return parts.transpose(1, 0, 2, 3).reshape(depth, n * r, c)


def _to_shards(name, full):
    depth, r, c = full.shape
    if name in COL_SHARDED:
        return full.reshape(depth, r, N_DEV, c // N_DEV).transpose(2, 0, 1, 3).reshape(N_DEV, -1)
    return full.reshape(depth, N_DEV, r // N_DEV, c).transpose(1, 0, 2, 3).reshape(N_DEV, -1)


def _heads_major(x, h):
    return x.reshape(x.shape[0], h, HEAD_DIM).transpose(1, 0, 2)


def _tokens_major(x):
    return x.transpose(1, 0, 2).reshape(x.shape[1], -1)


def _ffn_halves(p):
    w, b = p["ffn_conv_w"], p["ffn_conv_b"]
    return w[:, :D_FF], w[:, D_FF:], b[:, :D_FF], b[:, D_FF:]


def _layer_fwd(l, h, hn, p, mem, cos, sin, g_next, payload, unpack, ffn_shards, ffn_payload, rope_payload=None,
               rope_unpack=None):
    p = dict(p)
    sv = {"h0": h, "hn0": hn}
    u = _mm(hn, p["w_in"], "nt", F32, f"l{l}_in_proj")
    sv["u"] = u
    sb = _heads_major(u[:, :3 * SB_WIDTH].astype(BF16), 3 * SB_HEADS)
    sb_q, sb_k, sb_v = sb[:SB_HEADS], sb[SB_HEADS:2 * SB_HEADS], sb[2 * SB_HEADS:]
    a_out, sb_tot, sb_first, gathered = _sb_fwd(sb_q, sb_k, sb_v, payload, f"l{l}_sb_fwd")
    p.update(unpack(gathered))
    sv.update(sb_q=sb_q, sb_k=sb_k, sb_v=sb_v, sb_tot=sb_tot, sb_first=sb_first, p=p)

    qk, *rope_gathered = _rope_fwd(u, cos, sin, f"l{l}_rope_fwd", rope_payload)
    if rope_gathered:
        p.update(rope_unpack(rope_gathered[0]))
        sv["rope_gathered"] = rope_gathered[0]
    cv_s, cv_c = _cv_fwd(u, p["cv_w"], p["cv_b"], p["cv_ln_g"], p["cv_ln_b"], f"l{l}_cv_fwd")
    b_out = _mm(cv_s, p["cv_pw_w"], "nn", BF16, f"l{l}_cv_pw", bias=p["cv_pw_b"])
    sv.update(cv_s=cv_s, cv_c=cv_c)

    up_t = ffn_shards[0].astype(BF16).T
    half_rows = up_t.shape[0] // 2
    carried = (up_t[:half_rows], up_t[half_rows:], ffn_shards[1].astype(BF16))
    outs, lses, got = [], [], []
    for b, (_, dil) in enumerate(DL_PATTERN):
        o, lse, gathered = _dl_fwd(qk, u, dil, f"l{l}_dl{b}_fwd", carried[b])
        outs.append(o)
        lses.append(lse)
        got.append(gathered)
    up_blocks = jnp.concatenate(got[:2], axis=1)
    half = N_DEV // 2
    p["ffn_w_up"] = (up_blocks[:half].reshape(-1, PAYLOAD_COLS), up_blocks[half:].reshape(-1, PAYLOAD_COLS))
    p["ffn_w_down"] = got[2].reshape(-1, PAYLOAD_COLS)
    c_out, c_out_f32, w1, w2, w3 = _dl_mix_fwd(outs, lses, f"l{l}_dl_mix")
    sv.update(dl_qk=qk, dl_lse=lses, dl_o=c_out_f32, dl_w=(w1, w2, w3))

    mix = jnp.concatenate([_tokens_major(a_out), b_out, c_out], axis=-1)
    y = _mm(mix, p["w_out"], "nn", F32, f"l{l}_out_proj")
    h1, hn1 = _res_norm_fwd(h, y, p["mix_norm_post"], p["x_norm_pre"], f"l{l}_mix_post")
    sv.update(mix=mix, y_mix=y, h1=h1, hn1=hn1)

    xq = _mm(hn1, p["x_wq"], "nn", BF16, f"l{l}_xq")
    memn = _rms_fwd(mem, p["mem_norm"], f"l{l}_mem_norm")
    xk = _mm(memn, p["x_wk"], "nn", BF16, f"l{l}_xk")
    xv = _mm(memn, p["x_wv"], "nn", BF16, f"l{l}_xv")
    xo = _xattn_fwd(xq, xk, xv, f"l{l}_xattn_fwd")
    y = _mm(xo, p["x_wo"], "nn", F32, f"l{l}_xo_proj")
    h2, hn2 = _res_norm_fwd(h1, y, p["x_norm_post"], p["ffn_norm_pre"], f"l{l}_x_post")
    sv.update(xq=xq, xk=xk, xv=xv, xo=xo, memn=memn, y_x=y, h2=h2, hn2=hn2)

    up_g = _mm(hn2, p["ffn_w_up"][0], "nt", F32, f"l{l}_ffn_up_gate")
    up_v = _mm(hn2, p["ffn_w_up"][1], "nt", F32, f"l{l}_ffn_up_val")
    act, *ffn_gathered = _ffn_act_fwd(up_g, up_v, *_ffn_halves(p), f"l{l}_ffn_act", ffn_payload)
    y = _mm(act, p["ffn_w_down"], "nn", F32, f"l{l}_ffn_down")
    h3, hn3 = _res_norm_fwd(h2, y, p["ffn_norm_post"], g_next, f"l{l}_ffn_post")
    sv.update(up_g=up_g, up_v=up_v, act=act, y_ffn=y)
    return h3, hn3, sv, (ffn_gathered[0] if ffn_gathered else None)


def _layer_bwd(l, dh, dy, sv, mem, cos, sin, prev_post, ffn_payload, pack):
    p = sv["p"]
    gr = {}
    received = {}
    dact = _mm(dy, p["ffn_w_down"], "nt", F32, f"l{l}_d_act")
    gr["ffn_w_down"] = _mm(sv["act"], dy, "tn", BF16, f"l{l}_dw_down")
    dup_g, dup_v, dwg, dwv, dbg, dbv, *got = _ffn_act_bwd(sv["up_g"], sv["up_v"], dact, *_ffn_halves(p), f"l{l}_ffn_act_bwd",
                                                         ffn_payload)
    if got:
        received["ffn_payload"] = got[0]
    gr["ffn_conv_w"] = jnp.concatenate([dwg, dwv], axis=1)
    gr["ffn_conv_b"] = jnp.concatenate([dbg, dbv], axis=1)
    dhn = (_mm(dup_g, p["ffn_w_up"][0], "nn", F32, f"l{l}_d_hn2_gate"), _mm(dup_v, p["ffn_w_up"][1], "nn", F32, f"l{l}_d_hn2_val"))
    gr["ffn_w_up"] = (_mm(dup_g, sv["hn2"], "tn", BF16, f"l{l}_dw_up_gate"), _mm(dup_v, sv["hn2"], "tn", BF16, f"l{l}_dw_up_val"))
    dh, dy, gr["ffn_norm_pre"], gr["x_norm_post"] = _norm_bwd(
        dh, (sv["h2"], p["ffn_norm_pre"], dhn), (sv["y_x"], p["x_norm_post"]), f"l{l}_x_post_bwd")

    do = _mm(dy, p["x_wo"], "nt", BF16, f"l{l}_d_xo")
    gr["x_wo"] = _mm(sv["xo"], dy, "tn", BF16, f"l{l}_dw_xo")
    dq, dk, dv = _xattn_bwd(sv["xq"], sv["xk"], sv["xv"], do, f"l{l}_xattn_bwd")
    dhn = _mm(dq, p["x_wq"], "nt", F32, f"l{l}_d_hn1")
    gr["x_wq"] = _mm(sv["hn1"], dq, "tn", BF16, f"l{l}_dw_xq")
    gr["x_wk"] = _mm(sv["memn"], dk, "tn", BF16, f"l{l}_dw_xk")
    gr["x_wv"] = _mm(sv["memn"], dv, "tn", BF16, f"l{l}_dw_xv")
    dmemn = _mm(dk, p["x_wk"], "nt", F32, f"l{l}_d_memn_k") + _mm(dv, p["x_wv"], "nt", F32, f"l{l}_d_memn_v")
    gr["mem_norm"] = _rms_gain_grad(mem, p["mem_norm"], dmemn, f"l{l}_mem_norm_bwd")
    dh, dy, gr["x_norm_pre"], gr["mix_norm_post"] = _norm_bwd(
        dh, (sv["h1"], p["x_norm_pre"], dhn), (sv["y_mix"], p["mix_norm_post"]), f"l{l}_mix_post_bwd")

    dmix = _mm(dy, p["w_out"], "nt", F32, f"l{l}_d_mix")
    gr["w_out"] = _mm(sv["mix"], dy, "tn", BF16, f"l{l}_dw_out")
    do_a = _heads_major(dmix[:, :SB_WIDTH].astype(BF16), SB_HEADS)
    dq, dk, dv, received["ffn"] = _sb_bwd(sv["sb_q"], sv["sb_k"], sv["sb_v"], do_a, sv["sb_tot"], sv["sb_first"],
                                          pack("ffn", gr), f"l{l}_sb_bwd")
    du_sb = _tokens_major(jnp.concatenate([dq, dk, dv], axis=0))

    db_out = dmix[:, SB_WIDTH:SB_WIDTH + CV_WIDTH]
    ds = _mm(db_out, p["cv_pw_w"], "nt", F32, f"l{l}_d_cv_s")
    gr["cv_pw_w"] = _mm(sv["cv_s"], db_out, "tn", BF16, f"l{l}_dw_cv_pw")
    du_cv, dcvw, gr["cv_b"], gr["cv_ln_g"], gr["cv_ln_b"], gr["cv_pw_b"], *got = _cv_bwd(
        sv["u"], sv["cv_c"], ds, db_out, p["cv_w"], p["cv_ln_g"], p["cv_ln_b"], f"l{l}_cv_bwd", pack("out", gr))
    if got:
        received["out"] = got[0]
    gr["cv_w"] = dcvw[:CV_KERNEL]

    dqs, dks, dvs = [], [], []
    up_rows = jnp.concatenate([g.reshape(N_DEV // 2, -1, PAYLOAD_COLS) for g in gr["ffn_w_up"]], axis=0)
    half_rows = up_rows.shape[1] // 2
    carried = {"up0": up_rows[:, :half_rows], "up1": up_rows[:, half_rows:], "cross": pack("cross", gr)}
    for b, ((_, dil), what) in enumerate(zip(DL_PATTERN, carried)):
        dq, dk, dv, received[what] = _dl_bwd(sv["dl_qk"], sv["u"], dmix, sv["dl_o"], sv["dl_w"][b], sv["dl_lse"][b], dil,
                                             f"l{l}_dl{b}_bwd", carried[what])
        dqs.append(dq)
        dks.append(dk)
        dvs.append(dv)
    du_dl = _rope_bwd(dqs, dks, dvs, cos, sin, f"l{l}_rope_bwd")

    du = jnp.concatenate([du_sb, du_cv, du_dl], axis=-1)
    dhn = _mm(du, p["w_in"], "nn", F32, f"l{l}_d_hn0")
    gr["w_in"] = _mm(du, sv["hn0"], "tn", BF16, f"l{l}_dw_in")
    dh, dy, gr["mix_norm_pre"], dg_prev = _norm_bwd(dh, (sv["h0"], p["mix_norm_pre"], dhn), prev_post, f"l{l}_in_bwd")
    return dh, dy, gr, dg_prev, received


def kernel(x, mem, positions, mix_norm_pre, w_in, cv_w, cv_b, cv_ln_g, cv_ln_b, cv_pw_w, cv_pw_b, w_out, mix_norm_post, x_norm_pre, mem_norm, x_wq, x_wk, x_wv, x_wo, x_norm_post, ffn_norm_pre, ffn_w_up, ffn_conv_w, ffn_conv_b, ffn_w_down, ffn_norm_post, loss_target, m_mix_norm_pre, m_w_in, m_cv_w, m_cv_b, m_cv_ln_g, m_cv_ln_b, m_cv_pw_w, m_cv_pw_b, m_w_out, m_mix_norm_post, m_x_norm_pre, m_mem_norm, m_x_wq, m_x_wk, m_x_wv, m_x_wo, m_x_norm_post, m_ffn_norm_pre, m_ffn_w_up, m_ffn_conv_w, m_ffn_conv_b, m_ffn_w_down, m_ffn_norm_post, v_mix_norm_pre, v_w_in, v_cv_w, v_cv_b, v_cv_ln_g, v_cv_ln_b, v_cv_pw_w, v_cv_pw_b, v_w_out, v_mix_norm_post, v_x_norm_pre, v_mem_norm, v_x_wq, v_x_wk, v_x_wv, v_x_wo, v_x_norm_post, v_ffn_norm_pre, v_ffn_w_up, v_ffn_conv_w, v_ffn_conv_b, v_ffn_w_down, v_ffn_norm_post):
    args = locals()
    wts = {n: args[n] for n in WEIGHTS}
    mom = {n: args["m_" + n] for n in WEIGHTS}
    var = {n: args["v_" + n] for n in WEIGHTS}

    x2, mem2, target = x[0], mem[0], loss_target[0]

    small_payload = _pad_rows(jnp.concatenate([wts[n].reshape(-1) for n in SMALL_SHARDED]), PAYLOAD_COLS)
    gathered_a, small = _all_gather([_pack_weights("a", wts), small_payload], "weights_all_gather")
    small = small.reshape(N_DEV, -1)
    small_full = {}
    off = 0
    for n in SMALL_SHARDED:
        size = wts[n].size
        small_full[n] = _unshard(n, small[:, off:off + size].reshape((N_DEV,) + wts[n].shape))
        off += size
    pw_first = sum(r for _, _, r in GROUPS[PW_GROUP])

    def pw_of(gathered_pw_group):
        return _unshard("cv_pw_w", gathered_pw_group[:, pw_first:, :].reshape((N_DEV,) + wts["cv_pw_w"].shape))

    def small_params(l):
        p = {n: wts[n][l][None, :] for n in REPLICATED}
        p.update({n: small_full[n][l] for n in SMALL_SHARDED})
        return p

    def out_and_pw(gathered_pw_group):
        return {"w_out": _unpack_weights(PW_GROUP, gathered_pw_group)[("w_out", 0)], "cv_pw_w": pw_of(gathered_pw_group)[0]}

    def of_layer(group, l):
        return lambda gathered: {n: w for (n, ll), w in _unpack_weights(group, gathered).items() if ll == l}

    pos = positions[0].astype(F32)
    half = HEAD_DIM // 2
    inv_freq = ROPE_THETA ** (-jnp.arange(half, dtype=F32) / half)
    ang = pos[:, None] * inv_freq
    cos = jnp.tile(jnp.cos(ang), (1, LANES // half))
    sin = jnp.tile(jnp.sin(ang), (1, LANES // half))

    p0 = small_params(0)
    p0["w_in"] = _unpack_weights("a", gathered_a)[("w_in", 0)]
    hn = _rms_fwd(x2, p0["mix_norm_pre"], "l0_in_norm")
    ffn_shards = [(wts["ffn_w_up"][l], wts["ffn_w_down"][l]) for l in range(DEPTH)]
    h, hn, sv0, gathered_m1 = _layer_fwd(0, x2, hn, p0, mem2, cos, sin, wts["mix_norm_pre"][1][None, :],
                                         _pack_weights("bx", wts), of_layer("bx", 0), ffn_shards[0], _pack_weights("m1", wts),
                                         _pack_weights(PW_GROUP, wts), out_and_pw)
    p1 = small_params(1)
    mixer1 = _unpack_weights("m1", gathered_m1)
    p1.update(w_in=mixer1[("w_in", 1)], w_out=mixer1[("w_out", 1)], cv_pw_w=pw_of(sv0["rope_gathered"])[1])
    h, _, sv1, _ = _layer_fwd(1, h, hn, p1, mem2, cos, sin, None, _pack_weights("cx", wts), of_layer("cx", 1), ffn_shards[1],
                              None)
    loss_part, dh = _loss_fwd(h, target, "loss")
    loss = lax.psum(loss_part[0, 0], ("x", "y", "c"))

    grads = {n: [None] * DEPTH for n in WEIGHTS}
    dh, dy, _, grads["ffn_norm_post"][1] = _norm_bwd(dh, None, (sv1["y_ffn"], sv1["p"]["ffn_norm_post"]), "last_post_bwd")

    def packer(l, groups):
        def pack(which, gr):
            if which not in groups:
                return None
            if groups[which] == PW_GROUP:
                return _pack_grads(PW_GROUP, {"w_out": {0: gr["w_out"]}, "cv_pw_w": [gr["cv_pw_w"], grads["cv_pw_w"][1]]})
            return _pack_grads(groups[which], {n: {l: g} for n, g in gr.items()})
        return pack

    dh, dy, gr, grads["ffn_norm_post"][0], got1 = _layer_bwd(
        1, dh, dy, sv1, mem2, cos, sin, (sv0["y_ffn"], sv0["p"]["ffn_norm_post"]), None, packer(1, {"ffn": "cf", "cross": "cx"}))
    for n, g in gr.items():
        grads[n][1] = g
    dh, _, gr, _, got0 = _layer_bwd(0, dh, dy, sv0, mem2, cos, sin, None, _pack_grads("m1", grads),
                                    packer(0, {"ffn": "bf", "cross": "bx", "out": PW_GROUP}))
    for n, g in gr.items():
        grads[n][0] = g
    grad_x = dh

    small_rows = jnp.concatenate([_to_shards(n, jnp.stack(grads[n])) for n in SMALL_SHARDED], axis=1)
    rep_flat = jnp.concatenate([jnp.stack([g.reshape(-1) for g in grads[n]]).reshape(-1) for n in REPLICATED])
    rep_rows = jnp.broadcast_to(rep_flat[None], (N_DEV, rep_flat.shape[0]))
    f32_rows = _pad_rows(jnp.concatenate([small_rows, rep_rows], axis=1), PAYLOAD_COLS)
    received_a, small_parts = _all_to_all([_pack_grads("a", grads), f32_rows], "grads_all_to_all")
    received = {"a": received_a, PW_GROUP: got0["out"], "m1": got0["ffn_payload"], "bf": got0["ffn"], "bx": got0["cross"],
                "cf": got1["ffn"], "cx": got1["cross"]}

    res = {}
    for n in BIG:
        shape = wts[n].shape
        two_d = (shape[0] * shape[1], shape[2])
        operands = (wts[n].reshape(two_d), mom[n].reshape(two_d), var[n].reshape(two_d))
        if n == "cv_pw_w":
            outs = _adamw(received[PW_GROUP][:, pw_first:, :].reshape((N_DEV,) + two_d), *operands, f"adamw_{n}")
        elif n in TRANSPOSED:
            layers = []
            for l, got in enumerate((got0, got1)):
                if n == "ffn_w_up":
                    layers.append(jnp.concatenate([got["up0"], got["up1"]], axis=1))
                else:
                    group, first, r = _where_is(n, l)
                    layers.append(received[group][:, first:first + r, :])
            parts = jnp.stack(layers, axis=1).transpose(0, 1, 3, 2).reshape((N_DEV,) + two_d)
            outs = _adamw(parts, *operands, f"adamw_{n}")
        else:
            sources = []
            for l in range(DEPTH):
                group, first, _ = _where_is(n, l)
                sources.append((received[group], first))
            outs = _adamw_packed(sources, *operands, f"adamw_{n}")
        res[n] = [o.reshape(shape) for o in outs]
    small_names = SMALL_SHARDED + REPLICATED
    flat_w = _pad_rows(jnp.concatenate([wts[n].reshape(-1) for n in small_names]), PAYLOAD_COLS)
    flat_m = _pad_rows(jnp.concatenate([mom[n].reshape(-1) for n in small_names]), PAYLOAD_COLS)
    flat_v = _pad_rows(jnp.concatenate([var[n].reshape(-1) for n in small_names]), PAYLOAD_COLS)
    outs = _adamw(small_parts, flat_w, flat_m, flat_v, "adamw_small")
    outs = [o.reshape(-1) for o in outs]
    off = 0
    for n in small_names:
        size = wts[n].size
        res[n] = [o[off:off + size].reshape(wts[n].shape) for o in outs]
        off += size

    result = [loss, grad_x[None]]
    for kind in range(4):
        result += [res[n][kind] for n in WEIGHTS]
    return tuple(result)
```
